```python
import math
import jax, jax.numpy as jnp
from jax import lax
import numpy as np

D_MODEL = 1024
BATCH = 8
SEQ = 2048
DEPTH = 2
DEC_BATCH = 128
DEC_SEQ = 4
PAST_LEN = 16384
PAGE_SIZE = 128

SSD_INNER = D_MODEL
SSD_HEAD_DIM = 64
SSD_HEADS = SSD_INNER // SSD_HEAD_DIM
SSD_GROUPS = 2
SSD_STATE = 64
SSD_CONV = 4
SSD_CONV_CH = SSD_INNER + 2 * SSD_GROUPS * SSD_STATE
MLSTM_INNER = D_MODEL
MLSTM_HEADS = 4
MLSTM_HEAD_DIM = MLSTM_INNER // MLSTM_HEADS
CHUNK = 128
D_FF = ((8 * D_MODEL // 3 + 255) // 256) * 256
FFN_CONV = 3
IN_SPLITS = (SSD_INNER, SSD_CONV_CH, SSD_HEADS, MLSTM_INNER, MLSTM_INNER, MLSTM_INNER,
             2 * MLSTM_HEADS, MLSTM_INNER, 2 * D_MODEL)
N_IN = SSD_INNER + SSD_CONV_CH + SSD_HEADS + 4 * MLSTM_INNER + 2 * MLSTM_HEADS + 2 * D_MODEL
ALPHA = (2 * DEPTH) ** 0.25
BETA = (8 * DEPTH) ** -0.25
EPS = 1e-5

kernel_name = "hybrid_ssd_mlstm_convffn_step"


def _layer_norm(x, g, b):
    xf = x.astype(jnp.float32)
    mu = jnp.mean(xf, axis=-1, keepdims=True)
    var = jnp.mean(jnp.square(xf - mu), axis=-1, keepdims=True)
    y = (xf - mu) * lax.rsqrt(var + EPS) * g.astype(jnp.float32) + b.astype(jnp.float32)
    return y.astype(x.dtype)


def _rms_norm(xf, w):
    return xf * lax.rsqrt(jnp.mean(jnp.square(xf), axis=-1, keepdims=True) + EPS) * w.astype(jnp.float32)


def _causal_dwconv(x, buf, w, b):
    length = x.shape[1]
    width = w.shape[0]
    xp = jnp.concatenate([buf.astype(x.dtype), x], axis=1)
    y = xp[:, 0:length] * w[0]
    for j in range(1, width):
        y = y + xp[:, j:j + length] * w[j]
    return y + b, xp[:, length:]


def _chunk_len(length):
    q = min(length, CHUNK)
    if length % q:
        q = math.gcd(length, CHUNK)
    return q


def _ssd_chunked(xh, dt, a, bm, cm, h0):
    bsz, length, n_heads, p_dim = xh.shape
    n_groups, n_state = bm.shape[2], bm.shape[3]
    hpg = n_heads // n_groups
    q = _chunk_len(length)
    nc = length // q
    xdt = (xh * dt[..., None]).reshape(bsz, nc, q, n_groups, hpg, p_dim)
    acs = jnp.cumsum((dt * a).reshape(bsz, nc, q, n_groups, hpg), axis=2)
    bc = bm.reshape(bsz, nc, q, n_groups, n_state)
    cc = cm.reshape(bsz, nc, q, n_groups, n_state)
    seg = acs[:, :, :, None] - acs[:, :, None, :]
    causal = jnp.tril(jnp.ones((q, q), dtype=bool))[:, :, None, None]
    decay = jnp.exp(jnp.where(causal, seg, -jnp.inf))
    cb = jnp.einsum('bclgn,bcsgn->bclsg', cc, bc)
    y_diag = jnp.einsum('bclsg,bclsgk,bcsgkp->bclgkp', cb, decay, xdt)
    decay_end = jnp.exp(acs[:, :, -1:] - acs)
    local = jnp.einsum('bclgn,bclgk,bclgkp->bcgkpn', bc, decay_end, xdt)
    chunk_decay = jnp.exp(acs[:, :, -1])

    def step(h, inp):
        loc, dec = inp
        return h * dec[..., None, None] + loc, h

    h_last, h_in = lax.scan(step, h0.reshape(bsz, n_groups, hpg, p_dim, n_state),
                            (jnp.moveaxis(local, 1, 0), jnp.moveaxis(chunk_decay, 1, 0)))
    h_in = jnp.moveaxis(h_in, 0, 1)
    y_off = jnp.einsum('bclgn,bcgkpn,bclgk->bclgkp', cc, h_in, jnp.exp(acs))
    y = (y_diag + y_off).reshape(bsz, length, n_heads, p_dim)
    return y, h_last.reshape(bsz, n_heads, p_dim, n_state)


def _mlstm_chunked(q, k, v, i_pre, logf, c0, n0, m0):
    bsz, length, n_heads, d_head = q.shape
    qn = _chunk_len(length)
    nc = length // qn

    def vec_chunks(t):
        return t.reshape(bsz, nc, qn, n_heads, d_head).transpose(1, 0, 3, 2, 4)

    def gate_chunks(t):
        return t.reshape(bsz, nc, qn, n_heads).transpose(1, 0, 3, 2)

    causal = jnp.tril(jnp.ones((qn, qn), dtype=bool))

    def step(carry, inp):
        c, n, m = carry
        qc, kc, vc, ic, fc = inp
        bcum = jnp.cumsum(fc, axis=-1)
        dmat = jnp.where(causal, bcum[..., :, None] - bcum[..., None, :] + ic[..., None, :], -jnp.inf)
        inter = bcum + m[..., None]
        m_t = jnp.maximum(inter, jnp.max(dmat, axis=-1))
        w_intra = jnp.exp(dmat - m_t[..., None])
        w_inter = jnp.exp(inter - m_t)
        s = jnp.einsum('bhtd,bhsd->bhts', qc, kc) * w_intra
        num = jnp.einsum('bhts,bhse->bhte', s, vc) + w_inter[..., None] * jnp.einsum('bhtd,bhde->bhte', qc, c)
        den = jnp.sum(s, axis=-1) + w_inter * jnp.einsum('bhtd,bhd->bht', qc, n)
        h = num / jnp.maximum(jnp.abs(den), jnp.exp(-m_t))[..., None]
        m_end = m_t[..., -1]
        w_c = jnp.exp(bcum[..., -1] + m - m_end)
        w_s = jnp.exp(bcum[..., -1:] - bcum + ic - m_end[..., None])
        c_new = w_c[..., None, None] * c + jnp.einsum('bhs,bhsd,bhse->bhde', w_s, kc, vc)
        n_new = w_c[..., None] * n + jnp.einsum('bhs,bhsd->bhd', w_s, kc)
        return (c_new, n_new, m_end), h

    (c_last, n_last, m_last), hs = lax.scan(
        step, (c0, n0, m0),
        (vec_chunks(q), vec_chunks(k), vec_chunks(v), gate_chunks(i_pre), gate_chunks(logf)))
    hs = hs.transpose(1, 0, 3, 2, 4).reshape(bsz, length, n_heads, d_head)
    return hs, (c_last, n_last, m_last)


def _mixer(x, st_ssd, st_ssd_conv, st_c, st_n, st_m, w_in, ssd_conv_w, ssd_conv_b, ssd_dt_bias,
           ssd_a_log, ssd_d, ssd_norm_w, mlstm_gate_b, mlstm_norm_w, w_branch_a, w_branch_b, w_out):
    f32 = jnp.float32
    bsz, length, _ = x.shape
    proj = x @ w_in
    cuts = np.cumsum(IN_SPLITS)[:-1].tolist()
    z, xbc, dt, q, k, v, if_pre, o_pre, gates = jnp.split(proj, cuts, axis=-1)
    xbc, new_ssd_conv = _causal_dwconv(xbc, st_ssd_conv, ssd_conv_w, ssd_conv_b)
    xbc = jax.nn.silu(xbc.astype(f32))
    xs, bm, cm = jnp.split(xbc, [SSD_INNER, SSD_INNER + SSD_GROUPS * SSD_STATE], axis=-1)
    xh = xs.reshape(bsz, length, SSD_HEADS, SSD_HEAD_DIM)
    dt = jax.nn.softplus(dt.astype(f32) + ssd_dt_bias.astype(f32))
    a = -jnp.exp(ssd_a_log.astype(f32))
    y, new_ssd = _ssd_chunked(xh, dt, a,
                              bm.reshape(bsz, length, SSD_GROUPS, SSD_STATE),
                              cm.reshape(bsz, length, SSD_GROUPS, SSD_STATE),
                              st_ssd.astype(f32))
    y = y + ssd_d.astype(f32)[:, None] * xh
    y = _rms_norm(y.reshape(bsz, length, SSD_INNER) * jax.nn.silu(z.astype(f32)), ssd_norm_w)
    branch_a = y.astype(x.dtype) @ w_branch_a
    hd = (bsz, length, MLSTM_HEADS, MLSTM_HEAD_DIM)
    qh = q.astype(f32).reshape(hd)
    kh = k.astype(f32).reshape(hd) * (MLSTM_HEAD_DIM ** -0.5)
    vh = v.astype(f32).reshape(hd)
    i_pre, f_pre = jnp.split(if_pre.astype(f32) + mlstm_gate_b.astype(f32), 2, axis=-1)
    hm, (new_c, new_n, new_m) = _mlstm_chunked(qh, kh, vh, i_pre, jax.nn.log_sigmoid(f_pre),
                                               st_c.astype(f32), st_n.astype(f32), st_m.astype(f32))
    hm = _rms_norm(hm, mlstm_norm_w.reshape(MLSTM_HEADS, MLSTM_HEAD_DIM))
    hm = hm.reshape(bsz, length, MLSTM_INNER) * jax.nn.sigmoid(o_pre.astype(f32))
    branch_b = hm.astype(x.dtype) @ w_branch_b
    g_a, g_b = jnp.split(jax.nn.sigmoid(gates), 2, axis=-1)
    out = (g_a * branch_a + g_b * branch_b) @ w_out
    dty = x.dtype
    return out, (new_ssd.astype(dty), new_ssd_conv.astype(dty), new_c.astype(dty),
                 new_n.astype(dty), new_m.astype(dty))


def _conv_ffn(x, st_conv, w_up, conv_w, conv_b, w_down):
    up = x @ w_up
    up, new_conv = _causal_dwconv(up, st_conv, conv_w, conv_b)
    gate, val = jnp.split(up, 2, axis=-1)
    return (jax.nn.silu(gate) * val) @ w_down, new_conv.astype(x.dtype)


def _layer(x, states, params):
    st_ssd, st_ssd_conv, st_c, st_n, st_m, st_ffn = states
    (w_in, ssd_conv_w, ssd_conv_b, ssd_dt_bias, ssd_a_log, ssd_d, ssd_norm_w, mlstm_gate_b,
     mlstm_norm_w, w_branch_a, w_branch_b, w_out, ln1_g, ln1_b, ffn_w_up, ffn_conv_w,
     ffn_conv_b, ffn_w_down, ln2_g, ln2_b) = params
    mix, (n_ssd, n_ssd_conv, n_c, n_n, n_m) = _mixer(
        x, st_ssd, st_ssd_conv, st_c, st_n, st_m, w_in, ssd_conv_w, ssd_conv_b, ssd_dt_bias,
        ssd_a_log, ssd_d, ssd_norm_w, mlstm_gate_b, mlstm_norm_w, w_branch_a, w_branch_b, w_out)
    x = _layer_norm(ALPHA * x + mix, ln1_g, ln1_b)
    ffn, n_ffn = _conv_ffn(x, st_ffn, ffn_w_up, ffn_conv_w, ffn_conv_b, ffn_w_down)
    x = _layer_norm(ALPHA * x + ffn, ln2_g, ln2_b)
    return x, (n_ssd, n_ssd_conv, n_c, n_n, n_m, n_ffn)


def _trunk(x, states, params):
    new = []
    for layer in range(DEPTH):
        x, st = _layer(x, tuple(s[layer] for s in states), tuple(p[layer] for p in params))
        new.append(st)
    stacked = tuple(jnp.stack([st[i] for st in new]) for i in range(len(states)))
    return x, stacked


def setup_inputs(seed: int = 0) -> dict:
    key = jax.random.key(seed)
    ks = jax.random.split(key, 32)
    f = jnp.float32

    def nrm(k, shape, scale):
        return jax.random.normal(k, shape, f) * scale

    x_prompt = nrm(ks[0], (BATCH, SEQ, D_MODEL), 1.0)
    x_sample = nrm(ks[1], (DEC_BATCH, DEC_SEQ, D_MODEL), 1.0)
    state_ssd = nrm(ks[2], (DEPTH, DEC_BATCH, SSD_HEADS, SSD_HEAD_DIM, SSD_STATE), 0.1)
    state_ssd_conv = nrm(ks[3], (DEPTH, DEC_BATCH, SSD_CONV - 1, SSD_CONV_CH), 1.0)
    state_mlstm_c = nrm(ks[4], (DEPTH, DEC_BATCH, MLSTM_HEADS, MLSTM_HEAD_DIM, MLSTM_HEAD_DIM), 0.1)
    state_mlstm_n = nrm(ks[5], (DEPTH, DEC_BATCH, MLSTM_HEADS, MLSTM_HEAD_DIM), 0.1)
    state_mlstm_m = jax.random.uniform(ks[6], (DEPTH, DEC_BATCH, MLSTM_HEADS), f, 0.0, 1.0)
    state_ffn_conv = nrm(ks[7], (DEPTH, DEC_BATCH, FFN_CONV - 1, 2 * D_FF), 1.0)
    w_in = nrm(ks[8], (DEPTH, D_MODEL, N_IN), D_MODEL ** -0.5)
    ssd_conv_w = nrm(ks[9], (DEPTH, SSD_CONV, SSD_CONV_CH), SSD_CONV ** -0.5)
    ssd_conv_b = nrm(ks[10], (DEPTH, SSD_CONV_CH), 0.01)
    dt0 = jnp.exp(jax.random.uniform(ks[11], (DEPTH, SSD_HEADS), f, math.log(1e-3), math.log(1e-1)))
    ssd_dt_bias = dt0 + jnp.log(-jnp.expm1(-dt0))
    ssd_a_log = jnp.log(jax.random.uniform(ks[12], (DEPTH, SSD_HEADS), f, 1.0, 16.0))
    ssd_d = 1.0 + nrm(ks[13], (DEPTH, SSD_HEADS), 0.01)
    ssd_norm_w = 1.0 + nrm(ks[14], (DEPTH, SSD_INNER), 0.01)
    f_bias = jnp.linspace(3.0, 6.0, MLSTM_HEADS, dtype=f)
    mlstm_gate_b = jnp.concatenate([nrm(ks[15], (DEPTH, MLSTM_HEADS), 0.1),
                                    f_bias + nrm(ks[16], (DEPTH, MLSTM_HEADS), 0.01)], axis=-1)
    mlstm_norm_w = 1.0 + nrm(ks[17], (DEPTH, MLSTM_INNER), 0.01)
    w_branch_a = nrm(ks[18], (DEPTH, SSD_INNER, D_MODEL), SSD_INNER ** -0.5)
    w_branch_b = nrm(ks[19], (DEPTH, MLSTM_INNER, D_MODEL), MLSTM_INNER ** -0.5)
    w_out = nrm(ks[20], (DEPTH, D_MODEL, D_MODEL), BETA * D_MODEL ** -0.5)
    ln1_g = 1.0 + nrm(ks[21], (DEPTH, D_MODEL), 0.01)
    ln1_b = nrm(ks[22], (DEPTH, D_MODEL), 0.01)
    ffn_w_up = nrm(ks[23], (DEPTH, D_MODEL, 2 * D_FF), D_MODEL ** -0.5)
    ffn_conv_w = nrm(ks[24], (DEPTH, FFN_CONV, 2 * D_FF), FFN_CONV ** -0.5)
    ffn_conv_b = nrm(ks[25], (DEPTH, 2 * D_FF), 0.01)
    ffn_w_down = nrm(ks[26], (DEPTH, D_FF, D_MODEL), BETA * D_FF ** -0.5)
    ln2_g = 1.0 + nrm(ks[27], (DEPTH, D_MODEL), 0.01)
    ln2_b = nrm(ks[28], (DEPTH, D_MODEL), 0.01)
    return {"x_prompt": x_prompt, "x_sample": x_sample,
            "state_ssd": state_ssd, "state_ssd_conv": state_ssd_conv,
            "state_mlstm_c": state_mlstm_c, "state_mlstm_n": state_mlstm_n,
            "state_mlstm_m": state_mlstm_m, "state_ffn_conv": state_ffn_conv,
            "w_in": w_in, "ssd_conv_w": ssd_conv_w, "ssd_conv_b": ssd_conv_b,
            "ssd_dt_bias": ssd_dt_bias, "ssd_a_log": ssd_a_log, "ssd_d": ssd_d,
            "ssd_norm_w": ssd_norm_w, "mlstm_gate_b": mlstm_gate_b, "mlstm_norm_w": mlstm_norm_w,
            "w_branch_a": w_branch_a, "w_branch_b": w_branch_b, "w_out": w_out,
            "ln1_g": ln1_g, "ln1_b": ln1_b, "ffn_w_up": ffn_w_up, "ffn_conv_w": ffn_conv_w,
            "ffn_conv_b": ffn_conv_b, "ffn_w_down": ffn_w_down, "ln2_g": ln2_g, "ln2_b": ln2_b}


def reference(x_prompt, x_sample, state_ssd, state_ssd_conv, state_mlstm_c, state_mlstm_n,
              state_mlstm_m, state_ffn_conv, w_in, ssd_conv_w, ssd_conv_b, ssd_dt_bias, ssd_a_log,
              ssd_d, ssd_norm_w, mlstm_gate_b, mlstm_norm_w, w_branch_a, w_branch_b, w_out,
              ln1_g, ln1_b, ffn_w_up, ffn_conv_w, ffn_conv_b, ffn_w_down, ln2_g, ln2_b):
    params = (w_in, ssd_conv_w, ssd_conv_b, ssd_dt_bias, ssd_a_log, ssd_d, ssd_norm_w, mlstm_gate_b,
              mlstm_norm_w, w_branch_a, w_branch_b, w_out, ln1_g, ln1_b, ffn_w_up, ffn_conv_w,
              ffn_conv_b, ffn_w_down, ln2_g, ln2_b)
    bp = x_prompt.shape[0]
    dtp = x_prompt.dtype
    prompt_states = (jnp.zeros((DEPTH, bp) + state_ssd.shape[2:], dtp),
                     jnp.zeros((DEPTH, bp) + state_ssd_conv.shape[2:], dtp),
                     jnp.zeros((DEPTH, bp) + state_mlstm_c.shape[2:], dtp),
                     jnp.zeros((DEPTH, bp) + state_mlstm_n.shape[2:], dtp),
                     jnp.zeros((DEPTH, bp) + state_mlstm_m.shape[2:], dtp),
                     jnp.zeros((DEPTH, bp) + state_ffn_conv.shape[2:], dtp))
    y_prompt, (ssd_p, ssd_conv_p, c_p, n_p, m_p, ffn_p) = _trunk(x_prompt, prompt_states, params)
    sample_states = (state_ssd, state_ssd_conv, state_mlstm_c, state_mlstm_n, state_mlstm_m,
                     state_ffn_conv)
    y_sample, (ssd_s, ssd_conv_s, c_s, n_s, m_s, ffn_s) = _trunk(x_sample, sample_states, params)
    return (y_prompt, y_sample, ssd_p, ssd_s, ssd_conv_p, ssd_conv_s, c_p, c_s, n_p, n_s,
            m_p, m_s, ffn_p, ffn_s)
```

```python
import functools

import jax
import jax.numpy as jnp
import numpy as np
from jax import lax
from jax.experimental import pallas as pl
from jax.experimental.pallas import tpu as pltpu

f32 = jnp.float32
bf16 = jnp.bfloat16

D_MODEL = 1024
DEPTH = 2
SSD_HEADS = 16
SSD_HEAD_DIM = 64
SSD_STATE = 64
SSD_GROUPS = 2
SSD_CONV = 4
SSD_BC = 2 * SSD_GROUPS * SSD_STATE
MLSTM_HEADS = 4
MLSTM_HEAD_DIM = 256
CHUNK = 128
D_FF = 2816
FFN_CONV = 3
ALPHA = (2 * DEPTH) ** 0.25
EPS = 1e-5

LANES = 128
SUBLANES = 8
BC_PAD = 4 * LANES
SMALL = LANES
DT_OFF, I_OFF, F_OFF = 0, 16, 20
P32_Z, P32_O, P32_GA, P32_GB, P32_XS = 0, 1, 2, 3, 4
P32_BC_OFF = 5 * D_MODEL
P32_SM_OFF = P32_BC_OFF + BC_PAD
P32_W = 6 * D_MODEL
FF_CH = 256
FF_NCH = D_FF // FF_CH
NEG_BIG = -1e30

NT_DIMS = (((1,), (1,)), ((), ()))
TN_DIMS = (((0,), (0,)), ((), ()))


def _dot(a, b):
    return jnp.dot(a, b, preferred_element_type=f32)


def _split3(x):
    hi = x.astype(bf16)
    r = x - hi.astype(f32)
    mid = r.astype(bf16)
    lo = (r - mid.astype(f32)).astype(bf16)
    return hi, mid, lo


def _dot01_rhs(x, e):
    hi, mid, lo = _split3(x)
    return _dot(hi, e) + _dot(mid, e) + _dot(lo, e)


def _dot01_lhs(t, x):
    hi, mid, lo = _split3(x)
    return _dot(t, hi) + _dot(t, mid) + _dot(t, lo)


def _softplus(x):
    return jnp.maximum(x, 0.0) + jnp.log1p(jnp.exp(-jnp.abs(x)))


def _silu(x):
    return x * jax.nn.sigmoid(x)


def _tri(q):
    row = lax.broadcasted_iota(jnp.int32, (q, q), 0)
    col = lax.broadcasted_iota(jnp.int32, (q, q), 1)
    return row >= col


def _valid_rows(q, width, lr, is_last):
    row = lax.broadcasted_iota(jnp.int32, (q, width), 0)
    return row < jnp.where(is_last, lr, q)


def _proj_kernel(x_ref, w_ref, o_ref, xb):
    @pl.when(pl.program_id(1) == 0)
    def _():
        xb[...] = x_ref[...].astype(bf16)

    o_ref[...] = _dot(xb[...], w_ref[...]).astype(o_ref.dtype)


def _proj(x, w, out_dtype, tm, tn):
    m, k = x.shape
    n = w.shape[1]
    return pl.pallas_call(
        _proj_kernel,
        out_shape=jax.ShapeDtypeStruct((m, n), out_dtype),
        grid=(m // tm, n // tn),
        in_specs=[pl.BlockSpec((tm, k), lambda i, j: (i, 0)),
                  pl.BlockSpec((k, tn), lambda i, j: (0, j))],
        out_specs=pl.BlockSpec((tm, tn), lambda i, j: (i, j)),
        scratch_shapes=[pltpu.VMEM((tm, k), bf16)],
        compiler_params=pltpu.CompilerParams(dimension_semantics=("parallel", "arbitrary")),
        name="proj",
    )(x, w)


def _ssd_kernel(z_ref, xs_ref, bc_ref, sm_ref, csx_ref, csb_ref, h0_ref,
                cwx_ref, cbx_ref, cwb_ref, cbb_ref, dtb_ref, alog_ref, dexp_ref, nw_ref, e_ref,
                y_ref, ncsx_ref, ncsb_ref, hout_ref,
                xpx, xpb, ht, yb, *, q, lr, nc):
    c = pl.program_id(1)
    is_last = c == nc - 1
    hdr = SUBLANES
    lo = hdr - (SSD_CONV - 1)

    @pl.when(c == 0)
    def _():
        ht[...] = h0_ref[...].T
        xpx[lo:hdr, :] = csx_ref[...]
        xpb[lo:hdr, :] = csb_ref[...]

    xpx[hdr:hdr + q, :] = xs_ref[...]
    xpb[hdr:hdr + q, :] = bc_ref[...]

    def conv(xp, w_ref, b_ref):
        w = w_ref[...]
        acc = xp[lo:lo + q, :] * w[0:1, :]
        for j in range(1, SSD_CONV):
            acc = acc + xp[lo + j:lo + j + q, :] * w[j:j + 1, :]
        return acc + b_ref[...]

    cx = conv(xpx, cwx_ref, cbx_ref)
    cb = conv(xpb, cwb_ref, cbb_ref)

    @pl.when(is_last)
    def _():
        ncsx_ref[...] = xpx[lo + lr:hdr + lr, :]
        ncsb_ref[...] = xpb[lo + lr:hdr + lr, :]

    tail_x = xpx[lo + q:hdr + q, :]
    tail_b = xpb[lo + q:hdr + q, :]
    xpx[lo:hdr, :] = tail_x
    xpb[lo:hdr, :] = tail_b

    xs = _silu(cx)
    bcv = _silu(cb)

    dt = _softplus(sm_ref[...] + dtb_ref[...])
    if lr < q:
        dt = jnp.where(_valid_rows(q, SMALL, lr, is_last), dt, 0.0)
    a = -jnp.exp(alog_ref[...])
    d_a = dt * a
    causal = _tri(q)
    tril = jnp.where(causal, 1.0, 0.0).astype(bf16)
    acs = _dot01_lhs(tril, d_a)
    acs_t = acs.T
    e = e_ref[...]
    acs_x = _dot01_rhs(acs, e)
    dt_x = _dot01_rhs(dt, e)
    last_x = acs_x[q - 1:q, :]
    xdt = xs * dt_x
    xdt_b = xdt.astype(bf16)
    xdtw = (xdt * jnp.exp(last_x - acs_x)).astype(bf16)
    e_a = jnp.exp(acs_x)
    lane_lo = lax.broadcasted_iota(jnp.int32, (q, LANES), 1) < SSD_HEAD_DIM

    half = D_MODEL // SSD_GROUPS
    heads_per_group = SSD_HEADS // SSD_GROUPS
    y_groups = []
    for g in range(SSD_GROUPS):
        bg = bcv[:, g * LANES:(g + 1) * LANES].astype(bf16)
        cg = bcv[:, (SSD_GROUPS + g) * LANES:(SSD_GROUPS + g + 1) * LANES].astype(bf16)
        cbm = lax.dot_general(cg, bg, NT_DIMS, preferred_element_type=f32)
        ht_g = ht[:, g * half:(g + 1) * half]
        y_off = _dot(cg[:, :SSD_STATE], ht_g.astype(bf16))
        for j in range(heads_per_group // 2):
            p = g * (heads_per_group // 2) + j
            xpair = xdt_b[:, p * LANES:(p + 1) * LANES]
            ys = []
            for hh in (2 * p, 2 * p + 1):
                seg = acs[:, hh:hh + 1] - acs_t[hh:hh + 1, :]
                decay = jnp.exp(jnp.where(causal, seg, -jnp.inf))
                ys.append(_dot((cbm * decay).astype(bf16), xpair))
            yb[:, p * LANES:(p + 1) * LANES] = jnp.where(lane_lo, ys[0], ys[1])
        y_groups.append(yb[:, g * half:(g + 1) * half] + y_off * e_a[:, g * half:(g + 1) * half])
        dec = jnp.exp(last_x[:, g * half:(g + 1) * half])
        upd = lax.dot_general(bg[:, :SSD_STATE], xdtw[:, g * half:(g + 1) * half], TN_DIMS,
                              preferred_element_type=f32)
        ht[:, g * half:(g + 1) * half] = dec * ht_g + upd

    y = jnp.concatenate(y_groups, axis=1) + dexp_ref[...] * xs
    y = y * _silu(z_ref[...])
    y = y * lax.rsqrt(jnp.mean(jnp.square(y), axis=-1, keepdims=True) + EPS) * nw_ref[...]
    y_ref[...] = y.astype(y_ref.dtype)

    @pl.when(is_last)
    def _():
        hout_ref[...] = ht[...].T


def _mlstm_kernel(q_ref, k_ref, v_ref, o_ref, sm_ref, c0_ref, n0_ref, m0_ref, gb_ref, nw_ref,
                  h_ref, cout_ref, nout_ref, mout_ref,
                  cs, ns, ms, *, q, lr, nc):
    c = pl.program_id(1)
    is_last = c == nc - 1

    @pl.when(c == 0)
    def _():
        cs[...] = c0_ref[...]
        ns[...] = n0_ref[...]
        ms[...] = m0_ref[...]

    sm = sm_ref[...] + gb_ref[...]
    logf = -_softplus(-sm)
    ipre = sm
    if lr < q:
        valid = _valid_rows(q, SMALL, lr, is_last)
        logf = jnp.where(valid, logf, 0.0)
        ipre = jnp.where(valid, ipre, NEG_BIG)
    causal = _tri(q)
    tril = jnp.where(causal, 1.0, 0.0).astype(bf16)
    bcum = _dot01_lhs(tril, logf)
    bcum_t = bcum.T
    ipre_t = ipre.T
    lane = lax.broadcasted_iota(jnp.int32, (1, SMALL), 1)
    k_scale = MLSTM_HEAD_DIM ** -0.5

    for h in range(MLSTM_HEADS):
        sl = slice(h * MLSTM_HEAD_DIM, (h + 1) * MLSTM_HEAD_DIM)
        qh = q_ref[:, sl]
        kh = k_ref[:, sl] * k_scale
        vh = v_ref[:, sl]
        b_col = bcum[:, F_OFF + h:F_OFF + h + 1]
        b_row = bcum_t[F_OFF + h:F_OFF + h + 1, :]
        i_col = ipre[:, I_OFF + h:I_OFF + h + 1]
        i_row = ipre_t[I_OFF + h:I_OFF + h + 1, :]
        m_prev = ms[:, h:h + 1]
        dmat = jnp.where(causal, b_col - b_row + i_row, -jnp.inf)
        inter = b_col + m_prev
        m_t = jnp.maximum(inter, jnp.max(dmat, axis=-1, keepdims=True))
        w_intra = jnp.exp(dmat - m_t)
        w_inter = jnp.exp(inter - m_t)
        s = lax.dot_general(qh, kh, NT_DIMS, preferred_element_type=f32) * w_intra
        c_h = cs[h]
        n_h = ns[h:h + 1, :]
        num = _dot(s.astype(bf16), vh) + w_inter * _dot(qh, c_h.astype(bf16))
        qn = jnp.sum(qh.astype(f32) * n_h, axis=-1, keepdims=True)
        den = jnp.sum(s, axis=-1, keepdims=True) + w_inter * qn
        hv = num / jnp.maximum(jnp.abs(den), jnp.exp(-m_t))
        b_last = b_col[q - 1:q, :]
        m_end = m_t[q - 1:q, :]
        w_c = jnp.exp(b_last + m_prev - m_end)
        w_s = jnp.exp(b_last - b_col + i_col - m_end)
        kw = kh.astype(f32) * w_s
        cs[h] = w_c * c_h + lax.dot_general(kw.astype(bf16), vh, TN_DIMS, preferred_element_type=f32)
        ns[h:h + 1, :] = w_c * n_h + jnp.sum(kw, axis=0, keepdims=True)
        ms[...] = jnp.where(lane == h, m_end, ms[...])
        hn = hv * lax.rsqrt(jnp.mean(jnp.square(hv), axis=-1, keepdims=True) + EPS) * nw_ref[:, sl]
        h_ref[:, sl] = (hn * jax.nn.sigmoid(o_ref[:, sl])).astype(h_ref.dtype)

    @pl.when(is_last)
    def _():
        cout_ref[...] = cs[...]
        nout_ref[...] = ns[...]
        mout_ref[...] = ms[...]


def _layer_norm(r, g, b):
    mu = jnp.mean(r, axis=-1, keepdims=True)
    var = jnp.mean(jnp.square(r - mu), axis=-1, keepdims=True)
    return (r - mu) * lax.rsqrt(var + EPS) * g + b


def _merge_kernel(ys_ref, hm_ref, ga_ref, gb_ref, x_ref, wa_ref, wb_ref, wo_ref, g_ref, b_ref, o_ref):
    br_a = _dot(ys_ref[...], wa_ref[...])
    br_b = _dot(hm_ref[...], wb_ref[...])
    merged = jax.nn.sigmoid(ga_ref[...]) * br_a + jax.nn.sigmoid(gb_ref[...]) * br_b
    mix = _dot(merged.astype(bf16), wo_ref[...])
    o_ref[...] = _layer_norm(ALPHA * x_ref[...] + mix, g_ref[...], b_ref[...])


def _merge(ys, hm, p32, x, wa, wb, wo, g, b, tm):
    m = x.shape[0]
    row = lambda blk: pl.BlockSpec((tm, D_MODEL), lambda i: (i, blk))
    full = lambda shape: pl.BlockSpec(shape, lambda i: (0, 0))
    return pl.pallas_call(
        _merge_kernel,
        out_shape=jax.ShapeDtypeStruct((m, D_MODEL), f32),
        grid=(m // tm,),
        in_specs=[row(0), row(0), row(P32_GA), row(P32_GB), row(0),
                  full((D_MODEL, D_MODEL)), full((D_MODEL, D_MODEL)), full((D_MODEL, D_MODEL)),
                  full((1, D_MODEL)), full((1, D_MODEL))],
        out_specs=row(0),
        compiler_params=pltpu.CompilerParams(dimension_semantics=("parallel",)),
        name="merge",
    )(ys, hm, p32, p32, x, wa, wb, wo, g, b)


def _ffn_kernel(x_ref, hin_ref, wup_ref, cw_ref, cb_ref, wdn_ref, g_ref, b_ref,
                o_ref, sout_ref, xb, xp, carry, acc, *, tm, stride, hdr, so_start, so_rows):
    j = pl.program_id(1)
    c = pl.program_id(2)

    @pl.when(c == 0)
    def _():
        xb[...] = x_ref[...].astype(bf16)
        acc[...] = jnp.zeros_like(acc)

    u = _dot(xb[...], wup_ref[...])

    @pl.when(j == 0)
    def _():
        xp[0:hdr, :] = hin_ref[...]

    @pl.when(j > 0)
    def _():
        xp[0:hdr, :] = carry[c]

    xp[hdr:hdr + tm, :] = u
    carry[c] = u[tm - hdr:tm, :]
    sout_ref[...] = u[so_start:so_start + so_rows, :]

    w = cw_ref[...]
    conv = xp[hdr - 2 * stride:hdr - 2 * stride + tm, :] * w[0:1, :]
    conv = conv + xp[hdr - stride:hdr - stride + tm, :] * w[1:2, :]
    conv = conv + u * w[2:3, :]
    conv = conv + cb_ref[...]
    act = _silu(conv[:, :FF_CH]) * conv[:, FF_CH:]
    acc[...] += _dot(act.astype(bf16), wdn_ref[...])

    @pl.when(c == FF_NCH - 1)
    def _():
        o_ref[...] = _layer_norm(ALPHA * x_ref[...] + acc[...], g_ref[...], b_ref[...])


def _ffn(x, hin, wup, cw, cb, wdn, g, b, *, groups, tm, stride, hdr, so_start, so_rows):
    m = x.shape[0]
    tiles = m // (groups * tm)
    kern = functools.partial(_ffn_kernel, tm=tm, stride=stride, hdr=hdr, so_start=so_start, so_rows=so_rows)
    return pl.pallas_call(
        kern,
        out_shape=(jax.ShapeDtypeStruct((m, D_MODEL), f32),
                   jax.ShapeDtypeStruct((groups, tiles, FF_NCH, so_rows, 2 * FF_CH), f32)),
        grid=(groups, tiles, FF_NCH),
        in_specs=[pl.BlockSpec((tm, D_MODEL), lambda s, j, c: (s * tiles + j, 0)),
                  pl.BlockSpec((None, None, hdr, 2 * FF_CH), lambda s, j, c: (s, c, 0, 0)),
                  pl.BlockSpec((None, D_MODEL, 2 * FF_CH), lambda s, j, c: (c, 0, 0)),
                  pl.BlockSpec((None, FFN_CONV, 2 * FF_CH), lambda s, j, c: (c, 0, 0)),
                  pl.BlockSpec((None, 1, 2 * FF_CH), lambda s, j, c: (c, 0, 0)),
                  pl.BlockSpec((None, FF_CH, D_MODEL), lambda s, j, c: (c, 0, 0)),
                  pl.BlockSpec((1, D_MODEL), lambda s, j, c: (0, 0)),
                  pl.BlockSpec((1, D_MODEL), lambda s, j, c: (0, 0))],
        out_specs=(pl.BlockSpec((tm, D_MODEL), lambda s, j, c: (s * tiles + j, 0)),
                   pl.BlockSpec((None, None, None, so_rows, 2 * FF_CH), lambda s, j, c: (s, j, c, 0, 0))),
        scratch_shapes=[pltpu.VMEM((tm, D_MODEL), bf16),
                        pltpu.VMEM((hdr + tm, 2 * FF_CH), f32),
                        pltpu.VMEM((FF_NCH, hdr, 2 * FF_CH), f32),
                        pltpu.VMEM((tm, D_MODEL), f32)],
        compiler_params=pltpu.CompilerParams(
            dimension_semantics=("parallel", "arbitrary", "arbitrary")),
        name="ffn",
    )(x, hin, wup, cw, cb, wdn, g, b)


def _bc_pad(a):
    lead = a.shape[:-1]
    a = a.reshape(lead + (4, SSD_STATE))
    a = jnp.pad(a, [(0, 0)] * len(lead) + [(0, 0), (0, LANES - SSD_STATE)])
    return a.reshape(lead + (BC_PAD,))


def _bc_unpad(a):
    lead = a.shape[:-1]
    return a.reshape(lead + (4, LANES))[..., :SSD_STATE].reshape(lead + (SSD_BC,))


def _ff_chunk(a):
    lead = a.shape[:-1]
    a = a.reshape(lead + (2, FF_NCH, FF_CH))
    a = jnp.moveaxis(a, -3, -2)
    return a.reshape(lead + (FF_NCH, 2 * FF_CH))


def _ff_unchunk(a):
    lead = a.shape[:-2]
    a = a.reshape(lead + (FF_NCH, 2, FF_CH))
    a = jnp.moveaxis(a, -3, -2)
    return a.reshape(lead + (2 * D_FF,))


def _pad_lanes(v, off, width=SMALL):
    out = jnp.zeros((1, width), f32)
    return out.at[0, off:off + v.shape[0]].set(v.astype(f32))


def _prep_layer(w_in, ssd_conv_w, ssd_conv_b, ssd_dt_bias, ssd_a_log, ssd_d, ssd_norm_w, mlstm_gate_b,
                mlstm_norm_w, w_branch_a, w_branch_b, w_out, ln1_g, ln1_b, ffn_w_up, ffn_conv_w,
                ffn_conv_b, ffn_w_down, ln2_g, ln2_b):
    d = D_MODEL
    o_z, o_xbc, o_dt = 0, d, d + d + SSD_BC
    o_q = o_dt + SSD_HEADS
    o_k, o_v = o_q + d, o_q + 2 * d
    o_if = o_q + 3 * d
    o_o = o_if + 2 * MLSTM_HEADS
    o_g = o_o + d
    cols = lambda a, n: w_in[:, a:a + n]
    small = jnp.concatenate([cols(o_dt, SSD_HEADS), cols(o_if, 2 * MLSTM_HEADS),
                             jnp.zeros((d, SMALL - SSD_HEADS - 2 * MLSTM_HEADS), w_in.dtype)], axis=1)
    w32 = jnp.concatenate([cols(o_z, d), cols(o_o, d), cols(o_g, d), cols(o_g + d, d), cols(o_xbc, d),
                           _bc_pad(cols(o_xbc + d, SSD_BC)), small,
                           jnp.zeros((d, P32_W - P32_SM_OFF - SMALL), w_in.dtype)], axis=1).astype(bf16)
    wqkv = cols(o_q, 3 * d).astype(bf16)
    e = (np.arange(D_MODEL)[None, :] // SSD_HEAD_DIM == np.arange(LANES)[:, None])
    return dict(
        w32=w32, wqkv=wqkv,
        cwx=ssd_conv_w[:, :d], cbx=ssd_conv_b[None, :d],
        cwb=_bc_pad(ssd_conv_w[:, d:]), cbb=_bc_pad(ssd_conv_b[None, d:]),
        dtb=_pad_lanes(ssd_dt_bias, DT_OFF), alog=_pad_lanes(ssd_a_log, DT_OFF),
        dexp=jnp.repeat(ssd_d.astype(f32), SSD_HEAD_DIM)[None, :], ssd_nw=ssd_norm_w[None, :],
        e=jnp.asarray(e, bf16),
        gate_b=_pad_lanes(mlstm_gate_b, I_OFF), mlstm_nw=mlstm_norm_w[None, :],
        wa=w_branch_a.astype(bf16), wb=w_branch_b.astype(bf16), wo=w_out.astype(bf16),
        ln1_g=ln1_g[None, :], ln1_b=ln1_b[None, :],
        wup=jnp.moveaxis(_ff_chunk(ffn_w_up), 1, 0).astype(bf16),
        fcw=jnp.moveaxis(_ff_chunk(ffn_conv_w), 1, 0),
        fcb=_ff_chunk(ffn_conv_b)[:, None, :],
        wdn=ffn_w_down.reshape(FF_NCH, FF_CH, D_MODEL).astype(bf16),
        ln2_g=ln2_g[None, :], ln2_b=ln2_b[None, :],
    )


class _Path:
    def __init__(self, batch, length, q, lr, time_major):
        self.batch, self.length, self.q, self.lr, self.time_major = batch, length, q, lr, time_major
        self.nc = length // q
        self.rows = batch * length

    def view(self, a):
        c = a.shape[1]
        if self.time_major:
            return a.reshape(self.length, self.batch * c)
        return a.reshape(self.batch, self.length, c)

    def tile(self, width, blk, total):
        if self.time_major:
            per = total // width
            return pl.BlockSpec((self.q, width), lambda b, c: (0, b * per + blk))
        return pl.BlockSpec((None, self.q, width), lambda b, c: (b, c, blk))

    def out_shape(self, width, dtype):
        if self.time_major:
            return jax.ShapeDtypeStruct((self.length, self.batch * width), dtype)
        return jax.ShapeDtypeStruct((self.batch, self.length, width), dtype)


def _seq_spec(*shape):
    zeros = (0,) * len(shape)
    return pl.BlockSpec((None,) + shape, lambda b, c: (b,) + zeros)


def _const_spec(*shape):
    zeros = (0,) * len(shape)
    return pl.BlockSpec(shape, lambda b, c: zeros)


def _ssd(path, p32, csx, csb, h0, w):
    q = path.q
    pv = path.view(p32)
    kern = functools.partial(_ssd_kernel, q=q, lr=path.lr, nc=path.nc)
    hp = SSD_HEADS * SSD_HEAD_DIM
    return pl.pallas_call(
        kern,
        out_shape=(path.out_shape(D_MODEL, bf16),
                   jax.ShapeDtypeStruct((path.batch, SSD_CONV - 1, D_MODEL), f32),
                   jax.ShapeDtypeStruct((path.batch, SSD_CONV - 1, BC_PAD), f32),
                   jax.ShapeDtypeStruct((path.batch, hp, SSD_STATE), f32)),
        grid=(path.batch, path.nc),
        in_specs=[path.tile(D_MODEL, P32_Z, P32_W), path.tile(D_MODEL, P32_XS, P32_W),
                  path.tile(BC_PAD, P32_BC_OFF // BC_PAD, P32_W), path.tile(SMALL, P32_SM_OFF // SMALL, P32_W),
                  _seq_spec(SSD_CONV - 1, D_MODEL), _seq_spec(SSD_CONV - 1, BC_PAD), _seq_spec(hp, SSD_STATE),
                  _const_spec(SSD_CONV, D_MODEL), _const_spec(1, D_MODEL),
                  _const_spec(SSD_CONV, BC_PAD), _const_spec(1, BC_PAD),
                  _const_spec(1, SMALL), _const_spec(1, SMALL), _const_spec(1, D_MODEL), _const_spec(1, D_MODEL),
                  _const_spec(LANES, D_MODEL)],
        out_specs=(path.tile(D_MODEL, 0, D_MODEL), _seq_spec(SSD_CONV - 1, D_MODEL),
                   _seq_spec(SSD_CONV - 1, BC_PAD), _seq_spec(hp, SSD_STATE)),
        scratch_shapes=[pltpu.VMEM((SUBLANES + q, D_MODEL), f32), pltpu.VMEM((SUBLANES + q, BC_PAD), f32),
                        pltpu.VMEM((SSD_STATE, hp), f32), pltpu.VMEM((q, D_MODEL), f32)],
        compiler_params=pltpu.CompilerParams(dimension_semantics=("parallel", "arbitrary")),
        name="ssd",
    )(pv, pv, pv, pv, csx, csb, h0, w["cwx"], w["cbx"], w["cwb"], w["cbb"], w["dtb"], w["alog"],
      w["dexp"], w["ssd_nw"], w["e"])


def _mlstm(path, qkv, p32, c0, n0, m0, w):
    q = path.q
    qv = path.view(qkv)
    pv = path.view(p32)
    kern = functools.partial(_mlstm_kernel, q=q, lr=path.lr, nc=path.nc)
    hd = MLSTM_HEAD_DIM
    return pl.pallas_call(
        kern,
        out_shape=(path.out_shape(D_MODEL, bf16),
                   jax.ShapeDtypeStruct((path.batch, MLSTM_HEADS, hd, hd), f32),
                   jax.ShapeDtypeStruct((path.batch, MLSTM_HEADS, hd), f32),
                   jax.ShapeDtypeStruct((path.batch, 1, SMALL), f32)),
        grid=(path.batch, path.nc),
        in_specs=[path.tile(D_MODEL, 0, 3 * D_MODEL), path.tile(D_MODEL, 1, 3 * D_MODEL),
                  path.tile(D_MODEL, 2, 3 * D_MODEL), path.tile(D_MODEL, P32_O, P32_W),
                  path.tile(SMALL, P32_SM_OFF // SMALL, P32_W),
                  _seq_spec(MLSTM_HEADS, hd, hd), _seq_spec(MLSTM_HEADS, hd), _seq_spec(1, SMALL),
                  _const_spec(1, SMALL), _const_spec(1, D_MODEL)],
        out_specs=(path.tile(D_MODEL, 0, D_MODEL), _seq_spec(MLSTM_HEADS, hd, hd),
                   _seq_spec(MLSTM_HEADS, hd), _seq_spec(1, SMALL)),
        scratch_shapes=[pltpu.VMEM((MLSTM_HEADS, hd, hd), f32), pltpu.VMEM((MLSTM_HEADS, hd), f32),
                        pltpu.VMEM((1, SMALL), f32)],
        compiler_params=pltpu.CompilerParams(dimension_semantics=("parallel", "arbitrary")),
        name="mlstm",
    )(qv, qv, qv, pv, pv, c0, n0, m0, w["gate_b"], w["mlstm_nw"])


def _run_layer(path, x, st, w, tiles):
    p32 = _proj(x, w["w32"], f32, tiles["proj_tm"], 512)
    qkv = _proj(x, w["wqkv"], bf16, tiles["proj_tm"], 512)
    ys, ncsx, ncsb, h_new = _ssd(path, p32, st["csx"], st["csb"], st["h"], w)
    hm, c_new, n_new, m_new = _mlstm(path, qkv, p32, st["c"], st["n"], st["m"], w)
    x1 = _merge(ys.reshape(path.rows, D_MODEL), hm.reshape(path.rows, D_MODEL), p32, x,
                w["wa"], w["wb"], w["wo"], w["ln1_g"], w["ln1_b"], tiles["merge_tm"])
    x2, ffn_state = _ffn(x1, st["ffn"], w["wup"], w["fcw"], w["fcb"], w["wdn"], w["ln2_g"], w["ln2_b"],
                         **tiles["ffn"])
    new = dict(csx=ncsx, csb=ncsb, h=h_new, c=c_new, n=n_new, m=m_new, ffn=ffn_state)
    return x2, new


def _trunk(path, x, states, weights, tiles, ffn_in, ffn_out):
    outs = []
    for layer in range(DEPTH):
        s_ssd, s_conv, s_c, s_n, s_m, s_ffn = (s[layer] for s in states)
        b = path.batch
        st = dict(
            csx=s_conv[:, :, :D_MODEL], csb=_bc_pad(s_conv[:, :, D_MODEL:]),
            h=s_ssd.reshape(b, SSD_HEADS * SSD_HEAD_DIM, SSD_STATE),
            c=s_c, n=s_n,
            m=jnp.pad(s_m, ((0, 0), (0, SMALL - MLSTM_HEADS)))[:, None, :],
            ffn=ffn_in(s_ffn),
        )
        x, new = _run_layer(path, x, st, weights[layer], tiles)
        outs.append((
            new["h"].reshape(b, SSD_HEADS, SSD_HEAD_DIM, SSD_STATE),
            jnp.concatenate([new["csx"], _bc_unpad(new["csb"])], axis=-1),
            new["c"], new["n"], new["m"][:, 0, :MLSTM_HEADS],
            ffn_out(new["ffn"]),
        ))
    stacked = tuple(jnp.stack([o[i] for o in outs]) for i in range(6))
    return x, stacked


SAMPLE_PAD_LEN = SUBLANES


def kernel(x_prompt, x_sample, state_ssd, state_ssd_conv, state_mlstm_c, state_mlstm_n, state_mlstm_m,
           state_ffn_conv, w_in, ssd_conv_w, ssd_conv_b, ssd_dt_bias, ssd_a_log, ssd_d, ssd_norm_w,
           mlstm_gate_b, mlstm_norm_w, w_branch_a, w_branch_b, w_out, ln1_g, ln1_b, ffn_w_up, ffn_conv_w,
           ffn_conv_b, ffn_w_down, ln2_g, ln2_b):
    params = (w_in, ssd_conv_w, ssd_conv_b, ssd_dt_bias, ssd_a_log, ssd_d, ssd_norm_w, mlstm_gate_b,
              mlstm_norm_w, w_branch_a, w_branch_b, w_out, ln1_g, ln1_b, ffn_w_up, ffn_conv_w,
              ffn_conv_b, ffn_w_down, ln2_g, ln2_b)
    weights = [_prep_layer(*(p[layer] for p in params)) for layer in range(DEPTH)]

    bp, lp, _ = x_prompt.shape
    prompt = _Path(bp, lp, CHUNK, CHUNK, time_major=False)
    zeros = lambda ref: jnp.zeros((DEPTH, bp) + ref.shape[2:], f32)
    p_states = (zeros(state_ssd), zeros(state_ssd_conv), zeros(state_mlstm_c), zeros(state_mlstm_n),
                zeros(state_mlstm_m), zeros(state_ffn_conv))
    p_tm = 1024
    p_tiles = dict(proj_tm=2048, merge_tm=512,
                   ffn=dict(groups=bp, tm=p_tm, stride=1, hdr=SUBLANES, so_start=p_tm - SUBLANES,
                            so_rows=SUBLANES))

    def p_ffn_in(s):
        s = jnp.pad(s, ((0, 0), (SUBLANES - (FFN_CONV - 1), 0), (0, 0)))
        return jnp.moveaxis(_ff_chunk(s), 2, 1)

    def p_ffn_out(s):
        return _ff_unchunk(jnp.moveaxis(s[:, -1], 1, 2))[:, SUBLANES - (FFN_CONV - 1):, :]

    y_p, st_p = _trunk(prompt, x_prompt.reshape(bp * lp, D_MODEL), p_states, weights, p_tiles,
                       p_ffn_in, p_ffn_out)
    y_prompt = y_p.reshape(bp, lp, D_MODEL)

    bs, ls, _ = x_sample.shape
    lpad = SAMPLE_PAD_LEN
    sample = _Path(bs, lpad, lpad, ls, time_major=True)
    xs_tm = jnp.moveaxis(jnp.pad(x_sample, ((0, 0), (0, lpad - ls), (0, 0))), 0, 1)
    s_states = (state_ssd, state_ssd_conv, state_mlstm_c, state_mlstm_n, state_mlstm_m, state_ffn_conv)
    s_rows = bs * lpad
    s_tiles = dict(proj_tm=s_rows, merge_tm=512,
                   ffn=dict(groups=1, tm=s_rows, stride=bs, hdr=(FFN_CONV - 1) * bs,
                            so_start=(ls - (FFN_CONV - 1)) * bs, so_rows=(FFN_CONV - 1) * bs))

    def s_ffn_in(s):
        s = jnp.moveaxis(s, 0, 1).reshape((FFN_CONV - 1) * bs, 2 * D_FF)
        return jnp.moveaxis(_ff_chunk(s), 1, 0)[None]

    def s_ffn_out(s):
        s = _ff_unchunk(jnp.moveaxis(s[0, 0], 0, 1)).reshape(FFN_CONV - 1, bs, 2 * D_FF)
        return jnp.moveaxis(s, 0, 1)

    y_s, st_s = _trunk(sample, xs_tm.reshape(s_rows, D_MODEL), s_states, weights, s_tiles,
                       s_ffn_in, s_ffn_out)
    y_sample = jnp.moveaxis(y_s.reshape(lpad, bs, D_MODEL)[:ls], 0, 1)

    return (y_prompt, y_sample, st_p[0], st_s[0], st_p[1], st_s[1], st_p[2], st_s[2], st_p[3], st_s[3],
            st_p[4], st_s[4], st_p[5], st_s[5])
```

```python
import functools

import jax
import jax.numpy as jnp
import numpy as np
from jax import lax
from jax.experimental import pallas as pl
from jax.experimental.pallas import tpu as pltpu

f32 = jnp.float32
bf16 = jnp.bfloat16

D_MODEL = 1024
DEPTH = 2
SSD_HEADS = 16
SSD_HEAD_DIM = 64
SSD_STATE = 64
SSD_GROUPS = 2
SSD_CONV = 4
SSD_BC = 2 * SSD_GROUPS * SSD_STATE
SSD_HP = SSD_HEADS * SSD_HEAD_DIM
MLSTM_HEADS = 4
MLSTM_HEAD_DIM = 256
CHUNK = 128
D_FF = 2816
FFN_CONV = 3
ALPHA = (2 * DEPTH) ** 0.25
EPS = 1e-5

LANES = 128
SUBLANES = 8
SMALL = LANES
DT_OFF, I_OFF, F_OFF = 0, 16, 20
P32_Z, P32_O, P32_GA, P32_GB, P32_XS = 0, 1, 2, 3, 4
P32_BC_OFF = 5 * D_MODEL
P32_SM_OFF = P32_BC_OFF + SSD_BC
P32_W = P32_SM_OFF + 2 * SMALL
PROJ_TN = 512
FF_CH = 256
FF_NCH = D_FF // FF_CH
NEG_BIG = -1e30
SAMPLE_PAD_LEN = SUBLANES

NT_DIMS = (((1,), (1,)), ((), ()))
TN_DIMS = (((0,), (0,)), ((), ()))


def _dot(a, b):
    return jnp.dot(a, b, preferred_element_type=f32)


def _split3(x):
    hi = x.astype(bf16)
    r = x - hi.astype(f32)
    mid = r.astype(bf16)
    lo = (r - mid.astype(f32)).astype(bf16)
    return hi, mid, lo


def _dot01_rhs(x, e):
    hi, mid, lo = _split3(x)
    return _dot(hi, e) + _dot(mid, e) + _dot(lo, e)


def _dot01_lhs(t, x):
    hi, mid, lo = _split3(x)
    return _dot(t, hi) + _dot(t, mid) + _dot(t, lo)


def _softplus(x):
    return jnp.maximum(x, 0.0) + jnp.log1p(jnp.exp(-jnp.abs(x)))


def _silu(x):
    return x * jax.nn.sigmoid(x)


def _tri(q):
    row = lax.broadcasted_iota(jnp.int32, (q, q), 0)
    col = lax.broadcasted_iota(jnp.int32, (q, q), 1)
    return row >= col


def _valid_rows(q, width, lr, is_last):
    row = lax.broadcasted_iota(jnp.int32, (q, width), 0)
    return row < jnp.where(is_last, lr, q)


def _layer_norm(r, g, b):
    mu = jnp.mean(r, axis=-1, keepdims=True)
    var = jnp.mean(jnp.square(r - mu), axis=-1, keepdims=True)
    return (r - mu) * lax.rsqrt(var + EPS) * g + b


def _layer_spec(layer, *shape, **kw):
    zeros = (0,) * len(shape)
    return pl.BlockSpec((None,) + shape, lambda *_: (layer,) + zeros, **kw)


def _seq_spec(layer, *shape):
    zeros = (0,) * len(shape)
    return pl.BlockSpec((None, None) + shape, lambda b, c: (layer, b) + zeros)


def _stacked_call(kern, *, name, grid, inputs, in_specs, out_shape, out_specs, stacked, scratch_shapes,
                  dimension_semantics, vmem_limit_bytes=None):
    prev = [(i, a) for i, a in sorted(stacked.items()) if a is not None]
    n_in = len(inputs)

    def body(*refs):
        kern(*refs[:n_in], *refs[n_in + len(prev):])

    return pl.pallas_call(
        body,
        out_shape=out_shape,
        grid=grid,
        in_specs=list(in_specs) + [pl.BlockSpec(memory_space=pl.ANY)] * len(prev),
        out_specs=out_specs,
        scratch_shapes=scratch_shapes,
        input_output_aliases={n_in + k: i for k, (i, _) in enumerate(prev)},
        compiler_params=pltpu.CompilerParams(dimension_semantics=dimension_semantics,
                                             vmem_limit_bytes=vmem_limit_bytes),
        name=name,
    )(*inputs, *[a for _, a in prev])


def _proj_kernel(x_ref, w_ref, o_ref, xb):
    @pl.when(pl.program_id(1) == 0)
    def _():
        xb[...] = x_ref[...].astype(bf16)

    o_ref[...] = _dot(xb[...], w_ref[...]).astype(o_ref.dtype)


def _proj(x, w, layer, out_dtype, tm):
    m, k = x.shape
    n = w.shape[2]
    tn = PROJ_TN
    return pl.pallas_call(
        _proj_kernel,
        out_shape=jax.ShapeDtypeStruct((m, n), out_dtype),
        grid=(m // tm, n // tn),
        in_specs=[pl.BlockSpec((tm, k), lambda i, j: (i, 0)),
                  pl.BlockSpec((None, k, tn), lambda i, j: (layer, 0, j))],
        out_specs=pl.BlockSpec((tm, tn), lambda i, j: (i, j)),
        scratch_shapes=[pltpu.VMEM((tm, k), bf16)],
        compiler_params=pltpu.CompilerParams(dimension_semantics=("parallel", "arbitrary")),
        name="proj",
    )(x, w)


def _ssd_kernel(*refs, q, lr, nc, has_state):
    z_ref, xs_ref, bc_ref, sm_ref = refs[:4]
    refs = refs[4:]
    if has_state:
        csx_ref, csb_ref, h0_ref = refs[:3]
        refs = refs[3:]
    (cwx_ref, cbx_ref, cwb_ref, cbb_ref, dtb_ref, alog_ref, dexp_ref, nw_ref, e_ref,
     y_ref, ncsx_ref, ncsb_ref, hout_ref, xpx, xpb, ht, yb) = refs
    c = pl.program_id(1)
    is_last = c == nc - 1
    hdr = SUBLANES
    lo = hdr - (SSD_CONV - 1)
    n2 = SSD_GROUPS * SSD_STATE
    half = SSD_HP // SSD_GROUPS
    block_diag = ((lax.broadcasted_iota(jnp.int32, (n2, SSD_HP), 0) < SSD_STATE)
                  == (lax.broadcasted_iota(jnp.int32, (n2, SSD_HP), 1) < half))

    @pl.when(c == 0)
    def _():
        if has_state:
            h_t = h0_ref[...].T
            ht[...] = jnp.where(block_diag, jnp.concatenate([h_t, h_t], axis=0), 0.0)
            xpx[lo:hdr, :] = csx_ref[...]
            xpb[lo:hdr, :] = csb_ref[...]
        else:
            ht[...] = jnp.zeros_like(ht)
            xpx[lo:hdr, :] = jnp.zeros((SSD_CONV - 1, SSD_HP), f32)
            xpb[lo:hdr, :] = jnp.zeros((SSD_CONV - 1, SSD_BC), f32)

    xpx[hdr:hdr + q, :] = xs_ref[...]
    xpb[hdr:hdr + q, :] = bc_ref[...]

    def conv(xp, w_ref, b_ref):
        w = w_ref[...]
        acc = xp[lo:lo + q, :] * w[0:1, :]
        for j in range(1, SSD_CONV):
            acc = acc + xp[lo + j:lo + j + q, :] * w[j:j + 1, :]
        return acc + b_ref[...]

    cx = conv(xpx, cwx_ref, cbx_ref)
    cb = conv(xpb, cwb_ref, cbb_ref)

    @pl.when(is_last)
    def _():
        ncsx_ref[...] = xpx[lo + lr:hdr + lr, :]
        ncsb_ref[...] = xpb[lo + lr:hdr + lr, :]

    tail_x = xpx[lo + q:hdr + q, :]
    tail_b = xpb[lo + q:hdr + q, :]
    xpx[lo:hdr, :] = tail_x
    xpb[lo:hdr, :] = tail_b

    xs = _silu(cx)
    bcv = _silu(cb)
    bm = bcv[:, :n2].astype(bf16)
    cm = bcv[:, n2:]
    lane_g0 = lax.broadcasted_iota(jnp.int32, (q, n2), 1) < SSD_STATE

    dt = _softplus(sm_ref[...] + dtb_ref[...])
    if lr < q:
        dt = jnp.where(_valid_rows(q, SMALL, lr, is_last), dt, 0.0)
    a = -jnp.exp(alog_ref[...])
    d_a = dt * a
    causal = _tri(q)
    tril = jnp.where(causal, 1.0, 0.0).astype(bf16)
    acs = _dot01_lhs(tril, d_a)
    acs_t = acs.T
    e = e_ref[...]
    acs_x = _dot01_rhs(acs, e)
    dt_x = _dot01_rhs(dt, e)
    last_x = acs_x[q - 1:q, :]
    xdt = xs * dt_x
    xdt_b = xdt.astype(bf16)
    xdtw = (xdt * jnp.exp(last_x - acs_x)).astype(bf16)
    lane_lo = lax.broadcasted_iota(jnp.int32, (q, LANES), 1) < SSD_HEAD_DIM

    heads_per_group = SSD_HEADS // SSD_GROUPS
    for g in range(SSD_GROUPS):
        cg = jnp.where(lane_g0 if g == 0 else jnp.logical_not(lane_g0), cm, 0.0).astype(bf16)
        cbm = lax.dot_general(cg, bm, NT_DIMS, preferred_element_type=f32)
        for j in range(heads_per_group // 2):
            p = g * (heads_per_group // 2) + j
            xpair = xdt_b[:, p * LANES:(p + 1) * LANES]
            ys = []
            for hh in (2 * p, 2 * p + 1):
                seg = acs[:, hh:hh + 1] - acs_t[hh:hh + 1, :]
                decay = jnp.exp(jnp.where(causal, seg, -jnp.inf))
                ys.append(_dot((cbm * decay).astype(bf16), xpair))
            yb[:, p * LANES:(p + 1) * LANES] = jnp.where(lane_lo, ys[0], ys[1])

    h_prev = ht[...]
    y_off = _dot(cm.astype(bf16), h_prev.astype(bf16))
    upd = lax.dot_general(bm, xdtw, TN_DIMS, preferred_element_type=f32)
    ht[...] = jnp.exp(last_x) * h_prev + jnp.where(block_diag, upd, 0.0)

    y = yb[...] + y_off * jnp.exp(acs_x) + dexp_ref[...] * xs
    y = y * _silu(z_ref[...])
    y = y * lax.rsqrt(jnp.mean(jnp.square(y), axis=-1, keepdims=True) + EPS) * nw_ref[...]
    y_ref[...] = y.astype(y_ref.dtype)

    @pl.when(is_last)
    def _():
        h_new = ht[...]
        hout_ref[...] = (h_new[:SSD_STATE, :] + h_new[SSD_STATE:, :]).T


def _mlstm_kernel(*refs, q, lr, nc, has_state):
    q_ref, k_ref, v_ref, o_ref, sm_ref = refs[:5]
    refs = refs[5:]
    if has_state:
        c0_ref, n0_ref, m0_ref = refs[:3]
        refs = refs[3:]
    gb_ref, nw_ref, h_ref, cout_ref, nout_ref, mout_ref, cs, ns, ms = refs
    c = pl.program_id(1)
    is_last = c == nc - 1

    @pl.when(c == 0)
    def _():
        if has_state:
            cs[...] = c0_ref[...]
            ns[...] = n0_ref[...]
            ms[...] = m0_ref[...]
        else:
            cs[...] = jnp.zeros_like(cs)
            ns[...] = jnp.zeros_like(ns)
            ms[...] = jnp.zeros_like(ms)

    sm = sm_ref[...] + gb_ref[...]
    logf = -_softplus(-sm)
    ipre = sm
    if lr < q:
        valid = _valid_rows(q, SMALL, lr, is_last)
        logf = jnp.where(valid, logf, 0.0)
        ipre = jnp.where(valid, ipre, NEG_BIG)
    causal = _tri(q)
    tril = jnp.where(causal, 1.0, 0.0).astype(bf16)
    bcum = _dot01_lhs(tril, logf)
    bcum_t = bcum.T
    ipre_t = ipre.T
    lane = lax.broadcasted_iota(jnp.int32, (1, SMALL), 1)
    k_scale = MLSTM_HEAD_DIM ** -0.5

    for h in range(MLSTM_HEADS):
        sl = slice(h * MLSTM_HEAD_DIM, (h + 1) * MLSTM_HEAD_DIM)
        qh = q_ref[:, sl]
        kh = k_ref[:, sl] * k_scale
        vh = v_ref[:, sl]
        b_col = bcum[:, F_OFF + h:F_OFF + h + 1]
        b_row = bcum_t[F_OFF + h:F_OFF + h + 1, :]
        i_col = ipre[:, I_OFF + h:I_OFF + h + 1]
        i_row = ipre_t[I_OFF + h:I_OFF + h + 1, :]
        m_prev = ms[:, h:h + 1]
        dmat = jnp.where(causal, b_col - b_row + i_row, -jnp.inf)
        inter = b_col + m_prev
        m_t = jnp.maximum(inter, jnp.max(dmat, axis=-1, keepdims=True))
        w_intra = jnp.exp(dmat - m_t)
        w_inter = jnp.exp(inter - m_t)
        s = lax.dot_general(qh, kh, NT_DIMS, preferred_element_type=f32) * w_intra
        c_h = cs[h]
        n_h = ns[h:h + 1, :]
        num = _dot(s.astype(bf16), vh) + w_inter * _dot(qh, c_h.astype(bf16))
        qn = jnp.sum(qh.astype(f32) * n_h, axis=-1, keepdims=True)
        den = jnp.sum(s, axis=-1, keepdims=True) + w_inter * qn
        hv = num / jnp.maximum(jnp.abs(den), jnp.exp(-m_t))
        b_last = b_col[q - 1:q, :]
        m_end = m_t[q - 1:q, :]
        w_c = jnp.exp(b_last + m_prev - m_end)
        w_s = jnp.exp(b_last - b_col + i_col - m_end)
        kw = kh.astype(f32) * w_s
        cs[h] = w_c * c_h + lax.dot_general(kw.astype(bf16), vh, TN_DIMS, preferred_element_type=f32)
        ns[h:h + 1, :] = w_c * n_h + jnp.sum(kw, axis=0, keepdims=True)
        ms[...] = jnp.where(lane == h, m_end, ms[...])
        hn = hv * lax.rsqrt(jnp.mean(jnp.square(hv), axis=-1, keepdims=True) + EPS) * nw_ref[:, sl]
        h_ref[:, sl] = (hn * jax.nn.sigmoid(o_ref[:, sl])).astype(h_ref.dtype)

    @pl.when(is_last)
    def _():
        cout_ref[...] = cs[...]
        nout_ref[...] = ns[...]
        mout_ref[...] = ms[...]


def _merge_kernel(ys_ref, hm_ref, ga_ref, gb_ref, x_ref, wa_ref, wb_ref, wo_ref, g_ref, b_ref, o_ref):
    br_a = _dot(ys_ref[...], wa_ref[...])
    br_b = _dot(hm_ref[...], wb_ref[...])
    merged = jax.nn.sigmoid(ga_ref[...]) * br_a + jax.nn.sigmoid(gb_ref[...]) * br_b
    mix = _dot(merged.astype(bf16), wo_ref[...])
    o_ref[...] = _layer_norm(ALPHA * x_ref[...] + mix, g_ref[...], b_ref[...])


def _merge(ys, hm, p32, x, w, layer, tm):
    m = x.shape[0]
    row = lambda blk: pl.BlockSpec((tm, D_MODEL), lambda i: (i, blk))
    return pl.pallas_call(
        _merge_kernel,
        out_shape=jax.ShapeDtypeStruct((m, D_MODEL), f32),
        grid=(m // tm,),
        in_specs=[row(0), row(0), row(P32_GA), row(P32_GB), row(0),
                  _layer_spec(layer, D_MODEL, D_MODEL), _layer_spec(layer, D_MODEL, D_MODEL),
                  _layer_spec(layer, D_MODEL, D_MODEL), _layer_spec(layer, 1, D_MODEL),
                  _layer_spec(layer, 1, D_MODEL)],
        out_specs=row(0),
        compiler_params=pltpu.CompilerParams(dimension_semantics=("parallel",)),
        name="merge",
    )(ys, hm, p32, p32, x, w["wa"], w["wb"], w["wo"], w["ln1_g"], w["ln1_b"])


def _ffn_kernel(*refs, tm, seq_len):
    multi = seq_len > 0
    if multi:
        x_ref, st_ref, wup_ref, cw_ref, cb_ref, wdn_ref, g_ref, b_ref, o_ref, sout_ref, xp = refs
    else:
        x_ref, wup_ref, cw_ref, cb_ref, wdn_ref, g_ref, b_ref, o_ref, sout_ref, xp, carry = refs
        @pl.when(pl.program_id(1) == 0)
        def _():
            carry[...] = jnp.zeros_like(carry)

    hdr = SUBLANES
    x = x_ref[...]
    xb = x.astype(bf16)
    if multi:
        assert seq_len & (seq_len - 1) == 0
        nseq = tm // seq_len
        t = lax.broadcasted_iota(jnp.int32, (tm, FF_CH), 0) & (seq_len - 1)
        row = lax.broadcasted_iota(jnp.int32, (tm, 2 * nseq), 0)
        col = lax.broadcasted_iota(jnp.int32, (tm, 2 * nseq), 1)
        t_sel = row & (seq_len - 1)
        seq0 = lax.shift_right_logical(row - t_sel, (seq_len // 2).bit_length() - 1)
        sel_p2 = jnp.where(col == seq0 + t_sel, jnp.where(t_sel < 2, 1.0, 0.0), 0.0).astype(bf16)
        sel_p1 = jnp.where(col == seq0 + 1, jnp.where(t_sel == 0, 1.0, 0.0), 0.0).astype(bf16)
        xp[:, 0:hdr, :] = jnp.zeros((2, hdr, FF_CH), f32)

    acc = None
    for c in range(FF_NCH):
        halves = []
        for part in range(2):
            cols = slice(part * D_FF + c * FF_CH, part * D_FF + (c + 1) * FF_CH)
            u = _dot(xb, wup_ref[:, cols])
            slot = (2 * c + part) % 2
            xp[slot, hdr:hdr + tm, :] = u
            if multi:
                sout_ref[:, cols] = u
                st = st_ref[:, cols]
                p1 = jnp.where(t == 0, _dot01_lhs(sel_p1, st), xp[slot, hdr - 1:hdr - 1 + tm, :])
                p2 = jnp.where(t < 2, _dot01_lhs(sel_p2, st), xp[slot, hdr - 2:hdr - 2 + tm, :])
            else:
                xp[slot, 0:hdr, :] = carry[:, cols]
                p1 = xp[slot, hdr - 1:hdr - 1 + tm, :]
                p2 = xp[slot, hdr - 2:hdr - 2 + tm, :]
                carry[:, cols] = u[tm - hdr:tm, :]
            w = cw_ref[:, cols]
            halves.append(p2 * w[0:1, :] + p1 * w[1:2, :] + u * w[2:3, :] + cb_ref[:, cols])
        act = (_silu(halves[0]) * halves[1]).astype(bf16)
        d = _dot(act, wdn_ref[c * FF_CH:(c + 1) * FF_CH, :])
        acc = d if acc is None else acc + d

    if not multi:
        sout_ref[...] = carry[...]
    o_ref[...] = _layer_norm(ALPHA * x + acc, g_ref[...], b_ref[...])


def _ffn(x, st, w, layer, *, groups, tm, seq_len):
    m = x.shape[0]
    tiles = m // (groups * tm)
    multi = seq_len > 0
    kern = functools.partial(_ffn_kernel, tm=tm, seq_len=seq_len)
    once = dict(pipeline_mode=pl.Buffered(1))
    x_spec = pl.BlockSpec((tm, D_MODEL), lambda s, j: (s * tiles + j, 0))
    w_specs = [_layer_spec(layer, D_MODEL, 2 * D_FF, **once), _layer_spec(layer, FFN_CONV, 2 * D_FF, **once),
               _layer_spec(layer, 1, 2 * D_FF, **once), _layer_spec(layer, D_FF, D_MODEL, **once),
               _layer_spec(layer, 1, D_MODEL, **once), _layer_spec(layer, 1, D_MODEL, **once)]
    w_args = (w["wup"], w["fcw"], w["fcb"], w["wdn"], w["ln2_g"], w["ln2_b"])
    xp = pltpu.VMEM((2, SUBLANES + tm, FF_CH), f32)
    if multi:
        nst = 2 * (tm // seq_len)
        inputs = (x, st) + w_args
        in_specs = [x_spec, pl.BlockSpec((None, nst, 2 * D_FF), lambda s, j: (layer, s * tiles + j, 0))] + w_specs
        sout_shape = jax.ShapeDtypeStruct((m, 2 * D_FF), f32)
        sout_spec = pl.BlockSpec((tm, 2 * D_FF), lambda s, j: (s * tiles + j, 0))
        scratch = [xp]
    else:
        inputs = (x,) + w_args
        in_specs = [x_spec] + w_specs
        sout_shape = jax.ShapeDtypeStruct((groups, SUBLANES, 2 * D_FF), f32)
        sout_spec = pl.BlockSpec((None, SUBLANES, 2 * D_FF), lambda s, j: (s, 0, 0))
        scratch = [xp, pltpu.VMEM((SUBLANES, 2 * D_FF), f32)]
    return pl.pallas_call(
        kern,
        out_shape=(jax.ShapeDtypeStruct((m, D_MODEL), f32), sout_shape),
        grid=(groups, tiles),
        in_specs=in_specs,
        out_specs=(x_spec, sout_spec),
        scratch_shapes=scratch,
        compiler_params=pltpu.CompilerParams(dimension_semantics=("parallel", "arbitrary"),
                                             vmem_limit_bytes=56 * 1024 * 1024),
        name="ffn",
    )(*inputs)


def _pad_lanes(v, off, width=SMALL):
    out = jnp.zeros((v.shape[0], 1, width), f32)
    return out.at[:, 0, off:off + v.shape[1]].set(v.astype(f32))


def _prep_weights(w_in, ssd_conv_w, ssd_conv_b, ssd_dt_bias, ssd_a_log, ssd_d, ssd_norm_w, mlstm_gate_b,
                  mlstm_norm_w, w_branch_a, w_branch_b, w_out, ln1_g, ln1_b, ffn_w_up, ffn_conv_w,
                  ffn_conv_b, ffn_w_down, ln2_g, ln2_b):
    d = D_MODEL
    o_z, o_xbc, o_dt = 0, d, d + d + SSD_BC
    o_q = o_dt + SSD_HEADS
    o_if = o_q + 3 * d
    o_o = o_if + 2 * MLSTM_HEADS
    o_g = o_o + d
    cols = lambda a, n: w_in[:, :, a:a + n]
    zeros = lambda n: jnp.zeros((DEPTH, d, n), w_in.dtype)
    w32 = jnp.concatenate([cols(o_z, d), cols(o_o, d), cols(o_g, 2 * d), cols(o_xbc, d + SSD_BC),
                           cols(o_dt, SSD_HEADS), cols(o_if, 2 * MLSTM_HEADS),
                           zeros(P32_W - P32_SM_OFF - SSD_HEADS - 2 * MLSTM_HEADS)], axis=2).astype(bf16)
    e = (np.arange(SSD_HP)[None, :] // SSD_HEAD_DIM == np.arange(LANES)[:, None])
    row = lambda a: a[:, None, :]
    return dict(
        w32=w32, wqkv=cols(o_q, 3 * d).astype(bf16),
        cwx=ssd_conv_w[:, :, :d], cbx=row(ssd_conv_b[:, :d]),
        cwb=ssd_conv_w[:, :, d:], cbb=row(ssd_conv_b[:, d:]),
        dtb=_pad_lanes(ssd_dt_bias, DT_OFF), alog=_pad_lanes(ssd_a_log, DT_OFF),
        dexp=row(jnp.repeat(ssd_d.astype(f32), SSD_HEAD_DIM, axis=1)), ssd_nw=row(ssd_norm_w),
        e=jnp.asarray(e, bf16),
        gate_b=_pad_lanes(mlstm_gate_b, I_OFF), mlstm_nw=row(mlstm_norm_w),
        wa=w_branch_a.astype(bf16), wb=w_branch_b.astype(bf16), wo=w_out.astype(bf16),
        ln1_g=row(ln1_g), ln1_b=row(ln1_b),
        wup=ffn_w_up.astype(bf16), fcw=ffn_conv_w, fcb=row(ffn_conv_b), wdn=ffn_w_down.astype(bf16),
        ln2_g=row(ln2_g), ln2_b=row(ln2_b),
    )


class _Group:
    def __init__(self, batch, length, q, lr, proj_tm, merge_tm, ffn):
        self.batch, self.length, self.q, self.lr = batch, length, q, lr
        self.nc = length // q
        self.rows = batch * length
        self.proj_tm, self.merge_tm, self.ffn = proj_tm, merge_tm, ffn

    def tile(self, width, blk):
        nc = self.nc
        return pl.BlockSpec((self.q, width), lambda b, c: (b * nc + c, blk))


def _ssd(grp, p32, state, w, layer, prev):
    q, b = grp.q, grp.batch
    has_state = state is not None
    kern = functools.partial(_ssd_kernel, q=q, lr=grp.lr, nc=grp.nc, has_state=has_state)
    inputs = [p32, p32, p32, p32]
    in_specs = [grp.tile(D_MODEL, P32_Z), grp.tile(D_MODEL, P32_XS),
                grp.tile(SSD_BC, P32_BC_OFF // SSD_BC), grp.tile(SMALL, P32_SM_OFF // SMALL)]
    if has_state:
        inputs += [state["csx"], state["csb"], state["h"]]
        in_specs += [_seq_spec(layer, SSD_CONV - 1, D_MODEL), _seq_spec(layer, SSD_CONV - 1, SSD_BC),
                     _seq_spec(layer, SSD_HP, SSD_STATE)]
    inputs += [w["cwx"], w["cbx"], w["cwb"], w["cbb"], w["dtb"], w["alog"], w["dexp"], w["ssd_nw"], w["e"]]
    in_specs += [_layer_spec(layer, SSD_CONV, D_MODEL), _layer_spec(layer, 1, D_MODEL),
                 _layer_spec(layer, SSD_CONV, SSD_BC), _layer_spec(layer, 1, SSD_BC),
                 _layer_spec(layer, 1, SMALL), _layer_spec(layer, 1, SMALL), _layer_spec(layer, 1, D_MODEL),
                 _layer_spec(layer, 1, D_MODEL), pl.BlockSpec((LANES, SSD_HP), lambda b, c: (0, 0))]
    return _stacked_call(
        kern, name="ssd", grid=(b, grp.nc), inputs=inputs, in_specs=in_specs,
        out_shape=(jax.ShapeDtypeStruct((grp.rows, D_MODEL), bf16),
                   jax.ShapeDtypeStruct((DEPTH, b, SSD_CONV - 1, D_MODEL), f32),
                   jax.ShapeDtypeStruct((DEPTH, b, SSD_CONV - 1, SSD_BC), f32),
                   jax.ShapeDtypeStruct((DEPTH, b, SSD_HP, SSD_STATE), f32)),
        out_specs=(grp.tile(D_MODEL, 0), _seq_spec(layer, SSD_CONV - 1, D_MODEL),
                   _seq_spec(layer, SSD_CONV - 1, SSD_BC), _seq_spec(layer, SSD_HP, SSD_STATE)),
        stacked={1: prev and prev[0], 2: prev and prev[1], 3: prev and prev[2]},
        scratch_shapes=[pltpu.VMEM((SUBLANES + q, D_MODEL), f32), pltpu.VMEM((SUBLANES + q, SSD_BC), f32),
                        pltpu.VMEM((SSD_GROUPS * SSD_STATE, SSD_HP), f32), pltpu.VMEM((q, D_MODEL), f32)],
        dimension_semantics=("parallel", "arbitrary"))


def _mlstm(grp, qkv, p32, state, w, layer, prev):
    q, b = grp.q, grp.batch
    has_state = state is not None
    kern = functools.partial(_mlstm_kernel, q=q, lr=grp.lr, nc=grp.nc, has_state=has_state)
    hd = MLSTM_HEAD_DIM
    inputs = [qkv, qkv, qkv, p32, p32]
    in_specs = [grp.tile(D_MODEL, 0), grp.tile(D_MODEL, 1), grp.tile(D_MODEL, 2), grp.tile(D_MODEL, P32_O),
                grp.tile(SMALL, P32_SM_OFF // SMALL)]
    if has_state:
        inputs += [state["c"], state["n"], state["m"]]
        in_specs += [_seq_spec(layer, MLSTM_HEADS, hd, hd), _seq_spec(layer, MLSTM_HEADS, hd),
                     _seq_spec(layer, 1, SMALL)]
    inputs += [w["gate_b"], w["mlstm_nw"]]
    in_specs += [_layer_spec(layer, 1, SMALL), _layer_spec(layer, 1, D_MODEL)]
    return _stacked_call(
        kern, name="mlstm", grid=(b, grp.nc), inputs=inputs, in_specs=in_specs,
        out_shape=(jax.ShapeDtypeStruct((grp.rows, D_MODEL), bf16),
                   jax.ShapeDtypeStruct((DEPTH, b, MLSTM_HEADS, hd, hd), f32),
                   jax.ShapeDtypeStruct((DEPTH, b, MLSTM_HEADS, hd), f32),
                   jax.ShapeDtypeStruct((DEPTH, b, 1, SMALL), f32)),
        out_specs=(grp.tile(D_MODEL, 0), _seq_spec(layer, MLSTM_HEADS, hd, hd),
                   _seq_spec(layer, MLSTM_HEADS, hd), _seq_spec(layer, 1, SMALL)),
        stacked={1: prev and prev[0], 2: prev and prev[1], 3: prev and prev[2]},
        scratch_shapes=[pltpu.VMEM((MLSTM_HEADS, hd, hd), f32), pltpu.VMEM((MLSTM_HEADS, hd), f32),
                        pltpu.VMEM((1, SMALL), f32)],
        dimension_semantics=("parallel", "arbitrary"))


def _trunk(grp, x, state, w):
    ssd_out = mlstm_out = None
    ffn_out = []
    for layer in range(DEPTH):
        p32 = _proj(x, w["w32"], layer, f32, grp.proj_tm)
        qkv = _proj(x, w["wqkv"], layer, bf16, grp.proj_tm)
        ys, *ssd_out = _ssd(grp, p32, state, w, layer, ssd_out)
        hm, *mlstm_out = _mlstm(grp, qkv, p32, state, w, layer, mlstm_out)
        x1 = _merge(ys, hm, p32, x, w, layer, grp.merge_tm)
        x, s_ffn = _ffn(x1, state["ffn"] if state is not None else None, w, layer, **grp.ffn)
        ffn_out.append(s_ffn)
    return x, ssd_out, mlstm_out, ffn_out


def _unpack_states(batch, ssd_out, mlstm_out):
    csx, csb, h = ssd_out
    c, n, m = mlstm_out
    return (h.reshape(DEPTH, batch, SSD_HEADS, SSD_HEAD_DIM, SSD_STATE),
            jnp.concatenate([csx, csb], axis=-1), c, n, m[:, :, 0, :MLSTM_HEADS])


def kernel(x_prompt, x_sample, state_ssd, state_ssd_conv, state_mlstm_c, state_mlstm_n, state_mlstm_m,
           state_ffn_conv, w_in, ssd_conv_w, ssd_conv_b, ssd_dt_bias, ssd_a_log, ssd_d, ssd_norm_w,
           mlstm_gate_b, mlstm_norm_w, w_branch_a, w_branch_b, w_out, ln1_g, ln1_b, ffn_w_up, ffn_conv_w,
           ffn_conv_b, ffn_w_down, ln2_g, ln2_b):
    w = _prep_weights(w_in, ssd_conv_w, ssd_conv_b, ssd_dt_bias, ssd_a_log, ssd_d, ssd_norm_w, mlstm_gate_b,
                      mlstm_norm_w, w_branch_a, w_branch_b, w_out, ln1_g, ln1_b, ffn_w_up, ffn_conv_w,
                      ffn_conv_b, ffn_w_down, ln2_g, ln2_b)
    keep = FFN_CONV - 1

    bp, lp, _ = x_prompt.shape
    prompt = _Group(bp, lp, CHUNK, CHUNK, proj_tm=2048, merge_tm=512, ffn=dict(groups=bp, tm=512, seq_len=0))
    y_p, ssd_p, mlstm_p, ffn_p = _trunk(prompt, x_prompt.reshape(bp * lp, D_MODEL), None, w)
    st_p = _unpack_states(bp, ssd_p, mlstm_p)
    ffn_conv_p = jnp.stack(ffn_p)[:, :, SUBLANES - keep:, :]

    bs, ls, _ = x_sample.shape
    lpad = SAMPLE_PAD_LEN
    s_rows = bs * lpad
    sample = _Group(bs, lpad, lpad, ls, proj_tm=s_rows, merge_tm=512, ffn=dict(groups=1, tm=256, seq_len=lpad))
    s_state = dict(
        csx=state_ssd_conv[..., :D_MODEL], csb=state_ssd_conv[..., D_MODEL:],
        h=state_ssd.reshape(DEPTH, bs, SSD_HP, SSD_STATE),
        c=state_mlstm_c, n=state_mlstm_n,
        m=jnp.pad(state_mlstm_m, ((0, 0), (0, 0), (0, SMALL - MLSTM_HEADS)))[:, :, None, :],
        ffn=state_ffn_conv.reshape(DEPTH, bs * keep, 2 * D_FF),
    )
    xs = jnp.pad(x_sample, ((0, 0), (0, lpad - ls), (0, 0))).reshape(s_rows, D_MODEL)
    y_s, ssd_s, mlstm_s, ffn_s = _trunk(sample, xs, s_state, w)
    st_s = _unpack_states(bs, ssd_s, mlstm_s)
    ffn_conv_s = jnp.stack([u.reshape(bs, lpad, 2 * D_FF)[:, ls - keep:ls, :] for u in ffn_s])
    y_sample = y_s.reshape(bs, lpad, D_MODEL)[:, :ls, :]

    return (y_p.reshape(bp, lp, D_MODEL), y_sample, st_p[0], st_s[0], st_p[1], st_s[1], st_p[2], st_s[2],
            st_p[3], st_s[3], st_p[4], st_s[4], ffn_conv_p, ffn_conv_s)
```

```python
import functools

import jax
import jax.numpy as jnp
import numpy as np
from jax import lax
from jax.experimental import pallas as pl
from jax.experimental.pallas import tpu as pltpu

f32 = jnp.float32
bf16 = jnp.bfloat16

D_MODEL = 1024
DEPTH = 2
SSD_HEADS = 16
SSD_HEAD_DIM = 64
SSD_STATE = 64
SSD_GROUPS = 2
SSD_CONV = 4
SSD_BC = 2 * SSD_GROUPS * SSD_STATE
SSD_HP = SSD_HEADS * SSD_HEAD_DIM
MLSTM_HEADS = 4
MLSTM_HEAD_DIM = 256
CHUNK = 128
D_FF = 2816
FFN_CONV = 3
ALPHA = (2 * DEPTH) ** 0.25
EPS = 1e-5

LANES = 128
SUBLANES = 8
SMALL = LANES
DT_OFF, I_OFF, F_OFF = 0, 16, 20
P32_Z, P32_O, P32_GA, P32_GB, P32_XS = 0, 1, 2, 3, 4
P32_BC_OFF = 5 * D_MODEL
P32_SM_OFF = P32_BC_OFF + SSD_BC
P32_W = P32_SM_OFF + 2 * SMALL
PROJ_TN = 512
FF_CH = 256
FF_NCH = D_FF // FF_CH
NEG_BIG = -1e30
SAMPLE_PAD_LEN = SUBLANES

NT_DIMS = (((1,), (1,)), ((), ()))
TN_DIMS = (((0,), (0,)), ((), ()))


def _dot(a, b):
    return jnp.dot(a, b, preferred_element_type=f32)


def _split3(x):
    hi = x.astype(bf16)
    r = x - hi.astype(f32)
    mid = r.astype(bf16)
    lo = (r - mid.astype(f32)).astype(bf16)
    return hi, mid, lo


def _dot01_rhs(x, e):
    hi, mid, lo = _split3(x)
    return _dot(hi, e) + _dot(mid, e) + _dot(lo, e)


def _dot01_lhs(t, x):
    hi, mid, lo = _split3(x)
    return _dot(t, hi) + _dot(t, mid) + _dot(t, lo)


def _softplus(x):
    return jnp.maximum(x, 0.0) + jnp.log1p(jnp.exp(-jnp.abs(x)))


def _silu(x):
    return x * jax.nn.sigmoid(x)


def _tri(q):
    row = lax.broadcasted_iota(jnp.int32, (q, q), 0)
    col = lax.broadcasted_iota(jnp.int32, (q, q), 1)
    return row >= col


def _valid_rows(q, width, lr, is_last):
    row = lax.broadcasted_iota(jnp.int32, (q, width), 0)
    return row < jnp.where(is_last, lr, q)


def _layer_norm(r, g, b):
    mu = jnp.mean(r, axis=-1, keepdims=True)
    var = jnp.mean(jnp.square(r - mu), axis=-1, keepdims=True)
    return (r - mu) * lax.rsqrt(var + EPS) * g + b


def _layer_spec(layer, *shape, **kw):
    zeros = (0,) * len(shape)
    return pl.BlockSpec((None,) + shape, lambda *_: (layer,) + zeros, **kw)


def _seq_spec(layer, gs, *shape):
    zeros = (0,) * len(shape)
    return pl.BlockSpec((None, gs) + shape, lambda b, c: (layer, b) + zeros)


def _stacked_call(kern, *, name, grid, inputs, in_specs, out_shape, out_specs, stacked, scratch_shapes,
                  dimension_semantics, vmem_limit_bytes=None):
    prev = [(i, a) for i, a in sorted(stacked.items()) if a is not None]
    n_in = len(inputs)

    def body(*refs):
        kern(*refs[:n_in], *refs[n_in + len(prev):])

    return pl.pallas_call(
        body,
        out_shape=out_shape,
        grid=grid,
        in_specs=list(in_specs) + [pl.BlockSpec(memory_space=pl.ANY)] * len(prev),
        out_specs=out_specs,
        scratch_shapes=scratch_shapes,
        input_output_aliases={n_in + k: i for k, (i, _) in enumerate(prev)},
        compiler_params=pltpu.CompilerParams(dimension_semantics=dimension_semantics,
                                             vmem_limit_bytes=vmem_limit_bytes),
        name=name,
    )(*inputs, *[a for _, a in prev])


def _proj_kernel(x_ref, w_ref, o_ref, xb):
    @pl.when(pl.program_id(1) == 0)
    def _():
        xb[...] = x_ref[...].astype(bf16)

    o_ref[...] = lax.dot_general(xb[...], w_ref[...], NT_DIMS, preferred_element_type=f32).astype(o_ref.dtype)


def _proj(x, wt, layer, out_dtype, tm):
    m, k = x.shape
    n = wt.shape[1]
    tn = PROJ_TN
    return pl.pallas_call(
        _proj_kernel,
        out_shape=jax.ShapeDtypeStruct((m, n), out_dtype),
        grid=(m // tm, n // tn),
        in_specs=[pl.BlockSpec((tm, k), lambda i, j: (i, 0)),
                  pl.BlockSpec((None, tn, k), lambda i, j: (layer, j, 0))],
        out_specs=pl.BlockSpec((tm, tn), lambda i, j: (i, j)),
        scratch_shapes=[pltpu.VMEM((tm, k), bf16)],
        compiler_params=pltpu.CompilerParams(dimension_semantics=("parallel", "arbitrary")),
        name="proj",
    )(x, wt)


def _per_sequence(seq_fn, refs, n_tile, n_state, n_param, gs, has_state, nc):
    n_state = n_state if has_state else 0
    tiles, refs = refs[:n_tile], refs[n_tile:]
    state, refs = refs[:n_state], refs[n_state:]
    params, rest = refs[:n_param], refs[n_param:]
    phases = []
    for g in range(gs):
        at = lambda group: tuple(r.at[g] for r in group)
        phases.append(seq_fn(*at(tiles), *at(state), *params, *at(rest)))
    c = pl.program_id(1)

    @pl.when(c == 0)
    def _():
        for init, _, _ in phases:
            init()

    for _, body, _ in phases:
        body()

    @pl.when(c == nc - 1)
    def _():
        for _, _, final in phases:
            final()


def _ssd_kernel(*refs, q, lr, nc, has_state, gs, cps):
    seq = functools.partial(_ssd_seq, q=q, lr=lr, nc=nc, has_state=has_state, cps=cps)
    _per_sequence(seq, refs, 4, 3, 10, gs, has_state, nc)


def _ssd_seq(*refs, q, lr, nc, has_state, cps):
    z_ref, xs_ref, bc_ref, sm_ref = refs[:4]
    refs = refs[4:]
    if has_state:
        csx_ref, csb_ref, h0_ref = refs[:3]
        refs = refs[3:]
    (cwx_ref, cbx_ref, cwb_ref, cbb_ref, dtb_ref, alog_ref, dexp_ref, nw_ref, e_ref, bd_ref,
     y_ref, ncsx_ref, ncsb_ref, hout_ref, xpx, xpb, ht, yb) = refs
    hdr = SUBLANES
    lo = hdr - (SSD_CONV - 1)
    n2 = SSD_GROUPS * SSD_STATE
    assert lr >= SSD_CONV - 1

    def init():
        if has_state:
            h_t = h0_ref[...].T
            ht[...] = jnp.where(bd_ref[...] > 0.5, jnp.concatenate([h_t, h_t], axis=0), 0.0)
            xpx[lo:hdr, :] = csx_ref[...]
            xpb[lo:hdr, :] = csb_ref[...]
        else:
            ht[...] = jnp.zeros_like(ht)
            xpx[lo:hdr, :] = jnp.zeros((SSD_CONV - 1, SSD_HP), f32)
            xpb[lo:hdr, :] = jnp.zeros((SSD_CONV - 1, SSD_BC), f32)

    def final():
        ncsx_ref[...] = xpx[lo + lr:hdr + lr, :]
        ncsb_ref[...] = xpb[lo + lr:hdr + lr, :]
        h_new = ht[...]
        hout_ref[...] = (h_new[:SSD_STATE, :] + h_new[SSD_STATE:, :]).T

    def body():
        for k in range(cps):
            sub = lambda r: r.at[pl.ds(k * q, q)]
            is_last = (pl.program_id(1) == nc - 1) if k == cps - 1 else False
            _ssd_body(sub(z_ref), sub(xs_ref), sub(bc_ref), sub(sm_ref), cwx_ref, cbx_ref, cwb_ref, cbb_ref,
                      dtb_ref, alog_ref, dexp_ref, nw_ref, e_ref, bd_ref, sub(y_ref), xpx, xpb, ht, yb,
                      q=q, lr=lr, is_last=is_last)

    return init, body, final


def _ssd_body(z_ref, xs_ref, bc_ref, sm_ref, cwx_ref, cbx_ref, cwb_ref, cbb_ref, dtb_ref, alog_ref,
              dexp_ref, nw_ref, e_ref, bd_ref, y_ref, xpx, xpb, ht, yb, *, q, lr, is_last):
    hdr = SUBLANES
    lo = hdr - (SSD_CONV - 1)
    n2 = SSD_GROUPS * SSD_STATE
    block_diag = bd_ref[...] > 0.5

    xpx[hdr:hdr + q, :] = xs_ref[...]
    xpb[hdr:hdr + q, :] = bc_ref[...]

    def conv(xp, w_ref, b_ref):
        w = w_ref[...]
        acc = xp[lo:lo + q, :] * w[0:1, :]
        for j in range(1, SSD_CONV):
            acc = acc + xp[lo + j:lo + j + q, :] * w[j:j + 1, :]
        return acc + b_ref[...]

    cx = conv(xpx, cwx_ref, cbx_ref)
    cb = conv(xpb, cwb_ref, cbb_ref)

    tail_x = xpx[lo + q:hdr + q, :]
    tail_b = xpb[lo + q:hdr + q, :]
    xpx[lo:hdr, :] = tail_x
    xpb[lo:hdr, :] = tail_b

    xs = _silu(cx)
    bcv = _silu(cb)
    bm = bcv[:, :n2].astype(bf16)
    cm = bcv[:, n2:]
    lane_g0 = lax.broadcasted_iota(jnp.int32, (q, n2), 1) < SSD_STATE

    dt = _softplus(sm_ref[...] + dtb_ref[...])
    if lr < q:
        dt = jnp.where(_valid_rows(q, SMALL, lr, is_last), dt, 0.0)
    a = -jnp.exp(alog_ref[...])
    d_a = dt * a
    causal = _tri(q)
    tril = jnp.where(causal, 1.0, 0.0).astype(bf16)
    acs = _dot01_lhs(tril, d_a)
    acs_t = acs.T
    e = e_ref[...]
    acs_x = _dot01_rhs(acs, e)
    dt_x = _dot01_rhs(dt, e)
    last_x = acs_x[q - 1:q, :]
    xdt = xs * dt_x
    xdt_b = xdt.astype(bf16)
    xdtw = (xdt * jnp.exp(last_x - acs_x)).astype(bf16)
    lane_lo = lax.broadcasted_iota(jnp.int32, (q, LANES), 1) < SSD_HEAD_DIM

    heads_per_group = SSD_HEADS // SSD_GROUPS
    for g in range(SSD_GROUPS):
        cg = jnp.where(lane_g0 if g == 0 else jnp.logical_not(lane_g0), cm, 0.0).astype(bf16)
        cbm = lax.dot_general(cg, bm, NT_DIMS, preferred_element_type=f32)
        for j in range(heads_per_group // 2):
            p = g * (heads_per_group // 2) + j
            xpair = xdt_b[:, p * LANES:(p + 1) * LANES]
            ys = []
            for hh in (2 * p, 2 * p + 1):
                seg = acs[:, hh:hh + 1] - acs_t[hh:hh + 1, :]
                decay = jnp.exp(jnp.where(causal, seg, -jnp.inf))
                ys.append(_dot((cbm * decay).astype(bf16), xpair))
            yb[:, p * LANES:(p + 1) * LANES] = jnp.where(lane_lo, ys[0], ys[1])

    h_prev = ht[...]
    y_off = _dot(cm.astype(bf16), h_prev.astype(bf16))
    upd = lax.dot_general(bm, xdtw, TN_DIMS, preferred_element_type=f32)
    ht[...] = jnp.exp(last_x) * h_prev + jnp.where(block_diag, upd, 0.0)

    y = yb[...] + y_off * jnp.exp(acs_x) + dexp_ref[...] * xs
    y = y * _silu(z_ref[...])
    y = y * lax.rsqrt(jnp.mean(jnp.square(y), axis=-1, keepdims=True) + EPS) * nw_ref[...]
    y_ref[...] = y.astype(y_ref.dtype)


def _mlstm_kernel(*refs, q, lr, nc, has_state, gs, cps):
    seq = functools.partial(_mlstm_seq, q=q, lr=lr, nc=nc, has_state=has_state, cps=cps)
    _per_sequence(seq, refs, 5, 3, 2, gs, has_state, nc)


def _mlstm_seq(*refs, q, lr, nc, has_state, cps):
    q_ref, k_ref, v_ref, o_ref, sm_ref = refs[:5]
    refs = refs[5:]
    if has_state:
        c0_ref, n0_ref, m0_ref = refs[:3]
        refs = refs[3:]
    gb_ref, nw_ref, h_ref, cout_ref, nout_ref, mout_ref, cs, ns, ms = refs

    def init():
        if has_state:
            cs[...] = c0_ref[...]
            ns[...] = n0_ref[...]
            ms[...] = m0_ref[...]
        else:
            cs[...] = jnp.zeros_like(cs)
            ns[...] = jnp.zeros_like(ns)
            ms[...] = jnp.zeros_like(ms)

    def final():
        cout_ref[...] = cs[...]
        nout_ref[...] = ns[...]
        mout_ref[...] = ms[...]

    def body():
        for k in range(cps):
            sub = lambda r: r.at[pl.ds(k * q, q)]
            is_last = (pl.program_id(1) == nc - 1) if k == cps - 1 else False
            _mlstm_body(sub(q_ref), sub(k_ref), sub(v_ref), sub(o_ref), sub(sm_ref), gb_ref, nw_ref,
                        sub(h_ref), cs, ns, ms, q=q, lr=lr, is_last=is_last)

    return init, body, final


def _mlstm_body(q_ref, k_ref, v_ref, o_ref, sm_ref, gb_ref, nw_ref, h_ref, cs, ns, ms, *, q, lr, is_last):
    sm = sm_ref[...] + gb_ref[...]
    logf = -_softplus(-sm)
    ipre = sm
    if lr < q:
        valid = _valid_rows(q, SMALL, lr, is_last)
        logf = jnp.where(valid, logf, 0.0)
        ipre = jnp.where(valid, ipre, NEG_BIG)
    causal = _tri(q)
    tril = jnp.where(causal, 1.0, 0.0).astype(bf16)
    bcum = _dot01_lhs(tril, logf)
    bcum_t = bcum.T
    ipre_t = ipre.T
    lane = lax.broadcasted_iota(jnp.int32, (1, SMALL), 1)
    k_scale = MLSTM_HEAD_DIM ** -0.5

    for h in range(MLSTM_HEADS):
        sl = slice(h * MLSTM_HEAD_DIM, (h + 1) * MLSTM_HEAD_DIM)
        qh = q_ref[:, sl]
        kh = k_ref[:, sl] * k_scale
        vh = v_ref[:, sl]
        b_col = bcum[:, F_OFF + h:F_OFF + h + 1]
        b_row = bcum_t[F_OFF + h:F_OFF + h + 1, :]
        i_col = ipre[:, I_OFF + h:I_OFF + h + 1]
        i_row = ipre_t[I_OFF + h:I_OFF + h + 1, :]
        m_prev = ms[:, h:h + 1]
        dmat = jnp.where(causal, b_col - b_row + i_row, -jnp.inf)
        inter = b_col + m_prev
        m_t = jnp.maximum(inter, jnp.max(dmat, axis=-1, keepdims=True))
        w_intra = jnp.exp(dmat - m_t)
        w_inter = jnp.exp(inter - m_t)
        s = lax.dot_general(qh, kh, NT_DIMS, preferred_element_type=f32) * w_intra
        c_h = cs[h]
        n_h = ns[h:h + 1, :]
        num = _dot(s.astype(bf16), vh) + w_inter * _dot(qh, c_h.astype(bf16))
        qn = jnp.sum(qh.astype(f32) * n_h, axis=-1, keepdims=True)
        den = jnp.sum(s, axis=-1, keepdims=True) + w_inter * qn
        hv = num / jnp.maximum(jnp.abs(den), jnp.exp(-m_t))
        b_last = b_col[q - 1:q, :]
        m_end = m_t[q - 1:q, :]
        w_c = jnp.exp(b_last + m_prev - m_end)
        w_s = jnp.exp(b_last - b_col + i_col - m_end)
        kw = kh.astype(f32) * w_s
        cs[h] = w_c * c_h + lax.dot_general(kw.astype(bf16), vh, TN_DIMS, preferred_element_type=f32)
        ns[h:h + 1, :] = w_c * n_h + jnp.sum(kw, axis=0, keepdims=True)
        ms[...] = jnp.where(lane == h, m_end, ms[...])
        hn = hv * lax.rsqrt(jnp.mean(jnp.square(hv), axis=-1, keepdims=True) + EPS) * nw_ref[:, sl]
        h_ref[:, sl] = (hn * jax.nn.sigmoid(o_ref[:, sl])).astype(h_ref.dtype)


def _merge_kernel(ys_ref, hm_ref, ga_ref, gb_ref, x_ref, wa_ref, wb_ref, wo_ref, g_ref, b_ref, o_ref):
    br_a = _dot(ys_ref[...], wa_ref[...])
    br_b = _dot(hm_ref[...], wb_ref[...])
    merged = jax.nn.sigmoid(ga_ref[...]) * br_a + jax.nn.sigmoid(gb_ref[...]) * br_b
    mix = _dot(merged.astype(bf16), wo_ref[...])
    o_ref[...] = _layer_norm(ALPHA * x_ref[...] + mix, g_ref[...], b_ref[...])


def _merge(ys, hm, p32, x, w, layer, tm):
    m = x.shape[0]
    row = lambda blk: pl.BlockSpec((tm, D_MODEL), lambda i: (i, blk))
    return pl.pallas_call(
        _merge_kernel,
        out_shape=jax.ShapeDtypeStruct((m, D_MODEL), f32),
        grid=(m // tm,),
        in_specs=[row(0), row(0), row(P32_GA), row(P32_GB), row(0),
                  _layer_spec(layer, D_MODEL, D_MODEL), _layer_spec(layer, D_MODEL, D_MODEL),
                  _layer_spec(layer, D_MODEL, D_MODEL), _layer_spec(layer, 1, D_MODEL),
                  _layer_spec(layer, 1, D_MODEL)],
        out_specs=row(0),
        compiler_params=pltpu.CompilerParams(dimension_semantics=("parallel",)),
        name="merge",
    )(ys, hm, p32, p32, x, w["wa"], w["wb"], w["wo"], w["ln1_g"], w["ln1_b"])


def _ffn_kernel(*refs, tm, seq_len):
    multi = seq_len > 0
    if multi:
        x_ref, st_ref, wup_ref, cw_ref, cb_ref, wdn_ref, g_ref, b_ref, o_ref, sout_ref, xp = refs
    else:
        x_ref, wup_ref, cw_ref, cb_ref, wdn_ref, g_ref, b_ref, o_ref, sout_ref, xp, carry = refs
        @pl.when(pl.program_id(1) == 0)
        def _():
            carry[...] = jnp.zeros_like(carry)

    hdr = SUBLANES
    x = x_ref[...]
    xb = x.astype(bf16)
    if multi:
        assert seq_len & (seq_len - 1) == 0
        nseq = tm // seq_len
        t = lax.broadcasted_iota(jnp.int32, (tm, FF_CH), 0) & (seq_len - 1)
        row = lax.broadcasted_iota(jnp.int32, (tm, 2 * nseq), 0)
        col = lax.broadcasted_iota(jnp.int32, (tm, 2 * nseq), 1)
        t_sel = row & (seq_len - 1)
        seq0 = lax.shift_right_logical(row - t_sel, (seq_len // 2).bit_length() - 1)
        sel_p2 = jnp.where(col == seq0 + t_sel, jnp.where(t_sel < 2, 1.0, 0.0), 0.0).astype(bf16)
        sel_p1 = jnp.where(col == seq0 + 1, jnp.where(t_sel == 0, 1.0, 0.0), 0.0).astype(bf16)
        xp[:, 0:hdr, :] = jnp.zeros((2, hdr, FF_CH), f32)

    acc = None
    for c in range(FF_NCH):
        halves = []
        for part in range(2):
            cols = slice(part * D_FF + c * FF_CH, part * D_FF + (c + 1) * FF_CH)
            u = _dot(xb, wup_ref[:, cols])
            slot = (2 * c + part) % 2
            xp[slot, hdr:hdr + tm, :] = u
            if multi:
                sout_ref[:, cols] = u
                st = st_ref[:, cols]
                p1 = jnp.where(t == 0, _dot01_lhs(sel_p1, st), xp[slot, hdr - 1:hdr - 1 + tm, :])
                p2 = jnp.where(t < 2, _dot01_lhs(sel_p2, st), xp[slot, hdr - 2:hdr - 2 + tm, :])
            else:
                xp[slot, 0:hdr, :] = carry[:, cols]
                p1 = xp[slot, hdr - 1:hdr - 1 + tm, :]
                p2 = xp[slot, hdr - 2:hdr - 2 + tm, :]
                carry[:, cols] = u[tm - hdr:tm, :]
            w = cw_ref[:, cols]
            halves.append(p2 * w[0:1, :] + p1 * w[1:2, :] + u * w[2:3, :] + cb_ref[:, cols])
        act = (_silu(halves[0]) * halves[1]).astype(bf16)
        d = _dot(act, wdn_ref[c * FF_CH:(c + 1) * FF_CH, :])
        acc = d if acc is None else acc + d

    if not multi:
        sout_ref[...] = carry[...]
    o_ref[...] = _layer_norm(ALPHA * x + acc, g_ref[...], b_ref[...])


def _ffn(x, st, w, layer, *, groups, tm, seq_len):
    m = x.shape[0]
    tiles = m // (groups * tm)
    multi = seq_len > 0
    kern = functools.partial(_ffn_kernel, tm=tm, seq_len=seq_len)
    once = dict(pipeline_mode=pl.Buffered(1))
    x_spec = pl.BlockSpec((tm, D_MODEL), lambda s, j: (s * tiles + j, 0))
    w_specs = [_layer_spec(layer, D_MODEL, 2 * D_FF, **once), _layer_spec(layer, FFN_CONV, 2 * D_FF, **once),
               _layer_spec(layer, 1, 2 * D_FF, **once), _layer_spec(layer, D_FF, D_MODEL, **once),
               _layer_spec(layer, 1, D_MODEL, **once), _layer_spec(layer, 1, D_MODEL, **once)]
    w_args = (w["wup"], w["fcw"], w["fcb"], w["wdn"], w["ln2_g"], w["ln2_b"])
    xp = pltpu.VMEM((2, SUBLANES + tm, FF_CH), f32)
    if multi:
        nst = 2 * (tm // seq_len)
        inputs = (x, st) + w_args
        in_specs = [x_spec, pl.BlockSpec((None, nst, 2 * D_FF), lambda s, j: (layer, s * tiles + j, 0))] + w_specs
        sout_shape = jax.ShapeDtypeStruct((m, 2 * D_FF), f32)
        sout_spec = pl.BlockSpec((tm, 2 * D_FF), lambda s, j: (s * tiles + j, 0))
        scratch = [xp]
    else:
        inputs = (x,) + w_args
        in_specs = [x_spec] + w_specs
        sout_shape = jax.ShapeDtypeStruct((groups, SUBLANES, 2 * D_FF), f32)
        sout_spec = pl.BlockSpec((None, SUBLANES, 2 * D_FF), lambda s, j: (s, 0, 0))
        scratch = [xp, pltpu.VMEM((SUBLANES, 2 * D_FF), f32)]
    return pl.pallas_call(
        kern,
        out_shape=(jax.ShapeDtypeStruct((m, D_MODEL), f32), sout_shape),
        grid=(groups, tiles),
        in_specs=in_specs,
        out_specs=(x_spec, sout_spec),
        scratch_shapes=scratch,
        compiler_params=pltpu.CompilerParams(dimension_semantics=("parallel", "arbitrary"),
                                             vmem_limit_bytes=56 * 1024 * 1024),
        name="ffn",
    )(*inputs)


def _pad_lanes(v, off, width=SMALL):
    out = jnp.zeros((v.shape[0], 1, width), f32)
    return out.at[:, 0, off:off + v.shape[1]].set(v.astype(f32))


def _prep_weights(w_in, ssd_conv_w, ssd_conv_b, ssd_dt_bias, ssd_a_log, ssd_d, ssd_norm_w, mlstm_gate_b,
                  mlstm_norm_w, w_branch_a, w_branch_b, w_out, ln1_g, ln1_b, ffn_w_up, ffn_conv_w,
                  ffn_conv_b, ffn_w_down, ln2_g, ln2_b):
    d = D_MODEL
    o_z, o_xbc, o_dt = 0, d, d + d + SSD_BC
    o_q = o_dt + SSD_HEADS
    o_if = o_q + 3 * d
    o_o = o_if + 2 * MLSTM_HEADS
    o_g = o_o + d
    w_t = jnp.swapaxes(w_in, 1, 2)
    cols = lambda a, n: w_t[:, a:a + n, :]
    zeros = lambda n: jnp.zeros((DEPTH, n, d), w_in.dtype)
    w32 = jnp.concatenate([cols(o_z, d), cols(o_o, d), cols(o_g, 2 * d), cols(o_xbc, d + SSD_BC),
                           cols(o_dt, SSD_HEADS), cols(o_if, 2 * MLSTM_HEADS),
                           zeros(P32_W - P32_SM_OFF - SSD_HEADS - 2 * MLSTM_HEADS)], axis=1).astype(bf16)
    e = (np.arange(SSD_HP)[None, :] // SSD_HEAD_DIM == np.arange(LANES)[:, None])
    bd = ((np.arange(SSD_GROUPS * SSD_STATE)[:, None] < SSD_STATE)
          == (np.arange(SSD_HP)[None, :] < SSD_HP // SSD_GROUPS))
    row = lambda a: a[:, None, :]
    return dict(
        w32=w32, wqkv=cols(o_q, 3 * d).astype(bf16),
        cwx=ssd_conv_w[:, :, :d], cbx=row(ssd_conv_b[:, :d]),
        cwb=ssd_conv_w[:, :, d:], cbb=row(ssd_conv_b[:, d:]),
        dtb=_pad_lanes(ssd_dt_bias, DT_OFF), alog=_pad_lanes(ssd_a_log, DT_OFF),
        dexp=row(jnp.repeat(ssd_d.astype(f32), SSD_HEAD_DIM, axis=1)), ssd_nw=row(ssd_norm_w),
        e=jnp.asarray(e, bf16), bd=jnp.asarray(bd, f32),
        gate_b=_pad_lanes(mlstm_gate_b, I_OFF), mlstm_nw=row(mlstm_norm_w),
        wa=w_branch_a.astype(bf16), wb=w_branch_b.astype(bf16), wo=w_out.astype(bf16),
        ln1_g=row(ln1_g), ln1_b=row(ln1_b),
        wup=ffn_w_up.astype(bf16), fcw=ffn_conv_w, fcb=row(ffn_conv_b), wdn=ffn_w_down.astype(bf16),
        ln2_g=row(ln2_g), ln2_b=row(ln2_b),
    )


class _Group:
    def __init__(self, batch, length, q, lr, gs, ssd_cps, mlstm_cps, proj_tm, merge_tm, ffn):
        self.batch, self.length, self.q, self.lr, self.gs = batch, length, q, lr, gs
        self.ssd_cps, self.mlstm_cps = ssd_cps, mlstm_cps
        self.rows = batch * length
        self.proj_tm, self.merge_tm, self.ffn = proj_tm, merge_tm, ffn

    def view(self, a):
        return a.reshape(self.batch, self.length, a.shape[-1])

    def tiling(self, cps):
        rows = cps * self.q
        gs = self.gs
        return self.length // rows, lambda width, blk: pl.BlockSpec((gs, rows, width), lambda b, c: (b, c, blk))


def _ssd(grp, p32, state, w, layer, prev):
    q, b, gs = grp.q, grp.batch, grp.gs
    has_state = state is not None
    steps, tile = grp.tiling(grp.ssd_cps)
    kern = functools.partial(_ssd_kernel, q=q, lr=grp.lr, nc=steps, has_state=has_state, gs=gs, cps=grp.ssd_cps)
    pv = grp.view(p32)
    inputs = [pv, pv, pv, pv]
    in_specs = [tile(D_MODEL, P32_Z), tile(D_MODEL, P32_XS),
                tile(SSD_BC, P32_BC_OFF // SSD_BC), tile(SMALL, P32_SM_OFF // SMALL)]
    if has_state:
        inputs += [state["csx"], state["csb"], state["h"]]
        in_specs += [_seq_spec(layer, gs, SSD_CONV - 1, D_MODEL), _seq_spec(layer, gs, SSD_CONV - 1, SSD_BC),
                     _seq_spec(layer, gs, SSD_HP, SSD_STATE)]
    inputs += [w["cwx"], w["cbx"], w["cwb"], w["cbb"], w["dtb"], w["alog"], w["dexp"], w["ssd_nw"], w["e"],
               w["bd"]]
    const = lambda *shape: pl.BlockSpec(shape, lambda b, c: (0,) * len(shape))
    in_specs += [_layer_spec(layer, SSD_CONV, D_MODEL), _layer_spec(layer, 1, D_MODEL),
                 _layer_spec(layer, SSD_CONV, SSD_BC), _layer_spec(layer, 1, SSD_BC),
                 _layer_spec(layer, 1, SMALL), _layer_spec(layer, 1, SMALL), _layer_spec(layer, 1, D_MODEL),
                 _layer_spec(layer, 1, D_MODEL), const(LANES, SSD_HP), const(SSD_GROUPS * SSD_STATE, SSD_HP)]
    return _stacked_call(
        kern, name="ssd", grid=(b // gs, steps), inputs=inputs, in_specs=in_specs,
        out_shape=(jax.ShapeDtypeStruct((b, grp.length, D_MODEL), bf16),
                   jax.ShapeDtypeStruct((DEPTH, b, SSD_CONV - 1, D_MODEL), f32),
                   jax.ShapeDtypeStruct((DEPTH, b, SSD_CONV - 1, SSD_BC), f32),
                   jax.ShapeDtypeStruct((DEPTH, b, SSD_HP, SSD_STATE), f32)),
        out_specs=(tile(D_MODEL, 0), _seq_spec(layer, gs, SSD_CONV - 1, D_MODEL),
                   _seq_spec(layer, gs, SSD_CONV - 1, SSD_BC), _seq_spec(layer, gs, SSD_HP, SSD_STATE)),
        stacked={1: prev and prev[0], 2: prev and prev[1], 3: prev and prev[2]},
        scratch_shapes=[pltpu.VMEM((gs, SUBLANES + q, D_MODEL), f32), pltpu.VMEM((gs, SUBLANES + q, SSD_BC), f32),
                        pltpu.VMEM((gs, SSD_GROUPS * SSD_STATE, SSD_HP), f32), pltpu.VMEM((gs, q, D_MODEL), f32)],
        dimension_semantics=("parallel", "arbitrary"))


def _mlstm(grp, qkv, p32, state, w, layer, prev):
    q, b, gs = grp.q, grp.batch, grp.gs
    has_state = state is not None
    steps, tile = grp.tiling(grp.mlstm_cps)
    kern = functools.partial(_mlstm_kernel, q=q, lr=grp.lr, nc=steps, has_state=has_state, gs=gs,
                             cps=grp.mlstm_cps)
    hd = MLSTM_HEAD_DIM
    qv, pv = grp.view(qkv), grp.view(p32)
    inputs = [qv, qv, qv, pv, pv]
    in_specs = [tile(D_MODEL, 0), tile(D_MODEL, 1), tile(D_MODEL, 2), tile(D_MODEL, P32_O),
                tile(SMALL, P32_SM_OFF // SMALL)]
    if has_state:
        inputs += [state["c"], state["n"], state["m"]]
        in_specs += [_seq_spec(layer, gs, MLSTM_HEADS, hd, hd), _seq_spec(layer, gs, MLSTM_HEADS, hd),
                     _seq_spec(layer, gs, 1, SMALL)]
    inputs += [w["gate_b"], w["mlstm_nw"]]
    in_specs += [_layer_spec(layer, 1, SMALL), _layer_spec(layer, 1, D_MODEL)]
    return _stacked_call(
        kern, name="mlstm", grid=(b // gs, steps), inputs=inputs, in_specs=in_specs,
        out_shape=(jax.ShapeDtypeStruct((b, grp.length, D_MODEL), bf16),
                   jax.ShapeDtypeStruct((DEPTH, b, MLSTM_HEADS, hd, hd), f32),
                   jax.ShapeDtypeStruct((DEPTH, b, MLSTM_HEADS, hd), f32),
                   jax.ShapeDtypeStruct((DEPTH, b, 1, SMALL), f32)),
        out_specs=(tile(D_MODEL, 0), _seq_spec(layer, gs, MLSTM_HEADS, hd, hd),
                   _seq_spec(layer, gs, MLSTM_HEADS, hd), _seq_spec(layer, gs, 1, SMALL)),
        stacked={1: prev and prev[0], 2: prev and prev[1], 3: prev and prev[2]},
        scratch_shapes=[pltpu.VMEM((gs, MLSTM_HEADS, hd, hd), f32), pltpu.VMEM((gs, MLSTM_HEADS, hd), f32),
                        pltpu.VMEM((gs, 1, SMALL), f32)],
        dimension_semantics=("parallel", "arbitrary"))


def _trunk(grp, x, state, w):
    ssd_out = mlstm_out = None
    ffn_out = []
    for layer in range(DEPTH):
        p32 = _proj(x, w["w32"], layer, f32, grp.proj_tm)
        qkv = _proj(x, w["wqkv"], layer, bf16, grp.proj_tm)
        ys, *ssd_out = _ssd(grp, p32, state, w, layer, ssd_out)
        hm, *mlstm_out = _mlstm(grp, qkv, p32, state, w, layer, mlstm_out)
        x1 = _merge(ys.reshape(grp.rows, D_MODEL), hm.reshape(grp.rows, D_MODEL), p32, x, w, layer,
                    grp.merge_tm)
        x, s_ffn = _ffn(x1, state["ffn"] if state is not None else None, w, layer, **grp.ffn)
        ffn_out.append(s_ffn)
    return x, ssd_out, mlstm_out, ffn_out


def _unpack_states(batch, ssd_out, mlstm_out):
    csx, csb, h = ssd_out
    c, n, m = mlstm_out
    return (h.reshape(DEPTH, batch, SSD_HEADS, SSD_HEAD_DIM, SSD_STATE),
            jnp.concatenate([csx, csb], axis=-1), c, n, m[:, :, 0, :MLSTM_HEADS])


def kernel(x_prompt, x_sample, state_ssd, state_ssd_conv, state_mlstm_c, state_mlstm_n, state_mlstm_m,
           state_ffn_conv, w_in, ssd_conv_w, ssd_conv_b, ssd_dt_bias, ssd_a_log, ssd_d, ssd_norm_w,
           mlstm_gate_b, mlstm_norm_w, w_branch_a, w_branch_b, w_out, ln1_g, ln1_b, ffn_w_up, ffn_conv_w,
           ffn_conv_b, ffn_w_down, ln2_g, ln2_b):
    w = _prep_weights(w_in, ssd_conv_w, ssd_conv_b, ssd_dt_bias, ssd_a_log, ssd_d, ssd_norm_w, mlstm_gate_b,
                      mlstm_norm_w, w_branch_a, w_branch_b, w_out, ln1_g, ln1_b, ffn_w_up, ffn_conv_w,
                      ffn_conv_b, ffn_w_down, ln2_g, ln2_b)
    keep = FFN_CONV - 1

    bp, lp, _ = x_prompt.shape
    prompt = _Group(bp, lp, CHUNK, CHUNK, gs=1, ssd_cps=4, mlstm_cps=1, proj_tm=2048, merge_tm=512,
                    ffn=dict(groups=bp, tm=512, seq_len=0))
    y_p, ssd_p, mlstm_p, ffn_p = _trunk(prompt, x_prompt.reshape(bp * lp, D_MODEL), None, w)
    st_p = _unpack_states(bp, ssd_p, mlstm_p)
    ffn_conv_p = jnp.stack(ffn_p)[:, :, SUBLANES - keep:, :]

    bs, ls, _ = x_sample.shape
    lpad = SAMPLE_PAD_LEN
    s_rows = bs * lpad
    sample = _Group(bs, lpad, lpad, ls, gs=4, ssd_cps=1, mlstm_cps=1, proj_tm=s_rows, merge_tm=512,
                    ffn=dict(groups=1, tm=256, seq_len=lpad))
    s_state = dict(
        csx=state_ssd_conv[..., :D_MODEL], csb=state_ssd_conv[..., D_MODEL:],
        h=state_ssd.reshape(DEPTH, bs, SSD_HP, SSD_STATE),
        c=state_mlstm_c, n=state_mlstm_n,
        m=jnp.pad(state_mlstm_m, ((0, 0), (0, 0), (0, SMALL - MLSTM_HEADS)))[:, :, None, :],
        ffn=state_ffn_conv.reshape(DEPTH, bs * keep, 2 * D_FF),
    )
    xs = jnp.pad(x_sample, ((0, 0), (0, lpad - ls), (0, 0))).reshape(s_rows, D_MODEL)
    y_s, ssd_s, mlstm_s, ffn_s = _trunk(sample, xs, s_state, w)
    st_s = _unpack_states(bs, ssd_s, mlstm_s)
    ffn_conv_s = jnp.stack([u.reshape(bs, lpad, 2 * D_FF)[:, ls - keep:ls, :] for u in ffn_s])
    y_sample = y_s.reshape(bs, lpad, D_MODEL)[:, :ls, :]

    return (y_p.reshape(bp, lp, D_MODEL), y_sample, st_p[0], st_s[0], st_p[1], st_s[1], st_p[2], st_s[2],
            st_p[3], st_s[3], st_p[4], st_s[4], ffn_conv_p, ffn_conv_s)
```

```python
import functools
import itertools

import jax
import jax.numpy as jnp
import numpy as np
from jax import lax
from jax.experimental import pallas as pl
from jax.experimental.pallas import tpu as pltpu

f32 = jnp.float32
bf16 = jnp.bfloat16

D_MODEL = 1024
DEPTH = 2
SSD_HEADS = 16
SSD_HEAD_DIM = 64
SSD_STATE = 64
SSD_GROUPS = 2
SSD_CONV = 4
SSD_BC = 2 * SSD_GROUPS * SSD_STATE
SSD_HP = SSD_HEADS * SSD_HEAD_DIM
MLSTM_HEADS = 4
MLSTM_HEAD_DIM = 256
CHUNK = 128
D_FF = 2816
FFN_CONV = 3
ALPHA = (2 * DEPTH) ** 0.25
EPS = 1e-5

LANES = 128
SUBLANES = 8
SMALL = LANES
DT_OFF, I_OFF, F_OFF = 0, 16, 20
P32_Z, P32_O, P32_GA, P32_GB, P32_XS = 0, 1, 2, 3, 4
P32_BC_OFF = 5 * D_MODEL
P32_SM_OFF = P32_BC_OFF + SSD_BC
P32_W = P32_SM_OFF + 2 * SMALL
PROJ_TN = 512
FF_CH = 256
FF_NCH = D_FF // FF_CH
MERGE_PARTS = 2
FF_UP_AHEAD = 2
FF_XP_SLOTS = 4
NEG_BIG = -1e30
SAMPLE_PAD_LEN = SUBLANES

NT_DIMS = (((1,), (1,)), ((), ()))
TN_DIMS = (((0,), (0,)), ((), ()))


def _dot(a, b):
    return jnp.dot(a, b, preferred_element_type=f32)


def _split3(x):
    hi = x.astype(bf16)
    r = x - hi.astype(f32)
    mid = r.astype(bf16)
    lo = (r - mid.astype(f32)).astype(bf16)
    return hi, mid, lo


def _dot01_rhs(x, e):
    hi, mid, lo = _split3(x)
    return _dot(hi, e) + _dot(mid, e) + _dot(lo, e)


def _dot01_lhs(t, x):
    hi, mid, lo = _split3(x)
    return _dot(t, hi) + _dot(t, mid) + _dot(t, lo)


def _softplus(x):
    return jnp.maximum(x, 0.0) + jnp.log1p(jnp.exp(-jnp.abs(x)))


def _silu(x):
    return x * jax.nn.sigmoid(x)


def _tri(q):
    row = lax.broadcasted_iota(jnp.int32, (q, q), 0)
    col = lax.broadcasted_iota(jnp.int32, (q, q), 1)
    return row >= col


def _valid_rows(q, width, lr, is_last):
    row = lax.broadcasted_iota(jnp.int32, (q, width), 0)
    return row < jnp.where(is_last, lr, q)


def _layer_norm(r, g, b):
    mu = jnp.mean(r, axis=-1, keepdims=True)
    var = jnp.mean(jnp.square(r - mu), axis=-1, keepdims=True)
    return (r - mu) * lax.rsqrt(var + EPS) * g + b


def _layer_spec(layer, *shape, **kw):
    zeros = (0,) * len(shape)
    return pl.BlockSpec((None,) + shape, lambda *_: (layer,) + zeros, **kw)


def _seq_spec(layer, gs, *shape):
    zeros = (0,) * len(shape)
    return pl.BlockSpec((None, gs) + shape, lambda b, c: (layer, b) + zeros)


def _stacked_call(kern, *, name, grid, inputs, in_specs, out_shape, out_specs, stacked, scratch_shapes,
                  dimension_semantics, vmem_limit_bytes=None):
    prev = [(i, a) for i, a in sorted(stacked.items()) if a is not None]
    n_in = len(inputs)

    def body(*refs):
        kern(*refs[:n_in], *refs[n_in + len(prev):])

    return pl.pallas_call(
        body,
        out_shape=out_shape,
        grid=grid,
        in_specs=list(in_specs) + [pl.BlockSpec(memory_space=pl.ANY)] * len(prev),
        out_specs=out_specs,
        scratch_shapes=scratch_shapes,
        input_output_aliases={n_in + k: i for k, (i, _) in enumerate(prev)},
        compiler_params=pltpu.CompilerParams(dimension_semantics=dimension_semantics,
                                             vmem_limit_bytes=vmem_limit_bytes),
        name=name,
    )(*inputs, *[a for _, a in prev])


def _proj_kernel(x_ref, w_ref, o_ref, xb):
    @pl.when(pl.program_id(1) == 0)
    def _():
        xb[...] = x_ref[...].astype(bf16)

    o_ref[...] = lax.dot_general(xb[...], w_ref[...], NT_DIMS, preferred_element_type=f32).astype(o_ref.dtype)


def _proj(x, wt, layer, out_dtype, tm):
    m, k = x.shape
    n = wt.shape[1]
    tn = PROJ_TN
    return pl.pallas_call(
        _proj_kernel,
        out_shape=jax.ShapeDtypeStruct((m, n), out_dtype),
        grid=(m // tm, n // tn),
        in_specs=[pl.BlockSpec((tm, k), lambda i, j: (i, 0)),
                  pl.BlockSpec((None, tn, k), lambda i, j: (layer, j, 0))],
        out_specs=pl.BlockSpec((tm, tn), lambda i, j: (i, j)),
        scratch_shapes=[pltpu.VMEM((tm, k), bf16)],
        compiler_params=pltpu.CompilerParams(dimension_semantics=("parallel", "arbitrary")),
        name="proj",
    )(x, wt)


def _per_sequence(seq_fn, refs, n_tile, n_state, n_param, gs, has_state, nc):
    n_state = n_state if has_state else 0
    tiles, refs = refs[:n_tile], refs[n_tile:]
    state, refs = refs[:n_state], refs[n_state:]
    params, rest = refs[:n_param], refs[n_param:]
    phases = []
    for g in range(gs):
        at = lambda group: tuple(r.at[g] for r in group)
        phases.append(seq_fn(*at(tiles), *at(state), *params, *at(rest)))
    c = pl.program_id(1)

    @pl.when(c == 0)
    def _():
        for init, _, _ in phases:
            init()

    for _ in itertools.zip_longest(*[body() for _, body, _ in phases]):
        pass

    @pl.when(c == nc - 1)
    def _():
        for _, _, final in phases:
            final()


def _ssd_kernel(*refs, q, lr, nc, has_state, gs, cps):
    seq = functools.partial(_ssd_seq, q=q, lr=lr, nc=nc, has_state=has_state, cps=cps)
    _per_sequence(seq, refs, 4, 3, 10, gs, has_state, nc)


def _ssd_seq(*refs, q, lr, nc, has_state, cps):
    z_ref, xs_ref, bc_ref, sm_ref = refs[:4]
    refs = refs[4:]
    if has_state:
        csx_ref, csb_ref, h0_ref = refs[:3]
        refs = refs[3:]
    (cwx_ref, cbx_ref, cwb_ref, cbb_ref, dtb_ref, alog_ref, dexp_ref, nw_ref, e_ref, bd_ref,
     y_ref, ncsx_ref, ncsb_ref, hout_ref, xpx, xpb, ht, yb) = refs
    hdr = SUBLANES
    lo = hdr - (SSD_CONV - 1)
    n2 = SSD_GROUPS * SSD_STATE
    assert lr >= SSD_CONV - 1

    def init():
        if has_state:
            h_t = h0_ref[...].T
            ht[...] = jnp.where(bd_ref[...] > 0.5, jnp.concatenate([h_t, h_t], axis=0), 0.0)
            xpx[lo:hdr, :] = csx_ref[...]
            xpb[lo:hdr, :] = csb_ref[...]
        else:
            ht[...] = jnp.zeros_like(ht)
            xpx[lo:hdr, :] = jnp.zeros((SSD_CONV - 1, SSD_HP), f32)
            xpb[lo:hdr, :] = jnp.zeros((SSD_CONV - 1, SSD_BC), f32)

    def final():
        ncsx_ref[...] = xpx[lo + lr:hdr + lr, :]
        ncsb_ref[...] = xpb[lo + lr:hdr + lr, :]
        h_new = ht[...]
        hout_ref[...] = (h_new[:SSD_STATE, :] + h_new[SSD_STATE:, :]).T

    def body():
        for k in range(cps):
            sub = lambda r: r.at[pl.ds(k * q, q)]
            is_last = (pl.program_id(1) == nc - 1) if k == cps - 1 else False
            yield from _ssd_body(sub(z_ref), sub(xs_ref), sub(bc_ref), sub(sm_ref), cwx_ref, cbx_ref, cwb_ref,
                                 cbb_ref, dtb_ref, alog_ref, dexp_ref, nw_ref, e_ref, bd_ref, sub(y_ref),
                                 xpx, xpb, ht, yb, q=q, lr=lr, is_last=is_last)

    return init, body, final


def _ssd_body(z_ref, xs_ref, bc_ref, sm_ref, cwx_ref, cbx_ref, cwb_ref, cbb_ref, dtb_ref, alog_ref,
              dexp_ref, nw_ref, e_ref, bd_ref, y_ref, xpx, xpb, ht, yb, *, q, lr, is_last):
    hdr = SUBLANES
    lo = hdr - (SSD_CONV - 1)
    n2 = SSD_GROUPS * SSD_STATE
    block_diag = bd_ref[...] > 0.5

    dt = _softplus(sm_ref[...] + dtb_ref[...])
    if lr < q:
        dt = jnp.where(_valid_rows(q, SMALL, lr, is_last), dt, 0.0)
    a = -jnp.exp(alog_ref[...])
    d_a = dt * a
    causal = _tri(q)
    tril = jnp.where(causal, 1.0, 0.0).astype(bf16)
    e = e_ref[...]
    acs = _dot01_lhs(tril, d_a)
    dt_x = _dot01_rhs(dt, e)
    yield

    xpx[hdr:hdr + q, :] = xs_ref[...]
    xpb[hdr:hdr + q, :] = bc_ref[...]

    def conv(xp, w_ref, b_ref):
        w = w_ref[...]
        acc = xp[lo:lo + q, :] * w[0:1, :]
        for j in range(1, SSD_CONV):
            acc = acc + xp[lo + j:lo + j + q, :] * w[j:j + 1, :]
        return acc + b_ref[...]

    cb = conv(xpb, cwb_ref, cbb_ref)
    bcv = _silu(cb)
    bm = bcv[:, :n2].astype(bf16)
    cm = bcv[:, n2:]
    lane_g0 = lax.broadcasted_iota(jnp.int32, (q, n2), 1) < SSD_STATE
    acs_t = acs.T
    acs_x = _dot01_rhs(acs, e)
    yield
    cbms = [lax.dot_general(jnp.where(lane_g0 if g == 0 else jnp.logical_not(lane_g0), cm, 0.0).astype(bf16),
                            bm, NT_DIMS, preferred_element_type=f32) for g in range(SSD_GROUPS)]
    cx = conv(xpx, cwx_ref, cbx_ref)
    tail_x = xpx[lo + q:hdr + q, :]
    tail_b = xpb[lo + q:hdr + q, :]
    xpx[lo:hdr, :] = tail_x
    xpb[lo:hdr, :] = tail_b
    yield
    xs = _silu(cx)
    last_x = acs_x[q - 1:q, :]
    xdt = xs * dt_x
    xdt_b = xdt.astype(bf16)
    yield
    lane_lo = lax.broadcasted_iota(jnp.int32, (q, LANES), 1) < SSD_HEAD_DIM
    heads_per_group = SSD_HEADS // SSD_GROUPS
    decays = [jnp.exp(jnp.where(causal, acs[:, hh:hh + 1] - acs_t[hh:hh + 1, :], -jnp.inf))
              for hh in range(SSD_HEADS)]
    yield
    weights = [(cbms[hh // heads_per_group] * decays[hh]).astype(bf16) for hh in range(SSD_HEADS)]
    xdtw = (xdt * jnp.exp(last_x - acs_x)).astype(bf16)
    yield
    ys = [_dot(weights[hh], xdt_b[:, (hh // 2) * LANES:(hh // 2 + 1) * LANES]) for hh in range(SSD_HEADS)]
    upd = lax.dot_general(bm, xdtw, TN_DIMS, preferred_element_type=f32)
    h_prev = ht[...]
    y_off = _dot(cm.astype(bf16), h_prev.astype(bf16))
    yield
    for p in range(SSD_HEADS // 2):
        yb[:, p * LANES:(p + 1) * LANES] = jnp.where(lane_lo, ys[2 * p], ys[2 * p + 1])
    ht[...] = jnp.exp(last_x) * h_prev + jnp.where(block_diag, upd, 0.0)
    yield
    y = yb[...] + y_off * jnp.exp(acs_x) + dexp_ref[...] * xs
    y = y * _silu(z_ref[...])
    yield
    y = y * lax.rsqrt(jnp.mean(jnp.square(y), axis=-1, keepdims=True) + EPS) * nw_ref[...]
    y_ref[...] = y.astype(y_ref.dtype)


def _mlstm_kernel(*refs, q, lr, nc, has_state, gs, cps):
    seq = functools.partial(_mlstm_seq, q=q, lr=lr, nc=nc, has_state=has_state, cps=cps)
    _per_sequence(seq, refs, 5, 3, 2, gs, has_state, nc)


def _mlstm_seq(*refs, q, lr, nc, has_state, cps):
    q_ref, k_ref, v_ref, o_ref, sm_ref = refs[:5]
    refs = refs[5:]
    if has_state:
        c0_ref, n0_ref, m0_ref = refs[:3]
        refs = refs[3:]
    gb_ref, nw_ref, h_ref, cout_ref, nout_ref, mout_ref, cs, ns, ms = refs
    direct = has_state and nc == 1 and cps == 1

    def init():
        if direct:
            return
        if has_state:
            cs[...] = c0_ref[...]
            ns[...] = n0_ref[...]
            ms[...] = m0_ref[...]
        else:
            cs[...] = jnp.zeros_like(cs)
            ns[...] = jnp.zeros_like(ns)
            ms[...] = jnp.zeros_like(ms)

    def final():
        if direct:
            return
        cout_ref[...] = cs[...]
        nout_ref[...] = ns[...]
        mout_ref[...] = ms[...]

    def body():
        for k in range(cps):
            sub = lambda r: r.at[pl.ds(k * q, q)]
            is_last = (pl.program_id(1) == nc - 1) if k == cps - 1 else False
            src = (c0_ref, n0_ref, m0_ref) if direct else (cs, ns, ms)
            dst = (cout_ref, nout_ref, mout_ref) if direct else (cs, ns, ms)
            yield from _mlstm_body(sub(q_ref), sub(k_ref), sub(v_ref), sub(o_ref), sub(sm_ref), gb_ref, nw_ref,
                                   sub(h_ref), src, dst, q=q, lr=lr, is_last=is_last)

    return init, body, final


def _mlstm_body(q_ref, k_ref, v_ref, o_ref, sm_ref, gb_ref, nw_ref, h_ref, src, dst, *, q, lr, is_last):
    c_src, n_src, m_src = src
    c_dst, n_dst, m_dst = dst
    n_all = n_src[...]
    m_all = m_src[...]
    m_new = m_all
    sm = sm_ref[...] + gb_ref[...]
    logf = -_softplus(-sm)
    ipre = sm
    if lr < q:
        valid = _valid_rows(q, SMALL, lr, is_last)
        logf = jnp.where(valid, logf, 0.0)
        ipre = jnp.where(valid, ipre, NEG_BIG)
    causal = _tri(q)
    tril = jnp.where(causal, 1.0, 0.0).astype(bf16)
    yield
    bcum = _dot01_lhs(tril, logf)
    ipre_t = ipre.T
    yield
    bcum_t = bcum.T
    lane = lax.broadcasted_iota(jnp.int32, (1, SMALL), 1)
    k_scale = MLSTM_HEAD_DIM ** -0.5

    heads = range(MLSTM_HEADS)
    sls = [slice(h * MLSTM_HEAD_DIM, (h + 1) * MLSTM_HEAD_DIM) for h in heads]
    q_all, k_all, v_all, o_all = q_ref[...], k_ref[...], v_ref[...], o_ref[...]
    qs = [q_all[:, sl] for sl in sls]
    ks = [k_all[:, sl] * k_scale for sl in sls]
    vs = [v_all[:, sl] for sl in sls]
    cs_in = [c_src[h] for h in heads]
    b_cols = [bcum[:, F_OFF + h:F_OFF + h + 1] for h in heads]
    i_cols = [ipre[:, I_OFF + h:I_OFF + h + 1] for h in heads]
    m_prevs = [m_all[:, h:h + 1] for h in heads]
    dmats = [jnp.where(causal, b_cols[h] - bcum_t[F_OFF + h:F_OFF + h + 1, :] + ipre_t[I_OFF + h:I_OFF + h + 1, :],
                       -jnp.inf) for h in heads]
    yield
    qk = [lax.dot_general(qs[h], ks[h], NT_DIMS, preferred_element_type=f32) for h in heads]
    qc = [_dot(qs[h], cs_in[h].astype(bf16)) for h in heads]
    yield
    inters = [b_cols[h] + m_prevs[h] for h in heads]
    m_ts = [jnp.maximum(inters[h], jnp.max(dmats[h], axis=-1, keepdims=True)) for h in heads]
    yield
    w_inters = [jnp.exp(inters[h] - m_ts[h]) for h in heads]
    ss = [qk[h] * jnp.exp(dmats[h] - m_ts[h]) for h in heads]
    yield
    sv = [_dot(ss[h].astype(bf16), vs[h]) for h in heads]
    m_ends = [m_ts[h][q - 1:q, :] for h in heads]
    b_lasts = [b_cols[h][q - 1:q, :] for h in heads]
    kws = [ks[h].astype(f32) * jnp.exp(b_lasts[h] - b_cols[h] + i_cols[h] - m_ends[h]) for h in heads]
    yield
    kv = [lax.dot_general(kws[h].astype(bf16), vs[h], TN_DIMS, preferred_element_type=f32) for h in heads]
    qns = [jnp.sum(qs[h].astype(f32) * n_all[h:h + 1, :], axis=-1, keepdims=True) for h in heads]
    yield
    dens = [jnp.sum(ss[h], axis=-1, keepdims=True) + w_inters[h] * qns[h] for h in heads]
    yield
    hvs = [(sv[h] + w_inters[h] * qc[h]) / jnp.maximum(jnp.abs(dens[h]), jnp.exp(-m_ts[h])) for h in heads]
    yield
    rms = [lax.rsqrt(jnp.mean(jnp.square(hvs[h]), axis=-1, keepdims=True) + EPS) for h in heads]
    yield
    h_new = [(hvs[h] * rms[h] * nw_ref[:, sls[h]] * jax.nn.sigmoid(o_all[:, sls[h]])).astype(h_ref.dtype)
             for h in heads]
    w_cs = [jnp.exp(b_lasts[h] + m_prevs[h] - m_ends[h]) for h in heads]
    yield
    for h in heads:
        c_dst[h] = w_cs[h] * cs_in[h] + kv[h]
        m_new = jnp.where(lane == h, m_ends[h], m_new)
    h_ref[...] = jnp.concatenate(h_new, axis=1)
    n_dst[...] = jnp.concatenate(
        [w_cs[h] * n_all[h:h + 1, :] + jnp.sum(kws[h], axis=0, keepdims=True) for h in heads], axis=0)
    m_dst[...] = m_new


def _merge_kernel(ys_ref, hm_ref, ga_ref, gb_ref, x_ref, wa_ref, wb_ref, wo_ref, g_ref, b_ref, o_ref):
    tm = x_ref.shape[0]
    rows = [pl.ds(i * (tm // MERGE_PARTS), tm // MERGE_PARTS) for i in range(MERGE_PARTS)]
    br = [(_dot(ys_ref[r, :], wa_ref[...]), _dot(hm_ref[r, :], wb_ref[...])) for r in rows]
    merged = [(jax.nn.sigmoid(ga_ref[r, :]) * a + jax.nn.sigmoid(gb_ref[r, :]) * b).astype(bf16)
              for r, (a, b) in zip(rows, br)]
    mix = [_dot(m, wo_ref[...]) for m in merged]
    for r, m in zip(rows, mix):
        o_ref[r, :] = _layer_norm(ALPHA * x_ref[r, :] + m, g_ref[...], b_ref[...])


def _merge(ys, hm, p32, x, w, layer, tm):
    m = x.shape[0]
    row = lambda blk: pl.BlockSpec((tm, D_MODEL), lambda i: (i, blk))
    return pl.pallas_call(
        _merge_kernel,
        out_shape=jax.ShapeDtypeStruct((m, D_MODEL), f32),
        grid=(m // tm,),
        in_specs=[row(0), row(0), row(P32_GA), row(P32_GB), row(0),
                  _layer_spec(layer, D_MODEL, D_MODEL), _layer_spec(layer, D_MODEL, D_MODEL),
                  _layer_spec(layer, D_MODEL, D_MODEL), _layer_spec(layer, 1, D_MODEL),
                  _layer_spec(layer, 1, D_MODEL)],
        out_specs=row(0),
        compiler_params=pltpu.CompilerParams(dimension_semantics=("parallel",)),
        name="merge",
    )(ys, hm, p32, p32, x, w["wa"], w["wb"], w["wo"], w["ln1_g"], w["ln1_b"])


def _ffn_kernel(*refs, tm, seq_len):
    multi = seq_len > 0
    if multi:
        x_ref, st_ref, wup_ref, cw_ref, cb_ref, wdn_ref, g_ref, b_ref, o_ref, sout_ref, xp = refs
    else:
        x_ref, wup_ref, cw_ref, cb_ref, wdn_ref, g_ref, b_ref, o_ref, sout_ref, xp, carry = refs
        @pl.when(pl.program_id(1) == 0)
        def _():
            carry[...] = jnp.zeros_like(carry)

    hdr = SUBLANES
    x = x_ref[...]
    xb = x.astype(bf16)
    if multi:
        assert seq_len & (seq_len - 1) == 0
        nseq = tm // seq_len
        t = lax.broadcasted_iota(jnp.int32, (tm, FF_CH), 0) & (seq_len - 1)
        row = lax.broadcasted_iota(jnp.int32, (tm, 2 * nseq), 0)
        col = lax.broadcasted_iota(jnp.int32, (tm, 2 * nseq), 1)
        t_sel = row & (seq_len - 1)
        seq0 = lax.shift_right_logical(row - t_sel, (seq_len // 2).bit_length() - 1)
        sel_p2 = jnp.where(col == seq0 + t_sel, jnp.where(t_sel < 2, 1.0, 0.0), 0.0).astype(bf16)
        sel_p1 = jnp.where(col == seq0 + 1, jnp.where(t_sel == 0, 1.0, 0.0), 0.0).astype(bf16)
        xp[:, 0:hdr, :] = jnp.zeros((FF_XP_SLOTS, hdr, FF_CH), f32)

    def cols_of(c, part):
        return slice(part * D_FF + c * FF_CH, part * D_FF + (c + 1) * FF_CH)

    def up(c):
        return [_dot(xb, wup_ref[:, cols_of(c, part)]) for part in range(2)]

    def conv_act(c, us):
        halves = []
        for part, u in enumerate(us):
            cols = cols_of(c, part)
            slot = (2 * c + part) % FF_XP_SLOTS
            xp[slot, hdr:hdr + tm, :] = u
            if multi:
                sout_ref[:, cols] = u
                st = st_ref[:, cols]
                p1 = jnp.where(t == 0, _dot01_lhs(sel_p1, st), xp[slot, hdr - 1:hdr - 1 + tm, :])
                p2 = jnp.where(t < 2, _dot01_lhs(sel_p2, st), xp[slot, hdr - 2:hdr - 2 + tm, :])
            else:
                xp[slot, 0:hdr, :] = carry[:, cols]
                p1 = xp[slot, hdr - 1:hdr - 1 + tm, :]
                p2 = xp[slot, hdr - 2:hdr - 2 + tm, :]
                carry[:, cols] = u[tm - hdr:tm, :]
            w = cw_ref[:, cols]
            halves.append(p2 * w[0:1, :] + p1 * w[1:2, :] + u * w[2:3, :] + cb_ref[:, cols])
        return (_silu(halves[0]) * halves[1]).astype(bf16)

    acc = None
    ahead = [up(c) for c in range(min(FF_UP_AHEAD, FF_NCH))]
    pending = None
    for c in range(FF_NCH):
        if c + FF_UP_AHEAD < FF_NCH:
            ahead.append(up(c + FF_UP_AHEAD))
        if pending is not None:
            d = _dot(pending, wdn_ref[(c - 1) * FF_CH:c * FF_CH, :])
            acc = d if acc is None else acc + d
        pending = conv_act(c, ahead.pop(0))
    acc = acc + _dot(pending, wdn_ref[(FF_NCH - 1) * FF_CH:FF_NCH * FF_CH, :])

    if not multi:
        sout_ref[...] = carry[...]
    o_ref[...] = _layer_norm(ALPHA * x + acc, g_ref[...], b_ref[...])


def _ffn(x, st, w, layer, *, groups, tm, seq_len):
    m = x.shape[0]
    tiles = m // (groups * tm)
    multi = seq_len > 0
    kern = functools.partial(_ffn_kernel, tm=tm, seq_len=seq_len)
    once = dict(pipeline_mode=pl.Buffered(1))
    x_spec = pl.BlockSpec((tm, D_MODEL), lambda s, j: (s * tiles + j, 0))
    w_specs = [_layer_spec(layer, D_MODEL, 2 * D_FF, **once), _layer_spec(layer, FFN_CONV, 2 * D_FF, **once),
               _layer_spec(layer, 1, 2 * D_FF, **once), _layer_spec(layer, D_FF, D_MODEL, **once),
               _layer_spec(layer, 1, D_MODEL, **once), _layer_spec(layer, 1, D_MODEL, **once)]
    w_args = (w["wup"], w["fcw"], w["fcb"], w["wdn"], w["ln2_g"], w["ln2_b"])
    xp = pltpu.VMEM((FF_XP_SLOTS, SUBLANES + tm, FF_CH), f32)
    if multi:
        nst = 2 * (tm // seq_len)
        inputs = (x, st) + w_args
        in_specs = [x_spec, pl.BlockSpec((None, nst, 2 * D_FF), lambda s, j: (layer, s * tiles + j, 0))] + w_specs
        sout_shape = jax.ShapeDtypeStruct((m, 2 * D_FF), f32)
        sout_spec = pl.BlockSpec((tm, 2 * D_FF), lambda s, j: (s * tiles + j, 0))
        scratch = [xp]
    else:
        inputs = (x,) + w_args
        in_specs = [x_spec] + w_specs
        sout_shape = jax.ShapeDtypeStruct((groups, SUBLANES, 2 * D_FF), f32)
        sout_spec = pl.BlockSpec((None, SUBLANES, 2 * D_FF), lambda s, j: (s, 0, 0))
        scratch = [xp, pltpu.VMEM((SUBLANES, 2 * D_FF), f32)]
    return pl.pallas_call(
        kern,
        out_shape=(jax.ShapeDtypeStruct((m, D_MODEL), f32), sout_shape),
        grid=(groups, tiles),
        in_specs=in_specs,
        out_specs=(x_spec, sout_spec),
        scratch_shapes=scratch,
        compiler_params=pltpu.CompilerParams(dimension_semantics=("parallel", "arbitrary"),
                                             vmem_limit_bytes=56 * 1024 * 1024),
        name="ffn",
    )(*inputs)


def _pad_lanes(v, off, width=SMALL):
    out = jnp.zeros((v.shape[0], 1, width), f32)
    return out.at[:, 0, off:off + v.shape[1]].set(v.astype(f32))


def _prep_weights(w_in, ssd_conv_w, ssd_conv_b, ssd_dt_bias, ssd_a_log, ssd_d, ssd_norm_w, mlstm_gate_b,
                  mlstm_norm_w, w_branch_a, w_branch_b, w_out, ln1_g, ln1_b, ffn_w_up, ffn_conv_w,
                  ffn_conv_b, ffn_w_down, ln2_g, ln2_b):
    d = D_MODEL
    o_z, o_xbc, o_dt = 0, d, d + d + SSD_BC
    o_q = o_dt + SSD_HEADS
    o_if = o_q + 3 * d
    o_o = o_if + 2 * MLSTM_HEADS
    o_g = o_o + d
    w_t = jnp.swapaxes(w_in, 1, 2)
    cols = lambda a, n: w_t[:, a:a + n, :]
    zeros = lambda n: jnp.zeros((DEPTH, n, d), w_in.dtype)
    w32 = jnp.concatenate([cols(o_z, d), cols(o_o, d), cols(o_g, 2 * d), cols(o_xbc, d + SSD_BC),
                           cols(o_dt, SSD_HEADS), cols(o_if, 2 * MLSTM_HEADS),
                           zeros(P32_W - P32_SM_OFF - SSD_HEADS - 2 * MLSTM_HEADS)], axis=1).astype(bf16)
    e = (np.arange(SSD_HP)[None, :] // SSD_HEAD_DIM == np.arange(LANES)[:, None])
    bd = ((np.arange(SSD_GROUPS * SSD_STATE)[:, None] < SSD_STATE)
          == (np.arange(SSD_HP)[None, :] < SSD_HP // SSD_GROUPS))
    row = lambda a: a[:, None, :]
    return dict(
        w32=w32, wqkv=cols(o_q, 3 * d).astype(bf16),
        cwx=ssd_conv_w[:, :, :d], cbx=row(ssd_conv_b[:, :d]),
        cwb=ssd_conv_w[:, :, d:], cbb=row(ssd_conv_b[:, d:]),
        dtb=_pad_lanes(ssd_dt_bias, DT_OFF), alog=_pad_lanes(ssd_a_log, DT_OFF),
        dexp=row(jnp.repeat(ssd_d.astype(f32), SSD_HEAD_DIM, axis=1)), ssd_nw=row(ssd_norm_w),
        e=jnp.asarray(e, bf16), bd=jnp.asarray(bd, f32),
        gate_b=_pad_lanes(mlstm_gate_b, I_OFF), mlstm_nw=row(mlstm_norm_w),
        wa=w_branch_a.astype(bf16), wb=w_branch_b.astype(bf16), wo=w_out.astype(bf16),
        ln1_g=row(ln1_g), ln1_b=row(ln1_b),
        wup=ffn_w_up.astype(bf16), fcw=ffn_conv_w, fcb=row(ffn_conv_b), wdn=ffn_w_down.astype(bf16),
        ln2_g=row(ln2_g), ln2_b=row(ln2_b),
    )


class _Group:
    def __init__(self, batch, length, q, lr, gs, ssd_cps, mlstm_cps, proj_tm, merge_tm, ffn):
        self.batch, self.length, self.q, self.lr, self.gs = batch, length, q, lr, gs
        self.ssd_cps, self.mlstm_cps = ssd_cps, mlstm_cps
        self.rows = batch * length
        self.proj_tm, self.merge_tm, self.ffn = proj_tm, merge_tm, ffn

    def view(self, a):
        return a.reshape(self.batch, self.length, a.shape[-1])

    def tiling(self, cps):
        rows = cps * self.q
        gs = self.gs
        return self.length // rows, lambda width, blk: pl.BlockSpec((gs, rows, width), lambda b, c: (b, c, blk))


def _ssd(grp, p32, state, w, layer, prev):
    q, b, gs = grp.q, grp.batch, grp.gs
    has_state = state is not None
    steps, tile = grp.tiling(grp.ssd_cps)
    kern = functools.partial(_ssd_kernel, q=q, lr=grp.lr, nc=steps, has_state=has_state, gs=gs, cps=grp.ssd_cps)
    pv = grp.view(p32)
    inputs = [pv, pv, pv, pv]
    in_specs = [tile(D_MODEL, P32_Z), tile(D_MODEL, P32_XS),
                tile(SSD_BC, P32_BC_OFF // SSD_BC), tile(SMALL, P32_SM_OFF // SMALL)]
    if has_state:
        inputs += [state["csx"], state["csb"], state["h"]]
        in_specs += [_seq_spec(layer, gs, SSD_CONV - 1, D_MODEL), _seq_spec(layer, gs, SSD_CONV - 1, SSD_BC),
                     _seq_spec(layer, gs, SSD_HP, SSD_STATE)]
    inputs += [w["cwx"], w["cbx"], w["cwb"], w["cbb"], w["dtb"], w["alog"], w["dexp"], w["ssd_nw"], w["e"],
               w["bd"]]
    const = lambda *shape: pl.BlockSpec(shape, lambda b, c: (0,) * len(shape))
    in_specs += [_layer_spec(layer, SSD_CONV, D_MODEL), _layer_spec(layer, 1, D_MODEL),
                 _layer_spec(layer, SSD_CONV, SSD_BC), _layer_spec(layer, 1, SSD_BC),
                 _layer_spec(layer, 1, SMALL), _layer_spec(layer, 1, SMALL), _layer_spec(layer, 1, D_MODEL),
                 _layer_spec(layer, 1, D_MODEL), const(LANES, SSD_HP), const(SSD_GROUPS * SSD_STATE, SSD_HP)]
    return _stacked_call(
        kern, name="ssd", grid=(b // gs, steps), inputs=inputs, in_specs=in_specs,
        out_shape=(jax.ShapeDtypeStruct((b, grp.length, D_MODEL), bf16),
                   jax.ShapeDtypeStruct((DEPTH, b, SSD_CONV - 1, D_MODEL), f32),
                   jax.ShapeDtypeStruct((DEPTH, b, SSD_CONV - 1, SSD_BC), f32),
                   jax.ShapeDtypeStruct((DEPTH, b, SSD_HP, SSD_STATE), f32)),
        out_specs=(tile(D_MODEL, 0), _seq_spec(layer, gs, SSD_CONV - 1, D_MODEL),
                   _seq_spec(layer, gs, SSD_CONV - 1, SSD_BC), _seq_spec(layer, gs, SSD_HP, SSD_STATE)),
        stacked={1: prev and prev[0], 2: prev and prev[1], 3: prev and prev[2]},
        scratch_shapes=[pltpu.VMEM((gs, SUBLANES + q, D_MODEL), f32), pltpu.VMEM((gs, SUBLANES + q, SSD_BC), f32),
                        pltpu.VMEM((gs, SSD_GROUPS * SSD_STATE, SSD_HP), f32), pltpu.VMEM((gs, q, D_MODEL), f32)],
        dimension_semantics=("parallel", "arbitrary"))


def _mlstm(grp, qkv, p32, state, w, layer, prev):
    q, b, gs = grp.q, grp.batch, grp.gs
    has_state = state is not None
    steps, tile = grp.tiling(grp.mlstm_cps)
    kern = functools.partial(_mlstm_kernel, q=q, lr=grp.lr, nc=steps, has_state=has_state, gs=gs,
                             cps=grp.mlstm_cps)
    hd = MLSTM_HEAD_DIM
    carried = not (has_state and steps == 1 and grp.mlstm_cps == 1)
    qv, pv = grp.view(qkv), grp.view(p32)
    inputs = [qv, qv, qv, pv, pv]
    in_specs = [tile(D_MODEL, 0), tile(D_MODEL, 1), tile(D_MODEL, 2), tile(D_MODEL, P32_O),
                tile(SMALL, P32_SM_OFF // SMALL)]
    if has_state:
        inputs += [state["c"], state["n"], state["m"]]
        in_specs += [_seq_spec(layer, gs, MLSTM_HEADS, hd, hd), _seq_spec(layer, gs, MLSTM_HEADS, hd),
                     _seq_spec(layer, gs, 1, SMALL)]
    inputs += [w["gate_b"], w["mlstm_nw"]]
    in_specs += [_layer_spec(layer, 1, SMALL), _layer_spec(layer, 1, D_MODEL)]
    return _stacked_call(
        kern, name="mlstm", grid=(b // gs, steps), inputs=inputs, in_specs=in_specs,
        out_shape=(jax.ShapeDtypeStruct((b, grp.length, D_MODEL), bf16),
                   jax.ShapeDtypeStruct((DEPTH, b, MLSTM_HEADS, hd, hd), f32),
                   jax.ShapeDtypeStruct((DEPTH, b, MLSTM_HEADS, hd), f32),
                   jax.ShapeDtypeStruct((DEPTH, b, 1, SMALL), f32)),
        out_specs=(tile(D_MODEL, 0), _seq_spec(layer, gs, MLSTM_HEADS, hd, hd),
                   _seq_spec(layer, gs, MLSTM_HEADS, hd), _seq_spec(layer, gs, 1, SMALL)),
        stacked={1: prev and prev[0], 2: prev and prev[1], 3: prev and prev[2]},
        scratch_shapes=[pltpu.VMEM((gs, MLSTM_HEADS, hd, hd) if carried else (gs, 1, SUBLANES, LANES), f32),
                        pltpu.VMEM((gs, MLSTM_HEADS, hd), f32), pltpu.VMEM((gs, 1, SMALL), f32)],
        dimension_semantics=("parallel", "arbitrary"))


def _trunk(grp, x, state, w):
    ssd_out = mlstm_out = None
    ffn_out = []
    for layer in range(DEPTH):
        p32 = _proj(x, w["w32"], layer, f32, grp.proj_tm)
        qkv = _proj(x, w["wqkv"], layer, bf16, grp.proj_tm)
        ys, *ssd_out = _ssd(grp, p32, state, w, layer, ssd_out)
        hm, *mlstm_out = _mlstm(grp, qkv, p32, state, w, layer, mlstm_out)
        x1 = _merge(ys.reshape(grp.rows, D_MODEL), hm.reshape(grp.rows, D_MODEL), p32, x, w, layer,
                    grp.merge_tm)
        x, s_ffn = _ffn(x1, state["ffn"] if state is not None else None, w, layer, **grp.ffn)
        ffn_out.append(s_ffn)
    return x, ssd_out, mlstm_out, ffn_out


def _unpack_states(batch, ssd_out, mlstm_out):
    csx, csb, h = ssd_out
    c, n, m = mlstm_out
    return (h.reshape(DEPTH, batch, SSD_HEADS, SSD_HEAD_DIM, SSD_STATE),
            jnp.concatenate([csx, csb], axis=-1), c, n, m[:, :, 0, :MLSTM_HEADS])


def kernel(x_prompt, x_sample, state_ssd, state_ssd_conv, state_mlstm_c, state_mlstm_n, state_mlstm_m,
           state_ffn_conv, w_in, ssd_conv_w, ssd_conv_b, ssd_dt_bias, ssd_a_log, ssd_d, ssd_norm_w,
           mlstm_gate_b, mlstm_norm_w, w_branch_a, w_branch_b, w_out, ln1_g, ln1_b, ffn_w_up, ffn_conv_w,
           ffn_conv_b, ffn_w_down, ln2_g, ln2_b):
    w = _prep_weights(w_in, ssd_conv_w, ssd_conv_b, ssd_dt_bias, ssd_a_log, ssd_d, ssd_norm_w, mlstm_gate_b,
                      mlstm_norm_w, w_branch_a, w_branch_b, w_out, ln1_g, ln1_b, ffn_w_up, ffn_conv_w,
                      ffn_conv_b, ffn_w_down, ln2_g, ln2_b)
    keep = FFN_CONV - 1

    bp, lp, _ = x_prompt.shape
    prompt = _Group(bp, lp, CHUNK, CHUNK, gs=1, ssd_cps=4, mlstm_cps=1, proj_tm=2048, merge_tm=512,
                    ffn=dict(groups=bp, tm=512, seq_len=0))
    y_p, ssd_p, mlstm_p, ffn_p = _trunk(prompt, x_prompt.reshape(bp * lp, D_MODEL), None, w)
    st_p = _unpack_states(bp, ssd_p, mlstm_p)
    ffn_conv_p = jnp.stack(ffn_p)[:, :, SUBLANES - keep:, :]

    bs, ls, _ = x_sample.shape
    lpad = SAMPLE_PAD_LEN
    s_rows = bs * lpad
    sample = _Group(bs, lpad, lpad, ls, gs=8, ssd_cps=1, mlstm_cps=1, proj_tm=s_rows, merge_tm=512,
                    ffn=dict(groups=1, tm=256, seq_len=lpad))
    s_state = dict(
        csx=state_ssd_conv[..., :D_MODEL], csb=state_ssd_conv[..., D_MODEL:],
        h=state_ssd.reshape(DEPTH, bs, SSD_HP, SSD_STATE),
        c=state_mlstm_c, n=state_mlstm_n,
        m=jnp.pad(state_mlstm_m, ((0, 0), (0, 0), (0, SMALL - MLSTM_HEADS)))[:, :, None, :],
        ffn=state_ffn_conv.reshape(DEPTH, bs * keep, 2 * D_FF),
    )
    xs = jnp.pad(x_sample, ((0, 0), (0, lpad - ls), (0, 0))).reshape(s_rows, D_MODEL)
    y_s, ssd_s, mlstm_s, ffn_s = _trunk(sample, xs, s_state, w)
    st_s = _unpack_states(bs, ssd_s, mlstm_s)
    ffn_conv_s = jnp.stack([u.reshape(bs, lpad, 2 * D_FF)[:, ls - keep:ls, :] for u in ffn_s])
    y_sample = y_s.reshape(bs, lpad, D_MODEL)[:, :ls, :]

    return (y_p.reshape(bp, lp, D_MODEL), y_sample, st_p[0], st_s[0], st_p[1], st_s[1], st_p[2], st_s[2],
            st_p[3], st_s[3], st_p[4], st_s[4], ffn_conv_p, ffn_conv_s)
```

```python
import functools
import itertools

import jax
import jax.numpy as jnp
import numpy as np
from jax import lax
from jax.experimental import pallas as pl
from jax.experimental.pallas import tpu as pltpu

f32 = jnp.float32
bf16 = jnp.bfloat16

D_MODEL = 1024
DEPTH = 2
SSD_HEADS = 16
SSD_HEAD_DIM = 64
SSD_STATE = 64
SSD_GROUPS = 2
SSD_CONV = 4
SSD_BC = 2 * SSD_GROUPS * SSD_STATE
SSD_HP = SSD_HEADS * SSD_HEAD_DIM
MLSTM_HEADS = 4
MLSTM_HEAD_DIM = 256
CHUNK = 128
D_FF = 2816
FFN_CONV = 3
ALPHA = (2 * DEPTH) ** 0.25
EPS = 1e-5

LANES = 128
SUBLANES = 8
SMALL = LANES
DT_OFF, I_OFF, F_OFF = 0, 16, 20
P32_Z, P32_O, P32_GA, P32_GB, P32_XS = 0, 1, 2, 3, 4
P32_BC_OFF = 5 * D_MODEL
P32_SM_OFF = P32_BC_OFF + SSD_BC
P32_W = P32_SM_OFF + 2 * SMALL
PROJ_TN = 512
FF_CH = 256
FF_NCH = D_FF // FF_CH
MERGE_PARTS = 2
FF_UP_AHEAD = 2
FF_XP_SLOTS = 4
NEG_BIG = -1e30
SAMPLE_PAD_LEN = 4

NT_DIMS = (((1,), (1,)), ((), ()))
TN_DIMS = (((0,), (0,)), ((), ()))


def _dot(a, b):
    return jnp.dot(a, b, preferred_element_type=f32)


def _split3(x):
    hi = x.astype(bf16)
    r = x - hi.astype(f32)
    mid = r.astype(bf16)
    lo = (r - mid.astype(f32)).astype(bf16)
    return hi, mid, lo


def _dot01_rhs(x, e):
    hi, mid, lo = _split3(x)
    return _dot(hi, e) + _dot(mid, e) + _dot(lo, e)


def _dot01_lhs(t, x):
    hi, mid, lo = _split3(x)
    return _dot(t, hi) + _dot(t, mid) + _dot(t, lo)


def _softplus(x):
    return jnp.maximum(x, 0.0) + jnp.log1p(jnp.exp(-jnp.abs(x)))


def _silu(x):
    return x * jax.nn.sigmoid(x)


def _tri(q):
    row = lax.broadcasted_iota(jnp.int32, (q, q), 0)
    col = lax.broadcasted_iota(jnp.int32, (q, q), 1)
    return row >= col


def _valid_rows(q, width, lr, is_last):
    row = lax.broadcasted_iota(jnp.int32, (q, width), 0)
    return row < jnp.where(is_last, lr, q)


def _layer_norm(r, g, b):
    mu = jnp.mean(r, axis=-1, keepdims=True)
    var = jnp.mean(jnp.square(r - mu), axis=-1, keepdims=True)
    return (r - mu) * lax.rsqrt(var + EPS) * g + b


def _layer_spec(layer, *shape, **kw):
    zeros = (0,) * len(shape)
    return pl.BlockSpec((None,) + shape, lambda *_: (layer,) + zeros, **kw)


def _seq_spec(layer, gs, *shape):
    zeros = (0,) * len(shape)
    return pl.BlockSpec((None, gs) + shape, lambda b, c: (layer, b) + zeros)


def _stacked_call(kern, *, name, grid, inputs, in_specs, out_shape, out_specs, stacked, scratch_shapes,
                  dimension_semantics, vmem_limit_bytes=None):
    prev = [(i, a) for i, a in sorted(stacked.items()) if a is not None]
    n_in = len(inputs)

    def body(*refs):
        kern(*refs[:n_in], *refs[n_in + len(prev):])

    return pl.pallas_call(
        body,
        out_shape=out_shape,
        grid=grid,
        in_specs=list(in_specs) + [pl.BlockSpec(memory_space=pl.ANY)] * len(prev),
        out_specs=out_specs,
        scratch_shapes=scratch_shapes,
        input_output_aliases={n_in + k: i for k, (i, _) in enumerate(prev)},
        compiler_params=pltpu.CompilerParams(dimension_semantics=dimension_semantics,
                                             vmem_limit_bytes=vmem_limit_bytes),
        name=name,
    )(*inputs, *[a for _, a in prev])


def _proj_kernel(x_ref, w_ref, o_ref, xb):
    @pl.when(pl.program_id(1) == 0)
    def _():
        xb[...] = x_ref[...].astype(bf16)

    o_ref[...] = lax.dot_general(xb[...], w_ref[...], NT_DIMS, preferred_element_type=f32).astype(o_ref.dtype)


def _proj(x, wt, layer, out_dtype, tm):
    m, k = x.shape
    n = wt.shape[1]
    tn = PROJ_TN
    return pl.pallas_call(
        _proj_kernel,
        out_shape=jax.ShapeDtypeStruct((m, n), out_dtype),
        grid=(m // tm, n // tn),
        in_specs=[pl.BlockSpec((tm, k), lambda i, j: (i, 0)),
                  pl.BlockSpec((None, tn, k), lambda i, j: (layer, j, 0))],
        out_specs=pl.BlockSpec((tm, tn), lambda i, j: (i, j)),
        scratch_shapes=[pltpu.VMEM((tm, k), bf16)],
        compiler_params=pltpu.CompilerParams(dimension_semantics=("parallel", "arbitrary")),
        name="proj",
    )(x, wt)


def _per_sequence(seq_fn, refs, n_tile, n_state, n_param, gs, has_state, nc):
    n_state = n_state if has_state else 0
    tiles, refs = refs[:n_tile], refs[n_tile:]
    state, refs = refs[:n_state], refs[n_state:]
    params, rest = refs[:n_param], refs[n_param:]
    phases = []
    for g in range(gs):
        at = lambda group: tuple(r.at[g] for r in group)
        phases.append(seq_fn(*at(tiles), *at(state), *params, *at(rest)))
    c = pl.program_id(1)

    @pl.when(c == 0)
    def _():
        for init, _, _ in phases:
            init()

    for _ in itertools.zip_longest(*[body() for _, body, _ in phases]):
        pass

    @pl.when(c == nc - 1)
    def _():
        for _, _, final in phases:
            final()


def _ssd_kernel(*refs, q, lr, nc, has_state, gs, cps):
    seq = functools.partial(_ssd_seq, q=q, lr=lr, nc=nc, has_state=has_state, cps=cps)
    _per_sequence(seq, refs, 4, 3, 10, gs, has_state, nc)


def _ssd_seq(*refs, q, lr, nc, has_state, cps):
    z_ref, xs_ref, bc_ref, sm_ref = refs[:4]
    refs = refs[4:]
    if has_state:
        csx_ref, csb_ref, h0_ref = refs[:3]
        refs = refs[3:]
    (cwx_ref, cbx_ref, cwb_ref, cbb_ref, dtb_ref, alog_ref, dexp_ref, nw_ref, e_ref, bd_ref,
     y_ref, ncsx_ref, ncsb_ref, hout_ref, xpx, xpb, ht, yb) = refs
    hdr = SUBLANES
    lo = hdr - (SSD_CONV - 1)
    n2 = SSD_GROUPS * SSD_STATE
    assert lr >= SSD_CONV - 1

    def init():
        if has_state:
            h_t = h0_ref[...].T
            ht[...] = jnp.where(bd_ref[...] > 0.5, jnp.concatenate([h_t, h_t], axis=0), 0.0)
            xpx[lo:hdr, :] = csx_ref[...]
            xpb[lo:hdr, :] = csb_ref[...]
        else:
            ht[...] = jnp.zeros_like(ht)
            xpx[lo:hdr, :] = jnp.zeros((SSD_CONV - 1, SSD_HP), f32)
            xpb[lo:hdr, :] = jnp.zeros((SSD_CONV - 1, SSD_BC), f32)

    def final():
        ncsx_ref[...] = xpx[lo + lr:hdr + lr, :]
        ncsb_ref[...] = xpb[lo + lr:hdr + lr, :]
        h_new = ht[...]
        hout_ref[...] = (h_new[:SSD_STATE, :] + h_new[SSD_STATE:, :]).T

    def body():
        for k in range(cps):
            sub = lambda r: r.at[pl.ds(k * q, q)]
            is_last = (pl.program_id(1) == nc - 1) if k == cps - 1 else False
            yield from _ssd_body(sub(z_ref), sub(xs_ref), sub(bc_ref), sub(sm_ref), cwx_ref, cbx_ref, cwb_ref,
                                 cbb_ref, dtb_ref, alog_ref, dexp_ref, nw_ref, e_ref, bd_ref, sub(y_ref),
                                 xpx, xpb, ht, yb, q=q, lr=lr, is_last=is_last)

    return init, body, final


def _ssd_body(z_ref, xs_ref, bc_ref, sm_ref, cwx_ref, cbx_ref, cwb_ref, cbb_ref, dtb_ref, alog_ref,
              dexp_ref, nw_ref, e_ref, bd_ref, y_ref, xpx, xpb, ht, yb, *, q, lr, is_last):
    hdr = SUBLANES
    lo = hdr - (SSD_CONV - 1)
    n2 = SSD_GROUPS * SSD_STATE
    block_diag = bd_ref[...] > 0.5

    dt = _softplus(sm_ref[...] + dtb_ref[...])
    if lr < q:
        dt = jnp.where(_valid_rows(q, SMALL, lr, is_last), dt, 0.0)
    a = -jnp.exp(alog_ref[...])
    d_a = dt * a
    causal = _tri(q)
    tril = jnp.where(causal, 1.0, 0.0).astype(bf16)
    e = e_ref[...]
    acs = _dot01_lhs(tril, d_a)
    dt_x = _dot01_rhs(dt, e)
    yield

    xpx[hdr:hdr + q, :] = xs_ref[...]
    xpb[hdr:hdr + q, :] = bc_ref[...]

    def conv(xp, w_ref, b_ref):
        w = w_ref[...]
        acc = xp[lo:lo + q, :] * w[0:1, :]
        for j in range(1, SSD_CONV):
            acc = acc + xp[lo + j:lo + j + q, :] * w[j:j + 1, :]
        return acc + b_ref[...]

    cb = conv(xpb, cwb_ref, cbb_ref)
    bcv = _silu(cb)
    bm = bcv[:, :n2].astype(bf16)
    cm = bcv[:, n2:]
    lane_g0 = lax.broadcasted_iota(jnp.int32, (q, n2), 1) < SSD_STATE
    acs_t = acs.T
    acs_x = _dot01_rhs(acs, e)
    yield
    cbms = [lax.dot_general(jnp.where(lane_g0 if g == 0 else jnp.logical_not(lane_g0), cm, 0.0).astype(bf16),
                            bm, NT_DIMS, preferred_element_type=f32) for g in range(SSD_GROUPS)]
    h_prev = ht[...]
    y_off = _dot(cm.astype(bf16), h_prev.astype(bf16))
    cx = conv(xpx, cwx_ref, cbx_ref)
    tail_x = xpx[lo + q:hdr + q, :]
    tail_b = xpb[lo + q:hdr + q, :]
    xpx[lo:hdr, :] = tail_x
    xpb[lo:hdr, :] = tail_b
    yield
    xs = _silu(cx)
    last_x = acs_x[q - 1:q, :]
    xdt = xs * dt_x
    xdt_b = xdt.astype(bf16)
    yield
    lane_lo = lax.broadcasted_iota(jnp.int32, (q, LANES), 1) < SSD_HEAD_DIM
    heads_per_group = SSD_HEADS // SSD_GROUPS
    decays = [jnp.exp(jnp.where(causal, acs[:, hh:hh + 1] - acs_t[hh:hh + 1, :], -jnp.inf))
              for hh in range(SSD_HEADS)]
    yield
    weights = [(cbms[hh // heads_per_group] * decays[hh]).astype(bf16) for hh in range(SSD_HEADS)]
    xdtw = (xdt * jnp.exp(last_x - acs_x)).astype(bf16)
    yield
    ys = [_dot(weights[hh], xdt_b[:, (hh // 2) * LANES:(hh // 2 + 1) * LANES]) for hh in range(SSD_HEADS)]
    upd = lax.dot_general(bm, xdtw, TN_DIMS, preferred_element_type=f32)
    yield
    for p in range(SSD_HEADS // 2):
        yb[:, p * LANES:(p + 1) * LANES] = jnp.where(lane_lo, ys[2 * p], ys[2 * p + 1])
    ht[...] = jnp.exp(last_x) * h_prev + jnp.where(block_diag, upd, 0.0)
    yield
    y = yb[...] + y_off * jnp.exp(acs_x) + dexp_ref[...] * xs
    y = y * _silu(z_ref[...])
    yield
    y = y * lax.rsqrt(jnp.mean(jnp.square(y), axis=-1, keepdims=True) + EPS) * nw_ref[...]
    y_ref[...] = y.astype(y_ref.dtype)


def _mlstm_kernel(*refs, q, lr, nc, has_state, gs, cps):
    seq = functools.partial(_mlstm_seq, q=q, lr=lr, nc=nc, has_state=has_state, cps=cps)
    _per_sequence(seq, refs, 5, 3, 2, gs, has_state, nc)


def _mlstm_seq(*refs, q, lr, nc, has_state, cps):
    q_ref, k_ref, v_ref, o_ref, sm_ref = refs[:5]
    refs = refs[5:]
    if has_state:
        c0_ref, n0_ref, m0_ref = refs[:3]
        refs = refs[3:]
    gb_ref, nw_ref, h_ref, cout_ref, nout_ref, mout_ref, cs, ns, ms = refs
    direct = has_state and nc == 1 and cps == 1

    def init():
        if direct:
            return
        if has_state:
            cs[...] = c0_ref[...]
            ns[...] = n0_ref[...]
            ms[...] = m0_ref[...]
        else:
            cs[...] = jnp.zeros_like(cs)
            ns[...] = jnp.zeros_like(ns)
            ms[...] = jnp.zeros_like(ms)

    def final():
        if direct:
            return
        cout_ref[...] = cs[...]
        nout_ref[...] = ns[...]
        mout_ref[...] = ms[...]

    def body():
        chunks = []
        for k in range(cps):
            sub = lambda r, k=k: r.at[pl.ds(k * q, q)]
            is_last = (pl.program_id(1) == nc - 1) if k == cps - 1 else False
            src = (c0_ref, n0_ref, m0_ref) if direct else (cs, ns, ms)
            dst = (cout_ref, nout_ref, mout_ref) if direct else (cs, ns, ms)
            chunks.append(_mlstm_body(sub(q_ref), sub(k_ref), sub(v_ref), sub(o_ref), sub(sm_ref), gb_ref, nw_ref,
                                      sub(h_ref), src, dst, q=q, lr=lr, is_last=is_last))
        yield from _staggered(chunks, MLSTM_STATE_STAGES)

    return init, body, final


MLSTM_STATE_STAGES = 8


def _staggered(gens, skew):
    done = [False] * len(gens)
    t = 0
    while not all(done):
        for i, g in enumerate(gens):
            if done[i] or t < i * skew:
                continue
            try:
                next(g)
            except StopIteration:
                done[i] = True
        t += 1
        yield


def _mlstm_body(q_ref, k_ref, v_ref, o_ref, sm_ref, gb_ref, nw_ref, h_ref, src, dst, *, q, lr, is_last):
    c_src, n_src, m_src = src
    c_dst, n_dst, m_dst = dst
    sm = sm_ref[...] + gb_ref[...]
    logf = -_softplus(-sm)
    ipre = sm
    if lr < q:
        valid = _valid_rows(q, SMALL, lr, is_last)
        logf = jnp.where(valid, logf, 0.0)
        ipre = jnp.where(valid, ipre, NEG_BIG)
    causal = _tri(q)
    tril = jnp.where(causal, 1.0, 0.0).astype(bf16)
    yield
    bcum = _dot01_lhs(tril, logf)
    ipre_t = ipre.T
    yield
    bcum_t = bcum.T
    lane = lax.broadcasted_iota(jnp.int32, (1, SMALL), 1)
    k_scale = MLSTM_HEAD_DIM ** -0.5

    heads = range(MLSTM_HEADS)
    sls = [slice(h * MLSTM_HEAD_DIM, (h + 1) * MLSTM_HEAD_DIM) for h in heads]
    q_all, k_all, v_all, o_all = q_ref[...], k_ref[...], v_ref[...], o_ref[...]
    qs = [q_all[:, sl] for sl in sls]
    ks = [k_all[:, sl] * k_scale for sl in sls]
    vs = [v_all[:, sl] for sl in sls]
    b_cols = [bcum[:, F_OFF + h:F_OFF + h + 1] for h in heads]
    i_cols = [ipre[:, I_OFF + h:I_OFF + h + 1] for h in heads]
    dmats = [jnp.where(causal, b_cols[h] - bcum_t[F_OFF + h:F_OFF + h + 1, :] + ipre_t[I_OFF + h:I_OFF + h + 1, :],
                       -jnp.inf) for h in heads]
    yield
    qk = [lax.dot_general(qs[h], ks[h], NT_DIMS, preferred_element_type=f32) for h in heads]
    d_max = [jnp.max(dmats[h], axis=-1, keepdims=True) for h in heads]
    yield
    n_all = n_src[...]
    m_all = m_src[...]
    m_new = m_all
    cs_in = [c_src[h] for h in heads]
    m_prevs = [m_all[:, h:h + 1] for h in heads]
    qc = [_dot(qs[h], cs_in[h].astype(bf16)) for h in heads]
    inters = [b_cols[h] + m_prevs[h] for h in heads]
    m_ts = [jnp.maximum(inters[h], d_max[h]) for h in heads]
    yield
    w_inters = [jnp.exp(inters[h] - m_ts[h]) for h in heads]
    ss = [qk[h] * jnp.exp(dmats[h] - m_ts[h]) for h in heads]
    yield
    sv = [_dot(ss[h].astype(bf16), vs[h]) for h in heads]
    m_ends = [m_ts[h][q - 1:q, :] for h in heads]
    b_lasts = [b_cols[h][q - 1:q, :] for h in heads]
    kws = [ks[h].astype(f32) * jnp.exp(b_lasts[h] - b_cols[h] + i_cols[h] - m_ends[h]) for h in heads]
    yield
    kv = [lax.dot_general(kws[h].astype(bf16), vs[h], TN_DIMS, preferred_element_type=f32) for h in heads]
    qns = [jnp.sum(qs[h].astype(f32) * n_all[h:h + 1, :], axis=-1, keepdims=True) for h in heads]
    yield
    dens = [jnp.sum(ss[h], axis=-1, keepdims=True) + w_inters[h] * qns[h] for h in heads]
    yield
    hvs = [(sv[h] + w_inters[h] * qc[h]) / jnp.maximum(jnp.abs(dens[h]), jnp.exp(-m_ts[h])) for h in heads]
    yield
    rms = [lax.rsqrt(jnp.mean(jnp.square(hvs[h]), axis=-1, keepdims=True) + EPS) for h in heads]
    yield
    h_new = [(hvs[h] * rms[h] * nw_ref[:, sls[h]] * jax.nn.sigmoid(o_all[:, sls[h]])).astype(h_ref.dtype)
             for h in heads]
    w_cs = [jnp.exp(b_lasts[h] + m_prevs[h] - m_ends[h]) for h in heads]
    yield
    for h in heads:
        c_dst[h] = w_cs[h] * cs_in[h] + kv[h]
        m_new = jnp.where(lane == h, m_ends[h], m_new)
    h_ref[...] = jnp.concatenate(h_new, axis=1)
    n_dst[...] = jnp.concatenate(
        [w_cs[h] * n_all[h:h + 1, :] + jnp.sum(kws[h], axis=0, keepdims=True) for h in heads], axis=0)
    m_dst[...] = m_new


def _merge_kernel(ys_ref, hm_ref, ga_ref, gb_ref, x_ref, wa_ref, wb_ref, wo_ref, g_ref, b_ref, o_ref):
    tm = x_ref.shape[0]
    rows = [pl.ds(i * (tm // MERGE_PARTS), tm // MERGE_PARTS) for i in range(MERGE_PARTS)]
    br = [(_dot(ys_ref[r, :], wa_ref[...]), _dot(hm_ref[r, :], wb_ref[...])) for r in rows]
    merged = [(jax.nn.sigmoid(ga_ref[r, :]) * a + jax.nn.sigmoid(gb_ref[r, :]) * b).astype(bf16)
              for r, (a, b) in zip(rows, br)]
    mix = [_dot(m, wo_ref[...]) for m in merged]
    for r, m in zip(rows, mix):
        o_ref[r, :] = _layer_norm(ALPHA * x_ref[r, :] + m, g_ref[...], b_ref[...])


def _merge(ys, hm, p32, x, w, layer, tm):
    m = x.shape[0]
    row = lambda blk: pl.BlockSpec((tm, D_MODEL), lambda i: (i, blk))
    return pl.pallas_call(
        _merge_kernel,
        out_shape=jax.ShapeDtypeStruct((m, D_MODEL), f32),
        grid=(m // tm,),
        in_specs=[row(0), row(0), row(P32_GA), row(P32_GB), row(0),
                  _layer_spec(layer, D_MODEL, D_MODEL), _layer_spec(layer, D_MODEL, D_MODEL),
                  _layer_spec(layer, D_MODEL, D_MODEL), _layer_spec(layer, 1, D_MODEL),
                  _layer_spec(layer, 1, D_MODEL)],
        out_specs=row(0),
        compiler_params=pltpu.CompilerParams(dimension_semantics=("parallel",)),
        name="merge",
    )(ys, hm, p32, p32, x, w["wa"], w["wb"], w["wo"], w["ln1_g"], w["ln1_b"])


def _ffn_kernel(*refs, tm, seq_len):
    multi = seq_len > 0
    if multi:
        x_ref, st_ref, wup_ref, cw_ref, cb_ref, wdn_ref, g_ref, b_ref, o_ref, sout_ref, xp = refs
    else:
        x_ref, wup_ref, cw_ref, cb_ref, wdn_ref, g_ref, b_ref, o_ref, sout_ref, xp, carry = refs
        @pl.when(pl.program_id(1) == 0)
        def _():
            carry[...] = jnp.zeros_like(carry)

    hdr = SUBLANES
    x = x_ref[...]
    xb = x.astype(bf16)
    if multi:
        assert seq_len & (seq_len - 1) == 0
        nseq = tm // seq_len
        t = lax.broadcasted_iota(jnp.int32, (tm, FF_CH), 0) & (seq_len - 1)
        row = lax.broadcasted_iota(jnp.int32, (tm, 2 * nseq), 0)
        col = lax.broadcasted_iota(jnp.int32, (tm, 2 * nseq), 1)
        t_sel = row & (seq_len - 1)
        seq0 = lax.shift_right_logical(row - t_sel, (seq_len // 2).bit_length() - 1)
        sel_p2 = jnp.where(col == seq0 + t_sel, jnp.where(t_sel < 2, 1.0, 0.0), 0.0).astype(bf16)
        sel_p1 = jnp.where(col == seq0 + 1, jnp.where(t_sel == 0, 1.0, 0.0), 0.0).astype(bf16)
        xp[:, 0:hdr, :] = jnp.zeros((FF_XP_SLOTS, hdr, FF_CH), f32)

    def cols_of(c, part):
        return slice(part * D_FF + c * FF_CH, part * D_FF + (c + 1) * FF_CH)

    def up(c):
        return [_dot(xb, wup_ref[:, cols_of(c, part)]) for part in range(2)]

    def conv_act(c, us):
        halves = []
        for part, u in enumerate(us):
            cols = cols_of(c, part)
            slot = (2 * c + part) % FF_XP_SLOTS
            xp[slot, hdr:hdr + tm, :] = u
            if multi:
                sout_ref[:, cols] = u
                st = st_ref[:, cols]
                p1 = jnp.where(t == 0, _dot01_lhs(sel_p1, st), xp[slot, hdr - 1:hdr - 1 + tm, :])
                p2 = jnp.where(t < 2, _dot01_lhs(sel_p2, st), xp[slot, hdr - 2:hdr - 2 + tm, :])
            else:
                xp[slot, 0:hdr, :] = carry[:, cols]
                p1 = xp[slot, hdr - 1:hdr - 1 + tm, :]
                p2 = xp[slot, hdr - 2:hdr - 2 + tm, :]
                carry[:, cols] = u[tm - hdr:tm, :]
            w = cw_ref[:, cols]
            halves.append(p2 * w[0:1, :] + p1 * w[1:2, :] + u * w[2:3, :] + cb_ref[:, cols])
        return (_silu(halves[0]) * halves[1]).astype(bf16)

    acc = None
    ahead = [up(c) for c in range(min(FF_UP_AHEAD, FF_NCH))]
    pending = None
    for c in range(FF_NCH):
        if c + FF_UP_AHEAD < FF_NCH:
            ahead.append(up(c + FF_UP_AHEAD))
        if pending is not None:
            d = _dot(pending, wdn_ref[(c - 1) * FF_CH:c * FF_CH, :])
            acc = d if acc is None else acc + d
        pending = conv_act(c, ahead.pop(0))
    acc = acc + _dot(pending, wdn_ref[(FF_NCH - 1) * FF_CH:FF_NCH * FF_CH, :])

    if not multi:
        sout_ref[...] = carry[...]
    o_ref[...] = _layer_norm(ALPHA * x + acc, g_ref[...], b_ref[...])


def _ffn(x, st, w, layer, *, groups, tm, seq_len):
    m = x.shape[0]
    tiles = m // (groups * tm)
    multi = seq_len > 0
    kern = functools.partial(_ffn_kernel, tm=tm, seq_len=seq_len)
    once = dict(pipeline_mode=pl.Buffered(1))
    x_spec = pl.BlockSpec((tm, D_MODEL), lambda s, j: (s * tiles + j, 0))
    w_specs = [_layer_spec(layer, D_MODEL, 2 * D_FF, **once), _layer_spec(layer, FFN_CONV, 2 * D_FF, **once),
               _layer_spec(layer, 1, 2 * D_FF, **once), _layer_spec(layer, D_FF, D_MODEL, **once),
               _layer_spec(layer, 1, D_MODEL, **once), _layer_spec(layer, 1, D_MODEL, **once)]
    w_args = (w["wup"], w["fcw"], w["fcb"], w["wdn"], w["ln2_g"], w["ln2_b"])
    xp = pltpu.VMEM((FF_XP_SLOTS, SUBLANES + tm, FF_CH), f32)
    if multi:
        nst = 2 * (tm // seq_len)
        inputs = (x, st) + w_args
        in_specs = [x_spec, pl.BlockSpec((None, nst, 2 * D_FF), lambda s, j: (layer, s * tiles + j, 0))] + w_specs
        sout_shape = jax.ShapeDtypeStruct((m, 2 * D_FF), f32)
        sout_spec = pl.BlockSpec((tm, 2 * D_FF), lambda s, j: (s * tiles + j, 0))
        scratch = [xp]
    else:
        inputs = (x,) + w_args
        in_specs = [x_spec] + w_specs
        sout_shape = jax.ShapeDtypeStruct((groups, SUBLANES, 2 * D_FF), f32)
        sout_spec = pl.BlockSpec((None, SUBLANES, 2 * D_FF), lambda s, j: (s, 0, 0))
        scratch = [xp, pltpu.VMEM((SUBLANES, 2 * D_FF), f32)]
    return pl.pallas_call(
        kern,
        out_shape=(jax.ShapeDtypeStruct((m, D_MODEL), f32), sout_shape),
        grid=(groups, tiles),
        in_specs=in_specs,
        out_specs=(x_spec, sout_spec),
        scratch_shapes=scratch,
        compiler_params=pltpu.CompilerParams(dimension_semantics=("parallel", "arbitrary"),
                                             vmem_limit_bytes=56 * 1024 * 1024),
        name="ffn",
    )(*inputs)


def _pad_lanes(v, off, width=SMALL):
    out = jnp.zeros((v.shape[0], 1, width), f32)
    return out.at[:, 0, off:off + v.shape[1]].set(v.astype(f32))


def _prep_weights(w_in, ssd_conv_w, ssd_conv_b, ssd_dt_bias, ssd_a_log, ssd_d, ssd_norm_w, mlstm_gate_b,
                  mlstm_norm_w, w_branch_a, w_branch_b, w_out, ln1_g, ln1_b, ffn_w_up, ffn_conv_w,
                  ffn_conv_b, ffn_w_down, ln2_g, ln2_b):
    d = D_MODEL
    o_z, o_xbc, o_dt = 0, d, d + d + SSD_BC
    o_q = o_dt + SSD_HEADS
    o_if = o_q + 3 * d
    o_o = o_if + 2 * MLSTM_HEADS
    o_g = o_o + d
    w_t = jnp.swapaxes(w_in, 1, 2)
    cols = lambda a, n: w_t[:, a:a + n, :]
    zeros = lambda n: jnp.zeros((DEPTH, n, d), w_in.dtype)
    w32 = jnp.concatenate([cols(o_z, d), cols(o_o, d), cols(o_g, 2 * d), cols(o_xbc, d + SSD_BC),
                           cols(o_dt, SSD_HEADS), cols(o_if, 2 * MLSTM_HEADS),
                           zeros(P32_W - P32_SM_OFF - SSD_HEADS - 2 * MLSTM_HEADS)], axis=1).astype(bf16)
    e = (np.arange(SSD_HP)[None, :] // SSD_HEAD_DIM == np.arange(LANES)[:, None])
    bd = ((np.arange(SSD_GROUPS * SSD_STATE)[:, None] < SSD_STATE)
          == (np.arange(SSD_HP)[None, :] < SSD_HP // SSD_GROUPS))
    row = lambda a: a[:, None, :]
    return dict(
        w32=w32, wqkv=cols(o_q, 3 * d).astype(bf16),
        cwx=ssd_conv_w[:, :, :d], cbx=row(ssd_conv_b[:, :d]),
        cwb=ssd_conv_w[:, :, d:], cbb=row(ssd_conv_b[:, d:]),
        dtb=_pad_lanes(ssd_dt_bias, DT_OFF), alog=_pad_lanes(ssd_a_log, DT_OFF),
        dexp=row(jnp.repeat(ssd_d.astype(f32), SSD_HEAD_DIM, axis=1)), ssd_nw=row(ssd_norm_w),
        e=jnp.asarray(e, bf16), bd=jnp.asarray(bd, f32),
        gate_b=_pad_lanes(mlstm_gate_b, I_OFF), mlstm_nw=row(mlstm_norm_w),
        wa=w_branch_a.astype(bf16), wb=w_branch_b.astype(bf16), wo=w_out.astype(bf16),
        ln1_g=row(ln1_g), ln1_b=row(ln1_b),
        wup=ffn_w_up.astype(bf16), fcw=ffn_conv_w, fcb=row(ffn_conv_b), wdn=ffn_w_down.astype(bf16),
        ln2_g=row(ln2_g), ln2_b=row(ln2_b),
    )


class _Group:
    def __init__(self, batch, length, q, lr, gs, ssd_cps, mlstm_cps, proj_tm, merge_tm, ffn):
        self.batch, self.length, self.q, self.lr, self.gs = batch, length, q, lr, gs
        self.ssd_cps, self.mlstm_cps = ssd_cps, mlstm_cps
        self.rows = batch * length
        self.proj_tm, self.merge_tm, self.ffn = proj_tm, merge_tm, ffn

    def view(self, a):
        return a.reshape(self.batch, self.length, a.shape[-1])

    def tiling(self, cps):
        rows = cps * self.q
        gs = self.gs
        return self.length // rows, lambda width, blk: pl.BlockSpec((gs, rows, width), lambda b, c: (b, c, blk))


def _ssd(grp, p32, state, w, layer, prev):
    q, b, gs = grp.q, grp.batch, grp.gs
    has_state = state is not None
    steps, tile = grp.tiling(grp.ssd_cps)
    kern = functools.partial(_ssd_kernel, q=q, lr=grp.lr, nc=steps, has_state=has_state, gs=gs, cps=grp.ssd_cps)
    pv = grp.view(p32)
    inputs = [pv, pv, pv, pv]
    in_specs = [tile(D_MODEL, P32_Z), tile(D_MODEL, P32_XS),
                tile(SSD_BC, P32_BC_OFF // SSD_BC), tile(SMALL, P32_SM_OFF // SMALL)]
    if has_state:
        inputs += [state["csx"], state["csb"], state["h"]]
        in_specs += [_seq_spec(layer, gs, SSD_CONV - 1, D_MODEL), _seq_spec(layer, gs, SSD_CONV - 1, SSD_BC),
                     _seq_spec(layer, gs, SSD_HP, SSD_STATE)]
    inputs += [w["cwx"], w["cbx"], w["cwb"], w["cbb"], w["dtb"], w["alog"], w["dexp"], w["ssd_nw"], w["e"],
               w["bd"]]
    const = lambda *shape: pl.BlockSpec(shape, lambda b, c: (0,) * len(shape))
    in_specs += [_layer_spec(layer, SSD_CONV, D_MODEL), _layer_spec(layer, 1, D_MODEL),
                 _layer_spec(layer, SSD_CONV, SSD_BC), _layer_spec(layer, 1, SSD_BC),
                 _layer_spec(layer, 1, SMALL), _layer_spec(layer, 1, SMALL), _layer_spec(layer, 1, D_MODEL),
                 _layer_spec(layer, 1, D_MODEL), const(LANES, SSD_HP), const(SSD_GROUPS * SSD_STATE, SSD_HP)]
    return _stacked_call(
        kern, name="ssd", grid=(b // gs, steps), inputs=inputs, in_specs=in_specs,
        out_shape=(jax.ShapeDtypeStruct((b, grp.length, D_MODEL), bf16),
                   jax.ShapeDtypeStruct((DEPTH, b, SSD_CONV - 1, D_MODEL), f32),
                   jax.ShapeDtypeStruct((DEPTH, b, SSD_CONV - 1, SSD_BC), f32),
                   jax.ShapeDtypeStruct((DEPTH, b, SSD_HP, SSD_STATE), f32)),
        out_specs=(tile(D_MODEL, 0), _seq_spec(layer, gs, SSD_CONV - 1, D_MODEL),
                   _seq_spec(layer, gs, SSD_CONV - 1, SSD_BC), _seq_spec(layer, gs, SSD_HP, SSD_STATE)),
        stacked={1: prev and prev[0], 2: prev and prev[1], 3: prev and prev[2]},
        scratch_shapes=[pltpu.VMEM((gs, SUBLANES + q, D_MODEL), f32), pltpu.VMEM((gs, SUBLANES + q, SSD_BC), f32),
                        pltpu.VMEM((gs, SSD_GROUPS * SSD_STATE, SSD_HP), f32), pltpu.VMEM((gs, q, D_MODEL), f32)],
        dimension_semantics=("parallel", "arbitrary"))


def _mlstm(grp, qkv, p32, state, w, layer, prev):
    q, b, gs = grp.q, grp.batch, grp.gs
    has_state = state is not None
    steps, tile = grp.tiling(grp.mlstm_cps)
    kern = functools.partial(_mlstm_kernel, q=q, lr=grp.lr, nc=steps, has_state=has_state, gs=gs,
                             cps=grp.mlstm_cps)
    hd = MLSTM_HEAD_DIM
    carried = not (has_state and steps == 1 and grp.mlstm_cps == 1)
    qv, pv = grp.view(qkv), grp.view(p32)
    inputs = [qv, qv, qv, pv, pv]
    in_specs = [tile(D_MODEL, 0), tile(D_MODEL, 1), tile(D_MODEL, 2), tile(D_MODEL, P32_O),
                tile(SMALL, P32_SM_OFF // SMALL)]
    if has_state:
        inputs += [state["c"], state["n"], state["m"]]
        in_specs += [_seq_spec(layer, gs, MLSTM_HEADS, hd, hd), _seq_spec(layer, gs, MLSTM_HEADS, hd),
                     _seq_spec(layer, gs, 1, SMALL)]
    inputs += [w["gate_b"], w["mlstm_nw"]]
    in_specs += [_layer_spec(layer, 1, SMALL), _layer_spec(layer, 1, D_MODEL)]
    return _stacked_call(
        kern, name="mlstm", grid=(b // gs, steps), inputs=inputs, in_specs=in_specs,
        out_shape=(jax.ShapeDtypeStruct((b, grp.length, D_MODEL), bf16),
                   jax.ShapeDtypeStruct((DEPTH, b, MLSTM_HEADS, hd, hd), f32),
                   jax.ShapeDtypeStruct((DEPTH, b, MLSTM_HEADS, hd), f32),
                   jax.ShapeDtypeStruct((DEPTH, b, 1, SMALL), f32)),
        out_specs=(tile(D_MODEL, 0), _seq_spec(layer, gs, MLSTM_HEADS, hd, hd),
                   _seq_spec(layer, gs, MLSTM_HEADS, hd), _seq_spec(layer, gs, 1, SMALL)),
        stacked={1: prev and prev[0], 2: prev and prev[1], 3: prev and prev[2]},
        scratch_shapes=[pltpu.VMEM((gs, MLSTM_HEADS, hd, hd) if carried else (gs, 1, SUBLANES, LANES), f32),
                        pltpu.VMEM((gs, MLSTM_HEADS, hd), f32), pltpu.VMEM((gs, 1, SMALL), f32)],
        dimension_semantics=("parallel", "arbitrary"))


def _trunk(grp, x, state, w):
    ssd_out = mlstm_out = None
    ffn_out = []
    for layer in range(DEPTH):
        p32 = _proj(x, w["w32"], layer, f32, grp.proj_tm)
        qkv = _proj(x, w["wqkv"], layer, bf16, grp.proj_tm)
        ys, *ssd_out = _ssd(grp, p32, state, w, layer, ssd_out)
        hm, *mlstm_out = _mlstm(grp, qkv, p32, state, w, layer, mlstm_out)
        x1 = _merge(ys.reshape(grp.rows, D_MODEL), hm.reshape(grp.rows, D_MODEL), p32, x, w, layer,
                    grp.merge_tm)
        x, s_ffn = _ffn(x1, state["ffn"] if state is not None else None, w, layer, **grp.ffn)
        ffn_out.append(s_ffn)
    return x, ssd_out, mlstm_out, ffn_out


def _unpack_states(batch, ssd_out, mlstm_out):
    csx, csb, h = ssd_out
    c, n, m = mlstm_out
    return (h.reshape(DEPTH, batch, SSD_HEADS, SSD_HEAD_DIM, SSD_STATE),
            jnp.concatenate([csx, csb], axis=-1), c, n, m[:, :, 0, :MLSTM_HEADS])


def kernel(x_prompt, x_sample, state_ssd, state_ssd_conv, state_mlstm_c, state_mlstm_n, state_mlstm_m,
           state_ffn_conv, w_in, ssd_conv_w, ssd_conv_b, ssd_dt_bias, ssd_a_log, ssd_d, ssd_norm_w,
           mlstm_gate_b, mlstm_norm_w, w_branch_a, w_branch_b, w_out, ln1_g, ln1_b, ffn_w_up, ffn_conv_w,
           ffn_conv_b, ffn_w_down, ln2_g, ln2_b):
    w = _prep_weights(w_in, ssd_conv_w, ssd_conv_b, ssd_dt_bias, ssd_a_log, ssd_d, ssd_norm_w, mlstm_gate_b,
                      mlstm_norm_w, w_branch_a, w_branch_b, w_out, ln1_g, ln1_b, ffn_w_up, ffn_conv_w,
                      ffn_conv_b, ffn_w_down, ln2_g, ln2_b)
    keep = FFN_CONV - 1

    bp, lp, _ = x_prompt.shape
    prompt = _Group(bp, lp, CHUNK, CHUNK, gs=1, ssd_cps=4, mlstm_cps=4, proj_tm=2048, merge_tm=512,
                    ffn=dict(groups=bp, tm=512, seq_len=0))
    y_p, ssd_p, mlstm_p, ffn_p = _trunk(prompt, x_prompt.reshape(bp * lp, D_MODEL), None, w)
    st_p = _unpack_states(bp, ssd_p, mlstm_p)
    ffn_conv_p = jnp.stack(ffn_p)[:, :, SUBLANES - keep:, :]

    bs, ls, _ = x_sample.shape
    lpad = max(ls, SAMPLE_PAD_LEN)
    s_rows = bs * lpad
    sample = _Group(bs, lpad, lpad, ls, gs=8, ssd_cps=1, mlstm_cps=1, proj_tm=s_rows, merge_tm=512,
                    ffn=dict(groups=1, tm=256, seq_len=lpad))
    s_state = dict(
        csx=state_ssd_conv[..., :D_MODEL], csb=state_ssd_conv[..., D_MODEL:],
        h=state_ssd.reshape(DEPTH, bs, SSD_HP, SSD_STATE),
        c=state_mlstm_c, n=state_mlstm_n,
        m=jnp.pad(state_mlstm_m, ((0, 0), (0, 0), (0, SMALL - MLSTM_HEADS)))[:, :, None, :],
        ffn=state_ffn_conv.reshape(DEPTH, bs * keep, 2 * D_FF),
    )
    xs = jnp.pad(x_sample, ((0, 0), (0, lpad - ls), (0, 0))).reshape(s_rows, D_MODEL)
    y_s, ssd_s, mlstm_s, ffn_s = _trunk(sample, xs, s_state, w)
    st_s = _unpack_states(bs, ssd_s, mlstm_s)
    ffn_conv_s = jnp.stack([u.reshape(bs, lpad, 2 * D_FF)[:, ls - keep:ls, :] for u in ffn_s])
    y_sample = y_s.reshape(bs, lpad, D_MODEL)[:, :ls, :]

    return (y_p.reshape(bp, lp, D_MODEL), y_sample, st_p[0], st_s[0], st_p[1], st_s[1], st_p[2], st_s[2],
            st_p[3], st_s[3], st_p[4], st_s[4], ffn_conv_p, ffn_conv_s)
```

```python
import functools
import itertools

import jax
import jax.numpy as jnp
import numpy as np
from jax import lax
from jax.experimental import pallas as pl
from jax.experimental.pallas import tpu as pltpu

f32 = jnp.float32
bf16 = jnp.bfloat16

D_MODEL = 1024
DEPTH = 2
SSD_HEADS = 16
SSD_HEAD_DIM = 64
SSD_STATE = 64
SSD_GROUPS = 2
SSD_CONV = 4
SSD_BC = 2 * SSD_GROUPS * SSD_STATE
SSD_HP = SSD_HEADS * SSD_HEAD_DIM
MLSTM_HEADS = 4
MLSTM_HEAD_DIM = 256
CHUNK = 128
D_FF = 2816
FFN_CONV = 3
ALPHA = (2 * DEPTH) ** 0.25
EPS = 1e-5

LANES = 128
SUBLANES = 8
SMALL = LANES
DT_OFF, I_OFF, F_OFF = 0, 16, 20
P32_Z, P32_O, P32_GA, P32_GB, P32_XS = 0, 1, 2, 3, 4
P32_BC_OFF = 5 * D_MODEL
P32_SM_OFF = P32_BC_OFF + SSD_BC
P32_W = P32_SM_OFF + 2 * SMALL
FF_CH = 256
FF_NCH = D_FF // FF_CH
MERGE_PARTS = 2
FF_UP_AHEAD = 2
FF_XP_SLOTS = 4
NEG_BIG = -1e30
SAMPLE_PAD_LEN = 4

NT_DIMS = (((1,), (1,)), ((), ()))
TN_DIMS = (((0,), (0,)), ((), ()))


def _dot(a, b):
    return jnp.dot(a, b, preferred_element_type=f32)


def _split3(x):
    hi = x.astype(bf16)
    r = x - hi.astype(f32)
    mid = r.astype(bf16)
    lo = (r - mid.astype(f32)).astype(bf16)
    return hi, mid, lo


def _dot01_rhs(x, e):
    hi, mid, lo = _split3(x)
    return _dot(hi, e) + _dot(mid, e) + _dot(lo, e)


def _dot01_lhs(t, x):
    hi, mid, lo = _split3(x)
    return _dot(t, hi) + _dot(t, mid) + _dot(t, lo)


def _softplus(x):
    return jnp.maximum(x, 0.0) + jnp.log1p(jnp.exp(-jnp.abs(x)))


def _silu(x):
    return x * jax.nn.sigmoid(x)


def _tri(q):
    row = lax.broadcasted_iota(jnp.int32, (q, q), 0)
    col = lax.broadcasted_iota(jnp.int32, (q, q), 1)
    return row >= col


def _valid_rows(q, width, lr, is_last):
    row = lax.broadcasted_iota(jnp.int32, (q, width), 0)
    return row < jnp.where(is_last, lr, q)


def _layer_norm(r, g, b):
    mu = jnp.mean(r, axis=-1, keepdims=True)
    var = jnp.mean(jnp.square(r - mu), axis=-1, keepdims=True)
    return (r - mu) * lax.rsqrt(var + EPS) * g + b


def _layer_spec(layer, *shape, **kw):
    zeros = (0,) * len(shape)
    return pl.BlockSpec((None,) + shape, lambda *_: (layer,) + zeros, **kw)


def _seq_spec(layer, gs, *shape):
    zeros = (0,) * len(shape)
    return pl.BlockSpec((None, gs) + shape, lambda b, c: (layer, b) + zeros)


def _stacked_call(kern, *, name, grid, inputs, in_specs, out_shape, out_specs, stacked, scratch_shapes,
                  dimension_semantics, vmem_limit_bytes=None):
    prev = [(i, a) for i, a in sorted(stacked.items()) if a is not None]
    n_in = len(inputs)

    def body(*refs):
        kern(*refs[:n_in], *refs[n_in + len(prev):])

    return pl.pallas_call(
        body,
        out_shape=out_shape,
        grid=grid,
        in_specs=list(in_specs) + [pl.BlockSpec(memory_space=pl.ANY)] * len(prev),
        out_specs=out_specs,
        scratch_shapes=scratch_shapes,
        input_output_aliases={n_in + k: i for k, (i, _) in enumerate(prev)},
        compiler_params=pltpu.CompilerParams(dimension_semantics=dimension_semantics,
                                             vmem_limit_bytes=vmem_limit_bytes),
        name=name,
    )(*inputs, *[a for _, a in prev])


def _proj_kernel(x_ref, w_ref, o_ref, xb):
    @pl.when(pl.program_id(1) == 0)
    def _():
        xb[...] = x_ref[...].astype(bf16)

    o_ref[...] = lax.dot_general(xb[...], w_ref[...], NT_DIMS, preferred_element_type=f32).astype(o_ref.dtype)


def _proj(x, wt, layer, out_dtype, tm, tn):
    m, k = x.shape
    n = wt.shape[1]
    return pl.pallas_call(
        _proj_kernel,
        out_shape=jax.ShapeDtypeStruct((m, n), out_dtype),
        grid=(m // tm, n // tn),
        in_specs=[pl.BlockSpec((tm, k), lambda i, j: (i, 0)),
                  pl.BlockSpec((None, tn, k), lambda i, j: (layer, j, 0))],
        out_specs=pl.BlockSpec((tm, tn), lambda i, j: (i, j)),
        scratch_shapes=[pltpu.VMEM((tm, k), bf16)],
        compiler_params=pltpu.CompilerParams(dimension_semantics=("parallel", "arbitrary")),
        name="proj",
    )(x, wt)


class _Rows:
    def __init__(self, ref, start, n):
        self.ref, self.start, self.n, self.dtype = ref, start, n, ref.dtype

    def rows(self, off, n):
        return _Rows(self.ref, self.start + off, n)

    def _index(self, idx):
        cols = slice(None) if idx is Ellipsis else idx[1]
        return (slice(self.start, self.start + self.n), cols)

    def __getitem__(self, idx):
        return self.ref[self._index(idx)]

    def __setitem__(self, idx, value):
        self.ref[self._index(idx)] = value


def _per_sequence(seq_fn, refs, n_tile, n_state, n_param, gs, has_state, nc, rows):
    n_state = n_state if has_state else 0
    tiles, refs = refs[:n_tile], refs[n_tile:]
    state, refs = refs[:n_state], refs[n_state:]
    params, (y_tile, *rest) = refs[:n_param], refs[n_param:]
    phases = []
    for g in range(gs):
        at = lambda group: tuple(r.at[g] for r in group)
        seq_rows = lambda group: tuple(_Rows(r, g * rows, rows) for r in group)
        phases.append(seq_fn(*seq_rows(tiles), *at(state), *params, *seq_rows((y_tile,)), *at(rest)))
    c = pl.program_id(1)

    @pl.when(c == 0)
    def _():
        for init, _, _ in phases:
            init()

    for _ in itertools.zip_longest(*[body() for _, body, _ in phases]):
        pass

    @pl.when(c == nc - 1)
    def _():
        for _, _, final in phases:
            final()


def _ssd_kernel(*refs, q, lr, nc, has_state, gs, cps):
    seq = functools.partial(_ssd_seq, q=q, lr=lr, nc=nc, has_state=has_state, cps=cps)
    _per_sequence(seq, refs, 4, 3, 10, gs, has_state, nc, cps * q)


def _ssd_seq(*refs, q, lr, nc, has_state, cps):
    z_ref, xs_ref, bc_ref, sm_ref = refs[:4]
    refs = refs[4:]
    if has_state:
        csx_ref, csb_ref, h0_ref = refs[:3]
        refs = refs[3:]
    (cwx_ref, cbx_ref, cwb_ref, cbb_ref, dtb_ref, alog_ref, dexp_ref, nw_ref, e_ref, bd_ref,
     y_ref, ncsx_ref, ncsb_ref, hout_ref, xpx, xpb, ht, yb) = refs
    hdr = SUBLANES
    lo = hdr - (SSD_CONV - 1)
    n2 = SSD_GROUPS * SSD_STATE
    assert lr >= SSD_CONV - 1

    def init():
        if has_state:
            h_t = h0_ref[...].T
            ht[...] = jnp.where(bd_ref[...] > 0.5, jnp.concatenate([h_t, h_t], axis=0), 0.0)
            xpx[lo:hdr, :] = csx_ref[...]
            xpb[lo:hdr, :] = csb_ref[...]
        else:
            ht[...] = jnp.zeros_like(ht)
            xpx[lo:hdr, :] = jnp.zeros((SSD_CONV - 1, SSD_HP), f32)
            xpb[lo:hdr, :] = jnp.zeros((SSD_CONV - 1, SSD_BC), f32)

    def final():
        ncsx_ref[...] = xpx[lo + lr:hdr + lr, :]
        ncsb_ref[...] = xpb[lo + lr:hdr + lr, :]
        h_new = ht[...]
        hout_ref[...] = (h_new[:SSD_STATE, :] + h_new[SSD_STATE:, :]).T

    def body():
        for k in range(cps):
            sub = lambda r: r.rows(k * q, q)
            is_last = (pl.program_id(1) == nc - 1) if k == cps - 1 else False
            yield from _ssd_body(sub(z_ref), sub(xs_ref), sub(bc_ref), sub(sm_ref), cwx_ref, cbx_ref, cwb_ref,
                                 cbb_ref, dtb_ref, alog_ref, dexp_ref, nw_ref, e_ref, bd_ref, sub(y_ref),
                                 xpx, xpb, ht, yb, q=q, lr=lr, is_last=is_last)

    return init, body, final


def _ssd_body(z_ref, xs_ref, bc_ref, sm_ref, cwx_ref, cbx_ref, cwb_ref, cbb_ref, dtb_ref, alog_ref,
              dexp_ref, nw_ref, e_ref, bd_ref, y_ref, xpx, xpb, ht, yb, *, q, lr, is_last):
    hdr = SUBLANES
    lo = hdr - (SSD_CONV - 1)
    n2 = SSD_GROUPS * SSD_STATE
    block_diag = bd_ref[...] > 0.5

    dt = _softplus(sm_ref[...] + dtb_ref[...])
    if lr < q:
        dt = jnp.where(_valid_rows(q, SMALL, lr, is_last), dt, 0.0)
    a = -jnp.exp(alog_ref[...])
    d_a = dt * a
    causal = _tri(q)
    tril = jnp.where(causal, 1.0, 0.0).astype(bf16)
    e = e_ref[...]
    acs = _dot01_lhs(tril, d_a)
    dt_x = _dot01_rhs(dt, e)
    yield

    xpx[hdr:hdr + q, :] = xs_ref[...]
    xpb[hdr:hdr + q, :] = bc_ref[...]

    def conv(xp, w_ref, b_ref):
        w = w_ref[...]
        acc = xp[lo:lo + q, :] * w[0:1, :]
        for j in range(1, SSD_CONV):
            acc = acc + xp[lo + j:lo + j + q, :] * w[j:j + 1, :]
        return acc + b_ref[...]

    cb = conv(xpb, cwb_ref, cbb_ref)
    bcv = _silu(cb)
    bm = bcv[:, :n2].astype(bf16)
    cm = bcv[:, n2:]
    lane_g0 = lax.broadcasted_iota(jnp.int32, (q, n2), 1) < SSD_STATE
    acs_t = acs.T
    acs_x = _dot01_rhs(acs, e)
    yield
    cbms = [lax.dot_general(jnp.where(lane_g0 if g == 0 else jnp.logical_not(lane_g0), cm, 0.0).astype(bf16),
                            bm, NT_DIMS, preferred_element_type=f32) for g in range(SSD_GROUPS)]
    h_prev = ht[...]
    y_off = _dot(cm.astype(bf16), h_prev.astype(bf16))
    cx = conv(xpx, cwx_ref, cbx_ref)
    tail_x = xpx[lo + q:hdr + q, :]
    tail_b = xpb[lo + q:hdr + q, :]
    xpx[lo:hdr, :] = tail_x
    xpb[lo:hdr, :] = tail_b
    yield
    xs = _silu(cx)
    last_x = acs_x[q - 1:q, :]
    xdt = xs * dt_x
    xdt_b = xdt.astype(bf16)
    yield
    lane_lo = lax.broadcasted_iota(jnp.int32, (q, LANES), 1) < SSD_HEAD_DIM
    heads_per_group = SSD_HEADS // SSD_GROUPS
    decays = [jnp.exp(jnp.where(causal, acs[:, hh:hh + 1] - acs_t[hh:hh + 1, :], -jnp.inf))
              for hh in range(SSD_HEADS)]
    yield
    weights = [(cbms[hh // heads_per_group] * decays[hh]).astype(bf16) for hh in range(SSD_HEADS)]
    xdtw = (xdt * jnp.exp(last_x - acs_x)).astype(bf16)
    yield
    ys = [_dot(weights[hh], xdt_b[:, (hh // 2) * LANES:(hh // 2 + 1) * LANES]) for hh in range(SSD_HEADS)]
    upd = lax.dot_general(bm, xdtw, TN_DIMS, preferred_element_type=f32)
    yield
    for p in range(SSD_HEADS // 2):
        yb[:, p * LANES:(p + 1) * LANES] = jnp.where(lane_lo, ys[2 * p], ys[2 * p + 1])
    ht[...] = jnp.exp(last_x) * h_prev + jnp.where(block_diag, upd, 0.0)
    yield
    y = yb[...] + y_off * jnp.exp(acs_x) + dexp_ref[...] * xs
    y = y * _silu(z_ref[...])
    yield
    y = y * lax.rsqrt(jnp.mean(jnp.square(y), axis=-1, keepdims=True) + EPS) * nw_ref[...]
    y_ref[...] = y.astype(y_ref.dtype)


def _mlstm_kernel(*refs, q, lr, nc, has_state, gs, cps):
    seq = functools.partial(_mlstm_seq, q=q, lr=lr, nc=nc, has_state=has_state, cps=cps)
    _per_sequence(seq, refs, 5, 3, 2, gs, has_state, nc, cps * q)


def _mlstm_seq(*refs, q, lr, nc, has_state, cps):
    q_ref, k_ref, v_ref, o_ref, sm_ref = refs[:5]
    refs = refs[5:]
    if has_state:
        c0_ref, n0_ref, m0_ref = refs[:3]
        refs = refs[3:]
    gb_ref, nw_ref, h_ref, cout_ref, nout_ref, mout_ref, cs, ns, ms = refs
    direct = has_state and nc == 1 and cps == 1

    def init():
        if direct:
            return
        if has_state:
            cs[...] = c0_ref[...]
            ns[...] = n0_ref[...]
            ms[...] = m0_ref[...]
        else:
            cs[...] = jnp.zeros_like(cs)
            ns[...] = jnp.zeros_like(ns)
            ms[...] = jnp.zeros_like(ms)

    def final():
        if direct:
            return
        cout_ref[...] = cs[...]
        nout_ref[...] = ns[...]
        mout_ref[...] = ms[...]

    def body():
        chunks = []
        for k in range(cps):
            sub = lambda r, k=k: r.rows(k * q, q)
            is_last = (pl.program_id(1) == nc - 1) if k == cps - 1 else False
            src = (c0_ref, n0_ref, m0_ref) if direct else (cs, ns, ms)
            dst = (cout_ref, nout_ref, mout_ref) if direct else (cs, ns, ms)
            chunks.append(_mlstm_body(sub(q_ref), sub(k_ref), sub(v_ref), sub(o_ref), sub(sm_ref), gb_ref, nw_ref,
                                      sub(h_ref), src, dst, q=q, lr=lr, is_last=is_last))
        yield from _staggered(chunks, MLSTM_STATE_STAGES)

    return init, body, final


MLSTM_STATE_STAGES = 8


def _staggered(gens, skew):
    done = [False] * len(gens)
    t = 0
    while not all(done):
        for i, g in enumerate(gens):
            if done[i] or t < i * skew:
                continue
            try:
                next(g)
            except StopIteration:
                done[i] = True
        t += 1
        yield


def _mlstm_body(q_ref, k_ref, v_ref, o_ref, sm_ref, gb_ref, nw_ref, h_ref, src, dst, *, q, lr, is_last):
    c_src, n_src, m_src = src
    c_dst, n_dst, m_dst = dst
    sm = sm_ref[...] + gb_ref[...]
    logf = -_softplus(-sm)
    ipre = sm
    if lr < q:
        valid = _valid_rows(q, SMALL, lr, is_last)
        logf = jnp.where(valid, logf, 0.0)
        ipre = jnp.where(valid, ipre, NEG_BIG)
    causal = _tri(q)
    tril = jnp.where(causal, 1.0, 0.0).astype(bf16)
    yield
    bcum = _dot01_lhs(tril, logf)
    ipre_t = ipre.T
    yield
    bcum_t = bcum.T
    lane = lax.broadcasted_iota(jnp.int32, (1, SMALL), 1)
    k_scale = MLSTM_HEAD_DIM ** -0.5

    heads = range(MLSTM_HEADS)
    sls = [slice(h * MLSTM_HEAD_DIM, (h + 1) * MLSTM_HEAD_DIM) for h in heads]
    q_all, k_all, v_all, o_all = q_ref[...], k_ref[...], v_ref[...], o_ref[...]
    qs = [q_all[:, sl] for sl in sls]
    ks = [k_all[:, sl] * k_scale for sl in sls]
    vs = [v_all[:, sl] for sl in sls]
    b_cols = [bcum[:, F_OFF + h:F_OFF + h + 1] for h in heads]
    i_cols = [ipre[:, I_OFF + h:I_OFF + h + 1] for h in heads]
    dmats = [jnp.where(causal, b_cols[h] - bcum_t[F_OFF + h:F_OFF + h + 1, :] + ipre_t[I_OFF + h:I_OFF + h + 1, :],
                       -jnp.inf) for h in heads]
    yield
    qk = [lax.dot_general(qs[h], ks[h], NT_DIMS, preferred_element_type=f32) for h in heads]
    d_max = [jnp.max(dmats[h], axis=-1, keepdims=True) for h in heads]
    yield
    n_all = n_src[...]
    m_all = m_src[...]
    m_new = m_all
    cs_in = [c_src[h] for h in heads]
    m_prevs = [m_all[:, h:h + 1] for h in heads]
    qc = [_dot(qs[h], cs_in[h].astype(bf16)) for h in heads]
    inters = [b_cols[h] + m_prevs[h] for h in heads]
    m_ts = [jnp.maximum(inters[h], d_max[h]) for h in heads]
    yield
    w_inters = [jnp.exp(inters[h] - m_ts[h]) for h in heads]
    ss = [qk[h] * jnp.exp(dmats[h] - m_ts[h]) for h in heads]
    yield
    sv = [_dot(ss[h].astype(bf16), vs[h]) for h in heads]
    m_ends = [m_ts[h][q - 1:q, :] for h in heads]
    b_lasts = [b_cols[h][q - 1:q, :] for h in heads]
    kws = [ks[h].astype(f32) * jnp.exp(b_lasts[h] - b_cols[h] + i_cols[h] - m_ends[h]) for h in heads]
    yield
    kv = [lax.dot_general(kws[h].astype(bf16), vs[h], TN_DIMS, preferred_element_type=f32) for h in heads]
    qns = [jnp.sum(qs[h].astype(f32) * n_all[h:h + 1, :], axis=-1, keepdims=True) for h in heads]
    yield
    dens = [jnp.sum(ss[h], axis=-1, keepdims=True) + w_inters[h] * qns[h] for h in heads]
    yield
    hvs = [(sv[h] + w_inters[h] * qc[h]) / jnp.maximum(jnp.abs(dens[h]), jnp.exp(-m_ts[h])) for h in heads]
    yield
    rms = [lax.rsqrt(jnp.mean(jnp.square(hvs[h]), axis=-1, keepdims=True) + EPS) for h in heads]
    yield
    h_new = [(hvs[h] * rms[h] * nw_ref[:, sls[h]] * jax.nn.sigmoid(o_all[:, sls[h]])).astype(h_ref.dtype)
             for h in heads]
    w_cs = [jnp.exp(b_lasts[h] + m_prevs[h] - m_ends[h]) for h in heads]
    yield
    for h in heads:
        c_dst[h] = w_cs[h] * cs_in[h] + kv[h]
        m_new = jnp.where(lane == h, m_ends[h], m_new)
    h_ref[...] = jnp.concatenate(h_new, axis=1)
    n_dst[...] = jnp.concatenate(
        [w_cs[h] * n_all[h:h + 1, :] + jnp.sum(kws[h], axis=0, keepdims=True) for h in heads], axis=0)
    m_dst[...] = m_new


def _merge_kernel(ys_ref, hm_ref, ga_ref, gb_ref, x_ref, wa_ref, wb_ref, wo_ref, g_ref, b_ref, o_ref):
    tm = x_ref.shape[0]
    rows = [pl.ds(i * (tm // MERGE_PARTS), tm // MERGE_PARTS) for i in range(MERGE_PARTS)]
    br = [(_dot(ys_ref[r, :], wa_ref[...]), _dot(hm_ref[r, :], wb_ref[...])) for r in rows]
    merged = [(jax.nn.sigmoid(ga_ref[r, :]) * a + jax.nn.sigmoid(gb_ref[r, :]) * b).astype(bf16)
              for r, (a, b) in zip(rows, br)]
    mix = [_dot(m, wo_ref[...]) for m in merged]
    for r, m in zip(rows, mix):
        o_ref[r, :] = _layer_norm(ALPHA * x_ref[r, :] + m, g_ref[...], b_ref[...])


def _merge(ys, hm, p32, x, w, layer, tm):
    m = x.shape[0]
    row = lambda blk: pl.BlockSpec((tm, D_MODEL), lambda i: (i, blk))
    return pl.pallas_call(
        _merge_kernel,
        out_shape=jax.ShapeDtypeStruct((m, D_MODEL), f32),
        grid=(m // tm,),
        in_specs=[row(0), row(0), row(P32_GA), row(P32_GB), row(0),
                  _layer_spec(layer, D_MODEL, D_MODEL), _layer_spec(layer, D_MODEL, D_MODEL),
                  _layer_spec(layer, D_MODEL, D_MODEL), _layer_spec(layer, 1, D_MODEL),
                  _layer_spec(layer, 1, D_MODEL)],
        out_specs=row(0),
        compiler_params=pltpu.CompilerParams(dimension_semantics=("parallel",)),
        name="merge",
    )(ys, hm, p32, p32, x, w["wa"], w["wb"], w["wo"], w["ln1_g"], w["ln1_b"])


def _ffn_kernel(*refs, tm, seq_len):
    multi = seq_len > 0
    if multi:
        x_ref, st_ref, wup_ref, cw_ref, cb_ref, wdn_ref, g_ref, b_ref, o_ref, sout_ref, xp = refs
    else:
        x_ref, wup_ref, cw_ref, cb_ref, wdn_ref, g_ref, b_ref, o_ref, sout_ref, xp, carry = refs
        @pl.when(pl.program_id(1) == 0)
        def _():
            carry[...] = jnp.zeros_like(carry)

    hdr = SUBLANES
    x = x_ref[...]
    xb = x.astype(bf16)
    if multi:
        assert seq_len & (seq_len - 1) == 0
        nseq = tm // seq_len
        t = lax.broadcasted_iota(jnp.int32, (tm, FF_CH), 0) & (seq_len - 1)
        row = lax.broadcasted_iota(jnp.int32, (tm, 2 * nseq), 0)
        col = lax.broadcasted_iota(jnp.int32, (tm, 2 * nseq), 1)
        t_sel = row & (seq_len - 1)
        seq0 = lax.shift_right_logical(row - t_sel, (seq_len // 2).bit_length() - 1)
        sel_p2 = jnp.where(col == seq0 + t_sel, jnp.where(t_sel < 2, 1.0, 0.0), 0.0).astype(bf16)
        sel_p1 = jnp.where(col == seq0 + 1, jnp.where(t_sel == 0, 1.0, 0.0), 0.0).astype(bf16)
        xp[:, 0:hdr, :] = jnp.zeros((FF_XP_SLOTS, hdr, FF_CH), f32)

    def cols_of(c, part):
        return slice(part * D_FF + c * FF_CH, part * D_FF + (c + 1) * FF_CH)

    def up(c):
        return [_dot(xb, wup_ref[:, cols_of(c, part)]) for part in range(2)]

    def conv_act(c, us):
        halves = []
        for part, u in enumerate(us):
            cols = cols_of(c, part)
            slot = (2 * c + part) % FF_XP_SLOTS
            xp[slot, hdr:hdr + tm, :] = u
            if multi:
                sout_ref[:, cols] = u
                st = st_ref[:, cols]
                p1 = jnp.where(t == 0, _dot01_lhs(sel_p1, st), xp[slot, hdr - 1:hdr - 1 + tm, :])
                p2 = jnp.where(t < 2, _dot01_lhs(sel_p2, st), xp[slot, hdr - 2:hdr - 2 + tm, :])
            else:
                xp[slot, 0:hdr, :] = carry[:, cols]
                p1 = xp[slot, hdr - 1:hdr - 1 + tm, :]
                p2 = xp[slot, hdr - 2:hdr - 2 + tm, :]
                carry[:, cols] = u[tm - hdr:tm, :]
            w = cw_ref[:, cols]
            halves.append(p2 * w[0:1, :] + p1 * w[1:2, :] + u * w[2:3, :] + cb_ref[:, cols])
        return (_silu(halves[0]) * halves[1]).astype(bf16)

    acc = None
    ahead = [up(c) for c in range(min(FF_UP_AHEAD, FF_NCH))]
    pending = None
    for c in range(FF_NCH):
        if c + FF_UP_AHEAD < FF_NCH:
            ahead.append(up(c + FF_UP_AHEAD))
        if pending is not None:
            d = _dot(pending, wdn_ref[(c - 1) * FF_CH:c * FF_CH, :])
            acc = d if acc is None else acc + d
        pending = conv_act(c, ahead.pop(0))
    acc = acc + _dot(pending, wdn_ref[(FF_NCH - 1) * FF_CH:FF_NCH * FF_CH, :])

    if not multi:
        sout_ref[...] = carry[...]
    o_ref[...] = _layer_norm(ALPHA * x + acc, g_ref[...], b_ref[...])


def _ffn(x, st, w, layer, *, groups, tm, seq_len):
    m = x.shape[0]
    tiles = m // (groups * tm)
    multi = seq_len > 0
    kern = functools.partial(_ffn_kernel, tm=tm, seq_len=seq_len)
    once = dict(pipeline_mode=pl.Buffered(1))
    x_spec = pl.BlockSpec((tm, D_MODEL), lambda s, j: (s * tiles + j, 0))
    w_specs = [_layer_spec(layer, D_MODEL, 2 * D_FF, **once), _layer_spec(layer, FFN_CONV, 2 * D_FF, **once),
               _layer_spec(layer, 1, 2 * D_FF, **once), _layer_spec(layer, D_FF, D_MODEL, **once),
               _layer_spec(layer, 1, D_MODEL, **once), _layer_spec(layer, 1, D_MODEL, **once)]
    w_args = (w["wup"], w["fcw"], w["fcb"], w["wdn"], w["ln2_g"], w["ln2_b"])
    xp = pltpu.VMEM((FF_XP_SLOTS, SUBLANES + tm, FF_CH), f32)
    if multi:
        nst = 2 * (tm // seq_len)
        inputs = (x, st) + w_args
        in_specs = [x_spec, pl.BlockSpec((None, nst, 2 * D_FF), lambda s, j: (layer, s * tiles + j, 0))] + w_specs
        sout_shape = jax.ShapeDtypeStruct((m, 2 * D_FF), f32)
        sout_spec = pl.BlockSpec((tm, 2 * D_FF), lambda s, j: (s * tiles + j, 0))
        scratch = [xp]
    else:
        inputs = (x,) + w_args
        in_specs = [x_spec] + w_specs
        sout_shape = jax.ShapeDtypeStruct((groups, SUBLANES, 2 * D_FF), f32)
        sout_spec = pl.BlockSpec((None, SUBLANES, 2 * D_FF), lambda s, j: (s, 0, 0))
        scratch = [xp, pltpu.VMEM((SUBLANES, 2 * D_FF), f32)]
    return pl.pallas_call(
        kern,
        out_shape=(jax.ShapeDtypeStruct((m, D_MODEL), f32), sout_shape),
        grid=(groups, tiles),
        in_specs=in_specs,
        out_specs=(x_spec, sout_spec),
        scratch_shapes=scratch,
        compiler_params=pltpu.CompilerParams(dimension_semantics=("parallel", "arbitrary"),
                                             vmem_limit_bytes=56 * 1024 * 1024),
        name="ffn",
    )(*inputs)


def _pad_lanes(v, off, width=SMALL):
    out = jnp.zeros((v.shape[0], 1, width), f32)
    return out.at[:, 0, off:off + v.shape[1]].set(v.astype(f32))


def _prep_weights(w_in, ssd_conv_w, ssd_conv_b, ssd_dt_bias, ssd_a_log, ssd_d, ssd_norm_w, mlstm_gate_b,
                  mlstm_norm_w, w_branch_a, w_branch_b, w_out, ln1_g, ln1_b, ffn_w_up, ffn_conv_w,
                  ffn_conv_b, ffn_w_down, ln2_g, ln2_b):
    d = D_MODEL
    o_z, o_xbc, o_dt = 0, d, d + d + SSD_BC
    o_q = o_dt + SSD_HEADS
    o_if = o_q + 3 * d
    o_o = o_if + 2 * MLSTM_HEADS
    o_g = o_o + d
    w_t = jnp.swapaxes(w_in, 1, 2)
    cols = lambda a, n: w_t[:, a:a + n, :]
    zeros = lambda n: jnp.zeros((DEPTH, n, d), w_in.dtype)
    w32 = jnp.concatenate([cols(o_z, d), cols(o_o, d), cols(o_g, 2 * d), cols(o_xbc, d + SSD_BC),
                           cols(o_dt, SSD_HEADS), cols(o_if, 2 * MLSTM_HEADS),
                           zeros(P32_W - P32_SM_OFF - SSD_HEADS - 2 * MLSTM_HEADS)], axis=1).astype(bf16)
    e = (np.arange(SSD_HP)[None, :] // SSD_HEAD_DIM == np.arange(LANES)[:, None])
    bd = ((np.arange(SSD_GROUPS * SSD_STATE)[:, None] < SSD_STATE)
          == (np.arange(SSD_HP)[None, :] < SSD_HP // SSD_GROUPS))
    row = lambda a: a[:, None, :]
    return dict(
        w32=w32, wqkv=cols(o_q, 3 * d).astype(bf16),
        cwx=ssd_conv_w[:, :, :d], cbx=row(ssd_conv_b[:, :d]),
        cwb=ssd_conv_w[:, :, d:], cbb=row(ssd_conv_b[:, d:]),
        dtb=_pad_lanes(ssd_dt_bias, DT_OFF), alog=_pad_lanes(ssd_a_log, DT_OFF),
        dexp=row(jnp.repeat(ssd_d.astype(f32), SSD_HEAD_DIM, axis=1)), ssd_nw=row(ssd_norm_w),
        e=jnp.asarray(e, bf16), bd=jnp.asarray(bd, f32),
        gate_b=_pad_lanes(mlstm_gate_b, I_OFF), mlstm_nw=row(mlstm_norm_w),
        wa=w_branch_a.astype(bf16), wb=w_branch_b.astype(bf16), wo=w_out.astype(bf16),
        ln1_g=row(ln1_g), ln1_b=row(ln1_b),
        wup=ffn_w_up.astype(bf16), fcw=ffn_conv_w, fcb=row(ffn_conv_b), wdn=ffn_w_down.astype(bf16),
        ln2_g=row(ln2_g), ln2_b=row(ln2_b),
    )


class _Group:
    def __init__(self, batch, length, q, lr, gs, ssd_cps, mlstm_cps, proj_tm, p32_tn, qkv_tn, merge_tm, ffn):
        self.batch, self.length, self.q, self.lr, self.gs = batch, length, q, lr, gs
        self.ssd_cps, self.mlstm_cps = ssd_cps, mlstm_cps
        self.rows = batch * length
        self.proj_tm, self.p32_tn, self.qkv_tn, self.merge_tm, self.ffn = proj_tm, p32_tn, qkv_tn, merge_tm, ffn

    def cfg(self, name, layer):
        v = getattr(self, name)
        return v[layer] if isinstance(v, tuple) else v

    def tiling(self, cps):
        rows = cps * self.q
        gs, steps = self.gs, self.length // rows
        assert gs == 1 or steps == 1
        return steps, lambda width, blk: pl.BlockSpec((gs * rows, width), lambda b, c: (b * steps + c, blk))


def _ssd(grp, p32, state, w, layer, prev):
    q, b, gs = grp.q, grp.batch, grp.gs
    has_state = state is not None
    cps = grp.cfg("ssd_cps", layer)
    steps, tile = grp.tiling(cps)
    kern = functools.partial(_ssd_kernel, q=q, lr=grp.lr, nc=steps, has_state=has_state, gs=gs, cps=cps)
    inputs = [p32, p32, p32, p32]
    in_specs = [tile(D_MODEL, P32_Z), tile(D_MODEL, P32_XS),
                tile(SSD_BC, P32_BC_OFF // SSD_BC), tile(SMALL, P32_SM_OFF // SMALL)]
    if has_state:
        inputs += [state["csx"], state["csb"], state["h"]]
        in_specs += [_seq_spec(layer, gs, SSD_CONV - 1, D_MODEL), _seq_spec(layer, gs, SSD_CONV - 1, SSD_BC),
                     _seq_spec(layer, gs, SSD_HP, SSD_STATE)]
    inputs += [w["cwx"], w["cbx"], w["cwb"], w["cbb"], w["dtb"], w["alog"], w["dexp"], w["ssd_nw"], w["e"],
               w["bd"]]
    const = lambda *shape: pl.BlockSpec(shape, lambda b, c: (0,) * len(shape))
    in_specs += [_layer_spec(layer, SSD_CONV, D_MODEL), _layer_spec(layer, 1, D_MODEL),
                 _layer_spec(layer, SSD_CONV, SSD_BC), _layer_spec(layer, 1, SSD_BC),
                 _layer_spec(layer, 1, SMALL), _layer_spec(layer, 1, SMALL), _layer_spec(layer, 1, D_MODEL),
                 _layer_spec(layer, 1, D_MODEL), const(LANES, SSD_HP), const(SSD_GROUPS * SSD_STATE, SSD_HP)]
    return _stacked_call(
        kern, name="ssd", grid=(b // gs, steps), inputs=inputs, in_specs=in_specs,
        out_shape=(jax.ShapeDtypeStruct((grp.rows, D_MODEL), bf16),
                   jax.ShapeDtypeStruct((DEPTH, b, SSD_CONV - 1, D_MODEL), f32),
                   jax.ShapeDtypeStruct((DEPTH, b, SSD_CONV - 1, SSD_BC), f32),
                   jax.ShapeDtypeStruct((DEPTH, b, SSD_HP, SSD_STATE), f32)),
        out_specs=(tile(D_MODEL, 0), _seq_spec(layer, gs, SSD_CONV - 1, D_MODEL),
                   _seq_spec(layer, gs, SSD_CONV - 1, SSD_BC), _seq_spec(layer, gs, SSD_HP, SSD_STATE)),
        stacked={1: prev and prev[0], 2: prev and prev[1], 3: prev and prev[2]},
        scratch_shapes=[pltpu.VMEM((gs, SUBLANES + q, D_MODEL), f32), pltpu.VMEM((gs, SUBLANES + q, SSD_BC), f32),
                        pltpu.VMEM((gs, SSD_GROUPS * SSD_STATE, SSD_HP), f32), pltpu.VMEM((gs, q, D_MODEL), f32)],
        dimension_semantics=("parallel", "arbitrary"))


def _mlstm(grp, qkv, p32, state, w, layer, prev):
    q, b, gs = grp.q, grp.batch, grp.gs
    has_state = state is not None
    cps = grp.cfg("mlstm_cps", layer)
    steps, tile = grp.tiling(cps)
    kern = functools.partial(_mlstm_kernel, q=q, lr=grp.lr, nc=steps, has_state=has_state, gs=gs, cps=cps)
    hd = MLSTM_HEAD_DIM
    carried = not (has_state and steps == 1 and cps == 1)
    inputs = [qkv, qkv, qkv, p32, p32]
    in_specs = [tile(D_MODEL, 0), tile(D_MODEL, 1), tile(D_MODEL, 2), tile(D_MODEL, P32_O),
                tile(SMALL, P32_SM_OFF // SMALL)]
    if has_state:
        inputs += [state["c"], state["n"], state["m"]]
        in_specs += [_seq_spec(layer, gs, MLSTM_HEADS, hd, hd), _seq_spec(layer, gs, MLSTM_HEADS, hd),
                     _seq_spec(layer, gs, 1, SMALL)]
    inputs += [w["gate_b"], w["mlstm_nw"]]
    in_specs += [_layer_spec(layer, 1, SMALL), _layer_spec(layer, 1, D_MODEL)]
    return _stacked_call(
        kern, name="mlstm", grid=(b // gs, steps), inputs=inputs, in_specs=in_specs,
        out_shape=(jax.ShapeDtypeStruct((grp.rows, D_MODEL), bf16),
                   jax.ShapeDtypeStruct((DEPTH, b, MLSTM_HEADS, hd, hd), f32),
                   jax.ShapeDtypeStruct((DEPTH, b, MLSTM_HEADS, hd), f32),
                   jax.ShapeDtypeStruct((DEPTH, b, 1, SMALL), f32)),
        out_specs=(tile(D_MODEL, 0), _seq_spec(layer, gs, MLSTM_HEADS, hd, hd),
                   _seq_spec(layer, gs, MLSTM_HEADS, hd), _seq_spec(layer, gs, 1, SMALL)),
        stacked={1: prev and prev[0], 2: prev and prev[1], 3: prev and prev[2]},
        scratch_shapes=[pltpu.VMEM((gs, MLSTM_HEADS, hd, hd) if carried else (gs, 1, SUBLANES, LANES), f32),
                        pltpu.VMEM((gs, MLSTM_HEADS, hd), f32), pltpu.VMEM((gs, 1, SMALL), f32)],
        dimension_semantics=("parallel", "arbitrary"))


def _trunk(grp, x, state, w):
    ssd_out = mlstm_out = None
    ffn_out = []
    for layer in range(DEPTH):
        p32 = _proj(x, w["w32"], layer, f32, grp.cfg("proj_tm", layer), grp.cfg("p32_tn", layer))
        qkv = _proj(x, w["wqkv"], layer, bf16, grp.cfg("proj_tm", layer), grp.cfg("qkv_tn", layer))
        ys, *ssd_out = _ssd(grp, p32, state, w, layer, ssd_out)
        hm, *mlstm_out = _mlstm(grp, qkv, p32, state, w, layer, mlstm_out)
        x1 = _merge(ys, hm, p32, x, w, layer, grp.cfg("merge_tm", layer))
        x, s_ffn = _ffn(x1, state["ffn"] if state is not None else None, w, layer, **grp.ffn)
        ffn_out.append(s_ffn)
    return x, ssd_out, mlstm_out, ffn_out


def _unpack_states(batch, ssd_out, mlstm_out):
    csx, csb, h = ssd_out
    c, n, m = mlstm_out
    return (h.reshape(DEPTH, batch, SSD_HEADS, SSD_HEAD_DIM, SSD_STATE),
            jnp.concatenate([csx, csb], axis=-1), c, n, m[:, :, 0, :MLSTM_HEADS])


def kernel(x_prompt, x_sample, state_ssd, state_ssd_conv, state_mlstm_c, state_mlstm_n, state_mlstm_m,
           state_ffn_conv, w_in, ssd_conv_w, ssd_conv_b, ssd_dt_bias, ssd_a_log, ssd_d, ssd_norm_w,
           mlstm_gate_b, mlstm_norm_w, w_branch_a, w_branch_b, w_out, ln1_g, ln1_b, ffn_w_up, ffn_conv_w,
           ffn_conv_b, ffn_w_down, ln2_g, ln2_b):
    w = _prep_weights(w_in, ssd_conv_w, ssd_conv_b, ssd_dt_bias, ssd_a_log, ssd_d, ssd_norm_w, mlstm_gate_b,
                      mlstm_norm_w, w_branch_a, w_branch_b, w_out, ln1_g, ln1_b, ffn_w_up, ffn_conv_w,
                      ffn_conv_b, ffn_w_down, ln2_g, ln2_b)
    keep = FFN_CONV - 1

    bp, lp, _ = x_prompt.shape
    prompt = _Group(bp, lp, CHUNK, CHUNK, gs=1, ssd_cps=(4, 8), mlstm_cps=(4, 8), proj_tm=2048,
                    p32_tn=(512, 1408), qkv_tn=(1024, 1536), merge_tm=(512, 1024),
                    ffn=dict(groups=bp, tm=512, seq_len=0))
    y_p, ssd_p, mlstm_p, ffn_p = _trunk(prompt, x_prompt.reshape(bp * lp, D_MODEL), None, w)
    st_p = _unpack_states(bp, ssd_p, mlstm_p)
    ffn_conv_p = jnp.stack(ffn_p)[:, :, SUBLANES - keep:, :]

    bs, ls, _ = x_sample.shape
    lpad = max(ls, SAMPLE_PAD_LEN)
    s_rows = bs * lpad
    sample = _Group(bs, lpad, lpad, ls, gs=8, ssd_cps=1, mlstm_cps=1, proj_tm=s_rows, p32_tn=512, qkv_tn=1024,
                    merge_tm=512,
                    ffn=dict(groups=1, tm=256, seq_len=lpad))
    s_state = dict(
        csx=state_ssd_conv[..., :D_MODEL], csb=state_ssd_conv[..., D_MODEL:],
        h=state_ssd.reshape(DEPTH, bs, SSD_HP, SSD_STATE),
        c=state_mlstm_c, n=state_mlstm_n,
        m=jnp.pad(state_mlstm_m, ((0, 0), (0, 0), (0, SMALL - MLSTM_HEADS)))[:, :, None, :],
        ffn=state_ffn_conv.reshape(DEPTH, bs * keep, 2 * D_FF),
    )
    xs = jnp.pad(x_sample, ((0, 0), (0, lpad - ls), (0, 0))).reshape(s_rows, D_MODEL)
    y_s, ssd_s, mlstm_s, ffn_s = _trunk(sample, xs, s_state, w)
    st_s = _unpack_states(bs, ssd_s, mlstm_s)
    ffn_conv_s = jnp.stack([u.reshape(bs, lpad, 2 * D_FF)[:, ls - keep:ls, :] for u in ffn_s])
    y_sample = y_s.reshape(bs, lpad, D_MODEL)[:, :ls, :]

    return (y_p.reshape(bp, lp, D_MODEL), y_sample, st_p[0], st_s[0], st_p[1], st_s[1], st_p[2], st_s[2],
            st_p[3], st_s[3], st_p[4], st_s[4], ffn_conv_p, ffn_conv_s)
```

```python
import functools
import itertools

import jax
import jax.numpy as jnp
import numpy as np
from jax import lax
from jax.experimental import pallas as pl
from jax.experimental.pallas import tpu as pltpu

f32 = jnp.float32
bf16 = jnp.bfloat16

D_MODEL = 1024
DEPTH = 2
SSD_HEADS = 16
SSD_HEAD_DIM = 64
SSD_STATE = 64
SSD_GROUPS = 2
SSD_CONV = 4
SSD_BC = 2 * SSD_GROUPS * SSD_STATE
SSD_HP = SSD_HEADS * SSD_HEAD_DIM
MLSTM_HEADS = 4
MLSTM_HEAD_DIM = 256
CHUNK = 128
D_FF = 2816
FFN_CONV = 3
ALPHA = (2 * DEPTH) ** 0.25
EPS = 1e-5

LANES = 128
SUBLANES = 8
SMALL = LANES
DT_OFF, I_OFF, F_OFF = 0, 16, 20
P32_Z, P32_O, P32_GA, P32_GB, P32_XS = 0, 1, 2, 3, 4
P32_BC_OFF = 5 * D_MODEL
P32_SM_OFF = P32_BC_OFF + SSD_BC
P32_W = P32_SM_OFF + 2 * SMALL
FF_CH = 256
FF_NCH = D_FF // FF_CH
MERGE_PARTS = 2
FF_UP_AHEAD = 2
FF_XP_SLOTS = 4
NEG_BIG = -1e30
SAMPLE_PAD_LEN = 4

NT_DIMS = (((1,), (1,)), ((), ()))
TN_DIMS = (((0,), (0,)), ((), ()))


def _dot(a, b):
    return jnp.dot(a, b, preferred_element_type=f32)


def _split3(x):
    hi = x.astype(bf16)
    r = x - hi.astype(f32)
    mid = r.astype(bf16)
    lo = (r - mid.astype(f32)).astype(bf16)
    return hi, mid, lo


def _dot01_rhs(x, e):
    hi, mid, lo = _split3(x)
    return _dot(hi, e) + _dot(mid, e) + _dot(lo, e)


def _dot01_lhs(t, x):
    hi, mid, lo = _split3(x)
    return _dot(t, hi) + _dot(t, mid) + _dot(t, lo)


def _softplus(x):
    return jnp.maximum(x, 0.0) + jnp.log1p(jnp.exp(-jnp.abs(x)))


def _silu(x):
    return x * jax.nn.sigmoid(x)


def _row_time(i, q, perm):
    if not perm:
        return i
    return (i & (SUBLANES - 1)) * (q // SUBLANES) + lax.shift_right_logical(i, SUBLANES.bit_length() - 1)


def _tri(q, perm=False):
    row = lax.broadcasted_iota(jnp.int32, (q, q), 0)
    col = lax.broadcasted_iota(jnp.int32, (q, q), 1)
    return _row_time(row, q, perm) >= _row_time(col, q, perm)


def _wrap_rows(cur_tail, prev_tail):
    out = []
    for i in range(cur_tail.shape[0] // SUBLANES):
        rows = slice(i * SUBLANES, (i + 1) * SUBLANES)
        first = lax.broadcasted_iota(jnp.int32, (SUBLANES, cur_tail.shape[1]), 0) == 0
        out.append(jnp.where(first, pltpu.roll(prev_tail[rows], 1, axis=0), pltpu.roll(cur_tail[rows], 1, axis=0)))
    return jnp.concatenate(out, axis=0)


def _shift_back(x, wrapped, j):
    n = j * SUBLANES
    return jnp.concatenate([wrapped[wrapped.shape[0] - n:], x[:x.shape[0] - n]], axis=0)


def _valid_rows(q, width, lr, is_last):
    row = lax.broadcasted_iota(jnp.int32, (q, width), 0)
    return row < jnp.where(is_last, lr, q)


def _layer_norm(r, g, b):
    mu = jnp.mean(r, axis=-1, keepdims=True)
    var = jnp.mean(jnp.square(r - mu), axis=-1, keepdims=True)
    return (r - mu) * lax.rsqrt(var + EPS) * g + b


def _layer_spec(layer, *shape, **kw):
    zeros = (0,) * len(shape)
    return pl.BlockSpec((None,) + shape, lambda *_: (layer,) + zeros, **kw)


def _seq_spec(layer, gs, *shape):
    zeros = (0,) * len(shape)
    return pl.BlockSpec((None, gs) + shape, lambda b, c: (layer, b) + zeros)


def _stacked_call(kern, *, name, grid, inputs, in_specs, out_shape, out_specs, stacked, scratch_shapes,
                  dimension_semantics, vmem_limit_bytes=None):
    prev = [(i, a) for i, a in sorted(stacked.items()) if a is not None]
    n_in = len(inputs)

    def body(*refs):
        kern(*refs[:n_in], *refs[n_in + len(prev):])

    return pl.pallas_call(
        body,
        out_shape=out_shape,
        grid=grid,
        in_specs=list(in_specs) + [pl.BlockSpec(memory_space=pl.ANY)] * len(prev),
        out_specs=out_specs,
        scratch_shapes=scratch_shapes,
        input_output_aliases={n_in + k: i for k, (i, _) in enumerate(prev)},
        compiler_params=pltpu.CompilerParams(dimension_semantics=dimension_semantics,
                                             vmem_limit_bytes=vmem_limit_bytes),
        name=name,
    )(*inputs, *[a for _, a in prev])


def _proj_kernel(x_ref, w_ref, o_ref, xb):
    @pl.when(pl.program_id(1) == 0)
    def _():
        xb[...] = x_ref[...].astype(bf16)

    o_ref[...] = lax.dot_general(xb[...], w_ref[...], NT_DIMS, preferred_element_type=f32).astype(o_ref.dtype)


def _proj(x, wt, layer, out_dtype, tm, tn):
    m, k = x.shape
    n = wt.shape[1]
    return pl.pallas_call(
        _proj_kernel,
        out_shape=jax.ShapeDtypeStruct((m, n), out_dtype),
        grid=(m // tm, n // tn),
        in_specs=[pl.BlockSpec((tm, k), lambda i, j: (i, 0)),
                  pl.BlockSpec((None, tn, k), lambda i, j: (layer, j, 0))],
        out_specs=pl.BlockSpec((tm, tn), lambda i, j: (i, j)),
        scratch_shapes=[pltpu.VMEM((tm, k), bf16)],
        compiler_params=pltpu.CompilerParams(dimension_semantics=("parallel", "arbitrary")),
        name="proj",
    )(x, wt)


class _Rows:
    def __init__(self, ref, start, n):
        self.ref, self.start, self.n, self.dtype = ref, start, n, ref.dtype

    def rows(self, off, n):
        return _Rows(self.ref, self.start + off, n)

    def _index(self, idx):
        cols = slice(None) if idx is Ellipsis else idx[1]
        return (slice(self.start, self.start + self.n), cols)

    def __getitem__(self, idx):
        return self.ref[self._index(idx)]

    def __setitem__(self, idx, value):
        self.ref[self._index(idx)] = value


def _per_sequence(seq_fn, refs, n_tile, n_state, n_param, gs, has_state, nc, rows):
    n_state = n_state if has_state else 0
    tiles, refs = refs[:n_tile], refs[n_tile:]
    state, refs = refs[:n_state], refs[n_state:]
    params, (y_tile, *rest) = refs[:n_param], refs[n_param:]
    phases = []
    for g in range(gs):
        at = lambda group: tuple(r.at[g] for r in group)
        seq_rows = lambda group: tuple(_Rows(r, g * rows, rows) for r in group)
        phases.append(seq_fn(*seq_rows(tiles), *at(state), *params, *seq_rows((y_tile,)), *at(rest)))
    c = pl.program_id(1)

    @pl.when(c == 0)
    def _():
        for init, _, _ in phases:
            init()

    for _ in itertools.zip_longest(*[body() for _, body, _ in phases]):
        pass

    @pl.when(c == nc - 1)
    def _():
        for _, _, final in phases:
            final()


def _ssd_kernel(*refs, q, lr, nc, has_state, gs, cps, perm):
    seq = functools.partial(_ssd_seq, q=q, lr=lr, nc=nc, has_state=has_state, cps=cps, perm=perm)
    _per_sequence(seq, refs, 4, 3, 10, gs, has_state, nc, cps * q)


def _ssd_seq(*refs, q, lr, nc, has_state, cps, perm):
    z_ref, xs_ref, bc_ref, sm_ref = refs[:4]
    refs = refs[4:]
    if has_state:
        csx_ref, csb_ref, h0_ref = refs[:3]
        refs = refs[3:]
    (cwx_ref, cbx_ref, cwb_ref, cbb_ref, dtb_ref, alog_ref, dexp_ref, nw_ref, e_ref, bd_ref,
     y_ref, ncsx_ref, ncsb_ref, hout_ref, xpx, xpb, ht, yb) = refs
    hdr = SUBLANES
    lo = hdr - (SSD_CONV - 1)
    n2 = SSD_GROUPS * SSD_STATE
    assert lr >= SSD_CONV - 1
    keep = SSD_CONV - 1
    assert not (perm and (has_state or lr != q))
    carried = [(i + 1) * SUBLANES - 1 for i in range(keep)]

    def init():
        if has_state:
            h_t = h0_ref[...].T
            ht[...] = jnp.where(bd_ref[...] > 0.5, jnp.concatenate([h_t, h_t], axis=0), 0.0)
            xpx[lo:hdr, :] = csx_ref[...]
            xpb[lo:hdr, :] = csb_ref[...]
        else:
            ht[...] = jnp.zeros_like(ht)
            rows = slice(0, keep * SUBLANES) if perm else slice(lo, hdr)
            xpx[rows, :] = jnp.zeros((rows.stop - rows.start, SSD_HP), f32)
            xpb[rows, :] = jnp.zeros((rows.stop - rows.start, SSD_BC), f32)

    def final():
        if perm:
            for i, r in enumerate(carried):
                ncsx_ref[i:i + 1, :] = xpx[r:r + 1, :]
                ncsb_ref[i:i + 1, :] = xpb[r:r + 1, :]
        else:
            ncsx_ref[...] = xpx[lo + lr:hdr + lr, :]
            ncsb_ref[...] = xpb[lo + lr:hdr + lr, :]
        h_new = ht[...]
        hout_ref[...] = (h_new[:SSD_STATE, :] + h_new[SSD_STATE:, :]).T

    def body():
        for k in range(cps):
            sub = lambda r: r.rows(k * q, q)
            is_last = (pl.program_id(1) == nc - 1) if k == cps - 1 else False
            yield from _ssd_body(sub(z_ref), sub(xs_ref), sub(bc_ref), sub(sm_ref), cwx_ref, cbx_ref, cwb_ref,
                                 cbb_ref, dtb_ref, alog_ref, dexp_ref, nw_ref, e_ref, bd_ref, sub(y_ref),
                                 xpx, xpb, ht, yb, q=q, lr=lr, is_last=is_last, perm=perm)

    return init, body, final


def _ssd_body(z_ref, xs_ref, bc_ref, sm_ref, cwx_ref, cbx_ref, cwb_ref, cbb_ref, dtb_ref, alog_ref,
              dexp_ref, nw_ref, e_ref, bd_ref, y_ref, xpx, xpb, ht, yb, *, q, lr, is_last, perm):
    hdr = SUBLANES
    lo = hdr - (SSD_CONV - 1)
    n2 = SSD_GROUPS * SSD_STATE
    block_diag = bd_ref[...] > 0.5

    dt = _softplus(sm_ref[...] + dtb_ref[...])
    if lr < q:
        dt = jnp.where(_valid_rows(q, SMALL, lr, is_last), dt, 0.0)
    a = -jnp.exp(alog_ref[...])
    d_a = dt * a
    causal = _tri(q, perm)
    tril = jnp.where(causal, 1.0, 0.0).astype(bf16)
    e = e_ref[...]
    acs = _dot01_lhs(tril, d_a)
    dt_x = _dot01_rhs(dt, e)
    yield

    if perm:
        keep_rows = (SSD_CONV - 1) * SUBLANES

        def conv(xp, x_ref, w_ref, b_ref):
            w = w_ref[...]
            x = x_ref[...]
            wrapped = _wrap_rows(x[q - keep_rows:, :], xp[0:keep_rows, :])
            acc = _shift_back(x, wrapped, SSD_CONV - 1) * w[0:1, :]
            for j in range(1, SSD_CONV - 1):
                acc = acc + _shift_back(x, wrapped, SSD_CONV - 1 - j) * w[j:j + 1, :]
            acc = acc + x * w[SSD_CONV - 1:SSD_CONV, :]
            xp[0:keep_rows, :] = x[q - keep_rows:, :]
            return acc + b_ref[...]
    else:
        xpx[hdr:hdr + q, :] = xs_ref[...]
        xpb[hdr:hdr + q, :] = bc_ref[...]

        def conv(xp, x_ref, w_ref, b_ref):
            w = w_ref[...]
            acc = xp[lo:lo + q, :] * w[0:1, :]
            for j in range(1, SSD_CONV):
                acc = acc + xp[lo + j:lo + j + q, :] * w[j:j + 1, :]
            return acc + b_ref[...]

    cb = conv(xpb, bc_ref, cwb_ref, cbb_ref)
    bcv = _silu(cb)
    bm = bcv[:, :n2].astype(bf16)
    cm = bcv[:, n2:]
    lane_g0 = lax.broadcasted_iota(jnp.int32, (q, n2), 1) < SSD_STATE
    acs_t = acs.T
    acs_x = _dot01_rhs(acs, e)
    yield
    cbms = [lax.dot_general(jnp.where(lane_g0 if g == 0 else jnp.logical_not(lane_g0), cm, 0.0).astype(bf16),
                            bm, NT_DIMS, preferred_element_type=f32) for g in range(SSD_GROUPS)]
    h_prev = ht[...]
    y_off = _dot(cm.astype(bf16), h_prev.astype(bf16))
    cx = conv(xpx, xs_ref, cwx_ref, cbx_ref)
    if not perm:
        tail_x = xpx[lo + q:hdr + q, :]
        tail_b = xpb[lo + q:hdr + q, :]
        xpx[lo:hdr, :] = tail_x
        xpb[lo:hdr, :] = tail_b
    yield
    xs = _silu(cx)
    last_x = acs_x[q - 1:q, :]
    xdt = xs * dt_x
    xdt_b = xdt.astype(bf16)
    yield
    lane_lo = lax.broadcasted_iota(jnp.int32, (q, LANES), 1) < SSD_HEAD_DIM
    heads_per_group = SSD_HEADS // SSD_GROUPS
    decays = [jnp.exp(jnp.where(causal, acs[:, hh:hh + 1] - acs_t[hh:hh + 1, :], -jnp.inf))
              for hh in range(SSD_HEADS)]
    yield
    weights = [(cbms[hh // heads_per_group] * decays[hh]).astype(bf16) for hh in range(SSD_HEADS)]
    xdtw = (xdt * jnp.exp(last_x - acs_x)).astype(bf16)
    yield
    ys = [_dot(weights[hh], xdt_b[:, (hh // 2) * LANES:(hh // 2 + 1) * LANES]) for hh in range(SSD_HEADS)]
    upd = lax.dot_general(bm, xdtw, TN_DIMS, preferred_element_type=f32)
    yield
    for p in range(SSD_HEADS // 2):
        yb[:, p * LANES:(p + 1) * LANES] = jnp.where(lane_lo, ys[2 * p], ys[2 * p + 1])
    ht[...] = jnp.exp(last_x) * h_prev + jnp.where(block_diag, upd, 0.0)
    yield
    y = yb[...] + y_off * jnp.exp(acs_x) + dexp_ref[...] * xs
    y = y * _silu(z_ref[...])
    yield
    y = y * lax.rsqrt(jnp.mean(jnp.square(y), axis=-1, keepdims=True) + EPS) * nw_ref[...]
    y_ref[...] = y.astype(y_ref.dtype)


def _mlstm_kernel(*refs, q, lr, nc, has_state, gs, cps, perm):
    seq = functools.partial(_mlstm_seq, q=q, lr=lr, nc=nc, has_state=has_state, cps=cps, perm=perm)
    _per_sequence(seq, refs, 5, 3, 2, gs, has_state, nc, cps * q)


def _mlstm_seq(*refs, q, lr, nc, has_state, cps, perm):
    q_ref, k_ref, v_ref, o_ref, sm_ref = refs[:5]
    refs = refs[5:]
    if has_state:
        c0_ref, n0_ref, m0_ref = refs[:3]
        refs = refs[3:]
    gb_ref, nw_ref, h_ref, cout_ref, nout_ref, mout_ref, cs, ns, ms = refs
    direct = has_state and nc == 1 and cps == 1

    def init():
        if direct:
            return
        if has_state:
            cs[...] = c0_ref[...]
            ns[...] = n0_ref[...]
            ms[...] = m0_ref[...]
        else:
            cs[...] = jnp.zeros_like(cs)
            ns[...] = jnp.zeros_like(ns)
            ms[...] = jnp.zeros_like(ms)

    def final():
        if direct:
            return
        cout_ref[...] = cs[...]
        nout_ref[...] = ns[...]
        mout_ref[...] = ms[...]

    def body():
        chunks = []
        for k in range(cps):
            sub = lambda r, k=k: r.rows(k * q, q)
            is_last = (pl.program_id(1) == nc - 1) if k == cps - 1 else False
            src = (c0_ref, n0_ref, m0_ref) if direct else (cs, ns, ms)
            dst = (cout_ref, nout_ref, mout_ref) if direct else (cs, ns, ms)
            chunks.append(_mlstm_body(sub(q_ref), sub(k_ref), sub(v_ref), sub(o_ref), sub(sm_ref), gb_ref, nw_ref,
                                      sub(h_ref), src, dst, q=q, lr=lr, is_last=is_last, perm=perm))
        yield from _staggered(chunks, MLSTM_STATE_STAGES)

    return init, body, final


MLSTM_STATE_STAGES = 8


def _staggered(gens, skew):
    done = [False] * len(gens)
    t = 0
    while not all(done):
        for i, g in enumerate(gens):
            if done[i] or t < i * skew:
                continue
            try:
                next(g)
            except StopIteration:
                done[i] = True
        t += 1
        yield


def _mlstm_body(q_ref, k_ref, v_ref, o_ref, sm_ref, gb_ref, nw_ref, h_ref, src, dst, *, q, lr, is_last, perm):
    c_src, n_src, m_src = src
    c_dst, n_dst, m_dst = dst
    sm = sm_ref[...] + gb_ref[...]
    logf = -_softplus(-sm)
    ipre = sm
    if lr < q:
        valid = _valid_rows(q, SMALL, lr, is_last)
        logf = jnp.where(valid, logf, 0.0)
        ipre = jnp.where(valid, ipre, NEG_BIG)
    causal = _tri(q, perm)
    tril = jnp.where(causal, 1.0, 0.0).astype(bf16)
    yield
    bcum = _dot01_lhs(tril, logf)
    ipre_t = ipre.T
    yield
    bcum_t = bcum.T
    lane = lax.broadcasted_iota(jnp.int32, (1, SMALL), 1)
    k_scale = MLSTM_HEAD_DIM ** -0.5

    heads = range(MLSTM_HEADS)
    sls = [slice(h * MLSTM_HEAD_DIM, (h + 1) * MLSTM_HEAD_DIM) for h in heads]
    q_all, k_all, v_all, o_all = q_ref[...], k_ref[...], v_ref[...], o_ref[...]
    qs = [q_all[:, sl] for sl in sls]
    ks = [k_all[:, sl] * k_scale for sl in sls]
    vs = [v_all[:, sl] for sl in sls]
    b_cols = [bcum[:, F_OFF + h:F_OFF + h + 1] for h in heads]
    i_cols = [ipre[:, I_OFF + h:I_OFF + h + 1] for h in heads]
    dmats = [jnp.where(causal, b_cols[h] - bcum_t[F_OFF + h:F_OFF + h + 1, :] + ipre_t[I_OFF + h:I_OFF + h + 1, :],
                       -jnp.inf) for h in heads]
    yield
    qk = [lax.dot_general(qs[h], ks[h], NT_DIMS, preferred_element_type=f32) for h in heads]
    d_max = [jnp.max(dmats[h], axis=-1, keepdims=True) for h in heads]
    yield
    n_all = n_src[...]
    m_all = m_src[...]
    m_new = m_all
    cs_in = [c_src[h] for h in heads]
    m_prevs = [m_all[:, h:h + 1] for h in heads]
    qc = [_dot(qs[h], cs_in[h].astype(bf16)) for h in heads]
    inters = [b_cols[h] + m_prevs[h] for h in heads]
    m_ts = [jnp.maximum(inters[h], d_max[h]) for h in heads]
    yield
    w_inters = [jnp.exp(inters[h] - m_ts[h]) for h in heads]
    ss = [qk[h] * jnp.exp(dmats[h] - m_ts[h]) for h in heads]
    yield
    sv = [_dot(ss[h].astype(bf16), vs[h]) for h in heads]
    m_ends = [m_ts[h][q - 1:q, :] for h in heads]
    b_lasts = [b_cols[h][q - 1:q, :] for h in heads]
    kws = [ks[h].astype(f32) * jnp.exp(b_lasts[h] - b_cols[h] + i_cols[h] - m_ends[h]) for h in heads]
    yield
    kv = [lax.dot_general(kws[h].astype(bf16), vs[h], TN_DIMS, preferred_element_type=f32) for h in heads]
    qns = [jnp.sum(qs[h].astype(f32) * n_all[h:h + 1, :], axis=-1, keepdims=True) for h in heads]
    yield
    dens = [jnp.sum(ss[h], axis=-1, keepdims=True) + w_inters[h] * qns[h] for h in heads]
    yield
    hvs = [(sv[h] + w_inters[h] * qc[h]) / jnp.maximum(jnp.abs(dens[h]), jnp.exp(-m_ts[h])) for h in heads]
    yield
    rms = [lax.rsqrt(jnp.mean(jnp.square(hvs[h]), axis=-1, keepdims=True) + EPS) for h in heads]
    yield
    h_new = [(hvs[h] * rms[h] * nw_ref[:, sls[h]] * jax.nn.sigmoid(o_all[:, sls[h]])).astype(h_ref.dtype)
             for h in heads]
    w_cs = [jnp.exp(b_lasts[h] + m_prevs[h] - m_ends[h]) for h in heads]
    yield
    for h in heads:
        c_dst[h] = w_cs[h] * cs_in[h] + kv[h]
        m_new = jnp.where(lane == h, m_ends[h], m_new)
    h_ref[...] = jnp.concatenate(h_new, axis=1)
    n_dst[...] = jnp.concatenate(
        [w_cs[h] * n_all[h:h + 1, :] + jnp.sum(kws[h], axis=0, keepdims=True) for h in heads], axis=0)
    m_dst[...] = m_new


def _merge_kernel(ys_ref, hm_ref, ga_ref, gb_ref, x_ref, wa_ref, wb_ref, wo_ref, g_ref, b_ref, o_ref):
    tm = x_ref.shape[0]
    rows = [pl.ds(i * (tm // MERGE_PARTS), tm // MERGE_PARTS) for i in range(MERGE_PARTS)]
    br = [(_dot(ys_ref[r, :], wa_ref[...]), _dot(hm_ref[r, :], wb_ref[...])) for r in rows]
    merged = [(jax.nn.sigmoid(ga_ref[r, :]) * a + jax.nn.sigmoid(gb_ref[r, :]) * b).astype(bf16)
              for r, (a, b) in zip(rows, br)]
    mix = [_dot(m, wo_ref[...]) for m in merged]
    for r, m in zip(rows, mix):
        o_ref[r, :] = _layer_norm(ALPHA * x_ref[r, :] + m, g_ref[...], b_ref[...])


def _merge(ys, hm, p32, x, w, layer, tm):
    m = x.shape[0]
    row = lambda blk: pl.BlockSpec((tm, D_MODEL), lambda i: (i, blk))
    return pl.pallas_call(
        _merge_kernel,
        out_shape=jax.ShapeDtypeStruct((m, D_MODEL), f32),
        grid=(m // tm,),
        in_specs=[row(0), row(0), row(P32_GA), row(P32_GB), row(0),
                  _layer_spec(layer, D_MODEL, D_MODEL), _layer_spec(layer, D_MODEL, D_MODEL),
                  _layer_spec(layer, D_MODEL, D_MODEL), _layer_spec(layer, 1, D_MODEL),
                  _layer_spec(layer, 1, D_MODEL)],
        out_specs=row(0),
        compiler_params=pltpu.CompilerParams(dimension_semantics=("parallel",)),
        name="merge",
    )(ys, hm, p32, p32, x, w["wa"], w["wb"], w["wo"], w["ln1_g"], w["ln1_b"])


def _ffn_kernel(*refs, tm, seq_len, perm_q):
    multi = seq_len > 0
    keep_rows = (FFN_CONV - 1) * SUBLANES
    if multi:
        x_ref, st_ref, wup_ref, cw_ref, cb_ref, wdn_ref, g_ref, b_ref, o_ref, sout_ref, xp = refs
    else:
        x_ref, wup_ref, cw_ref, cb_ref, wdn_ref, g_ref, b_ref, o_ref, sout_ref, xp, carry = refs
        @pl.when(pl.program_id(1) == 0)
        def _():
            carry[...] = jnp.zeros_like(carry)

    hdr = SUBLANES
    x = x_ref[...]
    xb = x.astype(bf16)
    if multi:
        assert seq_len & (seq_len - 1) == 0
        nseq = tm // seq_len
        t = lax.broadcasted_iota(jnp.int32, (tm, FF_CH), 0) & (seq_len - 1)
        row = lax.broadcasted_iota(jnp.int32, (tm, 2 * nseq), 0)
        col = lax.broadcasted_iota(jnp.int32, (tm, 2 * nseq), 1)
        t_sel = row & (seq_len - 1)
        seq0 = lax.shift_right_logical(row - t_sel, (seq_len // 2).bit_length() - 1)
        sel_p2 = jnp.where(col == seq0 + t_sel, jnp.where(t_sel < 2, 1.0, 0.0), 0.0).astype(bf16)
        sel_p1 = jnp.where(col == seq0 + 1, jnp.where(t_sel == 0, 1.0, 0.0), 0.0).astype(bf16)
        xp[:, 0:hdr, :] = jnp.zeros((FF_XP_SLOTS, hdr, FF_CH), f32)

    def cols_of(c, part):
        return slice(part * D_FF + c * FF_CH, part * D_FF + (c + 1) * FF_CH)

    def up(c):
        return [_dot(xb, wup_ref[:, cols_of(c, part)]) for part in range(2)]

    def conv_act(c, us):
        halves = []
        for part, u in enumerate(us):
            cols = cols_of(c, part)
            slot = (2 * c + part) % FF_XP_SLOTS
            if perm_q:
                prev = carry[:, cols]
                p1, p2 = [], []
                for kk in range(tm // perm_q):
                    uc = u[kk * perm_q:(kk + 1) * perm_q, :]
                    wrapped = _wrap_rows(uc[perm_q - keep_rows:, :], prev)
                    p1.append(_shift_back(uc, wrapped, 1))
                    p2.append(_shift_back(uc, wrapped, 2))
                    prev = uc[perm_q - keep_rows:, :]
                carry[:, cols] = prev
                p1, p2 = jnp.concatenate(p1, axis=0), jnp.concatenate(p2, axis=0)
                w = cw_ref[:, cols]
                halves.append(p2 * w[0:1, :] + p1 * w[1:2, :] + u * w[2:3, :] + cb_ref[:, cols])
                continue
            xp[slot, hdr:hdr + tm, :] = u
            if multi:
                sout_ref[:, cols] = u
                st = st_ref[:, cols]
                p1 = jnp.where(t == 0, _dot01_lhs(sel_p1, st), xp[slot, hdr - 1:hdr - 1 + tm, :])
                p2 = jnp.where(t < 2, _dot01_lhs(sel_p2, st), xp[slot, hdr - 2:hdr - 2 + tm, :])
            else:
                xp[slot, 0:hdr, :] = carry[:, cols]
                p1 = xp[slot, hdr - 1:hdr - 1 + tm, :]
                p2 = xp[slot, hdr - 2:hdr - 2 + tm, :]
                carry[:, cols] = u[tm - hdr:tm, :]
            w = cw_ref[:, cols]
            halves.append(p2 * w[0:1, :] + p1 * w[1:2, :] + u * w[2:3, :] + cb_ref[:, cols])
        return (_silu(halves[0]) * halves[1]).astype(bf16)

    acc = None
    ahead = [up(c) for c in range(min(FF_UP_AHEAD, FF_NCH))]
    pending = None
    for c in range(FF_NCH):
        if c + FF_UP_AHEAD < FF_NCH:
            ahead.append(up(c + FF_UP_AHEAD))
        if pending is not None:
            d = _dot(pending, wdn_ref[(c - 1) * FF_CH:c * FF_CH, :])
            acc = d if acc is None else acc + d
        pending = conv_act(c, ahead.pop(0))
    acc = acc + _dot(pending, wdn_ref[(FF_NCH - 1) * FF_CH:FF_NCH * FF_CH, :])

    if not multi:
        sout_ref[...] = carry[...]
    o_ref[...] = _layer_norm(ALPHA * x + acc, g_ref[...], b_ref[...])


def _ffn(x, st, w, layer, *, groups, tm, seq_len, perm_q=0):
    m = x.shape[0]
    tiles = m // (groups * tm)
    multi = seq_len > 0
    kern = functools.partial(_ffn_kernel, tm=tm, seq_len=seq_len, perm_q=perm_q)
    carry_rows = (FFN_CONV - 1) * SUBLANES if perm_q else SUBLANES
    once = dict(pipeline_mode=pl.Buffered(1))
    x_spec = pl.BlockSpec((tm, D_MODEL), lambda s, j: (s * tiles + j, 0))
    w_specs = [_layer_spec(layer, D_MODEL, 2 * D_FF, **once), _layer_spec(layer, FFN_CONV, 2 * D_FF, **once),
               _layer_spec(layer, 1, 2 * D_FF, **once), _layer_spec(layer, D_FF, D_MODEL, **once),
               _layer_spec(layer, 1, D_MODEL, **once), _layer_spec(layer, 1, D_MODEL, **once)]
    w_args = (w["wup"], w["fcw"], w["fcb"], w["wdn"], w["ln2_g"], w["ln2_b"])
    xp = pltpu.VMEM((FF_XP_SLOTS, SUBLANES + (0 if perm_q else tm), FF_CH), f32)
    if multi:
        nst = 2 * (tm // seq_len)
        inputs = (x, st) + w_args
        in_specs = [x_spec, pl.BlockSpec((None, nst, 2 * D_FF), lambda s, j: (layer, s * tiles + j, 0))] + w_specs
        sout_shape = jax.ShapeDtypeStruct((m, 2 * D_FF), f32)
        sout_spec = pl.BlockSpec((tm, 2 * D_FF), lambda s, j: (s * tiles + j, 0))
        scratch = [xp]
    else:
        inputs = (x,) + w_args
        in_specs = [x_spec] + w_specs
        sout_shape = jax.ShapeDtypeStruct((groups, carry_rows, 2 * D_FF), f32)
        sout_spec = pl.BlockSpec((None, carry_rows, 2 * D_FF), lambda s, j: (s, 0, 0))
        scratch = [xp, pltpu.VMEM((carry_rows, 2 * D_FF), f32)]
    return pl.pallas_call(
        kern,
        out_shape=(jax.ShapeDtypeStruct((m, D_MODEL), f32), sout_shape),
        grid=(groups, tiles),
        in_specs=in_specs,
        out_specs=(x_spec, sout_spec),
        scratch_shapes=scratch,
        compiler_params=pltpu.CompilerParams(dimension_semantics=("parallel", "arbitrary"),
                                             vmem_limit_bytes=56 * 1024 * 1024),
        name="ffn",
    )(*inputs)


def _pad_lanes(v, off, width=SMALL):
    out = jnp.zeros((v.shape[0], 1, width), f32)
    return out.at[:, 0, off:off + v.shape[1]].set(v.astype(f32))


def _prep_weights(w_in, ssd_conv_w, ssd_conv_b, ssd_dt_bias, ssd_a_log, ssd_d, ssd_norm_w, mlstm_gate_b,
                  mlstm_norm_w, w_branch_a, w_branch_b, w_out, ln1_g, ln1_b, ffn_w_up, ffn_conv_w,
                  ffn_conv_b, ffn_w_down, ln2_g, ln2_b):
    d = D_MODEL
    o_z, o_xbc, o_dt = 0, d, d + d + SSD_BC
    o_q = o_dt + SSD_HEADS
    o_if = o_q + 3 * d
    o_o = o_if + 2 * MLSTM_HEADS
    o_g = o_o + d
    w_t = jnp.swapaxes(w_in, 1, 2)
    cols = lambda a, n: w_t[:, a:a + n, :]
    zeros = lambda n: jnp.zeros((DEPTH, n, d), w_in.dtype)
    w32 = jnp.concatenate([cols(o_z, d), cols(o_o, d), cols(o_g, 2 * d), cols(o_xbc, d + SSD_BC),
                           cols(o_dt, SSD_HEADS), cols(o_if, 2 * MLSTM_HEADS),
                           zeros(P32_W - P32_SM_OFF - SSD_HEADS - 2 * MLSTM_HEADS)], axis=1).astype(bf16)
    e = (np.arange(SSD_HP)[None, :] // SSD_HEAD_DIM == np.arange(LANES)[:, None])
    bd = ((np.arange(SSD_GROUPS * SSD_STATE)[:, None] < SSD_STATE)
          == (np.arange(SSD_HP)[None, :] < SSD_HP // SSD_GROUPS))
    row = lambda a: a[:, None, :]
    return dict(
        w32=w32, wqkv=cols(o_q, 3 * d).astype(bf16),
        cwx=ssd_conv_w[:, :, :d], cbx=row(ssd_conv_b[:, :d]),
        cwb=ssd_conv_w[:, :, d:], cbb=row(ssd_conv_b[:, d:]),
        dtb=_pad_lanes(ssd_dt_bias, DT_OFF), alog=_pad_lanes(ssd_a_log, DT_OFF),
        dexp=row(jnp.repeat(ssd_d.astype(f32), SSD_HEAD_DIM, axis=1)), ssd_nw=row(ssd_norm_w),
        e=jnp.asarray(e, bf16), bd=jnp.asarray(bd, f32),
        gate_b=_pad_lanes(mlstm_gate_b, I_OFF), mlstm_nw=row(mlstm_norm_w),
        wa=w_branch_a.astype(bf16), wb=w_branch_b.astype(bf16), wo=w_out.astype(bf16),
        ln1_g=row(ln1_g), ln1_b=row(ln1_b),
        wup=ffn_w_up.astype(bf16), fcw=ffn_conv_w, fcb=row(ffn_conv_b), wdn=ffn_w_down.astype(bf16),
        ln2_g=row(ln2_g), ln2_b=row(ln2_b),
    )


class _Group:
    def __init__(self, batch, length, q, lr, gs, ssd_cps, mlstm_cps, proj_tm, p32_tn, qkv_tn, merge_tm, ffn,
                 perm=False):
        self.perm = perm
        self.batch, self.length, self.q, self.lr, self.gs = batch, length, q, lr, gs
        self.ssd_cps, self.mlstm_cps = ssd_cps, mlstm_cps
        self.rows = batch * length
        self.proj_tm, self.p32_tn, self.qkv_tn, self.merge_tm, self.ffn = proj_tm, p32_tn, qkv_tn, merge_tm, ffn

    def cfg(self, name, layer):
        v = getattr(self, name)
        return v[layer] if isinstance(v, tuple) else v

    def tiling(self, cps):
        rows = cps * self.q
        gs, steps = self.gs, self.length // rows
        assert gs == 1 or steps == 1
        return steps, lambda width, blk: pl.BlockSpec((gs * rows, width), lambda b, c: (b * steps + c, blk))


def _ssd(grp, p32, state, w, layer, prev):
    q, b, gs = grp.q, grp.batch, grp.gs
    has_state = state is not None
    cps = grp.cfg("ssd_cps", layer)
    steps, tile = grp.tiling(cps)
    kern = functools.partial(_ssd_kernel, q=q, lr=grp.lr, nc=steps, has_state=has_state, gs=gs, cps=cps,
                             perm=grp.perm)
    inputs = [p32, p32, p32, p32]
    in_specs = [tile(D_MODEL, P32_Z), tile(D_MODEL, P32_XS),
                tile(SSD_BC, P32_BC_OFF // SSD_BC), tile(SMALL, P32_SM_OFF // SMALL)]
    if has_state:
        inputs += [state["csx"], state["csb"], state["h"]]
        in_specs += [_seq_spec(layer, gs, SSD_CONV - 1, D_MODEL), _seq_spec(layer, gs, SSD_CONV - 1, SSD_BC),
                     _seq_spec(layer, gs, SSD_HP, SSD_STATE)]
    inputs += [w["cwx"], w["cbx"], w["cwb"], w["cbb"], w["dtb"], w["alog"], w["dexp"], w["ssd_nw"], w["e"],
               w["bd"]]
    const = lambda *shape: pl.BlockSpec(shape, lambda b, c: (0,) * len(shape))
    in_specs += [_layer_spec(layer, SSD_CONV, D_MODEL), _layer_spec(layer, 1, D_MODEL),
                 _layer_spec(layer, SSD_CONV, SSD_BC), _layer_spec(layer, 1, SSD_BC),
                 _layer_spec(layer, 1, SMALL), _layer_spec(layer, 1, SMALL), _layer_spec(layer, 1, D_MODEL),
                 _layer_spec(layer, 1, D_MODEL), const(LANES, SSD_HP), const(SSD_GROUPS * SSD_STATE, SSD_HP)]
    return _stacked_call(
        kern, name="ssd", grid=(b // gs, steps), inputs=inputs, in_specs=in_specs,
        out_shape=(jax.ShapeDtypeStruct((grp.rows, D_MODEL), bf16),
                   jax.ShapeDtypeStruct((DEPTH, b, SSD_CONV - 1, D_MODEL), f32),
                   jax.ShapeDtypeStruct((DEPTH, b, SSD_CONV - 1, SSD_BC), f32),
                   jax.ShapeDtypeStruct((DEPTH, b, SSD_HP, SSD_STATE), f32)),
        out_specs=(tile(D_MODEL, 0), _seq_spec(layer, gs, SSD_CONV - 1, D_MODEL),
                   _seq_spec(layer, gs, SSD_CONV - 1, SSD_BC), _seq_spec(layer, gs, SSD_HP, SSD_STATE)),
        stacked={1: prev and prev[0], 2: prev and prev[1], 3: prev and prev[2]},
        scratch_shapes=[pltpu.VMEM((gs, SUBLANES + q, D_MODEL), f32), pltpu.VMEM((gs, SUBLANES + q, SSD_BC), f32),
                        pltpu.VMEM((gs, SSD_GROUPS * SSD_STATE, SSD_HP), f32), pltpu.VMEM((gs, q, D_MODEL), f32)],
        dimension_semantics=("parallel", "arbitrary"))


def _mlstm(grp, qkv, p32, state, w, layer, prev):
    q, b, gs = grp.q, grp.batch, grp.gs
    has_state = state is not None
    cps = grp.cfg("mlstm_cps", layer)
    steps, tile = grp.tiling(cps)
    kern = functools.partial(_mlstm_kernel, q=q, lr=grp.lr, nc=steps, has_state=has_state, gs=gs, cps=cps,
                             perm=grp.perm)
    hd = MLSTM_HEAD_DIM
    carried = not (has_state and steps == 1 and cps == 1)
    inputs = [qkv, qkv, qkv, p32, p32]
    in_specs = [tile(D_MODEL, 0), tile(D_MODEL, 1), tile(D_MODEL, 2), tile(D_MODEL, P32_O),
                tile(SMALL, P32_SM_OFF // SMALL)]
    if has_state:
        inputs += [state["c"], state["n"], state["m"]]
        in_specs += [_seq_spec(layer, gs, MLSTM_HEADS, hd, hd), _seq_spec(layer, gs, MLSTM_HEADS, hd),
                     _seq_spec(layer, gs, 1, SMALL)]
    inputs += [w["gate_b"], w["mlstm_nw"]]
    in_specs += [_layer_spec(layer, 1, SMALL), _layer_spec(layer, 1, D_MODEL)]
    return _stacked_call(
        kern, name="mlstm", grid=(b // gs, steps), inputs=inputs, in_specs=in_specs,
        out_shape=(jax.ShapeDtypeStruct((grp.rows, D_MODEL), bf16),
                   jax.ShapeDtypeStruct((DEPTH, b, MLSTM_HEADS, hd, hd), f32),
                   jax.ShapeDtypeStruct((DEPTH, b, MLSTM_HEADS, hd), f32),
                   jax.ShapeDtypeStruct((DEPTH, b, 1, SMALL), f32)),
        out_specs=(tile(D_MODEL, 0), _seq_spec(layer, gs, MLSTM_HEADS, hd, hd),
                   _seq_spec(layer, gs, MLSTM_HEADS, hd), _seq_spec(layer, gs, 1, SMALL)),
        stacked={1: prev and prev[0], 2: prev and prev[1], 3: prev and prev[2]},
        scratch_shapes=[pltpu.VMEM((gs, MLSTM_HEADS, hd, hd) if carried else (gs, 1, SUBLANES, LANES), f32),
                        pltpu.VMEM((gs, MLSTM_HEADS, hd), f32), pltpu.VMEM((gs, 1, SMALL), f32)],
        dimension_semantics=("parallel", "arbitrary"))


def _trunk(grp, x, state, w):
    ssd_out = mlstm_out = None
    ffn_out = []
    for layer in range(DEPTH):
        p32 = _proj(x, w["w32"], layer, f32, grp.cfg("proj_tm", layer), grp.cfg("p32_tn", layer))
        qkv = _proj(x, w["wqkv"], layer, bf16, grp.cfg("proj_tm", layer), grp.cfg("qkv_tn", layer))
        ys, *ssd_out = _ssd(grp, p32, state, w, layer, ssd_out)
        hm, *mlstm_out = _mlstm(grp, qkv, p32, state, w, layer, mlstm_out)
        x1 = _merge(ys, hm, p32, x, w, layer, grp.cfg("merge_tm", layer))
        x, s_ffn = _ffn(x1, state["ffn"] if state is not None else None, w, layer, **grp.ffn)
        ffn_out.append(s_ffn)
    return x, ssd_out, mlstm_out, ffn_out


def _unpack_states(batch, ssd_out, mlstm_out):
    csx, csb, h = ssd_out
    c, n, m = mlstm_out
    return (h.reshape(DEPTH, batch, SSD_HEADS, SSD_HEAD_DIM, SSD_STATE),
            jnp.concatenate([csx, csb], axis=-1), c, n, m[:, :, 0, :MLSTM_HEADS])


def kernel(x_prompt, x_sample, state_ssd, state_ssd_conv, state_mlstm_c, state_mlstm_n, state_mlstm_m,
           state_ffn_conv, w_in, ssd_conv_w, ssd_conv_b, ssd_dt_bias, ssd_a_log, ssd_d, ssd_norm_w,
           mlstm_gate_b, mlstm_norm_w, w_branch_a, w_branch_b, w_out, ln1_g, ln1_b, ffn_w_up, ffn_conv_w,
           ffn_conv_b, ffn_w_down, ln2_g, ln2_b):
    w = _prep_weights(w_in, ssd_conv_w, ssd_conv_b, ssd_dt_bias, ssd_a_log, ssd_d, ssd_norm_w, mlstm_gate_b,
                      mlstm_norm_w, w_branch_a, w_branch_b, w_out, ln1_g, ln1_b, ffn_w_up, ffn_conv_w,
                      ffn_conv_b, ffn_w_down, ln2_g, ln2_b)
    keep = FFN_CONV - 1

    bp, lp, _ = x_prompt.shape
    prompt = _Group(bp, lp, CHUNK, CHUNK, gs=1, ssd_cps=4, mlstm_cps=4, proj_tm=2048, p32_tn=1408, qkv_tn=1536,
                    merge_tm=1024, ffn=dict(groups=bp, tm=512, seq_len=0, perm_q=CHUNK), perm=True)
    per = CHUNK // SUBLANES
    xp_rows = x_prompt.reshape(bp, lp // CHUNK, SUBLANES, per, D_MODEL).swapaxes(2, 3)
    y_p, ssd_p, mlstm_p, ffn_p = _trunk(prompt, xp_rows.reshape(bp * lp, D_MODEL), None, w)
    y_p = y_p.reshape(bp, lp // CHUNK, per, SUBLANES, D_MODEL).swapaxes(2, 3)
    st_p = _unpack_states(bp, ssd_p, mlstm_p)
    ffn_conv_p = jnp.stack(ffn_p)[:, :, SUBLANES - 1::SUBLANES, :]

    bs, ls, _ = x_sample.shape
    lpad = max(ls, SAMPLE_PAD_LEN)
    s_rows = bs * lpad
    sample = _Group(bs, lpad, lpad, ls, gs=8, ssd_cps=1, mlstm_cps=1, proj_tm=s_rows, p32_tn=512, qkv_tn=1024,
                    merge_tm=512,
                    ffn=dict(groups=1, tm=256, seq_len=lpad))
    s_state = dict(
        csx=state_ssd_conv[..., :D_MODEL], csb=state_ssd_conv[..., D_MODEL:],
        h=state_ssd.reshape(DEPTH, bs, SSD_HP, SSD_STATE),
        c=state_mlstm_c, n=state_mlstm_n,
        m=jnp.pad(state_mlstm_m, ((0, 0), (0, 0), (0, SMALL - MLSTM_HEADS)))[:, :, None, :],
        ffn=state_ffn_conv.reshape(DEPTH, bs * keep, 2 * D_FF),
    )
    xs = jnp.pad(x_sample, ((0, 0), (0, lpad - ls), (0, 0))).reshape(s_rows, D_MODEL)
    y_s, ssd_s, mlstm_s, ffn_s = _trunk(sample, xs, s_state, w)
    st_s = _unpack_states(bs, ssd_s, mlstm_s)
    ffn_conv_s = jnp.stack([u.reshape(bs, lpad, 2 * D_FF)[:, ls - keep:ls, :] for u in ffn_s])
    y_sample = y_s.reshape(bs, lpad, D_MODEL)[:, :ls, :]

    return (y_p.reshape(bp, lp, D_MODEL), y_sample, st_p[0], st_s[0], st_p[1], st_s[1], st_p[2], st_s[2],
            st_p[3], st_s[3], st_p[4], st_s[4], ffn_conv_p, ffn_conv_s)
```

```python
import functools
import itertools

import jax
import jax.numpy as jnp
import numpy as np
from jax import lax
from jax.experimental import pallas as pl
from jax.experimental.pallas import tpu as pltpu

f32 = jnp.float32
bf16 = jnp.bfloat16

D_MODEL = 1024
DEPTH = 2
SSD_HEADS = 16
SSD_HEAD_DIM = 64
SSD_STATE = 64
SSD_GROUPS = 2
SSD_CONV = 4
SSD_BC = 2 * SSD_GROUPS * SSD_STATE
SSD_HP = SSD_HEADS * SSD_HEAD_DIM
MLSTM_HEADS = 4
MLSTM_HEAD_DIM = 256
CHUNK = 128
D_FF = 2816
FFN_CONV = 3
ALPHA = (2 * DEPTH) ** 0.25
EPS = 1e-5

LANES = 128
SUBLANES = 8
SMALL = LANES
DT_OFF, I_OFF, F_OFF = 0, 16, 20
P32_Z, P32_O, P32_GA, P32_GB, P32_XS = 0, 1, 2, 3, 4
P32_BC_OFF = 5 * D_MODEL
P32_SM_OFF = P32_BC_OFF + SSD_BC
P32_W = P32_SM_OFF + 2 * SMALL
FF_CH = 256
FF_NCH = D_FF // FF_CH
MERGE_PARTS = 2
FF_UP_AHEAD = 2
FF_XP_SLOTS = 4
NEG_BIG = -1e30
SAMPLE_PAD_LEN = 4

NT_DIMS = (((1,), (1,)), ((), ()))
TN_DIMS = (((0,), (0,)), ((), ()))


def _dot(a, b):
    return jnp.dot(a, b, preferred_element_type=f32)


def _split3(x):
    hi = x.astype(bf16)
    r = x - hi.astype(f32)
    mid = r.astype(bf16)
    lo = (r - mid.astype(f32)).astype(bf16)
    return hi, mid, lo


def _dot01_rhs(x, e):
    hi, mid, lo = _split3(x)
    return _dot(hi, e) + _dot(mid, e) + _dot(lo, e)


def _dot01_lhs(t, x):
    hi, mid, lo = _split3(x)
    return _dot(t, hi) + _dot(t, mid) + _dot(t, lo)


def _softplus(x):
    return jnp.maximum(x, 0.0) + jnp.log1p(jnp.exp(-jnp.abs(x)))


def _silu(x):
    return x * jax.nn.sigmoid(x)


def _row_time(i, q, perm):
    if not perm:
        return i
    return (i & (SUBLANES - 1)) * (q // SUBLANES) + lax.shift_right_logical(i, SUBLANES.bit_length() - 1)


def _tri(q, perm=False):
    row = lax.broadcasted_iota(jnp.int32, (q, q), 0)
    col = lax.broadcasted_iota(jnp.int32, (q, q), 1)
    return _row_time(row, q, perm) >= _row_time(col, q, perm)


def _wrap_rows(cur_tail, prev_tail):
    out = []
    for i in range(cur_tail.shape[0] // SUBLANES):
        rows = slice(i * SUBLANES, (i + 1) * SUBLANES)
        first = lax.broadcasted_iota(jnp.int32, (SUBLANES, cur_tail.shape[1]), 0) == 0
        out.append(jnp.where(first, pltpu.roll(prev_tail[rows], 1, axis=0), pltpu.roll(cur_tail[rows], 1, axis=0)))
    return jnp.concatenate(out, axis=0)


def _shift_back(x, wrapped, j):
    n = j * SUBLANES
    return jnp.concatenate([wrapped[wrapped.shape[0] - n:], x[:x.shape[0] - n]], axis=0)


def _valid_rows(q, width, lr, is_last):
    row = lax.broadcasted_iota(jnp.int32, (q, width), 0)
    return row < jnp.where(is_last, lr, q)


def _layer_norm(r, g, b):
    mu = jnp.mean(r, axis=-1, keepdims=True)
    var = jnp.mean(jnp.square(r - mu), axis=-1, keepdims=True)
    return (r - mu) * lax.rsqrt(var + EPS) * g + b


def _layer_spec(layer, *shape, **kw):
    zeros = (0,) * len(shape)
    return pl.BlockSpec((None,) + shape, lambda *_: (layer,) + zeros, **kw)


def _seq_spec(layer, gs, *shape):
    zeros = (0,) * len(shape)
    return pl.BlockSpec((None, gs) + shape, lambda b, c: (layer, b) + zeros)


def _stacked_call(kern, *, name, grid, inputs, in_specs, out_shape, out_specs, stacked, scratch_shapes,
                  dimension_semantics, vmem_limit_bytes=None):
    prev = [(i, a) for i, a in sorted(stacked.items()) if a is not None]
    n_in = len(inputs)

    def body(*refs):
        kern(*refs[:n_in], *refs[n_in + len(prev):])

    return pl.pallas_call(
        body,
        out_shape=out_shape,
        grid=grid,
        in_specs=list(in_specs) + [pl.BlockSpec(memory_space=pl.ANY)] * len(prev),
        out_specs=out_specs,
        scratch_shapes=scratch_shapes,
        input_output_aliases={n_in + k: i for k, (i, _) in enumerate(prev)},
        compiler_params=pltpu.CompilerParams(dimension_semantics=dimension_semantics,
                                             vmem_limit_bytes=vmem_limit_bytes),
        name=name,
    )(*inputs, *[a for _, a in prev])


def _proj_kernel(x_ref, w_ref, o_ref, xb):
    @pl.when(pl.program_id(1) == 0)
    def _():
        xb[...] = x_ref[...].astype(bf16)

    o_ref[...] = lax.dot_general(xb[...], w_ref[...], NT_DIMS, preferred_element_type=f32).astype(o_ref.dtype)


def _proj(x, wt, layer, out_dtype, tm, tn):
    m, k = x.shape
    n = wt.shape[1]
    return pl.pallas_call(
        _proj_kernel,
        out_shape=jax.ShapeDtypeStruct((m, n), out_dtype),
        grid=(m // tm, n // tn),
        in_specs=[pl.BlockSpec((tm, k), lambda i, j: (i, 0)),
                  pl.BlockSpec((None, tn, k), lambda i, j: (layer, j, 0))],
        out_specs=pl.BlockSpec((tm, tn), lambda i, j: (i, j)),
        scratch_shapes=[pltpu.VMEM((tm, k), bf16)],
        compiler_params=pltpu.CompilerParams(dimension_semantics=("parallel", "arbitrary")),
        name="proj",
    )(x, wt)


class _Rows:
    def __init__(self, ref, start, n):
        self.ref, self.start, self.n, self.dtype = ref, start, n, ref.dtype

    def rows(self, off, n):
        return _Rows(self.ref, self.start + off, n)

    def _index(self, idx):
        cols = slice(None) if idx is Ellipsis else idx[1]
        return (slice(self.start, self.start + self.n), cols)

    def __getitem__(self, idx):
        return self.ref[self._index(idx)]

    def __setitem__(self, idx, value):
        self.ref[self._index(idx)] = value


def _per_sequence(seq_fn, refs, n_tile, n_state, n_param, gs, has_state, nc, rows):
    n_state = n_state if has_state else 0
    tiles, refs = refs[:n_tile], refs[n_tile:]
    state, refs = refs[:n_state], refs[n_state:]
    params, (y_tile, *rest) = refs[:n_param], refs[n_param:]
    phases = []
    for g in range(gs):
        at = lambda group: tuple(r.at[g] for r in group)
        seq_rows = lambda group: tuple(_Rows(r, g * rows, rows) for r in group)
        phases.append(seq_fn(*seq_rows(tiles), *at(state), *params, *seq_rows((y_tile,)), *at(rest)))
    c = pl.program_id(1)

    @pl.when(c == 0)
    def _():
        for init, _, _ in phases:
            init()

    for _ in itertools.zip_longest(*[body() for _, body, _ in phases]):
        pass

    @pl.when(c == nc - 1)
    def _():
        for _, _, final in phases:
            final()


def _ssd_kernel(*refs, q, lr, nc, has_state, gs, cps, perm):
    seq = functools.partial(_ssd_seq, q=q, lr=lr, nc=nc, has_state=has_state, cps=cps, perm=perm)
    _per_sequence(seq, refs, 4, 3, 10, gs, has_state, nc, cps * q)


def _ssd_seq(*refs, q, lr, nc, has_state, cps, perm):
    z_ref, xs_ref, bc_ref, sm_ref = refs[:4]
    refs = refs[4:]
    if has_state:
        csx_ref, csb_ref, h0_ref = refs[:3]
        refs = refs[3:]
    (cwx_ref, cbx_ref, cwb_ref, cbb_ref, dtb_ref, alog_ref, dexp_ref, nw_ref, e_ref, bd_ref,
     y_ref, ncsx_ref, ncsb_ref, hout_ref, xpx, xpb, ht, yb) = refs
    hdr = SUBLANES
    lo = hdr - (SSD_CONV - 1)
    n2 = SSD_GROUPS * SSD_STATE
    assert lr >= SSD_CONV - 1
    keep = SSD_CONV - 1
    assert not (perm and (has_state or lr != q))
    carried = [(i + 1) * SUBLANES - 1 for i in range(keep)]

    def init():
        if has_state:
            h_t = h0_ref[...].T
            ht[...] = jnp.where(bd_ref[...] > 0.5, jnp.concatenate([h_t, h_t], axis=0), 0.0)
            xpx[lo:hdr, :] = csx_ref[...]
            xpb[lo:hdr, :] = csb_ref[...]
        else:
            ht[...] = jnp.zeros_like(ht)
            rows = slice(0, keep * SUBLANES) if perm else slice(lo, hdr)
            xpx[rows, :] = jnp.zeros((rows.stop - rows.start, SSD_HP), f32)
            xpb[rows, :] = jnp.zeros((rows.stop - rows.start, SSD_BC), f32)

    def final():
        if perm:
            for i, r in enumerate(carried):
                ncsx_ref[i:i + 1, :] = xpx[r:r + 1, :]
                ncsb_ref[i:i + 1, :] = xpb[r:r + 1, :]
        else:
            ncsx_ref[...] = xpx[lo + lr:hdr + lr, :]
            ncsb_ref[...] = xpb[lo + lr:hdr + lr, :]
        h_new = ht[...]
        hout_ref[...] = (h_new[:SSD_STATE, :] + h_new[SSD_STATE:, :]).T

    def body():
        for k in range(cps):
            sub = lambda r: r.rows(k * q, q)
            is_last = (pl.program_id(1) == nc - 1) if k == cps - 1 else False
            yield from _ssd_body(sub(z_ref), sub(xs_ref), sub(bc_ref), sub(sm_ref), cwx_ref, cbx_ref, cwb_ref,
                                 cbb_ref, dtb_ref, alog_ref, dexp_ref, nw_ref, e_ref, bd_ref, sub(y_ref),
                                 xpx, xpb, ht, yb, q=q, lr=lr, is_last=is_last, perm=perm)

    return init, body, final


def _ssd_body(z_ref, xs_ref, bc_ref, sm_ref, cwx_ref, cbx_ref, cwb_ref, cbb_ref, dtb_ref, alog_ref,
              dexp_ref, nw_ref, e_ref, bd_ref, y_ref, xpx, xpb, ht, yb, *, q, lr, is_last, perm):
    hdr = SUBLANES
    lo = hdr - (SSD_CONV - 1)
    n2 = SSD_GROUPS * SSD_STATE
    block_diag = bd_ref[...] > 0.5

    dt = _softplus(sm_ref[...] + dtb_ref[...])
    if lr < q:
        dt = jnp.where(_valid_rows(q, SMALL, lr, is_last), dt, 0.0)
    a = -jnp.exp(alog_ref[...])
    d_a = dt * a
    causal = _tri(q, perm)
    tril = jnp.where(causal, 1.0, 0.0).astype(bf16)
    e = e_ref[...]
    acs = _dot01_lhs(tril, d_a)
    dt_x = _dot01_rhs(dt, e)
    yield

    if perm:
        keep_rows = (SSD_CONV - 1) * SUBLANES

        def conv(xp, x_ref, w_ref, b_ref):
            w = w_ref[...]
            x = x_ref[...]
            wrapped = _wrap_rows(x[q - keep_rows:, :], xp[0:keep_rows, :])
            acc = _shift_back(x, wrapped, SSD_CONV - 1) * w[0:1, :]
            for j in range(1, SSD_CONV - 1):
                acc = acc + _shift_back(x, wrapped, SSD_CONV - 1 - j) * w[j:j + 1, :]
            acc = acc + x * w[SSD_CONV - 1:SSD_CONV, :]
            xp[0:keep_rows, :] = x[q - keep_rows:, :]
            return acc + b_ref[...]
    else:
        xpx[hdr:hdr + q, :] = xs_ref[...]
        xpb[hdr:hdr + q, :] = bc_ref[...]

        def conv(xp, x_ref, w_ref, b_ref):
            w = w_ref[...]
            acc = xp[lo:lo + q, :] * w[0:1, :]
            for j in range(1, SSD_CONV):
                acc = acc + xp[lo + j:lo + j + q, :] * w[j:j + 1, :]
            return acc + b_ref[...]

    cb = conv(xpb, bc_ref, cwb_ref, cbb_ref)
    bcv = _silu(cb)
    bm = bcv[:, :n2].astype(bf16)
    cm = bcv[:, n2:]
    lane_g0 = lax.broadcasted_iota(jnp.int32, (q, n2), 1) < SSD_STATE
    acs_t = acs.T
    acs_x = _dot01_rhs(acs, e)
    yield
    cbms = [lax.dot_general(jnp.where(lane_g0 if g == 0 else jnp.logical_not(lane_g0), cm, 0.0).astype(bf16),
                            bm, NT_DIMS, preferred_element_type=f32) for g in range(SSD_GROUPS)]
    h_prev = ht[...]
    y_off = _dot(cm.astype(bf16), h_prev.astype(bf16))
    cx = conv(xpx, xs_ref, cwx_ref, cbx_ref)
    if not perm:
        tail_x = xpx[lo + q:hdr + q, :]
        tail_b = xpb[lo + q:hdr + q, :]
        xpx[lo:hdr, :] = tail_x
        xpb[lo:hdr, :] = tail_b
    yield
    xs = _silu(cx)
    last_x = acs_x[q - 1:q, :]
    xdt = xs * dt_x
    xdt_b = xdt.astype(bf16)
    yield
    lane_lo = lax.broadcasted_iota(jnp.int32, (q, LANES), 1) < SSD_HEAD_DIM
    heads_per_group = SSD_HEADS // SSD_GROUPS
    decays = [jnp.exp(jnp.where(causal, acs[:, hh:hh + 1] - acs_t[hh:hh + 1, :], -jnp.inf))
              for hh in range(SSD_HEADS)]
    yield
    weights = [(cbms[hh // heads_per_group] * decays[hh]).astype(bf16) for hh in range(SSD_HEADS)]
    xdtw = (xdt * jnp.exp(last_x - acs_x)).astype(bf16)
    yield
    ys = [_dot(weights[hh], xdt_b[:, (hh // 2) * LANES:(hh // 2 + 1) * LANES]) for hh in range(SSD_HEADS)]
    upd = lax.dot_general(bm, xdtw, TN_DIMS, preferred_element_type=f32)
    yield
    for p in range(SSD_HEADS // 2):
        yb[:, p * LANES:(p + 1) * LANES] = jnp.where(lane_lo, ys[2 * p], ys[2 * p + 1])
    ht[...] = jnp.exp(last_x) * h_prev + jnp.where(block_diag, upd, 0.0)
    yield
    y = yb[...] + y_off * jnp.exp(acs_x) + dexp_ref[...] * xs
    y = y * _silu(z_ref[...])
    yield
    y = y * lax.rsqrt(jnp.mean(jnp.square(y), axis=-1, keepdims=True) + EPS) * nw_ref[...]
    y_ref[...] = y.astype(y_ref.dtype)


PT_ROWS = 5 * 512
PT_XS, PT_BC, PT_SM = D_MODEL, 2 * D_MODEL, 2 * D_MODEL + SSD_BC


def _proj_t_kernel(x_ref, w_ref, o_ref, xb):
    @pl.when(pl.program_id(0) == 0)
    def _():
        xb[...] = x_ref[...].astype(bf16)

    o_ref[...] = lax.dot_general(w_ref[...], xb[...], NT_DIMS, preferred_element_type=f32)


def _proj_t(x_tm, w32, layer):
    m, k = x_tm.shape
    tn = 512
    xs_blk = P32_XS * D_MODEL // tn
    return pl.pallas_call(
        _proj_t_kernel,
        out_shape=jax.ShapeDtypeStruct((PT_ROWS, m), f32),
        grid=(PT_ROWS // tn,),
        in_specs=[pl.BlockSpec((m, k), lambda j: (0, 0)),
                  pl.BlockSpec((None, tn, k), lambda j: (layer, jnp.where(j < D_MODEL // tn, j, j + xs_blk - D_MODEL // tn), 0))],
        out_specs=pl.BlockSpec((tn, m), lambda j: (j, 0)),
        scratch_shapes=[pltpu.VMEM((m, k), bf16)],
        compiler_params=pltpu.CompilerParams(dimension_semantics=("arbitrary",)),
        name="proj_t",
    )(x_tm, w32)


def _ssd_lanes_kernel(pt_ref, cst_ref, h0_ref, cw_ref, cb_ref, dtb_ref, alog_ref, dexp_ref, nw_ref,
                      y_ref, ncs_ref, hout_ref, xc, dts, decs, ysc, *, steps, batch):
    hd = pl.program_id(0)
    n_ch = D_MODEL + SSD_BC
    keep = SSD_CONV - 1
    lanes = lambda t: slice(t * batch, (t + 1) * batch)

    @pl.when(hd == 0)
    def _():
        for t in range(steps):
            acc = None
            for j in range(SSD_CONV):
                i = t + j
                src = cst_ref[:, lanes(i)] if i < keep else pt_ref[PT_XS:PT_XS + n_ch, lanes(i - keep)]
                term = src * cw_ref[j]
                acc = term if acc is None else acc + term
            xc[:, lanes(t)] = _silu(acc + cb_ref[...])
        ncs_ref[...] = pt_ref[PT_XS:PT_XS + n_ch, (steps - keep) * batch:steps * batch]
        dt = _softplus(pt_ref[PT_SM:PT_SM + SSD_HEADS, :] + jnp.concatenate([dtb_ref[...]] * steps, axis=1))
        dts[...] = dt
        decs[...] = jnp.exp(dt * jnp.concatenate([-jnp.exp(alog_ref[...])] * steps, axis=1))

    grp_row = (hd // (SSD_HEADS // SSD_GROUPS)) * SSD_STATE
    xh = xc[pl.ds(pl.multiple_of(hd * SSD_HEAD_DIM, SSD_HEAD_DIM), SSD_HEAD_DIM), :]
    bh = xc[pl.ds(pl.multiple_of(D_MODEL + grp_row, SSD_STATE), SSD_STATE), :]
    ch = xc[pl.ds(pl.multiple_of(D_MODEL + SSD_GROUPS * SSD_STATE + grp_row, SSD_STATE), SSD_STATE), :]
    dth = dts[pl.ds(hd, 1), :]
    dech = decs[pl.ds(hd, 1), :]
    d_skip = dexp_ref[pl.ds(hd, 1), :]
    y_rows = [[] for _ in range(steps)]
    for p in range(SSD_HEAD_DIM):
        h = h0_ref[p]
        for t in range(steps):
            x_row = xh[p:p + 1, lanes(t)]
            h = dech[:, lanes(t)] * h + (x_row * dth[:, lanes(t)]) * bh[:, lanes(t)]
            y_rows[t].append(jnp.sum(ch[:, lanes(t)] * h, axis=0, keepdims=True) + d_skip * x_row)
        hout_ref[p] = h
    rows = pl.ds(pl.multiple_of(hd * SSD_HEAD_DIM, SSD_HEAD_DIM), SSD_HEAD_DIM)
    for t in range(steps):
        ysc[rows, lanes(t)] = jnp.concatenate(y_rows[t], axis=0)

    @pl.when(hd == SSD_HEADS - 1)
    def _():
        y = ysc[...] * _silu(pt_ref[0:D_MODEL, :])
        y = y * lax.rsqrt(jnp.mean(jnp.square(y), axis=0, keepdims=True) + EPS)
        y_ref[...] = (y * jnp.concatenate([nw_ref[...]] * steps, axis=1)).astype(y_ref.dtype)


def _ssd_lanes(pt, cst, h0, w, layer, prev, *, steps, batch):
    n_ch = D_MODEL + SSD_BC
    tb = steps * batch
    keep = SSD_CONV - 1
    assert steps >= keep and batch % LANES == 0
    kern = functools.partial(_ssd_lanes_kernel, steps=steps, batch=batch)
    full = lambda *shape: pl.BlockSpec(shape, lambda hd: (0,) * len(shape))
    hblock = pl.BlockSpec((None, None, SSD_HEAD_DIM, SSD_STATE, batch), lambda hd: (layer, hd, 0, 0, 0))
    return _stacked_call(
        kern, name="ssd_lanes", grid=(SSD_HEADS,),
        inputs=[pt, cst, h0, w["cw_b"], w["cb_b"], w["dtb_b"], w["alog_b"], w["dexp_b"], w["nw_b"]],
        in_specs=[full(PT_ROWS, tb), _layer_spec(layer, n_ch, keep * batch), hblock,
                  _layer_spec(layer, SSD_CONV, n_ch, batch), _layer_spec(layer, n_ch, batch),
                  _layer_spec(layer, SSD_HEADS, batch), _layer_spec(layer, SSD_HEADS, batch),
                  _layer_spec(layer, SSD_HEADS, batch), _layer_spec(layer, D_MODEL, batch)],
        out_shape=(jax.ShapeDtypeStruct((D_MODEL, tb), bf16),
                   jax.ShapeDtypeStruct((DEPTH, n_ch, keep * batch), f32),
                   jax.ShapeDtypeStruct((DEPTH, SSD_HEADS, SSD_HEAD_DIM, SSD_STATE, batch), f32)),
        out_specs=(full(D_MODEL, tb), _layer_spec(layer, n_ch, keep * batch), hblock),
        stacked={1: prev and prev[0], 2: prev and prev[1]},
        scratch_shapes=[pltpu.VMEM((n_ch, tb), f32), pltpu.VMEM((SSD_HEADS, tb), f32),
                        pltpu.VMEM((SSD_HEADS, tb), f32), pltpu.VMEM((D_MODEL, tb), f32)],
        dimension_semantics=("arbitrary",))


def _mlstm_kernel(*refs, q, lr, nc, has_state, gs, cps, perm):
    seq = functools.partial(_mlstm_seq, q=q, lr=lr, nc=nc, has_state=has_state, cps=cps, perm=perm)
    _per_sequence(seq, refs, 5, 3, 2, gs, has_state, nc, cps * q)


def _mlstm_seq(*refs, q, lr, nc, has_state, cps, perm):
    q_ref, k_ref, v_ref, o_ref, sm_ref = refs[:5]
    refs = refs[5:]
    if has_state:
        c0_ref, n0_ref, m0_ref = refs[:3]
        refs = refs[3:]
    gb_ref, nw_ref, h_ref, cout_ref, nout_ref, mout_ref, cs, ns, ms = refs
    direct = has_state and nc == 1 and cps == 1

    def init():
        if direct:
            return
        if has_state:
            cs[...] = c0_ref[...]
            ns[...] = n0_ref[...]
            ms[...] = m0_ref[...]
        else:
            cs[...] = jnp.zeros_like(cs)
            ns[...] = jnp.zeros_like(ns)
            ms[...] = jnp.zeros_like(ms)

    def final():
        if direct:
            return
        cout_ref[...] = cs[...]
        nout_ref[...] = ns[...]
        mout_ref[...] = ms[...]

    def body():
        chunks = []
        for k in range(cps):
            sub = lambda r, k=k: r.rows(k * q, q)
            is_last = (pl.program_id(1) == nc - 1) if k == cps - 1 else False
            src = (c0_ref, n0_ref, m0_ref) if direct else (cs, ns, ms)
            dst = (cout_ref, nout_ref, mout_ref) if direct else (cs, ns, ms)
            chunks.append(_mlstm_body(sub(q_ref), sub(k_ref), sub(v_ref), sub(o_ref), sub(sm_ref), gb_ref, nw_ref,
                                      sub(h_ref), src, dst, q=q, lr=lr, is_last=is_last, perm=perm))
        yield from _staggered(chunks, MLSTM_STATE_STAGES)

    return init, body, final


MLSTM_STATE_STAGES = 8


def _staggered(gens, skew):
    done = [False] * len(gens)
    t = 0
    while not all(done):
        for i, g in enumerate(gens):
            if done[i] or t < i * skew:
                continue
            try:
                next(g)
            except StopIteration:
                done[i] = True
        t += 1
        yield


def _mlstm_body(q_ref, k_ref, v_ref, o_ref, sm_ref, gb_ref, nw_ref, h_ref, src, dst, *, q, lr, is_last, perm):
    c_src, n_src, m_src = src
    c_dst, n_dst, m_dst = dst
    sm = sm_ref[...] + gb_ref[...]
    logf = -_softplus(-sm)
    ipre = sm
    if lr < q:
        valid = _valid_rows(q, SMALL, lr, is_last)
        logf = jnp.where(valid, logf, 0.0)
        ipre = jnp.where(valid, ipre, NEG_BIG)
    causal = _tri(q, perm)
    tril = jnp.where(causal, 1.0, 0.0).astype(bf16)
    yield
    bcum = _dot01_lhs(tril, logf)
    ipre_t = ipre.T
    yield
    bcum_t = bcum.T
    lane = lax.broadcasted_iota(jnp.int32, (1, SMALL), 1)
    k_scale = MLSTM_HEAD_DIM ** -0.5

    heads = range(MLSTM_HEADS)
    sls = [slice(h * MLSTM_HEAD_DIM, (h + 1) * MLSTM_HEAD_DIM) for h in heads]
    q_all, k_all, v_all, o_all = q_ref[...], k_ref[...], v_ref[...], o_ref[...]
    qs = [q_all[:, sl] for sl in sls]
    ks = [k_all[:, sl] * k_scale for sl in sls]
    vs = [v_all[:, sl] for sl in sls]
    b_cols = [bcum[:, F_OFF + h:F_OFF + h + 1] for h in heads]
    i_cols = [ipre[:, I_OFF + h:I_OFF + h + 1] for h in heads]
    dmats = [jnp.where(causal, b_cols[h] - bcum_t[F_OFF + h:F_OFF + h + 1, :] + ipre_t[I_OFF + h:I_OFF + h + 1, :],
                       -jnp.inf) for h in heads]
    yield
    qk = [lax.dot_general(qs[h], ks[h], NT_DIMS, preferred_element_type=f32) for h in heads]
    d_max = [jnp.max(dmats[h], axis=-1, keepdims=True) for h in heads]
    yield
    n_all = n_src[...]
    m_all = m_src[...]
    m_new = m_all
    cs_in = [c_src[h] for h in heads]
    m_prevs = [m_all[:, h:h + 1] for h in heads]
    qc = [_dot(qs[h], cs_in[h].astype(bf16)) for h in heads]
    inters = [b_cols[h] + m_prevs[h] for h in heads]
    m_ts = [jnp.maximum(inters[h], d_max[h]) for h in heads]
    yield
    w_inters = [jnp.exp(inters[h] - m_ts[h]) for h in heads]
    ss = [qk[h] * jnp.exp(dmats[h] - m_ts[h]) for h in heads]
    yield
    sv = [_dot(ss[h].astype(bf16), vs[h]) for h in heads]
    m_ends = [m_ts[h][q - 1:q, :] for h in heads]
    b_lasts = [b_cols[h][q - 1:q, :] for h in heads]
    kws = [ks[h].astype(f32) * jnp.exp(b_lasts[h] - b_cols[h] + i_cols[h] - m_ends[h]) for h in heads]
    yield
    kv = [lax.dot_general(kws[h].astype(bf16), vs[h], TN_DIMS, preferred_element_type=f32) for h in heads]
    qns = [jnp.sum(qs[h].astype(f32) * n_all[h:h + 1, :], axis=-1, keepdims=True) for h in heads]
    yield
    dens = [jnp.sum(ss[h], axis=-1, keepdims=True) + w_inters[h] * qns[h] for h in heads]
    yield
    hvs = [(sv[h] + w_inters[h] * qc[h]) / jnp.maximum(jnp.abs(dens[h]), jnp.exp(-m_ts[h])) for h in heads]
    yield
    rms = [lax.rsqrt(jnp.mean(jnp.square(hvs[h]), axis=-1, keepdims=True) + EPS) for h in heads]
    yield
    h_new = [(hvs[h] * rms[h] * nw_ref[:, sls[h]] * jax.nn.sigmoid(o_all[:, sls[h]])).astype(h_ref.dtype)
             for h in heads]
    w_cs = [jnp.exp(b_lasts[h] + m_prevs[h] - m_ends[h]) for h in heads]
    yield
    for h in heads:
        c_dst[h] = w_cs[h] * cs_in[h] + kv[h]
        m_new = jnp.where(lane == h, m_ends[h], m_new)
    h_ref[...] = jnp.concatenate(h_new, axis=1)
    n_dst[...] = jnp.concatenate(
        [w_cs[h] * n_all[h:h + 1, :] + jnp.sum(kws[h], axis=0, keepdims=True) for h in heads], axis=0)
    m_dst[...] = m_new


def _merge_kernel(ys_ref, hm_ref, ga_ref, gb_ref, x_ref, wa_ref, wb_ref, wo_ref, g_ref, b_ref, o_ref):
    tm = x_ref.shape[0]
    rows = [pl.ds(i * (tm // MERGE_PARTS), tm // MERGE_PARTS) for i in range(MERGE_PARTS)]
    br = [(_dot(ys_ref[r, :], wa_ref[...]), _dot(hm_ref[r, :], wb_ref[...])) for r in rows]
    merged = [(jax.nn.sigmoid(ga_ref[r, :]) * a + jax.nn.sigmoid(gb_ref[r, :]) * b).astype(bf16)
              for r, (a, b) in zip(rows, br)]
    mix = [_dot(m, wo_ref[...]) for m in merged]
    for r, m in zip(rows, mix):
        o_ref[r, :] = _layer_norm(ALPHA * x_ref[r, :] + m, g_ref[...], b_ref[...])


def _merge(ys, hm, p32, x, w, layer, tm):
    m = x.shape[0]
    row = lambda blk: pl.BlockSpec((tm, D_MODEL), lambda i: (i, blk))
    return pl.pallas_call(
        _merge_kernel,
        out_shape=jax.ShapeDtypeStruct((m, D_MODEL), f32),
        grid=(m // tm,),
        in_specs=[row(0), row(0), row(P32_GA), row(P32_GB), row(0),
                  _layer_spec(layer, D_MODEL, D_MODEL), _layer_spec(layer, D_MODEL, D_MODEL),
                  _layer_spec(layer, D_MODEL, D_MODEL), _layer_spec(layer, 1, D_MODEL),
                  _layer_spec(layer, 1, D_MODEL)],
        out_specs=row(0),
        compiler_params=pltpu.CompilerParams(dimension_semantics=("parallel",)),
        name="merge",
    )(ys, hm, p32, p32, x, w["wa"], w["wb"], w["wo"], w["ln1_g"], w["ln1_b"])


def _ffn_kernel(*refs, tm, seq_len, perm_q):
    multi = seq_len > 0
    keep_rows = (FFN_CONV - 1) * SUBLANES
    if multi:
        x_ref, st_ref, wup_ref, cw_ref, cb_ref, wdn_ref, g_ref, b_ref, o_ref, sout_ref, xp = refs
    else:
        x_ref, wup_ref, cw_ref, cb_ref, wdn_ref, g_ref, b_ref, o_ref, sout_ref, xp, carry = refs
        @pl.when(pl.program_id(1) == 0)
        def _():
            carry[...] = jnp.zeros_like(carry)

    hdr = SUBLANES
    x = x_ref[...]
    xb = x.astype(bf16)
    if multi:
        assert seq_len & (seq_len - 1) == 0
        nseq = tm // seq_len
        t = lax.broadcasted_iota(jnp.int32, (tm, FF_CH), 0) & (seq_len - 1)
        row = lax.broadcasted_iota(jnp.int32, (tm, 2 * nseq), 0)
        col = lax.broadcasted_iota(jnp.int32, (tm, 2 * nseq), 1)
        t_sel = row & (seq_len - 1)
        seq0 = lax.shift_right_logical(row - t_sel, (seq_len // 2).bit_length() - 1)
        sel_p2 = jnp.where(col == seq0 + t_sel, jnp.where(t_sel < 2, 1.0, 0.0), 0.0).astype(bf16)
        sel_p1 = jnp.where(col == seq0 + 1, jnp.where(t_sel == 0, 1.0, 0.0), 0.0).astype(bf16)
        xp[:, 0:hdr, :] = jnp.zeros((FF_XP_SLOTS, hdr, FF_CH), f32)

    def cols_of(c, part):
        return slice(part * D_FF + c * FF_CH, part * D_FF + (c + 1) * FF_CH)

    def up(c):
        return [_dot(xb, wup_ref[:, cols_of(c, part)]) for part in range(2)]

    def conv_act(c, us):
        halves = []
        for part, u in enumerate(us):
            cols = cols_of(c, part)
            slot = (2 * c + part) % FF_XP_SLOTS
            if perm_q:
                prev = carry[:, cols]
                p1, p2 = [], []
                for kk in range(tm // perm_q):
                    uc = u[kk * perm_q:(kk + 1) * perm_q, :]
                    wrapped = _wrap_rows(uc[perm_q - keep_rows:, :], prev)
                    p1.append(_shift_back(uc, wrapped, 1))
                    p2.append(_shift_back(uc, wrapped, 2))
                    prev = uc[perm_q - keep_rows:, :]
                carry[:, cols] = prev
                p1, p2 = jnp.concatenate(p1, axis=0), jnp.concatenate(p2, axis=0)
                w = cw_ref[:, cols]
                halves.append(p2 * w[0:1, :] + p1 * w[1:2, :] + u * w[2:3, :] + cb_ref[:, cols])
                continue
            xp[slot, hdr:hdr + tm, :] = u
            if multi:
                sout_ref[:, cols] = u
                st = st_ref[:, cols]
                p1 = jnp.where(t == 0, _dot01_lhs(sel_p1, st), xp[slot, hdr - 1:hdr - 1 + tm, :])
                p2 = jnp.where(t < 2, _dot01_lhs(sel_p2, st), xp[slot, hdr - 2:hdr - 2 + tm, :])
            else:
                xp[slot, 0:hdr, :] = carry[:, cols]
                p1 = xp[slot, hdr - 1:hdr - 1 + tm, :]
                p2 = xp[slot, hdr - 2:hdr - 2 + tm, :]
                carry[:, cols] = u[tm - hdr:tm, :]
            w = cw_ref[:, cols]
            halves.append(p2 * w[0:1, :] + p1 * w[1:2, :] + u * w[2:3, :] + cb_ref[:, cols])
        return (_silu(halves[0]) * halves[1]).astype(bf16)

    acc = None
    ahead = [up(c) for c in range(min(FF_UP_AHEAD, FF_NCH))]
    pending = None
    for c in range(FF_NCH):
        if c + FF_UP_AHEAD < FF_NCH:
            ahead.append(up(c + FF_UP_AHEAD))
        if pending is not None:
            d = _dot(pending, wdn_ref[(c - 1) * FF_CH:c * FF_CH, :])
            acc = d if acc is None else acc + d
        pending = conv_act(c, ahead.pop(0))
    acc = acc + _dot(pending, wdn_ref[(FF_NCH - 1) * FF_CH:FF_NCH * FF_CH, :])

    if not multi:
        sout_ref[...] = carry[...]
    o_ref[...] = _layer_norm(ALPHA * x + acc, g_ref[...], b_ref[...])


def _ffn(x, st, w, layer, *, groups, tm, seq_len, perm_q=0):
    m = x.shape[0]
    tiles = m // (groups * tm)
    multi = seq_len > 0
    kern = functools.partial(_ffn_kernel, tm=tm, seq_len=seq_len, perm_q=perm_q)
    carry_rows = (FFN_CONV - 1) * SUBLANES if perm_q else SUBLANES
    once = dict(pipeline_mode=pl.Buffered(1))
    x_spec = pl.BlockSpec((tm, D_MODEL), lambda s, j: (s * tiles + j, 0))
    w_specs = [_layer_spec(layer, D_MODEL, 2 * D_FF, **once), _layer_spec(layer, FFN_CONV, 2 * D_FF, **once),
               _layer_spec(layer, 1, 2 * D_FF, **once), _layer_spec(layer, D_FF, D_MODEL, **once),
               _layer_spec(layer, 1, D_MODEL, **once), _layer_spec(layer, 1, D_MODEL, **once)]
    w_args = (w["wup"], w["fcw"], w["fcb"], w["wdn"], w["ln2_g"], w["ln2_b"])
    xp = pltpu.VMEM((FF_XP_SLOTS, SUBLANES + (0 if perm_q else tm), FF_CH), f32)
    if multi:
        nst = 2 * (tm // seq_len)
        inputs = (x, st) + w_args
        in_specs = [x_spec, pl.BlockSpec((None, nst, 2 * D_FF), lambda s, j: (layer, s * tiles + j, 0))] + w_specs
        sout_shape = jax.ShapeDtypeStruct((m, 2 * D_FF), f32)
        sout_spec = pl.BlockSpec((tm, 2 * D_FF), lambda s, j: (s * tiles + j, 0))
        scratch = [xp]
    else:
        inputs = (x,) + w_args
        in_specs = [x_spec] + w_specs
        sout_shape = jax.ShapeDtypeStruct((groups, carry_rows, 2 * D_FF), f32)
        sout_spec = pl.BlockSpec((None, carry_rows, 2 * D_FF), lambda s, j: (s, 0, 0))
        scratch = [xp, pltpu.VMEM((carry_rows, 2 * D_FF), f32)]
    return pl.pallas_call(
        kern,
        out_shape=(jax.ShapeDtypeStruct((m, D_MODEL), f32), sout_shape),
        grid=(groups, tiles),
        in_specs=in_specs,
        out_specs=(x_spec, sout_spec),
        scratch_shapes=scratch,
        compiler_params=pltpu.CompilerParams(dimension_semantics=("parallel", "arbitrary"),
                                             vmem_limit_bytes=56 * 1024 * 1024),
        name="ffn",
    )(*inputs)


def _pad_lanes(v, off, width=SMALL):
    out = jnp.zeros((v.shape[0], 1, width), f32)
    return out.at[:, 0, off:off + v.shape[1]].set(v.astype(f32))


def _prep_weights(w_in, ssd_conv_w, ssd_conv_b, ssd_dt_bias, ssd_a_log, ssd_d, ssd_norm_w, mlstm_gate_b,
                  mlstm_norm_w, w_branch_a, w_branch_b, w_out, ln1_g, ln1_b, ffn_w_up, ffn_conv_w,
                  ffn_conv_b, ffn_w_down, ln2_g, ln2_b):
    d = D_MODEL
    o_z, o_xbc, o_dt = 0, d, d + d + SSD_BC
    o_q = o_dt + SSD_HEADS
    o_if = o_q + 3 * d
    o_o = o_if + 2 * MLSTM_HEADS
    o_g = o_o + d
    w_t = jnp.swapaxes(w_in, 1, 2)
    cols = lambda a, n: w_t[:, a:a + n, :]
    zeros = lambda n: jnp.zeros((DEPTH, n, d), w_in.dtype)
    w32 = jnp.concatenate([cols(o_z, d), cols(o_o, d), cols(o_g, 2 * d), cols(o_xbc, d + SSD_BC),
                           cols(o_dt, SSD_HEADS), cols(o_if, 2 * MLSTM_HEADS),
                           zeros(P32_W - P32_SM_OFF - SSD_HEADS - 2 * MLSTM_HEADS)], axis=1).astype(bf16)
    e = (np.arange(SSD_HP)[None, :] // SSD_HEAD_DIM == np.arange(LANES)[:, None])
    bd = ((np.arange(SSD_GROUPS * SSD_STATE)[:, None] < SSD_STATE)
          == (np.arange(SSD_HP)[None, :] < SSD_HP // SSD_GROUPS))
    row = lambda a: a[:, None, :]
    return dict(
        w32=w32, wqkv=cols(o_q, 3 * d).astype(bf16),
        cwx=ssd_conv_w[:, :, :d], cbx=row(ssd_conv_b[:, :d]),
        cwb=ssd_conv_w[:, :, d:], cbb=row(ssd_conv_b[:, d:]),
        dtb=_pad_lanes(ssd_dt_bias, DT_OFF), alog=_pad_lanes(ssd_a_log, DT_OFF),
        dexp=row(jnp.repeat(ssd_d.astype(f32), SSD_HEAD_DIM, axis=1)), ssd_nw=row(ssd_norm_w),
        e=jnp.asarray(e, bf16), bd=jnp.asarray(bd, f32),
        gate_b=_pad_lanes(mlstm_gate_b, I_OFF), mlstm_nw=row(mlstm_norm_w),
        wa=w_branch_a.astype(bf16), wb=w_branch_b.astype(bf16), wo=w_out.astype(bf16),
        ln1_g=row(ln1_g), ln1_b=row(ln1_b),
        wup=ffn_w_up.astype(bf16), fcw=ffn_conv_w, fcb=row(ffn_conv_b), wdn=ffn_w_down.astype(bf16),
        ln2_g=row(ln2_g), ln2_b=row(ln2_b),
    )


class _Group:
    def __init__(self, batch, length, q, lr, gs, ssd_cps, mlstm_cps, proj_tm, p32_tn, qkv_tn, merge_tm, ffn,
                 perm=False, lanes_ssd=False):
        self.perm = perm
        self.lanes_ssd = lanes_ssd
        self.batch, self.length, self.q, self.lr, self.gs = batch, length, q, lr, gs
        self.ssd_cps, self.mlstm_cps = ssd_cps, mlstm_cps
        self.rows = batch * length
        self.proj_tm, self.p32_tn, self.qkv_tn, self.merge_tm, self.ffn = proj_tm, p32_tn, qkv_tn, merge_tm, ffn

    def cfg(self, name, layer):
        v = getattr(self, name)
        return v[layer] if isinstance(v, tuple) else v

    def tiling(self, cps):
        rows = cps * self.q
        gs, steps = self.gs, self.length // rows
        assert gs == 1 or steps == 1
        return steps, lambda width, blk: pl.BlockSpec((gs * rows, width), lambda b, c: (b * steps + c, blk))


def _ssd(grp, p32, state, w, layer, prev):
    q, b, gs = grp.q, grp.batch, grp.gs
    has_state = state is not None
    cps = grp.cfg("ssd_cps", layer)
    steps, tile = grp.tiling(cps)
    kern = functools.partial(_ssd_kernel, q=q, lr=grp.lr, nc=steps, has_state=has_state, gs=gs, cps=cps,
                             perm=grp.perm)
    inputs = [p32, p32, p32, p32]
    in_specs = [tile(D_MODEL, P32_Z), tile(D_MODEL, P32_XS),
                tile(SSD_BC, P32_BC_OFF // SSD_BC), tile(SMALL, P32_SM_OFF // SMALL)]
    if has_state:
        inputs += [state["csx"], state["csb"], state["h"]]
        in_specs += [_seq_spec(layer, gs, SSD_CONV - 1, D_MODEL), _seq_spec(layer, gs, SSD_CONV - 1, SSD_BC),
                     _seq_spec(layer, gs, SSD_HP, SSD_STATE)]
    inputs += [w["cwx"], w["cbx"], w["cwb"], w["cbb"], w["dtb"], w["alog"], w["dexp"], w["ssd_nw"], w["e"],
               w["bd"]]
    const = lambda *shape: pl.BlockSpec(shape, lambda b, c: (0,) * len(shape))
    in_specs += [_layer_spec(layer, SSD_CONV, D_MODEL), _layer_spec(layer, 1, D_MODEL),
                 _layer_spec(layer, SSD_CONV, SSD_BC), _layer_spec(layer, 1, SSD_BC),
                 _layer_spec(layer, 1, SMALL), _layer_spec(layer, 1, SMALL), _layer_spec(layer, 1, D_MODEL),
                 _layer_spec(layer, 1, D_MODEL), const(LANES, SSD_HP), const(SSD_GROUPS * SSD_STATE, SSD_HP)]
    return _stacked_call(
        kern, name="ssd", grid=(b // gs, steps), inputs=inputs, in_specs=in_specs,
        out_shape=(jax.ShapeDtypeStruct((grp.rows, D_MODEL), bf16),
                   jax.ShapeDtypeStruct((DEPTH, b, SSD_CONV - 1, D_MODEL), f32),
                   jax.ShapeDtypeStruct((DEPTH, b, SSD_CONV - 1, SSD_BC), f32),
                   jax.ShapeDtypeStruct((DEPTH, b, SSD_HP, SSD_STATE), f32)),
        out_specs=(tile(D_MODEL, 0), _seq_spec(layer, gs, SSD_CONV - 1, D_MODEL),
                   _seq_spec(layer, gs, SSD_CONV - 1, SSD_BC), _seq_spec(layer, gs, SSD_HP, SSD_STATE)),
        stacked={1: prev and prev[0], 2: prev and prev[1], 3: prev and prev[2]},
        scratch_shapes=[pltpu.VMEM((gs, SUBLANES + q, D_MODEL), f32), pltpu.VMEM((gs, SUBLANES + q, SSD_BC), f32),
                        pltpu.VMEM((gs, SSD_GROUPS * SSD_STATE, SSD_HP), f32), pltpu.VMEM((gs, q, D_MODEL), f32)],
        dimension_semantics=("parallel", "arbitrary"))


def _mlstm(grp, qkv, p32, state, w, layer, prev):
    q, b, gs = grp.q, grp.batch, grp.gs
    has_state = state is not None
    cps = grp.cfg("mlstm_cps", layer)
    steps, tile = grp.tiling(cps)
    kern = functools.partial(_mlstm_kernel, q=q, lr=grp.lr, nc=steps, has_state=has_state, gs=gs, cps=cps,
                             perm=grp.perm)
    hd = MLSTM_HEAD_DIM
    carried = not (has_state and steps == 1 and cps == 1)
    inputs = [qkv, qkv, qkv, p32, p32]
    in_specs = [tile(D_MODEL, 0), tile(D_MODEL, 1), tile(D_MODEL, 2), tile(D_MODEL, P32_O),
                tile(SMALL, P32_SM_OFF // SMALL)]
    if has_state:
        inputs += [state["c"], state["n"], state["m"]]
        in_specs += [_seq_spec(layer, gs, MLSTM_HEADS, hd, hd), _seq_spec(layer, gs, MLSTM_HEADS, hd),
                     _seq_spec(layer, gs, 1, SMALL)]
    inputs += [w["gate_b"], w["mlstm_nw"]]
    in_specs += [_layer_spec(layer, 1, SMALL), _layer_spec(layer, 1, D_MODEL)]
    return _stacked_call(
        kern, name="mlstm", grid=(b // gs, steps), inputs=inputs, in_specs=in_specs,
        out_shape=(jax.ShapeDtypeStruct((grp.rows, D_MODEL), bf16),
                   jax.ShapeDtypeStruct((DEPTH, b, MLSTM_HEADS, hd, hd), f32),
                   jax.ShapeDtypeStruct((DEPTH, b, MLSTM_HEADS, hd), f32),
                   jax.ShapeDtypeStruct((DEPTH, b, 1, SMALL), f32)),
        out_specs=(tile(D_MODEL, 0), _seq_spec(layer, gs, MLSTM_HEADS, hd, hd),
                   _seq_spec(layer, gs, MLSTM_HEADS, hd), _seq_spec(layer, gs, 1, SMALL)),
        stacked={1: prev and prev[0], 2: prev and prev[1], 3: prev and prev[2]},
        scratch_shapes=[pltpu.VMEM((gs, MLSTM_HEADS, hd, hd) if carried else (gs, 1, SUBLANES, LANES), f32),
                        pltpu.VMEM((gs, MLSTM_HEADS, hd), f32), pltpu.VMEM((gs, 1, SMALL), f32)],
        dimension_semantics=("parallel", "arbitrary"))


def _trunk(grp, x, state, w):
    ssd_out = mlstm_out = None
    ffn_out = []
    for layer in range(DEPTH):
        p32 = _proj(x, w["w32"], layer, f32, grp.cfg("proj_tm", layer), grp.cfg("p32_tn", layer))
        qkv = _proj(x, w["wqkv"], layer, bf16, grp.cfg("proj_tm", layer), grp.cfg("qkv_tn", layer))
        if grp.lanes_ssd:
            b, t = grp.batch, grp.length
            x_tm = x.reshape(b, t, D_MODEL).swapaxes(0, 1).reshape(t * b, D_MODEL)
            ys_t, *ssd_out = _ssd_lanes(_proj_t(x_tm, w["w32"], layer), state["cs_t"], state["h_lanes"], w, layer,
                                        ssd_out, steps=t, batch=b)
            ys = ys_t.reshape(D_MODEL, t, b).transpose(2, 1, 0).reshape(b * t, D_MODEL)
        else:
            ys, *ssd_out = _ssd(grp, p32, state, w, layer, ssd_out)
        hm, *mlstm_out = _mlstm(grp, qkv, p32, state, w, layer, mlstm_out)
        x1 = _merge(ys, hm, p32, x, w, layer, grp.cfg("merge_tm", layer))
        x, s_ffn = _ffn(x1, state["ffn"] if state is not None else None, w, layer, **grp.ffn)
        ffn_out.append(s_ffn)
    return x, ssd_out, mlstm_out, ffn_out


def _unpack_states(batch, ssd_out, mlstm_out):
    csx, csb, h = ssd_out
    c, n, m = mlstm_out
    return (h.reshape(DEPTH, batch, SSD_HEADS, SSD_HEAD_DIM, SSD_STATE),
            jnp.concatenate([csx, csb], axis=-1), c, n, m[:, :, 0, :MLSTM_HEADS])


def kernel(x_prompt, x_sample, state_ssd, state_ssd_conv, state_mlstm_c, state_mlstm_n, state_mlstm_m,
           state_ffn_conv, w_in, ssd_conv_w, ssd_conv_b, ssd_dt_bias, ssd_a_log, ssd_d, ssd_norm_w,
           mlstm_gate_b, mlstm_norm_w, w_branch_a, w_branch_b, w_out, ln1_g, ln1_b, ffn_w_up, ffn_conv_w,
           ffn_conv_b, ffn_w_down, ln2_g, ln2_b):
    w = _prep_weights(w_in, ssd_conv_w, ssd_conv_b, ssd_dt_bias, ssd_a_log, ssd_d, ssd_norm_w, mlstm_gate_b,
                      mlstm_norm_w, w_branch_a, w_branch_b, w_out, ln1_g, ln1_b, ffn_w_up, ffn_conv_w,
                      ffn_conv_b, ffn_w_down, ln2_g, ln2_b)
    keep = FFN_CONV - 1
    keep_ssd = SSD_CONV - 1

    bp, lp, _ = x_prompt.shape
    prompt = _Group(bp, lp, CHUNK, CHUNK, gs=1, ssd_cps=4, mlstm_cps=4, proj_tm=2048, p32_tn=1408, qkv_tn=1536,
                    merge_tm=1024, ffn=dict(groups=bp, tm=1024, seq_len=0, perm_q=CHUNK), perm=True)
    per = CHUNK // SUBLANES
    xp_rows = x_prompt.reshape(bp, lp // CHUNK, SUBLANES, per, D_MODEL).swapaxes(2, 3)
    y_p, ssd_p, mlstm_p, ffn_p = _trunk(prompt, xp_rows.reshape(bp * lp, D_MODEL), None, w)
    y_p = y_p.reshape(bp, lp // CHUNK, per, SUBLANES, D_MODEL).swapaxes(2, 3)
    st_p = _unpack_states(bp, ssd_p, mlstm_p)
    ffn_conv_p = jnp.stack(ffn_p)[:, :, SUBLANES - 1::SUBLANES, :]

    bs, ls, _ = x_sample.shape
    lpad = max(ls, SAMPLE_PAD_LEN)
    s_rows = bs * lpad
    sample = _Group(bs, lpad, lpad, ls, gs=8, ssd_cps=1, mlstm_cps=1, proj_tm=s_rows, p32_tn=512, qkv_tn=1024,
                    merge_tm=512,
                    ffn=dict(groups=1, tm=256, seq_len=lpad), lanes_ssd=True)
    assert lpad == ls
    lane_b = lambda a: jnp.broadcast_to(a.astype(f32)[..., None], a.shape + (bs,))
    w.update(cw_b=lane_b(ssd_conv_w), cb_b=lane_b(ssd_conv_b), dtb_b=lane_b(ssd_dt_bias), alog_b=lane_b(ssd_a_log),
             dexp_b=lane_b(ssd_d), nw_b=lane_b(ssd_norm_w))
    s_state = dict(
        cs_t=jnp.transpose(state_ssd_conv, (0, 3, 2, 1)).reshape(DEPTH, D_MODEL + SSD_BC, keep_ssd * bs),
        h_lanes=jnp.transpose(state_ssd, (0, 2, 3, 4, 1)),
        c=state_mlstm_c, n=state_mlstm_n,
        m=jnp.pad(state_mlstm_m, ((0, 0), (0, 0), (0, SMALL - MLSTM_HEADS)))[:, :, None, :],
        ffn=state_ffn_conv.reshape(DEPTH, bs * keep, 2 * D_FF),
    )
    xs = jnp.pad(x_sample, ((0, 0), (0, lpad - ls), (0, 0))).reshape(s_rows, D_MODEL)
    y_s, (cs_t, h_lanes), mlstm_s, ffn_s = _trunk(sample, xs, s_state, w)
    c_s, n_s, m_s = mlstm_s
    st_s = (jnp.transpose(h_lanes, (0, 4, 1, 2, 3)),
            jnp.transpose(cs_t.reshape(DEPTH, D_MODEL + SSD_BC, keep_ssd, bs), (0, 3, 2, 1)),
            c_s, n_s, m_s[:, :, 0, :MLSTM_HEADS])
    ffn_conv_s = jnp.stack([u.reshape(bs, lpad, 2 * D_FF)[:, ls - keep:ls, :] for u in ffn_s])
    y_sample = y_s.reshape(bs, lpad, D_MODEL)[:, :ls, :]

    return (y_p.reshape(bp, lp, D_MODEL), y_sample, st_p[0], st_s[0], st_p[1], st_s[1], st_p[2], st_s[2],
            st_p[3], st_s[3], st_p[4], st_s[4], ffn_conv_p, ffn_conv_s)
```

```python
import functools
import itertools

import jax
import jax.numpy as jnp
import numpy as np
from jax import lax
from jax.experimental import pallas as pl
from jax.experimental.pallas import tpu as pltpu

f32 = jnp.float32
bf16 = jnp.bfloat16

D_MODEL = 1024
DEPTH = 2
SSD_HEADS = 16
SSD_HEAD_DIM = 64
SSD_STATE = 64
SSD_GROUPS = 2
SSD_CONV = 4
SSD_BC = 2 * SSD_GROUPS * SSD_STATE
SSD_HP = SSD_HEADS * SSD_HEAD_DIM
MLSTM_HEADS = 4
MLSTM_HEAD_DIM = 256
CHUNK = 128
D_FF = 2816
FFN_CONV = 3
ALPHA = (2 * DEPTH) ** 0.25
EPS = 1e-5

LANES = 128
SUBLANES = 8
SMALL = LANES
DT_OFF, I_OFF, F_OFF = 0, 16, 20
P32_Z, P32_O, P32_GA, P32_GB, P32_XS = 0, 1, 2, 3, 4
P32_BC_OFF = 5 * D_MODEL
P32_SM_OFF = P32_BC_OFF + SSD_BC
P32_W = P32_SM_OFF + 2 * SMALL
FF_CH = 256
FF_NCH = D_FF // FF_CH
MERGE_PARTS = 2
FF_UP_AHEAD = 2
FF_XP_SLOTS = 4
NEG_BIG = -1e30
SAMPLE_PAD_LEN = 4

NT_DIMS = (((1,), (1,)), ((), ()))
TN_DIMS = (((0,), (0,)), ((), ()))


def _dot(a, b):
    return jnp.dot(a, b, preferred_element_type=f32)


def _split3(x):
    hi = x.astype(bf16)
    r = x - hi.astype(f32)
    mid = r.astype(bf16)
    lo = (r - mid.astype(f32)).astype(bf16)
    return hi, mid, lo


def _dot01_rhs(x, e):
    hi, mid, lo = _split3(x)
    return _dot(hi, e) + _dot(mid, e) + _dot(lo, e)


def _dot01_lhs(t, x):
    hi, mid, lo = _split3(x)
    return _dot(t, hi) + _dot(t, mid) + _dot(t, lo)


def _softplus(x):
    return jnp.maximum(x, 0.0) + jnp.log1p(jnp.exp(-jnp.abs(x)))


def _silu(x):
    return x * jax.nn.sigmoid(x)


def _row_time(i, q, perm):
    if not perm:
        return i
    return (i & (SUBLANES - 1)) * (q // SUBLANES) + lax.shift_right_logical(i, SUBLANES.bit_length() - 1)


def _tri(q, perm=False):
    row = lax.broadcasted_iota(jnp.int32, (q, q), 0)
    col = lax.broadcasted_iota(jnp.int32, (q, q), 1)
    return _row_time(row, q, perm) >= _row_time(col, q, perm)


def _wrap_rows(cur_tail, prev_tail):
    out = []
    for i in range(cur_tail.shape[0] // SUBLANES):
        rows = slice(i * SUBLANES, (i + 1) * SUBLANES)
        first = lax.broadcasted_iota(jnp.int32, (SUBLANES, cur_tail.shape[1]), 0) == 0
        out.append(jnp.where(first, pltpu.roll(prev_tail[rows], 1, axis=0), pltpu.roll(cur_tail[rows], 1, axis=0)))
    return jnp.concatenate(out, axis=0)


def _shift_back(x, wrapped, j):
    n = j * SUBLANES
    return jnp.concatenate([wrapped[wrapped.shape[0] - n:], x[:x.shape[0] - n]], axis=0)


def _valid_rows(q, width, lr, is_last):
    row = lax.broadcasted_iota(jnp.int32, (q, width), 0)
    return row < jnp.where(is_last, lr, q)


def _layer_norm(r, g, b):
    mu = jnp.mean(r, axis=-1, keepdims=True)
    var = jnp.mean(jnp.square(r - mu), axis=-1, keepdims=True)
    return (r - mu) * lax.rsqrt(var + EPS) * g + b


def _layer_spec(layer, *shape, **kw):
    zeros = (0,) * len(shape)
    return pl.BlockSpec((None,) + shape, lambda *_: (layer,) + zeros, **kw)


def _seq_spec(layer, gs, *shape):
    zeros = (0,) * len(shape)
    return pl.BlockSpec((None, gs) + shape, lambda b, c: (layer, b) + zeros)


def _stacked_call(kern, *, name, grid, inputs, in_specs, out_shape, out_specs, stacked, scratch_shapes,
                  dimension_semantics, vmem_limit_bytes=None):
    prev = [(i, a) for i, a in sorted(stacked.items()) if a is not None]
    n_in = len(inputs)

    def body(*refs):
        kern(*refs[:n_in], *refs[n_in + len(prev):])

    return pl.pallas_call(
        body,
        out_shape=out_shape,
        grid=grid,
        in_specs=list(in_specs) + [pl.BlockSpec(memory_space=pl.ANY)] * len(prev),
        out_specs=out_specs,
        scratch_shapes=scratch_shapes,
        input_output_aliases={n_in + k: i for k, (i, _) in enumerate(prev)},
        compiler_params=pltpu.CompilerParams(dimension_semantics=dimension_semantics,
                                             vmem_limit_bytes=vmem_limit_bytes),
        name=name,
    )(*inputs, *[a for _, a in prev])


def _proj_kernel(x_ref, w_ref, o_ref, xb):
    @pl.when(pl.program_id(1) == 0)
    def _():
        xb[...] = x_ref[...].astype(bf16)

    o_ref[...] = lax.dot_general(xb[...], w_ref[...], NT_DIMS, preferred_element_type=f32).astype(o_ref.dtype)


def _proj(x, wt, layer, out_dtype, tm, tn):
    m, k = x.shape
    n = wt.shape[1]
    return pl.pallas_call(
        _proj_kernel,
        out_shape=jax.ShapeDtypeStruct((m, n), out_dtype),
        grid=(m // tm, n // tn),
        in_specs=[pl.BlockSpec((tm, k), lambda i, j: (i, 0)),
                  pl.BlockSpec((None, tn, k), lambda i, j: (layer, j, 0))],
        out_specs=pl.BlockSpec((tm, tn), lambda i, j: (i, j)),
        scratch_shapes=[pltpu.VMEM((tm, k), bf16)],
        compiler_params=pltpu.CompilerParams(dimension_semantics=("parallel", "arbitrary")),
        name="proj",
    )(x, wt)


class _Rows:
    def __init__(self, ref, start, n):
        self.ref, self.start, self.n, self.dtype = ref, start, n, ref.dtype

    def rows(self, off, n):
        return _Rows(self.ref, self.start + off, n)

    def _index(self, idx):
        cols = slice(None) if idx is Ellipsis else idx[1]
        return (slice(self.start, self.start + self.n), cols)

    def __getitem__(self, idx):
        return self.ref[self._index(idx)]

    def __setitem__(self, idx, value):
        self.ref[self._index(idx)] = value


def _per_sequence(seq_fn, refs, n_tile, n_state, n_param, gs, has_state, nc, rows, tile3d=False):
    n_state = n_state if has_state else 0
    tiles, refs = refs[:n_tile], refs[n_tile:]
    state, refs = refs[:n_state], refs[n_state:]
    params, (y_tile, *rest) = refs[:n_param], refs[n_param:]
    phases = []
    for g in range(gs):
        at = lambda group: tuple(r.at[g] for r in group)
        if tile3d:
            seq_rows = lambda group: tuple(_Rows(r.at[g], 0, rows) for r in group)
        else:
            seq_rows = lambda group: tuple(_Rows(r, g * rows, rows) for r in group)
        phases.append(seq_fn(*seq_rows(tiles), *at(state), *params, *seq_rows((y_tile,)), *at(rest)))
    c = pl.program_id(1)

    @pl.when(c == 0)
    def _():
        for init, _, _ in phases:
            init()

    for _ in itertools.zip_longest(*[body() for _, body, _ in phases]):
        pass

    @pl.when(c == nc - 1)
    def _():
        for _, _, final in phases:
            final()


def _ssd_kernel(*refs, q, lr, nc, has_state, gs, cps, perm):
    seq = functools.partial(_ssd_seq, q=q, lr=lr, nc=nc, has_state=has_state, cps=cps, perm=perm)
    _per_sequence(seq, refs, 4, 3, 10, gs, has_state, nc, cps * q)


def _ssd_seq(*refs, q, lr, nc, has_state, cps, perm):
    z_ref, xs_ref, bc_ref, sm_ref = refs[:4]
    refs = refs[4:]
    if has_state:
        csx_ref, csb_ref, h0_ref = refs[:3]
        refs = refs[3:]
    (cwx_ref, cbx_ref, cwb_ref, cbb_ref, dtb_ref, alog_ref, dexp_ref, nw_ref, e_ref, bd_ref,
     y_ref, ncsx_ref, ncsb_ref, hout_ref, xpx, xpb, ht, yb) = refs
    hdr = SUBLANES
    lo = hdr - (SSD_CONV - 1)
    n2 = SSD_GROUPS * SSD_STATE
    assert lr >= SSD_CONV - 1
    keep = SSD_CONV - 1
    assert not (perm and (has_state or lr != q))
    carried = [(i + 1) * SUBLANES - 1 for i in range(keep)]

    def init():
        if has_state:
            h_t = h0_ref[...].T
            ht[...] = jnp.where(bd_ref[...] > 0.5, jnp.concatenate([h_t, h_t], axis=0), 0.0)
            xpx[lo:hdr, :] = csx_ref[...]
            xpb[lo:hdr, :] = csb_ref[...]
        else:
            ht[...] = jnp.zeros_like(ht)
            rows = slice(0, keep * SUBLANES) if perm else slice(lo, hdr)
            xpx[rows, :] = jnp.zeros((rows.stop - rows.start, SSD_HP), f32)
            xpb[rows, :] = jnp.zeros((rows.stop - rows.start, SSD_BC), f32)

    def final():
        if perm:
            for i, r in enumerate(carried):
                ncsx_ref[i:i + 1, :] = xpx[r:r + 1, :]
                ncsb_ref[i:i + 1, :] = xpb[r:r + 1, :]
        else:
            ncsx_ref[...] = xpx[lo + lr:hdr + lr, :]
            ncsb_ref[...] = xpb[lo + lr:hdr + lr, :]
        h_new = ht[...]
        hout_ref[...] = (h_new[:SSD_STATE, :] + h_new[SSD_STATE:, :]).T

    def body():
        for k in range(cps):
            sub = lambda r: r.rows(k * q, q)
            is_last = (pl.program_id(1) == nc - 1) if k == cps - 1 else False
            yield from _ssd_body(sub(z_ref), sub(xs_ref), sub(bc_ref), sub(sm_ref), cwx_ref, cbx_ref, cwb_ref,
                                 cbb_ref, dtb_ref, alog_ref, dexp_ref, nw_ref, e_ref, bd_ref, sub(y_ref),
                                 xpx, xpb, ht, yb, q=q, lr=lr, is_last=is_last, perm=perm)

    return init, body, final


def _ssd_body(z_ref, xs_ref, bc_ref, sm_ref, cwx_ref, cbx_ref, cwb_ref, cbb_ref, dtb_ref, alog_ref,
              dexp_ref, nw_ref, e_ref, bd_ref, y_ref, xpx, xpb, ht, yb, *, q, lr, is_last, perm):
    hdr = SUBLANES
    lo = hdr - (SSD_CONV - 1)
    n2 = SSD_GROUPS * SSD_STATE
    block_diag = bd_ref[...] > 0.5

    dt = _softplus(sm_ref[...] + dtb_ref[...])
    if lr < q:
        dt = jnp.where(_valid_rows(q, SMALL, lr, is_last), dt, 0.0)
    a = -jnp.exp(alog_ref[...])
    d_a = dt * a
    causal = _tri(q, perm)
    tril = jnp.where(causal, 1.0, 0.0).astype(bf16)
    e = e_ref[...]
    acs = _dot01_lhs(tril, d_a)
    dt_x = _dot01_rhs(dt, e)
    yield

    if perm:
        keep_rows = (SSD_CONV - 1) * SUBLANES

        def conv(xp, x_ref, w_ref, b_ref):
            w = w_ref[...]
            x = x_ref[...]
            wrapped = _wrap_rows(x[q - keep_rows:, :], xp[0:keep_rows, :])
            acc = _shift_back(x, wrapped, SSD_CONV - 1) * w[0:1, :]
            for j in range(1, SSD_CONV - 1):
                acc = acc + _shift_back(x, wrapped, SSD_CONV - 1 - j) * w[j:j + 1, :]
            acc = acc + x * w[SSD_CONV - 1:SSD_CONV, :]
            xp[0:keep_rows, :] = x[q - keep_rows:, :]
            return acc + b_ref[...]
    else:
        xpx[hdr:hdr + q, :] = xs_ref[...]
        xpb[hdr:hdr + q, :] = bc_ref[...]

        def conv(xp, x_ref, w_ref, b_ref):
            w = w_ref[...]
            acc = xp[lo:lo + q, :] * w[0:1, :]
            for j in range(1, SSD_CONV):
                acc = acc + xp[lo + j:lo + j + q, :] * w[j:j + 1, :]
            return acc + b_ref[...]

    cb = conv(xpb, bc_ref, cwb_ref, cbb_ref)
    bcv = _silu(cb)
    bm = bcv[:, :n2].astype(bf16)
    cm = bcv[:, n2:]
    lane_g0 = lax.broadcasted_iota(jnp.int32, (q, n2), 1) < SSD_STATE
    acs_t = acs.T
    acs_x = _dot01_rhs(acs, e)
    yield
    cbms = [lax.dot_general(jnp.where(lane_g0 if g == 0 else jnp.logical_not(lane_g0), cm, 0.0).astype(bf16),
                            bm, NT_DIMS, preferred_element_type=f32) for g in range(SSD_GROUPS)]
    h_prev = ht[...]
    y_off = _dot(cm.astype(bf16), h_prev.astype(bf16))
    cx = conv(xpx, xs_ref, cwx_ref, cbx_ref)
    if not perm:
        tail_x = xpx[lo + q:hdr + q, :]
        tail_b = xpb[lo + q:hdr + q, :]
        xpx[lo:hdr, :] = tail_x
        xpb[lo:hdr, :] = tail_b
    yield
    xs = _silu(cx)
    last_x = acs_x[q - 1:q, :]
    xdt = xs * dt_x
    xdt_b = xdt.astype(bf16)
    yield
    lane_lo = lax.broadcasted_iota(jnp.int32, (q, LANES), 1) < SSD_HEAD_DIM
    heads_per_group = SSD_HEADS // SSD_GROUPS
    decays = [jnp.exp(jnp.where(causal, acs[:, hh:hh + 1] - acs_t[hh:hh + 1, :], -jnp.inf))
              for hh in range(SSD_HEADS)]
    yield
    weights = [(cbms[hh // heads_per_group] * decays[hh]).astype(bf16) for hh in range(SSD_HEADS)]
    xdtw = (xdt * jnp.exp(last_x - acs_x)).astype(bf16)
    yield
    ys = [_dot(weights[hh], xdt_b[:, (hh // 2) * LANES:(hh // 2 + 1) * LANES]) for hh in range(SSD_HEADS)]
    upd = lax.dot_general(bm, xdtw, TN_DIMS, preferred_element_type=f32)
    yield
    for p in range(SSD_HEADS // 2):
        yb[:, p * LANES:(p + 1) * LANES] = jnp.where(lane_lo, ys[2 * p], ys[2 * p + 1])
    ht[...] = jnp.exp(last_x) * h_prev + jnp.where(block_diag, upd, 0.0)
    yield
    y = yb[...] + y_off * jnp.exp(acs_x) + dexp_ref[...] * xs
    y = y * _silu(z_ref[...])
    yield
    y = y * lax.rsqrt(jnp.mean(jnp.square(y), axis=-1, keepdims=True) + EPS) * nw_ref[...]
    y_ref[...] = y.astype(y_ref.dtype)


PT_ROWS = 5 * 512
PT_XS, PT_BC, PT_SM = D_MODEL, 2 * D_MODEL, 2 * D_MODEL + SSD_BC


def _proj_t_kernel(x_ref, w_ref, o_ref, xb):
    @pl.when(pl.program_id(0) == 0)
    def _():
        xb[...] = x_ref[...].astype(bf16)

    o_ref[...] = lax.dot_general(w_ref[...], xb[...], NT_DIMS, preferred_element_type=f32)


def _proj_t(x_tm, w32, layer):
    m, k = x_tm.shape
    tn = 512
    xs_blk = P32_XS * D_MODEL // tn
    return pl.pallas_call(
        _proj_t_kernel,
        out_shape=jax.ShapeDtypeStruct((PT_ROWS, m), f32),
        grid=(PT_ROWS // tn,),
        in_specs=[pl.BlockSpec((m, k), lambda j: (0, 0)),
                  pl.BlockSpec((None, tn, k), lambda j: (layer, jnp.where(j < D_MODEL // tn, j, j + xs_blk - D_MODEL // tn), 0))],
        out_specs=pl.BlockSpec((tn, m), lambda j: (j, 0)),
        scratch_shapes=[pltpu.VMEM((m, k), bf16)],
        compiler_params=pltpu.CompilerParams(dimension_semantics=("arbitrary",)),
        name="proj_t",
    )(x_tm, w32)


def _ssd_lanes_kernel(pt_ref, cst_ref, h0_ref, cw_ref, cb_ref, dtb_ref, alog_ref, dexp_ref, nw_ref,
                      y_ref, ncs_ref, hout_ref, xc, dts, decs, ysc, *, steps, batch):
    hd = pl.program_id(0)
    n_ch = D_MODEL + SSD_BC
    keep = SSD_CONV - 1
    lanes = lambda t: slice(t * batch, (t + 1) * batch)

    @pl.when(hd == 0)
    def _():
        for t in range(steps):
            acc = None
            for j in range(SSD_CONV):
                i = t + j
                src = cst_ref[:, lanes(i)] if i < keep else pt_ref[PT_XS:PT_XS + n_ch, lanes(i - keep)]
                term = src * cw_ref[j]
                acc = term if acc is None else acc + term
            xc[:, lanes(t)] = _silu(acc + cb_ref[...])
        ncs_ref[...] = pt_ref[PT_XS:PT_XS + n_ch, (steps - keep) * batch:steps * batch]
        dt = _softplus(pt_ref[PT_SM:PT_SM + SSD_HEADS, :] + jnp.concatenate([dtb_ref[...]] * steps, axis=1))
        dts[...] = dt
        decs[...] = jnp.exp(dt * jnp.concatenate([-jnp.exp(alog_ref[...])] * steps, axis=1))

    grp_row = (hd // (SSD_HEADS // SSD_GROUPS)) * SSD_STATE
    xh = xc[pl.ds(pl.multiple_of(hd * SSD_HEAD_DIM, SSD_HEAD_DIM), SSD_HEAD_DIM), :]
    bh = xc[pl.ds(pl.multiple_of(D_MODEL + grp_row, SSD_STATE), SSD_STATE), :]
    ch = xc[pl.ds(pl.multiple_of(D_MODEL + SSD_GROUPS * SSD_STATE + grp_row, SSD_STATE), SSD_STATE), :]
    dth = dts[pl.ds(hd, 1), :]
    dech = decs[pl.ds(hd, 1), :]
    d_skip = dexp_ref[pl.ds(hd, 1), :]
    y_rows = [[] for _ in range(steps)]
    for p in range(SSD_HEAD_DIM):
        h = h0_ref[p]
        for t in range(steps):
            x_row = xh[p:p + 1, lanes(t)]
            h = dech[:, lanes(t)] * h + (x_row * dth[:, lanes(t)]) * bh[:, lanes(t)]
            y_rows[t].append(jnp.sum(ch[:, lanes(t)] * h, axis=0, keepdims=True) + d_skip * x_row)
        hout_ref[p] = h
    rows = pl.ds(pl.multiple_of(hd * SSD_HEAD_DIM, SSD_HEAD_DIM), SSD_HEAD_DIM)
    for t in range(steps):
        ysc[rows, lanes(t)] = jnp.concatenate(y_rows[t], axis=0)

    @pl.when(hd == SSD_HEADS - 1)
    def _():
        y = ysc[...] * _silu(pt_ref[0:D_MODEL, :])
        y = y * lax.rsqrt(jnp.mean(jnp.square(y), axis=0, keepdims=True) + EPS)
        y_ref[...] = (y * jnp.concatenate([nw_ref[...]] * steps, axis=1)).astype(y_ref.dtype)


def _ssd_lanes(pt, cst, h0, w, layer, prev, *, steps, batch):
    n_ch = D_MODEL + SSD_BC
    tb = steps * batch
    keep = SSD_CONV - 1
    assert steps >= keep and batch % LANES == 0
    kern = functools.partial(_ssd_lanes_kernel, steps=steps, batch=batch)
    full = lambda *shape: pl.BlockSpec(shape, lambda hd: (0,) * len(shape))
    hblock = pl.BlockSpec((None, None, SSD_HEAD_DIM, SSD_STATE, batch), lambda hd: (layer, hd, 0, 0, 0))
    return _stacked_call(
        kern, name="ssd_lanes", grid=(SSD_HEADS,),
        inputs=[pt, cst, h0, w["cw_b"], w["cb_b"], w["dtb_b"], w["alog_b"], w["dexp_b"], w["nw_b"]],
        in_specs=[full(PT_ROWS, tb), _layer_spec(layer, n_ch, keep * batch), hblock,
                  _layer_spec(layer, SSD_CONV, n_ch, batch), _layer_spec(layer, n_ch, batch),
                  _layer_spec(layer, SSD_HEADS, batch), _layer_spec(layer, SSD_HEADS, batch),
                  _layer_spec(layer, SSD_HEADS, batch), _layer_spec(layer, D_MODEL, batch)],
        out_shape=(jax.ShapeDtypeStruct((D_MODEL, tb), bf16),
                   jax.ShapeDtypeStruct((DEPTH, n_ch, keep * batch), f32),
                   jax.ShapeDtypeStruct((DEPTH, SSD_HEADS, SSD_HEAD_DIM, SSD_STATE, batch), f32)),
        out_specs=(full(D_MODEL, tb), _layer_spec(layer, n_ch, keep * batch), hblock),
        stacked={1: prev and prev[0], 2: prev and prev[1]},
        scratch_shapes=[pltpu.VMEM((n_ch, tb), f32), pltpu.VMEM((SSD_HEADS, tb), f32),
                        pltpu.VMEM((SSD_HEADS, tb), f32), pltpu.VMEM((D_MODEL, tb), f32)],
        dimension_semantics=("arbitrary",))


def _mlstm_kernel(*refs, q, lr, nc, has_state, gs, cps, perm, tile3d):
    seq = functools.partial(_mlstm_seq, q=q, lr=lr, nc=nc, has_state=has_state, cps=cps, perm=perm)
    _per_sequence(seq, refs, 5, 3, 2, gs, has_state, nc, cps * q, tile3d)


def _mlstm_seq(*refs, q, lr, nc, has_state, cps, perm):
    q_ref, k_ref, v_ref, o_ref, sm_ref = refs[:5]
    refs = refs[5:]
    if has_state:
        c0_ref, n0_ref, m0_ref = refs[:3]
        refs = refs[3:]
    gb_ref, nw_ref, h_ref, cout_ref, nout_ref, mout_ref, cs, ns, ms = refs
    direct = has_state and nc == 1 and cps == 1

    def init():
        if direct:
            return
        if has_state:
            cs[...] = c0_ref[...]
            ns[...] = n0_ref[...]
            ms[...] = m0_ref[...]
        else:
            cs[...] = jnp.zeros_like(cs)
            ns[...] = jnp.zeros_like(ns)
            ms[...] = jnp.zeros_like(ms)

    def final():
        if direct:
            return
        cout_ref[...] = cs[...]
        nout_ref[...] = ns[...]
        mout_ref[...] = ms[...]

    def body():
        chunks = []
        for k in range(cps):
            sub = lambda r, k=k: r.rows(k * q, q)
            is_last = (pl.program_id(1) == nc - 1) if k == cps - 1 else False
            src = (c0_ref, n0_ref, m0_ref) if direct else (cs, ns, ms)
            dst = (cout_ref, nout_ref, mout_ref) if direct else (cs, ns, ms)
            chunks.append(_mlstm_body(sub(q_ref), sub(k_ref), sub(v_ref), sub(o_ref), sub(sm_ref), gb_ref, nw_ref,
                                      sub(h_ref), src, dst, q=q, lr=lr, is_last=is_last, perm=perm))
        yield from _staggered(chunks, MLSTM_STATE_STAGES)

    return init, body, final


MLSTM_STATE_STAGES = 8


def _staggered(gens, skew):
    done = [False] * len(gens)
    t = 0
    while not all(done):
        for i, g in enumerate(gens):
            if done[i] or t < i * skew:
                continue
            try:
                next(g)
            except StopIteration:
                done[i] = True
        t += 1
        yield


def _mlstm_body(q_ref, k_ref, v_ref, o_ref, sm_ref, gb_ref, nw_ref, h_ref, src, dst, *, q, lr, is_last, perm):
    c_src, n_src, m_src = src
    c_dst, n_dst, m_dst = dst
    sm = sm_ref[...] + gb_ref[...]
    logf = -_softplus(-sm)
    ipre = sm
    if lr < q:
        valid = _valid_rows(q, SMALL, lr, is_last)
        logf = jnp.where(valid, logf, 0.0)
        ipre = jnp.where(valid, ipre, NEG_BIG)
    causal = _tri(q, perm)
    tril = jnp.where(causal, 1.0, 0.0).astype(bf16)
    yield
    bcum = _dot01_lhs(tril, logf)
    ipre_t = ipre.T
    yield
    bcum_t = bcum.T
    lane = lax.broadcasted_iota(jnp.int32, (1, SMALL), 1)
    k_scale = MLSTM_HEAD_DIM ** -0.5

    heads = range(MLSTM_HEADS)
    sls = [slice(h * MLSTM_HEAD_DIM, (h + 1) * MLSTM_HEAD_DIM) for h in heads]
    q_all, k_all, v_all, o_all = q_ref[...], k_ref[...], v_ref[...], o_ref[...]
    qs = [q_all[:, sl] for sl in sls]
    ks = [k_all[:, sl] * k_scale for sl in sls]
    vs = [v_all[:, sl] for sl in sls]
    b_cols = [bcum[:, F_OFF + h:F_OFF + h + 1] for h in heads]
    i_cols = [ipre[:, I_OFF + h:I_OFF + h + 1] for h in heads]
    dmats = [jnp.where(causal, b_cols[h] - bcum_t[F_OFF + h:F_OFF + h + 1, :] + ipre_t[I_OFF + h:I_OFF + h + 1, :],
                       -jnp.inf) for h in heads]
    yield
    qk = [lax.dot_general(qs[h], ks[h], NT_DIMS, preferred_element_type=f32) for h in heads]
    d_max = [jnp.max(dmats[h], axis=-1, keepdims=True) for h in heads]
    yield
    n_all = n_src[...]
    m_all = m_src[...]
    m_new = m_all
    cs_in = [c_src[h] for h in heads]
    m_prevs = [m_all[:, h:h + 1] for h in heads]
    qc = [_dot(qs[h], cs_in[h].astype(bf16)) for h in heads]
    inters = [b_cols[h] + m_prevs[h] for h in heads]
    m_ts = [jnp.maximum(inters[h], d_max[h]) for h in heads]
    yield
    w_inters = [jnp.exp(inters[h] - m_ts[h]) for h in heads]
    ss = [qk[h] * jnp.exp(dmats[h] - m_ts[h]) for h in heads]
    yield
    sv = [_dot(ss[h].astype(bf16), vs[h]) for h in heads]
    m_ends = [m_ts[h][q - 1:q, :] for h in heads]
    b_lasts = [b_cols[h][q - 1:q, :] for h in heads]
    kws = [ks[h].astype(f32) * jnp.exp(b_lasts[h] - b_cols[h] + i_cols[h] - m_ends[h]) for h in heads]
    yield
    kv = [lax.dot_general(kws[h].astype(bf16), vs[h], TN_DIMS, preferred_element_type=f32) for h in heads]
    qns = [jnp.sum(qs[h].astype(f32) * n_all[h:h + 1, :], axis=-1, keepdims=True) for h in heads]
    yield
    dens = [jnp.sum(ss[h], axis=-1, keepdims=True) + w_inters[h] * qns[h] for h in heads]
    yield
    hvs = [(sv[h] + w_inters[h] * qc[h]) / jnp.maximum(jnp.abs(dens[h]), jnp.exp(-m_ts[h])) for h in heads]
    yield
    rms = [lax.rsqrt(jnp.mean(jnp.square(hvs[h]), axis=-1, keepdims=True) + EPS) for h in heads]
    yield
    h_new = [(hvs[h] * rms[h] * nw_ref[:, sls[h]] * jax.nn.sigmoid(o_all[:, sls[h]])).astype(h_ref.dtype)
             for h in heads]
    w_cs = [jnp.exp(b_lasts[h] + m_prevs[h] - m_ends[h]) for h in heads]
    yield
    for h in heads:
        c_dst[h] = w_cs[h] * cs_in[h] + kv[h]
        m_new = jnp.where(lane == h, m_ends[h], m_new)
    h_ref[...] = jnp.concatenate(h_new, axis=1)
    n_dst[...] = jnp.concatenate(
        [w_cs[h] * n_all[h:h + 1, :] + jnp.sum(kws[h], axis=0, keepdims=True) for h in heads], axis=0)
    m_dst[...] = m_new


def _merge_kernel(ys_ref, hm_ref, ga_ref, gb_ref, x_ref, wa_ref, wb_ref, wo_ref, g_ref, b_ref, o_ref):
    tm = x_ref.shape[0]
    rows = [pl.ds(i * (tm // MERGE_PARTS), tm // MERGE_PARTS) for i in range(MERGE_PARTS)]
    br = [(_dot(ys_ref[r, :], wa_ref[...]), _dot(hm_ref[r, :], wb_ref[...])) for r in rows]
    merged = [(jax.nn.sigmoid(ga_ref[r, :]) * a + jax.nn.sigmoid(gb_ref[r, :]) * b).astype(bf16)
              for r, (a, b) in zip(rows, br)]
    mix = [_dot(m, wo_ref[...]) for m in merged]
    for r, m in zip(rows, mix):
        o_ref[r, :] = _layer_norm(ALPHA * x_ref[r, :] + m, g_ref[...], b_ref[...])


def _merge(ys, hm, p32, x, w, layer, tm):
    m = x.shape[0]
    row = lambda blk: pl.BlockSpec((tm, D_MODEL), lambda i: (i, blk))
    return pl.pallas_call(
        _merge_kernel,
        out_shape=jax.ShapeDtypeStruct((m, D_MODEL), f32),
        grid=(m // tm,),
        in_specs=[row(0), row(0), row(P32_GA), row(P32_GB), row(0),
                  _layer_spec(layer, D_MODEL, D_MODEL), _layer_spec(layer, D_MODEL, D_MODEL),
                  _layer_spec(layer, D_MODEL, D_MODEL), _layer_spec(layer, 1, D_MODEL),
                  _layer_spec(layer, 1, D_MODEL)],
        out_specs=row(0),
        compiler_params=pltpu.CompilerParams(dimension_semantics=("parallel",)),
        name="merge",
    )(ys, hm, p32, p32, x, w["wa"], w["wb"], w["wo"], w["ln1_g"], w["ln1_b"])


def _ffn_kernel(*refs, tm, seq_len, perm_q):
    multi = seq_len > 0
    keep_rows = (FFN_CONV - 1) * SUBLANES
    if multi:
        x_ref, st_ref, wup_ref, cw_ref, cb_ref, wdn_ref, g_ref, b_ref, o_ref, sout_ref, xp = refs
    else:
        x_ref, wup_ref, cw_ref, cb_ref, wdn_ref, g_ref, b_ref, o_ref, sout_ref, xp, carry = refs
        @pl.when(pl.program_id(1) == 0)
        def _():
            carry[...] = jnp.zeros_like(carry)

    hdr = SUBLANES
    x = x_ref[...]
    xb = x.astype(bf16)
    if multi:
        assert seq_len & (seq_len - 1) == 0
        nseq = tm // seq_len
        t = lax.broadcasted_iota(jnp.int32, (tm, FF_CH), 0) & (seq_len - 1)
        row = lax.broadcasted_iota(jnp.int32, (tm, 2 * nseq), 0)
        col = lax.broadcasted_iota(jnp.int32, (tm, 2 * nseq), 1)
        t_sel = row & (seq_len - 1)
        seq0 = lax.shift_right_logical(row - t_sel, (seq_len // 2).bit_length() - 1)
        sel_p2 = jnp.where(col == seq0 + t_sel, jnp.where(t_sel < 2, 1.0, 0.0), 0.0).astype(bf16)
        sel_p1 = jnp.where(col == seq0 + 1, jnp.where(t_sel == 0, 1.0, 0.0), 0.0).astype(bf16)
        xp[:, 0:hdr, :] = jnp.zeros((FF_XP_SLOTS, hdr, FF_CH), f32)

    def cols_of(c, part):
        return slice(part * D_FF + c * FF_CH, part * D_FF + (c + 1) * FF_CH)

    def up(c):
        return [_dot(xb, wup_ref[:, cols_of(c, part)]) for part in range(2)]

    def conv_act(c, us):
        halves = []
        for part, u in enumerate(us):
            cols = cols_of(c, part)
            slot = (2 * c + part) % FF_XP_SLOTS
            if perm_q:
                prev = carry[:, cols]
                p1, p2 = [], []
                for kk in range(tm // perm_q):
                    uc = u[kk * perm_q:(kk + 1) * perm_q, :]
                    wrapped = _wrap_rows(uc[perm_q - keep_rows:, :], prev)
                    p1.append(_shift_back(uc, wrapped, 1))
                    p2.append(_shift_back(uc, wrapped, 2))
                    prev = uc[perm_q - keep_rows:, :]
                carry[:, cols] = prev
                p1, p2 = jnp.concatenate(p1, axis=0), jnp.concatenate(p2, axis=0)
                w = cw_ref[:, cols]
                halves.append(p2 * w[0:1, :] + p1 * w[1:2, :] + u * w[2:3, :] + cb_ref[:, cols])
                continue
            xp[slot, hdr:hdr + tm, :] = u
            if multi:
                sout_ref[:, cols] = u
                st = st_ref[:, cols]
                p1 = jnp.where(t == 0, _dot01_lhs(sel_p1, st), xp[slot, hdr - 1:hdr - 1 + tm, :])
                p2 = jnp.where(t < 2, _dot01_lhs(sel_p2, st), xp[slot, hdr - 2:hdr - 2 + tm, :])
            else:
                xp[slot, 0:hdr, :] = carry[:, cols]
                p1 = xp[slot, hdr - 1:hdr - 1 + tm, :]
                p2 = xp[slot, hdr - 2:hdr - 2 + tm, :]
                carry[:, cols] = u[tm - hdr:tm, :]
            w = cw_ref[:, cols]
            halves.append(p2 * w[0:1, :] + p1 * w[1:2, :] + u * w[2:3, :] + cb_ref[:, cols])
        return (_silu(halves[0]) * halves[1]).astype(bf16)

    acc = None
    ahead = [up(c) for c in range(min(FF_UP_AHEAD, FF_NCH))]
    pending = None
    for c in range(FF_NCH):
        if c + FF_UP_AHEAD < FF_NCH:
            ahead.append(up(c + FF_UP_AHEAD))
        if pending is not None:
            d = _dot(pending, wdn_ref[(c - 1) * FF_CH:c * FF_CH, :])
            acc = d if acc is None else acc + d
        pending = conv_act(c, ahead.pop(0))
    acc = acc + _dot(pending, wdn_ref[(FF_NCH - 1) * FF_CH:FF_NCH * FF_CH, :])

    if not multi:
        sout_ref[...] = carry[...]
    o_ref[...] = _layer_norm(ALPHA * x + acc, g_ref[...], b_ref[...])


def _ffn(x, st, w, layer, *, groups, tm, seq_len, perm_q=0):
    m = x.shape[0]
    tiles = m // (groups * tm)
    multi = seq_len > 0
    kern = functools.partial(_ffn_kernel, tm=tm, seq_len=seq_len, perm_q=perm_q)
    carry_rows = (FFN_CONV - 1) * SUBLANES if perm_q else SUBLANES
    once = dict(pipeline_mode=pl.Buffered(1))
    x_spec = pl.BlockSpec((tm, D_MODEL), lambda s, j: (s * tiles + j, 0))
    w_specs = [_layer_spec(layer, D_MODEL, 2 * D_FF, **once), _layer_spec(layer, FFN_CONV, 2 * D_FF, **once),
               _layer_spec(layer, 1, 2 * D_FF, **once), _layer_spec(layer, D_FF, D_MODEL, **once),
               _layer_spec(layer, 1, D_MODEL, **once), _layer_spec(layer, 1, D_MODEL, **once)]
    w_args = (w["wup"], w["fcw"], w["fcb"], w["wdn"], w["ln2_g"], w["ln2_b"])
    xp = pltpu.VMEM((FF_XP_SLOTS, SUBLANES + (0 if perm_q else tm), FF_CH), f32)
    if multi:
        nst = 2 * (tm // seq_len)
        inputs = (x, st) + w_args
        in_specs = [x_spec, pl.BlockSpec((None, nst, 2 * D_FF), lambda s, j: (layer, s * tiles + j, 0))] + w_specs
        sout_shape = jax.ShapeDtypeStruct((m, 2 * D_FF), f32)
        sout_spec = pl.BlockSpec((tm, 2 * D_FF), lambda s, j: (s * tiles + j, 0))
        scratch = [xp]
    else:
        inputs = (x,) + w_args
        in_specs = [x_spec] + w_specs
        sout_shape = jax.ShapeDtypeStruct((groups, carry_rows, 2 * D_FF), f32)
        sout_spec = pl.BlockSpec((None, carry_rows, 2 * D_FF), lambda s, j: (s, 0, 0))
        scratch = [xp, pltpu.VMEM((carry_rows, 2 * D_FF), f32)]
    return pl.pallas_call(
        kern,
        out_shape=(jax.ShapeDtypeStruct((m, D_MODEL), f32), sout_shape),
        grid=(groups, tiles),
        in_specs=in_specs,
        out_specs=(x_spec, sout_spec),
        scratch_shapes=scratch,
        compiler_params=pltpu.CompilerParams(dimension_semantics=("parallel", "arbitrary"),
                                             vmem_limit_bytes=56 * 1024 * 1024),
        name="ffn",
    )(*inputs)


def _pad_lanes(v, off, width=SMALL):
    out = jnp.zeros((v.shape[0], 1, width), f32)
    return out.at[:, 0, off:off + v.shape[1]].set(v.astype(f32))


def _prep_weights(w_in, ssd_conv_w, ssd_conv_b, ssd_dt_bias, ssd_a_log, ssd_d, ssd_norm_w, mlstm_gate_b,
                  mlstm_norm_w, w_branch_a, w_branch_b, w_out, ln1_g, ln1_b, ffn_w_up, ffn_conv_w,
                  ffn_conv_b, ffn_w_down, ln2_g, ln2_b):
    d = D_MODEL
    o_z, o_xbc, o_dt = 0, d, d + d + SSD_BC
    o_q = o_dt + SSD_HEADS
    o_if = o_q + 3 * d
    o_o = o_if + 2 * MLSTM_HEADS
    o_g = o_o + d
    w_t = jnp.swapaxes(w_in, 1, 2)
    cols = lambda a, n: w_t[:, a:a + n, :]
    zeros = lambda n: jnp.zeros((DEPTH, n, d), w_in.dtype)
    w32 = jnp.concatenate([cols(o_z, d), cols(o_o, d), cols(o_g, 2 * d), cols(o_xbc, d + SSD_BC),
                           cols(o_dt, SSD_HEADS), cols(o_if, 2 * MLSTM_HEADS),
                           zeros(P32_W - P32_SM_OFF - SSD_HEADS - 2 * MLSTM_HEADS)], axis=1).astype(bf16)
    e = (np.arange(SSD_HP)[None, :] // SSD_HEAD_DIM == np.arange(LANES)[:, None])
    bd = ((np.arange(SSD_GROUPS * SSD_STATE)[:, None] < SSD_STATE)
          == (np.arange(SSD_HP)[None, :] < SSD_HP // SSD_GROUPS))
    row = lambda a: a[:, None, :]
    return dict(
        w32=w32, wqkv=cols(o_q, 3 * d).astype(bf16),
        cwx=ssd_conv_w[:, :, :d], cbx=row(ssd_conv_b[:, :d]),
        cwb=ssd_conv_w[:, :, d:], cbb=row(ssd_conv_b[:, d:]),
        dtb=_pad_lanes(ssd_dt_bias, DT_OFF), alog=_pad_lanes(ssd_a_log, DT_OFF),
        dexp=row(jnp.repeat(ssd_d.astype(f32), SSD_HEAD_DIM, axis=1)), ssd_nw=row(ssd_norm_w),
        e=jnp.asarray(e, bf16), bd=jnp.asarray(bd, f32),
        gate_b=_pad_lanes(mlstm_gate_b, I_OFF), mlstm_nw=row(mlstm_norm_w),
        wa=w_branch_a.astype(bf16), wb=w_branch_b.astype(bf16), wo=w_out.astype(bf16),
        ln1_g=row(ln1_g), ln1_b=row(ln1_b),
        wup=ffn_w_up.astype(bf16), fcw=ffn_conv_w, fcb=row(ffn_conv_b), wdn=ffn_w_down.astype(bf16),
        ln2_g=row(ln2_g), ln2_b=row(ln2_b),
    )


class _Group:
    def __init__(self, batch, length, q, lr, gs, ssd_cps, mlstm_cps, proj_tm, p32_tn, qkv_tn, merge_tm, ffn,
                 perm=False, lanes_ssd=False, mlstm_gs=None):
        self.mlstm_gs = gs if mlstm_gs is None else mlstm_gs
        self.perm = perm
        self.lanes_ssd = lanes_ssd
        self.batch, self.length, self.q, self.lr, self.gs = batch, length, q, lr, gs
        self.ssd_cps, self.mlstm_cps = ssd_cps, mlstm_cps
        self.rows = batch * length
        self.proj_tm, self.p32_tn, self.qkv_tn, self.merge_tm, self.ffn = proj_tm, p32_tn, qkv_tn, merge_tm, ffn

    def cfg(self, name, layer):
        v = getattr(self, name)
        return v[layer] if isinstance(v, tuple) else v

    def tiling(self, cps, gs=None):
        rows = cps * self.q
        gs = self.gs if gs is None else gs
        steps = self.length // rows
        if gs == 1 or steps == 1:
            spec = lambda width, blk: pl.BlockSpec((gs * rows, width), lambda b, c: (b * steps + c, blk))
            return steps, spec, (lambda a: a), False
        spec = lambda width, blk: pl.BlockSpec((gs, rows, width), lambda b, c: (b, c, blk))
        return steps, spec, (lambda a: a.reshape(self.batch, self.length, a.shape[-1])), True


def _ssd(grp, p32, state, w, layer, prev):
    q, b, gs = grp.q, grp.batch, grp.gs
    has_state = state is not None
    cps = grp.cfg("ssd_cps", layer)
    steps, tile, _, tile3d = grp.tiling(cps)
    assert not tile3d
    kern = functools.partial(_ssd_kernel, q=q, lr=grp.lr, nc=steps, has_state=has_state, gs=gs, cps=cps,
                             perm=grp.perm)
    inputs = [p32, p32, p32, p32]
    in_specs = [tile(D_MODEL, P32_Z), tile(D_MODEL, P32_XS),
                tile(SSD_BC, P32_BC_OFF // SSD_BC), tile(SMALL, P32_SM_OFF // SMALL)]
    if has_state:
        inputs += [state["csx"], state["csb"], state["h"]]
        in_specs += [_seq_spec(layer, gs, SSD_CONV - 1, D_MODEL), _seq_spec(layer, gs, SSD_CONV - 1, SSD_BC),
                     _seq_spec(layer, gs, SSD_HP, SSD_STATE)]
    inputs += [w["cwx"], w["cbx"], w["cwb"], w["cbb"], w["dtb"], w["alog"], w["dexp"], w["ssd_nw"], w["e"],
               w["bd"]]
    const = lambda *shape: pl.BlockSpec(shape, lambda b, c: (0,) * len(shape))
    in_specs += [_layer_spec(layer, SSD_CONV, D_MODEL), _layer_spec(layer, 1, D_MODEL),
                 _layer_spec(layer, SSD_CONV, SSD_BC), _layer_spec(layer, 1, SSD_BC),
                 _layer_spec(layer, 1, SMALL), _layer_spec(layer, 1, SMALL), _layer_spec(layer, 1, D_MODEL),
                 _layer_spec(layer, 1, D_MODEL), const(LANES, SSD_HP), const(SSD_GROUPS * SSD_STATE, SSD_HP)]
    return _stacked_call(
        kern, name="ssd", grid=(b // gs, steps), inputs=inputs, in_specs=in_specs,
        out_shape=(jax.ShapeDtypeStruct((grp.rows, D_MODEL), bf16),
                   jax.ShapeDtypeStruct((DEPTH, b, SSD_CONV - 1, D_MODEL), f32),
                   jax.ShapeDtypeStruct((DEPTH, b, SSD_CONV - 1, SSD_BC), f32),
                   jax.ShapeDtypeStruct((DEPTH, b, SSD_HP, SSD_STATE), f32)),
        out_specs=(tile(D_MODEL, 0), _seq_spec(layer, gs, SSD_CONV - 1, D_MODEL),
                   _seq_spec(layer, gs, SSD_CONV - 1, SSD_BC), _seq_spec(layer, gs, SSD_HP, SSD_STATE)),
        stacked={1: prev and prev[0], 2: prev and prev[1], 3: prev and prev[2]},
        scratch_shapes=[pltpu.VMEM((gs, SUBLANES + q, D_MODEL), f32), pltpu.VMEM((gs, SUBLANES + q, SSD_BC), f32),
                        pltpu.VMEM((gs, SSD_GROUPS * SSD_STATE, SSD_HP), f32), pltpu.VMEM((gs, q, D_MODEL), f32)],
        dimension_semantics=("parallel", "arbitrary"))


def _mlstm(grp, qkv, p32, state, w, layer, prev):
    q, b, gs = grp.q, grp.batch, grp.mlstm_gs
    has_state = state is not None
    cps = grp.cfg("mlstm_cps", layer)
    steps, tile, view, tile3d = grp.tiling(cps, gs)
    kern = functools.partial(_mlstm_kernel, q=q, lr=grp.lr, nc=steps, has_state=has_state, gs=gs, cps=cps,
                             perm=grp.perm, tile3d=tile3d)
    hd = MLSTM_HEAD_DIM
    carried = not (has_state and steps == 1 and cps == 1)
    qkv, p32 = view(qkv), view(p32)
    inputs = [qkv, qkv, qkv, p32, p32]
    in_specs = [tile(D_MODEL, 0), tile(D_MODEL, 1), tile(D_MODEL, 2), tile(D_MODEL, P32_O),
                tile(SMALL, P32_SM_OFF // SMALL)]
    if has_state:
        inputs += [state["c"], state["n"], state["m"]]
        in_specs += [_seq_spec(layer, gs, MLSTM_HEADS, hd, hd), _seq_spec(layer, gs, MLSTM_HEADS, hd),
                     _seq_spec(layer, gs, 1, SMALL)]
    inputs += [w["gate_b"], w["mlstm_nw"]]
    in_specs += [_layer_spec(layer, 1, SMALL), _layer_spec(layer, 1, D_MODEL)]
    hm, *new = _stacked_call(
        kern, name="mlstm", grid=(b // gs, steps), inputs=inputs, in_specs=in_specs,
        out_shape=(jax.ShapeDtypeStruct((b, grp.length, D_MODEL) if tile3d else (grp.rows, D_MODEL), bf16),
                   jax.ShapeDtypeStruct((DEPTH, b, MLSTM_HEADS, hd, hd), f32),
                   jax.ShapeDtypeStruct((DEPTH, b, MLSTM_HEADS, hd), f32),
                   jax.ShapeDtypeStruct((DEPTH, b, 1, SMALL), f32)),
        out_specs=(tile(D_MODEL, 0), _seq_spec(layer, gs, MLSTM_HEADS, hd, hd),
                   _seq_spec(layer, gs, MLSTM_HEADS, hd), _seq_spec(layer, gs, 1, SMALL)),
        stacked={1: prev and prev[0], 2: prev and prev[1], 3: prev and prev[2]},
        scratch_shapes=[pltpu.VMEM((gs, MLSTM_HEADS, hd, hd) if carried else (gs, 1, SUBLANES, LANES), f32),
                        pltpu.VMEM((gs, MLSTM_HEADS, hd), f32), pltpu.VMEM((gs, 1, SMALL), f32)],
        dimension_semantics=("parallel", "arbitrary"))
    return (hm.reshape(grp.rows, D_MODEL), *new)


def _trunk(grp, x, state, w):
    ssd_out = mlstm_out = None
    ffn_out = []
    for layer in range(DEPTH):
        p32 = _proj(x, w["w32"], layer, f32, grp.cfg("proj_tm", layer), grp.cfg("p32_tn", layer))
        qkv = _proj(x, w["wqkv"], layer, bf16, grp.cfg("proj_tm", layer), grp.cfg("qkv_tn", layer))
        if grp.lanes_ssd:
            b, t = grp.batch, grp.length
            x_tm = x.reshape(b, t, D_MODEL).swapaxes(0, 1).reshape(t * b, D_MODEL)
            ys_t, *ssd_out = _ssd_lanes(_proj_t(x_tm, w["w32"], layer), state["cs_t"], state["h_lanes"], w, layer,
                                        ssd_out, steps=t, batch=b)
            ys = ys_t.reshape(D_MODEL, t, b).transpose(2, 1, 0).reshape(b * t, D_MODEL)
        else:
            ys, *ssd_out = _ssd(grp, p32, state, w, layer, ssd_out)
        hm, *mlstm_out = _mlstm(grp, qkv, p32, state, w, layer, mlstm_out)
        x1 = _merge(ys, hm, p32, x, w, layer, grp.cfg("merge_tm", layer))
        x, s_ffn = _ffn(x1, state["ffn"] if state is not None else None, w, layer, **grp.ffn)
        ffn_out.append(s_ffn)
    return x, ssd_out, mlstm_out, ffn_out


def _unpack_states(batch, ssd_out, mlstm_out):
    csx, csb, h = ssd_out
    c, n, m = mlstm_out
    return (h.reshape(DEPTH, batch, SSD_HEADS, SSD_HEAD_DIM, SSD_STATE),
            jnp.concatenate([csx, csb], axis=-1), c, n, m[:, :, 0, :MLSTM_HEADS])


def kernel(x_prompt, x_sample, state_ssd, state_ssd_conv, state_mlstm_c, state_mlstm_n, state_mlstm_m,
           state_ffn_conv, w_in, ssd_conv_w, ssd_conv_b, ssd_dt_bias, ssd_a_log, ssd_d, ssd_norm_w,
           mlstm_gate_b, mlstm_norm_w, w_branch_a, w_branch_b, w_out, ln1_g, ln1_b, ffn_w_up, ffn_conv_w,
           ffn_conv_b, ffn_w_down, ln2_g, ln2_b):
    w = _prep_weights(w_in, ssd_conv_w, ssd_conv_b, ssd_dt_bias, ssd_a_log, ssd_d, ssd_norm_w, mlstm_gate_b,
                      mlstm_norm_w, w_branch_a, w_branch_b, w_out, ln1_g, ln1_b, ffn_w_up, ffn_conv_w,
                      ffn_conv_b, ffn_w_down, ln2_g, ln2_b)
    keep = FFN_CONV - 1
    keep_ssd = SSD_CONV - 1

    bp, lp, _ = x_prompt.shape
    prompt = _Group(bp, lp, CHUNK, CHUNK, gs=1, ssd_cps=4, mlstm_cps=4, proj_tm=2048, p32_tn=1408, qkv_tn=1536,
                    merge_tm=1024, ffn=dict(groups=bp, tm=1024, seq_len=0, perm_q=CHUNK), perm=True, mlstm_gs=2)
    per = CHUNK // SUBLANES
    xp_rows = x_prompt.reshape(bp, lp // CHUNK, SUBLANES, per, D_MODEL).swapaxes(2, 3)
    y_p, ssd_p, mlstm_p, ffn_p = _trunk(prompt, xp_rows.reshape(bp * lp, D_MODEL), None, w)
    y_p = y_p.reshape(bp, lp // CHUNK, per, SUBLANES, D_MODEL).swapaxes(2, 3)
    st_p = _unpack_states(bp, ssd_p, mlstm_p)
    ffn_conv_p = jnp.stack(ffn_p)[:, :, SUBLANES - 1::SUBLANES, :]

    bs, ls, _ = x_sample.shape
    lpad = max(ls, SAMPLE_PAD_LEN)
    s_rows = bs * lpad
    sample = _Group(bs, lpad, lpad, ls, gs=8, ssd_cps=1, mlstm_cps=1, proj_tm=s_rows, p32_tn=512, qkv_tn=1024,
                    merge_tm=512,
                    ffn=dict(groups=1, tm=256, seq_len=lpad), lanes_ssd=True)
    assert lpad == ls
    lane_b = lambda a: jnp.broadcast_to(a.astype(f32)[..., None], a.shape + (bs,))
    w.update(cw_b=lane_b(ssd_conv_w), cb_b=lane_b(ssd_conv_b), dtb_b=lane_b(ssd_dt_bias), alog_b=lane_b(ssd_a_log),
             dexp_b=lane_b(ssd_d), nw_b=lane_b(ssd_norm_w))
    s_state = dict(
        cs_t=jnp.transpose(state_ssd_conv, (0, 3, 2, 1)).reshape(DEPTH, D_MODEL + SSD_BC, keep_ssd * bs),
        h_lanes=jnp.transpose(state_ssd, (0, 2, 3, 4, 1)),
        c=state_mlstm_c, n=state_mlstm_n,
        m=jnp.pad(state_mlstm_m, ((0, 0), (0, 0), (0, SMALL - MLSTM_HEADS)))[:, :, None, :],
        ffn=state_ffn_conv.reshape(DEPTH, bs * keep, 2 * D_FF),
    )
    xs = jnp.pad(x_sample, ((0, 0), (0, lpad - ls), (0, 0))).reshape(s_rows, D_MODEL)
    y_s, (cs_t, h_lanes), mlstm_s, ffn_s = _trunk(sample, xs, s_state, w)
    c_s, n_s, m_s = mlstm_s
    st_s = (jnp.transpose(h_lanes, (0, 4, 1, 2, 3)),
            jnp.transpose(cs_t.reshape(DEPTH, D_MODEL + SSD_BC, keep_ssd, bs), (0, 3, 2, 1)),
            c_s, n_s, m_s[:, :, 0, :MLSTM_HEADS])
    ffn_conv_s = jnp.stack([u.reshape(bs, lpad, 2 * D_FF)[:, ls - keep:ls, :] for u in ffn_s])
    y_sample = y_s.reshape(bs, lpad, D_MODEL)[:, :ls, :]

    return (y_p.reshape(bp, lp, D_MODEL), y_sample, st_p[0], st_s[0], st_p[1], st_s[1], st_p[2], st_s[2],
            st_p[3], st_s[3], st_p[4], st_s[4], ffn_conv_p, ffn_conv_s)
```

```python
import functools
import itertools

import jax
import jax.numpy as jnp
import numpy as np
from jax import lax
from jax.experimental import pallas as pl
from jax.experimental.pallas import tpu as pltpu

f32 = jnp.float32
bf16 = jnp.bfloat16

D_MODEL = 1024
DEPTH = 2
SSD_HEADS = 16
SSD_HEAD_DIM = 64
SSD_STATE = 64
SSD_GROUPS = 2
SSD_CONV = 4
SSD_BC = 2 * SSD_GROUPS * SSD_STATE
SSD_HP = SSD_HEADS * SSD_HEAD_DIM
MLSTM_HEADS = 4
MLSTM_HEAD_DIM = 256
CHUNK = 128
D_FF = 2816
FFN_CONV = 3
ALPHA = (2 * DEPTH) ** 0.25
EPS = 1e-5

LANES = 128
SUBLANES = 8
SMALL = LANES
DT_OFF, I_OFF, F_OFF = 0, 16, 20
P32_Z, P32_O, P32_GA, P32_GB, P32_XS = 0, 1, 2, 3, 4
P32_BC_OFF = 5 * D_MODEL
P32_SM_OFF = P32_BC_OFF + SSD_BC
P32_W = P32_SM_OFF + 2 * SMALL
FF_CH = 256
FF_NCH = D_FF // FF_CH
MERGE_PARTS = 2
FF_UP_AHEAD = 2
FF_XP_SLOTS = 4
NEG_BIG = -1e30
SAMPLE_PAD_LEN = 4

NT_DIMS = (((1,), (1,)), ((), ()))
TN_DIMS = (((0,), (0,)), ((), ()))


def _dot(a, b):
    return jnp.dot(a, b, preferred_element_type=f32)


def _split3(x):
    hi = x.astype(bf16)
    r = x - hi.astype(f32)
    mid = r.astype(bf16)
    lo = (r - mid.astype(f32)).astype(bf16)
    return hi, mid, lo


def _dot01_rhs(x, e):
    hi, mid, lo = _split3(x)
    return _dot(hi, e) + _dot(mid, e) + _dot(lo, e)


def _dot01_lhs(t, x):
    hi, mid, lo = _split3(x)
    return _dot(t, hi) + _dot(t, mid) + _dot(t, lo)


def _softplus(x):
    return jnp.maximum(x, 0.0) + jnp.log1p(jnp.exp(-jnp.abs(x)))


def _silu(x):
    return x * jax.nn.sigmoid(x)


def _row_time(i, q, perm):
    if not perm:
        return i
    return (i & (SUBLANES - 1)) * (q // SUBLANES) + lax.shift_right_logical(i, SUBLANES.bit_length() - 1)


def _tri(q, perm=False):
    row = lax.broadcasted_iota(jnp.int32, (q, q), 0)
    col = lax.broadcasted_iota(jnp.int32, (q, q), 1)
    return _row_time(row, q, perm) >= _row_time(col, q, perm)


def _wrap_rows(cur_tail, prev_tail):
    out = []
    for i in range(cur_tail.shape[0] // SUBLANES):
        rows = slice(i * SUBLANES, (i + 1) * SUBLANES)
        first = lax.broadcasted_iota(jnp.int32, (SUBLANES, cur_tail.shape[1]), 0) == 0
        out.append(jnp.where(first, pltpu.roll(prev_tail[rows], 1, axis=0), pltpu.roll(cur_tail[rows], 1, axis=0)))
    return jnp.concatenate(out, axis=0)


def _shift_back(x, wrapped, j):
    n = j * SUBLANES
    return jnp.concatenate([wrapped[wrapped.shape[0] - n:], x[:x.shape[0] - n]], axis=0)


def _valid_rows(q, width, lr, is_last):
    row = lax.broadcasted_iota(jnp.int32, (q, width), 0)
    return row < jnp.where(is_last, lr, q)


def _layer_norm(r, g, b):
    mu = jnp.mean(r, axis=-1, keepdims=True)
    var = jnp.mean(jnp.square(r - mu), axis=-1, keepdims=True)
    return (r - mu) * lax.rsqrt(var + EPS) * g + b


def _layer_spec(layer, *shape, **kw):
    zeros = (0,) * len(shape)
    return pl.BlockSpec((None,) + shape, lambda *_: (layer,) + zeros, **kw)


def _seq_spec(layer, gs, *shape):
    zeros = (0,) * len(shape)
    return pl.BlockSpec((None, gs) + shape, lambda b, c: (layer, b) + zeros)


def _stacked_call(kern, *, name, grid, inputs, in_specs, out_shape, out_specs, stacked, scratch_shapes,
                  dimension_semantics, vmem_limit_bytes=None):
    prev = [(i, a) for i, a in sorted(stacked.items()) if a is not None]
    n_in = len(inputs)

    def body(*refs):
        kern(*refs[:n_in], *refs[n_in + len(prev):])

    return pl.pallas_call(
        body,
        out_shape=out_shape,
        grid=grid,
        in_specs=list(in_specs) + [pl.BlockSpec(memory_space=pl.ANY)] * len(prev),
        out_specs=out_specs,
        scratch_shapes=scratch_shapes,
        input_output_aliases={n_in + k: i for k, (i, _) in enumerate(prev)},
        compiler_params=pltpu.CompilerParams(dimension_semantics=dimension_semantics,
                                             vmem_limit_bytes=vmem_limit_bytes),
        name=name,
    )(*inputs, *[a for _, a in prev])


def _proj_kernel(x_ref, w_ref, o_ref, xb):
    @pl.when(pl.program_id(1) == 0)
    def _():
        xb[...] = x_ref[...].astype(bf16)

    o_ref[...] = lax.dot_general(xb[...], w_ref[...], NT_DIMS, preferred_element_type=f32).astype(o_ref.dtype)


def _proj(x, wt, layer, out_dtype, tm, tn):
    m, k = x.shape
    n = wt.shape[1]
    return pl.pallas_call(
        _proj_kernel,
        out_shape=jax.ShapeDtypeStruct((m, n), out_dtype),
        grid=(m // tm, n // tn),
        in_specs=[pl.BlockSpec((tm, k), lambda i, j: (i, 0)),
                  pl.BlockSpec((None, tn, k), lambda i, j: (layer, j, 0))],
        out_specs=pl.BlockSpec((tm, tn), lambda i, j: (i, j)),
        scratch_shapes=[pltpu.VMEM((tm, k), bf16)],
        compiler_params=pltpu.CompilerParams(dimension_semantics=("parallel", "arbitrary")),
        name="proj",
    )(x, wt)


class _Rows:
    def __init__(self, ref, start, n):
        self.ref, self.start, self.n, self.dtype = ref, start, n, ref.dtype

    def rows(self, off, n):
        return _Rows(self.ref, self.start + off, n)

    def _index(self, idx):
        cols = slice(None) if idx is Ellipsis else idx[1]
        return (slice(self.start, self.start + self.n), cols)

    def __getitem__(self, idx):
        return self.ref[self._index(idx)]

    def __setitem__(self, idx, value):
        self.ref[self._index(idx)] = value


def _per_sequence(seq_fn, refs, n_tile, n_state, n_param, gs, has_state, nc, rows, tile3d=False):
    n_state = n_state if has_state else 0
    tiles, refs = refs[:n_tile], refs[n_tile:]
    state, refs = refs[:n_state], refs[n_state:]
    params, (y_tile, *rest) = refs[:n_param], refs[n_param:]
    phases = []
    for g in range(gs):
        at = lambda group: tuple(r.at[g] for r in group)
        if tile3d:
            seq_rows = lambda group: tuple(_Rows(r.at[g], 0, rows) for r in group)
        else:
            seq_rows = lambda group: tuple(_Rows(r, g * rows, rows) for r in group)
        phases.append(seq_fn(*seq_rows(tiles), *at(state), *params, *seq_rows((y_tile,)), *at(rest)))
    c = pl.program_id(1)

    @pl.when(c == 0)
    def _():
        for init, _, _ in phases:
            init()

    for _ in itertools.zip_longest(*[body() for _, body, _ in phases]):
        pass

    @pl.when(c == nc - 1)
    def _():
        for _, _, final in phases:
            final()


def _ssd_kernel(*refs, q, lr, nc, has_state, gs, cps, perm, tile3d):
    seq = functools.partial(_ssd_seq, q=q, lr=lr, nc=nc, has_state=has_state, cps=cps, perm=perm)
    _per_sequence(seq, refs, 4, 3, 10, gs, has_state, nc, cps * q, tile3d)


def _ssd_seq(*refs, q, lr, nc, has_state, cps, perm):
    z_ref, xs_ref, bc_ref, sm_ref = refs[:4]
    refs = refs[4:]
    if has_state:
        csx_ref, csb_ref, h0_ref = refs[:3]
        refs = refs[3:]
    (cwx_ref, cbx_ref, cwb_ref, cbb_ref, dtb_ref, alog_ref, dexp_ref, nw_ref, e_ref, bd_ref,
     y_ref, ncsx_ref, ncsb_ref, hout_ref, xpx, xpb, ht, yb) = refs
    hdr = SUBLANES
    lo = hdr - (SSD_CONV - 1)
    n2 = SSD_GROUPS * SSD_STATE
    assert lr >= SSD_CONV - 1
    keep = SSD_CONV - 1
    assert not (perm and (has_state or lr != q))
    carried = [(i + 1) * SUBLANES - 1 for i in range(keep)]

    def init():
        if has_state:
            h_t = h0_ref[...].T
            ht[...] = jnp.where(bd_ref[...] > 0.5, jnp.concatenate([h_t, h_t], axis=0), 0.0)
            xpx[lo:hdr, :] = csx_ref[...]
            xpb[lo:hdr, :] = csb_ref[...]
        else:
            ht[...] = jnp.zeros_like(ht)
            rows = slice(0, keep * SUBLANES) if perm else slice(lo, hdr)
            xpx[rows, :] = jnp.zeros((rows.stop - rows.start, SSD_HP), f32)
            xpb[rows, :] = jnp.zeros((rows.stop - rows.start, SSD_BC), f32)

    def final():
        if perm:
            for i, r in enumerate(carried):
                ncsx_ref[i:i + 1, :] = xpx[r:r + 1, :]
                ncsb_ref[i:i + 1, :] = xpb[r:r + 1, :]
        else:
            ncsx_ref[...] = xpx[lo + lr:hdr + lr, :]
            ncsb_ref[...] = xpb[lo + lr:hdr + lr, :]
        h_new = ht[...]
        hout_ref[...] = (h_new[:SSD_STATE, :] + h_new[SSD_STATE:, :]).T

    def body():
        for k in range(cps):
            sub = lambda r: r.rows(k * q, q)
            is_last = (pl.program_id(1) == nc - 1) if k == cps - 1 else False
            yield from _ssd_body(sub(z_ref), sub(xs_ref), sub(bc_ref), sub(sm_ref), cwx_ref, cbx_ref, cwb_ref,
                                 cbb_ref, dtb_ref, alog_ref, dexp_ref, nw_ref, e_ref, bd_ref, sub(y_ref),
                                 xpx, xpb, ht, yb, q=q, lr=lr, is_last=is_last, perm=perm)

    return init, body, final


def _ssd_body(z_ref, xs_ref, bc_ref, sm_ref, cwx_ref, cbx_ref, cwb_ref, cbb_ref, dtb_ref, alog_ref,
              dexp_ref, nw_ref, e_ref, bd_ref, y_ref, xpx, xpb, ht, yb, *, q, lr, is_last, perm):
    hdr = SUBLANES
    lo = hdr - (SSD_CONV - 1)
    n2 = SSD_GROUPS * SSD_STATE
    block_diag = bd_ref[...] > 0.5

    dt = _softplus(sm_ref[...] + dtb_ref[...])
    if lr < q:
        dt = jnp.where(_valid_rows(q, SMALL, lr, is_last), dt, 0.0)
    a = -jnp.exp(alog_ref[...])
    d_a = dt * a
    causal = _tri(q, perm)
    tril = jnp.where(causal, 1.0, 0.0).astype(bf16)
    e = e_ref[...]
    acs = _dot01_lhs(tril, d_a)
    dt_x = _dot01_rhs(dt, e)
    yield

    if perm:
        keep_rows = (SSD_CONV - 1) * SUBLANES

        def conv(xp, x_ref, w_ref, b_ref):
            w = w_ref[...]
            x = x_ref[...]
            wrapped = _wrap_rows(x[q - keep_rows:, :], xp[0:keep_rows, :])
            acc = _shift_back(x, wrapped, SSD_CONV - 1) * w[0:1, :]
            for j in range(1, SSD_CONV - 1):
                acc = acc + _shift_back(x, wrapped, SSD_CONV - 1 - j) * w[j:j + 1, :]
            acc = acc + x * w[SSD_CONV - 1:SSD_CONV, :]
            xp[0:keep_rows, :] = x[q - keep_rows:, :]
            return acc + b_ref[...]
    else:
        xpx[hdr:hdr + q, :] = xs_ref[...]
        xpb[hdr:hdr + q, :] = bc_ref[...]

        def conv(xp, x_ref, w_ref, b_ref):
            w = w_ref[...]
            acc = xp[lo:lo + q, :] * w[0:1, :]
            for j in range(1, SSD_CONV):
                acc = acc + xp[lo + j:lo + j + q, :] * w[j:j + 1, :]
            return acc + b_ref[...]

    cb = conv(xpb, bc_ref, cwb_ref, cbb_ref)
    bcv = _silu(cb)
    bm = bcv[:, :n2].astype(bf16)
    cm = bcv[:, n2:]
    lane_g0 = lax.broadcasted_iota(jnp.int32, (q, n2), 1) < SSD_STATE
    acs_t = acs.T
    acs_x = _dot01_rhs(acs, e)
    yield
    cbms = [lax.dot_general(jnp.where(lane_g0 if g == 0 else jnp.logical_not(lane_g0), cm, 0.0).astype(bf16),
                            bm, NT_DIMS, preferred_element_type=f32) for g in range(SSD_GROUPS)]
    h_prev = ht[...]
    y_off = _dot(cm.astype(bf16), h_prev.astype(bf16))
    cx = conv(xpx, xs_ref, cwx_ref, cbx_ref)
    if not perm:
        tail_x = xpx[lo + q:hdr + q, :]
        tail_b = xpb[lo + q:hdr + q, :]
        xpx[lo:hdr, :] = tail_x
        xpb[lo:hdr, :] = tail_b
    yield
    xs = _silu(cx)
    last_x = acs_x[q - 1:q, :]
    xdt = xs * dt_x
    xdt_b = xdt.astype(bf16)
    yield
    lane_lo = lax.broadcasted_iota(jnp.int32, (q, LANES), 1) < SSD_HEAD_DIM
    heads_per_group = SSD_HEADS // SSD_GROUPS
    decays = [jnp.exp(jnp.where(causal, acs[:, hh:hh + 1] - acs_t[hh:hh + 1, :], -jnp.inf))
              for hh in range(SSD_HEADS)]
    yield
    weights = [(cbms[hh // heads_per_group] * decays[hh]).astype(bf16) for hh in range(SSD_HEADS)]
    xdtw = (xdt * jnp.exp(last_x - acs_x)).astype(bf16)
    yield
    ys = [_dot(weights[hh], xdt_b[:, (hh // 2) * LANES:(hh // 2 + 1) * LANES]) for hh in range(SSD_HEADS)]
    upd = lax.dot_general(bm, xdtw, TN_DIMS, preferred_element_type=f32)
    yield
    for p in range(SSD_HEADS // 2):
        yb[:, p * LANES:(p + 1) * LANES] = jnp.where(lane_lo, ys[2 * p], ys[2 * p + 1])
    ht[...] = jnp.exp(last_x) * h_prev + jnp.where(block_diag, upd, 0.0)
    yield
    y = yb[...] + y_off * jnp.exp(acs_x) + dexp_ref[...] * xs
    y = y * _silu(z_ref[...])
    yield
    y = y * lax.rsqrt(jnp.mean(jnp.square(y), axis=-1, keepdims=True) + EPS) * nw_ref[...]
    y_ref[...] = y.astype(y_ref.dtype)


PT_ROWS = 5 * 512
PT_XS, PT_BC, PT_SM = D_MODEL, 2 * D_MODEL, 2 * D_MODEL + SSD_BC


def _proj_t_kernel(x_ref, w_ref, o_ref, xb):
    @pl.when(pl.program_id(0) == 0)
    def _():
        xb[...] = x_ref[...].astype(bf16)

    o_ref[...] = lax.dot_general(w_ref[...], xb[...], NT_DIMS, preferred_element_type=f32)


def _proj_t(x_tm, w32, layer):
    m, k = x_tm.shape
    tn = 512
    xs_blk = P32_XS * D_MODEL // tn
    return pl.pallas_call(
        _proj_t_kernel,
        out_shape=jax.ShapeDtypeStruct((PT_ROWS, m), f32),
        grid=(PT_ROWS // tn,),
        in_specs=[pl.BlockSpec((m, k), lambda j: (0, 0)),
                  pl.BlockSpec((None, tn, k), lambda j: (layer, jnp.where(j < D_MODEL // tn, j, j + xs_blk - D_MODEL // tn), 0))],
        out_specs=pl.BlockSpec((tn, m), lambda j: (j, 0)),
        scratch_shapes=[pltpu.VMEM((m, k), bf16)],
        compiler_params=pltpu.CompilerParams(dimension_semantics=("arbitrary",)),
        name="proj_t",
    )(x_tm, w32)


def _ssd_lanes_kernel(pt_ref, cst_ref, h0_ref, cw_ref, cb_ref, dtb_ref, alog_ref, dexp_ref, nw_ref,
                      y_ref, ncs_ref, hout_ref, xc, dts, decs, ysc, *, steps, batch):
    hd = pl.program_id(0)
    n_ch = D_MODEL + SSD_BC
    keep = SSD_CONV - 1
    lanes = lambda t: slice(t * batch, (t + 1) * batch)

    @pl.when(hd == 0)
    def _():
        for t in range(steps):
            acc = None
            for j in range(SSD_CONV):
                i = t + j
                src = cst_ref[:, lanes(i)] if i < keep else pt_ref[PT_XS:PT_XS + n_ch, lanes(i - keep)]
                term = src * cw_ref[j]
                acc = term if acc is None else acc + term
            xc[:, lanes(t)] = _silu(acc + cb_ref[...])
        ncs_ref[...] = pt_ref[PT_XS:PT_XS + n_ch, (steps - keep) * batch:steps * batch]
        dt = _softplus(pt_ref[PT_SM:PT_SM + SSD_HEADS, :] + jnp.concatenate([dtb_ref[...]] * steps, axis=1))
        dts[...] = dt
        decs[...] = jnp.exp(dt * jnp.concatenate([-jnp.exp(alog_ref[...])] * steps, axis=1))

    grp_row = (hd // (SSD_HEADS // SSD_GROUPS)) * SSD_STATE
    xh = xc[pl.ds(pl.multiple_of(hd * SSD_HEAD_DIM, SSD_HEAD_DIM), SSD_HEAD_DIM), :]
    bh = xc[pl.ds(pl.multiple_of(D_MODEL + grp_row, SSD_STATE), SSD_STATE), :]
    ch = xc[pl.ds(pl.multiple_of(D_MODEL + SSD_GROUPS * SSD_STATE + grp_row, SSD_STATE), SSD_STATE), :]
    dth = dts[pl.ds(hd, 1), :]
    dech = decs[pl.ds(hd, 1), :]
    d_skip = dexp_ref[pl.ds(hd, 1), :]
    y_rows = [[] for _ in range(steps)]
    for p in range(SSD_HEAD_DIM):
        h = h0_ref[p]
        for t in range(steps):
            x_row = xh[p:p + 1, lanes(t)]
            h = dech[:, lanes(t)] * h + (x_row * dth[:, lanes(t)]) * bh[:, lanes(t)]
            y_rows[t].append(jnp.sum(ch[:, lanes(t)] * h, axis=0, keepdims=True) + d_skip * x_row)
        hout_ref[p] = h
    rows = pl.ds(pl.multiple_of(hd * SSD_HEAD_DIM, SSD_HEAD_DIM), SSD_HEAD_DIM)
    for t in range(steps):
        ysc[rows, lanes(t)] = jnp.concatenate(y_rows[t], axis=0)

    @pl.when(hd == SSD_HEADS - 1)
    def _():
        y = ysc[...] * _silu(pt_ref[0:D_MODEL, :])
        y = y * lax.rsqrt(jnp.mean(jnp.square(y), axis=0, keepdims=True) + EPS)
        y_ref[...] = (y * jnp.concatenate([nw_ref[...]] * steps, axis=1)).astype(y_ref.dtype)


def _ssd_lanes(pt, cst, h0, w, layer, prev, *, steps, batch):
    n_ch = D_MODEL + SSD_BC
    tb = steps * batch
    keep = SSD_CONV - 1
    assert steps >= keep and batch % LANES == 0
    kern = functools.partial(_ssd_lanes_kernel, steps=steps, batch=batch)
    full = lambda *shape: pl.BlockSpec(shape, lambda hd: (0,) * len(shape))
    hblock = pl.BlockSpec((None, None, SSD_HEAD_DIM, SSD_STATE, batch), lambda hd: (layer, hd, 0, 0, 0))
    return _stacked_call(
        kern, name="ssd_lanes", grid=(SSD_HEADS,),
        inputs=[pt, cst, h0, w["cw_b"], w["cb_b"], w["dtb_b"], w["alog_b"], w["dexp_b"], w["nw_b"]],
        in_specs=[full(PT_ROWS, tb), _layer_spec(layer, n_ch, keep * batch), hblock,
                  _layer_spec(layer, SSD_CONV, n_ch, batch), _layer_spec(layer, n_ch, batch),
                  _layer_spec(layer, SSD_HEADS, batch), _layer_spec(layer, SSD_HEADS, batch),
                  _layer_spec(layer, SSD_HEADS, batch), _layer_spec(layer, D_MODEL, batch)],
        out_shape=(jax.ShapeDtypeStruct((D_MODEL, tb), bf16),
                   jax.ShapeDtypeStruct((DEPTH, n_ch, keep * batch), f32),
                   jax.ShapeDtypeStruct((DEPTH, SSD_HEADS, SSD_HEAD_DIM, SSD_STATE, batch), f32)),
        out_specs=(full(D_MODEL, tb), _layer_spec(layer, n_ch, keep * batch), hblock),
        stacked={1: prev and prev[0], 2: prev and prev[1]},
        scratch_shapes=[pltpu.VMEM((n_ch, tb), f32), pltpu.VMEM((SSD_HEADS, tb), f32),
                        pltpu.VMEM((SSD_HEADS, tb), f32), pltpu.VMEM((D_MODEL, tb), f32)],
        dimension_semantics=("arbitrary",))


def _mlstm_kernel(*refs, q, lr, nc, has_state, gs, cps, perm, tile3d):
    seq = functools.partial(_mlstm_seq, q=q, lr=lr, nc=nc, has_state=has_state, cps=cps, perm=perm)
    _per_sequence(seq, refs, 5, 3, 2, gs, has_state, nc, cps * q, tile3d)


def _mlstm_seq(*refs, q, lr, nc, has_state, cps, perm):
    q_ref, k_ref, v_ref, o_ref, sm_ref = refs[:5]
    refs = refs[5:]
    if has_state:
        c0_ref, n0_ref, m0_ref = refs[:3]
        refs = refs[3:]
    gb_ref, nw_ref, h_ref, cout_ref, nout_ref, mout_ref, cs, ns, ms = refs
    direct = has_state and nc == 1 and cps == 1

    def init():
        if direct:
            return
        if has_state:
            cs[...] = c0_ref[...]
            ns[...] = n0_ref[...]
            ms[...] = m0_ref[...]
        else:
            cs[...] = jnp.zeros_like(cs)
            ns[...] = jnp.zeros_like(ns)
            ms[...] = jnp.zeros_like(ms)

    def final():
        if direct:
            return
        cout_ref[...] = cs[...]
        nout_ref[...] = ns[...]
        mout_ref[...] = ms[...]

    def body():
        chunks = []
        for k in range(cps):
            sub = lambda r, k=k: r.rows(k * q, q)
            is_last = (pl.program_id(1) == nc - 1) if k == cps - 1 else False
            src = (c0_ref, n0_ref, m0_ref) if direct else (cs, ns, ms)
            dst = (cout_ref, nout_ref, mout_ref) if direct else (cs, ns, ms)
            chunks.append(_mlstm_body(sub(q_ref), sub(k_ref), sub(v_ref), sub(o_ref), sub(sm_ref), gb_ref, nw_ref,
                                      sub(h_ref), src, dst, q=q, lr=lr, is_last=is_last, perm=perm))
        yield from _staggered(chunks, MLSTM_STATE_STAGES)

    return init, body, final


MLSTM_STATE_STAGES = 8


def _staggered(gens, skew):
    done = [False] * len(gens)
    t = 0
    while not all(done):
        for i, g in enumerate(gens):
            if done[i] or t < i * skew:
                continue
            try:
                next(g)
            except StopIteration:
                done[i] = True
        t += 1
        yield


def _mlstm_body(q_ref, k_ref, v_ref, o_ref, sm_ref, gb_ref, nw_ref, h_ref, src, dst, *, q, lr, is_last, perm):
    c_src, n_src, m_src = src
    c_dst, n_dst, m_dst = dst
    sm = sm_ref[...] + gb_ref[...]
    logf = -_softplus(-sm)
    ipre = sm
    if lr < q:
        valid = _valid_rows(q, SMALL, lr, is_last)
        logf = jnp.where(valid, logf, 0.0)
        ipre = jnp.where(valid, ipre, NEG_BIG)
    causal = _tri(q, perm)
    tril = jnp.where(causal, 1.0, 0.0).astype(bf16)
    yield
    bcum = _dot01_lhs(tril, logf)
    ipre_t = ipre.T
    yield
    bcum_t = bcum.T
    lane = lax.broadcasted_iota(jnp.int32, (1, SMALL), 1)
    k_scale = MLSTM_HEAD_DIM ** -0.5

    heads = range(MLSTM_HEADS)
    sls = [slice(h * MLSTM_HEAD_DIM, (h + 1) * MLSTM_HEAD_DIM) for h in heads]
    q_all, k_all, v_all, o_all = q_ref[...], k_ref[...], v_ref[...], o_ref[...]
    qs = [q_all[:, sl] for sl in sls]
    ks = [k_all[:, sl] * k_scale for sl in sls]
    vs = [v_all[:, sl] for sl in sls]
    b_cols = [bcum[:, F_OFF + h:F_OFF + h + 1] for h in heads]
    i_cols = [ipre[:, I_OFF + h:I_OFF + h + 1] for h in heads]
    dmats = [jnp.where(causal, b_cols[h] - bcum_t[F_OFF + h:F_OFF + h + 1, :] + ipre_t[I_OFF + h:I_OFF + h + 1, :],
                       -jnp.inf) for h in heads]
    yield
    qk = [lax.dot_general(qs[h], ks[h], NT_DIMS, preferred_element_type=f32) for h in heads]
    d_max = [jnp.max(dmats[h], axis=-1, keepdims=True) for h in heads]
    yield
    n_all = n_src[...]
    m_all = m_src[...]
    m_new = m_all
    cs_in = [c_src[h] for h in heads]
    m_prevs = [m_all[:, h:h + 1] for h in heads]
    qc = [_dot(qs[h], cs_in[h].astype(bf16)) for h in heads]
    inters = [b_cols[h] + m_prevs[h] for h in heads]
    m_ts = [jnp.maximum(inters[h], d_max[h]) for h in heads]
    yield
    w_inters = [jnp.exp(inters[h] - m_ts[h]) for h in heads]
    ss = [qk[h] * jnp.exp(dmats[h] - m_ts[h]) for h in heads]
    yield
    sv = [_dot(ss[h].astype(bf16), vs[h]) for h in heads]
    m_ends = [m_ts[h][q - 1:q, :] for h in heads]
    b_lasts = [b_cols[h][q - 1:q, :] for h in heads]
    kws = [ks[h].astype(f32) * jnp.exp(b_lasts[h] - b_cols[h] + i_cols[h] - m_ends[h]) for h in heads]
    yield
    kv = [lax.dot_general(kws[h].astype(bf16), vs[h], TN_DIMS, preferred_element_type=f32) for h in heads]
    qns = [jnp.sum(qs[h].astype(f32) * n_all[h:h + 1, :], axis=-1, keepdims=True) for h in heads]
    yield
    dens = [jnp.sum(ss[h], axis=-1, keepdims=True) + w_inters[h] * qns[h] for h in heads]
    yield
    hvs = [(sv[h] + w_inters[h] * qc[h]) / jnp.maximum(jnp.abs(dens[h]), jnp.exp(-m_ts[h])) for h in heads]
    yield
    rms = [lax.rsqrt(jnp.mean(jnp.square(hvs[h]), axis=-1, keepdims=True) + EPS) for h in heads]
    yield
    h_new = [(hvs[h] * rms[h] * nw_ref[:, sls[h]] * jax.nn.sigmoid(o_all[:, sls[h]])).astype(h_ref.dtype)
             for h in heads]
    w_cs = [jnp.exp(b_lasts[h] + m_prevs[h] - m_ends[h]) for h in heads]
    yield
    for h in heads:
        c_dst[h] = w_cs[h] * cs_in[h] + kv[h]
        m_new = jnp.where(lane == h, m_ends[h], m_new)
    h_ref[...] = jnp.concatenate(h_new, axis=1)
    n_dst[...] = jnp.concatenate(
        [w_cs[h] * n_all[h:h + 1, :] + jnp.sum(kws[h], axis=0, keepdims=True) for h in heads], axis=0)
    m_dst[...] = m_new


def _merge_kernel(ys_ref, hm_ref, ga_ref, gb_ref, x_ref, wa_ref, wb_ref, wo_ref, g_ref, b_ref, o_ref):
    tm = x_ref.shape[0]
    rows = [pl.ds(i * (tm // MERGE_PARTS), tm // MERGE_PARTS) for i in range(MERGE_PARTS)]
    br = [(_dot(ys_ref[r, :], wa_ref[...]), _dot(hm_ref[r, :], wb_ref[...])) for r in rows]
    merged = [(jax.nn.sigmoid(ga_ref[r, :]) * a + jax.nn.sigmoid(gb_ref[r, :]) * b).astype(bf16)
              for r, (a, b) in zip(rows, br)]
    mix = [_dot(m, wo_ref[...]) for m in merged]
    for r, m in zip(rows, mix):
        o_ref[r, :] = _layer_norm(ALPHA * x_ref[r, :] + m, g_ref[...], b_ref[...])


def _merge(ys, hm, p32, x, w, layer, tm):
    m = x.shape[0]
    row = lambda blk: pl.BlockSpec((tm, D_MODEL), lambda i: (i, blk))
    return pl.pallas_call(
        _merge_kernel,
        out_shape=jax.ShapeDtypeStruct((m, D_MODEL), f32),
        grid=(m // tm,),
        in_specs=[row(0), row(0), row(P32_GA), row(P32_GB), row(0),
                  _layer_spec(layer, D_MODEL, D_MODEL), _layer_spec(layer, D_MODEL, D_MODEL),
                  _layer_spec(layer, D_MODEL, D_MODEL), _layer_spec(layer, 1, D_MODEL),
                  _layer_spec(layer, 1, D_MODEL)],
        out_specs=row(0),
        compiler_params=pltpu.CompilerParams(dimension_semantics=("parallel",)),
        name="merge",
    )(ys, hm, p32, p32, x, w["wa"], w["wb"], w["wo"], w["ln1_g"], w["ln1_b"])


def _ffn_kernel(*refs, tm, seq_len, perm_q):
    multi = seq_len > 0
    keep_rows = (FFN_CONV - 1) * SUBLANES
    if multi:
        x_ref, st_ref, wup_ref, cw_ref, cb_ref, wdn_ref, g_ref, b_ref, o_ref, sout_ref, xp = refs
    else:
        x_ref, wup_ref, cw_ref, cb_ref, wdn_ref, g_ref, b_ref, o_ref, sout_ref, xp, carry = refs
        @pl.when(pl.program_id(1) == 0)
        def _():
            carry[...] = jnp.zeros_like(carry)

    hdr = SUBLANES
    x = x_ref[...]
    xb = x.astype(bf16)
    if multi:
        assert seq_len & (seq_len - 1) == 0
        nseq = tm // seq_len
        t = lax.broadcasted_iota(jnp.int32, (tm, FF_CH), 0) & (seq_len - 1)
        row = lax.broadcasted_iota(jnp.int32, (tm, 2 * nseq), 0)
        col = lax.broadcasted_iota(jnp.int32, (tm, 2 * nseq), 1)
        t_sel = row & (seq_len - 1)
        seq0 = lax.shift_right_logical(row - t_sel, (seq_len // 2).bit_length() - 1)
        sel_p2 = jnp.where(col == seq0 + t_sel, jnp.where(t_sel < 2, 1.0, 0.0), 0.0).astype(bf16)
        sel_p1 = jnp.where(col == seq0 + 1, jnp.where(t_sel == 0, 1.0, 0.0), 0.0).astype(bf16)
        xp[:, 0:hdr, :] = jnp.zeros((FF_XP_SLOTS, hdr, FF_CH), f32)

    def cols_of(c, part):
        return slice(part * D_FF + c * FF_CH, part * D_FF + (c + 1) * FF_CH)

    def up(c):
        return [_dot(xb, wup_ref[:, cols_of(c, part)]) for part in range(2)]

    def conv_act(c, us):
        halves = []
        for part, u in enumerate(us):
            cols = cols_of(c, part)
            slot = (2 * c + part) % FF_XP_SLOTS
            if perm_q:
                prev = carry[:, cols]
                p1, p2 = [], []
                for kk in range(tm // perm_q):
                    uc = u[kk * perm_q:(kk + 1) * perm_q, :]
                    wrapped = _wrap_rows(uc[perm_q - keep_rows:, :], prev)
                    p1.append(_shift_back(uc, wrapped, 1))
                    p2.append(_shift_back(uc, wrapped, 2))
                    prev = uc[perm_q - keep_rows:, :]
                carry[:, cols] = prev
                p1, p2 = jnp.concatenate(p1, axis=0), jnp.concatenate(p2, axis=0)
                w = cw_ref[:, cols]
                halves.append(p2 * w[0:1, :] + p1 * w[1:2, :] + u * w[2:3, :] + cb_ref[:, cols])
                continue
            xp[slot, hdr:hdr + tm, :] = u
            if multi:
                sout_ref[:, cols] = u
                st = st_ref[:, cols]
                p1 = jnp.where(t == 0, _dot01_lhs(sel_p1, st), xp[slot, hdr - 1:hdr - 1 + tm, :])
                p2 = jnp.where(t < 2, _dot01_lhs(sel_p2, st), xp[slot, hdr - 2:hdr - 2 + tm, :])
            else:
                xp[slot, 0:hdr, :] = carry[:, cols]
                p1 = xp[slot, hdr - 1:hdr - 1 + tm, :]
                p2 = xp[slot, hdr - 2:hdr - 2 + tm, :]
                carry[:, cols] = u[tm - hdr:tm, :]
            w = cw_ref[:, cols]
            halves.append(p2 * w[0:1, :] + p1 * w[1:2, :] + u * w[2:3, :] + cb_ref[:, cols])
        return (_silu(halves[0]) * halves[1]).astype(bf16)

    acc = None
    ahead = [up(c) for c in range(min(FF_UP_AHEAD, FF_NCH))]
    pending = None
    for c in range(FF_NCH):
        if c + FF_UP_AHEAD < FF_NCH:
            ahead.append(up(c + FF_UP_AHEAD))
        if pending is not None:
            d = _dot(pending, wdn_ref[(c - 1) * FF_CH:c * FF_CH, :])
            acc = d if acc is None else acc + d
        pending = conv_act(c, ahead.pop(0))
    acc = acc + _dot(pending, wdn_ref[(FF_NCH - 1) * FF_CH:FF_NCH * FF_CH, :])

    if not multi:
        sout_ref[...] = carry[...]
    o_ref[...] = _layer_norm(ALPHA * x + acc, g_ref[...], b_ref[...])


def _ffn(x, st, w, layer, *, groups, tm, seq_len, perm_q=0):
    m = x.shape[0]
    tiles = m // (groups * tm)
    multi = seq_len > 0
    kern = functools.partial(_ffn_kernel, tm=tm, seq_len=seq_len, perm_q=perm_q)
    carry_rows = (FFN_CONV - 1) * SUBLANES if perm_q else SUBLANES
    once = dict(pipeline_mode=pl.Buffered(1))
    x_spec = pl.BlockSpec((tm, D_MODEL), lambda s, j: (s * tiles + j, 0))
    w_specs = [_layer_spec(layer, D_MODEL, 2 * D_FF, **once), _layer_spec(layer, FFN_CONV, 2 * D_FF, **once),
               _layer_spec(layer, 1, 2 * D_FF, **once), _layer_spec(layer, D_FF, D_MODEL, **once),
               _layer_spec(layer, 1, D_MODEL, **once), _layer_spec(layer, 1, D_MODEL, **once)]
    w_args = (w["wup"], w["fcw"], w["fcb"], w["wdn"], w["ln2_g"], w["ln2_b"])
    xp = pltpu.VMEM((FF_XP_SLOTS, SUBLANES + (0 if perm_q else tm), FF_CH), f32)
    if multi:
        nst = 2 * (tm // seq_len)
        inputs = (x, st) + w_args
        in_specs = [x_spec, pl.BlockSpec((None, nst, 2 * D_FF), lambda s, j: (layer, s * tiles + j, 0))] + w_specs
        sout_shape = jax.ShapeDtypeStruct((m, 2 * D_FF), f32)
        sout_spec = pl.BlockSpec((tm, 2 * D_FF), lambda s, j: (s * tiles + j, 0))
        scratch = [xp]
    else:
        inputs = (x,) + w_args
        in_specs = [x_spec] + w_specs
        sout_shape = jax.ShapeDtypeStruct((groups, carry_rows, 2 * D_FF), f32)
        sout_spec = pl.BlockSpec((None, carry_rows, 2 * D_FF), lambda s, j: (s, 0, 0))
        scratch = [xp, pltpu.VMEM((carry_rows, 2 * D_FF), f32)]
    return pl.pallas_call(
        kern,
        out_shape=(jax.ShapeDtypeStruct((m, D_MODEL), f32), sout_shape),
        grid=(groups, tiles),
        in_specs=in_specs,
        out_specs=(x_spec, sout_spec),
        scratch_shapes=scratch,
        compiler_params=pltpu.CompilerParams(dimension_semantics=("parallel", "arbitrary"),
                                             vmem_limit_bytes=56 * 1024 * 1024),
        name="ffn",
    )(*inputs)


def _pad_lanes(v, off, width=SMALL):
    out = jnp.zeros((v.shape[0], 1, width), f32)
    return out.at[:, 0, off:off + v.shape[1]].set(v.astype(f32))


def _prep_weights(w_in, ssd_conv_w, ssd_conv_b, ssd_dt_bias, ssd_a_log, ssd_d, ssd_norm_w, mlstm_gate_b,
                  mlstm_norm_w, w_branch_a, w_branch_b, w_out, ln1_g, ln1_b, ffn_w_up, ffn_conv_w,
                  ffn_conv_b, ffn_w_down, ln2_g, ln2_b):
    d = D_MODEL
    o_z, o_xbc, o_dt = 0, d, d + d + SSD_BC
    o_q = o_dt + SSD_HEADS
    o_if = o_q + 3 * d
    o_o = o_if + 2 * MLSTM_HEADS
    o_g = o_o + d
    w_t = jnp.swapaxes(w_in, 1, 2)
    cols = lambda a, n: w_t[:, a:a + n, :]
    zeros = lambda n: jnp.zeros((DEPTH, n, d), w_in.dtype)
    w32 = jnp.concatenate([cols(o_z, d), cols(o_o, d), cols(o_g, 2 * d), cols(o_xbc, d + SSD_BC),
                           cols(o_dt, SSD_HEADS), cols(o_if, 2 * MLSTM_HEADS),
                           zeros(P32_W - P32_SM_OFF - SSD_HEADS - 2 * MLSTM_HEADS)], axis=1).astype(bf16)
    e = (np.arange(SSD_HP)[None, :] // SSD_HEAD_DIM == np.arange(LANES)[:, None])
    bd = ((np.arange(SSD_GROUPS * SSD_STATE)[:, None] < SSD_STATE)
          == (np.arange(SSD_HP)[None, :] < SSD_HP // SSD_GROUPS))
    row = lambda a: a[:, None, :]
    return dict(
        w32=w32, wqkv=cols(o_q, 3 * d).astype(bf16),
        cwx=ssd_conv_w[:, :, :d], cbx=row(ssd_conv_b[:, :d]),
        cwb=ssd_conv_w[:, :, d:], cbb=row(ssd_conv_b[:, d:]),
        dtb=_pad_lanes(ssd_dt_bias, DT_OFF), alog=_pad_lanes(ssd_a_log, DT_OFF),
        dexp=row(jnp.repeat(ssd_d.astype(f32), SSD_HEAD_DIM, axis=1)), ssd_nw=row(ssd_norm_w),
        e=jnp.asarray(e, bf16), bd=jnp.asarray(bd, f32),
        gate_b=_pad_lanes(mlstm_gate_b, I_OFF), mlstm_nw=row(mlstm_norm_w),
        wa=w_branch_a.astype(bf16), wb=w_branch_b.astype(bf16), wo=w_out.astype(bf16),
        ln1_g=row(ln1_g), ln1_b=row(ln1_b),
        wup=ffn_w_up.astype(bf16), fcw=ffn_conv_w, fcb=row(ffn_conv_b), wdn=ffn_w_down.astype(bf16),
        ln2_g=row(ln2_g), ln2_b=row(ln2_b),
    )


class _Group:
    def __init__(self, batch, length, q, lr, gs, ssd_cps, mlstm_cps, proj_tm, p32_tn, qkv_tn, merge_tm, ffn,
                 perm=False, lanes_ssd=False, mlstm_gs=None, ssd_gs=None):
        self.mlstm_gs = gs if mlstm_gs is None else mlstm_gs
        self.ssd_gs = gs if ssd_gs is None else ssd_gs
        self.perm = perm
        self.lanes_ssd = lanes_ssd
        self.batch, self.length, self.q, self.lr, self.gs = batch, length, q, lr, gs
        self.ssd_cps, self.mlstm_cps = ssd_cps, mlstm_cps
        self.rows = batch * length
        self.proj_tm, self.p32_tn, self.qkv_tn, self.merge_tm, self.ffn = proj_tm, p32_tn, qkv_tn, merge_tm, ffn

    def cfg(self, name, layer):
        v = getattr(self, name)
        return v[layer] if isinstance(v, tuple) else v

    def tiling(self, cps, gs=None):
        rows = cps * self.q
        gs = self.gs if gs is None else gs
        steps = self.length // rows
        if gs == 1 or steps == 1:
            spec = lambda width, blk: pl.BlockSpec((gs * rows, width), lambda b, c: (b * steps + c, blk))
            return steps, spec, (lambda a: a), False
        spec = lambda width, blk: pl.BlockSpec((gs, rows, width), lambda b, c: (b, c, blk))
        return steps, spec, (lambda a: a.reshape(self.batch, self.length, a.shape[-1])), True


def _ssd(grp, p32, state, w, layer, prev):
    q, b, gs = grp.q, grp.batch, grp.cfg("ssd_gs", layer)
    has_state = state is not None
    cps = grp.cfg("ssd_cps", layer)
    steps, tile, view, tile3d = grp.tiling(cps, gs)
    kern = functools.partial(_ssd_kernel, q=q, lr=grp.lr, nc=steps, has_state=has_state, gs=gs, cps=cps,
                             perm=grp.perm, tile3d=tile3d)
    p32 = view(p32)
    inputs = [p32, p32, p32, p32]
    in_specs = [tile(D_MODEL, P32_Z), tile(D_MODEL, P32_XS),
                tile(SSD_BC, P32_BC_OFF // SSD_BC), tile(SMALL, P32_SM_OFF // SMALL)]
    if has_state:
        inputs += [state["csx"], state["csb"], state["h"]]
        in_specs += [_seq_spec(layer, gs, SSD_CONV - 1, D_MODEL), _seq_spec(layer, gs, SSD_CONV - 1, SSD_BC),
                     _seq_spec(layer, gs, SSD_HP, SSD_STATE)]
    inputs += [w["cwx"], w["cbx"], w["cwb"], w["cbb"], w["dtb"], w["alog"], w["dexp"], w["ssd_nw"], w["e"],
               w["bd"]]
    const = lambda *shape: pl.BlockSpec(shape, lambda b, c: (0,) * len(shape))
    in_specs += [_layer_spec(layer, SSD_CONV, D_MODEL), _layer_spec(layer, 1, D_MODEL),
                 _layer_spec(layer, SSD_CONV, SSD_BC), _layer_spec(layer, 1, SSD_BC),
                 _layer_spec(layer, 1, SMALL), _layer_spec(layer, 1, SMALL), _layer_spec(layer, 1, D_MODEL),
                 _layer_spec(layer, 1, D_MODEL), const(LANES, SSD_HP), const(SSD_GROUPS * SSD_STATE, SSD_HP)]
    ys, *new = _stacked_call(
        kern, name="ssd", grid=(b // gs, steps), inputs=inputs, in_specs=in_specs,
        out_shape=(jax.ShapeDtypeStruct((b, grp.length, D_MODEL) if tile3d else (grp.rows, D_MODEL), bf16),
                   jax.ShapeDtypeStruct((DEPTH, b, SSD_CONV - 1, D_MODEL), f32),
                   jax.ShapeDtypeStruct((DEPTH, b, SSD_CONV - 1, SSD_BC), f32),
                   jax.ShapeDtypeStruct((DEPTH, b, SSD_HP, SSD_STATE), f32)),
        out_specs=(tile(D_MODEL, 0), _seq_spec(layer, gs, SSD_CONV - 1, D_MODEL),
                   _seq_spec(layer, gs, SSD_CONV - 1, SSD_BC), _seq_spec(layer, gs, SSD_HP, SSD_STATE)),
        stacked={1: prev and prev[0], 2: prev and prev[1], 3: prev and prev[2]},
        scratch_shapes=[pltpu.VMEM((gs, SUBLANES + q, D_MODEL), f32), pltpu.VMEM((gs, SUBLANES + q, SSD_BC), f32),
                        pltpu.VMEM((gs, SSD_GROUPS * SSD_STATE, SSD_HP), f32), pltpu.VMEM((gs, q, D_MODEL), f32)],
        dimension_semantics=("parallel", "arbitrary"))
    return (ys.reshape(grp.rows, D_MODEL), *new)


def _mlstm(grp, qkv, p32, state, w, layer, prev):
    q, b, gs = grp.q, grp.batch, grp.mlstm_gs
    has_state = state is not None
    cps = grp.cfg("mlstm_cps", layer)
    steps, tile, view, tile3d = grp.tiling(cps, gs)
    kern = functools.partial(_mlstm_kernel, q=q, lr=grp.lr, nc=steps, has_state=has_state, gs=gs, cps=cps,
                             perm=grp.perm, tile3d=tile3d)
    hd = MLSTM_HEAD_DIM
    carried = not (has_state and steps == 1 and cps == 1)
    qkv, p32 = view(qkv), view(p32)
    inputs = [qkv, qkv, qkv, p32, p32]
    in_specs = [tile(D_MODEL, 0), tile(D_MODEL, 1), tile(D_MODEL, 2), tile(D_MODEL, P32_O),
                tile(SMALL, P32_SM_OFF // SMALL)]
    if has_state:
        inputs += [state["c"], state["n"], state["m"]]
        in_specs += [_seq_spec(layer, gs, MLSTM_HEADS, hd, hd), _seq_spec(layer, gs, MLSTM_HEADS, hd),
                     _seq_spec(layer, gs, 1, SMALL)]
    inputs += [w["gate_b"], w["mlstm_nw"]]
    in_specs += [_layer_spec(layer, 1, SMALL), _layer_spec(layer, 1, D_MODEL)]
    hm, *new = _stacked_call(
        kern, name="mlstm", grid=(b // gs, steps), inputs=inputs, in_specs=in_specs,
        out_shape=(jax.ShapeDtypeStruct((b, grp.length, D_MODEL) if tile3d else (grp.rows, D_MODEL), bf16),
                   jax.ShapeDtypeStruct((DEPTH, b, MLSTM_HEADS, hd, hd), f32),
                   jax.ShapeDtypeStruct((DEPTH, b, MLSTM_HEADS, hd), f32),
                   jax.ShapeDtypeStruct((DEPTH, b, 1, SMALL), f32)),
        out_specs=(tile(D_MODEL, 0), _seq_spec(layer, gs, MLSTM_HEADS, hd, hd),
                   _seq_spec(layer, gs, MLSTM_HEADS, hd), _seq_spec(layer, gs, 1, SMALL)),
        stacked={1: prev and prev[0], 2: prev and prev[1], 3: prev and prev[2]},
        scratch_shapes=[pltpu.VMEM((gs, MLSTM_HEADS, hd, hd) if carried else (gs, 1, SUBLANES, LANES), f32),
                        pltpu.VMEM((gs, MLSTM_HEADS, hd), f32), pltpu.VMEM((gs, 1, SMALL), f32)],
        dimension_semantics=("parallel", "arbitrary"))
    return (hm.reshape(grp.rows, D_MODEL), *new)


def _trunk(grp, x, state, w):
    ssd_out = mlstm_out = None
    ffn_out = []
    for layer in range(DEPTH):
        p32 = _proj(x, w["w32"], layer, f32, grp.cfg("proj_tm", layer), grp.cfg("p32_tn", layer))
        qkv = _proj(x, w["wqkv"], layer, bf16, grp.cfg("proj_tm", layer), grp.cfg("qkv_tn", layer))
        if grp.lanes_ssd:
            b, t = grp.batch, grp.length
            x_tm = x.reshape(b, t, D_MODEL).swapaxes(0, 1).reshape(t * b, D_MODEL)
            ys_t, *ssd_out = _ssd_lanes(_proj_t(x_tm, w["w32"], layer), state["cs_t"], state["h_lanes"], w, layer,
                                        ssd_out, steps=t, batch=b)
            ys = ys_t.reshape(D_MODEL, t, b).transpose(2, 1, 0).reshape(b * t, D_MODEL)
        else:
            ys, *ssd_out = _ssd(grp, p32, state, w, layer, ssd_out)
        hm, *mlstm_out = _mlstm(grp, qkv, p32, state, w, layer, mlstm_out)
        x1 = _merge(ys, hm, p32, x, w, layer, grp.cfg("merge_tm", layer))
        x, s_ffn = _ffn(x1, state["ffn"] if state is not None else None, w, layer, **grp.ffn)
        ffn_out.append(s_ffn)
    return x, ssd_out, mlstm_out, ffn_out


def _unpack_states(batch, ssd_out, mlstm_out):
    csx, csb, h = ssd_out
    c, n, m = mlstm_out
    return (h.reshape(DEPTH, batch, SSD_HEADS, SSD_HEAD_DIM, SSD_STATE),
            jnp.concatenate([csx, csb], axis=-1), c, n, m[:, :, 0, :MLSTM_HEADS])


def kernel(x_prompt, x_sample, state_ssd, state_ssd_conv, state_mlstm_c, state_mlstm_n, state_mlstm_m,
           state_ffn_conv, w_in, ssd_conv_w, ssd_conv_b, ssd_dt_bias, ssd_a_log, ssd_d, ssd_norm_w,
           mlstm_gate_b, mlstm_norm_w, w_branch_a, w_branch_b, w_out, ln1_g, ln1_b, ffn_w_up, ffn_conv_w,
           ffn_conv_b, ffn_w_down, ln2_g, ln2_b):
    w = _prep_weights(w_in, ssd_conv_w, ssd_conv_b, ssd_dt_bias, ssd_a_log, ssd_d, ssd_norm_w, mlstm_gate_b,
                      mlstm_norm_w, w_branch_a, w_branch_b, w_out, ln1_g, ln1_b, ffn_w_up, ffn_conv_w,
                      ffn_conv_b, ffn_w_down, ln2_g, ln2_b)
    keep = FFN_CONV - 1
    keep_ssd = SSD_CONV - 1

    bp, lp, _ = x_prompt.shape
    prompt = _Group(bp, lp, CHUNK, CHUNK, gs=1, ssd_cps=4, mlstm_cps=4, proj_tm=(2048, 1024), p32_tn=(1408, 2816),
                    qkv_tn=(1536, 3072), merge_tm=1024, ffn=dict(groups=bp, tm=1024, seq_len=0, perm_q=CHUNK),
                    perm=True, mlstm_gs=2, ssd_gs=(1, 2))
    per = CHUNK // SUBLANES
    xp_rows = x_prompt.reshape(bp, lp // CHUNK, SUBLANES, per, D_MODEL).swapaxes(2, 3)
    y_p, ssd_p, mlstm_p, ffn_p = _trunk(prompt, xp_rows.reshape(bp * lp, D_MODEL), None, w)
    y_p = y_p.reshape(bp, lp // CHUNK, per, SUBLANES, D_MODEL).swapaxes(2, 3)
    st_p = _unpack_states(bp, ssd_p, mlstm_p)
    ffn_conv_p = jnp.stack(ffn_p)[:, :, SUBLANES - 1::SUBLANES, :]

    bs, ls, _ = x_sample.shape
    lpad = max(ls, SAMPLE_PAD_LEN)
    s_rows = bs * lpad
    sample = _Group(bs, lpad, lpad, ls, gs=8, ssd_cps=1, mlstm_cps=1, proj_tm=s_rows, p32_tn=512, qkv_tn=1024,
                    merge_tm=512,
                    ffn=dict(groups=1, tm=256, seq_len=lpad), lanes_ssd=True)
    assert lpad == ls
    lane_b = lambda a: jnp.broadcast_to(a.astype(f32)[..., None], a.shape + (bs,))
    w.update(cw_b=lane_b(ssd_conv_w), cb_b=lane_b(ssd_conv_b), dtb_b=lane_b(ssd_dt_bias), alog_b=lane_b(ssd_a_log),
             dexp_b=lane_b(ssd_d), nw_b=lane_b(ssd_norm_w))
    s_state = dict(
        cs_t=jnp.transpose(state_ssd_conv, (0, 3, 2, 1)).reshape(DEPTH, D_MODEL + SSD_BC, keep_ssd * bs),
        h_lanes=jnp.transpose(state_ssd, (0, 2, 3, 4, 1)),
        c=state_mlstm_c, n=state_mlstm_n,
        m=jnp.pad(state_mlstm_m, ((0, 0), (0, 0), (0, SMALL - MLSTM_HEADS)))[:, :, None, :],
        ffn=state_ffn_conv.reshape(DEPTH, bs * keep, 2 * D_FF),
    )
    xs = jnp.pad(x_sample, ((0, 0), (0, lpad - ls), (0, 0))).reshape(s_rows, D_MODEL)
    y_s, (cs_t, h_lanes), mlstm_s, ffn_s = _trunk(sample, xs, s_state, w)
    c_s, n_s, m_s = mlstm_s
    st_s = (jnp.transpose(h_lanes, (0, 4, 1, 2, 3)),
            jnp.transpose(cs_t.reshape(DEPTH, D_MODEL + SSD_BC, keep_ssd, bs), (0, 3, 2, 1)),
            c_s, n_s, m_s[:, :, 0, :MLSTM_HEADS])
    ffn_conv_s = jnp.stack([u.reshape(bs, lpad, 2 * D_FF)[:, ls - keep:ls, :] for u in ffn_s])
    y_sample = y_s.reshape(bs, lpad, D_MODEL)[:, :ls, :]

    return (y_p.reshape(bp, lp, D_MODEL), y_sample, st_p[0], st_s[0], st_p[1], st_s[1], st_p[2], st_s[2],
            st_p[3], st_s[3], st_p[4], st_s[4], ffn_conv_p, ffn_conv_s)
```

```python
import functools
import itertools

import jax
import jax.numpy as jnp
import numpy as np
from jax import lax
from jax.experimental import pallas as pl
from jax.experimental.pallas import tpu as pltpu

f32 = jnp.float32
bf16 = jnp.bfloat16

D_MODEL = 1024
DEPTH = 2
SSD_HEADS = 16
SSD_HEAD_DIM = 64
SSD_STATE = 64
SSD_GROUPS = 2
SSD_CONV = 4
SSD_BC = 2 * SSD_GROUPS * SSD_STATE
SSD_HP = SSD_HEADS * SSD_HEAD_DIM
MLSTM_HEADS = 4
MLSTM_HEAD_DIM = 256
CHUNK = 128
D_FF = 2816
FFN_CONV = 3
ALPHA = (2 * DEPTH) ** 0.25
EPS = 1e-5

LANES = 128
SUBLANES = 8
SMALL = LANES
DT_OFF, I_OFF, F_OFF = 0, 16, 20
P32_Z, P32_O, P32_GA, P32_GB, P32_XS = 0, 1, 2, 3, 4
P32_BC_OFF = 5 * D_MODEL
P32_SM_OFF = P32_BC_OFF + SSD_BC
P32_W = P32_SM_OFF + 2 * SMALL
FF_CH = 256
FF_NCH = D_FF // FF_CH
MERGE_PARTS = 2
FF_UP_AHEAD = 2
FF_XP_SLOTS = 4
NEG_BIG = -1e30
SAMPLE_PAD_LEN = 4

NT_DIMS = (((1,), (1,)), ((), ()))
TN_DIMS = (((0,), (0,)), ((), ()))


def _dot(a, b):
    return jnp.dot(a, b, preferred_element_type=f32)


def _split3(x):
    hi = x.astype(bf16)
    r = x - hi.astype(f32)
    mid = r.astype(bf16)
    lo = (r - mid.astype(f32)).astype(bf16)
    return hi, mid, lo


def _dot01_rhs(x, e):
    hi, mid, lo = _split3(x)
    return _dot(hi, e) + _dot(mid, e) + _dot(lo, e)


def _dot01_lhs(t, x):
    hi, mid, lo = _split3(x)
    return _dot(t, hi) + _dot(t, mid) + _dot(t, lo)


def _softplus(x):
    return jnp.maximum(x, 0.0) + jnp.log1p(jnp.exp(-jnp.abs(x)))


def _silu(x):
    return x * jax.nn.sigmoid(x)


def _row_time(i, q, perm):
    if not perm:
        return i
    return (i & (SUBLANES - 1)) * (q // SUBLANES) + lax.shift_right_logical(i, SUBLANES.bit_length() - 1)


def _tri(q, perm=False):
    row = lax.broadcasted_iota(jnp.int32, (q, q), 0)
    col = lax.broadcasted_iota(jnp.int32, (q, q), 1)
    return _row_time(row, q, perm) >= _row_time(col, q, perm)


def _wrap_rows(cur_tail, prev_tail):
    out = []
    for i in range(cur_tail.shape[0] // SUBLANES):
        rows = slice(i * SUBLANES, (i + 1) * SUBLANES)
        first = lax.broadcasted_iota(jnp.int32, (SUBLANES, cur_tail.shape[1]), 0) == 0
        out.append(jnp.where(first, pltpu.roll(prev_tail[rows], 1, axis=0), pltpu.roll(cur_tail[rows], 1, axis=0)))
    return jnp.concatenate(out, axis=0)


def _shift_back(x, wrapped, j):
    n = j * SUBLANES
    return jnp.concatenate([wrapped[wrapped.shape[0] - n:], x[:x.shape[0] - n]], axis=0)


def _valid_rows(q, width, lr, is_last):
    row = lax.broadcasted_iota(jnp.int32, (q, width), 0)
    return row < jnp.where(is_last, lr, q)


def _layer_norm(r, g, b):
    mu = jnp.mean(r, axis=-1, keepdims=True)
    var = jnp.mean(jnp.square(r - mu), axis=-1, keepdims=True)
    return (r - mu) * lax.rsqrt(var + EPS) * g + b


def _layer_spec(layer, *shape, **kw):
    zeros = (0,) * len(shape)
    return pl.BlockSpec((None,) + shape, lambda *_: (layer,) + zeros, **kw)


def _seq_spec(layer, gs, *shape):
    zeros = (0,) * len(shape)
    return pl.BlockSpec((None, gs) + shape, lambda b, c: (layer, b) + zeros)


def _stacked_call(kern, *, name, grid, inputs, in_specs, out_shape, out_specs, stacked, scratch_shapes,
                  dimension_semantics, vmem_limit_bytes=None):
    prev = [(i, a) for i, a in sorted(stacked.items()) if a is not None]
    n_in = len(inputs)

    def body(*refs):
        kern(*refs[:n_in], *refs[n_in + len(prev):])

    return pl.pallas_call(
        body,
        out_shape=out_shape,
        grid=grid,
        in_specs=list(in_specs) + [pl.BlockSpec(memory_space=pl.ANY)] * len(prev),
        out_specs=out_specs,
        scratch_shapes=scratch_shapes,
        input_output_aliases={n_in + k: i for k, (i, _) in enumerate(prev)},
        compiler_params=pltpu.CompilerParams(dimension_semantics=dimension_semantics,
                                             vmem_limit_bytes=vmem_limit_bytes),
        name=name,
    )(*inputs, *[a for _, a in prev])


def _proj_kernel(x_ref, w_ref, o_ref, xb):
    @pl.when(pl.program_id(1) == 0)
    def _():
        xb[...] = x_ref[...].astype(bf16)

    o_ref[...] = lax.dot_general(xb[...], w_ref[...], NT_DIMS, preferred_element_type=f32).astype(o_ref.dtype)


def _proj(x, wt, layer, out_dtype, tm, tn):
    m, k = x.shape
    n = wt.shape[1]
    return pl.pallas_call(
        _proj_kernel,
        out_shape=jax.ShapeDtypeStruct((m, n), out_dtype),
        grid=(m // tm, n // tn),
        in_specs=[pl.BlockSpec((tm, k), lambda i, j: (i, 0)),
                  pl.BlockSpec((None, tn, k), lambda i, j: (layer, j, 0))],
        out_specs=pl.BlockSpec((tm, tn), lambda i, j: (i, j)),
        scratch_shapes=[pltpu.VMEM((tm, k), bf16)],
        compiler_params=pltpu.CompilerParams(dimension_semantics=("parallel", "arbitrary")),
        name="proj",
    )(x, wt)


class _Rows:
    def __init__(self, ref, start, n):
        self.ref, self.start, self.n, self.dtype = ref, start, n, ref.dtype

    def rows(self, off, n):
        return _Rows(self.ref, self.start + off, n)

    def _index(self, idx):
        cols = slice(None) if idx is Ellipsis else idx[1]
        return (slice(self.start, self.start + self.n), cols)

    def __getitem__(self, idx):
        return self.ref[self._index(idx)]

    def __setitem__(self, idx, value):
        self.ref[self._index(idx)] = value


def _per_sequence(seq_fn, refs, n_tile, n_state, n_param, gs, has_state, nc, rows, tile3d=False):
    n_state = n_state if has_state else 0
    tiles, refs = refs[:n_tile], refs[n_tile:]
    state, refs = refs[:n_state], refs[n_state:]
    params, (y_tile, *rest) = refs[:n_param], refs[n_param:]
    phases = []
    for g in range(gs):
        at = lambda group: tuple(r.at[g] for r in group)
        if tile3d:
            seq_rows = lambda group: tuple(_Rows(r.at[g], 0, rows) for r in group)
        else:
            seq_rows = lambda group: tuple(_Rows(r, g * rows, rows) for r in group)
        phases.append(seq_fn(*seq_rows(tiles), *at(state), *params, *seq_rows((y_tile,)), *at(rest)))
    c = pl.program_id(1)

    @pl.when(c == 0)
    def _():
        for init, _, _ in phases:
            init()

    for _ in itertools.zip_longest(*[body() for _, body, _ in phases]):
        pass

    @pl.when(c == nc - 1)
    def _():
        for _, _, final in phases:
            final()


def _ssd_kernel(*refs, q, lr, nc, has_state, gs, cps, perm, tile3d):
    seq = functools.partial(_ssd_seq, q=q, lr=lr, nc=nc, has_state=has_state, cps=cps, perm=perm)
    _per_sequence(seq, refs, 4, 3, 10, gs, has_state, nc, cps * q, tile3d)


def _ssd_seq(*refs, q, lr, nc, has_state, cps, perm):
    z_ref, xs_ref, bc_ref, sm_ref = refs[:4]
    refs = refs[4:]
    if has_state:
        csx_ref, csb_ref, h0_ref = refs[:3]
        refs = refs[3:]
    (cwx_ref, cbx_ref, cwb_ref, cbb_ref, dtb_ref, alog_ref, dexp_ref, nw_ref, e_ref, bd_ref,
     y_ref, ncsx_ref, ncsb_ref, hout_ref, xpx, xpb, ht, yb) = refs
    hdr = SUBLANES
    lo = hdr - (SSD_CONV - 1)
    n2 = SSD_GROUPS * SSD_STATE
    assert lr >= SSD_CONV - 1
    keep = SSD_CONV - 1
    assert not (perm and (has_state or lr != q))
    carried = [(i + 1) * SUBLANES - 1 for i in range(keep)]

    def init():
        if has_state:
            h_t = h0_ref[...].T
            ht[...] = jnp.where(bd_ref[...] > 0.5, jnp.concatenate([h_t, h_t], axis=0), 0.0)
            xpx[lo:hdr, :] = csx_ref[...]
            xpb[lo:hdr, :] = csb_ref[...]
        else:
            ht[...] = jnp.zeros_like(ht)
            rows = slice(0, keep * SUBLANES) if perm else slice(lo, hdr)
            xpx[rows, :] = jnp.zeros((rows.stop - rows.start, SSD_HP), f32)
            xpb[rows, :] = jnp.zeros((rows.stop - rows.start, SSD_BC), f32)

    def final():
        if perm:
            for i, r in enumerate(carried):
                ncsx_ref[i:i + 1, :] = xpx[r:r + 1, :]
                ncsb_ref[i:i + 1, :] = xpb[r:r + 1, :]
        else:
            ncsx_ref[...] = xpx[lo + lr:hdr + lr, :]
            ncsb_ref[...] = xpb[lo + lr:hdr + lr, :]
        h_new = ht[...]
        hout_ref[...] = (h_new[:SSD_STATE, :] + h_new[SSD_STATE:, :]).T

    def body():
        for k in range(cps):
            sub = lambda r: r.rows(k * q, q)
            is_last = (pl.program_id(1) == nc - 1) if k == cps - 1 else False
            yield from _ssd_body(sub(z_ref), sub(xs_ref), sub(bc_ref), sub(sm_ref), cwx_ref, cbx_ref, cwb_ref,
                                 cbb_ref, dtb_ref, alog_ref, dexp_ref, nw_ref, e_ref, bd_ref, sub(y_ref),
                                 xpx, xpb, ht, yb, q=q, lr=lr, is_last=is_last, perm=perm)

    return init, body, final


def _ssd_body(z_ref, xs_ref, bc_ref, sm_ref, cwx_ref, cbx_ref, cwb_ref, cbb_ref, dtb_ref, alog_ref,
              dexp_ref, nw_ref, e_ref, bd_ref, y_ref, xpx, xpb, ht, yb, *, q, lr, is_last, perm):
    hdr = SUBLANES
    lo = hdr - (SSD_CONV - 1)
    n2 = SSD_GROUPS * SSD_STATE
    block_diag = bd_ref[...] > 0.5

    dt = _softplus(sm_ref[...] + dtb_ref[...])
    if lr < q:
        dt = jnp.where(_valid_rows(q, SMALL, lr, is_last), dt, 0.0)
    a = -jnp.exp(alog_ref[...])
    d_a = dt * a
    causal = _tri(q, perm)
    tril = jnp.where(causal, 1.0, 0.0).astype(bf16)
    e = e_ref[...]
    acs = _dot01_lhs(tril, d_a)
    dt_x = _dot01_rhs(dt, e)
    yield

    if perm:
        keep_rows = (SSD_CONV - 1) * SUBLANES

        def conv(xp, x_ref, w_ref, b_ref):
            w = w_ref[...]
            x = x_ref[...]
            wrapped = _wrap_rows(x[q - keep_rows:, :], xp[0:keep_rows, :])
            acc = _shift_back(x, wrapped, SSD_CONV - 1) * w[0:1, :]
            for j in range(1, SSD_CONV - 1):
                acc = acc + _shift_back(x, wrapped, SSD_CONV - 1 - j) * w[j:j + 1, :]
            acc = acc + x * w[SSD_CONV - 1:SSD_CONV, :]
            xp[0:keep_rows, :] = x[q - keep_rows:, :]
            return acc + b_ref[...]
    else:
        xpx[hdr:hdr + q, :] = xs_ref[...]
        xpb[hdr:hdr + q, :] = bc_ref[...]

        def conv(xp, x_ref, w_ref, b_ref):
            w = w_ref[...]
            acc = xp[lo:lo + q, :] * w[0:1, :]
            for j in range(1, SSD_CONV):
                acc = acc + xp[lo + j:lo + j + q, :] * w[j:j + 1, :]
            return acc + b_ref[...]

    cb = conv(xpb, bc_ref, cwb_ref, cbb_ref)
    bcv = _silu(cb)
    bm = bcv[:, :n2].astype(bf16)
    cm = bcv[:, n2:]
    lane_g0 = lax.broadcasted_iota(jnp.int32, (q, n2), 1) < SSD_STATE
    acs_t = acs.T
    acs_x = _dot01_rhs(acs, e)
    yield
    cbms = [lax.dot_general(jnp.where(lane_g0 if g == 0 else jnp.logical_not(lane_g0), cm, 0.0).astype(bf16),
                            bm, NT_DIMS, preferred_element_type=f32) for g in range(SSD_GROUPS)]
    h_prev = ht[...]
    y_off = _dot(cm.astype(bf16), h_prev.astype(bf16))
    cx = conv(xpx, xs_ref, cwx_ref, cbx_ref)
    if not perm:
        tail_x = xpx[lo + q:hdr + q, :]
        tail_b = xpb[lo + q:hdr + q, :]
        xpx[lo:hdr, :] = tail_x
        xpb[lo:hdr, :] = tail_b
    yield
    xs = _silu(cx)
    last_x = acs_x[q - 1:q, :]
    xdt = xs * dt_x
    xdt_b = xdt.astype(bf16)
    yield
    lane_lo = lax.broadcasted_iota(jnp.int32, (q, LANES), 1) < SSD_HEAD_DIM
    heads_per_group = SSD_HEADS // SSD_GROUPS
    decays = [jnp.exp(jnp.where(causal, acs[:, hh:hh + 1] - acs_t[hh:hh + 1, :], -jnp.inf))
              for hh in range(SSD_HEADS)]
    yield
    weights = [(cbms[hh // heads_per_group] * decays[hh]).astype(bf16) for hh in range(SSD_HEADS)]
    xdtw = (xdt * jnp.exp(last_x - acs_x)).astype(bf16)
    yield
    ys = [_dot(weights[hh], xdt_b[:, (hh // 2) * LANES:(hh // 2 + 1) * LANES]) for hh in range(SSD_HEADS)]
    upd = lax.dot_general(bm, xdtw, TN_DIMS, preferred_element_type=f32)
    yield
    for p in range(SSD_HEADS // 2):
        yb[:, p * LANES:(p + 1) * LANES] = jnp.where(lane_lo, ys[2 * p], ys[2 * p + 1])
    ht[...] = jnp.exp(last_x) * h_prev + jnp.where(block_diag, upd, 0.0)
    yield
    y = yb[...] + y_off * jnp.exp(acs_x) + dexp_ref[...] * xs
    y = y * _silu(z_ref[...])
    yield
    y = y * lax.rsqrt(jnp.mean(jnp.square(y), axis=-1, keepdims=True) + EPS) * nw_ref[...]
    y_ref[...] = y.astype(y_ref.dtype)


PT_ROWS = 5 * 512
PT_XS, PT_BC, PT_SM = D_MODEL, 2 * D_MODEL, 2 * D_MODEL + SSD_BC


def _proj_t_kernel(x_ref, w_ref, o_ref, xb):
    @pl.when(pl.program_id(0) == 0)
    def _():
        xb[...] = x_ref[...].astype(bf16)

    o_ref[...] = lax.dot_general(w_ref[...], xb[...], NT_DIMS, preferred_element_type=f32)


def _proj_t(x_tm, w32, layer):
    m, k = x_tm.shape
    tn = 512
    xs_blk = P32_XS * D_MODEL // tn
    return pl.pallas_call(
        _proj_t_kernel,
        out_shape=jax.ShapeDtypeStruct((PT_ROWS, m), f32),
        grid=(PT_ROWS // tn,),
        in_specs=[pl.BlockSpec((m, k), lambda j: (0, 0)),
                  pl.BlockSpec((None, tn, k), lambda j: (layer, jnp.where(j < D_MODEL // tn, j, j + xs_blk - D_MODEL // tn), 0))],
        out_specs=pl.BlockSpec((tn, m), lambda j: (j, 0)),
        scratch_shapes=[pltpu.VMEM((m, k), bf16)],
        compiler_params=pltpu.CompilerParams(dimension_semantics=("arbitrary",)),
        name="proj_t",
    )(x_tm, w32)


def _ssd_lanes_kernel(pt_ref, cst_ref, h0_ref, cw_ref, cb_ref, dtb_ref, alog_ref, dexp_ref, nw_ref,
                      y_ref, ncs_ref, hout_ref, xc, dts, decs, ysc, *, steps, batch):
    hd = pl.program_id(0)
    n_ch = D_MODEL + SSD_BC
    keep = SSD_CONV - 1
    lanes = lambda t: slice(t * batch, (t + 1) * batch)

    @pl.when(hd == 0)
    def _():
        for t in range(steps):
            acc = None
            for j in range(SSD_CONV):
                i = t + j
                src = cst_ref[:, lanes(i)] if i < keep else pt_ref[PT_XS:PT_XS + n_ch, lanes(i - keep)]
                term = src * cw_ref[j]
                acc = term if acc is None else acc + term
            xc[:, lanes(t)] = _silu(acc + cb_ref[...])
        ncs_ref[...] = pt_ref[PT_XS:PT_XS + n_ch, (steps - keep) * batch:steps * batch]
        dt = _softplus(pt_ref[PT_SM:PT_SM + SSD_HEADS, :] + jnp.concatenate([dtb_ref[...]] * steps, axis=1))
        dts[...] = dt
        decs[...] = jnp.exp(dt * jnp.concatenate([-jnp.exp(alog_ref[...])] * steps, axis=1))

    grp_row = (hd // (SSD_HEADS // SSD_GROUPS)) * SSD_STATE
    xh = xc[pl.ds(pl.multiple_of(hd * SSD_HEAD_DIM, SSD_HEAD_DIM), SSD_HEAD_DIM), :]
    bh = xc[pl.ds(pl.multiple_of(D_MODEL + grp_row, SSD_STATE), SSD_STATE), :]
    ch = xc[pl.ds(pl.multiple_of(D_MODEL + SSD_GROUPS * SSD_STATE + grp_row, SSD_STATE), SSD_STATE), :]
    dth = dts[pl.ds(hd, 1), :]
    dech = decs[pl.ds(hd, 1), :]
    d_skip = dexp_ref[pl.ds(hd, 1), :]
    y_rows = [[] for _ in range(steps)]
    for p in range(SSD_HEAD_DIM):
        h = h0_ref[p]
        for t in range(steps):
            x_row = xh[p:p + 1, lanes(t)]
            h = dech[:, lanes(t)] * h + (x_row * dth[:, lanes(t)]) * bh[:, lanes(t)]
            y_rows[t].append(jnp.sum(ch[:, lanes(t)] * h, axis=0, keepdims=True) + d_skip * x_row)
        hout_ref[p] = h
    rows = pl.ds(pl.multiple_of(hd * SSD_HEAD_DIM, SSD_HEAD_DIM), SSD_HEAD_DIM)
    for t in range(steps):
        ysc[rows, lanes(t)] = jnp.concatenate(y_rows[t], axis=0)

    @pl.when(hd == SSD_HEADS - 1)
    def _():
        y = ysc[...] * _silu(pt_ref[0:D_MODEL, :])
        y = y * lax.rsqrt(jnp.mean(jnp.square(y), axis=0, keepdims=True) + EPS)
        y_ref[...] = (y * jnp.concatenate([nw_ref[...]] * steps, axis=1)).astype(y_ref.dtype)


def _ssd_lanes(pt, cst, h0, w, layer, prev, *, steps, batch):
    n_ch = D_MODEL + SSD_BC
    tb = steps * batch
    keep = SSD_CONV - 1
    assert steps >= keep and batch % LANES == 0
    kern = functools.partial(_ssd_lanes_kernel, steps=steps, batch=batch)
    full = lambda *shape: pl.BlockSpec(shape, lambda hd: (0,) * len(shape))
    hblock = pl.BlockSpec((None, None, SSD_HEAD_DIM, SSD_STATE, batch), lambda hd: (layer, hd, 0, 0, 0))
    return _stacked_call(
        kern, name="ssd_lanes", grid=(SSD_HEADS,),
        inputs=[pt, cst, h0, w["cw_b"], w["cb_b"], w["dtb_b"], w["alog_b"], w["dexp_b"], w["nw_b"]],
        in_specs=[full(PT_ROWS, tb), _layer_spec(layer, n_ch, keep * batch), hblock,
                  _layer_spec(layer, SSD_CONV, n_ch, batch), _layer_spec(layer, n_ch, batch),
                  _layer_spec(layer, SSD_HEADS, batch), _layer_spec(layer, SSD_HEADS, batch),
                  _layer_spec(layer, SSD_HEADS, batch), _layer_spec(layer, D_MODEL, batch)],
        out_shape=(jax.ShapeDtypeStruct((D_MODEL, tb), bf16),
                   jax.ShapeDtypeStruct((DEPTH, n_ch, keep * batch), f32),
                   jax.ShapeDtypeStruct((DEPTH, SSD_HEADS, SSD_HEAD_DIM, SSD_STATE, batch), f32)),
        out_specs=(full(D_MODEL, tb), _layer_spec(layer, n_ch, keep * batch), hblock),
        stacked={1: prev and prev[0], 2: prev and prev[1]},
        scratch_shapes=[pltpu.VMEM((n_ch, tb), f32), pltpu.VMEM((SSD_HEADS, tb), f32),
                        pltpu.VMEM((SSD_HEADS, tb), f32), pltpu.VMEM((D_MODEL, tb), f32)],
        dimension_semantics=("arbitrary",))


def _mlstm_kernel(*refs, q, lr, nc, has_state, gs, cps, perm, tile3d):
    seq = functools.partial(_mlstm_seq, q=q, lr=lr, nc=nc, has_state=has_state, cps=cps, perm=perm)
    _per_sequence(seq, refs, 5, 3, 2, gs, has_state, nc, cps * q, tile3d)


def _mlstm_seq(*refs, q, lr, nc, has_state, cps, perm):
    q_ref, k_ref, v_ref, o_ref, sm_ref = refs[:5]
    refs = refs[5:]
    if has_state:
        c0_ref, n0_ref, m0_ref = refs[:3]
        refs = refs[3:]
    gb_ref, nw_ref, h_ref, cout_ref, nout_ref, mout_ref, cs, ns, ms = refs
    direct = has_state and nc == 1 and cps == 1

    def init():
        if direct:
            return
        if has_state:
            cs[...] = c0_ref[...]
            ns[...] = n0_ref[...]
            ms[...] = m0_ref[...]
        else:
            cs[...] = jnp.zeros_like(cs)
            ns[...] = jnp.zeros_like(ns)
            ms[...] = jnp.zeros_like(ms)

    def final():
        if direct:
            return
        cout_ref[...] = cs[...]
        nout_ref[...] = ns[...]
        mout_ref[...] = ms[...]

    def body():
        chunks = []
        for k in range(cps):
            sub = lambda r, k=k: r.rows(k * q, q)
            is_last = (pl.program_id(1) == nc - 1) if k == cps - 1 else False
            src = (c0_ref, n0_ref, m0_ref) if direct else (cs, ns, ms)
            dst = (cout_ref, nout_ref, mout_ref) if direct else (cs, ns, ms)
            chunks.append(_mlstm_body(sub(q_ref), sub(k_ref), sub(v_ref), sub(o_ref), sub(sm_ref), gb_ref, nw_ref,
                                      sub(h_ref), src, dst, q=q, lr=lr, is_last=is_last, perm=perm))
        yield from _staggered(chunks, MLSTM_STATE_STAGES)

    return init, body, final


MLSTM_STATE_STAGES = 8


def _staggered(gens, skew):
    done = [False] * len(gens)
    t = 0
    while not all(done):
        for i, g in enumerate(gens):
            if done[i] or t < i * skew:
                continue
            try:
                next(g)
            except StopIteration:
                done[i] = True
        t += 1
        yield


def _mlstm_body(q_ref, k_ref, v_ref, o_ref, sm_ref, gb_ref, nw_ref, h_ref, src, dst, *, q, lr, is_last, perm):
    c_src, n_src, m_src = src
    c_dst, n_dst, m_dst = dst
    sm = sm_ref[...] + gb_ref[...]
    logf = -_softplus(-sm)
    ipre = sm
    if lr < q:
        valid = _valid_rows(q, SMALL, lr, is_last)
        logf = jnp.where(valid, logf, 0.0)
        ipre = jnp.where(valid, ipre, NEG_BIG)
    causal = _tri(q, perm)
    tril = jnp.where(causal, 1.0, 0.0).astype(bf16)
    yield
    bcum = _dot01_lhs(tril, logf)
    ipre_t = ipre.T
    yield
    bcum_t = bcum.T
    lane = lax.broadcasted_iota(jnp.int32, (1, SMALL), 1)
    k_scale = MLSTM_HEAD_DIM ** -0.5

    heads = range(MLSTM_HEADS)
    sls = [slice(h * MLSTM_HEAD_DIM, (h + 1) * MLSTM_HEAD_DIM) for h in heads]
    q_all, k_all, v_all, o_all = q_ref[...], k_ref[...], v_ref[...], o_ref[...]
    qs = [q_all[:, sl] for sl in sls]
    ks = [k_all[:, sl] * k_scale for sl in sls]
    vs = [v_all[:, sl] for sl in sls]
    b_cols = [bcum[:, F_OFF + h:F_OFF + h + 1] for h in heads]
    i_cols = [ipre[:, I_OFF + h:I_OFF + h + 1] for h in heads]
    dmats = [jnp.where(causal, b_cols[h] - bcum_t[F_OFF + h:F_OFF + h + 1, :] + ipre_t[I_OFF + h:I_OFF + h + 1, :],
                       -jnp.inf) for h in heads]
    yield
    qk = [lax.dot_general(qs[h], ks[h], NT_DIMS, preferred_element_type=f32) for h in heads]
    d_max = [jnp.max(dmats[h], axis=-1, keepdims=True) for h in heads]
    yield
    n_all = n_src[...]
    m_all = m_src[...]
    m_new = m_all
    cs_in = [c_src[h] for h in heads]
    m_prevs = [m_all[:, h:h + 1] for h in heads]
    qc = [_dot(qs[h], cs_in[h].astype(bf16)) for h in heads]
    inters = [b_cols[h] + m_prevs[h] for h in heads]
    m_ts = [jnp.maximum(inters[h], d_max[h]) for h in heads]
    yield
    w_inters = [jnp.exp(inters[h] - m_ts[h]) for h in heads]
    ss = [qk[h] * jnp.exp(dmats[h] - m_ts[h]) for h in heads]
    yield
    sv = [_dot(ss[h].astype(bf16), vs[h]) for h in heads]
    m_ends = [m_ts[h][q - 1:q, :] for h in heads]
    b_lasts = [b_cols[h][q - 1:q, :] for h in heads]
    kws = [ks[h].astype(f32) * jnp.exp(b_lasts[h] - b_cols[h] + i_cols[h] - m_ends[h]) for h in heads]
    yield
    kv = [lax.dot_general(kws[h].astype(bf16), vs[h], TN_DIMS, preferred_element_type=f32) for h in heads]
    qns = [jnp.sum(qs[h].astype(f32) * n_all[h:h + 1, :], axis=-1, keepdims=True) for h in heads]
    yield
    dens = [jnp.sum(ss[h], axis=-1, keepdims=True) + w_inters[h] * qns[h] for h in heads]
    yield
    hvs = [(sv[h] + w_inters[h] * qc[h]) / jnp.maximum(jnp.abs(dens[h]), jnp.exp(-m_ts[h])) for h in heads]
    yield
    rms = [lax.rsqrt(jnp.mean(jnp.square(hvs[h]), axis=-1, keepdims=True) + EPS) for h in heads]
    yield
    h_new = [(hvs[h] * rms[h] * nw_ref[:, sls[h]] * jax.nn.sigmoid(o_all[:, sls[h]])).astype(h_ref.dtype)
             for h in heads]
    w_cs = [jnp.exp(b_lasts[h] + m_prevs[h] - m_ends[h]) for h in heads]
    yield
    for h in heads:
        c_dst[h] = w_cs[h] * cs_in[h] + kv[h]
        m_new = jnp.where(lane == h, m_ends[h], m_new)
    h_ref[...] = jnp.concatenate(h_new, axis=1)
    n_dst[...] = jnp.concatenate(
        [w_cs[h] * n_all[h:h + 1, :] + jnp.sum(kws[h], axis=0, keepdims=True) for h in heads], axis=0)
    m_dst[...] = m_new


def _merge_kernel(ys_ref, hm_ref, ga_ref, gb_ref, x_ref, wa_ref, wb_ref, wo_ref, g_ref, b_ref, o_ref, *, parts):
    tm = x_ref.shape[0]
    rows = [pl.ds(i * (tm // parts), tm // parts) for i in range(parts)]
    br = [(_dot(ys_ref[r, :], wa_ref[...]), _dot(hm_ref[r, :], wb_ref[...])) for r in rows]
    merged = [(jax.nn.sigmoid(ga_ref[r, :]) * a + jax.nn.sigmoid(gb_ref[r, :]) * b).astype(bf16)
              for r, (a, b) in zip(rows, br)]
    mix = [_dot(m, wo_ref[...]) for m in merged]
    for r, m in zip(rows, mix):
        o_ref[r, :] = _layer_norm(ALPHA * x_ref[r, :] + m, g_ref[...], b_ref[...])


def _merge(ys, hm, p32, x, w, layer, tm, parts=MERGE_PARTS):
    m = x.shape[0]
    row = lambda blk: pl.BlockSpec((tm, D_MODEL), lambda i: (i, blk))
    return pl.pallas_call(
        functools.partial(_merge_kernel, parts=parts),
        out_shape=jax.ShapeDtypeStruct((m, D_MODEL), f32),
        grid=(m // tm,),
        in_specs=[row(0), row(0), row(P32_GA), row(P32_GB), row(0),
                  _layer_spec(layer, D_MODEL, D_MODEL), _layer_spec(layer, D_MODEL, D_MODEL),
                  _layer_spec(layer, D_MODEL, D_MODEL), _layer_spec(layer, 1, D_MODEL),
                  _layer_spec(layer, 1, D_MODEL)],
        out_specs=row(0),
        compiler_params=pltpu.CompilerParams(dimension_semantics=("parallel",)),
        name="merge",
    )(ys, hm, p32, p32, x, w["wa"], w["wb"], w["wo"], w["ln1_g"], w["ln1_b"])


def _ffn_kernel(*refs, tm, seq_len, perm_q, ahead_n):
    multi = seq_len > 0
    keep_rows = (FFN_CONV - 1) * SUBLANES
    if multi:
        x_ref, st_ref, wup_ref, cw_ref, cb_ref, wdn_ref, g_ref, b_ref, o_ref, sout_ref, xp = refs
    else:
        x_ref, wup_ref, cw_ref, cb_ref, wdn_ref, g_ref, b_ref, o_ref, sout_ref, xp, carry = refs
        @pl.when(pl.program_id(1) == 0)
        def _():
            carry[...] = jnp.zeros_like(carry)

    hdr = SUBLANES
    x = x_ref[...]
    xb = x.astype(bf16)
    if multi:
        assert seq_len & (seq_len - 1) == 0
        nseq = tm // seq_len
        t = lax.broadcasted_iota(jnp.int32, (tm, FF_CH), 0) & (seq_len - 1)
        row = lax.broadcasted_iota(jnp.int32, (tm, 2 * nseq), 0)
        col = lax.broadcasted_iota(jnp.int32, (tm, 2 * nseq), 1)
        t_sel = row & (seq_len - 1)
        seq0 = lax.shift_right_logical(row - t_sel, (seq_len // 2).bit_length() - 1)
        sel_p2 = jnp.where(col == seq0 + t_sel, jnp.where(t_sel < 2, 1.0, 0.0), 0.0).astype(bf16)
        sel_p1 = jnp.where(col == seq0 + 1, jnp.where(t_sel == 0, 1.0, 0.0), 0.0).astype(bf16)
        xp[:, 0:hdr, :] = jnp.zeros((FF_XP_SLOTS, hdr, FF_CH), f32)

    def cols_of(c, part):
        return slice(part * D_FF + c * FF_CH, part * D_FF + (c + 1) * FF_CH)

    def up(c):
        return [_dot(xb, wup_ref[:, cols_of(c, part)]) for part in range(2)]

    def conv_act(c, us):
        halves = []
        for part, u in enumerate(us):
            cols = cols_of(c, part)
            slot = (2 * c + part) % FF_XP_SLOTS
            if perm_q:
                prev = carry[:, cols]
                p1, p2 = [], []
                for kk in range(tm // perm_q):
                    uc = u[kk * perm_q:(kk + 1) * perm_q, :]
                    wrapped = _wrap_rows(uc[perm_q - keep_rows:, :], prev)
                    p1.append(_shift_back(uc, wrapped, 1))
                    p2.append(_shift_back(uc, wrapped, 2))
                    prev = uc[perm_q - keep_rows:, :]
                carry[:, cols] = prev
                p1, p2 = jnp.concatenate(p1, axis=0), jnp.concatenate(p2, axis=0)
                w = cw_ref[:, cols]
                halves.append(p2 * w[0:1, :] + p1 * w[1:2, :] + u * w[2:3, :] + cb_ref[:, cols])
                continue
            xp[slot, hdr:hdr + tm, :] = u
            if multi:
                sout_ref[:, cols] = u
                st = st_ref[:, cols]
                p1 = jnp.where(t == 0, _dot01_lhs(sel_p1, st), xp[slot, hdr - 1:hdr - 1 + tm, :])
                p2 = jnp.where(t < 2, _dot01_lhs(sel_p2, st), xp[slot, hdr - 2:hdr - 2 + tm, :])
            else:
                xp[slot, 0:hdr, :] = carry[:, cols]
                p1 = xp[slot, hdr - 1:hdr - 1 + tm, :]
                p2 = xp[slot, hdr - 2:hdr - 2 + tm, :]
                carry[:, cols] = u[tm - hdr:tm, :]
            w = cw_ref[:, cols]
            halves.append(p2 * w[0:1, :] + p1 * w[1:2, :] + u * w[2:3, :] + cb_ref[:, cols])
        return (_silu(halves[0]) * halves[1]).astype(bf16)

    acc = None
    ahead = [up(c) for c in range(min(ahead_n, FF_NCH))]
    pending = None
    for c in range(FF_NCH):
        if c + ahead_n < FF_NCH:
            ahead.append(up(c + ahead_n))
        if pending is not None:
            d = _dot(pending, wdn_ref[(c - 1) * FF_CH:c * FF_CH, :])
            acc = d if acc is None else acc + d
        pending = conv_act(c, ahead.pop(0))
    acc = acc + _dot(pending, wdn_ref[(FF_NCH - 1) * FF_CH:FF_NCH * FF_CH, :])

    if not multi:
        sout_ref[...] = carry[...]
    o_ref[...] = _layer_norm(ALPHA * x + acc, g_ref[...], b_ref[...])


def _ffn(x, st, w, layer, *, groups, tm, seq_len, perm_q=0, ahead=FF_UP_AHEAD):
    m = x.shape[0]
    ahead = ahead[layer] if isinstance(ahead, tuple) else ahead
    tiles = m // (groups * tm)
    multi = seq_len > 0
    kern = functools.partial(_ffn_kernel, tm=tm, seq_len=seq_len, perm_q=perm_q, ahead_n=ahead)
    carry_rows = (FFN_CONV - 1) * SUBLANES if perm_q else SUBLANES
    once = dict(pipeline_mode=pl.Buffered(1))
    x_spec = pl.BlockSpec((tm, D_MODEL), lambda s, j: (s * tiles + j, 0))
    w_specs = [_layer_spec(layer, D_MODEL, 2 * D_FF, **once), _layer_spec(layer, FFN_CONV, 2 * D_FF, **once),
               _layer_spec(layer, 1, 2 * D_FF, **once), _layer_spec(layer, D_FF, D_MODEL, **once),
               _layer_spec(layer, 1, D_MODEL, **once), _layer_spec(layer, 1, D_MODEL, **once)]
    w_args = (w["wup"], w["fcw"], w["fcb"], w["wdn"], w["ln2_g"], w["ln2_b"])
    xp = pltpu.VMEM((FF_XP_SLOTS, SUBLANES + (0 if perm_q else tm), FF_CH), f32)
    if multi:
        nst = 2 * (tm // seq_len)
        inputs = (x, st) + w_args
        in_specs = [x_spec, pl.BlockSpec((None, nst, 2 * D_FF), lambda s, j: (layer, s * tiles + j, 0))] + w_specs
        sout_shape = jax.ShapeDtypeStruct((m, 2 * D_FF), f32)
        sout_spec = pl.BlockSpec((tm, 2 * D_FF), lambda s, j: (s * tiles + j, 0))
        scratch = [xp]
    else:
        inputs = (x,) + w_args
        in_specs = [x_spec] + w_specs
        sout_shape = jax.ShapeDtypeStruct((groups, carry_rows, 2 * D_FF), f32)
        sout_spec = pl.BlockSpec((None, carry_rows, 2 * D_FF), lambda s, j: (s, 0, 0))
        scratch = [xp, pltpu.VMEM((carry_rows, 2 * D_FF), f32)]
    return pl.pallas_call(
        kern,
        out_shape=(jax.ShapeDtypeStruct((m, D_MODEL), f32), sout_shape),
        grid=(groups, tiles),
        in_specs=in_specs,
        out_specs=(x_spec, sout_spec),
        scratch_shapes=scratch,
        compiler_params=pltpu.CompilerParams(dimension_semantics=("parallel", "arbitrary"),
                                             vmem_limit_bytes=56 * 1024 * 1024),
        name="ffn",
    )(*inputs)


def _pad_lanes(v, off, width=SMALL):
    out = jnp.zeros((v.shape[0], 1, width), f32)
    return out.at[:, 0, off:off + v.shape[1]].set(v.astype(f32))


def _prep_weights(w_in, ssd_conv_w, ssd_conv_b, ssd_dt_bias, ssd_a_log, ssd_d, ssd_norm_w, mlstm_gate_b,
                  mlstm_norm_w, w_branch_a, w_branch_b, w_out, ln1_g, ln1_b, ffn_w_up, ffn_conv_w,
                  ffn_conv_b, ffn_w_down, ln2_g, ln2_b):
    d = D_MODEL
    o_z, o_xbc, o_dt = 0, d, d + d + SSD_BC
    o_q = o_dt + SSD_HEADS
    o_if = o_q + 3 * d
    o_o = o_if + 2 * MLSTM_HEADS
    o_g = o_o + d
    w_t = jnp.swapaxes(w_in, 1, 2)
    cols = lambda a, n: w_t[:, a:a + n, :]
    zeros = lambda n: jnp.zeros((DEPTH, n, d), w_in.dtype)
    w32 = jnp.concatenate([cols(o_z, d), cols(o_o, d), cols(o_g, 2 * d), cols(o_xbc, d + SSD_BC),
                           cols(o_dt, SSD_HEADS), cols(o_if, 2 * MLSTM_HEADS),
                           zeros(P32_W - P32_SM_OFF - SSD_HEADS - 2 * MLSTM_HEADS)], axis=1).astype(bf16)
    e = (np.arange(SSD_HP)[None, :] // SSD_HEAD_DIM == np.arange(LANES)[:, None])
    bd = ((np.arange(SSD_GROUPS * SSD_STATE)[:, None] < SSD_STATE)
          == (np.arange(SSD_HP)[None, :] < SSD_HP // SSD_GROUPS))
    row = lambda a: a[:, None, :]
    return dict(
        w32=w32, wqkv=cols(o_q, 3 * d).astype(bf16),
        cwx=ssd_conv_w[:, :, :d], cbx=row(ssd_conv_b[:, :d]),
        cwb=ssd_conv_w[:, :, d:], cbb=row(ssd_conv_b[:, d:]),
        dtb=_pad_lanes(ssd_dt_bias, DT_OFF), alog=_pad_lanes(ssd_a_log, DT_OFF),
        dexp=row(jnp.repeat(ssd_d.astype(f32), SSD_HEAD_DIM, axis=1)), ssd_nw=row(ssd_norm_w),
        e=jnp.asarray(e, bf16), bd=jnp.asarray(bd, f32),
        gate_b=_pad_lanes(mlstm_gate_b, I_OFF), mlstm_nw=row(mlstm_norm_w),
        wa=w_branch_a.astype(bf16), wb=w_branch_b.astype(bf16), wo=w_out.astype(bf16),
        ln1_g=row(ln1_g), ln1_b=row(ln1_b),
        wup=ffn_w_up.astype(bf16), fcw=ffn_conv_w, fcb=row(ffn_conv_b), wdn=ffn_w_down.astype(bf16),
        ln2_g=row(ln2_g), ln2_b=row(ln2_b),
    )


class _Group:
    def __init__(self, batch, length, q, lr, gs, ssd_cps, mlstm_cps, proj_tm, p32_tn, qkv_tn, merge_tm, ffn,
                 perm=False, lanes_ssd=False, mlstm_gs=None, ssd_gs=None, merge_parts=MERGE_PARTS):
        self.mlstm_gs = gs if mlstm_gs is None else mlstm_gs
        self.ssd_gs = gs if ssd_gs is None else ssd_gs
        self.merge_parts = merge_parts
        self.perm = perm
        self.lanes_ssd = lanes_ssd
        self.batch, self.length, self.q, self.lr, self.gs = batch, length, q, lr, gs
        self.ssd_cps, self.mlstm_cps = ssd_cps, mlstm_cps
        self.rows = batch * length
        self.proj_tm, self.p32_tn, self.qkv_tn, self.merge_tm, self.ffn = proj_tm, p32_tn, qkv_tn, merge_tm, ffn

    def cfg(self, name, layer):
        v = getattr(self, name)
        return v[layer] if isinstance(v, tuple) else v

    def tiling(self, cps, gs=None):
        rows = cps * self.q
        gs = self.gs if gs is None else gs
        steps = self.length // rows
        if gs == 1 or steps == 1:
            spec = lambda width, blk: pl.BlockSpec((gs * rows, width), lambda b, c: (b * steps + c, blk))
            return steps, spec, (lambda a: a), False
        spec = lambda width, blk: pl.BlockSpec((gs, rows, width), lambda b, c: (b, c, blk))
        return steps, spec, (lambda a: a.reshape(self.batch, self.length, a.shape[-1])), True


def _ssd(grp, p32, state, w, layer, prev):
    q, b, gs = grp.q, grp.batch, grp.cfg("ssd_gs", layer)
    has_state = state is not None
    cps = grp.cfg("ssd_cps", layer)
    steps, tile, view, tile3d = grp.tiling(cps, gs)
    kern = functools.partial(_ssd_kernel, q=q, lr=grp.lr, nc=steps, has_state=has_state, gs=gs, cps=cps,
                             perm=grp.perm, tile3d=tile3d)
    p32 = view(p32)
    inputs = [p32, p32, p32, p32]
    in_specs = [tile(D_MODEL, P32_Z), tile(D_MODEL, P32_XS),
                tile(SSD_BC, P32_BC_OFF // SSD_BC), tile(SMALL, P32_SM_OFF // SMALL)]
    if has_state:
        inputs += [state["csx"], state["csb"], state["h"]]
        in_specs += [_seq_spec(layer, gs, SSD_CONV - 1, D_MODEL), _seq_spec(layer, gs, SSD_CONV - 1, SSD_BC),
                     _seq_spec(layer, gs, SSD_HP, SSD_STATE)]
    inputs += [w["cwx"], w["cbx"], w["cwb"], w["cbb"], w["dtb"], w["alog"], w["dexp"], w["ssd_nw"], w["e"],
               w["bd"]]
    const = lambda *shape: pl.BlockSpec(shape, lambda b, c: (0,) * len(shape))
    in_specs += [_layer_spec(layer, SSD_CONV, D_MODEL), _layer_spec(layer, 1, D_MODEL),
                 _layer_spec(layer, SSD_CONV, SSD_BC), _layer_spec(layer, 1, SSD_BC),
                 _layer_spec(layer, 1, SMALL), _layer_spec(layer, 1, SMALL), _layer_spec(layer, 1, D_MODEL),
                 _layer_spec(layer, 1, D_MODEL), const(LANES, SSD_HP), const(SSD_GROUPS * SSD_STATE, SSD_HP)]
    ys, *new = _stacked_call(
        kern, name="ssd", grid=(b // gs, steps), inputs=inputs, in_specs=in_specs,
        out_shape=(jax.ShapeDtypeStruct((b, grp.length, D_MODEL) if tile3d else (grp.rows, D_MODEL), bf16),
                   jax.ShapeDtypeStruct((DEPTH, b, SSD_CONV - 1, D_MODEL), f32),
                   jax.ShapeDtypeStruct((DEPTH, b, SSD_CONV - 1, SSD_BC), f32),
                   jax.ShapeDtypeStruct((DEPTH, b, SSD_HP, SSD_STATE), f32)),
        out_specs=(tile(D_MODEL, 0), _seq_spec(layer, gs, SSD_CONV - 1, D_MODEL),
                   _seq_spec(layer, gs, SSD_CONV - 1, SSD_BC), _seq_spec(layer, gs, SSD_HP, SSD_STATE)),
        stacked={1: prev and prev[0], 2: prev and prev[1], 3: prev and prev[2]},
        scratch_shapes=[pltpu.VMEM((gs, SUBLANES + q, D_MODEL), f32), pltpu.VMEM((gs, SUBLANES + q, SSD_BC), f32),
                        pltpu.VMEM((gs, SSD_GROUPS * SSD_STATE, SSD_HP), f32), pltpu.VMEM((gs, q, D_MODEL), f32)],
        dimension_semantics=("parallel", "arbitrary"))
    return (ys.reshape(grp.rows, D_MODEL), *new)


def _mlstm(grp, qkv, p32, state, w, layer, prev):
    q, b, gs = grp.q, grp.batch, grp.mlstm_gs
    has_state = state is not None
    cps = grp.cfg("mlstm_cps", layer)
    steps, tile, view, tile3d = grp.tiling(cps, gs)
    kern = functools.partial(_mlstm_kernel, q=q, lr=grp.lr, nc=steps, has_state=has_state, gs=gs, cps=cps,
                             perm=grp.perm, tile3d=tile3d)
    hd = MLSTM_HEAD_DIM
    carried = not (has_state and steps == 1 and cps == 1)
    qkv, p32 = view(qkv), view(p32)
    inputs = [qkv, qkv, qkv, p32, p32]
    in_specs = [tile(D_MODEL, 0), tile(D_MODEL, 1), tile(D_MODEL, 2), tile(D_MODEL, P32_O),
                tile(SMALL, P32_SM_OFF // SMALL)]
    if has_state:
        inputs += [state["c"], state["n"], state["m"]]
        in_specs += [_seq_spec(layer, gs, MLSTM_HEADS, hd, hd), _seq_spec(layer, gs, MLSTM_HEADS, hd),
                     _seq_spec(layer, gs, 1, SMALL)]
    inputs += [w["gate_b"], w["mlstm_nw"]]
    in_specs += [_layer_spec(layer, 1, SMALL), _layer_spec(layer, 1, D_MODEL)]
    hm, *new = _stacked_call(
        kern, name="mlstm", grid=(b // gs, steps), inputs=inputs, in_specs=in_specs,
        out_shape=(jax.ShapeDtypeStruct((b, grp.length, D_MODEL) if tile3d else (grp.rows, D_MODEL), bf16),
                   jax.ShapeDtypeStruct((DEPTH, b, MLSTM_HEADS, hd, hd), f32),
                   jax.ShapeDtypeStruct((DEPTH, b, MLSTM_HEADS, hd), f32),
                   jax.ShapeDtypeStruct((DEPTH, b, 1, SMALL), f32)),
        out_specs=(tile(D_MODEL, 0), _seq_spec(layer, gs, MLSTM_HEADS, hd, hd),
                   _seq_spec(layer, gs, MLSTM_HEADS, hd), _seq_spec(layer, gs, 1, SMALL)),
        stacked={1: prev and prev[0], 2: prev and prev[1], 3: prev and prev[2]},
        scratch_shapes=[pltpu.VMEM((gs, MLSTM_HEADS, hd, hd) if carried else (gs, 1, SUBLANES, LANES), f32),
                        pltpu.VMEM((gs, MLSTM_HEADS, hd), f32), pltpu.VMEM((gs, 1, SMALL), f32)],
        dimension_semantics=("parallel", "arbitrary"))
    return (hm.reshape(grp.rows, D_MODEL), *new)


def _trunk(grp, x, state, w):
    ssd_out = mlstm_out = None
    ffn_out = []
    for layer in range(DEPTH):
        p32 = _proj(x, w["w32"], layer, f32, grp.cfg("proj_tm", layer), grp.cfg("p32_tn", layer))
        qkv = _proj(x, w["wqkv"], layer, bf16, grp.cfg("proj_tm", layer), grp.cfg("qkv_tn", layer))
        if grp.lanes_ssd:
            b, t = grp.batch, grp.length
            x_tm = x.reshape(b, t, D_MODEL).swapaxes(0, 1).reshape(t * b, D_MODEL)
            ys_t, *ssd_out = _ssd_lanes(_proj_t(x_tm, w["w32"], layer), state["cs_t"], state["h_lanes"], w, layer,
                                        ssd_out, steps=t, batch=b)
            ys = ys_t.reshape(D_MODEL, t, b).transpose(2, 1, 0).reshape(b * t, D_MODEL)
        else:
            ys, *ssd_out = _ssd(grp, p32, state, w, layer, ssd_out)
        hm, *mlstm_out = _mlstm(grp, qkv, p32, state, w, layer, mlstm_out)
        x1 = _merge(ys, hm, p32, x, w, layer, grp.cfg("merge_tm", layer), grp.cfg("merge_parts", layer))
        x, s_ffn = _ffn(x1, state["ffn"] if state is not None else None, w, layer, **grp.ffn)
        ffn_out.append(s_ffn)
    return x, ssd_out, mlstm_out, ffn_out


def _unpack_states(batch, ssd_out, mlstm_out):
    csx, csb, h = ssd_out
    c, n, m = mlstm_out
    return (h.reshape(DEPTH, batch, SSD_HEADS, SSD_HEAD_DIM, SSD_STATE),
            jnp.concatenate([csx, csb], axis=-1), c, n, m[:, :, 0, :MLSTM_HEADS])


def kernel(x_prompt, x_sample, state_ssd, state_ssd_conv, state_mlstm_c, state_mlstm_n, state_mlstm_m,
           state_ffn_conv, w_in, ssd_conv_w, ssd_conv_b, ssd_dt_bias, ssd_a_log, ssd_d, ssd_norm_w,
           mlstm_gate_b, mlstm_norm_w, w_branch_a, w_branch_b, w_out, ln1_g, ln1_b, ffn_w_up, ffn_conv_w,
           ffn_conv_b, ffn_w_down, ln2_g, ln2_b):
    w = _prep_weights(w_in, ssd_conv_w, ssd_conv_b, ssd_dt_bias, ssd_a_log, ssd_d, ssd_norm_w, mlstm_gate_b,
                      mlstm_norm_w, w_branch_a, w_branch_b, w_out, ln1_g, ln1_b, ffn_w_up, ffn_conv_w,
                      ffn_conv_b, ffn_w_down, ln2_g, ln2_b)
    keep = FFN_CONV - 1
    keep_ssd = SSD_CONV - 1

    bp, lp, _ = x_prompt.shape
    prompt = _Group(bp, lp, CHUNK, CHUNK, gs=1, ssd_cps=4, mlstm_cps=4, proj_tm=(1024, 512), p32_tn=(2816, 5632),
                    qkv_tn=3072, merge_tm=1024, merge_parts=(2, 4),
                    ffn=dict(groups=bp, tm=1024, seq_len=0, perm_q=CHUNK, ahead=(2, 3)),
                    perm=True, mlstm_gs=2, ssd_gs=2)
    per = CHUNK // SUBLANES
    xp_rows = x_prompt.reshape(bp, lp // CHUNK, SUBLANES, per, D_MODEL).swapaxes(2, 3)
    y_p, ssd_p, mlstm_p, ffn_p = _trunk(prompt, xp_rows.reshape(bp * lp, D_MODEL), None, w)
    y_p = y_p.reshape(bp, lp // CHUNK, per, SUBLANES, D_MODEL).swapaxes(2, 3)
    st_p = _unpack_states(bp, ssd_p, mlstm_p)
    ffn_conv_p = jnp.stack(ffn_p)[:, :, SUBLANES - 1::SUBLANES, :]

    bs, ls, _ = x_sample.shape
    lpad = max(ls, SAMPLE_PAD_LEN)
    s_rows = bs * lpad
    sample = _Group(bs, lpad, lpad, ls, gs=8, ssd_cps=1, mlstm_cps=1, proj_tm=s_rows, p32_tn=512, qkv_tn=1024,
                    merge_tm=512,
                    ffn=dict(groups=1, tm=256, seq_len=lpad), lanes_ssd=True)
    assert lpad == ls
    lane_b = lambda a: jnp.broadcast_to(a.astype(f32)[..., None], a.shape + (bs,))
    w.update(cw_b=lane_b(ssd_conv_w), cb_b=lane_b(ssd_conv_b), dtb_b=lane_b(ssd_dt_bias), alog_b=lane_b(ssd_a_log),
             dexp_b=lane_b(ssd_d), nw_b=lane_b(ssd_norm_w))
    s_state = dict(
        cs_t=jnp.transpose(state_ssd_conv, (0, 3, 2, 1)).reshape(DEPTH, D_MODEL + SSD_BC, keep_ssd * bs),
        h_lanes=jnp.transpose(state_ssd, (0, 2, 3, 4, 1)),
        c=state_mlstm_c, n=state_mlstm_n,
        m=jnp.pad(state_mlstm_m, ((0, 0), (0, 0), (0, SMALL - MLSTM_HEADS)))[:, :, None, :],
        ffn=state_ffn_conv.reshape(DEPTH, bs * keep, 2 * D_FF),
    )
    xs = jnp.pad(x_sample, ((0, 0), (0, lpad - ls), (0, 0))).reshape(s_rows, D_MODEL)
    y_s, (cs_t, h_lanes), mlstm_s, ffn_s = _trunk(sample, xs, s_state, w)
    c_s, n_s, m_s = mlstm_s
    st_s = (jnp.transpose(h_lanes, (0, 4, 1, 2, 3)),
            jnp.transpose(cs_t.reshape(DEPTH, D_MODEL + SSD_BC, keep_ssd, bs), (0, 3, 2, 1)),
            c_s, n_s, m_s[:, :, 0, :MLSTM_HEADS])
    ffn_conv_s = jnp.stack([u.reshape(bs, lpad, 2 * D_FF)[:, ls - keep:ls, :] for u in ffn_s])
    y_sample = y_s.reshape(bs, lpad, D_MODEL)[:, :ls, :]

    return (y_p.reshape(bp, lp, D_MODEL), y_sample, st_p[0], st_s[0], st_p[1], st_s[1], st_p[2], st_s[2],
            st_p[3], st_s[3], st_p[4], st_s[4], ffn_conv_p, ffn_conv_s)
```

```python
import functools
import itertools

import jax
import jax.numpy as jnp
import numpy as np
from jax import lax
from jax.experimental import pallas as pl
from jax.experimental.pallas import tpu as pltpu

f32 = jnp.float32
bf16 = jnp.bfloat16

D_MODEL = 1024
DEPTH = 2
SSD_HEADS = 16
SSD_HEAD_DIM = 64
SSD_STATE = 64
SSD_GROUPS = 2
SSD_CONV = 4
SSD_BC = 2 * SSD_GROUPS * SSD_STATE
SSD_HP = SSD_HEADS * SSD_HEAD_DIM
MLSTM_HEADS = 4
MLSTM_HEAD_DIM = 256
CHUNK = 128
D_FF = 2816
FFN_CONV = 3
ALPHA = (2 * DEPTH) ** 0.25
EPS = 1e-5

LANES = 128
SUBLANES = 8
SMALL = LANES
DT_OFF, I_OFF, F_OFF = 0, 16, 20
P32_Z, P32_O, P32_GA, P32_GB, P32_XS = 0, 1, 2, 3, 4
P32_BC_OFF = 5 * D_MODEL
P32_SM_OFF = P32_BC_OFF + SSD_BC
P32_W = P32_SM_OFF + 2 * SMALL
FF_CH = 256
FF_NCH = D_FF // FF_CH
MERGE_PARTS = 2
FF_UP_AHEAD = 2
FF_XP_SLOTS = 4
NEG_BIG = -1e30
SAMPLE_PAD_LEN = 4

NT_DIMS = (((1,), (1,)), ((), ()))
TN_DIMS = (((0,), (0,)), ((), ()))


def _dot(a, b):
    return jnp.dot(a, b, preferred_element_type=f32)


def _split3(x):
    hi = x.astype(bf16)
    r = x - hi.astype(f32)
    mid = r.astype(bf16)
    lo = (r - mid.astype(f32)).astype(bf16)
    return hi, mid, lo


def _dot01_rhs(x, e):
    hi, mid, lo = _split3(x)
    return _dot(hi, e) + _dot(mid, e) + _dot(lo, e)


def _dot01_lhs(t, x):
    hi, mid, lo = _split3(x)
    return _dot(t, hi) + _dot(t, mid) + _dot(t, lo)


def _softplus(x):
    return jnp.maximum(x, 0.0) + jnp.log1p(jnp.exp(-jnp.abs(x)))


def _silu(x):
    return x * jax.nn.sigmoid(x)


def _row_time(i, q, perm):
    if not perm:
        return i
    return (i & (SUBLANES - 1)) * (q // SUBLANES) + lax.shift_right_logical(i, SUBLANES.bit_length() - 1)


def _tri(q, perm=False):
    row = lax.broadcasted_iota(jnp.int32, (q, q), 0)
    col = lax.broadcasted_iota(jnp.int32, (q, q), 1)
    return _row_time(row, q, perm) >= _row_time(col, q, perm)


def _wrap_rows(cur_tail, prev_tail):
    out = []
    for i in range(cur_tail.shape[0] // SUBLANES):
        rows = slice(i * SUBLANES, (i + 1) * SUBLANES)
        first = lax.broadcasted_iota(jnp.int32, (SUBLANES, cur_tail.shape[1]), 0) == 0
        out.append(jnp.where(first, pltpu.roll(prev_tail[rows], 1, axis=0), pltpu.roll(cur_tail[rows], 1, axis=0)))
    return jnp.concatenate(out, axis=0)


def _shift_back(x, wrapped, j):
    n = j * SUBLANES
    return jnp.concatenate([wrapped[wrapped.shape[0] - n:], x[:x.shape[0] - n]], axis=0)


def _valid_rows(q, width, lr, is_last):
    row = lax.broadcasted_iota(jnp.int32, (q, width), 0)
    return row < jnp.where(is_last, lr, q)


def _layer_norm(r, g, b):
    mu = jnp.mean(r, axis=-1, keepdims=True)
    var = jnp.mean(jnp.square(r - mu), axis=-1, keepdims=True)
    return (r - mu) * lax.rsqrt(var + EPS) * g + b


def _layer_spec(layer, *shape, **kw):
    zeros = (0,) * len(shape)
    return pl.BlockSpec((None,) + shape, lambda *_: (layer,) + zeros, **kw)


def _seq_spec(layer, gs, *shape):
    zeros = (0,) * len(shape)
    return pl.BlockSpec((None, gs) + shape, lambda b, c: (layer, b) + zeros)


def _stacked_call(kern, *, name, grid, inputs, in_specs, out_shape, out_specs, stacked, scratch_shapes,
                  dimension_semantics, vmem_limit_bytes=None):
    prev = [(i, a) for i, a in sorted(stacked.items()) if a is not None]
    n_in = len(inputs)

    def body(*refs):
        kern(*refs[:n_in], *refs[n_in + len(prev):])

    return pl.pallas_call(
        body,
        out_shape=out_shape,
        grid=grid,
        in_specs=list(in_specs) + [pl.BlockSpec(memory_space=pl.ANY)] * len(prev),
        out_specs=out_specs,
        scratch_shapes=scratch_shapes,
        input_output_aliases={n_in + k: i for k, (i, _) in enumerate(prev)},
        compiler_params=pltpu.CompilerParams(dimension_semantics=dimension_semantics,
                                             vmem_limit_bytes=vmem_limit_bytes),
        name=name,
    )(*inputs, *[a for _, a in prev])


def _proj_kernel(x_ref, w32_ref, wqkv_ref, o32_ref, oqkv_ref):
    xb = x_ref[...].astype(bf16)
    o32_ref[...] = lax.dot_general(xb, w32_ref[...], NT_DIMS, preferred_element_type=f32)
    oqkv_ref[...] = lax.dot_general(xb, wqkv_ref[...], NT_DIMS, preferred_element_type=f32).astype(oqkv_ref.dtype)


def _proj(x, w32, wqkv, layer, tm):
    m, k = x.shape
    n32, nq = w32.shape[1], wqkv.shape[1]
    once = dict(pipeline_mode=pl.Buffered(1))
    return pl.pallas_call(
        _proj_kernel,
        out_shape=(jax.ShapeDtypeStruct((m, n32), f32), jax.ShapeDtypeStruct((m, nq), bf16)),
        grid=(m // tm,),
        in_specs=[pl.BlockSpec((tm, k), lambda i: (i, 0)),
                  _layer_spec(layer, n32, k, **once), _layer_spec(layer, nq, k, **once)],
        out_specs=(pl.BlockSpec((tm, n32), lambda i: (i, 0)), pl.BlockSpec((tm, nq), lambda i: (i, 0))),
        compiler_params=pltpu.CompilerParams(dimension_semantics=("parallel",),
                                             vmem_limit_bytes=56 * 1024 * 1024),
        name="proj",
    )(x, w32, wqkv)


class _Rows:
    def __init__(self, ref, start, n):
        self.ref, self.start, self.n, self.dtype = ref, start, n, ref.dtype

    def rows(self, off, n):
        return _Rows(self.ref, self.start + off, n)

    def _index(self, idx):
        cols = slice(None) if idx is Ellipsis else idx[1]
        return (slice(self.start, self.start + self.n), cols)

    def __getitem__(self, idx):
        return self.ref[self._index(idx)]

    def __setitem__(self, idx, value):
        self.ref[self._index(idx)] = value


def _per_sequence(seq_fn, refs, n_tile, n_state, n_param, gs, has_state, nc, rows, tile3d=False):
    n_state = n_state if has_state else 0
    tiles, refs = refs[:n_tile], refs[n_tile:]
    state, refs = refs[:n_state], refs[n_state:]
    params, (y_tile, *rest) = refs[:n_param], refs[n_param:]
    phases = []
    for g in range(gs):
        at = lambda group: tuple(r.at[g] for r in group)
        if tile3d:
            seq_rows = lambda group: tuple(_Rows(r.at[g], 0, rows) for r in group)
        else:
            seq_rows = lambda group: tuple(_Rows(r, g * rows, rows) for r in group)
        phases.append(seq_fn(*seq_rows(tiles), *at(state), *params, *seq_rows((y_tile,)), *at(rest)))
    c = pl.program_id(1)

    @pl.when(c == 0)
    def _():
        for init, _, _ in phases:
            init()

    for _ in itertools.zip_longest(*[body() for _, body, _ in phases]):
        pass

    @pl.when(c == nc - 1)
    def _():
        for _, _, final in phases:
            final()


def _ssd_kernel(*refs, q, lr, nc, has_state, gs, cps, perm, tile3d):
    seq = functools.partial(_ssd_seq, q=q, lr=lr, nc=nc, has_state=has_state, cps=cps, perm=perm)
    _per_sequence(seq, refs, 4, 3, 10, gs, has_state, nc, cps * q, tile3d)


def _ssd_seq(*refs, q, lr, nc, has_state, cps, perm):
    z_ref, xs_ref, bc_ref, sm_ref = refs[:4]
    refs = refs[4:]
    if has_state:
        csx_ref, csb_ref, h0_ref = refs[:3]
        refs = refs[3:]
    (cwx_ref, cbx_ref, cwb_ref, cbb_ref, dtb_ref, alog_ref, dexp_ref, nw_ref, e_ref, bd_ref,
     y_ref, ncsx_ref, ncsb_ref, hout_ref, xpx, xpb, ht, yb) = refs
    hdr = SUBLANES
    lo = hdr - (SSD_CONV - 1)
    n2 = SSD_GROUPS * SSD_STATE
    assert lr >= SSD_CONV - 1
    keep = SSD_CONV - 1
    assert not (perm and (has_state or lr != q))
    carried = [(i + 1) * SUBLANES - 1 for i in range(keep)]

    def init():
        if has_state:
            h_t = h0_ref[...].T
            ht[...] = jnp.where(bd_ref[...] > 0.5, jnp.concatenate([h_t, h_t], axis=0), 0.0)
            xpx[lo:hdr, :] = csx_ref[...]
            xpb[lo:hdr, :] = csb_ref[...]
        else:
            ht[...] = jnp.zeros_like(ht)
            rows = slice(0, keep * SUBLANES) if perm else slice(lo, hdr)
            xpx[rows, :] = jnp.zeros((rows.stop - rows.start, SSD_HP), f32)
            xpb[rows, :] = jnp.zeros((rows.stop - rows.start, SSD_BC), f32)

    def final():
        if perm:
            for i, r in enumerate(carried):
                ncsx_ref[i:i + 1, :] = xpx[r:r + 1, :]
                ncsb_ref[i:i + 1, :] = xpb[r:r + 1, :]
        else:
            ncsx_ref[...] = xpx[lo + lr:hdr + lr, :]
            ncsb_ref[...] = xpb[lo + lr:hdr + lr, :]
        h_new = ht[...]
        hout_ref[...] = (h_new[:SSD_STATE, :] + h_new[SSD_STATE:, :]).T

    def body():
        for k in range(cps):
            sub = lambda r: r.rows(k * q, q)
            is_last = (pl.program_id(1) == nc - 1) if k == cps - 1 else False
            yield from _ssd_body(sub(z_ref), sub(xs_ref), sub(bc_ref), sub(sm_ref), cwx_ref, cbx_ref, cwb_ref,
                                 cbb_ref, dtb_ref, alog_ref, dexp_ref, nw_ref, e_ref, bd_ref, sub(y_ref),
                                 xpx, xpb, ht, yb, q=q, lr=lr, is_last=is_last, perm=perm)

    return init, body, final


def _ssd_body(z_ref, xs_ref, bc_ref, sm_ref, cwx_ref, cbx_ref, cwb_ref, cbb_ref, dtb_ref, alog_ref,
              dexp_ref, nw_ref, e_ref, bd_ref, y_ref, xpx, xpb, ht, yb, *, q, lr, is_last, perm):
    hdr = SUBLANES
    lo = hdr - (SSD_CONV - 1)
    n2 = SSD_GROUPS * SSD_STATE
    block_diag = bd_ref[...] > 0.5

    dt = _softplus(sm_ref[...] + dtb_ref[...])
    if lr < q:
        dt = jnp.where(_valid_rows(q, SMALL, lr, is_last), dt, 0.0)
    a = -jnp.exp(alog_ref[...])
    d_a = dt * a
    causal = _tri(q, perm)
    tril = jnp.where(causal, 1.0, 0.0).astype(bf16)
    e = e_ref[...]
    acs = _dot01_lhs(tril, d_a)
    dt_x = _dot01_rhs(dt, e)
    yield

    if perm:
        keep_rows = (SSD_CONV - 1) * SUBLANES

        def conv(xp, x_ref, w_ref, b_ref):
            w = w_ref[...]
            x = x_ref[...]
            wrapped = _wrap_rows(x[q - keep_rows:, :], xp[0:keep_rows, :])
            acc = _shift_back(x, wrapped, SSD_CONV - 1) * w[0:1, :]
            for j in range(1, SSD_CONV - 1):
                acc = acc + _shift_back(x, wrapped, SSD_CONV - 1 - j) * w[j:j + 1, :]
            acc = acc + x * w[SSD_CONV - 1:SSD_CONV, :]
            xp[0:keep_rows, :] = x[q - keep_rows:, :]
            return acc + b_ref[...]
    else:
        xpx[hdr:hdr + q, :] = xs_ref[...]
        xpb[hdr:hdr + q, :] = bc_ref[...]

        def conv(xp, x_ref, w_ref, b_ref):
            w = w_ref[...]
            acc = xp[lo:lo + q, :] * w[0:1, :]
            for j in range(1, SSD_CONV):
                acc = acc + xp[lo + j:lo + j + q, :] * w[j:j + 1, :]
            return acc + b_ref[...]

    cb = conv(xpb, bc_ref, cwb_ref, cbb_ref)
    bcv = _silu(cb)
    bm = bcv[:, :n2].astype(bf16)
    cm = bcv[:, n2:]
    lane_g0 = lax.broadcasted_iota(jnp.int32, (q, n2), 1) < SSD_STATE
    acs_t = acs.T
    acs_x = _dot01_rhs(acs, e)
    yield
    cbms = [lax.dot_general(jnp.where(lane_g0 if g == 0 else jnp.logical_not(lane_g0), cm, 0.0).astype(bf16),
                            bm, NT_DIMS, preferred_element_type=f32) for g in range(SSD_GROUPS)]
    h_prev = ht[...]
    y_off = _dot(cm.astype(bf16), h_prev.astype(bf16))
    cx = conv(xpx, xs_ref, cwx_ref, cbx_ref)
    if not perm:
        tail_x = xpx[lo + q:hdr + q, :]
        tail_b = xpb[lo + q:hdr + q, :]
        xpx[lo:hdr, :] = tail_x
        xpb[lo:hdr, :] = tail_b
    yield
    xs = _silu(cx)
    last_x = acs_x[q - 1:q, :]
    xdt = xs * dt_x
    xdt_b = xdt.astype(bf16)
    yield
    lane_lo = lax.broadcasted_iota(jnp.int32, (q, LANES), 1) < SSD_HEAD_DIM
    heads_per_group = SSD_HEADS // SSD_GROUPS
    decays = [jnp.exp(jnp.where(causal, acs[:, hh:hh + 1] - acs_t[hh:hh + 1, :], -jnp.inf))
              for hh in range(SSD_HEADS)]
    yield
    weights = [(cbms[hh // heads_per_group] * decays[hh]).astype(bf16) for hh in range(SSD_HEADS)]
    xdtw = (xdt * jnp.exp(last_x - acs_x)).astype(bf16)
    yield
    ys = [_dot(weights[hh], xdt_b[:, (hh // 2) * LANES:(hh // 2 + 1) * LANES]) for hh in range(SSD_HEADS)]
    upd = lax.dot_general(bm, xdtw, TN_DIMS, preferred_element_type=f32)
    yield
    for p in range(SSD_HEADS // 2):
        yb[:, p * LANES:(p + 1) * LANES] = jnp.where(lane_lo, ys[2 * p], ys[2 * p + 1])
    ht[...] = jnp.exp(last_x) * h_prev + jnp.where(block_diag, upd, 0.0)
    yield
    y = yb[...] + y_off * jnp.exp(acs_x) + dexp_ref[...] * xs
    y = y * _silu(z_ref[...])
    yield
    y = y * lax.rsqrt(jnp.mean(jnp.square(y), axis=-1, keepdims=True) + EPS) * nw_ref[...]
    y_ref[...] = y.astype(y_ref.dtype)


PT_ROWS = 5 * 512
PT_XS, PT_BC, PT_SM = D_MODEL, 2 * D_MODEL, 2 * D_MODEL + SSD_BC


def _proj_t_kernel(x_ref, w_ref, o_ref, xb):
    @pl.when(pl.program_id(0) == 0)
    def _():
        xb[...] = x_ref[...].astype(bf16)

    o_ref[...] = lax.dot_general(w_ref[...], xb[...], NT_DIMS, preferred_element_type=f32)


def _proj_t(x_tm, w32, layer):
    m, k = x_tm.shape
    tn = 512
    xs_blk = P32_XS * D_MODEL // tn
    return pl.pallas_call(
        _proj_t_kernel,
        out_shape=jax.ShapeDtypeStruct((PT_ROWS, m), f32),
        grid=(PT_ROWS // tn,),
        in_specs=[pl.BlockSpec((m, k), lambda j: (0, 0)),
                  pl.BlockSpec((None, tn, k), lambda j: (layer, jnp.where(j < D_MODEL // tn, j, j + xs_blk - D_MODEL // tn), 0))],
        out_specs=pl.BlockSpec((tn, m), lambda j: (j, 0)),
        scratch_shapes=[pltpu.VMEM((m, k), bf16)],
        compiler_params=pltpu.CompilerParams(dimension_semantics=("arbitrary",)),
        name="proj_t",
    )(x_tm, w32)


def _ssd_lanes_kernel(pt_ref, cst_ref, h0_ref, cw_ref, cb_ref, dtb_ref, alog_ref, dexp_ref, nw_ref,
                      y_ref, ncs_ref, hout_ref, xc, dts, decs, ysc, *, steps, batch):
    hd = pl.program_id(0)
    n_ch = D_MODEL + SSD_BC
    keep = SSD_CONV - 1
    lanes = lambda t: slice(t * batch, (t + 1) * batch)

    @pl.when(hd == 0)
    def _():
        for t in range(steps):
            acc = None
            for j in range(SSD_CONV):
                i = t + j
                src = cst_ref[:, lanes(i)] if i < keep else pt_ref[PT_XS:PT_XS + n_ch, lanes(i - keep)]
                term = src * cw_ref[j]
                acc = term if acc is None else acc + term
            xc[:, lanes(t)] = _silu(acc + cb_ref[...])
        ncs_ref[...] = pt_ref[PT_XS:PT_XS + n_ch, (steps - keep) * batch:steps * batch]
        dt = _softplus(pt_ref[PT_SM:PT_SM + SSD_HEADS, :] + jnp.concatenate([dtb_ref[...]] * steps, axis=1))
        dts[...] = dt
        decs[...] = jnp.exp(dt * jnp.concatenate([-jnp.exp(alog_ref[...])] * steps, axis=1))

    grp_row = (hd // (SSD_HEADS // SSD_GROUPS)) * SSD_STATE
    xh = xc[pl.ds(pl.multiple_of(hd * SSD_HEAD_DIM, SSD_HEAD_DIM), SSD_HEAD_DIM), :]
    bh = xc[pl.ds(pl.multiple_of(D_MODEL + grp_row, SSD_STATE), SSD_STATE), :]
    ch = xc[pl.ds(pl.multiple_of(D_MODEL + SSD_GROUPS * SSD_STATE + grp_row, SSD_STATE), SSD_STATE), :]
    dth = dts[pl.ds(hd, 1), :]
    dech = decs[pl.ds(hd, 1), :]
    d_skip = dexp_ref[pl.ds(hd, 1), :]
    y_rows = [[] for _ in range(steps)]
    for p in range(SSD_HEAD_DIM):
        h = h0_ref[p]
        for t in range(steps):
            x_row = xh[p:p + 1, lanes(t)]
            h = dech[:, lanes(t)] * h + (x_row * dth[:, lanes(t)]) * bh[:, lanes(t)]
            y_rows[t].append(jnp.sum(ch[:, lanes(t)] * h, axis=0, keepdims=True) + d_skip * x_row)
        hout_ref[p] = h
    rows = pl.ds(pl.multiple_of(hd * SSD_HEAD_DIM, SSD_HEAD_DIM), SSD_HEAD_DIM)
    for t in range(steps):
        ysc[rows, lanes(t)] = jnp.concatenate(y_rows[t], axis=0)

    @pl.when(hd == SSD_HEADS - 1)
    def _():
        y = ysc[...] * _silu(pt_ref[0:D_MODEL, :])
        y = y * lax.rsqrt(jnp.mean(jnp.square(y), axis=0, keepdims=True) + EPS)
        y_ref[...] = (y * jnp.concatenate([nw_ref[...]] * steps, axis=1)).astype(y_ref.dtype)


def _ssd_lanes(pt, cst, h0, w, layer, prev, *, steps, batch):
    n_ch = D_MODEL + SSD_BC
    tb = steps * batch
    keep = SSD_CONV - 1
    assert steps >= keep and batch % LANES == 0
    kern = functools.partial(_ssd_lanes_kernel, steps=steps, batch=batch)
    full = lambda *shape: pl.BlockSpec(shape, lambda hd: (0,) * len(shape))
    hblock = pl.BlockSpec((None, None, SSD_HEAD_DIM, SSD_STATE, batch), lambda hd: (layer, hd, 0, 0, 0))
    return _stacked_call(
        kern, name="ssd_lanes", grid=(SSD_HEADS,),
        inputs=[pt, cst, h0, w["cw_b"], w["cb_b"], w["dtb_b"], w["alog_b"], w["dexp_b"], w["nw_b"]],
        in_specs=[full(PT_ROWS, tb), _layer_spec(layer, n_ch, keep * batch), hblock,
                  _layer_spec(layer, SSD_CONV, n_ch, batch), _layer_spec(layer, n_ch, batch),
                  _layer_spec(layer, SSD_HEADS, batch), _layer_spec(layer, SSD_HEADS, batch),
                  _layer_spec(layer, SSD_HEADS, batch), _layer_spec(layer, D_MODEL, batch)],
        out_shape=(jax.ShapeDtypeStruct((D_MODEL, tb), bf16),
                   jax.ShapeDtypeStruct((DEPTH, n_ch, keep * batch), f32),
                   jax.ShapeDtypeStruct((DEPTH, SSD_HEADS, SSD_HEAD_DIM, SSD_STATE, batch), f32)),
        out_specs=(full(D_MODEL, tb), _layer_spec(layer, n_ch, keep * batch), hblock),
        stacked={1: prev and prev[0], 2: prev and prev[1]},
        scratch_shapes=[pltpu.VMEM((n_ch, tb), f32), pltpu.VMEM((SSD_HEADS, tb), f32),
                        pltpu.VMEM((SSD_HEADS, tb), f32), pltpu.VMEM((D_MODEL, tb), f32)],
        dimension_semantics=("arbitrary",))


def _mlstm_kernel(*refs, q, lr, nc, has_state, gs, cps, perm, tile3d):
    seq = functools.partial(_mlstm_seq, q=q, lr=lr, nc=nc, has_state=has_state, cps=cps, perm=perm)
    _per_sequence(seq, refs, 5, 3, 2, gs, has_state, nc, cps * q, tile3d)


def _mlstm_seq(*refs, q, lr, nc, has_state, cps, perm):
    q_ref, k_ref, v_ref, o_ref, sm_ref = refs[:5]
    refs = refs[5:]
    if has_state:
        c0_ref, n0_ref, m0_ref = refs[:3]
        refs = refs[3:]
    gb_ref, nw_ref, h_ref, cout_ref, nout_ref, mout_ref, cs, ns, ms = refs
    direct = has_state and nc == 1 and cps == 1

    def init():
        if direct:
            return
        if has_state:
            cs[...] = c0_ref[...]
            ns[...] = n0_ref[...]
            ms[...] = m0_ref[...]
        else:
            cs[...] = jnp.zeros_like(cs)
            ns[...] = jnp.zeros_like(ns)
            ms[...] = jnp.zeros_like(ms)

    def final():
        if direct:
            return
        cout_ref[...] = cs[...]
        nout_ref[...] = ns[...]
        mout_ref[...] = ms[...]

    def body():
        chunks = []
        for k in range(cps):
            sub = lambda r, k=k: r.rows(k * q, q)
            is_last = (pl.program_id(1) == nc - 1) if k == cps - 1 else False
            src = (c0_ref, n0_ref, m0_ref) if direct else (cs, ns, ms)
            dst = (cout_ref, nout_ref, mout_ref) if direct else (cs, ns, ms)
            chunks.append(_mlstm_body(sub(q_ref), sub(k_ref), sub(v_ref), sub(o_ref), sub(sm_ref), gb_ref, nw_ref,
                                      sub(h_ref), src, dst, q=q, lr=lr, is_last=is_last, perm=perm))
        yield from _staggered(chunks, MLSTM_STATE_STAGES)

    return init, body, final


MLSTM_STATE_STAGES = 8


def _staggered(gens, skew):
    done = [False] * len(gens)
    t = 0
    while not all(done):
        for i, g in enumerate(gens):
            if done[i] or t < i * skew:
                continue
            try:
                next(g)
            except StopIteration:
                done[i] = True
        t += 1
        yield


def _mlstm_body(q_ref, k_ref, v_ref, o_ref, sm_ref, gb_ref, nw_ref, h_ref, src, dst, *, q, lr, is_last, perm):
    c_src, n_src, m_src = src
    c_dst, n_dst, m_dst = dst
    sm = sm_ref[...] + gb_ref[...]
    logf = -_softplus(-sm)
    ipre = sm
    if lr < q:
        valid = _valid_rows(q, SMALL, lr, is_last)
        logf = jnp.where(valid, logf, 0.0)
        ipre = jnp.where(valid, ipre, NEG_BIG)
    causal = _tri(q, perm)
    tril = jnp.where(causal, 1.0, 0.0).astype(bf16)
    yield
    bcum = _dot01_lhs(tril, logf)
    ipre_t = ipre.T
    yield
    bcum_t = bcum.T
    lane = lax.broadcasted_iota(jnp.int32, (1, SMALL), 1)
    k_scale = MLSTM_HEAD_DIM ** -0.5

    heads = range(MLSTM_HEADS)
    sls = [slice(h * MLSTM_HEAD_DIM, (h + 1) * MLSTM_HEAD_DIM) for h in heads]
    q_all, k_all, v_all, o_all = q_ref[...], k_ref[...], v_ref[...], o_ref[...]
    qs = [q_all[:, sl] for sl in sls]
    ks = [k_all[:, sl] * k_scale for sl in sls]
    vs = [v_all[:, sl] for sl in sls]
    b_cols = [bcum[:, F_OFF + h:F_OFF + h + 1] for h in heads]
    i_cols = [ipre[:, I_OFF + h:I_OFF + h + 1] for h in heads]
    dmats = [jnp.where(causal, b_cols[h] - bcum_t[F_OFF + h:F_OFF + h + 1, :] + ipre_t[I_OFF + h:I_OFF + h + 1, :],
                       -jnp.inf) for h in heads]
    yield
    qk = [lax.dot_general(qs[h], ks[h], NT_DIMS, preferred_element_type=f32) for h in heads]
    d_max = [jnp.max(dmats[h], axis=-1, keepdims=True) for h in heads]
    yield
    n_all = n_src[...]
    m_all = m_src[...]
    m_new = m_all
    cs_in = [c_src[h] for h in heads]
    m_prevs = [m_all[:, h:h + 1] for h in heads]
    qc = [_dot(qs[h], cs_in[h].astype(bf16)) for h in heads]
    inters = [b_cols[h] + m_prevs[h] for h in heads]
    m_ts = [jnp.maximum(inters[h], d_max[h]) for h in heads]
    yield
    w_inters = [jnp.exp(inters[h] - m_ts[h]) for h in heads]
    ss = [qk[h] * jnp.exp(dmats[h] - m_ts[h]) for h in heads]
    yield
    sv = [_dot(ss[h].astype(bf16), vs[h]) for h in heads]
    m_ends = [m_ts[h][q - 1:q, :] for h in heads]
    b_lasts = [b_cols[h][q - 1:q, :] for h in heads]
    kws = [ks[h].astype(f32) * jnp.exp(b_lasts[h] - b_cols[h] + i_cols[h] - m_ends[h]) for h in heads]
    yield
    kv = [lax.dot_general(kws[h].astype(bf16), vs[h], TN_DIMS, preferred_element_type=f32) for h in heads]
    qns = [jnp.sum(qs[h].astype(f32) * n_all[h:h + 1, :], axis=-1, keepdims=True) for h in heads]
    yield
    dens = [jnp.sum(ss[h], axis=-1, keepdims=True) + w_inters[h] * qns[h] for h in heads]
    yield
    hvs = [(sv[h] + w_inters[h] * qc[h]) / jnp.maximum(jnp.abs(dens[h]), jnp.exp(-m_ts[h])) for h in heads]
    yield
    rms = [lax.rsqrt(jnp.mean(jnp.square(hvs[h]), axis=-1, keepdims=True) + EPS) for h in heads]
    yield
    h_new = [(hvs[h] * rms[h] * nw_ref[:, sls[h]] * jax.nn.sigmoid(o_all[:, sls[h]])).astype(h_ref.dtype)
             for h in heads]
    w_cs = [jnp.exp(b_lasts[h] + m_prevs[h] - m_ends[h]) for h in heads]
    yield
    for h in heads:
        c_dst[h] = w_cs[h] * cs_in[h] + kv[h]
        m_new = jnp.where(lane == h, m_ends[h], m_new)
    h_ref[...] = jnp.concatenate(h_new, axis=1)
    n_dst[...] = jnp.concatenate(
        [w_cs[h] * n_all[h:h + 1, :] + jnp.sum(kws[h], axis=0, keepdims=True) for h in heads], axis=0)
    m_dst[...] = m_new


def _merge_kernel(ys_ref, hm_ref, ga_ref, gb_ref, x_ref, wa_ref, wb_ref, wo_ref, g_ref, b_ref, o_ref, *, parts):
    tm = x_ref.shape[0]
    rows = [pl.ds(i * (tm // parts), tm // parts) for i in range(parts)]
    br = [(_dot(ys_ref[r, :], wa_ref[...]), _dot(hm_ref[r, :], wb_ref[...])) for r in rows]
    merged = [(jax.nn.sigmoid(ga_ref[r, :]) * a + jax.nn.sigmoid(gb_ref[r, :]) * b).astype(bf16)
              for r, (a, b) in zip(rows, br)]
    mix = [_dot(m, wo_ref[...]) for m in merged]
    for r, m in zip(rows, mix):
        o_ref[r, :] = _layer_norm(ALPHA * x_ref[r, :] + m, g_ref[...], b_ref[...])


def _merge(ys, hm, p32, x, w, layer, tm, parts=MERGE_PARTS):
    m = x.shape[0]
    row = lambda blk: pl.BlockSpec((tm, D_MODEL), lambda i: (i, blk))
    return pl.pallas_call(
        functools.partial(_merge_kernel, parts=parts),
        out_shape=jax.ShapeDtypeStruct((m, D_MODEL), f32),
        grid=(m // tm,),
        in_specs=[row(0), row(0), row(P32_GA), row(P32_GB), row(0),
                  _layer_spec(layer, D_MODEL, D_MODEL), _layer_spec(layer, D_MODEL, D_MODEL),
                  _layer_spec(layer, D_MODEL, D_MODEL), _layer_spec(layer, 1, D_MODEL),
                  _layer_spec(layer, 1, D_MODEL)],
        out_specs=row(0),
        compiler_params=pltpu.CompilerParams(dimension_semantics=("parallel",)),
        name="merge",
    )(ys, hm, p32, p32, x, w["wa"], w["wb"], w["wo"], w["ln1_g"], w["ln1_b"])


def _ffn_kernel(*refs, tm, seq_len, perm_q, ahead_n):
    multi = seq_len > 0
    keep_rows = (FFN_CONV - 1) * SUBLANES
    if multi:
        x_ref, st_ref, wup_ref, cw_ref, cb_ref, wdn_ref, g_ref, b_ref, o_ref, sout_ref, xp = refs
    else:
        x_ref, wup_ref, cw_ref, cb_ref, wdn_ref, g_ref, b_ref, o_ref, sout_ref, xp, carry = refs
        @pl.when(pl.program_id(1) == 0)
        def _():
            carry[...] = jnp.zeros_like(carry)

    hdr = SUBLANES
    x = x_ref[...]
    xb = x.astype(bf16)
    if multi:
        assert seq_len & (seq_len - 1) == 0
        nseq = tm // seq_len
        t = lax.broadcasted_iota(jnp.int32, (tm, FF_CH), 0) & (seq_len - 1)
        row = lax.broadcasted_iota(jnp.int32, (tm, 2 * nseq), 0)
        col = lax.broadcasted_iota(jnp.int32, (tm, 2 * nseq), 1)
        t_sel = row & (seq_len - 1)
        seq0 = lax.shift_right_logical(row - t_sel, (seq_len // 2).bit_length() - 1)
        sel_p2 = jnp.where(col == seq0 + t_sel, jnp.where(t_sel < 2, 1.0, 0.0), 0.0).astype(bf16)
        sel_p1 = jnp.where(col == seq0 + 1, jnp.where(t_sel == 0, 1.0, 0.0), 0.0).astype(bf16)
        xp[:, 0:hdr, :] = jnp.zeros((FF_XP_SLOTS, hdr, FF_CH), f32)

    def cols_of(c, part):
        return slice(part * D_FF + c * FF_CH, part * D_FF + (c + 1) * FF_CH)

    def up(c):
        return [_dot(xb, wup_ref[:, cols_of(c, part)]) for part in range(2)]

    def conv_act(c, us):
        halves = []
        for part, u in enumerate(us):
            cols = cols_of(c, part)
            slot = (2 * c + part) % FF_XP_SLOTS
            if perm_q:
                prev = carry[:, cols]
                p1, p2 = [], []
                for kk in range(tm // perm_q):
                    uc = u[kk * perm_q:(kk + 1) * perm_q, :]
                    wrapped = _wrap_rows(uc[perm_q - keep_rows:, :], prev)
                    p1.append(_shift_back(uc, wrapped, 1))
                    p2.append(_shift_back(uc, wrapped, 2))
                    prev = uc[perm_q - keep_rows:, :]
                carry[:, cols] = prev
                p1, p2 = jnp.concatenate(p1, axis=0), jnp.concatenate(p2, axis=0)
                w = cw_ref[:, cols]
                halves.append(p2 * w[0:1, :] + p1 * w[1:2, :] + u * w[2:3, :] + cb_ref[:, cols])
                continue
            xp[slot, hdr:hdr + tm, :] = u
            if multi:
                sout_ref[:, cols] = u
                st = st_ref[:, cols]
                p1 = jnp.where(t == 0, _dot01_lhs(sel_p1, st), xp[slot, hdr - 1:hdr - 1 + tm, :])
                p2 = jnp.where(t < 2, _dot01_lhs(sel_p2, st), xp[slot, hdr - 2:hdr - 2 + tm, :])
            else:
                xp[slot, 0:hdr, :] = carry[:, cols]
                p1 = xp[slot, hdr - 1:hdr - 1 + tm, :]
                p2 = xp[slot, hdr - 2:hdr - 2 + tm, :]
                carry[:, cols] = u[tm - hdr:tm, :]
            w = cw_ref[:, cols]
            halves.append(p2 * w[0:1, :] + p1 * w[1:2, :] + u * w[2:3, :] + cb_ref[:, cols])
        return (_silu(halves[0]) * halves[1]).astype(bf16)

    acc = None
    ahead = [up(c) for c in range(min(ahead_n, FF_NCH))]
    pending = None
    for c in range(FF_NCH):
        if c + ahead_n < FF_NCH:
            ahead.append(up(c + ahead_n))
        if pending is not None:
            d = _dot(pending, wdn_ref[(c - 1) * FF_CH:c * FF_CH, :])
            acc = d if acc is None else acc + d
        pending = conv_act(c, ahead.pop(0))
    acc = acc + _dot(pending, wdn_ref[(FF_NCH - 1) * FF_CH:FF_NCH * FF_CH, :])

    if not multi:
        sout_ref[...] = carry[...]
    o_ref[...] = _layer_norm(ALPHA * x + acc, g_ref[...], b_ref[...])


def _ffn(x, st, w, layer, *, groups, tm, seq_len, perm_q=0, ahead=FF_UP_AHEAD):
    m = x.shape[0]
    ahead = ahead[layer] if isinstance(ahead, tuple) else ahead
    tiles = m // (groups * tm)
    multi = seq_len > 0
    kern = functools.partial(_ffn_kernel, tm=tm, seq_len=seq_len, perm_q=perm_q, ahead_n=ahead)
    carry_rows = (FFN_CONV - 1) * SUBLANES if perm_q else SUBLANES
    once = dict(pipeline_mode=pl.Buffered(1))
    x_spec = pl.BlockSpec((tm, D_MODEL), lambda s, j: (s * tiles + j, 0))
    w_specs = [_layer_spec(layer, D_MODEL, 2 * D_FF, **once), _layer_spec(layer, FFN_CONV, 2 * D_FF, **once),
               _layer_spec(layer, 1, 2 * D_FF, **once), _layer_spec(layer, D_FF, D_MODEL, **once),
               _layer_spec(layer, 1, D_MODEL, **once), _layer_spec(layer, 1, D_MODEL, **once)]
    w_args = (w["wup"], w["fcw"], w["fcb"], w["wdn"], w["ln2_g"], w["ln2_b"])
    xp = pltpu.VMEM((FF_XP_SLOTS, SUBLANES + (0 if perm_q else tm), FF_CH), f32)
    if multi:
        nst = 2 * (tm // seq_len)
        inputs = (x, st) + w_args
        in_specs = [x_spec, pl.BlockSpec((None, nst, 2 * D_FF), lambda s, j: (layer, s * tiles + j, 0))] + w_specs
        sout_shape = jax.ShapeDtypeStruct((m, 2 * D_FF), f32)
        sout_spec = pl.BlockSpec((tm, 2 * D_FF), lambda s, j: (s * tiles + j, 0))
        scratch = [xp]
    else:
        inputs = (x,) + w_args
        in_specs = [x_spec] + w_specs
        sout_shape = jax.ShapeDtypeStruct((groups, carry_rows, 2 * D_FF), f32)
        sout_spec = pl.BlockSpec((None, carry_rows, 2 * D_FF), lambda s, j: (s, 0, 0))
        scratch = [xp, pltpu.VMEM((carry_rows, 2 * D_FF), f32)]
    return pl.pallas_call(
        kern,
        out_shape=(jax.ShapeDtypeStruct((m, D_MODEL), f32), sout_shape),
        grid=(groups, tiles),
        in_specs=in_specs,
        out_specs=(x_spec, sout_spec),
        scratch_shapes=scratch,
        compiler_params=pltpu.CompilerParams(dimension_semantics=("parallel", "arbitrary"),
                                             vmem_limit_bytes=56 * 1024 * 1024),
        name="ffn",
    )(*inputs)


def _pad_lanes(v, off, width=SMALL):
    out = jnp.zeros((v.shape[0], 1, width), f32)
    return out.at[:, 0, off:off + v.shape[1]].set(v.astype(f32))


def _prep_weights(w_in, ssd_conv_w, ssd_conv_b, ssd_dt_bias, ssd_a_log, ssd_d, ssd_norm_w, mlstm_gate_b,
                  mlstm_norm_w, w_branch_a, w_branch_b, w_out, ln1_g, ln1_b, ffn_w_up, ffn_conv_w,
                  ffn_conv_b, ffn_w_down, ln2_g, ln2_b):
    d = D_MODEL
    o_z, o_xbc, o_dt = 0, d, d + d + SSD_BC
    o_q = o_dt + SSD_HEADS
    o_if = o_q + 3 * d
    o_o = o_if + 2 * MLSTM_HEADS
    o_g = o_o + d
    w_t = jnp.swapaxes(w_in, 1, 2)
    cols = lambda a, n: w_t[:, a:a + n, :]
    zeros = lambda n: jnp.zeros((DEPTH, n, d), w_in.dtype)
    w32 = jnp.concatenate([cols(o_z, d), cols(o_o, d), cols(o_g, 2 * d), cols(o_xbc, d + SSD_BC),
                           cols(o_dt, SSD_HEADS), cols(o_if, 2 * MLSTM_HEADS),
                           zeros(P32_W - P32_SM_OFF - SSD_HEADS - 2 * MLSTM_HEADS)], axis=1).astype(bf16)
    e = (np.arange(SSD_HP)[None, :] // SSD_HEAD_DIM == np.arange(LANES)[:, None])
    bd = ((np.arange(SSD_GROUPS * SSD_STATE)[:, None] < SSD_STATE)
          == (np.arange(SSD_HP)[None, :] < SSD_HP // SSD_GROUPS))
    row = lambda a: a[:, None, :]
    return dict(
        w32=w32, wqkv=cols(o_q, 3 * d).astype(bf16),
        cwx=ssd_conv_w[:, :, :d], cbx=row(ssd_conv_b[:, :d]),
        cwb=ssd_conv_w[:, :, d:], cbb=row(ssd_conv_b[:, d:]),
        dtb=_pad_lanes(ssd_dt_bias, DT_OFF), alog=_pad_lanes(ssd_a_log, DT_OFF),
        dexp=row(jnp.repeat(ssd_d.astype(f32), SSD_HEAD_DIM, axis=1)), ssd_nw=row(ssd_norm_w),
        e=jnp.asarray(e, bf16), bd=jnp.asarray(bd, f32),
        gate_b=_pad_lanes(mlstm_gate_b, I_OFF), mlstm_nw=row(mlstm_norm_w),
        wa=w_branch_a.astype(bf16), wb=w_branch_b.astype(bf16), wo=w_out.astype(bf16),
        ln1_g=row(ln1_g), ln1_b=row(ln1_b),
        wup=ffn_w_up.astype(bf16), fcw=ffn_conv_w, fcb=row(ffn_conv_b), wdn=ffn_w_down.astype(bf16),
        ln2_g=row(ln2_g), ln2_b=row(ln2_b),
    )


class _Group:
    def __init__(self, batch, length, q, lr, gs, ssd_cps, mlstm_cps, proj_tm, merge_tm, ffn,
                 perm=False, lanes_ssd=False, mlstm_gs=None, ssd_gs=None, merge_parts=MERGE_PARTS):
        self.mlstm_gs = gs if mlstm_gs is None else mlstm_gs
        self.ssd_gs = gs if ssd_gs is None else ssd_gs
        self.merge_parts = merge_parts
        self.perm = perm
        self.lanes_ssd = lanes_ssd
        self.batch, self.length, self.q, self.lr, self.gs = batch, length, q, lr, gs
        self.ssd_cps, self.mlstm_cps = ssd_cps, mlstm_cps
        self.rows = batch * length
        self.proj_tm, self.merge_tm, self.ffn = proj_tm, merge_tm, ffn

    def cfg(self, name, layer):
        v = getattr(self, name)
        return v[layer] if isinstance(v, tuple) else v

    def tiling(self, cps, gs=None):
        rows = cps * self.q
        gs = self.gs if gs is None else gs
        steps = self.length // rows
        if gs == 1 or steps == 1:
            spec = lambda width, blk: pl.BlockSpec((gs * rows, width), lambda b, c: (b * steps + c, blk))
            return steps, spec, (lambda a: a), False
        spec = lambda width, blk: pl.BlockSpec((gs, rows, width), lambda b, c: (b, c, blk))
        return steps, spec, (lambda a: a.reshape(self.batch, self.length, a.shape[-1])), True


def _ssd(grp, p32, state, w, layer, prev):
    q, b, gs = grp.q, grp.batch, grp.cfg("ssd_gs", layer)
    has_state = state is not None
    cps = grp.cfg("ssd_cps", layer)
    steps, tile, view, tile3d = grp.tiling(cps, gs)
    kern = functools.partial(_ssd_kernel, q=q, lr=grp.lr, nc=steps, has_state=has_state, gs=gs, cps=cps,
                             perm=grp.perm, tile3d=tile3d)
    p32 = view(p32)
    inputs = [p32, p32, p32, p32]
    in_specs = [tile(D_MODEL, P32_Z), tile(D_MODEL, P32_XS),
                tile(SSD_BC, P32_BC_OFF // SSD_BC), tile(SMALL, P32_SM_OFF // SMALL)]
    if has_state:
        inputs += [state["csx"], state["csb"], state["h"]]
        in_specs += [_seq_spec(layer, gs, SSD_CONV - 1, D_MODEL), _seq_spec(layer, gs, SSD_CONV - 1, SSD_BC),
                     _seq_spec(layer, gs, SSD_HP, SSD_STATE)]
    inputs += [w["cwx"], w["cbx"], w["cwb"], w["cbb"], w["dtb"], w["alog"], w["dexp"], w["ssd_nw"], w["e"],
               w["bd"]]
    const = lambda *shape: pl.BlockSpec(shape, lambda b, c: (0,) * len(shape))
    in_specs += [_layer_spec(layer, SSD_CONV, D_MODEL), _layer_spec(layer, 1, D_MODEL),
                 _layer_spec(layer, SSD_CONV, SSD_BC), _layer_spec(layer, 1, SSD_BC),
                 _layer_spec(layer, 1, SMALL), _layer_spec(layer, 1, SMALL), _layer_spec(layer, 1, D_MODEL),
                 _layer_spec(layer, 1, D_MODEL), const(LANES, SSD_HP), const(SSD_GROUPS * SSD_STATE, SSD_HP)]
    ys, *new = _stacked_call(
        kern, name="ssd", grid=(b // gs, steps), inputs=inputs, in_specs=in_specs,
        out_shape=(jax.ShapeDtypeStruct((b, grp.length, D_MODEL) if tile3d else (grp.rows, D_MODEL), bf16),
                   jax.ShapeDtypeStruct((DEPTH, b, SSD_CONV - 1, D_MODEL), f32),
                   jax.ShapeDtypeStruct((DEPTH, b, SSD_CONV - 1, SSD_BC), f32),
                   jax.ShapeDtypeStruct((DEPTH, b, SSD_HP, SSD_STATE), f32)),
        out_specs=(tile(D_MODEL, 0), _seq_spec(layer, gs, SSD_CONV - 1, D_MODEL),
                   _seq_spec(layer, gs, SSD_CONV - 1, SSD_BC), _seq_spec(layer, gs, SSD_HP, SSD_STATE)),
        stacked={1: prev and prev[0], 2: prev and prev[1], 3: prev and prev[2]},
        scratch_shapes=[pltpu.VMEM((gs, SUBLANES + q, D_MODEL), f32), pltpu.VMEM((gs, SUBLANES + q, SSD_BC), f32),
                        pltpu.VMEM((gs, SSD_GROUPS * SSD_STATE, SSD_HP), f32), pltpu.VMEM((gs, q, D_MODEL), f32)],
        dimension_semantics=("parallel", "arbitrary"))
    return (ys.reshape(grp.rows, D_MODEL), *new)


def _mlstm(grp, qkv, p32, state, w, layer, prev):
    q, b, gs = grp.q, grp.batch, grp.mlstm_gs
    has_state = state is not None
    cps = grp.cfg("mlstm_cps", layer)
    steps, tile, view, tile3d = grp.tiling(cps, gs)
    kern = functools.partial(_mlstm_kernel, q=q, lr=grp.lr, nc=steps, has_state=has_state, gs=gs, cps=cps,
                             perm=grp.perm, tile3d=tile3d)
    hd = MLSTM_HEAD_DIM
    carried = not (has_state and steps == 1 and cps == 1)
    qkv, p32 = view(qkv), view(p32)
    inputs = [qkv, qkv, qkv, p32, p32]
    in_specs = [tile(D_MODEL, 0), tile(D_MODEL, 1), tile(D_MODEL, 2), tile(D_MODEL, P32_O),
                tile(SMALL, P32_SM_OFF // SMALL)]
    if has_state:
        inputs += [state["c"], state["n"], state["m"]]
        in_specs += [_seq_spec(layer, gs, MLSTM_HEADS, hd, hd), _seq_spec(layer, gs, MLSTM_HEADS, hd),
                     _seq_spec(layer, gs, 1, SMALL)]
    inputs += [w["gate_b"], w["mlstm_nw"]]
    in_specs += [_layer_spec(layer, 1, SMALL), _layer_spec(layer, 1, D_MODEL)]
    hm, *new = _stacked_call(
        kern, name="mlstm", grid=(b // gs, steps), inputs=inputs, in_specs=in_specs,
        out_shape=(jax.ShapeDtypeStruct((b, grp.length, D_MODEL) if tile3d else (grp.rows, D_MODEL), bf16),
                   jax.ShapeDtypeStruct((DEPTH, b, MLSTM_HEADS, hd, hd), f32),
                   jax.ShapeDtypeStruct((DEPTH, b, MLSTM_HEADS, hd), f32),
                   jax.ShapeDtypeStruct((DEPTH, b, 1, SMALL), f32)),
        out_specs=(tile(D_MODEL, 0), _seq_spec(layer, gs, MLSTM_HEADS, hd, hd),
                   _seq_spec(layer, gs, MLSTM_HEADS, hd), _seq_spec(layer, gs, 1, SMALL)),
        stacked={1: prev and prev[0], 2: prev and prev[1], 3: prev and prev[2]},
        scratch_shapes=[pltpu.VMEM((gs, MLSTM_HEADS, hd, hd) if carried else (gs, 1, SUBLANES, LANES), f32),
                        pltpu.VMEM((gs, MLSTM_HEADS, hd), f32), pltpu.VMEM((gs, 1, SMALL), f32)],
        dimension_semantics=("parallel", "arbitrary"))
    return (hm.reshape(grp.rows, D_MODEL), *new)


def _trunk(grp, x, state, w):
    ssd_out = mlstm_out = None
    ffn_out = []
    for layer in range(DEPTH):
        p32, qkv = _proj(x, w["w32"], w["wqkv"], layer, grp.cfg("proj_tm", layer))
        if grp.lanes_ssd:
            b, t = grp.batch, grp.length
            x_tm = x.reshape(b, t, D_MODEL).swapaxes(0, 1).reshape(t * b, D_MODEL)
            ys_t, *ssd_out = _ssd_lanes(_proj_t(x_tm, w["w32"], layer), state["cs_t"], state["h_lanes"], w, layer,
                                        ssd_out, steps=t, batch=b)
            ys = ys_t.reshape(D_MODEL, t, b).transpose(2, 1, 0).reshape(b * t, D_MODEL)
        else:
            ys, *ssd_out = _ssd(grp, p32, state, w, layer, ssd_out)
        hm, *mlstm_out = _mlstm(grp, qkv, p32, state, w, layer, mlstm_out)
        x1 = _merge(ys, hm, p32, x, w, layer, grp.cfg("merge_tm", layer), grp.cfg("merge_parts", layer))
        x, s_ffn = _ffn(x1, state["ffn"] if state is not None else None, w, layer, **grp.ffn)
        ffn_out.append(s_ffn)
    return x, ssd_out, mlstm_out, ffn_out


def _unpack_states(batch, ssd_out, mlstm_out):
    csx, csb, h = ssd_out
    c, n, m = mlstm_out
    return (h.reshape(DEPTH, batch, SSD_HEADS, SSD_HEAD_DIM, SSD_STATE),
            jnp.concatenate([csx, csb], axis=-1), c, n, m[:, :, 0, :MLSTM_HEADS])


def kernel(x_prompt, x_sample, state_ssd, state_ssd_conv, state_mlstm_c, state_mlstm_n, state_mlstm_m,
           state_ffn_conv, w_in, ssd_conv_w, ssd_conv_b, ssd_dt_bias, ssd_a_log, ssd_d, ssd_norm_w,
           mlstm_gate_b, mlstm_norm_w, w_branch_a, w_branch_b, w_out, ln1_g, ln1_b, ffn_w_up, ffn_conv_w,
           ffn_conv_b, ffn_w_down, ln2_g, ln2_b):
    w = _prep_weights(w_in, ssd_conv_w, ssd_conv_b, ssd_dt_bias, ssd_a_log, ssd_d, ssd_norm_w, mlstm_gate_b,
                      mlstm_norm_w, w_branch_a, w_branch_b, w_out, ln1_g, ln1_b, ffn_w_up, ffn_conv_w,
                      ffn_conv_b, ffn_w_down, ln2_g, ln2_b)
    keep = FFN_CONV - 1
    keep_ssd = SSD_CONV - 1

    bp, lp, _ = x_prompt.shape
    prompt = _Group(bp, lp, CHUNK, CHUNK, gs=1, ssd_cps=4, mlstm_cps=4, proj_tm=(512, 256), merge_tm=1024,
                    ffn=dict(groups=bp, tm=1024, seq_len=0, perm_q=CHUNK), perm=True, mlstm_gs=2, ssd_gs=2)
    per = CHUNK // SUBLANES
    xp_rows = x_prompt.reshape(bp, lp // CHUNK, SUBLANES, per, D_MODEL).swapaxes(2, 3)
    y_p, ssd_p, mlstm_p, ffn_p = _trunk(prompt, xp_rows.reshape(bp * lp, D_MODEL), None, w)
    y_p = y_p.reshape(bp, lp // CHUNK, per, SUBLANES, D_MODEL).swapaxes(2, 3)
    st_p = _unpack_states(bp, ssd_p, mlstm_p)
    ffn_conv_p = jnp.stack(ffn_p)[:, :, SUBLANES - 1::SUBLANES, :]

    bs, ls, _ = x_sample.shape
    lpad = max(ls, SAMPLE_PAD_LEN)
    s_rows = bs * lpad
    sample = _Group(bs, lpad, lpad, ls, gs=8, ssd_cps=1, mlstm_cps=1, proj_tm=s_rows,
                    merge_tm=512,
                    ffn=dict(groups=1, tm=256, seq_len=lpad), lanes_ssd=True)
    assert lpad == ls
    lane_b = lambda a: jnp.broadcast_to(a.astype(f32)[..., None], a.shape + (bs,))
    w.update(cw_b=lane_b(ssd_conv_w), cb_b=lane_b(ssd_conv_b), dtb_b=lane_b(ssd_dt_bias), alog_b=lane_b(ssd_a_log),
             dexp_b=lane_b(ssd_d), nw_b=lane_b(ssd_norm_w))
    s_state = dict(
        cs_t=jnp.transpose(state_ssd_conv, (0, 3, 2, 1)).reshape(DEPTH, D_MODEL + SSD_BC, keep_ssd * bs),
        h_lanes=jnp.transpose(state_ssd, (0, 2, 3, 4, 1)),
        c=state_mlstm_c, n=state_mlstm_n,
        m=jnp.pad(state_mlstm_m, ((0, 0), (0, 0), (0, SMALL - MLSTM_HEADS)))[:, :, None, :],
        ffn=state_ffn_conv.reshape(DEPTH, bs * keep, 2 * D_FF),
    )
    xs = jnp.pad(x_sample, ((0, 0), (0, lpad - ls), (0, 0))).reshape(s_rows, D_MODEL)
    y_s, (cs_t, h_lanes), mlstm_s, ffn_s = _trunk(sample, xs, s_state, w)
    c_s, n_s, m_s = mlstm_s
    st_s = (jnp.transpose(h_lanes, (0, 4, 1, 2, 3)),
            jnp.transpose(cs_t.reshape(DEPTH, D_MODEL + SSD_BC, keep_ssd, bs), (0, 3, 2, 1)),
            c_s, n_s, m_s[:, :, 0, :MLSTM_HEADS])
    ffn_conv_s = jnp.stack([u.reshape(bs, lpad, 2 * D_FF)[:, ls - keep:ls, :] for u in ffn_s])
    y_sample = y_s.reshape(bs, lpad, D_MODEL)[:, :ls, :]

    return (y_p.reshape(bp, lp, D_MODEL), y_sample, st_p[0], st_s[0], st_p[1], st_s[1], st_p[2], st_s[2],
            st_p[3], st_s[3], st_p[4], st_s[4], ffn_conv_p, ffn_conv_s)
```

```python
import functools
import itertools

import jax
import jax.numpy as jnp
import numpy as np
from jax import lax
from jax.experimental import pallas as pl
from jax.experimental.pallas import tpu as pltpu

f32 = jnp.float32
bf16 = jnp.bfloat16

D_MODEL = 1024
DEPTH = 2
SSD_HEADS = 16
SSD_HEAD_DIM = 64
SSD_STATE = 64
SSD_GROUPS = 2
SSD_CONV = 4
SSD_BC = 2 * SSD_GROUPS * SSD_STATE
SSD_HP = SSD_HEADS * SSD_HEAD_DIM
MLSTM_HEADS = 4
MLSTM_HEAD_DIM = 256
CHUNK = 128
D_FF = 2816
FFN_CONV = 3
ALPHA = (2 * DEPTH) ** 0.25
EPS = 1e-5

LANES = 128
SUBLANES = 8
SMALL = LANES
DT_OFF, I_OFF, F_OFF = 0, 16, 20
P32_Z, P32_O, P32_GA, P32_GB, P32_XS = 0, 1, 2, 3, 4
P32_BC_OFF = 5 * D_MODEL
P32_SM_OFF = P32_BC_OFF + SSD_BC
P32_W = P32_SM_OFF + 2 * SMALL
FF_CH = 256
FF_NCH = D_FF // FF_CH
MERGE_PARTS = 2
FF_UP_AHEAD = 2
FF_XP_SLOTS = 4
NEG_BIG = -1e30
SAMPLE_PAD_LEN = 4

NT_DIMS = (((1,), (1,)), ((), ()))
TN_DIMS = (((0,), (0,)), ((), ()))


def _dot(a, b):
    return jnp.dot(a, b, preferred_element_type=f32)


def _split3(x):
    hi = x.astype(bf16)
    r = x - hi.astype(f32)
    mid = r.astype(bf16)
    lo = (r - mid.astype(f32)).astype(bf16)
    return hi, mid, lo


def _dot01_rhs(x, e):
    hi, mid, lo = _split3(x)
    return _dot(hi, e) + _dot(mid, e) + _dot(lo, e)


def _dot01_lhs(t, x):
    hi, mid, lo = _split3(x)
    return _dot(t, hi) + _dot(t, mid) + _dot(t, lo)


def _softplus(x):
    return jnp.maximum(x, 0.0) + jnp.log1p(jnp.exp(-jnp.abs(x)))


def _silu(x):
    return x * jax.nn.sigmoid(x)


def _row_time(i, q, perm):
    if not perm:
        return i
    return (i & (SUBLANES - 1)) * (q // SUBLANES) + lax.shift_right_logical(i, SUBLANES.bit_length() - 1)


def _tri(q, perm=False):
    row = lax.broadcasted_iota(jnp.int32, (q, q), 0)
    col = lax.broadcasted_iota(jnp.int32, (q, q), 1)
    return _row_time(row, q, perm) >= _row_time(col, q, perm)


def _wrap_rows(cur_tail, prev_tail):
    out = []
    for i in range(cur_tail.shape[0] // SUBLANES):
        rows = slice(i * SUBLANES, (i + 1) * SUBLANES)
        first = lax.broadcasted_iota(jnp.int32, (SUBLANES, cur_tail.shape[1]), 0) == 0
        out.append(jnp.where(first, pltpu.roll(prev_tail[rows], 1, axis=0), pltpu.roll(cur_tail[rows], 1, axis=0)))
    return jnp.concatenate(out, axis=0)


def _shift_back(x, wrapped, j):
    n = j * SUBLANES
    return jnp.concatenate([wrapped[wrapped.shape[0] - n:], x[:x.shape[0] - n]], axis=0)


def _valid_rows(q, width, lr, is_last):
    row = lax.broadcasted_iota(jnp.int32, (q, width), 0)
    return row < jnp.where(is_last, lr, q)


def _layer_norm(r, g, b):
    mu = jnp.mean(r, axis=-1, keepdims=True)
    var = jnp.mean(jnp.square(r - mu), axis=-1, keepdims=True)
    return (r - mu) * lax.rsqrt(var + EPS) * g + b


def _layer_spec(layer, *shape, **kw):
    zeros = (0,) * len(shape)
    return pl.BlockSpec((None,) + shape, lambda *_: (layer,) + zeros, **kw)


def _seq_spec(layer, gs, *shape):
    zeros = (0,) * len(shape)
    return pl.BlockSpec((None, gs) + shape, lambda b, c: (layer, b) + zeros)


def _stacked_call(kern, *, name, grid, inputs, in_specs, out_shape, out_specs, stacked, scratch_shapes,
                  dimension_semantics, vmem_limit_bytes=None):
    prev = [(i, a) for i, a in sorted(stacked.items()) if a is not None]
    n_in = len(inputs)

    def body(*refs):
        kern(*refs[:n_in], *refs[n_in + len(prev):])

    return pl.pallas_call(
        body,
        out_shape=out_shape,
        grid=grid,
        in_specs=list(in_specs) + [pl.BlockSpec(memory_space=pl.ANY)] * len(prev),
        out_specs=out_specs,
        scratch_shapes=scratch_shapes,
        input_output_aliases={n_in + k: i for k, (i, _) in enumerate(prev)},
        compiler_params=pltpu.CompilerParams(dimension_semantics=dimension_semantics,
                                             vmem_limit_bytes=vmem_limit_bytes),
        name=name,
    )(*inputs, *[a for _, a in prev])


def _proj_kernel(x_ref, w32_ref, wqkv_ref, o32_ref, oqkv_ref):
    xb = x_ref[...].astype(bf16)
    o32_ref[...] = lax.dot_general(xb, w32_ref[...], NT_DIMS, preferred_element_type=f32)
    oqkv_ref[...] = lax.dot_general(xb, wqkv_ref[...], NT_DIMS, preferred_element_type=f32).astype(oqkv_ref.dtype)


def _proj(x, w32, wqkv, layer, tm):
    m, k = x.shape
    n32, nq = w32.shape[1], wqkv.shape[1]
    once = dict(pipeline_mode=pl.Buffered(1))
    return pl.pallas_call(
        _proj_kernel,
        out_shape=(jax.ShapeDtypeStruct((m, n32), f32), jax.ShapeDtypeStruct((m, nq), bf16)),
        grid=(m // tm,),
        in_specs=[pl.BlockSpec((tm, k), lambda i: (i, 0)),
                  _layer_spec(layer, n32, k, **once), _layer_spec(layer, nq, k, **once)],
        out_specs=(pl.BlockSpec((tm, n32), lambda i: (i, 0)), pl.BlockSpec((tm, nq), lambda i: (i, 0))),
        compiler_params=pltpu.CompilerParams(dimension_semantics=("parallel",),
                                             vmem_limit_bytes=56 * 1024 * 1024),
        name="proj",
    )(x, w32, wqkv)


class _Rows:
    def __init__(self, ref, start, n):
        self.ref, self.start, self.n, self.dtype = ref, start, n, ref.dtype

    def rows(self, off, n):
        return _Rows(self.ref, self.start + off, n)

    def _index(self, idx):
        cols = slice(None) if idx is Ellipsis else idx[1]
        return (slice(self.start, self.start + self.n), cols)

    def __getitem__(self, idx):
        return self.ref[self._index(idx)]

    def __setitem__(self, idx, value):
        self.ref[self._index(idx)] = value


def _per_sequence(seq_fn, refs, n_tile, n_state, n_param, gs, has_state, nc, rows, tile3d=False):
    n_state = n_state if has_state else 0
    tiles, refs = refs[:n_tile], refs[n_tile:]
    state, refs = refs[:n_state], refs[n_state:]
    params, (y_tile, *rest) = refs[:n_param], refs[n_param:]
    phases = []
    for g in range(gs):
        at = lambda group: tuple(r.at[g] for r in group)
        if tile3d:
            seq_rows = lambda group: tuple(_Rows(r.at[g], 0, rows) for r in group)
        else:
            seq_rows = lambda group: tuple(_Rows(r, g * rows, rows) for r in group)
        phases.append(seq_fn(*seq_rows(tiles), *at(state), *params, *seq_rows((y_tile,)), *at(rest)))
    c = pl.program_id(1)

    @pl.when(c == 0)
    def _():
        for init, _, _ in phases:
            init()

    for _ in itertools.zip_longest(*[body() for _, body, _ in phases]):
        pass

    @pl.when(c == nc - 1)
    def _():
        for _, _, final in phases:
            final()


def _ssd_kernel(*refs, q, lr, nc, has_state, gs, cps, perm, tile3d):
    seq = functools.partial(_ssd_seq, q=q, lr=lr, nc=nc, has_state=has_state, cps=cps, perm=perm)
    _per_sequence(seq, refs, 4, 3, 10, gs, has_state, nc, cps * q, tile3d)


def _ssd_seq(*refs, q, lr, nc, has_state, cps, perm):
    z_ref, xs_ref, bc_ref, sm_ref = refs[:4]
    refs = refs[4:]
    if has_state:
        csx_ref, csb_ref, h0_ref = refs[:3]
        refs = refs[3:]
    (cwx_ref, cbx_ref, cwb_ref, cbb_ref, dtb_ref, alog_ref, dexp_ref, nw_ref, e_ref, bd_ref,
     y_ref, ncsx_ref, ncsb_ref, hout_ref, xpx, xpb, ht, yb) = refs
    hdr = SUBLANES
    lo = hdr - (SSD_CONV - 1)
    n2 = SSD_GROUPS * SSD_STATE
    assert lr >= SSD_CONV - 1
    keep = SSD_CONV - 1
    assert not (perm and (has_state or lr != q))
    carried = [(i + 1) * SUBLANES - 1 for i in range(keep)]

    def init():
        if has_state:
            h_t = h0_ref[...].T
            ht[...] = jnp.where(bd_ref[...] > 0.5, jnp.concatenate([h_t, h_t], axis=0), 0.0)
            xpx[lo:hdr, :] = csx_ref[...]
            xpb[lo:hdr, :] = csb_ref[...]
        else:
            ht[...] = jnp.zeros_like(ht)
            rows = slice(0, keep * SUBLANES) if perm else slice(lo, hdr)
            xpx[rows, :] = jnp.zeros((rows.stop - rows.start, SSD_HP), f32)
            xpb[rows, :] = jnp.zeros((rows.stop - rows.start, SSD_BC), f32)

    def final():
        if perm:
            for i, r in enumerate(carried):
                ncsx_ref[i:i + 1, :] = xpx[r:r + 1, :]
                ncsb_ref[i:i + 1, :] = xpb[r:r + 1, :]
        else:
            ncsx_ref[...] = xpx[lo + lr:hdr + lr, :]
            ncsb_ref[...] = xpb[lo + lr:hdr + lr, :]
        h_new = ht[...]
        hout_ref[...] = (h_new[:SSD_STATE, :] + h_new[SSD_STATE:, :]).T

    def body():
        for k in range(cps):
            sub = lambda r: r.rows(k * q, q)
            is_last = (pl.program_id(1) == nc - 1) if k == cps - 1 else False
            yield from _ssd_body(sub(z_ref), sub(xs_ref), sub(bc_ref), sub(sm_ref), cwx_ref, cbx_ref, cwb_ref,
                                 cbb_ref, dtb_ref, alog_ref, dexp_ref, nw_ref, e_ref, bd_ref, sub(y_ref),
                                 xpx, xpb, ht, yb, q=q, lr=lr, is_last=is_last, perm=perm)

    return init, body, final


def _ssd_body(z_ref, xs_ref, bc_ref, sm_ref, cwx_ref, cbx_ref, cwb_ref, cbb_ref, dtb_ref, alog_ref,
              dexp_ref, nw_ref, e_ref, bd_ref, y_ref, xpx, xpb, ht, yb, *, q, lr, is_last, perm):
    hdr = SUBLANES
    lo = hdr - (SSD_CONV - 1)
    n2 = SSD_GROUPS * SSD_STATE
    block_diag = bd_ref[...] > 0.5

    dt = _softplus(sm_ref[...] + dtb_ref[...])
    if lr < q:
        dt = jnp.where(_valid_rows(q, SMALL, lr, is_last), dt, 0.0)
    a = -jnp.exp(alog_ref[...])
    d_a = dt * a
    causal = _tri(q, perm)
    tril = jnp.where(causal, 1.0, 0.0).astype(bf16)
    e = e_ref[...]
    acs = _dot01_lhs(tril, d_a)
    dt_x = _dot01_rhs(dt, e)
    yield

    if perm:
        keep_rows = (SSD_CONV - 1) * SUBLANES

        def conv(xp, x_ref, w_ref, b_ref):
            w = w_ref[...]
            x = x_ref[...]
            wrapped = _wrap_rows(x[q - keep_rows:, :], xp[0:keep_rows, :])
            acc = _shift_back(x, wrapped, SSD_CONV - 1) * w[0:1, :]
            for j in range(1, SSD_CONV - 1):
                acc = acc + _shift_back(x, wrapped, SSD_CONV - 1 - j) * w[j:j + 1, :]
            acc = acc + x * w[SSD_CONV - 1:SSD_CONV, :]
            xp[0:keep_rows, :] = x[q - keep_rows:, :]
            return acc + b_ref[...]
    else:
        xpx[hdr:hdr + q, :] = xs_ref[...]
        xpb[hdr:hdr + q, :] = bc_ref[...]

        def conv(xp, x_ref, w_ref, b_ref):
            w = w_ref[...]
            acc = xp[lo:lo + q, :] * w[0:1, :]
            for j in range(1, SSD_CONV):
                acc = acc + xp[lo + j:lo + j + q, :] * w[j:j + 1, :]
            return acc + b_ref[...]

    cb = conv(xpb, bc_ref, cwb_ref, cbb_ref)
    bcv = _silu(cb)
    bm = bcv[:, :n2].astype(bf16)
    cm = bcv[:, n2:]
    lane_g0 = lax.broadcasted_iota(jnp.int32, (q, n2), 1) < SSD_STATE
    acs_t = acs.T
    acs_x = _dot01_rhs(acs, e)
    yield
    cbms = [lax.dot_general(jnp.where(lane_g0 if g == 0 else jnp.logical_not(lane_g0), cm, 0.0).astype(bf16),
                            bm, NT_DIMS, preferred_element_type=f32) for g in range(SSD_GROUPS)]
    h_prev = ht[...]
    y_off = _dot(cm.astype(bf16), h_prev.astype(bf16))
    cx = conv(xpx, xs_ref, cwx_ref, cbx_ref)
    if not perm:
        tail_x = xpx[lo + q:hdr + q, :]
        tail_b = xpb[lo + q:hdr + q, :]
        xpx[lo:hdr, :] = tail_x
        xpb[lo:hdr, :] = tail_b
    yield
    xs = _silu(cx)
    last_x = acs_x[q - 1:q, :]
    xdt = xs * dt_x
    xdt_b = xdt.astype(bf16)
    yield
    lane_lo = lax.broadcasted_iota(jnp.int32, (q, LANES), 1) < SSD_HEAD_DIM
    heads_per_group = SSD_HEADS // SSD_GROUPS
    decays = [jnp.exp(jnp.where(causal, acs[:, hh:hh + 1] - acs_t[hh:hh + 1, :], -jnp.inf))
              for hh in range(SSD_HEADS)]
    yield
    weights = [(cbms[hh // heads_per_group] * decays[hh]).astype(bf16) for hh in range(SSD_HEADS)]
    xdtw = (xdt * jnp.exp(last_x - acs_x)).astype(bf16)
    yield
    ys = [_dot(weights[hh], xdt_b[:, (hh // 2) * LANES:(hh // 2 + 1) * LANES]) for hh in range(SSD_HEADS)]
    upd = lax.dot_general(bm, xdtw, TN_DIMS, preferred_element_type=f32)
    yield
    for p in range(SSD_HEADS // 2):
        yb[:, p * LANES:(p + 1) * LANES] = jnp.where(lane_lo, ys[2 * p], ys[2 * p + 1])
    ht[...] = jnp.exp(last_x) * h_prev + jnp.where(block_diag, upd, 0.0)
    yield
    y = yb[...] + y_off * jnp.exp(acs_x) + dexp_ref[...] * xs
    y = y * _silu(z_ref[...])
    yield
    y = y * lax.rsqrt(jnp.mean(jnp.square(y), axis=-1, keepdims=True) + EPS) * nw_ref[...]
    y_ref[...] = y.astype(y_ref.dtype)


PT_ROWS = 5 * 512
PT_XS, PT_BC, PT_SM = D_MODEL, 2 * D_MODEL, 2 * D_MODEL + SSD_BC


def _proj_t_kernel(x_ref, w_ref, o_ref, xb):
    @pl.when(pl.program_id(0) == 0)
    def _():
        xb[...] = x_ref[...].astype(bf16)

    o_ref[...] = lax.dot_general(w_ref[...], xb[...], NT_DIMS, preferred_element_type=f32)


def _proj_t(x_tm, w32, layer):
    m, k = x_tm.shape
    tn = 512
    xs_blk = P32_XS * D_MODEL // tn
    return pl.pallas_call(
        _proj_t_kernel,
        out_shape=jax.ShapeDtypeStruct((PT_ROWS, m), f32),
        grid=(PT_ROWS // tn,),
        in_specs=[pl.BlockSpec((m, k), lambda j: (0, 0)),
                  pl.BlockSpec((None, tn, k), lambda j: (layer, jnp.where(j < D_MODEL // tn, j, j + xs_blk - D_MODEL // tn), 0))],
        out_specs=pl.BlockSpec((tn, m), lambda j: (j, 0)),
        scratch_shapes=[pltpu.VMEM((m, k), bf16)],
        compiler_params=pltpu.CompilerParams(dimension_semantics=("arbitrary",)),
        name="proj_t",
    )(x_tm, w32)


def _ssd_lanes_kernel(pt_ref, cst_ref, h0_ref, cw_ref, cb_ref, dtb_ref, alog_ref, dexp_ref, nw_ref,
                      y_ref, ncs_ref, hout_ref, xc, dts, decs, ysc, *, steps, batch):
    hd = pl.program_id(0)
    n_ch = D_MODEL + SSD_BC
    keep = SSD_CONV - 1
    lanes = lambda t: slice(t * batch, (t + 1) * batch)

    @pl.when(hd == 0)
    def _():
        for t in range(steps):
            acc = None
            for j in range(SSD_CONV):
                i = t + j
                src = cst_ref[:, lanes(i)] if i < keep else pt_ref[PT_XS:PT_XS + n_ch, lanes(i - keep)]
                term = src * cw_ref[j]
                acc = term if acc is None else acc + term
            xc[:, lanes(t)] = _silu(acc + cb_ref[...])
        ncs_ref[...] = pt_ref[PT_XS:PT_XS + n_ch, (steps - keep) * batch:steps * batch]
        dt = _softplus(pt_ref[PT_SM:PT_SM + SSD_HEADS, :] + jnp.concatenate([dtb_ref[...]] * steps, axis=1))
        dts[...] = dt
        decs[...] = jnp.exp(dt * jnp.concatenate([-jnp.exp(alog_ref[...])] * steps, axis=1))

    grp_row = (hd // (SSD_HEADS // SSD_GROUPS)) * SSD_STATE
    xh = xc[pl.ds(pl.multiple_of(hd * SSD_HEAD_DIM, SSD_HEAD_DIM), SSD_HEAD_DIM), :]
    bh = xc[pl.ds(pl.multiple_of(D_MODEL + grp_row, SSD_STATE), SSD_STATE), :]
    ch = xc[pl.ds(pl.multiple_of(D_MODEL + SSD_GROUPS * SSD_STATE + grp_row, SSD_STATE), SSD_STATE), :]
    dth = dts[pl.ds(hd, 1), :]
    dech = decs[pl.ds(hd, 1), :]
    d_skip = dexp_ref[pl.ds(hd, 1), :]
    y_rows = [[] for _ in range(steps)]
    for p in range(SSD_HEAD_DIM):
        h = h0_ref[p]
        for t in range(steps):
            x_row = xh[p:p + 1, lanes(t)]
            h = dech[:, lanes(t)] * h + (x_row * dth[:, lanes(t)]) * bh[:, lanes(t)]
            y_rows[t].append(jnp.sum(ch[:, lanes(t)] * h, axis=0, keepdims=True) + d_skip * x_row)
        hout_ref[p] = h
    rows = pl.ds(pl.multiple_of(hd * SSD_HEAD_DIM, SSD_HEAD_DIM), SSD_HEAD_DIM)
    for t in range(steps):
        ysc[rows, lanes(t)] = jnp.concatenate(y_rows[t], axis=0)

    @pl.when(hd == SSD_HEADS - 1)
    def _():
        y = ysc[...] * _silu(pt_ref[0:D_MODEL, :])
        y = y * lax.rsqrt(jnp.mean(jnp.square(y), axis=0, keepdims=True) + EPS)
        y_ref[...] = (y * jnp.concatenate([nw_ref[...]] * steps, axis=1)).astype(y_ref.dtype)


def _ssd_lanes(pt, cst, h0, w, layer, prev, *, steps, batch):
    n_ch = D_MODEL + SSD_BC
    tb = steps * batch
    keep = SSD_CONV - 1
    assert steps >= keep and batch % LANES == 0
    kern = functools.partial(_ssd_lanes_kernel, steps=steps, batch=batch)
    full = lambda *shape: pl.BlockSpec(shape, lambda hd: (0,) * len(shape))
    hblock = pl.BlockSpec((None, None, SSD_HEAD_DIM, SSD_STATE, batch), lambda hd: (layer, hd, 0, 0, 0))
    return _stacked_call(
        kern, name="ssd_lanes", grid=(SSD_HEADS,),
        inputs=[pt, cst, h0, w["cw_b"], w["cb_b"], w["dtb_b"], w["alog_b"], w["dexp_b"], w["nw_b"]],
        in_specs=[full(PT_ROWS, tb), _layer_spec(layer, n_ch, keep * batch), hblock,
                  _layer_spec(layer, SSD_CONV, n_ch, batch), _layer_spec(layer, n_ch, batch),
                  _layer_spec(layer, SSD_HEADS, batch), _layer_spec(layer, SSD_HEADS, batch),
                  _layer_spec(layer, SSD_HEADS, batch), _layer_spec(layer, D_MODEL, batch)],
        out_shape=(jax.ShapeDtypeStruct((D_MODEL, tb), bf16),
                   jax.ShapeDtypeStruct((DEPTH, n_ch, keep * batch), f32),
                   jax.ShapeDtypeStruct((DEPTH, SSD_HEADS, SSD_HEAD_DIM, SSD_STATE, batch), f32)),
        out_specs=(full(D_MODEL, tb), _layer_spec(layer, n_ch, keep * batch), hblock),
        stacked={1: prev and prev[0], 2: prev and prev[1]},
        scratch_shapes=[pltpu.VMEM((n_ch, tb), f32), pltpu.VMEM((SSD_HEADS, tb), f32),
                        pltpu.VMEM((SSD_HEADS, tb), f32), pltpu.VMEM((D_MODEL, tb), f32)],
        dimension_semantics=("arbitrary",))


def _mlstm_kernel(*refs, q, lr, nc, has_state, gs, cps, perm, tile3d):
    seq = functools.partial(_mlstm_seq, q=q, lr=lr, nc=nc, has_state=has_state, cps=cps, perm=perm)
    _per_sequence(seq, refs, 5, 3, 2, gs, has_state, nc, cps * q, tile3d)


def _mlstm_seq(*refs, q, lr, nc, has_state, cps, perm):
    q_ref, k_ref, v_ref, o_ref, sm_ref = refs[:5]
    refs = refs[5:]
    if has_state:
        c0_ref, n0_ref, m0_ref = refs[:3]
        refs = refs[3:]
    gb_ref, nw_ref, h_ref, cout_ref, nout_ref, mout_ref, cs, ns, ms = refs
    direct = has_state and nc == 1 and cps == 1

    def init():
        if direct:
            return
        if has_state:
            cs[...] = c0_ref[...]
            ns[...] = n0_ref[...]
            ms[...] = m0_ref[...]
        else:
            cs[...] = jnp.zeros_like(cs)
            ns[...] = jnp.zeros_like(ns)
            ms[...] = jnp.zeros_like(ms)

    def final():
        if direct:
            return
        cout_ref[...] = cs[...]
        nout_ref[...] = ns[...]
        mout_ref[...] = ms[...]

    def body():
        chunks = []
        for k in range(cps):
            sub = lambda r, k=k: r.rows(k * q, q)
            is_last = (pl.program_id(1) == nc - 1) if k == cps - 1 else False
            src = (c0_ref, n0_ref, m0_ref) if direct else (cs, ns, ms)
            dst = (cout_ref, nout_ref, mout_ref) if direct else (cs, ns, ms)
            chunks.append(_mlstm_body(sub(q_ref), sub(k_ref), sub(v_ref), sub(o_ref), sub(sm_ref), gb_ref, nw_ref,
                                      sub(h_ref), src, dst, q=q, lr=lr, is_last=is_last, perm=perm))
        yield from _staggered(chunks, MLSTM_STATE_STAGES)

    return init, body, final


MLSTM_STATE_STAGES = 8


def _staggered(gens, skew):
    done = [False] * len(gens)
    t = 0
    while not all(done):
        for i, g in enumerate(gens):
            if done[i] or t < i * skew:
                continue
            try:
                next(g)
            except StopIteration:
                done[i] = True
        t += 1
        yield


def _mlstm_body(q_ref, k_ref, v_ref, o_ref, sm_ref, gb_ref, nw_ref, h_ref, src, dst, *, q, lr, is_last, perm):
    c_src, n_src, m_src = src
    c_dst, n_dst, m_dst = dst
    sm = sm_ref[...] + gb_ref[...]
    logf = -_softplus(-sm)
    ipre = sm
    if lr < q:
        valid = _valid_rows(q, SMALL, lr, is_last)
        logf = jnp.where(valid, logf, 0.0)
        ipre = jnp.where(valid, ipre, NEG_BIG)
    causal = _tri(q, perm)
    tril = jnp.where(causal, 1.0, 0.0).astype(bf16)
    yield
    bcum = _dot01_lhs(tril, logf)
    ipre_t = ipre.T
    yield
    bcum_t = bcum.T
    lane = lax.broadcasted_iota(jnp.int32, (1, SMALL), 1)
    k_scale = MLSTM_HEAD_DIM ** -0.5

    heads = range(MLSTM_HEADS)
    sls = [slice(h * MLSTM_HEAD_DIM, (h + 1) * MLSTM_HEAD_DIM) for h in heads]
    q_all, k_all, v_all, o_all = q_ref[...], k_ref[...], v_ref[...], o_ref[...]
    qs = [q_all[:, sl] for sl in sls]
    ks = [k_all[:, sl] * k_scale for sl in sls]
    vs = [v_all[:, sl] for sl in sls]
    b_cols = [bcum[:, F_OFF + h:F_OFF + h + 1] for h in heads]
    i_cols = [ipre[:, I_OFF + h:I_OFF + h + 1] for h in heads]
    dmats = [jnp.where(causal, b_cols[h] - bcum_t[F_OFF + h:F_OFF + h + 1, :] + ipre_t[I_OFF + h:I_OFF + h + 1, :],
                       -jnp.inf) for h in heads]
    yield
    qk = [lax.dot_general(qs[h], ks[h], NT_DIMS, preferred_element_type=f32) for h in heads]
    d_max = [jnp.max(dmats[h], axis=-1, keepdims=True) for h in heads]
    yield
    n_all = n_src[...]
    m_all = m_src[...]
    m_new = m_all
    cs_in = [c_src[h] for h in heads]
    m_prevs = [m_all[:, h:h + 1] for h in heads]
    qc = [_dot(qs[h], cs_in[h].astype(bf16)) for h in heads]
    inters = [b_cols[h] + m_prevs[h] for h in heads]
    m_ts = [jnp.maximum(inters[h], d_max[h]) for h in heads]
    yield
    w_inters = [jnp.exp(inters[h] - m_ts[h]) for h in heads]
    ss = [qk[h] * jnp.exp(dmats[h] - m_ts[h]) for h in heads]
    yield
    sv = [_dot(ss[h].astype(bf16), vs[h]) for h in heads]
    m_ends = [m_ts[h][q - 1:q, :] for h in heads]
    b_lasts = [b_cols[h][q - 1:q, :] for h in heads]
    kws = [ks[h].astype(f32) * jnp.exp(b_lasts[h] - b_cols[h] + i_cols[h] - m_ends[h]) for h in heads]
    yield
    kv = [lax.dot_general(kws[h].astype(bf16), vs[h], TN_DIMS, preferred_element_type=f32) for h in heads]
    qns = [jnp.sum(qs[h].astype(f32) * n_all[h:h + 1, :], axis=-1, keepdims=True) for h in heads]
    yield
    dens = [jnp.sum(ss[h], axis=-1, keepdims=True) + w_inters[h] * qns[h] for h in heads]
    yield
    hvs = [(sv[h] + w_inters[h] * qc[h]) / jnp.maximum(jnp.abs(dens[h]), jnp.exp(-m_ts[h])) for h in heads]
    yield
    rms = [lax.rsqrt(jnp.mean(jnp.square(hvs[h]), axis=-1, keepdims=True) + EPS) for h in heads]
    yield
    h_new = [(hvs[h] * rms[h] * nw_ref[:, sls[h]] * jax.nn.sigmoid(o_all[:, sls[h]])).astype(h_ref.dtype)
             for h in heads]
    w_cs = [jnp.exp(b_lasts[h] + m_prevs[h] - m_ends[h]) for h in heads]
    yield
    for h in heads:
        c_dst[h] = w_cs[h] * cs_in[h] + kv[h]
        m_new = jnp.where(lane == h, m_ends[h], m_new)
    h_ref[...] = jnp.concatenate(h_new, axis=1)
    n_dst[...] = jnp.concatenate(
        [w_cs[h] * n_all[h:h + 1, :] + jnp.sum(kws[h], axis=0, keepdims=True) for h in heads], axis=0)
    m_dst[...] = m_new


def _merge_kernel(ys_ref, hm_ref, ga_ref, gb_ref, x_ref, wa_ref, wb_ref, wo_ref, g_ref, b_ref, o_ref, *, parts):
    tm = x_ref.shape[0]
    rows = [pl.ds(i * (tm // parts), tm // parts) for i in range(parts)]
    br = [(_dot(ys_ref[r, :], wa_ref[...]), _dot(hm_ref[r, :], wb_ref[...])) for r in rows]
    merged = [(jax.nn.sigmoid(ga_ref[r, :]) * a + jax.nn.sigmoid(gb_ref[r, :]) * b).astype(bf16)
              for r, (a, b) in zip(rows, br)]
    mix = [_dot(m, wo_ref[...]) for m in merged]
    for r, m in zip(rows, mix):
        o_ref[r, :] = _layer_norm(ALPHA * x_ref[r, :] + m, g_ref[...], b_ref[...])


def _merge(ys, hm, p32, x, w, layer, tm, parts=MERGE_PARTS):
    m = x.shape[0]
    row = lambda blk: pl.BlockSpec((tm, D_MODEL), lambda i: (i, blk))
    return pl.pallas_call(
        functools.partial(_merge_kernel, parts=parts),
        out_shape=jax.ShapeDtypeStruct((m, D_MODEL), f32),
        grid=(m // tm,),
        in_specs=[row(0), row(0), row(P32_GA), row(P32_GB), row(0),
                  _layer_spec(layer, D_MODEL, D_MODEL), _layer_spec(layer, D_MODEL, D_MODEL),
                  _layer_spec(layer, D_MODEL, D_MODEL), _layer_spec(layer, 1, D_MODEL),
                  _layer_spec(layer, 1, D_MODEL)],
        out_specs=row(0),
        compiler_params=pltpu.CompilerParams(dimension_semantics=("parallel",)),
        name="merge",
    )(ys, hm, p32, p32, x, w["wa"], w["wb"], w["wo"], w["ln1_g"], w["ln1_b"])


def _ffn_kernel(*refs, tm, seq_len, perm_q, ahead_n):
    multi = seq_len > 0
    keep_rows = (FFN_CONV - 1) * SUBLANES
    if multi:
        x_ref, st_ref, wup_ref, cw_ref, cb_ref, wdn_ref, g_ref, b_ref, o_ref, sout_ref, xp = refs
    else:
        x_ref, wup_ref, cw_ref, cb_ref, wdn_ref, g_ref, b_ref, o_ref, sout_ref, xp, carry = refs
        @pl.when(pl.program_id(1) == 0)
        def _():
            carry[...] = jnp.zeros_like(carry)

    hdr = SUBLANES
    x = x_ref[...]
    xb = x.astype(bf16)
    if multi:
        assert seq_len & (seq_len - 1) == 0
        nseq = tm // seq_len
        t = lax.broadcasted_iota(jnp.int32, (tm, FF_CH), 0) & (seq_len - 1)
        row = lax.broadcasted_iota(jnp.int32, (tm, 2 * nseq), 0)
        col = lax.broadcasted_iota(jnp.int32, (tm, 2 * nseq), 1)
        t_sel = row & (seq_len - 1)
        seq0 = lax.shift_right_logical(row - t_sel, (seq_len // 2).bit_length() - 1)
        sel_p2 = jnp.where(col == seq0 + t_sel, jnp.where(t_sel < 2, 1.0, 0.0), 0.0).astype(bf16)
        sel_p1 = jnp.where(col == seq0 + 1, jnp.where(t_sel == 0, 1.0, 0.0), 0.0).astype(bf16)
        xp[:, 0:hdr, :] = jnp.zeros((FF_XP_SLOTS, hdr, FF_CH), f32)

    def cols_of(c, part):
        return slice(part * D_FF + c * FF_CH, part * D_FF + (c + 1) * FF_CH)

    def up(c):
        return [_dot(xb, wup_ref[:, cols_of(c, part)]) for part in range(2)]

    def conv_act(c, us):
        halves = []
        for part, u in enumerate(us):
            cols = cols_of(c, part)
            slot = (2 * c + part) % FF_XP_SLOTS
            if perm_q:
                prev = carry[:, cols]
                p1, p2 = [], []
                for kk in range(tm // perm_q):
                    uc = u[kk * perm_q:(kk + 1) * perm_q, :]
                    wrapped = _wrap_rows(uc[perm_q - keep_rows:, :], prev)
                    p1.append(_shift_back(uc, wrapped, 1))
                    p2.append(_shift_back(uc, wrapped, 2))
                    prev = uc[perm_q - keep_rows:, :]
                carry[:, cols] = prev
                p1, p2 = jnp.concatenate(p1, axis=0), jnp.concatenate(p2, axis=0)
                w = cw_ref[:, cols]
                halves.append(p2 * w[0:1, :] + p1 * w[1:2, :] + u * w[2:3, :] + cb_ref[:, cols])
                continue
            xp[slot, hdr:hdr + tm, :] = u
            if multi:
                sout_ref[:, cols] = u
                st = st_ref[:, cols]
                p1 = jnp.where(t == 0, _dot01_lhs(sel_p1, st), xp[slot, hdr - 1:hdr - 1 + tm, :])
                p2 = jnp.where(t < 2, _dot01_lhs(sel_p2, st), xp[slot, hdr - 2:hdr - 2 + tm, :])
            else:
                xp[slot, 0:hdr, :] = carry[:, cols]
                p1 = xp[slot, hdr - 1:hdr - 1 + tm, :]
                p2 = xp[slot, hdr - 2:hdr - 2 + tm, :]
                carry[:, cols] = u[tm - hdr:tm, :]
            w = cw_ref[:, cols]
            halves.append(p2 * w[0:1, :] + p1 * w[1:2, :] + u * w[2:3, :] + cb_ref[:, cols])
        return (_silu(halves[0]) * halves[1]).astype(bf16)

    acc = None
    ahead = [up(c) for c in range(min(ahead_n, FF_NCH))]
    pending = None
    for c in range(FF_NCH):
        if c + ahead_n < FF_NCH:
            ahead.append(up(c + ahead_n))
        if pending is not None:
            d = _dot(pending, wdn_ref[(c - 1) * FF_CH:c * FF_CH, :])
            acc = d if acc is None else acc + d
        pending = conv_act(c, ahead.pop(0))
    acc = acc + _dot(pending, wdn_ref[(FF_NCH - 1) * FF_CH:FF_NCH * FF_CH, :])

    if not multi:
        sout_ref[...] = carry[...]
    o_ref[...] = _layer_norm(ALPHA * x + acc, g_ref[...], b_ref[...])


def _ffn(x, st, w, layer, *, groups, tm, seq_len, perm_q=0, ahead=FF_UP_AHEAD):
    m = x.shape[0]
    ahead = ahead[layer] if isinstance(ahead, tuple) else ahead
    tm = tm[layer] if isinstance(tm, tuple) else tm
    tiles = m // (groups * tm)
    multi = seq_len > 0
    kern = functools.partial(_ffn_kernel, tm=tm, seq_len=seq_len, perm_q=perm_q, ahead_n=ahead)
    carry_rows = (FFN_CONV - 1) * SUBLANES if perm_q else SUBLANES
    once = dict(pipeline_mode=pl.Buffered(1))
    x_spec = pl.BlockSpec((tm, D_MODEL), lambda s, j: (s * tiles + j, 0))
    w_specs = [_layer_spec(layer, D_MODEL, 2 * D_FF, **once), _layer_spec(layer, FFN_CONV, 2 * D_FF, **once),
               _layer_spec(layer, 1, 2 * D_FF, **once), _layer_spec(layer, D_FF, D_MODEL, **once),
               _layer_spec(layer, 1, D_MODEL, **once), _layer_spec(layer, 1, D_MODEL, **once)]
    w_args = (w["wup"], w["fcw"], w["fcb"], w["wdn"], w["ln2_g"], w["ln2_b"])
    xp = pltpu.VMEM((FF_XP_SLOTS, SUBLANES + (0 if perm_q else tm), FF_CH), f32)
    if multi:
        nst = 2 * (tm // seq_len)
        inputs = (x, st) + w_args
        in_specs = [x_spec, pl.BlockSpec((None, nst, 2 * D_FF), lambda s, j: (layer, s * tiles + j, 0))] + w_specs
        sout_shape = jax.ShapeDtypeStruct((m, 2 * D_FF), f32)
        sout_spec = pl.BlockSpec((tm, 2 * D_FF), lambda s, j: (s * tiles + j, 0))
        scratch = [xp]
    else:
        inputs = (x,) + w_args
        in_specs = [x_spec] + w_specs
        sout_shape = jax.ShapeDtypeStruct((groups, carry_rows, 2 * D_FF), f32)
        sout_spec = pl.BlockSpec((None, carry_rows, 2 * D_FF), lambda s, j: (s, 0, 0))
        scratch = [xp, pltpu.VMEM((carry_rows, 2 * D_FF), f32)]
    return pl.pallas_call(
        kern,
        out_shape=(jax.ShapeDtypeStruct((m, D_MODEL), f32), sout_shape),
        grid=(groups, tiles),
        in_specs=in_specs,
        out_specs=(x_spec, sout_spec),
        scratch_shapes=scratch,
        compiler_params=pltpu.CompilerParams(dimension_semantics=("parallel", "arbitrary"),
                                             vmem_limit_bytes=56 * 1024 * 1024),
        name="ffn",
    )(*inputs)


def _pad_lanes(v, off, width=SMALL):
    out = jnp.zeros((v.shape[0], 1, width), f32)
    return out.at[:, 0, off:off + v.shape[1]].set(v.astype(f32))


def _prep_weights(w_in, ssd_conv_w, ssd_conv_b, ssd_dt_bias, ssd_a_log, ssd_d, ssd_norm_w, mlstm_gate_b,
                  mlstm_norm_w, w_branch_a, w_branch_b, w_out, ln1_g, ln1_b, ffn_w_up, ffn_conv_w,
                  ffn_conv_b, ffn_w_down, ln2_g, ln2_b):
    d = D_MODEL
    o_z, o_xbc, o_dt = 0, d, d + d + SSD_BC
    o_q = o_dt + SSD_HEADS
    o_if = o_q + 3 * d
    o_o = o_if + 2 * MLSTM_HEADS
    o_g = o_o + d
    w_t = jnp.swapaxes(w_in, 1, 2)
    cols = lambda a, n: w_t[:, a:a + n, :]
    zeros = lambda n: jnp.zeros((DEPTH, n, d), w_in.dtype)
    w32 = jnp.concatenate([cols(o_z, d), cols(o_o, d), cols(o_g, 2 * d), cols(o_xbc, d + SSD_BC),
                           cols(o_dt, SSD_HEADS), cols(o_if, 2 * MLSTM_HEADS),
                           zeros(P32_W - P32_SM_OFF - SSD_HEADS - 2 * MLSTM_HEADS)], axis=1).astype(bf16)
    e = (np.arange(SSD_HP)[None, :] // SSD_HEAD_DIM == np.arange(LANES)[:, None])
    bd = ((np.arange(SSD_GROUPS * SSD_STATE)[:, None] < SSD_STATE)
          == (np.arange(SSD_HP)[None, :] < SSD_HP // SSD_GROUPS))
    row = lambda a: a[:, None, :]
    return dict(
        w32=w32, wqkv=cols(o_q, 3 * d).astype(bf16),
        cwx=ssd_conv_w[:, :, :d], cbx=row(ssd_conv_b[:, :d]),
        cwb=ssd_conv_w[:, :, d:], cbb=row(ssd_conv_b[:, d:]),
        dtb=_pad_lanes(ssd_dt_bias, DT_OFF), alog=_pad_lanes(ssd_a_log, DT_OFF),
        dexp=row(jnp.repeat(ssd_d.astype(f32), SSD_HEAD_DIM, axis=1)), ssd_nw=row(ssd_norm_w),
        e=jnp.asarray(e, bf16), bd=jnp.asarray(bd, f32),
        gate_b=_pad_lanes(mlstm_gate_b, I_OFF), mlstm_nw=row(mlstm_norm_w),
        wa=w_branch_a.astype(bf16), wb=w_branch_b.astype(bf16), wo=w_out.astype(bf16),
        ln1_g=row(ln1_g), ln1_b=row(ln1_b),
        wup=ffn_w_up.astype(bf16), fcw=ffn_conv_w, fcb=row(ffn_conv_b), wdn=ffn_w_down.astype(bf16),
        ln2_g=row(ln2_g), ln2_b=row(ln2_b),
    )


class _Group:
    def __init__(self, batch, length, q, lr, gs, ssd_cps, mlstm_cps, proj_tm, merge_tm, ffn,
                 perm=False, lanes_ssd=False, mlstm_gs=None, ssd_gs=None, merge_parts=MERGE_PARTS):
        self.mlstm_gs = gs if mlstm_gs is None else mlstm_gs
        self.ssd_gs = gs if ssd_gs is None else ssd_gs
        self.merge_parts = merge_parts
        self.perm = perm
        self.lanes_ssd = lanes_ssd
        self.batch, self.length, self.q, self.lr, self.gs = batch, length, q, lr, gs
        self.ssd_cps, self.mlstm_cps = ssd_cps, mlstm_cps
        self.rows = batch * length
        self.proj_tm, self.merge_tm, self.ffn = proj_tm, merge_tm, ffn

    def cfg(self, name, layer):
        v = getattr(self, name)
        return v[layer] if isinstance(v, tuple) else v

    def tiling(self, cps, gs=None):
        rows = cps * self.q
        gs = self.gs if gs is None else gs
        steps = self.length // rows
        if gs == 1 or steps == 1:
            spec = lambda width, blk: pl.BlockSpec((gs * rows, width), lambda b, c: (b * steps + c, blk))
            return steps, spec, (lambda a: a), False
        spec = lambda width, blk: pl.BlockSpec((gs, rows, width), lambda b, c: (b, c, blk))
        return steps, spec, (lambda a: a.reshape(self.batch, self.length, a.shape[-1])), True


def _ssd(grp, p32, state, w, layer, prev):
    q, b, gs = grp.q, grp.batch, grp.cfg("ssd_gs", layer)
    has_state = state is not None
    cps = grp.cfg("ssd_cps", layer)
    steps, tile, view, tile3d = grp.tiling(cps, gs)
    kern = functools.partial(_ssd_kernel, q=q, lr=grp.lr, nc=steps, has_state=has_state, gs=gs, cps=cps,
                             perm=grp.perm, tile3d=tile3d)
    p32 = view(p32)
    inputs = [p32, p32, p32, p32]
    in_specs = [tile(D_MODEL, P32_Z), tile(D_MODEL, P32_XS),
                tile(SSD_BC, P32_BC_OFF // SSD_BC), tile(SMALL, P32_SM_OFF // SMALL)]
    if has_state:
        inputs += [state["csx"], state["csb"], state["h"]]
        in_specs += [_seq_spec(layer, gs, SSD_CONV - 1, D_MODEL), _seq_spec(layer, gs, SSD_CONV - 1, SSD_BC),
                     _seq_spec(layer, gs, SSD_HP, SSD_STATE)]
    inputs += [w["cwx"], w["cbx"], w["cwb"], w["cbb"], w["dtb"], w["alog"], w["dexp"], w["ssd_nw"], w["e"],
               w["bd"]]
    const = lambda *shape: pl.BlockSpec(shape, lambda b, c: (0,) * len(shape))
    in_specs += [_layer_spec(layer, SSD_CONV, D_MODEL), _layer_spec(layer, 1, D_MODEL),
                 _layer_spec(layer, SSD_CONV, SSD_BC), _layer_spec(layer, 1, SSD_BC),
                 _layer_spec(layer, 1, SMALL), _layer_spec(layer, 1, SMALL), _layer_spec(layer, 1, D_MODEL),
                 _layer_spec(layer, 1, D_MODEL), const(LANES, SSD_HP), const(SSD_GROUPS * SSD_STATE, SSD_HP)]
    ys, *new = _stacked_call(
        kern, name="ssd", grid=(b // gs, steps), inputs=inputs, in_specs=in_specs,
        out_shape=(jax.ShapeDtypeStruct((b, grp.length, D_MODEL) if tile3d else (grp.rows, D_MODEL), bf16),
                   jax.ShapeDtypeStruct((DEPTH, b, SSD_CONV - 1, D_MODEL), f32),
                   jax.ShapeDtypeStruct((DEPTH, b, SSD_CONV - 1, SSD_BC), f32),
                   jax.ShapeDtypeStruct((DEPTH, b, SSD_HP, SSD_STATE), f32)),
        out_specs=(tile(D_MODEL, 0), _seq_spec(layer, gs, SSD_CONV - 1, D_MODEL),
                   _seq_spec(layer, gs, SSD_CONV - 1, SSD_BC), _seq_spec(layer, gs, SSD_HP, SSD_STATE)),
        stacked={1: prev and prev[0], 2: prev and prev[1], 3: prev and prev[2]},
        scratch_shapes=[pltpu.VMEM((gs, SUBLANES + q, D_MODEL), f32), pltpu.VMEM((gs, SUBLANES + q, SSD_BC), f32),
                        pltpu.VMEM((gs, SSD_GROUPS * SSD_STATE, SSD_HP), f32), pltpu.VMEM((gs, q, D_MODEL), f32)],
        dimension_semantics=("parallel", "arbitrary"))
    return (ys.reshape(grp.rows, D_MODEL), *new)


def _mlstm(grp, qkv, p32, state, w, layer, prev):
    q, b, gs = grp.q, grp.batch, grp.mlstm_gs
    has_state = state is not None
    cps = grp.cfg("mlstm_cps", layer)
    steps, tile, view, tile3d = grp.tiling(cps, gs)
    kern = functools.partial(_mlstm_kernel, q=q, lr=grp.lr, nc=steps, has_state=has_state, gs=gs, cps=cps,
                             perm=grp.perm, tile3d=tile3d)
    hd = MLSTM_HEAD_DIM
    carried = not (has_state and steps == 1 and cps == 1)
    qkv, p32 = view(qkv), view(p32)
    inputs = [qkv, qkv, qkv, p32, p32]
    in_specs = [tile(D_MODEL, 0), tile(D_MODEL, 1), tile(D_MODEL, 2), tile(D_MODEL, P32_O),
                tile(SMALL, P32_SM_OFF // SMALL)]
    if has_state:
        inputs += [state["c"], state["n"], state["m"]]
        in_specs += [_seq_spec(layer, gs, MLSTM_HEADS, hd, hd), _seq_spec(layer, gs, MLSTM_HEADS, hd),
                     _seq_spec(layer, gs, 1, SMALL)]
    inputs += [w["gate_b"], w["mlstm_nw"]]
    in_specs += [_layer_spec(layer, 1, SMALL), _layer_spec(layer, 1, D_MODEL)]
    hm, *new = _stacked_call(
        kern, name="mlstm", grid=(b // gs, steps), inputs=inputs, in_specs=in_specs,
        out_shape=(jax.ShapeDtypeStruct((b, grp.length, D_MODEL) if tile3d else (grp.rows, D_MODEL), bf16),
                   jax.ShapeDtypeStruct((DEPTH, b, MLSTM_HEADS, hd, hd), f32),
                   jax.ShapeDtypeStruct((DEPTH, b, MLSTM_HEADS, hd), f32),
                   jax.ShapeDtypeStruct((DEPTH, b, 1, SMALL), f32)),
        out_specs=(tile(D_MODEL, 0), _seq_spec(layer, gs, MLSTM_HEADS, hd, hd),
                   _seq_spec(layer, gs, MLSTM_HEADS, hd), _seq_spec(layer, gs, 1, SMALL)),
        stacked={1: prev and prev[0], 2: prev and prev[1], 3: prev and prev[2]},
        scratch_shapes=[pltpu.VMEM((gs, MLSTM_HEADS, hd, hd) if carried else (gs, 1, SUBLANES, LANES), f32),
                        pltpu.VMEM((gs, MLSTM_HEADS, hd), f32), pltpu.VMEM((gs, 1, SMALL), f32)],
        dimension_semantics=("parallel", "arbitrary"))
    return (hm.reshape(grp.rows, D_MODEL), *new)


def _trunk(grp, x, state, w):
    ssd_out = mlstm_out = None
    ffn_out = []
    for layer in range(DEPTH):
        p32, qkv = _proj(x, w["w32"], w["wqkv"], layer, grp.cfg("proj_tm", layer))
        if grp.lanes_ssd:
            b, t = grp.batch, grp.length
            x_tm = x.reshape(b, t, D_MODEL).swapaxes(0, 1).reshape(t * b, D_MODEL)
            ys_t, *ssd_out = _ssd_lanes(_proj_t(x_tm, w["w32"], layer), state["cs_t"], state["h_lanes"], w, layer,
                                        ssd_out, steps=t, batch=b)
            ys = ys_t.reshape(D_MODEL, t, b).transpose(2, 1, 0).reshape(b * t, D_MODEL)
        else:
            ys, *ssd_out = _ssd(grp, p32, state, w, layer, ssd_out)
        hm, *mlstm_out = _mlstm(grp, qkv, p32, state, w, layer, mlstm_out)
        x1 = _merge(ys, hm, p32, x, w, layer, grp.cfg("merge_tm", layer), grp.cfg("merge_parts", layer))
        x, s_ffn = _ffn(x1, state["ffn"] if state is not None else None, w, layer, **grp.ffn)
        ffn_out.append(s_ffn)
    return x, ssd_out, mlstm_out, ffn_out


def _unpack_states(batch, ssd_out, mlstm_out):
    csx, csb, h = ssd_out
    c, n, m = mlstm_out
    return (h.reshape(DEPTH, batch, SSD_HEADS, SSD_HEAD_DIM, SSD_STATE),
            jnp.concatenate([csx, csb], axis=-1), c, n, m[:, :, 0, :MLSTM_HEADS])


def kernel(x_prompt, x_sample, state_ssd, state_ssd_conv, state_mlstm_c, state_mlstm_n, state_mlstm_m,
           state_ffn_conv, w_in, ssd_conv_w, ssd_conv_b, ssd_dt_bias, ssd_a_log, ssd_d, ssd_norm_w,
           mlstm_gate_b, mlstm_norm_w, w_branch_a, w_branch_b, w_out, ln1_g, ln1_b, ffn_w_up, ffn_conv_w,
           ffn_conv_b, ffn_w_down, ln2_g, ln2_b):
    w = _prep_weights(w_in, ssd_conv_w, ssd_conv_b, ssd_dt_bias, ssd_a_log, ssd_d, ssd_norm_w, mlstm_gate_b,
                      mlstm_norm_w, w_branch_a, w_branch_b, w_out, ln1_g, ln1_b, ffn_w_up, ffn_conv_w,
                      ffn_conv_b, ffn_w_down, ln2_g, ln2_b)
    keep = FFN_CONV - 1
    keep_ssd = SSD_CONV - 1

    bp, lp, _ = x_prompt.shape
    prompt = _Group(bp, lp, CHUNK, CHUNK, gs=1, ssd_cps=4, mlstm_cps=4, proj_tm=512, merge_tm=1024,
                    ffn=dict(groups=bp, tm=(1024, 512), seq_len=0, perm_q=CHUNK), perm=True, mlstm_gs=2, ssd_gs=2)
    per = CHUNK // SUBLANES
    xp_rows = x_prompt.reshape(bp, lp // CHUNK, SUBLANES, per, D_MODEL).swapaxes(2, 3)
    y_p, ssd_p, mlstm_p, ffn_p = _trunk(prompt, xp_rows.reshape(bp * lp, D_MODEL), None, w)
    y_p = y_p.reshape(bp, lp // CHUNK, per, SUBLANES, D_MODEL).swapaxes(2, 3)
    st_p = _unpack_states(bp, ssd_p, mlstm_p)
    ffn_conv_p = jnp.stack(ffn_p)[:, :, SUBLANES - 1::SUBLANES, :]

    bs, ls, _ = x_sample.shape
    lpad = max(ls, SAMPLE_PAD_LEN)
    s_rows = bs * lpad
    sample = _Group(bs, lpad, lpad, ls, gs=8, ssd_cps=1, mlstm_cps=1, proj_tm=s_rows,
                    merge_tm=512,
                    ffn=dict(groups=1, tm=256, seq_len=lpad), lanes_ssd=True)
    assert lpad == ls
    lane_b = lambda a: jnp.broadcast_to(a.astype(f32)[..., None], a.shape + (bs,))
    w.update(cw_b=lane_b(ssd_conv_w), cb_b=lane_b(ssd_conv_b), dtb_b=lane_b(ssd_dt_bias), alog_b=lane_b(ssd_a_log),
             dexp_b=lane_b(ssd_d), nw_b=lane_b(ssd_norm_w))
    s_state = dict(
        cs_t=jnp.transpose(state_ssd_conv, (0, 3, 2, 1)).reshape(DEPTH, D_MODEL + SSD_BC, keep_ssd * bs),
        h_lanes=jnp.transpose(state_ssd, (0, 2, 3, 4, 1)),
        c=state_mlstm_c, n=state_mlstm_n,
        m=jnp.pad(state_mlstm_m, ((0, 0), (0, 0), (0, SMALL - MLSTM_HEADS)))[:, :, None, :],
        ffn=state_ffn_conv.reshape(DEPTH, bs * keep, 2 * D_FF),
    )
    xs = jnp.pad(x_sample, ((0, 0), (0, lpad - ls), (0, 0))).reshape(s_rows, D_MODEL)
    y_s, (cs_t, h_lanes), mlstm_s, ffn_s = _trunk(sample, xs, s_state, w)
    c_s, n_s, m_s = mlstm_s
    st_s = (jnp.transpose(h_lanes, (0, 4, 1, 2, 3)),
            jnp.transpose(cs_t.reshape(DEPTH, D_MODEL + SSD_BC, keep_ssd, bs), (0, 3, 2, 1)),
            c_s, n_s, m_s[:, :, 0, :MLSTM_HEADS])
    ffn_conv_s = jnp.stack([u.reshape(bs, lpad, 2 * D_FF)[:, ls - keep:ls, :] for u in ffn_s])
    y_sample = y_s.reshape(bs, lpad, D_MODEL)[:, :ls, :]

    return (y_p.reshape(bp, lp, D_MODEL), y_sample, st_p[0], st_s[0], st_p[1], st_s[1], st_p[2], st_s[2],
            st_p[3], st_s[3], st_p[4], st_s[4], ffn_conv_p, ffn_conv_s)
```

```python
import functools
import itertools

import jax
import jax.numpy as jnp
import numpy as np
from jax import lax
from jax.experimental import pallas as pl
from jax.experimental.pallas import tpu as pltpu

f32 = jnp.float32
bf16 = jnp.bfloat16

D_MODEL = 1024
DEPTH = 2
SSD_HEADS = 16
SSD_HEAD_DIM = 64
SSD_STATE = 64
SSD_GROUPS = 2
SSD_CONV = 4
SSD_BC = 2 * SSD_GROUPS * SSD_STATE
SSD_HP = SSD_HEADS * SSD_HEAD_DIM
MLSTM_HEADS = 4
MLSTM_HEAD_DIM = 256
CHUNK = 128
D_FF = 2816
FFN_CONV = 3
ALPHA = (2 * DEPTH) ** 0.25
EPS = 1e-5

LANES = 128
SUBLANES = 8
SMALL = LANES
DT_OFF, I_OFF, F_OFF = 0, 16, 20
P32_Z, P32_O, P32_GA, P32_GB, P32_XS = 0, 1, 2, 3, 4
P32_BC_OFF = 5 * D_MODEL
P32_SM_OFF = P32_BC_OFF + SSD_BC
P32_W = P32_SM_OFF + 2 * SMALL
FF_CH = 256
FF_NCH = D_FF // FF_CH
MERGE_PARTS = 2
FF_UP_AHEAD = 2
FF_XP_SLOTS = 4
NEG_BIG = -1e30

NT_DIMS = (((1,), (1,)), ((), ()))
TN_DIMS = (((0,), (0,)), ((), ()))


def _dot(a, b):
    return jnp.dot(a, b, preferred_element_type=f32)


def _split3(x):
    hi = x.astype(bf16)
    r = x - hi.astype(f32)
    mid = r.astype(bf16)
    lo = (r - mid.astype(f32)).astype(bf16)
    return hi, mid, lo


def _dot01_rhs(x, e):
    hi, mid, lo = _split3(x)
    return _dot(hi, e) + _dot(mid, e) + _dot(lo, e)


def _dot01_lhs(t, x):
    hi, mid, lo = _split3(x)
    return _dot(t, hi) + _dot(t, mid) + _dot(t, lo)


def _softplus(x):
    return jnp.maximum(x, 0.0) + jnp.log1p(jnp.exp(-jnp.abs(x)))


def _silu(x):
    return x * jax.nn.sigmoid(x)


def _row_time(i, q, perm):
    if not perm:
        return i
    return (i & (SUBLANES - 1)) * (q // SUBLANES) + lax.shift_right_logical(i, SUBLANES.bit_length() - 1)


def _tri(q, perm=False):
    row = lax.broadcasted_iota(jnp.int32, (q, q), 0)
    col = lax.broadcasted_iota(jnp.int32, (q, q), 1)
    return _row_time(row, q, perm) >= _row_time(col, q, perm)


def _wrap_rows(cur_tail, prev_tail):
    out = []
    for i in range(cur_tail.shape[0] // SUBLANES):
        rows = slice(i * SUBLANES, (i + 1) * SUBLANES)
        first = lax.broadcasted_iota(jnp.int32, (SUBLANES, cur_tail.shape[1]), 0) == 0
        out.append(jnp.where(first, pltpu.roll(prev_tail[rows], 1, axis=0), pltpu.roll(cur_tail[rows], 1, axis=0)))
    return jnp.concatenate(out, axis=0)


def _shift_back(x, wrapped, j):
    n = j * SUBLANES
    return jnp.concatenate([wrapped[wrapped.shape[0] - n:], x[:x.shape[0] - n]], axis=0)


def _valid_rows(q, width, lr, is_last):
    row = lax.broadcasted_iota(jnp.int32, (q, width), 0)
    return row < jnp.where(is_last, lr, q)


def _layer_norm(r, g, b):
    mu = jnp.mean(r, axis=-1, keepdims=True)
    var = jnp.mean(jnp.square(r - mu), axis=-1, keepdims=True)
    return (r - mu) * lax.rsqrt(var + EPS) * g + b


def _layer_spec(layer, *shape, **kw):
    zeros = (0,) * len(shape)
    return pl.BlockSpec((None,) + shape, lambda *_: (layer,) + zeros, **kw)


def _seq_spec(layer, gs, *shape):
    zeros = (0,) * len(shape)
    return pl.BlockSpec((None, gs) + shape, lambda b, c: (layer, b) + zeros)


def _stacked_call(kern, *, name, grid, inputs, in_specs, out_shape, out_specs, stacked, scratch_shapes,
                  dimension_semantics, vmem_limit_bytes=None):
    prev = [(i, a) for i, a in sorted(stacked.items()) if a is not None]
    n_in = len(inputs)

    def body(*refs):
        kern(*refs[:n_in], *refs[n_in + len(prev):])

    return pl.pallas_call(
        body,
        out_shape=out_shape,
        grid=grid,
        in_specs=list(in_specs) + [pl.BlockSpec(memory_space=pl.ANY)] * len(prev),
        out_specs=out_specs,
        scratch_shapes=scratch_shapes,
        input_output_aliases={n_in + k: i for k, (i, _) in enumerate(prev)},
        compiler_params=pltpu.CompilerParams(dimension_semantics=dimension_semantics,
                                             vmem_limit_bytes=vmem_limit_bytes),
        name=name,
    )(*inputs, *[a for _, a in prev])


def _proj_kernel(x_ref, w32_ref, wqkv_ref, o32_ref, oqkv_ref):
    xb = x_ref[...].astype(bf16)
    o32_ref[...] = lax.dot_general(xb, w32_ref[...], NT_DIMS, preferred_element_type=f32)
    oqkv_ref[...] = lax.dot_general(xb, wqkv_ref[...], NT_DIMS, preferred_element_type=f32).astype(oqkv_ref.dtype)


def _proj(x, w32, wqkv, layer, tm):
    m, k = x.shape
    n32, nq = w32.shape[1], wqkv.shape[1]
    once = dict(pipeline_mode=pl.Buffered(1))
    return pl.pallas_call(
        _proj_kernel,
        out_shape=(jax.ShapeDtypeStruct((m, n32), f32), jax.ShapeDtypeStruct((m, nq), bf16)),
        grid=(m // tm,),
        in_specs=[pl.BlockSpec((tm, k), lambda i: (i, 0)),
                  _layer_spec(layer, n32, k, **once), _layer_spec(layer, nq, k, **once)],
        out_specs=(pl.BlockSpec((tm, n32), lambda i: (i, 0)), pl.BlockSpec((tm, nq), lambda i: (i, 0))),
        compiler_params=pltpu.CompilerParams(dimension_semantics=("parallel",),
                                             vmem_limit_bytes=56 * 1024 * 1024),
        name="proj",
    )(x, w32, wqkv)


class _Rows:
    def __init__(self, ref, start, n):
        self.ref, self.start, self.n, self.dtype = ref, start, n, ref.dtype

    def rows(self, off, n):
        return _Rows(self.ref, self.start + off, n)

    def _index(self, idx):
        cols = slice(None) if idx is Ellipsis else idx[1]
        return (slice(self.start, self.start + self.n), cols)

    def __getitem__(self, idx):
        return self.ref[self._index(idx)]

    def __setitem__(self, idx, value):
        self.ref[self._index(idx)] = value


def _per_sequence(seq_fn, refs, n_tile, n_state, n_param, gs, has_state, nc, rows, tile3d=False):
    n_state = n_state if has_state else 0
    tiles, refs = refs[:n_tile], refs[n_tile:]
    state, refs = refs[:n_state], refs[n_state:]
    params, (y_tile, *rest) = refs[:n_param], refs[n_param:]
    phases = []
    for g in range(gs):
        at = lambda group: tuple(r.at[g] for r in group)
        if tile3d:
            seq_rows = lambda group: tuple(_Rows(r.at[g], 0, rows) for r in group)
        else:
            seq_rows = lambda group: tuple(_Rows(r, g * rows, rows) for r in group)
        phases.append(seq_fn(*seq_rows(tiles), *at(state), *params, *seq_rows((y_tile,)), *at(rest)))
    c = pl.program_id(1)

    @pl.when(c == 0)
    def _():
        for init, _, _ in phases:
            init()

    for _ in itertools.zip_longest(*[body() for _, body, _ in phases]):
        pass

    @pl.when(c == nc - 1)
    def _():
        for _, _, final in phases:
            final()


def _ssd_kernel(*refs, q, lr, nc, has_state, gs, cps, perm, tile3d):
    seq = functools.partial(_ssd_seq, q=q, lr=lr, nc=nc, has_state=has_state, cps=cps, perm=perm)
    _per_sequence(seq, refs, 4, 3, 10, gs, has_state, nc, cps * q, tile3d)


def _ssd_seq(*refs, q, lr, nc, has_state, cps, perm):
    z_ref, xs_ref, bc_ref, sm_ref = refs[:4]
    refs = refs[4:]
    if has_state:
        csx_ref, csb_ref, h0_ref = refs[:3]
        refs = refs[3:]
    (cwx_ref, cbx_ref, cwb_ref, cbb_ref, dtb_ref, alog_ref, dexp_ref, nw_ref, e_ref, bd_ref,
     y_ref, ncsx_ref, ncsb_ref, hout_ref, xpx, xpb, ht, yb) = refs
    hdr = SUBLANES
    lo = hdr - (SSD_CONV - 1)
    n2 = SSD_GROUPS * SSD_STATE
    assert lr >= SSD_CONV - 1
    keep = SSD_CONV - 1
    assert not (perm and (has_state or lr != q))
    carried = [(i + 1) * SUBLANES - 1 for i in range(keep)]

    def init():
        if has_state:
            h_t = h0_ref[...].T
            ht[...] = jnp.where(bd_ref[...] > 0.5, jnp.concatenate([h_t, h_t], axis=0), 0.0)
            xpx[lo:hdr, :] = csx_ref[...]
            xpb[lo:hdr, :] = csb_ref[...]
        else:
            ht[...] = jnp.zeros_like(ht)
            rows = slice(0, keep * SUBLANES) if perm else slice(lo, hdr)
            xpx[rows, :] = jnp.zeros((rows.stop - rows.start, SSD_HP), f32)
            xpb[rows, :] = jnp.zeros((rows.stop - rows.start, SSD_BC), f32)

    def final():
        if perm:
            for i, r in enumerate(carried):
                ncsx_ref[i:i + 1, :] = xpx[r:r + 1, :]
                ncsb_ref[i:i + 1, :] = xpb[r:r + 1, :]
        else:
            ncsx_ref[...] = xpx[lo + lr:hdr + lr, :]
            ncsb_ref[...] = xpb[lo + lr:hdr + lr, :]
        h_new = ht[...]
        hout_ref[...] = (h_new[:SSD_STATE, :] + h_new[SSD_STATE:, :]).T

    def body():
        for k in range(cps):
            sub = lambda r: r.rows(k * q, q)
            is_last = (pl.program_id(1) == nc - 1) if k == cps - 1 else False
            yield from _ssd_body(sub(z_ref), sub(xs_ref), sub(bc_ref), sub(sm_ref), cwx_ref, cbx_ref, cwb_ref,
                                 cbb_ref, dtb_ref, alog_ref, dexp_ref, nw_ref, e_ref, bd_ref, sub(y_ref),
                                 xpx, xpb, ht, yb, q=q, lr=lr, is_last=is_last, perm=perm)

    return init, body, final


def _ssd_body(z_ref, xs_ref, bc_ref, sm_ref, cwx_ref, cbx_ref, cwb_ref, cbb_ref, dtb_ref, alog_ref,
              dexp_ref, nw_ref, e_ref, bd_ref, y_ref, xpx, xpb, ht, yb, *, q, lr, is_last, perm):
    hdr = SUBLANES
    lo = hdr - (SSD_CONV - 1)
    n2 = SSD_GROUPS * SSD_STATE
    block_diag = bd_ref[...] > 0.5

    dt = _softplus(sm_ref[...] + dtb_ref[...])
    if lr < q:
        dt = jnp.where(_valid_rows(q, SMALL, lr, is_last), dt, 0.0)
    a = -jnp.exp(alog_ref[...])
    d_a = dt * a
    causal = _tri(q, perm)
    tril = jnp.where(causal, 1.0, 0.0).astype(bf16)
    e = e_ref[...]
    acs = _dot01_lhs(tril, d_a)
    dt_x = _dot01_rhs(dt, e)
    yield

    if perm:
        keep_rows = (SSD_CONV - 1) * SUBLANES

        def conv(xp, x_ref, w_ref, b_ref):
            w = w_ref[...]
            x = x_ref[...]
            wrapped = _wrap_rows(x[q - keep_rows:, :], xp[0:keep_rows, :])
            acc = _shift_back(x, wrapped, SSD_CONV - 1) * w[0:1, :]
            for j in range(1, SSD_CONV - 1):
                acc = acc + _shift_back(x, wrapped, SSD_CONV - 1 - j) * w[j:j + 1, :]
            acc = acc + x * w[SSD_CONV - 1:SSD_CONV, :]
            xp[0:keep_rows, :] = x[q - keep_rows:, :]
            return acc + b_ref[...]
    else:
        xpx[hdr:hdr + q, :] = xs_ref[...]
        xpb[hdr:hdr + q, :] = bc_ref[...]

        def conv(xp, x_ref, w_ref, b_ref):
            w = w_ref[...]
            acc = xp[lo:lo + q, :] * w[0:1, :]
            for j in range(1, SSD_CONV):
                acc = acc + xp[lo + j:lo + j + q, :] * w[j:j + 1, :]
            return acc + b_ref[...]

    cb = conv(xpb, bc_ref, cwb_ref, cbb_ref)
    bcv = _silu(cb)
    bm = bcv[:, :n2].astype(bf16)
    cm = bcv[:, n2:]
    lane_g0 = lax.broadcasted_iota(jnp.int32, (q, n2), 1) < SSD_STATE
    acs_t = acs.T
    acs_x = _dot01_rhs(acs, e)
    yield
    cbms = [lax.dot_general(jnp.where(lane_g0 if g == 0 else jnp.logical_not(lane_g0), cm, 0.0).astype(bf16),
                            bm, NT_DIMS, preferred_element_type=f32) for g in range(SSD_GROUPS)]
    h_prev = ht[...]
    y_off = _dot(cm.astype(bf16), h_prev.astype(bf16))
    cx = conv(xpx, xs_ref, cwx_ref, cbx_ref)
    if not perm:
        tail_x = xpx[lo + q:hdr + q, :]
        tail_b = xpb[lo + q:hdr + q, :]
        xpx[lo:hdr, :] = tail_x
        xpb[lo:hdr, :] = tail_b
    yield
    xs = _silu(cx)
    last_x = acs_x[q - 1:q, :]
    xdt = xs * dt_x
    xdt_b = xdt.astype(bf16)
    yield
    lane_lo = lax.broadcasted_iota(jnp.int32, (q, LANES), 1) < SSD_HEAD_DIM
    heads_per_group = SSD_HEADS // SSD_GROUPS
    decays = [jnp.exp(jnp.where(causal, acs[:, hh:hh + 1] - acs_t[hh:hh + 1, :], -jnp.inf))
              for hh in range(SSD_HEADS)]
    yield
    weights = [(cbms[hh // heads_per_group] * decays[hh]).astype(bf16) for hh in range(SSD_HEADS)]
    xdtw = (xdt * jnp.exp(last_x - acs_x)).astype(bf16)
    yield
    ys = [_dot(weights[hh], xdt_b[:, (hh // 2) * LANES:(hh // 2 + 1) * LANES]) for hh in range(SSD_HEADS)]
    upd = lax.dot_general(bm, xdtw, TN_DIMS, preferred_element_type=f32)
    yield
    for p in range(SSD_HEADS // 2):
        yb[:, p * LANES:(p + 1) * LANES] = jnp.where(lane_lo, ys[2 * p], ys[2 * p + 1])
    ht[...] = jnp.exp(last_x) * h_prev + jnp.where(block_diag, upd, 0.0)
    yield
    y = yb[...] + y_off * jnp.exp(acs_x) + dexp_ref[...] * xs
    y = y * _silu(z_ref[...])
    yield
    y = y * lax.rsqrt(jnp.mean(jnp.square(y), axis=-1, keepdims=True) + EPS) * nw_ref[...]
    y_ref[...] = y.astype(y_ref.dtype)


PT_ROWS = 5 * 512
PT_XS, PT_BC, PT_SM = D_MODEL, 2 * D_MODEL, 2 * D_MODEL + SSD_BC


def _proj_t_kernel(x_ref, w_ref, o_ref, xb):
    @pl.when(pl.program_id(0) == 0)
    def _():
        xb[...] = x_ref[...].astype(bf16)

    o_ref[...] = lax.dot_general(w_ref[...], xb[...], NT_DIMS, preferred_element_type=f32)


def _proj_t(x_tm, w32, layer):
    m, k = x_tm.shape
    tn = 512
    xs_blk = P32_XS * D_MODEL // tn
    return pl.pallas_call(
        _proj_t_kernel,
        out_shape=jax.ShapeDtypeStruct((PT_ROWS, m), f32),
        grid=(PT_ROWS // tn,),
        in_specs=[pl.BlockSpec((m, k), lambda j: (0, 0)),
                  pl.BlockSpec((None, tn, k), lambda j: (layer, jnp.where(j < D_MODEL // tn, j, j + xs_blk - D_MODEL // tn), 0))],
        out_specs=pl.BlockSpec((tn, m), lambda j: (j, 0)),
        scratch_shapes=[pltpu.VMEM((m, k), bf16)],
        compiler_params=pltpu.CompilerParams(dimension_semantics=("arbitrary",)),
        name="proj_t",
    )(x_tm, w32)


def _ssd_lanes_kernel(pt_ref, cst_ref, h0_ref, cw_ref, cb_ref, dtb_ref, alog_ref, dexp_ref, nw_ref,
                      y_ref, ncs_ref, hout_ref, xc, dts, decs, ysc, *, steps, batch):
    hd = pl.program_id(0)
    n_ch = D_MODEL + SSD_BC
    keep = SSD_CONV - 1
    lanes = lambda t: slice(t * batch, (t + 1) * batch)

    @pl.when(hd == 0)
    def _():
        for t in range(steps):
            acc = None
            for j in range(SSD_CONV):
                i = t + j
                src = cst_ref[:, lanes(i)] if i < keep else pt_ref[PT_XS:PT_XS + n_ch, lanes(i - keep)]
                term = src * cw_ref[j]
                acc = term if acc is None else acc + term
            xc[:, lanes(t)] = _silu(acc + cb_ref[...])
        ncs_ref[...] = pt_ref[PT_XS:PT_XS + n_ch, (steps - keep) * batch:steps * batch]
        dt = _softplus(pt_ref[PT_SM:PT_SM + SSD_HEADS, :] + jnp.concatenate([dtb_ref[...]] * steps, axis=1))
        dts[...] = dt
        decs[...] = jnp.exp(dt * jnp.concatenate([-jnp.exp(alog_ref[...])] * steps, axis=1))

    grp_row = (hd // (SSD_HEADS // SSD_GROUPS)) * SSD_STATE
    xh = xc[pl.ds(pl.multiple_of(hd * SSD_HEAD_DIM, SSD_HEAD_DIM), SSD_HEAD_DIM), :]
    bh = xc[pl.ds(pl.multiple_of(D_MODEL + grp_row, SSD_STATE), SSD_STATE), :]
    ch = xc[pl.ds(pl.multiple_of(D_MODEL + SSD_GROUPS * SSD_STATE + grp_row, SSD_STATE), SSD_STATE), :]
    dth = dts[pl.ds(hd, 1), :]
    dech = decs[pl.ds(hd, 1), :]
    d_skip = dexp_ref[pl.ds(hd, 1), :]
    y_rows = [[] for _ in range(steps)]
    for p in range(SSD_HEAD_DIM):
        h = h0_ref[p]
        for t in range(steps):
            x_row = xh[p:p + 1, lanes(t)]
            h = dech[:, lanes(t)] * h + (x_row * dth[:, lanes(t)]) * bh[:, lanes(t)]
            y_rows[t].append(jnp.sum(ch[:, lanes(t)] * h, axis=0, keepdims=True) + d_skip * x_row)
        hout_ref[p] = h
    rows = pl.ds(pl.multiple_of(hd * SSD_HEAD_DIM, SSD_HEAD_DIM), SSD_HEAD_DIM)
    for t in range(steps):
        ysc[rows, lanes(t)] = jnp.concatenate(y_rows[t], axis=0)

    @pl.when(hd == SSD_HEADS - 1)
    def _():
        y = ysc[...] * _silu(pt_ref[0:D_MODEL, :])
        y = y * lax.rsqrt(jnp.mean(jnp.square(y), axis=0, keepdims=True) + EPS)
        y_ref[...] = (y * jnp.concatenate([nw_ref[...]] * steps, axis=1)).astype(y_ref.dtype)


def _ssd_lanes(pt, cst, h0, w, layer, prev, *, steps, batch):
    n_ch = D_MODEL + SSD_BC
    tb = steps * batch
    keep = SSD_CONV - 1
    assert steps >= keep and batch % LANES == 0
    kern = functools.partial(_ssd_lanes_kernel, steps=steps, batch=batch)
    full = lambda *shape: pl.BlockSpec(shape, lambda hd: (0,) * len(shape))
    hblock = pl.BlockSpec((None, None, SSD_HEAD_DIM, SSD_STATE, batch), lambda hd: (layer, hd, 0, 0, 0))
    return _stacked_call(
        kern, name="ssd_lanes", grid=(SSD_HEADS,),
        inputs=[pt, cst, h0, w["cw_b"], w["cb_b"], w["dtb_b"], w["alog_b"], w["dexp_b"], w["nw_b"]],
        in_specs=[full(PT_ROWS, tb), _layer_spec(layer, n_ch, keep * batch), hblock,
                  _layer_spec(layer, SSD_CONV, n_ch, batch), _layer_spec(layer, n_ch, batch),
                  _layer_spec(layer, SSD_HEADS, batch), _layer_spec(layer, SSD_HEADS, batch),
                  _layer_spec(layer, SSD_HEADS, batch), _layer_spec(layer, D_MODEL, batch)],
        out_shape=(jax.ShapeDtypeStruct((D_MODEL, tb), bf16),
                   jax.ShapeDtypeStruct((DEPTH, n_ch, keep * batch), f32),
                   jax.ShapeDtypeStruct((DEPTH, SSD_HEADS, SSD_HEAD_DIM, SSD_STATE, batch), f32)),
        out_specs=(full(D_MODEL, tb), _layer_spec(layer, n_ch, keep * batch), hblock),
        stacked={1: prev and prev[0], 2: prev and prev[1]},
        scratch_shapes=[pltpu.VMEM((n_ch, tb), f32), pltpu.VMEM((SSD_HEADS, tb), f32),
                        pltpu.VMEM((SSD_HEADS, tb), f32), pltpu.VMEM((D_MODEL, tb), f32)],
        dimension_semantics=("arbitrary",))


def _mlstm_kernel(*refs, q, lr, nc, has_state, gs, cps, perm, tile3d):
    seq = functools.partial(_mlstm_seq, q=q, lr=lr, nc=nc, has_state=has_state, cps=cps, perm=perm)
    _per_sequence(seq, refs, 5, 3, 2, gs, has_state, nc, cps * q, tile3d)


def _mlstm_seq(*refs, q, lr, nc, has_state, cps, perm):
    q_ref, k_ref, v_ref, o_ref, sm_ref = refs[:5]
    refs = refs[5:]
    if has_state:
        c0_ref, n0_ref, m0_ref = refs[:3]
        refs = refs[3:]
    gb_ref, nw_ref, h_ref, cout_ref, nout_ref, mout_ref, cs, ns, ms = refs
    direct = has_state and nc == 1 and cps == 1

    def init():
        if direct:
            return
        if has_state:
            cs[...] = c0_ref[...]
            ns[...] = n0_ref[...]
            ms[...] = m0_ref[...]
        else:
            cs[...] = jnp.zeros_like(cs)
            ns[...] = jnp.zeros_like(ns)
            ms[...] = jnp.zeros_like(ms)

    def final():
        if direct:
            return
        cout_ref[...] = cs[...]
        nout_ref[...] = ns[...]
        mout_ref[...] = ms[...]

    def body():
        chunks = []
        for k in range(cps):
            sub = lambda r, k=k: r.rows(k * q, q)
            is_last = (pl.program_id(1) == nc - 1) if k == cps - 1 else False
            src = (c0_ref, n0_ref, m0_ref) if direct else (cs, ns, ms)
            dst = (cout_ref, nout_ref, mout_ref) if direct else (cs, ns, ms)
            chunks.append(_mlstm_body(sub(q_ref), sub(k_ref), sub(v_ref), sub(o_ref), sub(sm_ref), gb_ref, nw_ref,
                                      sub(h_ref), src, dst, q=q, lr=lr, is_last=is_last, perm=perm))
        yield from _staggered(chunks, MLSTM_STATE_STAGES)

    return init, body, final


MLSTM_STATE_STAGES = 8


def _staggered(gens, skew):
    done = [False] * len(gens)
    t = 0
    while not all(done):
        for i, g in enumerate(gens):
            if done[i] or t < i * skew:
                continue
            try:
                next(g)
            except StopIteration:
                done[i] = True
        t += 1
        yield


def _mlstm_body(q_ref, k_ref, v_ref, o_ref, sm_ref, gb_ref, nw_ref, h_ref, src, dst, *, q, lr, is_last, perm):
    c_src, n_src, m_src = src
    c_dst, n_dst, m_dst = dst
    sm = sm_ref[...] + gb_ref[...]
    logf = -_softplus(-sm)
    ipre = sm
    if lr < q:
        valid = _valid_rows(q, SMALL, lr, is_last)
        logf = jnp.where(valid, logf, 0.0)
        ipre = jnp.where(valid, ipre, NEG_BIG)
    causal = _tri(q, perm)
    tril = jnp.where(causal, 1.0, 0.0).astype(bf16)
    yield
    bcum = _dot01_lhs(tril, logf)
    ipre_t = ipre.T
    yield
    bcum_t = bcum.T
    lane = lax.broadcasted_iota(jnp.int32, (1, SMALL), 1)
    k_scale = MLSTM_HEAD_DIM ** -0.5

    heads = range(MLSTM_HEADS)
    sls = [slice(h * MLSTM_HEAD_DIM, (h + 1) * MLSTM_HEAD_DIM) for h in heads]
    q_all, k_all, v_all, o_all = q_ref[...], k_ref[...], v_ref[...], o_ref[...]
    qs = [q_all[:, sl] for sl in sls]
    ks = [k_all[:, sl] * k_scale for sl in sls]
    vs = [v_all[:, sl] for sl in sls]
    b_cols = [bcum[:, F_OFF + h:F_OFF + h + 1] for h in heads]
    i_cols = [ipre[:, I_OFF + h:I_OFF + h + 1] for h in heads]
    dmats = [jnp.where(causal, b_cols[h] - bcum_t[F_OFF + h:F_OFF + h + 1, :] + ipre_t[I_OFF + h:I_OFF + h + 1, :],
                       -jnp.inf) for h in heads]
    yield
    qk = [lax.dot_general(qs[h], ks[h], NT_DIMS, preferred_element_type=f32) for h in heads]
    d_max = [jnp.max(dmats[h], axis=-1, keepdims=True) for h in heads]
    yield
    n_all = n_src[...]
    m_all = m_src[...]
    m_new = m_all
    cs_in = [c_src[h] for h in heads]
    m_prevs = [m_all[:, h:h + 1] for h in heads]
    qc = [_dot(qs[h], cs_in[h].astype(bf16)) for h in heads]
    inters = [b_cols[h] + m_prevs[h] for h in heads]
    m_ts = [jnp.maximum(inters[h], d_max[h]) for h in heads]
    yield
    w_inters = [jnp.exp(inters[h] - m_ts[h]) for h in heads]
    ss = [qk[h] * jnp.exp(dmats[h] - m_ts[h]) for h in heads]
    yield
    sv = [_dot(ss[h].astype(bf16), vs[h]) for h in heads]
    m_ends = [m_ts[h][q - 1:q, :] for h in heads]
    b_lasts = [b_cols[h][q - 1:q, :] for h in heads]
    kws = [ks[h].astype(f32) * jnp.exp(b_lasts[h] - b_cols[h] + i_cols[h] - m_ends[h]) for h in heads]
    yield
    kv = [lax.dot_general(kws[h].astype(bf16), vs[h], TN_DIMS, preferred_element_type=f32) for h in heads]
    qns = [jnp.sum(qs[h].astype(f32) * n_all[h:h + 1, :], axis=-1, keepdims=True) for h in heads]
    yield
    dens = [jnp.sum(ss[h], axis=-1, keepdims=True) + w_inters[h] * qns[h] for h in heads]
    yield
    hvs = [(sv[h] + w_inters[h] * qc[h]) / jnp.maximum(jnp.abs(dens[h]), jnp.exp(-m_ts[h])) for h in heads]
    yield
    rms = [lax.rsqrt(jnp.mean(jnp.square(hvs[h]), axis=-1, keepdims=True) + EPS) for h in heads]
    yield
    h_new = [(hvs[h] * rms[h] * nw_ref[:, sls[h]] * jax.nn.sigmoid(o_all[:, sls[h]])).astype(h_ref.dtype)
             for h in heads]
    w_cs = [jnp.exp(b_lasts[h] + m_prevs[h] - m_ends[h]) for h in heads]
    yield
    for h in heads:
        c_dst[h] = w_cs[h] * cs_in[h] + kv[h]
        m_new = jnp.where(lane == h, m_ends[h], m_new)
    h_ref[...] = jnp.concatenate(h_new, axis=1)
    n_dst[...] = jnp.concatenate(
        [w_cs[h] * n_all[h:h + 1, :] + jnp.sum(kws[h], axis=0, keepdims=True) for h in heads], axis=0)
    m_dst[...] = m_new


def _merge_kernel(ys_ref, hm_ref, ga_ref, gb_ref, x_ref, wa_ref, wb_ref, wo_ref, g_ref, b_ref, o_ref, *, parts):
    tm = x_ref.shape[0]
    rows = [pl.ds(i * (tm // parts), tm // parts) for i in range(parts)]
    br = [(_dot(ys_ref[r, :], wa_ref[...]), _dot(hm_ref[r, :], wb_ref[...])) for r in rows]
    merged = [(jax.nn.sigmoid(ga_ref[r, :]) * a + jax.nn.sigmoid(gb_ref[r, :]) * b).astype(bf16)
              for r, (a, b) in zip(rows, br)]
    mix = [_dot(m, wo_ref[...]) for m in merged]
    for r, m in zip(rows, mix):
        o_ref[r, :] = _layer_norm(ALPHA * x_ref[r, :] + m, g_ref[...], b_ref[...])


def _merge(ys, hm, p32, x, w, layer, tm, parts=MERGE_PARTS):
    m = x.shape[0]
    row = lambda blk: pl.BlockSpec((tm, D_MODEL), lambda i: (i, blk))
    return pl.pallas_call(
        functools.partial(_merge_kernel, parts=parts),
        out_shape=jax.ShapeDtypeStruct((m, D_MODEL), f32),
        grid=(m // tm,),
        in_specs=[row(0), row(0), row(P32_GA), row(P32_GB), row(0),
                  _layer_spec(layer, D_MODEL, D_MODEL), _layer_spec(layer, D_MODEL, D_MODEL),
                  _layer_spec(layer, D_MODEL, D_MODEL), _layer_spec(layer, 1, D_MODEL),
                  _layer_spec(layer, 1, D_MODEL)],
        out_specs=row(0),
        compiler_params=pltpu.CompilerParams(dimension_semantics=("parallel",)),
        name="merge",
    )(ys, hm, p32, p32, x, w["wa"], w["wb"], w["wo"], w["ln1_g"], w["ln1_b"])


def _ffn_kernel(*refs, tm, seq_len, perm_q, ahead_n):
    multi = seq_len > 0
    keep_rows = (FFN_CONV - 1) * SUBLANES
    if multi:
        x_ref, st_ref, wup_ref, cw_ref, cb_ref, wdn_ref, g_ref, b_ref, o_ref, sout_ref, xp = refs
    else:
        x_ref, wup_ref, cw_ref, cb_ref, wdn_ref, g_ref, b_ref, o_ref, sout_ref, xp, carry = refs
        @pl.when(pl.program_id(1) == 0)
        def _():
            carry[...] = jnp.zeros_like(carry)

    hdr = SUBLANES
    x = x_ref[...]
    xb = x.astype(bf16)
    if multi:
        assert seq_len & (seq_len - 1) == 0
        nseq = tm // seq_len
        t = lax.broadcasted_iota(jnp.int32, (tm, FF_CH), 0) & (seq_len - 1)
        row = lax.broadcasted_iota(jnp.int32, (tm, 2 * nseq), 0)
        col = lax.broadcasted_iota(jnp.int32, (tm, 2 * nseq), 1)
        t_sel = row & (seq_len - 1)
        seq0 = lax.shift_right_logical(row - t_sel, (seq_len // 2).bit_length() - 1)
        sel_p2 = jnp.where(col == seq0 + t_sel, jnp.where(t_sel < 2, 1.0, 0.0), 0.0).astype(bf16)
        sel_p1 = jnp.where(col == seq0 + 1, jnp.where(t_sel == 0, 1.0, 0.0), 0.0).astype(bf16)
        xp[:, 0:hdr, :] = jnp.zeros((FF_XP_SLOTS, hdr, FF_CH), f32)

    def cols_of(c, part):
        return slice(part * D_FF + c * FF_CH, part * D_FF + (c + 1) * FF_CH)

    def up(c):
        return [_dot(xb, wup_ref[:, cols_of(c, part)]) for part in range(2)]

    def conv_act(c, us):
        halves = []
        for part, u in enumerate(us):
            cols = cols_of(c, part)
            slot = (2 * c + part) % FF_XP_SLOTS
            if perm_q:
                prev = carry[:, cols]
                p1, p2 = [], []
                for kk in range(tm // perm_q):
                    uc = u[kk * perm_q:(kk + 1) * perm_q, :]
                    wrapped = _wrap_rows(uc[perm_q - keep_rows:, :], prev)
                    p1.append(_shift_back(uc, wrapped, 1))
                    p2.append(_shift_back(uc, wrapped, 2))
                    prev = uc[perm_q - keep_rows:, :]
                carry[:, cols] = prev
                p1, p2 = jnp.concatenate(p1, axis=0), jnp.concatenate(p2, axis=0)
                w = cw_ref[:, cols]
                halves.append(p2 * w[0:1, :] + p1 * w[1:2, :] + u * w[2:3, :] + cb_ref[:, cols])
                continue
            xp[slot, hdr:hdr + tm, :] = u
            if multi:
                sout_ref[:, cols] = u
                st = st_ref[:, cols]
                p1 = jnp.where(t == 0, _dot01_lhs(sel_p1, st), xp[slot, hdr - 1:hdr - 1 + tm, :])
                p2 = jnp.where(t < 2, _dot01_lhs(sel_p2, st), xp[slot, hdr - 2:hdr - 2 + tm, :])
            else:
                xp[slot, 0:hdr, :] = carry[:, cols]
                p1 = xp[slot, hdr - 1:hdr - 1 + tm, :]
                p2 = xp[slot, hdr - 2:hdr - 2 + tm, :]
                carry[:, cols] = u[tm - hdr:tm, :]
            w = cw_ref[:, cols]
            halves.append(p2 * w[0:1, :] + p1 * w[1:2, :] + u * w[2:3, :] + cb_ref[:, cols])
        return (_silu(halves[0]) * halves[1]).astype(bf16)

    acc = None
    ahead = [up(c) for c in range(min(ahead_n, FF_NCH))]
    pending = None
    for c in range(FF_NCH):
        if c + ahead_n < FF_NCH:
            ahead.append(up(c + ahead_n))
        if pending is not None:
            d = _dot(pending, wdn_ref[(c - 1) * FF_CH:c * FF_CH, :])
            acc = d if acc is None else acc + d
        pending = conv_act(c, ahead.pop(0))
    acc = acc + _dot(pending, wdn_ref[(FF_NCH - 1) * FF_CH:FF_NCH * FF_CH, :])

    if not multi:
        sout_ref[...] = carry[...]
    o_ref[...] = _layer_norm(ALPHA * x + acc, g_ref[...], b_ref[...])


def _ffn(x, st, w, layer, *, groups, tm, seq_len, perm_q=0, ahead=FF_UP_AHEAD):
    m = x.shape[0]
    tiles = m // (groups * tm)
    multi = seq_len > 0
    kern = functools.partial(_ffn_kernel, tm=tm, seq_len=seq_len, perm_q=perm_q, ahead_n=ahead)
    carry_rows = (FFN_CONV - 1) * SUBLANES if perm_q else SUBLANES
    once = dict(pipeline_mode=pl.Buffered(1))
    x_spec = pl.BlockSpec((tm, D_MODEL), lambda s, j: (s * tiles + j, 0))
    w_specs = [_layer_spec(layer, D_MODEL, 2 * D_FF, **once), _layer_spec(layer, FFN_CONV, 2 * D_FF, **once),
               _layer_spec(layer, 1, 2 * D_FF, **once), _layer_spec(layer, D_FF, D_MODEL, **once),
               _layer_spec(layer, 1, D_MODEL, **once), _layer_spec(layer, 1, D_MODEL, **once)]
    w_args = (w["wup"], w["fcw"], w["fcb"], w["wdn"], w["ln2_g"], w["ln2_b"])
    xp = pltpu.VMEM((FF_XP_SLOTS, SUBLANES + (0 if perm_q else tm), FF_CH), f32)
    if multi:
        nst = 2 * (tm // seq_len)
        inputs = (x, st) + w_args
        in_specs = [x_spec, pl.BlockSpec((None, nst, 2 * D_FF), lambda s, j: (layer, s * tiles + j, 0))] + w_specs
        sout_shape = jax.ShapeDtypeStruct((m, 2 * D_FF), f32)
        sout_spec = pl.BlockSpec((tm, 2 * D_FF), lambda s, j: (s * tiles + j, 0))
        scratch = [xp]
    else:
        inputs = (x,) + w_args
        in_specs = [x_spec] + w_specs
        sout_shape = jax.ShapeDtypeStruct((groups, carry_rows, 2 * D_FF), f32)
        sout_spec = pl.BlockSpec((None, carry_rows, 2 * D_FF), lambda s, j: (s, 0, 0))
        scratch = [xp, pltpu.VMEM((carry_rows, 2 * D_FF), f32)]
    return pl.pallas_call(
        kern,
        out_shape=(jax.ShapeDtypeStruct((m, D_MODEL), f32), sout_shape),
        grid=(groups, tiles),
        in_specs=in_specs,
        out_specs=(x_spec, sout_spec),
        scratch_shapes=scratch,
        compiler_params=pltpu.CompilerParams(dimension_semantics=("parallel", "arbitrary"),
                                             vmem_limit_bytes=56 * 1024 * 1024),
        name="ffn",
    )(*inputs)


def _pad_lanes(v, off, width=SMALL):
    out = jnp.zeros((v.shape[0], 1, width), f32)
    return out.at[:, 0, off:off + v.shape[1]].set(v.astype(f32))


def _prep_weights(w_in, ssd_conv_w, ssd_conv_b, ssd_dt_bias, ssd_a_log, ssd_d, ssd_norm_w, mlstm_gate_b,
                  mlstm_norm_w, w_branch_a, w_branch_b, w_out, ln1_g, ln1_b, ffn_w_up, ffn_conv_w,
                  ffn_conv_b, ffn_w_down, ln2_g, ln2_b):
    d = D_MODEL
    o_z, o_xbc, o_dt = 0, d, d + d + SSD_BC
    o_q = o_dt + SSD_HEADS
    o_if = o_q + 3 * d
    o_o = o_if + 2 * MLSTM_HEADS
    o_g = o_o + d
    w_t = jnp.swapaxes(w_in, 1, 2)
    cols = lambda a, n: w_t[:, a:a + n, :]
    zeros = lambda n: jnp.zeros((DEPTH, n, d), w_in.dtype)
    w32 = jnp.concatenate([cols(o_z, d), cols(o_o, d), cols(o_g, 2 * d), cols(o_xbc, d + SSD_BC),
                           cols(o_dt, SSD_HEADS), cols(o_if, 2 * MLSTM_HEADS),
                           zeros(P32_W - P32_SM_OFF - SSD_HEADS - 2 * MLSTM_HEADS)], axis=1).astype(bf16)
    e = (np.arange(SSD_HP)[None, :] // SSD_HEAD_DIM == np.arange(LANES)[:, None])
    bd = ((np.arange(SSD_GROUPS * SSD_STATE)[:, None] < SSD_STATE)
          == (np.arange(SSD_HP)[None, :] < SSD_HP // SSD_GROUPS))
    row = lambda a: a[:, None, :]
    return dict(
        w32=w32, wqkv=cols(o_q, 3 * d).astype(bf16),
        cwx=ssd_conv_w[:, :, :d], cbx=row(ssd_conv_b[:, :d]),
        cwb=ssd_conv_w[:, :, d:], cbb=row(ssd_conv_b[:, d:]),
        dtb=_pad_lanes(ssd_dt_bias, DT_OFF), alog=_pad_lanes(ssd_a_log, DT_OFF),
        dexp=row(jnp.repeat(ssd_d.astype(f32), SSD_HEAD_DIM, axis=1)), ssd_nw=row(ssd_norm_w),
        e=jnp.asarray(e, bf16), bd=jnp.asarray(bd, f32),
        gate_b=_pad_lanes(mlstm_gate_b, I_OFF), mlstm_nw=row(mlstm_norm_w),
        wa=w_branch_a.astype(bf16), wb=w_branch_b.astype(bf16), wo=w_out.astype(bf16),
        ln1_g=row(ln1_g), ln1_b=row(ln1_b),
        wup=ffn_w_up.astype(bf16), fcw=ffn_conv_w, fcb=row(ffn_conv_b), wdn=ffn_w_down.astype(bf16),
        ln2_g=row(ln2_g), ln2_b=row(ln2_b),
    )


class _Group:
    def __init__(self, batch, length, q, lr, gs, ssd_cps, mlstm_cps, proj_tm, merge_tm, ffn,
                 perm=False, lanes_ssd=False, mlstm_gs=None, ssd_gs=None, merge_parts=MERGE_PARTS):
        self.mlstm_gs = gs if mlstm_gs is None else mlstm_gs
        self.ssd_gs = gs if ssd_gs is None else ssd_gs
        self.merge_parts = merge_parts
        self.perm = perm
        self.lanes_ssd = lanes_ssd
        self.batch, self.length, self.q, self.lr, self.gs = batch, length, q, lr, gs
        self.ssd_cps, self.mlstm_cps = ssd_cps, mlstm_cps
        self.rows = batch * length
        self.proj_tm, self.merge_tm, self.ffn = proj_tm, merge_tm, ffn

    def cfg(self, name, layer):
        v = getattr(self, name)
        return v[layer] if isinstance(v, tuple) else v

    def tiling(self, cps, gs=None):
        rows = cps * self.q
        gs = self.gs if gs is None else gs
        steps = self.length // rows
        if gs == 1 or steps == 1:
            spec = lambda width, blk: pl.BlockSpec((gs * rows, width), lambda b, c: (b * steps + c, blk))
            return steps, spec, (lambda a: a), False
        spec = lambda width, blk: pl.BlockSpec((gs, rows, width), lambda b, c: (b, c, blk))
        return steps, spec, (lambda a: a.reshape(self.batch, self.length, a.shape[-1])), True


def _ssd(grp, p32, state, w, layer, prev):
    q, b, gs = grp.q, grp.batch, grp.cfg("ssd_gs", layer)
    has_state = state is not None
    cps = grp.cfg("ssd_cps", layer)
    steps, tile, view, tile3d = grp.tiling(cps, gs)
    kern = functools.partial(_ssd_kernel, q=q, lr=grp.lr, nc=steps, has_state=has_state, gs=gs, cps=cps,
                             perm=grp.perm, tile3d=tile3d)
    p32 = view(p32)
    inputs = [p32, p32, p32, p32]
    in_specs = [tile(D_MODEL, P32_Z), tile(D_MODEL, P32_XS),
                tile(SSD_BC, P32_BC_OFF // SSD_BC), tile(SMALL, P32_SM_OFF // SMALL)]
    if has_state:
        inputs += [state["csx"], state["csb"], state["h"]]
        in_specs += [_seq_spec(layer, gs, SSD_CONV - 1, D_MODEL), _seq_spec(layer, gs, SSD_CONV - 1, SSD_BC),
                     _seq_spec(layer, gs, SSD_HP, SSD_STATE)]
    inputs += [w["cwx"], w["cbx"], w["cwb"], w["cbb"], w["dtb"], w["alog"], w["dexp"], w["ssd_nw"], w["e"],
               w["bd"]]
    const = lambda *shape: pl.BlockSpec(shape, lambda b, c: (0,) * len(shape))
    in_specs += [_layer_spec(layer, SSD_CONV, D_MODEL), _layer_spec(layer, 1, D_MODEL),
                 _layer_spec(layer, SSD_CONV, SSD_BC), _layer_spec(layer, 1, SSD_BC),
                 _layer_spec(layer, 1, SMALL), _layer_spec(layer, 1, SMALL), _layer_spec(layer, 1, D_MODEL),
                 _layer_spec(layer, 1, D_MODEL), const(LANES, SSD_HP), const(SSD_GROUPS * SSD_STATE, SSD_HP)]
    ys, *new = _stacked_call(
        kern, name="ssd", grid=(b // gs, steps), inputs=inputs, in_specs=in_specs,
        out_shape=(jax.ShapeDtypeStruct((b, grp.length, D_MODEL) if tile3d else (grp.rows, D_MODEL), bf16),
                   jax.ShapeDtypeStruct((DEPTH, b, SSD_CONV - 1, D_MODEL), f32),
                   jax.ShapeDtypeStruct((DEPTH, b, SSD_CONV - 1, SSD_BC), f32),
                   jax.ShapeDtypeStruct((DEPTH, b, SSD_HP, SSD_STATE), f32)),
        out_specs=(tile(D_MODEL, 0), _seq_spec(layer, gs, SSD_CONV - 1, D_MODEL),
                   _seq_spec(layer, gs, SSD_CONV - 1, SSD_BC), _seq_spec(layer, gs, SSD_HP, SSD_STATE)),
        stacked={1: prev and prev[0], 2: prev and prev[1], 3: prev and prev[2]},
        scratch_shapes=[pltpu.VMEM((gs, SUBLANES + q, D_MODEL), f32), pltpu.VMEM((gs, SUBLANES + q, SSD_BC), f32),
                        pltpu.VMEM((gs, SSD_GROUPS * SSD_STATE, SSD_HP), f32), pltpu.VMEM((gs, q, D_MODEL), f32)],
        dimension_semantics=("parallel", "arbitrary"))
    return (ys.reshape(grp.rows, D_MODEL), *new)


def _mlstm(grp, qkv, p32, state, w, layer, prev):
    q, b, gs = grp.q, grp.batch, grp.mlstm_gs
    has_state = state is not None
    cps = grp.cfg("mlstm_cps", layer)
    steps, tile, view, tile3d = grp.tiling(cps, gs)
    kern = functools.partial(_mlstm_kernel, q=q, lr=grp.lr, nc=steps, has_state=has_state, gs=gs, cps=cps,
                             perm=grp.perm, tile3d=tile3d)
    hd = MLSTM_HEAD_DIM
    carried = not (has_state and steps == 1 and cps == 1)
    qkv, p32 = view(qkv), view(p32)
    inputs = [qkv, qkv, qkv, p32, p32]
    in_specs = [tile(D_MODEL, 0), tile(D_MODEL, 1), tile(D_MODEL, 2), tile(D_MODEL, P32_O),
                tile(SMALL, P32_SM_OFF // SMALL)]
    if has_state:
        inputs += [state["c"], state["n"], state["m"]]
        in_specs += [_seq_spec(layer, gs, MLSTM_HEADS, hd, hd), _seq_spec(layer, gs, MLSTM_HEADS, hd),
                     _seq_spec(layer, gs, 1, SMALL)]
    inputs += [w["gate_b"], w["mlstm_nw"]]
    in_specs += [_layer_spec(layer, 1, SMALL), _layer_spec(layer, 1, D_MODEL)]
    hm, *new = _stacked_call(
        kern, name="mlstm", grid=(b // gs, steps), inputs=inputs, in_specs=in_specs,
        out_shape=(jax.ShapeDtypeStruct((b, grp.length, D_MODEL) if tile3d else (grp.rows, D_MODEL), bf16),
                   jax.ShapeDtypeStruct((DEPTH, b, MLSTM_HEADS, hd, hd), f32),
                   jax.ShapeDtypeStruct((DEPTH, b, MLSTM_HEADS, hd), f32),
                   jax.ShapeDtypeStruct((DEPTH, b, 1, SMALL), f32)),
        out_specs=(tile(D_MODEL, 0), _seq_spec(layer, gs, MLSTM_HEADS, hd, hd),
                   _seq_spec(layer, gs, MLSTM_HEADS, hd), _seq_spec(layer, gs, 1, SMALL)),
        stacked={1: prev and prev[0], 2: prev and prev[1], 3: prev and prev[2]},
        scratch_shapes=[pltpu.VMEM((gs, MLSTM_HEADS, hd, hd) if carried else (gs, 1, SUBLANES, LANES), f32),
                        pltpu.VMEM((gs, MLSTM_HEADS, hd), f32), pltpu.VMEM((gs, 1, SMALL), f32)],
        dimension_semantics=("parallel", "arbitrary"))
    return (hm.reshape(grp.rows, D_MODEL), *new)


def _trunk(grp, x, state, w):
    ssd_out = mlstm_out = None
    ffn_out = []
    for layer in range(DEPTH):
        p32, qkv = _proj(x, w["w32"], w["wqkv"], layer, grp.cfg("proj_tm", layer))
        if grp.lanes_ssd:
            b, t = grp.batch, grp.length
            x_tm = x.reshape(b, t, D_MODEL).swapaxes(0, 1).reshape(t * b, D_MODEL)
            ys_t, *ssd_out = _ssd_lanes(_proj_t(x_tm, w["w32"], layer), state["cs_t"], state["h_lanes"], w, layer,
                                        ssd_out, steps=t, batch=b)
            ys = ys_t.reshape(D_MODEL, t, b).transpose(2, 1, 0).reshape(b * t, D_MODEL)
        else:
            ys, *ssd_out = _ssd(grp, p32, state, w, layer, ssd_out)
        hm, *mlstm_out = _mlstm(grp, qkv, p32, state, w, layer, mlstm_out)
        x1 = _merge(ys, hm, p32, x, w, layer, grp.cfg("merge_tm", layer), grp.cfg("merge_parts", layer))
        x, s_ffn = _ffn(x1, state["ffn"] if state is not None else None, w, layer, **grp.ffn)
        ffn_out.append(s_ffn)
    return x, ssd_out, mlstm_out, ffn_out


def _unpack_states(batch, ssd_out, mlstm_out):
    csx, csb, h = ssd_out
    c, n, m = mlstm_out
    return (h.reshape(DEPTH, batch, SSD_HEADS, SSD_HEAD_DIM, SSD_STATE),
            jnp.concatenate([csx, csb], axis=-1), c, n, m[:, :, 0, :MLSTM_HEADS])


def kernel(x_prompt, x_sample, state_ssd, state_ssd_conv, state_mlstm_c, state_mlstm_n, state_mlstm_m,
           state_ffn_conv, w_in, ssd_conv_w, ssd_conv_b, ssd_dt_bias, ssd_a_log, ssd_d, ssd_norm_w,
           mlstm_gate_b, mlstm_norm_w, w_branch_a, w_branch_b, w_out, ln1_g, ln1_b, ffn_w_up, ffn_conv_w,
           ffn_conv_b, ffn_w_down, ln2_g, ln2_b):
    w = _prep_weights(w_in, ssd_conv_w, ssd_conv_b, ssd_dt_bias, ssd_a_log, ssd_d, ssd_norm_w, mlstm_gate_b,
                      mlstm_norm_w, w_branch_a, w_branch_b, w_out, ln1_g, ln1_b, ffn_w_up, ffn_conv_w,
                      ffn_conv_b, ffn_w_down, ln2_g, ln2_b)
    keep = FFN_CONV - 1
    keep_ssd = SSD_CONV - 1

    bp, lp, _ = x_prompt.shape
    prompt = _Group(bp, lp, CHUNK, CHUNK, gs=1, ssd_cps=4, mlstm_cps=4, proj_tm=512, merge_tm=1024,
                    ffn=dict(groups=bp, tm=1024, seq_len=0, perm_q=CHUNK), perm=True, mlstm_gs=2, ssd_gs=2)
    per = CHUNK // SUBLANES
    xp_rows = x_prompt.reshape(bp, lp // CHUNK, SUBLANES, per, D_MODEL).swapaxes(2, 3)
    y_p, ssd_p, mlstm_p, ffn_p = _trunk(prompt, xp_rows.reshape(bp * lp, D_MODEL), None, w)
    y_p = y_p.reshape(bp, lp // CHUNK, per, SUBLANES, D_MODEL).swapaxes(2, 3)
    st_p = _unpack_states(bp, ssd_p, mlstm_p)
    ffn_conv_p = jnp.stack(ffn_p)[:, :, SUBLANES - 1::SUBLANES, :]

    bs, ls, _ = x_sample.shape
    s_rows = bs * ls
    sample = _Group(bs, ls, ls, ls, gs=8, ssd_cps=1, mlstm_cps=1, proj_tm=s_rows, merge_tm=s_rows,
                    ffn=dict(groups=1, tm=256, seq_len=ls), lanes_ssd=True)
    lane_b = lambda a: jnp.broadcast_to(a.astype(f32)[..., None], a.shape + (bs,))
    w.update(cw_b=lane_b(ssd_conv_w), cb_b=lane_b(ssd_conv_b), dtb_b=lane_b(ssd_dt_bias), alog_b=lane_b(ssd_a_log),
             dexp_b=lane_b(ssd_d), nw_b=lane_b(ssd_norm_w))
    s_state = dict(
        cs_t=jnp.transpose(state_ssd_conv, (0, 3, 2, 1)).reshape(DEPTH, D_MODEL + SSD_BC, keep_ssd * bs),
        h_lanes=jnp.transpose(state_ssd, (0, 2, 3, 4, 1)),
        c=state_mlstm_c, n=state_mlstm_n,
        m=jnp.pad(state_mlstm_m, ((0, 0), (0, 0), (0, SMALL - MLSTM_HEADS)))[:, :, None, :],
        ffn=state_ffn_conv.reshape(DEPTH, bs * keep, 2 * D_FF),
    )
    y_s, (cs_t, h_lanes), mlstm_s, ffn_s = _trunk(sample, x_sample.reshape(s_rows, D_MODEL), s_state, w)
    c_s, n_s, m_s = mlstm_s
    st_s = (jnp.transpose(h_lanes, (0, 4, 1, 2, 3)),
            jnp.transpose(cs_t.reshape(DEPTH, D_MODEL + SSD_BC, keep_ssd, bs), (0, 3, 2, 1)),
            c_s, n_s, m_s[:, :, 0, :MLSTM_HEADS])
    ffn_conv_s = jnp.stack([u.reshape(bs, ls, 2 * D_FF)[:, ls - keep:, :] for u in ffn_s])
    y_sample = y_s.reshape(bs, ls, D_MODEL)

    return (y_p.reshape(bp, lp, D_MODEL), y_sample, st_p[0], st_s[0], st_p[1], st_s[1], st_p[2], st_s[2],
            st_p[3], st_s[3], st_p[4], st_s[4], ffn_conv_p, ffn_conv_s)
```

```python
import functools
import itertools

import jax
import jax.numpy as jnp
import numpy as np
from jax import lax
from jax.experimental import pallas as pl
from jax.experimental.pallas import tpu as pltpu

f32 = jnp.float32
bf16 = jnp.bfloat16

D_MODEL = 1024
DEPTH = 2
SSD_HEADS = 16
SSD_HEAD_DIM = 64
SSD_STATE = 64
SSD_GROUPS = 2
SSD_CONV = 4
SSD_BC = 2 * SSD_GROUPS * SSD_STATE
SSD_HP = SSD_HEADS * SSD_HEAD_DIM
MLSTM_HEADS = 4
MLSTM_HEAD_DIM = 256
CHUNK = 128
D_FF = 2816
FFN_CONV = 3
ALPHA = (2 * DEPTH) ** 0.25
EPS = 1e-5

LANES = 128
SUBLANES = 8
SMALL = LANES
DT_OFF, I_OFF, F_OFF = 0, 16, 20
P32_Z, P32_O, P32_GA, P32_GB, P32_XS = 0, 1, 2, 3, 4
P32_BC_OFF = 5 * D_MODEL
P32_SM_OFF = P32_BC_OFF + SSD_BC
P32_W = P32_SM_OFF + 2 * SMALL
FF_BOUNDS = (0, 256, 768, 1280, 1792, 2304, 2816)
FF_CH = max(b - a for a, b in zip(FF_BOUNDS, FF_BOUNDS[1:]))
FF_NCH = len(FF_BOUNDS) - 1
MERGE_PARTS = 2
FF_UP_AHEAD = 2
FF_XP_SLOTS = 4
NEG_BIG = -1e30

NT_DIMS = (((1,), (1,)), ((), ()))
TN_DIMS = (((0,), (0,)), ((), ()))


def _dot(a, b):
    return jnp.dot(a, b, preferred_element_type=f32)


def _split3(x):
    hi = x.astype(bf16)
    r = x - hi.astype(f32)
    mid = r.astype(bf16)
    lo = (r - mid.astype(f32)).astype(bf16)
    return hi, mid, lo


def _dot01_rhs(x, e):
    hi, mid, lo = _split3(x)
    return _dot(hi, e) + _dot(mid, e) + _dot(lo, e)


def _dot01_lhs(t, x):
    hi, mid, lo = _split3(x)
    return _dot(t, hi) + _dot(t, mid) + _dot(t, lo)


def _softplus(x):
    return jnp.maximum(x, 0.0) + jnp.log1p(jnp.exp(-jnp.abs(x)))


def _silu(x):
    return x * jax.nn.sigmoid(x)


def _row_time(i, q, perm):
    if not perm:
        return i
    return (i & (SUBLANES - 1)) * (q // SUBLANES) + lax.shift_right_logical(i, SUBLANES.bit_length() - 1)


def _tri(q, perm=False):
    row = lax.broadcasted_iota(jnp.int32, (q, q), 0)
    col = lax.broadcasted_iota(jnp.int32, (q, q), 1)
    return _row_time(row, q, perm) >= _row_time(col, q, perm)


def _wrap_rows(cur_tail, prev_tail):
    out = []
    for i in range(cur_tail.shape[0] // SUBLANES):
        rows = slice(i * SUBLANES, (i + 1) * SUBLANES)
        first = lax.broadcasted_iota(jnp.int32, (SUBLANES, cur_tail.shape[1]), 0) == 0
        out.append(jnp.where(first, pltpu.roll(prev_tail[rows], 1, axis=0), pltpu.roll(cur_tail[rows], 1, axis=0)))
    return jnp.concatenate(out, axis=0)


def _shift_back(x, wrapped, j):
    n = j * SUBLANES
    return jnp.concatenate([wrapped[wrapped.shape[0] - n:], x[:x.shape[0] - n]], axis=0)


def _valid_rows(q, width, lr, is_last):
    row = lax.broadcasted_iota(jnp.int32, (q, width), 0)
    return row < jnp.where(is_last, lr, q)


def _layer_norm(r, g, b):
    mu = jnp.mean(r, axis=-1, keepdims=True)
    var = jnp.mean(jnp.square(r - mu), axis=-1, keepdims=True)
    return (r - mu) * lax.rsqrt(var + EPS) * g + b


def _layer_spec(layer, *shape, **kw):
    zeros = (0,) * len(shape)
    return pl.BlockSpec((None,) + shape, lambda *_: (layer,) + zeros, **kw)


def _seq_spec(layer, gs, *shape):
    zeros = (0,) * len(shape)
    return pl.BlockSpec((None, gs) + shape, lambda b, c: (layer, b) + zeros)


def _stacked_call(kern, *, name, grid, inputs, in_specs, out_shape, out_specs, stacked, scratch_shapes,
                  dimension_semantics, vmem_limit_bytes=None):
    prev = [(i, a) for i, a in sorted(stacked.items()) if a is not None]
    n_in = len(inputs)

    def body(*refs):
        kern(*refs[:n_in], *refs[n_in + len(prev):])

    return pl.pallas_call(
        body,
        out_shape=out_shape,
        grid=grid,
        in_specs=list(in_specs) + [pl.BlockSpec(memory_space=pl.ANY)] * len(prev),
        out_specs=out_specs,
        scratch_shapes=scratch_shapes,
        input_output_aliases={n_in + k: i for k, (i, _) in enumerate(prev)},
        compiler_params=pltpu.CompilerParams(dimension_semantics=dimension_semantics,
                                             vmem_limit_bytes=vmem_limit_bytes),
        name=name,
    )(*inputs, *[a for _, a in prev])


def _proj_kernel(x_ref, w32_ref, wqkv_ref, o32_ref, oqkv_ref):
    xb = x_ref[...].astype(bf16)
    o32_ref[...] = lax.dot_general(xb, w32_ref[...], NT_DIMS, preferred_element_type=f32)
    oqkv_ref[...] = lax.dot_general(xb, wqkv_ref[...], NT_DIMS, preferred_element_type=f32).astype(oqkv_ref.dtype)


def _proj(x, w32, wqkv, layer, tm):
    m, k = x.shape
    n32, nq = w32.shape[1], wqkv.shape[1]
    once = dict(pipeline_mode=pl.Buffered(1))
    return pl.pallas_call(
        _proj_kernel,
        out_shape=(jax.ShapeDtypeStruct((m, n32), f32), jax.ShapeDtypeStruct((m, nq), bf16)),
        grid=(m // tm,),
        in_specs=[pl.BlockSpec((tm, k), lambda i: (i, 0)),
                  _layer_spec(layer, n32, k, **once), _layer_spec(layer, nq, k, **once)],
        out_specs=(pl.BlockSpec((tm, n32), lambda i: (i, 0)), pl.BlockSpec((tm, nq), lambda i: (i, 0))),
        compiler_params=pltpu.CompilerParams(dimension_semantics=("parallel",),
                                             vmem_limit_bytes=56 * 1024 * 1024),
        name="proj",
    )(x, w32, wqkv)


class _Rows:
    def __init__(self, ref, start, n):
        self.ref, self.start, self.n, self.dtype = ref, start, n, ref.dtype

    def rows(self, off, n):
        return _Rows(self.ref, self.start + off, n)

    def _index(self, idx):
        cols = slice(None) if idx is Ellipsis else idx[1]
        return (slice(self.start, self.start + self.n), cols)

    def __getitem__(self, idx):
        return self.ref[self._index(idx)]

    def __setitem__(self, idx, value):
        self.ref[self._index(idx)] = value


def _per_sequence(seq_fn, refs, n_tile, n_state, n_param, gs, has_state, nc, rows, tile3d=False):
    n_state = n_state if has_state else 0
    tiles, refs = refs[:n_tile], refs[n_tile:]
    state, refs = refs[:n_state], refs[n_state:]
    params, (y_tile, *rest) = refs[:n_param], refs[n_param:]
    phases = []
    for g in range(gs):
        at = lambda group: tuple(r.at[g] for r in group)
        if tile3d:
            seq_rows = lambda group: tuple(_Rows(r.at[g], 0, rows) for r in group)
        else:
            seq_rows = lambda group: tuple(_Rows(r, g * rows, rows) for r in group)
        phases.append(seq_fn(*seq_rows(tiles), *at(state), *params, *seq_rows((y_tile,)), *at(rest)))
    c = pl.program_id(1)

    @pl.when(c == 0)
    def _():
        for init, _, _ in phases:
            init()

    for _ in itertools.zip_longest(*[body() for _, body, _ in phases]):
        pass

    @pl.when(c == nc - 1)
    def _():
        for _, _, final in phases:
            final()


def _ssd_kernel(*refs, q, lr, nc, has_state, gs, cps, perm, tile3d):
    seq = functools.partial(_ssd_seq, q=q, lr=lr, nc=nc, has_state=has_state, cps=cps, perm=perm)
    _per_sequence(seq, refs, 4, 3, 10, gs, has_state, nc, cps * q, tile3d)


def _ssd_seq(*refs, q, lr, nc, has_state, cps, perm):
    z_ref, xs_ref, bc_ref, sm_ref = refs[:4]
    refs = refs[4:]
    if has_state:
        csx_ref, csb_ref, h0_ref = refs[:3]
        refs = refs[3:]
    (cwx_ref, cbx_ref, cwb_ref, cbb_ref, dtb_ref, alog_ref, dexp_ref, nw_ref, e_ref, bd_ref,
     y_ref, ncsx_ref, ncsb_ref, hout_ref, xpx, xpb, ht, yb) = refs
    hdr = SUBLANES
    lo = hdr - (SSD_CONV - 1)
    n2 = SSD_GROUPS * SSD_STATE
    assert lr >= SSD_CONV - 1
    keep = SSD_CONV - 1
    assert not (perm and (has_state or lr != q))
    carried = [(i + 1) * SUBLANES - 1 for i in range(keep)]

    def init():
        if has_state:
            h_t = h0_ref[...].T
            ht[...] = jnp.where(bd_ref[...] > 0.5, jnp.concatenate([h_t, h_t], axis=0), 0.0)
            xpx[lo:hdr, :] = csx_ref[...]
            xpb[lo:hdr, :] = csb_ref[...]
        else:
            ht[...] = jnp.zeros_like(ht)
            rows = slice(0, keep * SUBLANES) if perm else slice(lo, hdr)
            xpx[rows, :] = jnp.zeros((rows.stop - rows.start, SSD_HP), f32)
            xpb[rows, :] = jnp.zeros((rows.stop - rows.start, SSD_BC), f32)

    def final():
        if perm:
            for i, r in enumerate(carried):
                ncsx_ref[i:i + 1, :] = xpx[r:r + 1, :]
                ncsb_ref[i:i + 1, :] = xpb[r:r + 1, :]
        else:
            ncsx_ref[...] = xpx[lo + lr:hdr + lr, :]
            ncsb_ref[...] = xpb[lo + lr:hdr + lr, :]
        h_new = ht[...]
        hout_ref[...] = (h_new[:SSD_STATE, :] + h_new[SSD_STATE:, :]).T

    def body():
        for k in range(cps):
            sub = lambda r: r.rows(k * q, q)
            is_last = (pl.program_id(1) == nc - 1) if k == cps - 1 else False
            yield from _ssd_body(sub(z_ref), sub(xs_ref), sub(bc_ref), sub(sm_ref), cwx_ref, cbx_ref, cwb_ref,
                                 cbb_ref, dtb_ref, alog_ref, dexp_ref, nw_ref, e_ref, bd_ref, sub(y_ref),
                                 xpx, xpb, ht, yb, q=q, lr=lr, is_last=is_last, perm=perm)

    return init, body, final


def _ssd_body(z_ref, xs_ref, bc_ref, sm_ref, cwx_ref, cbx_ref, cwb_ref, cbb_ref, dtb_ref, alog_ref,
              dexp_ref, nw_ref, e_ref, bd_ref, y_ref, xpx, xpb, ht, yb, *, q, lr, is_last, perm):
    hdr = SUBLANES
    lo = hdr - (SSD_CONV - 1)
    n2 = SSD_GROUPS * SSD_STATE
    block_diag = bd_ref[...] > 0.5

    dt = _softplus(sm_ref[...] + dtb_ref[...])
    if lr < q:
        dt = jnp.where(_valid_rows(q, SMALL, lr, is_last), dt, 0.0)
    a = -jnp.exp(alog_ref[...])
    d_a = dt * a
    causal = _tri(q, perm)
    tril = jnp.where(causal, 1.0, 0.0).astype(bf16)
    e = e_ref[...]
    acs = _dot01_lhs(tril, d_a)
    dt_x = _dot01_rhs(dt, e)
    yield

    if perm:
        keep_rows = (SSD_CONV - 1) * SUBLANES

        def conv(xp, x_ref, w_ref, b_ref):
            w = w_ref[...]
            x = x_ref[...]
            wrapped = _wrap_rows(x[q - keep_rows:, :], xp[0:keep_rows, :])
            acc = _shift_back(x, wrapped, SSD_CONV - 1) * w[0:1, :]
            for j in range(1, SSD_CONV - 1):
                acc = acc + _shift_back(x, wrapped, SSD_CONV - 1 - j) * w[j:j + 1, :]
            acc = acc + x * w[SSD_CONV - 1:SSD_CONV, :]
            xp[0:keep_rows, :] = x[q - keep_rows:, :]
            return acc + b_ref[...]
    else:
        xpx[hdr:hdr + q, :] = xs_ref[...]
        xpb[hdr:hdr + q, :] = bc_ref[...]

        def conv(xp, x_ref, w_ref, b_ref):
            w = w_ref[...]
            acc = xp[lo:lo + q, :] * w[0:1, :]
            for j in range(1, SSD_CONV):
                acc = acc + xp[lo + j:lo + j + q, :] * w[j:j + 1, :]
            return acc + b_ref[...]

    cb = conv(xpb, bc_ref, cwb_ref, cbb_ref)
    bcv = _silu(cb)
    bm = bcv[:, :n2].astype(bf16)
    cm = bcv[:, n2:]
    lane_g0 = lax.broadcasted_iota(jnp.int32, (q, n2), 1) < SSD_STATE
    acs_t = acs.T
    acs_x = _dot01_rhs(acs, e)
    yield
    cbms = [lax.dot_general(jnp.where(lane_g0 if g == 0 else jnp.logical_not(lane_g0), cm, 0.0).astype(bf16),
                            bm, NT_DIMS, preferred_element_type=f32) for g in range(SSD_GROUPS)]
    h_prev = ht[...]
    y_off = _dot(cm.astype(bf16), h_prev.astype(bf16))
    cx = conv(xpx, xs_ref, cwx_ref, cbx_ref)
    if not perm:
        tail_x = xpx[lo + q:hdr + q, :]
        tail_b = xpb[lo + q:hdr + q, :]
        xpx[lo:hdr, :] = tail_x
        xpb[lo:hdr, :] = tail_b
    yield
    xs = _silu(cx)
    last_x = acs_x[q - 1:q, :]
    xdt = xs * dt_x
    xdt_b = xdt.astype(bf16)
    yield
    lane_lo = lax.broadcasted_iota(jnp.int32, (q, LANES), 1) < SSD_HEAD_DIM
    heads_per_group = SSD_HEADS // SSD_GROUPS
    decays = [jnp.exp(jnp.where(causal, acs[:, hh:hh + 1] - acs_t[hh:hh + 1, :], -jnp.inf))
              for hh in range(SSD_HEADS)]
    yield
    weights = [(cbms[hh // heads_per_group] * decays[hh]).astype(bf16) for hh in range(SSD_HEADS)]
    xdtw = (xdt * jnp.exp(last_x - acs_x)).astype(bf16)
    yield
    ys = [_dot(weights[hh], xdt_b[:, (hh // 2) * LANES:(hh // 2 + 1) * LANES]) for hh in range(SSD_HEADS)]
    upd = lax.dot_general(bm, xdtw, TN_DIMS, preferred_element_type=f32)
    yield
    for p in range(SSD_HEADS // 2):
        yb[:, p * LANES:(p + 1) * LANES] = jnp.where(lane_lo, ys[2 * p], ys[2 * p + 1])
    ht[...] = jnp.exp(last_x) * h_prev + jnp.where(block_diag, upd, 0.0)
    yield
    y = yb[...] + y_off * jnp.exp(acs_x) + dexp_ref[...] * xs
    y = y * _silu(z_ref[...])
    yield
    y = y * lax.rsqrt(jnp.mean(jnp.square(y), axis=-1, keepdims=True) + EPS) * nw_ref[...]
    y_ref[...] = y.astype(y_ref.dtype)


PT_ROWS = 5 * 512
PT_XS, PT_BC, PT_SM = D_MODEL, 2 * D_MODEL, 2 * D_MODEL + SSD_BC


def _proj_t_kernel(x_ref, w_ref, o_ref, xb):
    @pl.when(pl.program_id(0) == 0)
    def _():
        xb[...] = x_ref[...].astype(bf16)

    o_ref[...] = lax.dot_general(w_ref[...], xb[...], NT_DIMS, preferred_element_type=f32)


def _proj_t(x_tm, w32, layer):
    m, k = x_tm.shape
    tn = 512
    xs_blk = P32_XS * D_MODEL // tn
    return pl.pallas_call(
        _proj_t_kernel,
        out_shape=jax.ShapeDtypeStruct((PT_ROWS, m), f32),
        grid=(PT_ROWS // tn,),
        in_specs=[pl.BlockSpec((m, k), lambda j: (0, 0)),
                  pl.BlockSpec((None, tn, k), lambda j: (layer, jnp.where(j < D_MODEL // tn, j, j + xs_blk - D_MODEL // tn), 0))],
        out_specs=pl.BlockSpec((tn, m), lambda j: (j, 0)),
        scratch_shapes=[pltpu.VMEM((m, k), bf16)],
        compiler_params=pltpu.CompilerParams(dimension_semantics=("arbitrary",)),
        name="proj_t",
    )(x_tm, w32)


def _ssd_lanes_kernel(pt_ref, cst_ref, h0_ref, cw_ref, cb_ref, dtb_ref, alog_ref, dexp_ref, nw_ref,
                      y_ref, ncs_ref, hout_ref, xc, dts, decs, ysc, *, steps, batch):
    hd = pl.program_id(0)
    n_ch = D_MODEL + SSD_BC
    keep = SSD_CONV - 1
    lanes = lambda t: slice(t * batch, (t + 1) * batch)

    @pl.when(hd == 0)
    def _():
        for t in range(steps):
            acc = None
            for j in range(SSD_CONV):
                i = t + j
                src = cst_ref[:, lanes(i)] if i < keep else pt_ref[PT_XS:PT_XS + n_ch, lanes(i - keep)]
                term = src * cw_ref[j]
                acc = term if acc is None else acc + term
            xc[:, lanes(t)] = _silu(acc + cb_ref[...])
        ncs_ref[...] = pt_ref[PT_XS:PT_XS + n_ch, (steps - keep) * batch:steps * batch]
        dt = _softplus(pt_ref[PT_SM:PT_SM + SSD_HEADS, :] + jnp.concatenate([dtb_ref[...]] * steps, axis=1))
        dts[...] = dt
        decs[...] = jnp.exp(dt * jnp.concatenate([-jnp.exp(alog_ref[...])] * steps, axis=1))

    grp_row = (hd // (SSD_HEADS // SSD_GROUPS)) * SSD_STATE
    xh = xc[pl.ds(pl.multiple_of(hd * SSD_HEAD_DIM, SSD_HEAD_DIM), SSD_HEAD_DIM), :]
    bh = xc[pl.ds(pl.multiple_of(D_MODEL + grp_row, SSD_STATE), SSD_STATE), :]
    ch = xc[pl.ds(pl.multiple_of(D_MODEL + SSD_GROUPS * SSD_STATE + grp_row, SSD_STATE), SSD_STATE), :]
    dth = dts[pl.ds(hd, 1), :]
    dech = decs[pl.ds(hd, 1), :]
    d_skip = dexp_ref[pl.ds(hd, 1), :]
    y_rows = [[] for _ in range(steps)]
    for p in range(SSD_HEAD_DIM):
        h = h0_ref[p]
        for t in range(steps):
            x_row = xh[p:p + 1, lanes(t)]
            h = dech[:, lanes(t)] * h + (x_row * dth[:, lanes(t)]) * bh[:, lanes(t)]
            y_rows[t].append(jnp.sum(ch[:, lanes(t)] * h, axis=0, keepdims=True) + d_skip * x_row)
        hout_ref[p] = h
    rows = pl.ds(pl.multiple_of(hd * SSD_HEAD_DIM, SSD_HEAD_DIM), SSD_HEAD_DIM)
    for t in range(steps):
        ysc[rows, lanes(t)] = jnp.concatenate(y_rows[t], axis=0)

    @pl.when(hd == SSD_HEADS - 1)
    def _():
        y = ysc[...] * _silu(pt_ref[0:D_MODEL, :])
        y = y * lax.rsqrt(jnp.mean(jnp.square(y), axis=0, keepdims=True) + EPS)
        y_ref[...] = (y * jnp.concatenate([nw_ref[...]] * steps, axis=1)).astype(y_ref.dtype)


def _ssd_lanes(pt, cst, h0, w, layer, prev, *, steps, batch):
    n_ch = D_MODEL + SSD_BC
    tb = steps * batch
    keep = SSD_CONV - 1
    assert steps >= keep and batch % LANES == 0
    kern = functools.partial(_ssd_lanes_kernel, steps=steps, batch=batch)
    full = lambda *shape: pl.BlockSpec(shape, lambda hd: (0,) * len(shape))
    hblock = pl.BlockSpec((None, None, SSD_HEAD_DIM, SSD_STATE, batch), lambda hd: (layer, hd, 0, 0, 0))
    return _stacked_call(
        kern, name="ssd_lanes", grid=(SSD_HEADS,),
        inputs=[pt, cst, h0, w["cw_b"], w["cb_b"], w["dtb_b"], w["alog_b"], w["dexp_b"], w["nw_b"]],
        in_specs=[full(PT_ROWS, tb), _layer_spec(layer, n_ch, keep * batch), hblock,
                  _layer_spec(layer, SSD_CONV, n_ch, batch), _layer_spec(layer, n_ch, batch),
                  _layer_spec(layer, SSD_HEADS, batch), _layer_spec(layer, SSD_HEADS, batch),
                  _layer_spec(layer, SSD_HEADS, batch), _layer_spec(layer, D_MODEL, batch)],
        out_shape=(jax.ShapeDtypeStruct((D_MODEL, tb), bf16),
                   jax.ShapeDtypeStruct((DEPTH, n_ch, keep * batch), f32),
                   jax.ShapeDtypeStruct((DEPTH, SSD_HEADS, SSD_HEAD_DIM, SSD_STATE, batch), f32)),
        out_specs=(full(D_MODEL, tb), _layer_spec(layer, n_ch, keep * batch), hblock),
        stacked={1: prev and prev[0], 2: prev and prev[1]},
        scratch_shapes=[pltpu.VMEM((n_ch, tb), f32), pltpu.VMEM((SSD_HEADS, tb), f32),
                        pltpu.VMEM((SSD_HEADS, tb), f32), pltpu.VMEM((D_MODEL, tb), f32)],
        dimension_semantics=("arbitrary",))


def _mlstm_kernel(*refs, q, lr, nc, has_state, gs, cps, perm, tile3d):
    seq = functools.partial(_mlstm_seq, q=q, lr=lr, nc=nc, has_state=has_state, cps=cps, perm=perm)
    _per_sequence(seq, refs, 5, 3, 2, gs, has_state, nc, cps * q, tile3d)


def _mlstm_seq(*refs, q, lr, nc, has_state, cps, perm):
    q_ref, k_ref, v_ref, o_ref, sm_ref = refs[:5]
    refs = refs[5:]
    if has_state:
        c0_ref, n0_ref, m0_ref = refs[:3]
        refs = refs[3:]
    gb_ref, nw_ref, h_ref, cout_ref, nout_ref, mout_ref, cs, ns, ms = refs
    direct = has_state and nc == 1 and cps == 1

    def init():
        if direct:
            return
        if has_state:
            cs[...] = c0_ref[...]
            ns[...] = n0_ref[...]
            ms[...] = m0_ref[...]
        else:
            cs[...] = jnp.zeros_like(cs)
            ns[...] = jnp.zeros_like(ns)
            ms[...] = jnp.zeros_like(ms)

    def final():
        if direct:
            return
        cout_ref[...] = cs[...]
        nout_ref[...] = ns[...]
        mout_ref[...] = ms[...]

    def body():
        chunks = []
        for k in range(cps):
            sub = lambda r, k=k: r.rows(k * q, q)
            is_last = (pl.program_id(1) == nc - 1) if k == cps - 1 else False
            src = (c0_ref, n0_ref, m0_ref) if direct else (cs, ns, ms)
            dst = (cout_ref, nout_ref, mout_ref) if direct else (cs, ns, ms)
            chunks.append(_mlstm_body(sub(q_ref), sub(k_ref), sub(v_ref), sub(o_ref), sub(sm_ref), gb_ref, nw_ref,
                                      sub(h_ref), src, dst, q=q, lr=lr, is_last=is_last, perm=perm))
        yield from _staggered(chunks, MLSTM_STATE_STAGES)

    return init, body, final


MLSTM_STATE_STAGES = 8


def _staggered(gens, skew):
    done = [False] * len(gens)
    t = 0
    while not all(done):
        for i, g in enumerate(gens):
            if done[i] or t < i * skew:
                continue
            try:
                next(g)
            except StopIteration:
                done[i] = True
        t += 1
        yield


def _mlstm_body(q_ref, k_ref, v_ref, o_ref, sm_ref, gb_ref, nw_ref, h_ref, src, dst, *, q, lr, is_last, perm):
    c_src, n_src, m_src = src
    c_dst, n_dst, m_dst = dst
    sm = sm_ref[...] + gb_ref[...]
    logf = -_softplus(-sm)
    ipre = sm
    if lr < q:
        valid = _valid_rows(q, SMALL, lr, is_last)
        logf = jnp.where(valid, logf, 0.0)
        ipre = jnp.where(valid, ipre, NEG_BIG)
    causal = _tri(q, perm)
    tril = jnp.where(causal, 1.0, 0.0).astype(bf16)
    yield
    bcum = _dot01_lhs(tril, logf)
    ipre_t = ipre.T
    yield
    bcum_t = bcum.T
    lane = lax.broadcasted_iota(jnp.int32, (1, SMALL), 1)
    k_scale = MLSTM_HEAD_DIM ** -0.5

    heads = range(MLSTM_HEADS)
    sls = [slice(h * MLSTM_HEAD_DIM, (h + 1) * MLSTM_HEAD_DIM) for h in heads]
    q_all, k_all, v_all, o_all = q_ref[...], k_ref[...], v_ref[...], o_ref[...]
    qs = [q_all[:, sl] for sl in sls]
    ks = [k_all[:, sl] * k_scale for sl in sls]
    vs = [v_all[:, sl] for sl in sls]
    b_cols = [bcum[:, F_OFF + h:F_OFF + h + 1] for h in heads]
    i_cols = [ipre[:, I_OFF + h:I_OFF + h + 1] for h in heads]
    dmats = [jnp.where(causal, b_cols[h] - bcum_t[F_OFF + h:F_OFF + h + 1, :] + ipre_t[I_OFF + h:I_OFF + h + 1, :],
                       -jnp.inf) for h in heads]
    yield
    qk = [lax.dot_general(qs[h], ks[h], NT_DIMS, preferred_element_type=f32) for h in heads]
    d_max = [jnp.max(dmats[h], axis=-1, keepdims=True) for h in heads]
    yield
    n_all = n_src[...]
    m_all = m_src[...]
    m_new = m_all
    cs_in = [c_src[h] for h in heads]
    m_prevs = [m_all[:, h:h + 1] for h in heads]
    qc = [_dot(qs[h], cs_in[h].astype(bf16)) for h in heads]
    inters = [b_cols[h] + m_prevs[h] for h in heads]
    m_ts = [jnp.maximum(inters[h], d_max[h]) for h in heads]
    yield
    w_inters = [jnp.exp(inters[h] - m_ts[h]) for h in heads]
    ss = [qk[h] * jnp.exp(dmats[h] - m_ts[h]) for h in heads]
    yield
    sv = [_dot(ss[h].astype(bf16), vs[h]) for h in heads]
    m_ends = [m_ts[h][q - 1:q, :] for h in heads]
    b_lasts = [b_cols[h][q - 1:q, :] for h in heads]
    kws = [ks[h].astype(f32) * jnp.exp(b_lasts[h] - b_cols[h] + i_cols[h] - m_ends[h]) for h in heads]
    yield
    kv = [lax.dot_general(kws[h].astype(bf16), vs[h], TN_DIMS, preferred_element_type=f32) for h in heads]
    qns = [jnp.sum(qs[h].astype(f32) * n_all[h:h + 1, :], axis=-1, keepdims=True) for h in heads]
    yield
    dens = [jnp.sum(ss[h], axis=-1, keepdims=True) + w_inters[h] * qns[h] for h in heads]
    yield
    hvs = [(sv[h] + w_inters[h] * qc[h]) / jnp.maximum(jnp.abs(dens[h]), jnp.exp(-m_ts[h])) for h in heads]
    yield
    rms = [lax.rsqrt(jnp.mean(jnp.square(hvs[h]), axis=-1, keepdims=True) + EPS) for h in heads]
    yield
    h_new = [(hvs[h] * rms[h] * nw_ref[:, sls[h]] * jax.nn.sigmoid(o_all[:, sls[h]])).astype(h_ref.dtype)
             for h in heads]
    w_cs = [jnp.exp(b_lasts[h] + m_prevs[h] - m_ends[h]) for h in heads]
    yield
    for h in heads:
        c_dst[h] = w_cs[h] * cs_in[h] + kv[h]
        m_new = jnp.where(lane == h, m_ends[h], m_new)
    h_ref[...] = jnp.concatenate(h_new, axis=1)
    n_dst[...] = jnp.concatenate(
        [w_cs[h] * n_all[h:h + 1, :] + jnp.sum(kws[h], axis=0, keepdims=True) for h in heads], axis=0)
    m_dst[...] = m_new


def _merge_kernel(ys_ref, hm_ref, ga_ref, gb_ref, x_ref, wa_ref, wb_ref, wo_ref, g_ref, b_ref, o_ref, *, parts):
    tm = x_ref.shape[0]
    rows = [pl.ds(i * (tm // parts), tm // parts) for i in range(parts)]
    br = [(_dot(ys_ref[r, :], wa_ref[...]), _dot(hm_ref[r, :], wb_ref[...])) for r in rows]
    merged = [(jax.nn.sigmoid(ga_ref[r, :]) * a + jax.nn.sigmoid(gb_ref[r, :]) * b).astype(bf16)
              for r, (a, b) in zip(rows, br)]
    mix = [_dot(m, wo_ref[...]) for m in merged]
    for r, m in zip(rows, mix):
        o_ref[r, :] = _layer_norm(ALPHA * x_ref[r, :] + m, g_ref[...], b_ref[...])


def _merge(ys, hm, p32, x, w, layer, tm, parts=MERGE_PARTS):
    m = x.shape[0]
    row = lambda blk: pl.BlockSpec((tm, D_MODEL), lambda i: (i, blk))
    return pl.pallas_call(
        functools.partial(_merge_kernel, parts=parts),
        out_shape=jax.ShapeDtypeStruct((m, D_MODEL), f32),
        grid=(m // tm,),
        in_specs=[row(0), row(0), row(P32_GA), row(P32_GB), row(0),
                  _layer_spec(layer, D_MODEL, D_MODEL), _layer_spec(layer, D_MODEL, D_MODEL),
                  _layer_spec(layer, D_MODEL, D_MODEL), _layer_spec(layer, 1, D_MODEL),
                  _layer_spec(layer, 1, D_MODEL)],
        out_specs=row(0),
        compiler_params=pltpu.CompilerParams(dimension_semantics=("parallel",)),
        name="merge",
    )(ys, hm, p32, p32, x, w["wa"], w["wb"], w["wo"], w["ln1_g"], w["ln1_b"])


def _ffn_kernel(*refs, tm, seq_len, perm_q, ahead_n):
    multi = seq_len > 0
    keep_rows = (FFN_CONV - 1) * SUBLANES
    if multi:
        x_ref, st_ref, wup_ref, cw_ref, cb_ref, wdn_ref, g_ref, b_ref, o_ref, sout_ref, xp = refs
    else:
        x_ref, wup_ref, cw_ref, cb_ref, wdn_ref, g_ref, b_ref, o_ref, sout_ref, xp, carry = refs
        @pl.when(pl.program_id(1) == 0)
        def _():
            carry[...] = jnp.zeros_like(carry)

    hdr = SUBLANES
    x = x_ref[...]
    xb = x.astype(bf16)
    if multi:
        assert seq_len & (seq_len - 1) == 0
        nseq = tm // seq_len
        t = lax.broadcasted_iota(jnp.int32, (tm, 1), 0) & (seq_len - 1)
        row = lax.broadcasted_iota(jnp.int32, (tm, 2 * nseq), 0)
        col = lax.broadcasted_iota(jnp.int32, (tm, 2 * nseq), 1)
        t_sel = row & (seq_len - 1)
        seq0 = lax.shift_right_logical(row - t_sel, (seq_len // 2).bit_length() - 1)
        sel_p2 = jnp.where(col == seq0 + t_sel, jnp.where(t_sel < 2, 1.0, 0.0), 0.0).astype(bf16)
        sel_p1 = jnp.where(col == seq0 + 1, jnp.where(t_sel == 0, 1.0, 0.0), 0.0).astype(bf16)
        xp[:, 0:hdr, :] = jnp.zeros((FF_XP_SLOTS, hdr, FF_CH), f32)

    def cols_of(c, part):
        return slice(part * D_FF + FF_BOUNDS[c], part * D_FF + FF_BOUNDS[c + 1])

    def up(c):
        return [_dot(xb, wup_ref[:, cols_of(c, part)]) for part in range(2)]

    def conv_act(c, us):
        halves = []
        for part, u in enumerate(us):
            cols = cols_of(c, part)
            slot = (2 * c + part) % FF_XP_SLOTS
            if perm_q:
                prev = carry[:, cols]
                p1, p2 = [], []
                for kk in range(tm // perm_q):
                    uc = u[kk * perm_q:(kk + 1) * perm_q, :]
                    wrapped = _wrap_rows(uc[perm_q - keep_rows:, :], prev)
                    p1.append(_shift_back(uc, wrapped, 1))
                    p2.append(_shift_back(uc, wrapped, 2))
                    prev = uc[perm_q - keep_rows:, :]
                carry[:, cols] = prev
                p1, p2 = jnp.concatenate(p1, axis=0), jnp.concatenate(p2, axis=0)
                w = cw_ref[:, cols]
                halves.append(p2 * w[0:1, :] + p1 * w[1:2, :] + u * w[2:3, :] + cb_ref[:, cols])
                continue
            wd = slice(0, u.shape[1])
            xp[slot, hdr:hdr + tm, wd] = u
            if multi:
                sout_ref[:, cols] = u
                st = st_ref[:, cols]
                p1 = jnp.where(t == 0, _dot01_lhs(sel_p1, st), xp[slot, hdr - 1:hdr - 1 + tm, wd])
                p2 = jnp.where(t < 2, _dot01_lhs(sel_p2, st), xp[slot, hdr - 2:hdr - 2 + tm, wd])
            else:
                xp[slot, 0:hdr, wd] = carry[:, cols]
                p1 = xp[slot, hdr - 1:hdr - 1 + tm, wd]
                p2 = xp[slot, hdr - 2:hdr - 2 + tm, wd]
                carry[:, cols] = u[tm - hdr:tm, :]
            w = cw_ref[:, cols]
            halves.append(p2 * w[0:1, :] + p1 * w[1:2, :] + u * w[2:3, :] + cb_ref[:, cols])
        return (_silu(halves[0]) * halves[1]).astype(bf16)

    acc = None
    ahead = [up(c) for c in range(min(ahead_n, FF_NCH))]
    pending = None
    for c in range(FF_NCH):
        if c + ahead_n < FF_NCH:
            ahead.append(up(c + ahead_n))
        if pending is not None:
            d = _dot(pending, wdn_ref[FF_BOUNDS[c - 1]:FF_BOUNDS[c], :])
            acc = d if acc is None else acc + d
        pending = conv_act(c, ahead.pop(0))
    acc = acc + _dot(pending, wdn_ref[FF_BOUNDS[FF_NCH - 1]:FF_BOUNDS[FF_NCH], :])

    if not multi:
        sout_ref[...] = carry[...]
    o_ref[...] = _layer_norm(ALPHA * x + acc, g_ref[...], b_ref[...])


def _ffn(x, st, w, layer, *, groups, tm, seq_len, perm_q=0, ahead=FF_UP_AHEAD):
    m = x.shape[0]
    tiles = m // (groups * tm)
    multi = seq_len > 0
    kern = functools.partial(_ffn_kernel, tm=tm, seq_len=seq_len, perm_q=perm_q, ahead_n=ahead)
    carry_rows = (FFN_CONV - 1) * SUBLANES if perm_q else SUBLANES
    once = dict(pipeline_mode=pl.Buffered(1))
    x_spec = pl.BlockSpec((tm, D_MODEL), lambda s, j: (s * tiles + j, 0))
    w_specs = [_layer_spec(layer, D_MODEL, 2 * D_FF, **once), _layer_spec(layer, FFN_CONV, 2 * D_FF, **once),
               _layer_spec(layer, 1, 2 * D_FF, **once), _layer_spec(layer, D_FF, D_MODEL, **once),
               _layer_spec(layer, 1, D_MODEL, **once), _layer_spec(layer, 1, D_MODEL, **once)]
    w_args = (w["wup"], w["fcw"], w["fcb"], w["wdn"], w["ln2_g"], w["ln2_b"])
    xp = pltpu.VMEM((FF_XP_SLOTS, SUBLANES + (0 if perm_q else tm), FF_CH), f32)
    if multi:
        nst = 2 * (tm // seq_len)
        inputs = (x, st) + w_args
        in_specs = [x_spec, pl.BlockSpec((None, nst, 2 * D_FF), lambda s, j: (layer, s * tiles + j, 0))] + w_specs
        sout_shape = jax.ShapeDtypeStruct((m, 2 * D_FF), f32)
        sout_spec = pl.BlockSpec((tm, 2 * D_FF), lambda s, j: (s * tiles + j, 0))
        scratch = [xp]
    else:
        inputs = (x,) + w_args
        in_specs = [x_spec] + w_specs
        sout_shape = jax.ShapeDtypeStruct((groups, carry_rows, 2 * D_FF), f32)
        sout_spec = pl.BlockSpec((None, carry_rows, 2 * D_FF), lambda s, j: (s, 0, 0))
        scratch = [xp, pltpu.VMEM((carry_rows, 2 * D_FF), f32)]
    return pl.pallas_call(
        kern,
        out_shape=(jax.ShapeDtypeStruct((m, D_MODEL), f32), sout_shape),
        grid=(groups, tiles),
        in_specs=in_specs,
        out_specs=(x_spec, sout_spec),
        scratch_shapes=scratch,
        compiler_params=pltpu.CompilerParams(dimension_semantics=("parallel", "arbitrary"),
                                             vmem_limit_bytes=56 * 1024 * 1024),
        name="ffn",
    )(*inputs)


def _pad_lanes(v, off, width=SMALL):
    out = jnp.zeros((v.shape[0], 1, width), f32)
    return out.at[:, 0, off:off + v.shape[1]].set(v.astype(f32))


def _prep_weights(w_in, ssd_conv_w, ssd_conv_b, ssd_dt_bias, ssd_a_log, ssd_d, ssd_norm_w, mlstm_gate_b,
                  mlstm_norm_w, w_branch_a, w_branch_b, w_out, ln1_g, ln1_b, ffn_w_up, ffn_conv_w,
                  ffn_conv_b, ffn_w_down, ln2_g, ln2_b):
    d = D_MODEL
    o_z, o_xbc, o_dt = 0, d, d + d + SSD_BC
    o_q = o_dt + SSD_HEADS
    o_if = o_q + 3 * d
    o_o = o_if + 2 * MLSTM_HEADS
    o_g = o_o + d
    w_t = jnp.swapaxes(w_in, 1, 2)
    cols = lambda a, n: w_t[:, a:a + n, :]
    zeros = lambda n: jnp.zeros((DEPTH, n, d), w_in.dtype)
    w32 = jnp.concatenate([cols(o_z, d), cols(o_o, d), cols(o_g, 2 * d), cols(o_xbc, d + SSD_BC),
                           cols(o_dt, SSD_HEADS), cols(o_if, 2 * MLSTM_HEADS),
                           zeros(P32_W - P32_SM_OFF - SSD_HEADS - 2 * MLSTM_HEADS)], axis=1).astype(bf16)
    e = (np.arange(SSD_HP)[None, :] // SSD_HEAD_DIM == np.arange(LANES)[:, None])
    bd = ((np.arange(SSD_GROUPS * SSD_STATE)[:, None] < SSD_STATE)
          == (np.arange(SSD_HP)[None, :] < SSD_HP // SSD_GROUPS))
    row = lambda a: a[:, None, :]
    return dict(
        w32=w32, wqkv=cols(o_q, 3 * d).astype(bf16),
        cwx=ssd_conv_w[:, :, :d], cbx=row(ssd_conv_b[:, :d]),
        cwb=ssd_conv_w[:, :, d:], cbb=row(ssd_conv_b[:, d:]),
        dtb=_pad_lanes(ssd_dt_bias, DT_OFF), alog=_pad_lanes(ssd_a_log, DT_OFF),
        dexp=row(jnp.repeat(ssd_d.astype(f32), SSD_HEAD_DIM, axis=1)), ssd_nw=row(ssd_norm_w),
        e=jnp.asarray(e, bf16), bd=jnp.asarray(bd, f32),
        gate_b=_pad_lanes(mlstm_gate_b, I_OFF), mlstm_nw=row(mlstm_norm_w),
        wa=w_branch_a.astype(bf16), wb=w_branch_b.astype(bf16), wo=w_out.astype(bf16),
        ln1_g=row(ln1_g), ln1_b=row(ln1_b),
        wup=ffn_w_up.astype(bf16), fcw=ffn_conv_w, fcb=row(ffn_conv_b), wdn=ffn_w_down.astype(bf16),
        ln2_g=row(ln2_g), ln2_b=row(ln2_b),
    )


class _Group:
    def __init__(self, batch, length, q, lr, gs, ssd_cps, mlstm_cps, proj_tm, merge_tm, ffn,
                 perm=False, lanes_ssd=False, mlstm_gs=None, ssd_gs=None, merge_parts=MERGE_PARTS):
        self.mlstm_gs = gs if mlstm_gs is None else mlstm_gs
        self.ssd_gs = gs if ssd_gs is None else ssd_gs
        self.merge_parts = merge_parts
        self.perm = perm
        self.lanes_ssd = lanes_ssd
        self.batch, self.length, self.q, self.lr, self.gs = batch, length, q, lr, gs
        self.ssd_cps, self.mlstm_cps = ssd_cps, mlstm_cps
        self.rows = batch * length
        self.proj_tm, self.merge_tm, self.ffn = proj_tm, merge_tm, ffn

    def cfg(self, name, layer):
        v = getattr(self, name)
        return v[layer] if isinstance(v, tuple) else v

    def tiling(self, cps, gs=None):
        rows = cps * self.q
        gs = self.gs if gs is None else gs
        steps = self.length // rows
        if gs == 1 or steps == 1:
            spec = lambda width, blk: pl.BlockSpec((gs * rows, width), lambda b, c: (b * steps + c, blk))
            return steps, spec, (lambda a: a), False
        spec = lambda width, blk: pl.BlockSpec((gs, rows, width), lambda b, c: (b, c, blk))
        return steps, spec, (lambda a: a.reshape(self.batch, self.length, a.shape[-1])), True


def _ssd(grp, p32, state, w, layer, prev):
    q, b, gs = grp.q, grp.batch, grp.cfg("ssd_gs", layer)
    has_state = state is not None
    cps = grp.cfg("ssd_cps", layer)
    steps, tile, view, tile3d = grp.tiling(cps, gs)
    kern = functools.partial(_ssd_kernel, q=q, lr=grp.lr, nc=steps, has_state=has_state, gs=gs, cps=cps,
                             perm=grp.perm, tile3d=tile3d)
    p32 = view(p32)
    inputs = [p32, p32, p32, p32]
    in_specs = [tile(D_MODEL, P32_Z), tile(D_MODEL, P32_XS),
                tile(SSD_BC, P32_BC_OFF // SSD_BC), tile(SMALL, P32_SM_OFF // SMALL)]
    if has_state:
        inputs += [state["csx"], state["csb"], state["h"]]
        in_specs += [_seq_spec(layer, gs, SSD_CONV - 1, D_MODEL), _seq_spec(layer, gs, SSD_CONV - 1, SSD_BC),
                     _seq_spec(layer, gs, SSD_HP, SSD_STATE)]
    inputs += [w["cwx"], w["cbx"], w["cwb"], w["cbb"], w["dtb"], w["alog"], w["dexp"], w["ssd_nw"], w["e"],
               w["bd"]]
    const = lambda *shape: pl.BlockSpec(shape, lambda b, c: (0,) * len(shape))
    in_specs += [_layer_spec(layer, SSD_CONV, D_MODEL), _layer_spec(layer, 1, D_MODEL),
                 _layer_spec(layer, SSD_CONV, SSD_BC), _layer_spec(layer, 1, SSD_BC),
                 _layer_spec(layer, 1, SMALL), _layer_spec(layer, 1, SMALL), _layer_spec(layer, 1, D_MODEL),
                 _layer_spec(layer, 1, D_MODEL), const(LANES, SSD_HP), const(SSD_GROUPS * SSD_STATE, SSD_HP)]
    ys, *new = _stacked_call(
        kern, name="ssd", grid=(b // gs, steps), inputs=inputs, in_specs=in_specs,
        out_shape=(jax.ShapeDtypeStruct((b, grp.length, D_MODEL) if tile3d else (grp.rows, D_MODEL), bf16),
                   jax.ShapeDtypeStruct((DEPTH, b, SSD_CONV - 1, D_MODEL), f32),
                   jax.ShapeDtypeStruct((DEPTH, b, SSD_CONV - 1, SSD_BC), f32),
                   jax.ShapeDtypeStruct((DEPTH, b, SSD_HP, SSD_STATE), f32)),
        out_specs=(tile(D_MODEL, 0), _seq_spec(layer, gs, SSD_CONV - 1, D_MODEL),
                   _seq_spec(layer, gs, SSD_CONV - 1, SSD_BC), _seq_spec(layer, gs, SSD_HP, SSD_STATE)),
        stacked={1: prev and prev[0], 2: prev and prev[1], 3: prev and prev[2]},
        scratch_shapes=[pltpu.VMEM((gs, SUBLANES + q, D_MODEL), f32), pltpu.VMEM((gs, SUBLANES + q, SSD_BC), f32),
                        pltpu.VMEM((gs, SSD_GROUPS * SSD_STATE, SSD_HP), f32), pltpu.VMEM((gs, q, D_MODEL), f32)],
        dimension_semantics=("parallel", "arbitrary"))
    return (ys.reshape(grp.rows, D_MODEL), *new)


def _mlstm(grp, qkv, p32, state, w, layer, prev):
    q, b, gs = grp.q, grp.batch, grp.mlstm_gs
    has_state = state is not None
    cps = grp.cfg("mlstm_cps", layer)
    steps, tile, view, tile3d = grp.tiling(cps, gs)
    kern = functools.partial(_mlstm_kernel, q=q, lr=grp.lr, nc=steps, has_state=has_state, gs=gs, cps=cps,
                             perm=grp.perm, tile3d=tile3d)
    hd = MLSTM_HEAD_DIM
    carried = not (has_state and steps == 1 and cps == 1)
    qkv, p32 = view(qkv), view(p32)
    inputs = [qkv, qkv, qkv, p32, p32]
    in_specs = [tile(D_MODEL, 0), tile(D_MODEL, 1), tile(D_MODEL, 2), tile(D_MODEL, P32_O),
                tile(SMALL, P32_SM_OFF // SMALL)]
    if has_state:
        inputs += [state["c"], state["n"], state["m"]]
        in_specs += [_seq_spec(layer, gs, MLSTM_HEADS, hd, hd), _seq_spec(layer, gs, MLSTM_HEADS, hd),
                     _seq_spec(layer, gs, 1, SMALL)]
    inputs += [w["gate_b"], w["mlstm_nw"]]
    in_specs += [_layer_spec(layer, 1, SMALL), _layer_spec(layer, 1, D_MODEL)]
    hm, *new = _stacked_call(
        kern, name="mlstm", grid=(b // gs, steps), inputs=inputs, in_specs=in_specs,
        out_shape=(jax.ShapeDtypeStruct((b, grp.length, D_MODEL) if tile3d else (grp.rows, D_MODEL), bf16),
                   jax.ShapeDtypeStruct((DEPTH, b, MLSTM_HEADS, hd, hd), f32),
                   jax.ShapeDtypeStruct((DEPTH, b, MLSTM_HEADS, hd), f32),
                   jax.ShapeDtypeStruct((DEPTH, b, 1, SMALL), f32)),
        out_specs=(tile(D_MODEL, 0), _seq_spec(layer, gs, MLSTM_HEADS, hd, hd),
                   _seq_spec(layer, gs, MLSTM_HEADS, hd), _seq_spec(layer, gs, 1, SMALL)),
        stacked={1: prev and prev[0], 2: prev and prev[1], 3: prev and prev[2]},
        scratch_shapes=[pltpu.VMEM((gs, MLSTM_HEADS, hd, hd) if carried else (gs, 1, SUBLANES, LANES), f32),
                        pltpu.VMEM((gs, MLSTM_HEADS, hd), f32), pltpu.VMEM((gs, 1, SMALL), f32)],
        dimension_semantics=("parallel", "arbitrary"))
    return (hm.reshape(grp.rows, D_MODEL), *new)


def _trunk(grp, x, state, w):
    ssd_out = mlstm_out = None
    ffn_out = []
    for layer in range(DEPTH):
        p32, qkv = _proj(x, w["w32"], w["wqkv"], layer, grp.cfg("proj_tm", layer))
        if grp.lanes_ssd:
            b, t = grp.batch, grp.length
            x_tm = x.reshape(b, t, D_MODEL).swapaxes(0, 1).reshape(t * b, D_MODEL)
            ys_t, *ssd_out = _ssd_lanes(_proj_t(x_tm, w["w32"], layer), state["cs_t"], state["h_lanes"], w, layer,
                                        ssd_out, steps=t, batch=b)
            ys = ys_t.reshape(D_MODEL, t, b).transpose(2, 1, 0).reshape(b * t, D_MODEL)
        else:
            ys, *ssd_out = _ssd(grp, p32, state, w, layer, ssd_out)
        hm, *mlstm_out = _mlstm(grp, qkv, p32, state, w, layer, mlstm_out)
        x1 = _merge(ys, hm, p32, x, w, layer, grp.cfg("merge_tm", layer), grp.cfg("merge_parts", layer))
        x, s_ffn = _ffn(x1, state["ffn"] if state is not None else None, w, layer, **grp.ffn)
        ffn_out.append(s_ffn)
    return x, ssd_out, mlstm_out, ffn_out


def _unpack_states(batch, ssd_out, mlstm_out):
    csx, csb, h = ssd_out
    c, n, m = mlstm_out
    return (h.reshape(DEPTH, batch, SSD_HEADS, SSD_HEAD_DIM, SSD_STATE),
            jnp.concatenate([csx, csb], axis=-1), c, n, m[:, :, 0, :MLSTM_HEADS])


def kernel(x_prompt, x_sample, state_ssd, state_ssd_conv, state_mlstm_c, state_mlstm_n, state_mlstm_m,
           state_ffn_conv, w_in, ssd_conv_w, ssd_conv_b, ssd_dt_bias, ssd_a_log, ssd_d, ssd_norm_w,
           mlstm_gate_b, mlstm_norm_w, w_branch_a, w_branch_b, w_out, ln1_g, ln1_b, ffn_w_up, ffn_conv_w,
           ffn_conv_b, ffn_w_down, ln2_g, ln2_b):
    w = _prep_weights(w_in, ssd_conv_w, ssd_conv_b, ssd_dt_bias, ssd_a_log, ssd_d, ssd_norm_w, mlstm_gate_b,
                      mlstm_norm_w, w_branch_a, w_branch_b, w_out, ln1_g, ln1_b, ffn_w_up, ffn_conv_w,
                      ffn_conv_b, ffn_w_down, ln2_g, ln2_b)
    keep = FFN_CONV - 1
    keep_ssd = SSD_CONV - 1

    bp, lp, _ = x_prompt.shape
    prompt = _Group(bp, lp, CHUNK, CHUNK, gs=1, ssd_cps=4, mlstm_cps=4, proj_tm=512, merge_tm=1024,
                    ffn=dict(groups=bp, tm=1024, seq_len=0, perm_q=CHUNK), perm=True, mlstm_gs=2, ssd_gs=2)
    per = CHUNK // SUBLANES
    xp_rows = x_prompt.reshape(bp, lp // CHUNK, SUBLANES, per, D_MODEL).swapaxes(2, 3)
    y_p, ssd_p, mlstm_p, ffn_p = _trunk(prompt, xp_rows.reshape(bp * lp, D_MODEL), None, w)
    y_p = y_p.reshape(bp, lp // CHUNK, per, SUBLANES, D_MODEL).swapaxes(2, 3)
    st_p = _unpack_states(bp, ssd_p, mlstm_p)
    ffn_conv_p = jnp.stack(ffn_p)[:, :, SUBLANES - 1::SUBLANES, :]

    bs, ls, _ = x_sample.shape
    s_rows = bs * ls
    sample = _Group(bs, ls, ls, ls, gs=8, ssd_cps=1, mlstm_cps=1, proj_tm=s_rows, merge_tm=s_rows,
                    ffn=dict(groups=1, tm=256, seq_len=ls), lanes_ssd=True)
    lane_b = lambda a: jnp.broadcast_to(a.astype(f32)[..., None], a.shape + (bs,))
    w.update(cw_b=lane_b(ssd_conv_w), cb_b=lane_b(ssd_conv_b), dtb_b=lane_b(ssd_dt_bias), alog_b=lane_b(ssd_a_log),
             dexp_b=lane_b(ssd_d), nw_b=lane_b(ssd_norm_w))
    s_state = dict(
        cs_t=jnp.transpose(state_ssd_conv, (0, 3, 2, 1)).reshape(DEPTH, D_MODEL + SSD_BC, keep_ssd * bs),
        h_lanes=jnp.transpose(state_ssd, (0, 2, 3, 4, 1)),
        c=state_mlstm_c, n=state_mlstm_n,
        m=jnp.pad(state_mlstm_m, ((0, 0), (0, 0), (0, SMALL - MLSTM_HEADS)))[:, :, None, :],
        ffn=state_ffn_conv.reshape(DEPTH, bs * keep, 2 * D_FF),
    )
    y_s, (cs_t, h_lanes), mlstm_s, ffn_s = _trunk(sample, x_sample.reshape(s_rows, D_MODEL), s_state, w)
    c_s, n_s, m_s = mlstm_s
    st_s = (jnp.transpose(h_lanes, (0, 4, 1, 2, 3)),
            jnp.transpose(cs_t.reshape(DEPTH, D_MODEL + SSD_BC, keep_ssd, bs), (0, 3, 2, 1)),
            c_s, n_s, m_s[:, :, 0, :MLSTM_HEADS])
    ffn_conv_s = jnp.stack([u.reshape(bs, ls, 2 * D_FF)[:, ls - keep:, :] for u in ffn_s])
    y_sample = y_s.reshape(bs, ls, D_MODEL)

    return (y_p.reshape(bp, lp, D_MODEL), y_sample, st_p[0], st_s[0], st_p[1], st_s[1], st_p[2], st_s[2],
            st_p[3], st_s[3], st_p[4], st_s[4], ffn_conv_p, ffn_conv_s)
```

```python
import functools
import itertools

import jax
import jax.numpy as jnp
import numpy as np
from jax import lax
from jax.experimental import pallas as pl
from jax.experimental.pallas import tpu as pltpu

f32 = jnp.float32
bf16 = jnp.bfloat16

D_MODEL = 1024
DEPTH = 2
SSD_HEADS = 16
SSD_HEAD_DIM = 64
SSD_STATE = 64
SSD_GROUPS = 2
SSD_CONV = 4
SSD_BC = 2 * SSD_GROUPS * SSD_STATE
SSD_HP = SSD_HEADS * SSD_HEAD_DIM
MLSTM_HEADS = 4
MLSTM_HEAD_DIM = 256
CHUNK = 128
D_FF = 2816
FFN_CONV = 3
ALPHA = (2 * DEPTH) ** 0.25
EPS = 1e-5

LANES = 128
SUBLANES = 8
SMALL = LANES
DT_OFF, I_OFF, F_OFF = 0, 16, 20
P32_Z, P32_O, P32_GA, P32_GB, P32_XS = 0, 1, 2, 3, 4
P32_BC_OFF = 5 * D_MODEL
P32_SM_OFF = P32_BC_OFF + SSD_BC
P32_W = P32_SM_OFF + 2 * SMALL
FF_BOUNDS = (0, 256, 1280, 2304, 2816)
MERGE_PARTS = 2
FF_UP_AHEAD = 2
FF_XP_SLOTS = 4
NEG_BIG = -1e30

NT_DIMS = (((1,), (1,)), ((), ()))
TN_DIMS = (((0,), (0,)), ((), ()))


def _dot(a, b):
    return jnp.dot(a, b, preferred_element_type=f32)


def _split3(x):
    hi = x.astype(bf16)
    r = x - hi.astype(f32)
    mid = r.astype(bf16)
    lo = (r - mid.astype(f32)).astype(bf16)
    return hi, mid, lo


def _dot01_rhs(x, e):
    hi, mid, lo = _split3(x)
    return _dot(hi, e) + _dot(mid, e) + _dot(lo, e)


def _dot01_lhs(t, x):
    hi, mid, lo = _split3(x)
    return _dot(t, hi) + _dot(t, mid) + _dot(t, lo)


def _softplus(x):
    return jnp.maximum(x, 0.0) + jnp.log1p(jnp.exp(-jnp.abs(x)))


def _silu(x):
    return x * jax.nn.sigmoid(x)


def _row_time(i, q, perm):
    if not perm:
        return i
    return (i & (SUBLANES - 1)) * (q // SUBLANES) + lax.shift_right_logical(i, SUBLANES.bit_length() - 1)


def _tri(q, perm=False):
    row = lax.broadcasted_iota(jnp.int32, (q, q), 0)
    col = lax.broadcasted_iota(jnp.int32, (q, q), 1)
    return _row_time(row, q, perm) >= _row_time(col, q, perm)


def _wrap_rows(cur_tail, prev_tail):
    out = []
    for i in range(cur_tail.shape[0] // SUBLANES):
        rows = slice(i * SUBLANES, (i + 1) * SUBLANES)
        first = lax.broadcasted_iota(jnp.int32, (SUBLANES, cur_tail.shape[1]), 0) == 0
        out.append(jnp.where(first, pltpu.roll(prev_tail[rows], 1, axis=0), pltpu.roll(cur_tail[rows], 1, axis=0)))
    return jnp.concatenate(out, axis=0)


def _shift_back(x, wrapped, j):
    n = j * SUBLANES
    return jnp.concatenate([wrapped[wrapped.shape[0] - n:], x[:x.shape[0] - n]], axis=0)


def _valid_rows(q, width, lr, is_last):
    row = lax.broadcasted_iota(jnp.int32, (q, width), 0)
    return row < jnp.where(is_last, lr, q)


def _layer_norm(r, g, b):
    mu = jnp.mean(r, axis=-1, keepdims=True)
    var = jnp.mean(jnp.square(r - mu), axis=-1, keepdims=True)
    return (r - mu) * lax.rsqrt(var + EPS) * g + b


def _layer_spec(layer, *shape, **kw):
    zeros = (0,) * len(shape)
    return pl.BlockSpec((None,) + shape, lambda *_: (layer,) + zeros, **kw)


def _seq_spec(layer, gs, *shape):
    zeros = (0,) * len(shape)
    return pl.BlockSpec((None, gs) + shape, lambda b, c: (layer, b) + zeros)


def _stacked_call(kern, *, name, grid, inputs, in_specs, out_shape, out_specs, stacked, scratch_shapes,
                  dimension_semantics, vmem_limit_bytes=None):
    prev = [(i, a) for i, a in sorted(stacked.items()) if a is not None]
    n_in = len(inputs)

    def body(*refs):
        kern(*refs[:n_in], *refs[n_in + len(prev):])

    return pl.pallas_call(
        body,
        out_shape=out_shape,
        grid=grid,
        in_specs=list(in_specs) + [pl.BlockSpec(memory_space=pl.ANY)] * len(prev),
        out_specs=out_specs,
        scratch_shapes=scratch_shapes,
        input_output_aliases={n_in + k: i for k, (i, _) in enumerate(prev)},
        compiler_params=pltpu.CompilerParams(dimension_semantics=dimension_semantics,
                                             vmem_limit_bytes=vmem_limit_bytes),
        name=name,
    )(*inputs, *[a for _, a in prev])


def _proj_kernel(x_ref, w32_ref, wqkv_ref, o32_ref, oqkv_ref):
    xb = x_ref[...].astype(bf16)
    o32_ref[...] = lax.dot_general(xb, w32_ref[...], NT_DIMS, preferred_element_type=f32)
    oqkv_ref[...] = lax.dot_general(xb, wqkv_ref[...], NT_DIMS, preferred_element_type=f32).astype(oqkv_ref.dtype)


def _proj(x, w32, wqkv, layer, tm):
    m, k = x.shape
    n32, nq = w32.shape[1], wqkv.shape[1]
    once = dict(pipeline_mode=pl.Buffered(1))
    return pl.pallas_call(
        _proj_kernel,
        out_shape=(jax.ShapeDtypeStruct((m, n32), f32), jax.ShapeDtypeStruct((m, nq), bf16)),
        grid=(m // tm,),
        in_specs=[pl.BlockSpec((tm, k), lambda i: (i, 0)),
                  _layer_spec(layer, n32, k, **once), _layer_spec(layer, nq, k, **once)],
        out_specs=(pl.BlockSpec((tm, n32), lambda i: (i, 0)), pl.BlockSpec((tm, nq), lambda i: (i, 0))),
        compiler_params=pltpu.CompilerParams(dimension_semantics=("parallel",),
                                             vmem_limit_bytes=56 * 1024 * 1024),
        name="proj",
    )(x, w32, wqkv)


class _Rows:
    def __init__(self, ref, start, n):
        self.ref, self.start, self.n, self.dtype = ref, start, n, ref.dtype

    def rows(self, off, n):
        return _Rows(self.ref, self.start + off, n)

    def _index(self, idx):
        cols = slice(None) if idx is Ellipsis else idx[1]
        return (slice(self.start, self.start + self.n), cols)

    def __getitem__(self, idx):
        return self.ref[self._index(idx)]

    def __setitem__(self, idx, value):
        self.ref[self._index(idx)] = value


def _per_sequence(seq_fn, refs, n_tile, n_state, n_param, gs, has_state, nc, rows, tile3d=False):
    n_state = n_state if has_state else 0
    tiles, refs = refs[:n_tile], refs[n_tile:]
    state, refs = refs[:n_state], refs[n_state:]
    params, (y_tile, *rest) = refs[:n_param], refs[n_param:]
    phases = []
    for g in range(gs):
        at = lambda group: tuple(r.at[g] for r in group)
        if tile3d:
            seq_rows = lambda group: tuple(_Rows(r.at[g], 0, rows) for r in group)
        else:
            seq_rows = lambda group: tuple(_Rows(r, g * rows, rows) for r in group)
        phases.append(seq_fn(*seq_rows(tiles), *at(state), *params, *seq_rows((y_tile,)), *at(rest)))
    c = pl.program_id(1)

    @pl.when(c == 0)
    def _():
        for init, _, _ in phases:
            init()

    for _ in itertools.zip_longest(*[body() for _, body, _ in phases]):
        pass

    @pl.when(c == nc - 1)
    def _():
        for _, _, final in phases:
            final()


def _ssd_kernel(*refs, q, lr, nc, has_state, gs, cps, perm, tile3d):
    seq = functools.partial(_ssd_seq, q=q, lr=lr, nc=nc, has_state=has_state, cps=cps, perm=perm)
    _per_sequence(seq, refs, 4, 3, 10, gs, has_state, nc, cps * q, tile3d)


def _ssd_seq(*refs, q, lr, nc, has_state, cps, perm):
    z_ref, xs_ref, bc_ref, sm_ref = refs[:4]
    refs = refs[4:]
    if has_state:
        csx_ref, csb_ref, h0_ref = refs[:3]
        refs = refs[3:]
    (cwx_ref, cbx_ref, cwb_ref, cbb_ref, dtb_ref, alog_ref, dexp_ref, nw_ref, e_ref, bd_ref,
     y_ref, ncsx_ref, ncsb_ref, hout_ref, xpx, xpb, ht, yb) = refs
    hdr = SUBLANES
    lo = hdr - (SSD_CONV - 1)
    n2 = SSD_GROUPS * SSD_STATE
    assert lr >= SSD_CONV - 1
    keep = SSD_CONV - 1
    assert not (perm and (has_state or lr != q))
    carried = [(i + 1) * SUBLANES - 1 for i in range(keep)]

    def init():
        if has_state:
            h_t = h0_ref[...].T
            ht[...] = jnp.where(bd_ref[...] > 0.5, jnp.concatenate([h_t, h_t], axis=0), 0.0)
            xpx[lo:hdr, :] = csx_ref[...]
            xpb[lo:hdr, :] = csb_ref[...]
        else:
            ht[...] = jnp.zeros_like(ht)
            rows = slice(0, keep * SUBLANES) if perm else slice(lo, hdr)
            xpx[rows, :] = jnp.zeros((rows.stop - rows.start, SSD_HP), f32)
            xpb[rows, :] = jnp.zeros((rows.stop - rows.start, SSD_BC), f32)

    def final():
        if perm:
            for i, r in enumerate(carried):
                ncsx_ref[i:i + 1, :] = xpx[r:r + 1, :]
                ncsb_ref[i:i + 1, :] = xpb[r:r + 1, :]
        else:
            ncsx_ref[...] = xpx[lo + lr:hdr + lr, :]
            ncsb_ref[...] = xpb[lo + lr:hdr + lr, :]
        h_new = ht[...]
        hout_ref[...] = (h_new[:SSD_STATE, :] + h_new[SSD_STATE:, :]).T

    def body():
        for k in range(cps):
            sub = lambda r: r.rows(k * q, q)
            is_last = (pl.program_id(1) == nc - 1) if k == cps - 1 else False
            yield from _ssd_body(sub(z_ref), sub(xs_ref), sub(bc_ref), sub(sm_ref), cwx_ref, cbx_ref, cwb_ref,
                                 cbb_ref, dtb_ref, alog_ref, dexp_ref, nw_ref, e_ref, bd_ref, sub(y_ref),
                                 xpx, xpb, ht, yb, q=q, lr=lr, is_last=is_last, perm=perm)

    return init, body, final


def _ssd_body(z_ref, xs_ref, bc_ref, sm_ref, cwx_ref, cbx_ref, cwb_ref, cbb_ref, dtb_ref, alog_ref,
              dexp_ref, nw_ref, e_ref, bd_ref, y_ref, xpx, xpb, ht, yb, *, q, lr, is_last, perm):
    hdr = SUBLANES
    lo = hdr - (SSD_CONV - 1)
    n2 = SSD_GROUPS * SSD_STATE
    block_diag = bd_ref[...] > 0.5

    dt = _softplus(sm_ref[...] + dtb_ref[...])
    if lr < q:
        dt = jnp.where(_valid_rows(q, SMALL, lr, is_last), dt, 0.0)
    a = -jnp.exp(alog_ref[...])
    d_a = dt * a
    causal = _tri(q, perm)
    tril = jnp.where(causal, 1.0, 0.0).astype(bf16)
    e = e_ref[...]
    acs = _dot01_lhs(tril, d_a)
    dt_x = _dot01_rhs(dt, e)
    yield

    if perm:
        keep_rows = (SSD_CONV - 1) * SUBLANES

        def conv(xp, x_ref, w_ref, b_ref):
            w = w_ref[...]
            x = x_ref[...]
            wrapped = _wrap_rows(x[q - keep_rows:, :], xp[0:keep_rows, :])
            acc = _shift_back(x, wrapped, SSD_CONV - 1) * w[0:1, :]
            for j in range(1, SSD_CONV - 1):
                acc = acc + _shift_back(x, wrapped, SSD_CONV - 1 - j) * w[j:j + 1, :]
            acc = acc + x * w[SSD_CONV - 1:SSD_CONV, :]
            xp[0:keep_rows, :] = x[q - keep_rows:, :]
            return acc + b_ref[...]
    else:
        xpx[hdr:hdr + q, :] = xs_ref[...]
        xpb[hdr:hdr + q, :] = bc_ref[...]

        def conv(xp, x_ref, w_ref, b_ref):
            w = w_ref[...]
            acc = xp[lo:lo + q, :] * w[0:1, :]
            for j in range(1, SSD_CONV):
                acc = acc + xp[lo + j:lo + j + q, :] * w[j:j + 1, :]
            return acc + b_ref[...]

    cb = conv(xpb, bc_ref, cwb_ref, cbb_ref)
    bcv = _silu(cb)
    bm = bcv[:, :n2].astype(bf16)
    cm = bcv[:, n2:]
    lane_g0 = lax.broadcasted_iota(jnp.int32, (q, n2), 1) < SSD_STATE
    acs_t = acs.T
    acs_x = _dot01_rhs(acs, e)
    yield
    cbms = [lax.dot_general(jnp.where(lane_g0 if g == 0 else jnp.logical_not(lane_g0), cm, 0.0).astype(bf16),
                            bm, NT_DIMS, preferred_element_type=f32) for g in range(SSD_GROUPS)]
    h_prev = ht[...]
    y_off = _dot(cm.astype(bf16), h_prev.astype(bf16))
    cx = conv(xpx, xs_ref, cwx_ref, cbx_ref)
    if not perm:
        tail_x = xpx[lo + q:hdr + q, :]
        tail_b = xpb[lo + q:hdr + q, :]
        xpx[lo:hdr, :] = tail_x
        xpb[lo:hdr, :] = tail_b
    yield
    xs = _silu(cx)
    last_x = acs_x[q - 1:q, :]
    xdt = xs * dt_x
    xdt_b = xdt.astype(bf16)
    yield
    lane_lo = lax.broadcasted_iota(jnp.int32, (q, LANES), 1) < SSD_HEAD_DIM
    heads_per_group = SSD_HEADS // SSD_GROUPS
    decays = [jnp.exp(jnp.where(causal, acs[:, hh:hh + 1] - acs_t[hh:hh + 1, :], -jnp.inf))
              for hh in range(SSD_HEADS)]
    yield
    weights = [(cbms[hh // heads_per_group] * decays[hh]).astype(bf16) for hh in range(SSD_HEADS)]
    xdtw = (xdt * jnp.exp(last_x - acs_x)).astype(bf16)
    yield
    ys = [_dot(weights[hh], xdt_b[:, (hh // 2) * LANES:(hh // 2 + 1) * LANES]) for hh in range(SSD_HEADS)]
    upd = lax.dot_general(bm, xdtw, TN_DIMS, preferred_element_type=f32)
    yield
    for p in range(SSD_HEADS // 2):
        yb[:, p * LANES:(p + 1) * LANES] = jnp.where(lane_lo, ys[2 * p], ys[2 * p + 1])
    ht[...] = jnp.exp(last_x) * h_prev + jnp.where(block_diag, upd, 0.0)
    yield
    y = yb[...] + y_off * jnp.exp(acs_x) + dexp_ref[...] * xs
    y = y * _silu(z_ref[...])
    yield
    y = y * lax.rsqrt(jnp.mean(jnp.square(y), axis=-1, keepdims=True) + EPS) * nw_ref[...]
    y_ref[...] = y.astype(y_ref.dtype)


PT_ROWS = 5 * 512
PT_XS, PT_BC, PT_SM = D_MODEL, 2 * D_MODEL, 2 * D_MODEL + SSD_BC


def _proj_t_kernel(x_ref, w_ref, o_ref, xb):
    @pl.when(pl.program_id(0) == 0)
    def _():
        xb[...] = x_ref[...].astype(bf16)

    o_ref[...] = lax.dot_general(w_ref[...], xb[...], NT_DIMS, preferred_element_type=f32)


def _proj_t(x_tm, w32, layer):
    m, k = x_tm.shape
    tn = 512
    xs_blk = P32_XS * D_MODEL // tn
    return pl.pallas_call(
        _proj_t_kernel,
        out_shape=jax.ShapeDtypeStruct((PT_ROWS, m), f32),
        grid=(PT_ROWS // tn,),
        in_specs=[pl.BlockSpec((m, k), lambda j: (0, 0)),
                  pl.BlockSpec((None, tn, k), lambda j: (layer, jnp.where(j < D_MODEL // tn, j, j + xs_blk - D_MODEL // tn), 0))],
        out_specs=pl.BlockSpec((tn, m), lambda j: (j, 0)),
        scratch_shapes=[pltpu.VMEM((m, k), bf16)],
        compiler_params=pltpu.CompilerParams(dimension_semantics=("arbitrary",)),
        name="proj_t",
    )(x_tm, w32)


def _ssd_lanes_kernel(pt_ref, cst_ref, h0_ref, cw_ref, cb_ref, dtb_ref, alog_ref, dexp_ref, nw_ref,
                      y_ref, ncs_ref, hout_ref, xc, dts, decs, ysc, *, steps, batch):
    hd = pl.program_id(0)
    n_ch = D_MODEL + SSD_BC
    keep = SSD_CONV - 1
    lanes = lambda t: slice(t * batch, (t + 1) * batch)

    @pl.when(hd == 0)
    def _():
        for t in range(steps):
            acc = None
            for j in range(SSD_CONV):
                i = t + j
                src = cst_ref[:, lanes(i)] if i < keep else pt_ref[PT_XS:PT_XS + n_ch, lanes(i - keep)]
                term = src * cw_ref[j]
                acc = term if acc is None else acc + term
            xc[:, lanes(t)] = _silu(acc + cb_ref[...])
        ncs_ref[...] = pt_ref[PT_XS:PT_XS + n_ch, (steps - keep) * batch:steps * batch]
        dt = _softplus(pt_ref[PT_SM:PT_SM + SSD_HEADS, :] + jnp.concatenate([dtb_ref[...]] * steps, axis=1))
        dts[...] = dt
        decs[...] = jnp.exp(dt * jnp.concatenate([-jnp.exp(alog_ref[...])] * steps, axis=1))

    grp_row = (hd // (SSD_HEADS // SSD_GROUPS)) * SSD_STATE
    xh = xc[pl.ds(pl.multiple_of(hd * SSD_HEAD_DIM, SSD_HEAD_DIM), SSD_HEAD_DIM), :]
    bh = xc[pl.ds(pl.multiple_of(D_MODEL + grp_row, SSD_STATE), SSD_STATE), :]
    ch = xc[pl.ds(pl.multiple_of(D_MODEL + SSD_GROUPS * SSD_STATE + grp_row, SSD_STATE), SSD_STATE), :]
    dth = dts[pl.ds(hd, 1), :]
    dech = decs[pl.ds(hd, 1), :]
    d_skip = dexp_ref[pl.ds(hd, 1), :]
    y_rows = [[] for _ in range(steps)]
    for p in range(SSD_HEAD_DIM):
        h = h0_ref[p]
        for t in range(steps):
            x_row = xh[p:p + 1, lanes(t)]
            h = dech[:, lanes(t)] * h + (x_row * dth[:, lanes(t)]) * bh[:, lanes(t)]
            y_rows[t].append(jnp.sum(ch[:, lanes(t)] * h, axis=0, keepdims=True) + d_skip * x_row)
        hout_ref[p] = h
    rows = pl.ds(pl.multiple_of(hd * SSD_HEAD_DIM, SSD_HEAD_DIM), SSD_HEAD_DIM)
    for t in range(steps):
        ysc[rows, lanes(t)] = jnp.concatenate(y_rows[t], axis=0)

    @pl.when(hd == SSD_HEADS - 1)
    def _():
        y = ysc[...] * _silu(pt_ref[0:D_MODEL, :])
        y = y * lax.rsqrt(jnp.mean(jnp.square(y), axis=0, keepdims=True) + EPS)
        y_ref[...] = (y * jnp.concatenate([nw_ref[...]] * steps, axis=1)).astype(y_ref.dtype)


def _ssd_lanes(pt, cst, h0, w, layer, prev, *, steps, batch):
    n_ch = D_MODEL + SSD_BC
    tb = steps * batch
    keep = SSD_CONV - 1
    assert steps >= keep and batch % LANES == 0
    kern = functools.partial(_ssd_lanes_kernel, steps=steps, batch=batch)
    full = lambda *shape: pl.BlockSpec(shape, lambda hd: (0,) * len(shape))
    hblock = pl.BlockSpec((None, None, SSD_HEAD_DIM, SSD_STATE, batch), lambda hd: (layer, hd, 0, 0, 0))
    return _stacked_call(
        kern, name="ssd_lanes", grid=(SSD_HEADS,),
        inputs=[pt, cst, h0, w["cw_b"], w["cb_b"], w["dtb_b"], w["alog_b"], w["dexp_b"], w["nw_b"]],
        in_specs=[full(PT_ROWS, tb), _layer_spec(layer, n_ch, keep * batch), hblock,
                  _layer_spec(layer, SSD_CONV, n_ch, batch), _layer_spec(layer, n_ch, batch),
                  _layer_spec(layer, SSD_HEADS, batch), _layer_spec(layer, SSD_HEADS, batch),
                  _layer_spec(layer, SSD_HEADS, batch), _layer_spec(layer, D_MODEL, batch)],
        out_shape=(jax.ShapeDtypeStruct((D_MODEL, tb), bf16),
                   jax.ShapeDtypeStruct((DEPTH, n_ch, keep * batch), f32),
                   jax.ShapeDtypeStruct((DEPTH, SSD_HEADS, SSD_HEAD_DIM, SSD_STATE, batch), f32)),
        out_specs=(full(D_MODEL, tb), _layer_spec(layer, n_ch, keep * batch), hblock),
        stacked={1: prev and prev[0], 2: prev and prev[1]},
        scratch_shapes=[pltpu.VMEM((n_ch, tb), f32), pltpu.VMEM((SSD_HEADS, tb), f32),
                        pltpu.VMEM((SSD_HEADS, tb), f32), pltpu.VMEM((D_MODEL, tb), f32)],
        dimension_semantics=("arbitrary",))


def _mlstm_kernel(*refs, q, lr, nc, has_state, gs, cps, perm, tile3d):
    seq = functools.partial(_mlstm_seq, q=q, lr=lr, nc=nc, has_state=has_state, cps=cps, perm=perm)
    _per_sequence(seq, refs, 5, 3, 2, gs, has_state, nc, cps * q, tile3d)


def _mlstm_seq(*refs, q, lr, nc, has_state, cps, perm):
    q_ref, k_ref, v_ref, o_ref, sm_ref = refs[:5]
    refs = refs[5:]
    if has_state:
        c0_ref, n0_ref, m0_ref = refs[:3]
        refs = refs[3:]
    gb_ref, nw_ref, h_ref, cout_ref, nout_ref, mout_ref, cs, ns, ms = refs
    direct = has_state and nc == 1 and cps == 1

    def init():
        if direct:
            return
        if has_state:
            cs[...] = c0_ref[...]
            ns[...] = n0_ref[...]
            ms[...] = m0_ref[...]
        else:
            cs[...] = jnp.zeros_like(cs)
            ns[...] = jnp.zeros_like(ns)
            ms[...] = jnp.zeros_like(ms)

    def final():
        if direct:
            return
        cout_ref[...] = cs[...]
        nout_ref[...] = ns[...]
        mout_ref[...] = ms[...]

    def body():
        chunks = []
        for k in range(cps):
            sub = lambda r, k=k: r.rows(k * q, q)
            is_last = (pl.program_id(1) == nc - 1) if k == cps - 1 else False
            src = (c0_ref, n0_ref, m0_ref) if direct else (cs, ns, ms)
            dst = (cout_ref, nout_ref, mout_ref) if direct else (cs, ns, ms)
            chunks.append(_mlstm_body(sub(q_ref), sub(k_ref), sub(v_ref), sub(o_ref), sub(sm_ref), gb_ref, nw_ref,
                                      sub(h_ref), src, dst, q=q, lr=lr, is_last=is_last, perm=perm))
        yield from _staggered(chunks, MLSTM_STATE_STAGES)

    return init, body, final


MLSTM_STATE_STAGES = 8


def _staggered(gens, skew):
    done = [False] * len(gens)
    t = 0
    while not all(done):
        for i, g in enumerate(gens):
            if done[i] or t < i * skew:
                continue
            try:
                next(g)
            except StopIteration:
                done[i] = True
        t += 1
        yield


def _mlstm_body(q_ref, k_ref, v_ref, o_ref, sm_ref, gb_ref, nw_ref, h_ref, src, dst, *, q, lr, is_last, perm):
    c_src, n_src, m_src = src
    c_dst, n_dst, m_dst = dst
    sm = sm_ref[...] + gb_ref[...]
    logf = -_softplus(-sm)
    ipre = sm
    if lr < q:
        valid = _valid_rows(q, SMALL, lr, is_last)
        logf = jnp.where(valid, logf, 0.0)
        ipre = jnp.where(valid, ipre, NEG_BIG)
    causal = _tri(q, perm)
    tril = jnp.where(causal, 1.0, 0.0).astype(bf16)
    yield
    bcum = _dot01_lhs(tril, logf)
    ipre_t = ipre.T
    yield
    bcum_t = bcum.T
    lane = lax.broadcasted_iota(jnp.int32, (1, SMALL), 1)
    k_scale = MLSTM_HEAD_DIM ** -0.5

    heads = range(MLSTM_HEADS)
    sls = [slice(h * MLSTM_HEAD_DIM, (h + 1) * MLSTM_HEAD_DIM) for h in heads]
    q_all, k_all, v_all, o_all = q_ref[...], k_ref[...], v_ref[...], o_ref[...]
    qs = [q_all[:, sl] for sl in sls]
    ks = [k_all[:, sl] * k_scale for sl in sls]
    vs = [v_all[:, sl] for sl in sls]
    b_cols = [bcum[:, F_OFF + h:F_OFF + h + 1] for h in heads]
    i_cols = [ipre[:, I_OFF + h:I_OFF + h + 1] for h in heads]
    dmats = [jnp.where(causal, b_cols[h] - bcum_t[F_OFF + h:F_OFF + h + 1, :] + ipre_t[I_OFF + h:I_OFF + h + 1, :],
                       -jnp.inf) for h in heads]
    yield
    qk = [lax.dot_general(qs[h], ks[h], NT_DIMS, preferred_element_type=f32) for h in heads]
    d_max = [jnp.max(dmats[h], axis=-1, keepdims=True) for h in heads]
    yield
    n_all = n_src[...]
    m_all = m_src[...]
    m_new = m_all
    cs_in = [c_src[h] for h in heads]
    m_prevs = [m_all[:, h:h + 1] for h in heads]
    qc = [_dot(qs[h], cs_in[h].astype(bf16)) for h in heads]
    inters = [b_cols[h] + m_prevs[h] for h in heads]
    m_ts = [jnp.maximum(inters[h], d_max[h]) for h in heads]
    yield
    w_inters = [jnp.exp(inters[h] - m_ts[h]) for h in heads]
    ss = [qk[h] * jnp.exp(dmats[h] - m_ts[h]) for h in heads]
    yield
    sv = [_dot(ss[h].astype(bf16), vs[h]) for h in heads]
    m_ends = [m_ts[h][q - 1:q, :] for h in heads]
    b_lasts = [b_cols[h][q - 1:q, :] for h in heads]
    kws = [ks[h].astype(f32) * jnp.exp(b_lasts[h] - b_cols[h] + i_cols[h] - m_ends[h]) for h in heads]
    yield
    kv = [lax.dot_general(kws[h].astype(bf16), vs[h], TN_DIMS, preferred_element_type=f32) for h in heads]
    qns = [jnp.sum(qs[h].astype(f32) * n_all[h:h + 1, :], axis=-1, keepdims=True) for h in heads]
    yield
    dens = [jnp.sum(ss[h], axis=-1, keepdims=True) + w_inters[h] * qns[h] for h in heads]
    yield
    hvs = [(sv[h] + w_inters[h] * qc[h]) / jnp.maximum(jnp.abs(dens[h]), jnp.exp(-m_ts[h])) for h in heads]
    yield
    rms = [lax.rsqrt(jnp.mean(jnp.square(hvs[h]), axis=-1, keepdims=True) + EPS) for h in heads]
    yield
    h_new = [(hvs[h] * rms[h] * nw_ref[:, sls[h]] * jax.nn.sigmoid(o_all[:, sls[h]])).astype(h_ref.dtype)
             for h in heads]
    w_cs = [jnp.exp(b_lasts[h] + m_prevs[h] - m_ends[h]) for h in heads]
    yield
    for h in heads:
        c_dst[h] = w_cs[h] * cs_in[h] + kv[h]
        m_new = jnp.where(lane == h, m_ends[h], m_new)
    h_ref[...] = jnp.concatenate(h_new, axis=1)
    n_dst[...] = jnp.concatenate(
        [w_cs[h] * n_all[h:h + 1, :] + jnp.sum(kws[h], axis=0, keepdims=True) for h in heads], axis=0)
    m_dst[...] = m_new


def _merge_kernel(ys_ref, hm_ref, ga_ref, gb_ref, x_ref, wa_ref, wb_ref, wo_ref, g_ref, b_ref, o_ref, *, parts):
    tm = x_ref.shape[0]
    rows = [pl.ds(i * (tm // parts), tm // parts) for i in range(parts)]
    br = [(_dot(ys_ref[r, :], wa_ref[...]), _dot(hm_ref[r, :], wb_ref[...])) for r in rows]
    merged = [(jax.nn.sigmoid(ga_ref[r, :]) * a + jax.nn.sigmoid(gb_ref[r, :]) * b).astype(bf16)
              for r, (a, b) in zip(rows, br)]
    mix = [_dot(m, wo_ref[...]) for m in merged]
    for r, m in zip(rows, mix):
        o_ref[r, :] = _layer_norm(ALPHA * x_ref[r, :] + m, g_ref[...], b_ref[...])


def _merge(ys, hm, p32, x, w, layer, tm, parts=MERGE_PARTS):
    m = x.shape[0]
    row = lambda blk: pl.BlockSpec((tm, D_MODEL), lambda i: (i, blk))
    return pl.pallas_call(
        functools.partial(_merge_kernel, parts=parts),
        out_shape=jax.ShapeDtypeStruct((m, D_MODEL), f32),
        grid=(m // tm,),
        in_specs=[row(0), row(0), row(P32_GA), row(P32_GB), row(0),
                  _layer_spec(layer, D_MODEL, D_MODEL), _layer_spec(layer, D_MODEL, D_MODEL),
                  _layer_spec(layer, D_MODEL, D_MODEL), _layer_spec(layer, 1, D_MODEL),
                  _layer_spec(layer, 1, D_MODEL)],
        out_specs=row(0),
        compiler_params=pltpu.CompilerParams(dimension_semantics=("parallel",)),
        name="merge",
    )(ys, hm, p32, p32, x, w["wa"], w["wb"], w["wo"], w["ln1_g"], w["ln1_b"])


def _ffn_kernel(*refs, tm, seq_len, perm_q, ahead_n, bounds):
    multi = seq_len > 0
    keep_rows = (FFN_CONV - 1) * SUBLANES
    n_chunks = len(bounds) - 1
    if multi:
        x_ref, st_ref, wup_ref, cw_ref, cb_ref, wdn_ref, g_ref, b_ref, o_ref, sout_ref, xp = refs
    else:
        x_ref, wup_ref, cw_ref, cb_ref, wdn_ref, g_ref, b_ref, o_ref, sout_ref, xp, carry = refs
        @pl.when(pl.program_id(1) == 0)
        def _():
            carry[...] = jnp.zeros_like(carry)

    hdr = SUBLANES
    x = x_ref[...]
    xb = x.astype(bf16)
    if multi:
        assert seq_len & (seq_len - 1) == 0
        nseq = tm // seq_len
        t = lax.broadcasted_iota(jnp.int32, (tm, 1), 0) & (seq_len - 1)
        row = lax.broadcasted_iota(jnp.int32, (tm, 2 * nseq), 0)
        col = lax.broadcasted_iota(jnp.int32, (tm, 2 * nseq), 1)
        t_sel = row & (seq_len - 1)
        seq0 = lax.shift_right_logical(row - t_sel, (seq_len // 2).bit_length() - 1)
        sel_p2 = jnp.where(col == seq0 + t_sel, jnp.where(t_sel < 2, 1.0, 0.0), 0.0).astype(bf16)
        sel_p1 = jnp.where(col == seq0 + 1, jnp.where(t_sel == 0, 1.0, 0.0), 0.0).astype(bf16)
        xp[:, 0:hdr, :] = jnp.zeros((FF_XP_SLOTS, hdr, xp.shape[2]), f32)

    def cols_of(c, part):
        return slice(part * D_FF + bounds[c], part * D_FF + bounds[c + 1])

    def up(c):
        return [_dot(xb, wup_ref[:, cols_of(c, part)]) for part in range(2)]

    def conv_act(c, us):
        halves = []
        for part, u in enumerate(us):
            cols = cols_of(c, part)
            slot = (2 * c + part) % FF_XP_SLOTS
            if perm_q:
                prev = carry[:, cols]
                p1, p2 = [], []
                for kk in range(tm // perm_q):
                    uc = u[kk * perm_q:(kk + 1) * perm_q, :]
                    wrapped = _wrap_rows(uc[perm_q - keep_rows:, :], prev)
                    p1.append(_shift_back(uc, wrapped, 1))
                    p2.append(_shift_back(uc, wrapped, 2))
                    prev = uc[perm_q - keep_rows:, :]
                carry[:, cols] = prev
                p1, p2 = jnp.concatenate(p1, axis=0), jnp.concatenate(p2, axis=0)
                w = cw_ref[:, cols]
                halves.append(p2 * w[0:1, :] + p1 * w[1:2, :] + u * w[2:3, :] + cb_ref[:, cols])
                continue
            wd = slice(0, u.shape[1])
            xp[slot, hdr:hdr + tm, wd] = u
            if multi:
                sout_ref[:, cols] = u
                st = st_ref[:, cols]
                p1 = jnp.where(t == 0, _dot01_lhs(sel_p1, st), xp[slot, hdr - 1:hdr - 1 + tm, wd])
                p2 = jnp.where(t < 2, _dot01_lhs(sel_p2, st), xp[slot, hdr - 2:hdr - 2 + tm, wd])
            else:
                xp[slot, 0:hdr, wd] = carry[:, cols]
                p1 = xp[slot, hdr - 1:hdr - 1 + tm, wd]
                p2 = xp[slot, hdr - 2:hdr - 2 + tm, wd]
                carry[:, cols] = u[tm - hdr:tm, :]
            w = cw_ref[:, cols]
            halves.append(p2 * w[0:1, :] + p1 * w[1:2, :] + u * w[2:3, :] + cb_ref[:, cols])
        return (_silu(halves[0]) * halves[1]).astype(bf16)

    acc = None
    ahead = [up(c) for c in range(min(ahead_n, n_chunks))]
    pending = None
    for c in range(n_chunks):
        if c + ahead_n < n_chunks:
            ahead.append(up(c + ahead_n))
        if pending is not None:
            d = _dot(pending, wdn_ref[bounds[c - 1]:bounds[c], :])
            acc = d if acc is None else acc + d
        pending = conv_act(c, ahead.pop(0))
    acc = acc + _dot(pending, wdn_ref[bounds[n_chunks - 1]:bounds[n_chunks], :])

    if not multi:
        sout_ref[...] = carry[...]
    o_ref[...] = _layer_norm(ALPHA * x + acc, g_ref[...], b_ref[...])


def _ffn(x, st, w, layer, *, groups, tm, seq_len, perm_q=0, ahead=FF_UP_AHEAD, bounds=FF_BOUNDS):
    m = x.shape[0]
    bounds = bounds[layer] if isinstance(bounds[0], tuple) else bounds
    chunk_w = max(b - a for a, b in zip(bounds, bounds[1:]))
    tiles = m // (groups * tm)
    multi = seq_len > 0
    kern = functools.partial(_ffn_kernel, tm=tm, seq_len=seq_len, perm_q=perm_q, ahead_n=ahead, bounds=bounds)
    carry_rows = (FFN_CONV - 1) * SUBLANES if perm_q else SUBLANES
    once = dict(pipeline_mode=pl.Buffered(1))
    x_spec = pl.BlockSpec((tm, D_MODEL), lambda s, j: (s * tiles + j, 0))
    w_specs = [_layer_spec(layer, D_MODEL, 2 * D_FF, **once), _layer_spec(layer, FFN_CONV, 2 * D_FF, **once),
               _layer_spec(layer, 1, 2 * D_FF, **once), _layer_spec(layer, D_FF, D_MODEL, **once),
               _layer_spec(layer, 1, D_MODEL, **once), _layer_spec(layer, 1, D_MODEL, **once)]
    w_args = (w["wup"], w["fcw"], w["fcb"], w["wdn"], w["ln2_g"], w["ln2_b"])
    xp = pltpu.VMEM((FF_XP_SLOTS, SUBLANES + (0 if perm_q else tm), chunk_w), f32)
    if multi:
        nst = 2 * (tm // seq_len)
        inputs = (x, st) + w_args
        in_specs = [x_spec, pl.BlockSpec((None, nst, 2 * D_FF), lambda s, j: (layer, s * tiles + j, 0))] + w_specs
        sout_shape = jax.ShapeDtypeStruct((m, 2 * D_FF), f32)
        sout_spec = pl.BlockSpec((tm, 2 * D_FF), lambda s, j: (s * tiles + j, 0))
        scratch = [xp]
    else:
        inputs = (x,) + w_args
        in_specs = [x_spec] + w_specs
        sout_shape = jax.ShapeDtypeStruct((groups, carry_rows, 2 * D_FF), f32)
        sout_spec = pl.BlockSpec((None, carry_rows, 2 * D_FF), lambda s, j: (s, 0, 0))
        scratch = [xp, pltpu.VMEM((carry_rows, 2 * D_FF), f32)]
    return pl.pallas_call(
        kern,
        out_shape=(jax.ShapeDtypeStruct((m, D_MODEL), f32), sout_shape),
        grid=(groups, tiles),
        in_specs=in_specs,
        out_specs=(x_spec, sout_spec),
        scratch_shapes=scratch,
        compiler_params=pltpu.CompilerParams(dimension_semantics=("parallel", "arbitrary"),
                                             vmem_limit_bytes=56 * 1024 * 1024),
        name="ffn",
    )(*inputs)


def _pad_lanes(v, off, width=SMALL):
    out = jnp.zeros((v.shape[0], 1, width), f32)
    return out.at[:, 0, off:off + v.shape[1]].set(v.astype(f32))


def _prep_weights(w_in, ssd_conv_w, ssd_conv_b, ssd_dt_bias, ssd_a_log, ssd_d, ssd_norm_w, mlstm_gate_b,
                  mlstm_norm_w, w_branch_a, w_branch_b, w_out, ln1_g, ln1_b, ffn_w_up, ffn_conv_w,
                  ffn_conv_b, ffn_w_down, ln2_g, ln2_b):
    d = D_MODEL
    o_z, o_xbc, o_dt = 0, d, d + d + SSD_BC
    o_q = o_dt + SSD_HEADS
    o_if = o_q + 3 * d
    o_o = o_if + 2 * MLSTM_HEADS
    o_g = o_o + d
    w_t = jnp.swapaxes(w_in, 1, 2)
    cols = lambda a, n: w_t[:, a:a + n, :]
    zeros = lambda n: jnp.zeros((DEPTH, n, d), w_in.dtype)
    w32 = jnp.concatenate([cols(o_z, d), cols(o_o, d), cols(o_g, 2 * d), cols(o_xbc, d + SSD_BC),
                           cols(o_dt, SSD_HEADS), cols(o_if, 2 * MLSTM_HEADS),
                           zeros(P32_W - P32_SM_OFF - SSD_HEADS - 2 * MLSTM_HEADS)], axis=1).astype(bf16)
    e = (np.arange(SSD_HP)[None, :] // SSD_HEAD_DIM == np.arange(LANES)[:, None])
    bd = ((np.arange(SSD_GROUPS * SSD_STATE)[:, None] < SSD_STATE)
          == (np.arange(SSD_HP)[None, :] < SSD_HP // SSD_GROUPS))
    row = lambda a: a[:, None, :]
    return dict(
        w32=w32, wqkv=cols(o_q, 3 * d).astype(bf16),
        cwx=ssd_conv_w[:, :, :d], cbx=row(ssd_conv_b[:, :d]),
        cwb=ssd_conv_w[:, :, d:], cbb=row(ssd_conv_b[:, d:]),
        dtb=_pad_lanes(ssd_dt_bias, DT_OFF), alog=_pad_lanes(ssd_a_log, DT_OFF),
        dexp=row(jnp.repeat(ssd_d.astype(f32), SSD_HEAD_DIM, axis=1)), ssd_nw=row(ssd_norm_w),
        e=jnp.asarray(e, bf16), bd=jnp.asarray(bd, f32),
        gate_b=_pad_lanes(mlstm_gate_b, I_OFF), mlstm_nw=row(mlstm_norm_w),
        wa=w_branch_a.astype(bf16), wb=w_branch_b.astype(bf16), wo=w_out.astype(bf16),
        ln1_g=row(ln1_g), ln1_b=row(ln1_b),
        wup=ffn_w_up.astype(bf16), fcw=ffn_conv_w, fcb=row(ffn_conv_b), wdn=ffn_w_down.astype(bf16),
        ln2_g=row(ln2_g), ln2_b=row(ln2_b),
    )


class _Group:
    def __init__(self, batch, length, q, lr, gs, ssd_cps, mlstm_cps, proj_tm, merge_tm, ffn,
                 perm=False, lanes_ssd=False, mlstm_gs=None, ssd_gs=None, merge_parts=MERGE_PARTS):
        self.mlstm_gs = gs if mlstm_gs is None else mlstm_gs
        self.ssd_gs = gs if ssd_gs is None else ssd_gs
        self.merge_parts = merge_parts
        self.perm = perm
        self.lanes_ssd = lanes_ssd
        self.batch, self.length, self.q, self.lr, self.gs = batch, length, q, lr, gs
        self.ssd_cps, self.mlstm_cps = ssd_cps, mlstm_cps
        self.rows = batch * length
        self.proj_tm, self.merge_tm, self.ffn = proj_tm, merge_tm, ffn

    def cfg(self, name, layer):
        v = getattr(self, name)
        return v[layer] if isinstance(v, tuple) else v

    def tiling(self, cps, gs=None):
        rows = cps * self.q
        gs = self.gs if gs is None else gs
        steps = self.length // rows
        if gs == 1 or steps == 1:
            spec = lambda width, blk: pl.BlockSpec((gs * rows, width), lambda b, c: (b * steps + c, blk))
            return steps, spec, (lambda a: a), False
        spec = lambda width, blk: pl.BlockSpec((gs, rows, width), lambda b, c: (b, c, blk))
        return steps, spec, (lambda a: a.reshape(self.batch, self.length, a.shape[-1])), True


def _ssd(grp, p32, state, w, layer, prev):
    q, b, gs = grp.q, grp.batch, grp.cfg("ssd_gs", layer)
    has_state = state is not None
    cps = grp.cfg("ssd_cps", layer)
    steps, tile, view, tile3d = grp.tiling(cps, gs)
    kern = functools.partial(_ssd_kernel, q=q, lr=grp.lr, nc=steps, has_state=has_state, gs=gs, cps=cps,
                             perm=grp.perm, tile3d=tile3d)
    p32 = view(p32)
    inputs = [p32, p32, p32, p32]
    in_specs = [tile(D_MODEL, P32_Z), tile(D_MODEL, P32_XS),
                tile(SSD_BC, P32_BC_OFF // SSD_BC), tile(SMALL, P32_SM_OFF // SMALL)]
    if has_state:
        inputs += [state["csx"], state["csb"], state["h"]]
        in_specs += [_seq_spec(layer, gs, SSD_CONV - 1, D_MODEL), _seq_spec(layer, gs, SSD_CONV - 1, SSD_BC),
                     _seq_spec(layer, gs, SSD_HP, SSD_STATE)]
    inputs += [w["cwx"], w["cbx"], w["cwb"], w["cbb"], w["dtb"], w["alog"], w["dexp"], w["ssd_nw"], w["e"],
               w["bd"]]
    const = lambda *shape: pl.BlockSpec(shape, lambda b, c: (0,) * len(shape))
    in_specs += [_layer_spec(layer, SSD_CONV, D_MODEL), _layer_spec(layer, 1, D_MODEL),
                 _layer_spec(layer, SSD_CONV, SSD_BC), _layer_spec(layer, 1, SSD_BC),
                 _layer_spec(layer, 1, SMALL), _layer_spec(layer, 1, SMALL), _layer_spec(layer, 1, D_MODEL),
                 _layer_spec(layer, 1, D_MODEL), const(LANES, SSD_HP), const(SSD_GROUPS * SSD_STATE, SSD_HP)]
    ys, *new = _stacked_call(
        kern, name="ssd", grid=(b // gs, steps), inputs=inputs, in_specs=in_specs,
        out_shape=(jax.ShapeDtypeStruct((b, grp.length, D_MODEL) if tile3d else (grp.rows, D_MODEL), bf16),
                   jax.ShapeDtypeStruct((DEPTH, b, SSD_CONV - 1, D_MODEL), f32),
                   jax.ShapeDtypeStruct((DEPTH, b, SSD_CONV - 1, SSD_BC), f32),
                   jax.ShapeDtypeStruct((DEPTH, b, SSD_HP, SSD_STATE), f32)),
        out_specs=(tile(D_MODEL, 0), _seq_spec(layer, gs, SSD_CONV - 1, D_MODEL),
                   _seq_spec(layer, gs, SSD_CONV - 1, SSD_BC), _seq_spec(layer, gs, SSD_HP, SSD_STATE)),
        stacked={1: prev and prev[0], 2: prev and prev[1], 3: prev and prev[2]},
        scratch_shapes=[pltpu.VMEM((gs, SUBLANES + q, D_MODEL), f32), pltpu.VMEM((gs, SUBLANES + q, SSD_BC), f32),
                        pltpu.VMEM((gs, SSD_GROUPS * SSD_STATE, SSD_HP), f32), pltpu.VMEM((gs, q, D_MODEL), f32)],
        dimension_semantics=("parallel", "arbitrary"))
    return (ys.reshape(grp.rows, D_MODEL), *new)


def _mlstm(grp, qkv, p32, state, w, layer, prev):
    q, b, gs = grp.q, grp.batch, grp.mlstm_gs
    has_state = state is not None
    cps = grp.cfg("mlstm_cps", layer)
    steps, tile, view, tile3d = grp.tiling(cps, gs)
    kern = functools.partial(_mlstm_kernel, q=q, lr=grp.lr, nc=steps, has_state=has_state, gs=gs, cps=cps,
                             perm=grp.perm, tile3d=tile3d)
    hd = MLSTM_HEAD_DIM
    carried = not (has_state and steps == 1 and cps == 1)
    qkv, p32 = view(qkv), view(p32)
    inputs = [qkv, qkv, qkv, p32, p32]
    in_specs = [tile(D_MODEL, 0), tile(D_MODEL, 1), tile(D_MODEL, 2), tile(D_MODEL, P32_O),
                tile(SMALL, P32_SM_OFF // SMALL)]
    if has_state:
        inputs += [state["c"], state["n"], state["m"]]
        in_specs += [_seq_spec(layer, gs, MLSTM_HEADS, hd, hd), _seq_spec(layer, gs, MLSTM_HEADS, hd),
                     _seq_spec(layer, gs, 1, SMALL)]
    inputs += [w["gate_b"], w["mlstm_nw"]]
    in_specs += [_layer_spec(layer, 1, SMALL), _layer_spec(layer, 1, D_MODEL)]
    hm, *new = _stacked_call(
        kern, name="mlstm", grid=(b // gs, steps), inputs=inputs, in_specs=in_specs,
        out_shape=(jax.ShapeDtypeStruct((b, grp.length, D_MODEL) if tile3d else (grp.rows, D_MODEL), bf16),
                   jax.ShapeDtypeStruct((DEPTH, b, MLSTM_HEADS, hd, hd), f32),
                   jax.ShapeDtypeStruct((DEPTH, b, MLSTM_HEADS, hd), f32),
                   jax.ShapeDtypeStruct((DEPTH, b, 1, SMALL), f32)),
        out_specs=(tile(D_MODEL, 0), _seq_spec(layer, gs, MLSTM_HEADS, hd, hd),
                   _seq_spec(layer, gs, MLSTM_HEADS, hd), _seq_spec(layer, gs, 1, SMALL)),
        stacked={1: prev and prev[0], 2: prev and prev[1], 3: prev and prev[2]},
        scratch_shapes=[pltpu.VMEM((gs, MLSTM_HEADS, hd, hd) if carried else (gs, 1, SUBLANES, LANES), f32),
                        pltpu.VMEM((gs, MLSTM_HEADS, hd), f32), pltpu.VMEM((gs, 1, SMALL), f32)],
        dimension_semantics=("parallel", "arbitrary"))
    return (hm.reshape(grp.rows, D_MODEL), *new)


def _trunk(grp, x, state, w):
    ssd_out = mlstm_out = None
    ffn_out = []
    for layer in range(DEPTH):
        p32, qkv = _proj(x, w["w32"], w["wqkv"], layer, grp.cfg("proj_tm", layer))
        if grp.lanes_ssd:
            b, t = grp.batch, grp.length
            x_tm = x.reshape(b, t, D_MODEL).swapaxes(0, 1).reshape(t * b, D_MODEL)
            ys_t, *ssd_out = _ssd_lanes(_proj_t(x_tm, w["w32"], layer), state["cs_t"], state["h_lanes"], w, layer,
                                        ssd_out, steps=t, batch=b)
            ys = ys_t.reshape(D_MODEL, t, b).transpose(2, 1, 0).reshape(b * t, D_MODEL)
        else:
            ys, *ssd_out = _ssd(grp, p32, state, w, layer, ssd_out)
        hm, *mlstm_out = _mlstm(grp, qkv, p32, state, w, layer, mlstm_out)
        x1 = _merge(ys, hm, p32, x, w, layer, grp.cfg("merge_tm", layer), grp.cfg("merge_parts", layer))
        x, s_ffn = _ffn(x1, state["ffn"] if state is not None else None, w, layer, **grp.ffn)
        ffn_out.append(s_ffn)
    return x, ssd_out, mlstm_out, ffn_out


def _unpack_states(batch, ssd_out, mlstm_out):
    csx, csb, h = ssd_out
    c, n, m = mlstm_out
    return (h.reshape(DEPTH, batch, SSD_HEADS, SSD_HEAD_DIM, SSD_STATE),
            jnp.concatenate([csx, csb], axis=-1), c, n, m[:, :, 0, :MLSTM_HEADS])


def kernel(x_prompt, x_sample, state_ssd, state_ssd_conv, state_mlstm_c, state_mlstm_n, state_mlstm_m,
           state_ffn_conv, w_in, ssd_conv_w, ssd_conv_b, ssd_dt_bias, ssd_a_log, ssd_d, ssd_norm_w,
           mlstm_gate_b, mlstm_norm_w, w_branch_a, w_branch_b, w_out, ln1_g, ln1_b, ffn_w_up, ffn_conv_w,
           ffn_conv_b, ffn_w_down, ln2_g, ln2_b):
    w = _prep_weights(w_in, ssd_conv_w, ssd_conv_b, ssd_dt_bias, ssd_a_log, ssd_d, ssd_norm_w, mlstm_gate_b,
                      mlstm_norm_w, w_branch_a, w_branch_b, w_out, ln1_g, ln1_b, ffn_w_up, ffn_conv_w,
                      ffn_conv_b, ffn_w_down, ln2_g, ln2_b)
    keep = FFN_CONV - 1
    keep_ssd = SSD_CONV - 1

    bp, lp, _ = x_prompt.shape
    prompt = _Group(bp, lp, CHUNK, CHUNK, gs=1, ssd_cps=4, mlstm_cps=4, proj_tm=512, merge_tm=1024,
                    ffn=dict(groups=bp, tm=1024, seq_len=0, perm_q=CHUNK), merge_parts=(1, 2),
                    perm=True, mlstm_gs=2, ssd_gs=2)
    per = CHUNK // SUBLANES
    xp_rows = x_prompt.reshape(bp, lp // CHUNK, SUBLANES, per, D_MODEL).swapaxes(2, 3)
    y_p, ssd_p, mlstm_p, ffn_p = _trunk(prompt, xp_rows.reshape(bp * lp, D_MODEL), None, w)
    y_p = y_p.reshape(bp, lp // CHUNK, per, SUBLANES, D_MODEL).swapaxes(2, 3)
    st_p = _unpack_states(bp, ssd_p, mlstm_p)
    ffn_conv_p = jnp.stack(ffn_p)[:, :, SUBLANES - 1::SUBLANES, :]

    bs, ls, _ = x_sample.shape
    s_rows = bs * ls
    sample = _Group(bs, ls, ls, ls, gs=8, ssd_cps=1, mlstm_cps=1, proj_tm=s_rows, merge_tm=s_rows,
                    ffn=dict(groups=1, tm=256, seq_len=ls, bounds=tuple(range(0, D_FF + 1, 256))), lanes_ssd=True)
    lane_b = lambda a: jnp.broadcast_to(a.astype(f32)[..., None], a.shape + (bs,))
    w.update(cw_b=lane_b(ssd_conv_w), cb_b=lane_b(ssd_conv_b), dtb_b=lane_b(ssd_dt_bias), alog_b=lane_b(ssd_a_log),
             dexp_b=lane_b(ssd_d), nw_b=lane_b(ssd_norm_w))
    s_state = dict(
        cs_t=jnp.transpose(state_ssd_conv, (0, 3, 2, 1)).reshape(DEPTH, D_MODEL + SSD_BC, keep_ssd * bs),
        h_lanes=jnp.transpose(state_ssd, (0, 2, 3, 4, 1)),
        c=state_mlstm_c, n=state_mlstm_n,
        m=jnp.pad(state_mlstm_m, ((0, 0), (0, 0), (0, SMALL - MLSTM_HEADS)))[:, :, None, :],
        ffn=state_ffn_conv.reshape(DEPTH, bs * keep, 2 * D_FF),
    )
    y_s, (cs_t, h_lanes), mlstm_s, ffn_s = _trunk(sample, x_sample.reshape(s_rows, D_MODEL), s_state, w)
    c_s, n_s, m_s = mlstm_s
    st_s = (jnp.transpose(h_lanes, (0, 4, 1, 2, 3)),
            jnp.transpose(cs_t.reshape(DEPTH, D_MODEL + SSD_BC, keep_ssd, bs), (0, 3, 2, 1)),
            c_s, n_s, m_s[:, :, 0, :MLSTM_HEADS])
    ffn_conv_s = jnp.stack([u.reshape(bs, ls, 2 * D_FF)[:, ls - keep:, :] for u in ffn_s])
    y_sample = y_s.reshape(bs, ls, D_MODEL)

    return (y_p.reshape(bp, lp, D_MODEL), y_sample, st_p[0], st_s[0], st_p[1], st_s[1], st_p[2], st_s[2],
            st_p[3], st_s[3], st_p[4], st_s[4], ffn_conv_p, ffn_conv_s)
```

```python
import functools
import itertools

import jax
import jax.numpy as jnp
import numpy as np
from jax import lax
from jax.experimental import pallas as pl
from jax.experimental.pallas import tpu as pltpu

f32 = jnp.float32
bf16 = jnp.bfloat16

D_MODEL = 1024
DEPTH = 2
SSD_HEADS = 16
SSD_HEAD_DIM = 64
SSD_STATE = 64
SSD_GROUPS = 2
SSD_CONV = 4
SSD_BC = 2 * SSD_GROUPS * SSD_STATE
SSD_HP = SSD_HEADS * SSD_HEAD_DIM
MLSTM_HEADS = 4
MLSTM_HEAD_DIM = 256
CHUNK = 128
D_FF = 2816
FFN_CONV = 3
ALPHA = (2 * DEPTH) ** 0.25
EPS = 1e-5

LANES = 128
SUBLANES = 8
SMALL = LANES
DT_OFF, I_OFF, F_OFF = 0, 16, 20
P32_Z, P32_O, P32_GA, P32_GB, P32_XS = 0, 1, 2, 3, 4
P32_BC_OFF = 5 * D_MODEL
P32_SM_OFF = P32_BC_OFF + SSD_BC
P32_W = P32_SM_OFF + 2 * SMALL
FF_BOUNDS = (0, 256, 1280, 2304, 2816)
MERGE_PARTS = 2
FF_UP_AHEAD = 2
FF_XP_SLOTS = 4
NEG_BIG = -1e30

NT_DIMS = (((1,), (1,)), ((), ()))
TN_DIMS = (((0,), (0,)), ((), ()))


def _dot(a, b):
    return jnp.dot(a, b, preferred_element_type=f32)


def _split3(x):
    hi = x.astype(bf16)
    r = x - hi.astype(f32)
    mid = r.astype(bf16)
    lo = (r - mid.astype(f32)).astype(bf16)
    return hi, mid, lo


def _dot01_rhs(x, e):
    hi, mid, lo = _split3(x)
    return _dot(hi, e) + _dot(mid, e) + _dot(lo, e)


def _dot01_lhs(t, x):
    hi, mid, lo = _split3(x)
    return _dot(t, hi) + _dot(t, mid) + _dot(t, lo)


def _softplus(x):
    return jnp.maximum(x, 0.0) + jnp.log1p(jnp.exp(-jnp.abs(x)))


def _silu(x):
    return x * jax.nn.sigmoid(x)


def _row_time(i, q, perm):
    if not perm:
        return i
    return (i & (SUBLANES - 1)) * (q // SUBLANES) + lax.shift_right_logical(i, SUBLANES.bit_length() - 1)


def _tri(q, perm=False):
    row = lax.broadcasted_iota(jnp.int32, (q, q), 0)
    col = lax.broadcasted_iota(jnp.int32, (q, q), 1)
    return _row_time(row, q, perm) >= _row_time(col, q, perm)


def _wrap_rows(cur_tail, prev_tail):
    out = []
    for i in range(cur_tail.shape[0] // SUBLANES):
        rows = slice(i * SUBLANES, (i + 1) * SUBLANES)
        first = lax.broadcasted_iota(jnp.int32, (SUBLANES, cur_tail.shape[1]), 0) == 0
        out.append(jnp.where(first, pltpu.roll(prev_tail[rows], 1, axis=0), pltpu.roll(cur_tail[rows], 1, axis=0)))
    return jnp.concatenate(out, axis=0)


def _shift_back(x, wrapped, j):
    n = j * SUBLANES
    return jnp.concatenate([wrapped[wrapped.shape[0] - n:], x[:x.shape[0] - n]], axis=0)


def _valid_rows(q, width, lr, is_last):
    row = lax.broadcasted_iota(jnp.int32, (q, width), 0)
    return row < jnp.where(is_last, lr, q)


def _layer_norm(r, g, b):
    mu = jnp.mean(r, axis=-1, keepdims=True)
    var = jnp.mean(jnp.square(r - mu), axis=-1, keepdims=True)
    return (r - mu) * lax.rsqrt(var + EPS) * g + b


def _layer_spec(layer, *shape, **kw):
    zeros = (0,) * len(shape)
    return pl.BlockSpec((None,) + shape, lambda *_: (layer,) + zeros, **kw)


def _seq_spec(layer, gs, *shape):
    zeros = (0,) * len(shape)
    return pl.BlockSpec((None, gs) + shape, lambda b, c: (layer, b) + zeros)


def _stacked_call(kern, *, name, grid, inputs, in_specs, out_shape, out_specs, stacked, scratch_shapes,
                  dimension_semantics, vmem_limit_bytes=None):
    prev = [(i, a) for i, a in sorted(stacked.items()) if a is not None]
    n_in = len(inputs)

    def body(*refs):
        kern(*refs[:n_in], *refs[n_in + len(prev):])

    return pl.pallas_call(
        body,
        out_shape=out_shape,
        grid=grid,
        in_specs=list(in_specs) + [pl.BlockSpec(memory_space=pl.ANY)] * len(prev),
        out_specs=out_specs,
        scratch_shapes=scratch_shapes,
        input_output_aliases={n_in + k: i for k, (i, _) in enumerate(prev)},
        compiler_params=pltpu.CompilerParams(dimension_semantics=dimension_semantics,
                                             vmem_limit_bytes=vmem_limit_bytes),
        name=name,
    )(*inputs, *[a for _, a in prev])


def _proj_kernel(x_ref, w32_ref, wqkv_ref, o32_ref, oqkv_ref):
    xb = x_ref[...].astype(bf16)
    o32_ref[...] = lax.dot_general(xb, w32_ref[...], NT_DIMS, preferred_element_type=f32)
    oqkv_ref[...] = lax.dot_general(xb, wqkv_ref[...], NT_DIMS, preferred_element_type=f32).astype(oqkv_ref.dtype)


def _proj(x, w32, wqkv, layer, tm):
    m, k = x.shape
    n32, nq = w32.shape[1], wqkv.shape[1]
    once = dict(pipeline_mode=pl.Buffered(1))
    return pl.pallas_call(
        _proj_kernel,
        out_shape=(jax.ShapeDtypeStruct((m, n32), f32), jax.ShapeDtypeStruct((m, nq), bf16)),
        grid=(m // tm,),
        in_specs=[pl.BlockSpec((tm, k), lambda i: (i, 0)),
                  _layer_spec(layer, n32, k, **once), _layer_spec(layer, nq, k, **once)],
        out_specs=(pl.BlockSpec((tm, n32), lambda i: (i, 0)), pl.BlockSpec((tm, nq), lambda i: (i, 0))),
        compiler_params=pltpu.CompilerParams(dimension_semantics=("parallel",),
                                             vmem_limit_bytes=56 * 1024 * 1024),
        name="proj",
    )(x, w32, wqkv)


class _Rows:
    def __init__(self, ref, start, n):
        self.ref, self.start, self.n, self.dtype = ref, start, n, ref.dtype

    def rows(self, off, n):
        return _Rows(self.ref, self.start + off, n)

    def _index(self, idx):
        cols = slice(None) if idx is Ellipsis else idx[1]
        return (slice(self.start, self.start + self.n), cols)

    def __getitem__(self, idx):
        return self.ref[self._index(idx)]

    def __setitem__(self, idx, value):
        self.ref[self._index(idx)] = value


def _per_sequence(seq_fn, refs, n_tile, n_state, n_param, gs, has_state, nc, rows, tile3d=False):
    n_state = n_state if has_state else 0
    tiles, refs = refs[:n_tile], refs[n_tile:]
    state, refs = refs[:n_state], refs[n_state:]
    params, (y_tile, *rest) = refs[:n_param], refs[n_param:]
    phases = []
    for g in range(gs):
        at = lambda group: tuple(r.at[g] for r in group)
        if tile3d:
            seq_rows = lambda group: tuple(_Rows(r.at[g], 0, rows) for r in group)
        else:
            seq_rows = lambda group: tuple(_Rows(r, g * rows, rows) for r in group)
        phases.append(seq_fn(*seq_rows(tiles), *at(state), *params, *seq_rows((y_tile,)), *at(rest)))
    c = pl.program_id(1)

    @pl.when(c == 0)
    def _():
        for init, _, _ in phases:
            init()

    for _ in itertools.zip_longest(*[body() for _, body, _ in phases]):
        pass

    @pl.when(c == nc - 1)
    def _():
        for _, _, final in phases:
            final()


def _ssd_kernel(*refs, q, lr, nc, has_state, gs, cps, perm, tile3d):
    seq = functools.partial(_ssd_seq, q=q, lr=lr, nc=nc, has_state=has_state, cps=cps, perm=perm)
    _per_sequence(seq, refs, 4, 3, 10, gs, has_state, nc, cps * q, tile3d)


def _ssd_seq(*refs, q, lr, nc, has_state, cps, perm):
    z_ref, xs_ref, bc_ref, sm_ref = refs[:4]
    refs = refs[4:]
    if has_state:
        csx_ref, csb_ref, h0_ref = refs[:3]
        refs = refs[3:]
    (cwx_ref, cbx_ref, cwb_ref, cbb_ref, dtb_ref, alog_ref, dexp_ref, nw_ref, e_ref, bd_ref,
     y_ref, ncsx_ref, ncsb_ref, hout_ref, xpx, xpb, ht, yb) = refs
    hdr = SUBLANES
    lo = hdr - (SSD_CONV - 1)
    n2 = SSD_GROUPS * SSD_STATE
    assert lr >= SSD_CONV - 1
    keep = SSD_CONV - 1
    assert not (perm and (has_state or lr != q))
    carried = [(i + 1) * SUBLANES - 1 for i in range(keep)]

    def init():
        if has_state:
            h_t = h0_ref[...].T
            ht[...] = jnp.where(bd_ref[...] > 0.5, jnp.concatenate([h_t, h_t], axis=0), 0.0)
            xpx[lo:hdr, :] = csx_ref[...]
            xpb[lo:hdr, :] = csb_ref[...]
        else:
            ht[...] = jnp.zeros_like(ht)
            rows = slice(0, keep * SUBLANES) if perm else slice(lo, hdr)
            xpx[rows, :] = jnp.zeros((rows.stop - rows.start, SSD_HP), f32)
            xpb[rows, :] = jnp.zeros((rows.stop - rows.start, SSD_BC), f32)

    def final():
        if perm:
            for i, r in enumerate(carried):
                ncsx_ref[i:i + 1, :] = xpx[r:r + 1, :]
                ncsb_ref[i:i + 1, :] = xpb[r:r + 1, :]
        else:
            ncsx_ref[...] = xpx[lo + lr:hdr + lr, :]
            ncsb_ref[...] = xpb[lo + lr:hdr + lr, :]
        h_new = ht[...]
        hout_ref[...] = (h_new[:SSD_STATE, :] + h_new[SSD_STATE:, :]).T

    def body():
        for k in range(cps):
            sub = lambda r: r.rows(k * q, q)
            is_last = (pl.program_id(1) == nc - 1) if k == cps - 1 else False
            yield from _ssd_body(sub(z_ref), sub(xs_ref), sub(bc_ref), sub(sm_ref), cwx_ref, cbx_ref, cwb_ref,
                                 cbb_ref, dtb_ref, alog_ref, dexp_ref, nw_ref, e_ref, bd_ref, sub(y_ref),
                                 xpx, xpb, ht, yb, q=q, lr=lr, is_last=is_last, perm=perm)

    return init, body, final


def _ssd_body(z_ref, xs_ref, bc_ref, sm_ref, cwx_ref, cbx_ref, cwb_ref, cbb_ref, dtb_ref, alog_ref,
              dexp_ref, nw_ref, e_ref, bd_ref, y_ref, xpx, xpb, ht, yb, *, q, lr, is_last, perm):
    hdr = SUBLANES
    lo = hdr - (SSD_CONV - 1)
    n2 = SSD_GROUPS * SSD_STATE
    block_diag = bd_ref[...] > 0.5

    dt = _softplus(sm_ref[...] + dtb_ref[...])
    if lr < q:
        dt = jnp.where(_valid_rows(q, SMALL, lr, is_last), dt, 0.0)
    a = -jnp.exp(alog_ref[...])
    d_a = dt * a
    causal = _tri(q, perm)
    tril = jnp.where(causal, 1.0, 0.0).astype(bf16)
    e = e_ref[...]
    acs = _dot01_lhs(tril, d_a)
    dt_x = _dot01_rhs(dt, e)
    yield

    if perm:
        keep_rows = (SSD_CONV - 1) * SUBLANES

        def conv(xp, x_ref, w_ref, b_ref):
            w = w_ref[...]
            x = x_ref[...]
            wrapped = _wrap_rows(x[q - keep_rows:, :], xp[0:keep_rows, :])
            acc = _shift_back(x, wrapped, SSD_CONV - 1) * w[0:1, :]
            for j in range(1, SSD_CONV - 1):
                acc = acc + _shift_back(x, wrapped, SSD_CONV - 1 - j) * w[j:j + 1, :]
            acc = acc + x * w[SSD_CONV - 1:SSD_CONV, :]
            xp[0:keep_rows, :] = x[q - keep_rows:, :]
            return acc + b_ref[...]
    else:
        xpx[hdr:hdr + q, :] = xs_ref[...]
        xpb[hdr:hdr + q, :] = bc_ref[...]

        def conv(xp, x_ref, w_ref, b_ref):
            w = w_ref[...]
            acc = xp[lo:lo + q, :] * w[0:1, :]
            for j in range(1, SSD_CONV):
                acc = acc + xp[lo + j:lo + j + q, :] * w[j:j + 1, :]
            return acc + b_ref[...]

    cb = conv(xpb, bc_ref, cwb_ref, cbb_ref)
    bcv = _silu(cb)
    bm = bcv[:, :n2].astype(bf16)
    cm = bcv[:, n2:]
    lane_g0 = lax.broadcasted_iota(jnp.int32, (q, n2), 1) < SSD_STATE
    acs_t = acs.T
    acs_x = _dot01_rhs(acs, e)
    yield
    cbms = [lax.dot_general(jnp.where(lane_g0 if g == 0 else jnp.logical_not(lane_g0), cm, 0.0).astype(bf16),
                            bm, NT_DIMS, preferred_element_type=f32) for g in range(SSD_GROUPS)]
    h_prev = ht[...]
    y_off = _dot(cm.astype(bf16), h_prev.astype(bf16))
    cx = conv(xpx, xs_ref, cwx_ref, cbx_ref)
    if not perm:
        tail_x = xpx[lo + q:hdr + q, :]
        tail_b = xpb[lo + q:hdr + q, :]
        xpx[lo:hdr, :] = tail_x
        xpb[lo:hdr, :] = tail_b
    yield
    xs = _silu(cx)
    last_x = acs_x[q - 1:q, :]
    xdt = xs * dt_x
    xdt_b = xdt.astype(bf16)
    yield
    lane_lo = lax.broadcasted_iota(jnp.int32, (q, LANES), 1) < SSD_HEAD_DIM
    heads_per_group = SSD_HEADS // SSD_GROUPS
    decays = [jnp.exp(jnp.where(causal, acs[:, hh:hh + 1] - acs_t[hh:hh + 1, :], -jnp.inf))
              for hh in range(SSD_HEADS)]
    yield
    weights = [(cbms[hh // heads_per_group] * decays[hh]).astype(bf16) for hh in range(SSD_HEADS)]
    xdtw = (xdt * jnp.exp(last_x - acs_x)).astype(bf16)
    yield
    ys = [_dot(weights[hh], xdt_b[:, (hh // 2) * LANES:(hh // 2 + 1) * LANES]) for hh in range(SSD_HEADS)]
    upd = lax.dot_general(bm, xdtw, TN_DIMS, preferred_element_type=f32)
    yield
    for p in range(SSD_HEADS // 2):
        yb[:, p * LANES:(p + 1) * LANES] = jnp.where(lane_lo, ys[2 * p], ys[2 * p + 1])
    ht[...] = jnp.exp(last_x) * h_prev + jnp.where(block_diag, upd, 0.0)
    yield
    y = yb[...] + y_off * jnp.exp(acs_x) + dexp_ref[...] * xs
    y = y * _silu(z_ref[...])
    yield
    y = y * lax.rsqrt(jnp.mean(jnp.square(y), axis=-1, keepdims=True) + EPS) * nw_ref[...]
    y_ref[...] = y.astype(y_ref.dtype)


PT_ROWS = 5 * 512
PT_XS, PT_BC, PT_SM = D_MODEL, 2 * D_MODEL, 2 * D_MODEL + SSD_BC


def _proj_t_kernel(x_ref, w_ref, o_ref, xb):
    @pl.when(pl.program_id(0) == 0)
    def _():
        xb[...] = x_ref[...].astype(bf16)

    o_ref[...] = lax.dot_general(w_ref[...], xb[...], NT_DIMS, preferred_element_type=f32)


def _proj_t(x_tm, w32, layer):
    m, k = x_tm.shape
    tn = 512
    xs_blk = P32_XS * D_MODEL // tn
    return pl.pallas_call(
        _proj_t_kernel,
        out_shape=jax.ShapeDtypeStruct((PT_ROWS, m), f32),
        grid=(PT_ROWS // tn,),
        in_specs=[pl.BlockSpec((m, k), lambda j: (0, 0)),
                  pl.BlockSpec((None, tn, k), lambda j: (layer, jnp.where(j < D_MODEL // tn, j, j + xs_blk - D_MODEL // tn), 0))],
        out_specs=pl.BlockSpec((tn, m), lambda j: (j, 0)),
        scratch_shapes=[pltpu.VMEM((m, k), bf16)],
        compiler_params=pltpu.CompilerParams(dimension_semantics=("arbitrary",)),
        name="proj_t",
    )(x_tm, w32)


def _ssd_lanes_kernel(pt_ref, cst_ref, h0_ref, cw_ref, cb_ref, dtb_ref, alog_ref, dexp_ref, nw_ref,
                      y_ref, ncs_ref, hout_ref, xc, dts, decs, ysc, *, steps, batch):
    hd = pl.program_id(0)
    n_ch = D_MODEL + SSD_BC
    keep = SSD_CONV - 1
    lanes = lambda t: slice(t * batch, (t + 1) * batch)

    @pl.when(hd == 0)
    def _():
        for t in range(steps):
            acc = None
            for j in range(SSD_CONV):
                i = t + j
                src = cst_ref[:, lanes(i)] if i < keep else pt_ref[PT_XS:PT_XS + n_ch, lanes(i - keep)]
                term = src * cw_ref[j]
                acc = term if acc is None else acc + term
            xc[:, lanes(t)] = _silu(acc + cb_ref[...])
        ncs_ref[...] = pt_ref[PT_XS:PT_XS + n_ch, (steps - keep) * batch:steps * batch]
        dt = _softplus(pt_ref[PT_SM:PT_SM + SSD_HEADS, :] + jnp.concatenate([dtb_ref[...]] * steps, axis=1))
        dts[...] = dt
        decs[...] = jnp.exp(dt * jnp.concatenate([-jnp.exp(alog_ref[...])] * steps, axis=1))

    grp_row = (hd // (SSD_HEADS // SSD_GROUPS)) * SSD_STATE
    xh = xc[pl.ds(pl.multiple_of(hd * SSD_HEAD_DIM, SSD_HEAD_DIM), SSD_HEAD_DIM), :]
    bh = xc[pl.ds(pl.multiple_of(D_MODEL + grp_row, SSD_STATE), SSD_STATE), :]
    ch = xc[pl.ds(pl.multiple_of(D_MODEL + SSD_GROUPS * SSD_STATE + grp_row, SSD_STATE), SSD_STATE), :]
    dth = dts[pl.ds(hd, 1), :]
    dech = decs[pl.ds(hd, 1), :]
    d_skip = dexp_ref[pl.ds(hd, 1), :]
    y_rows = [[] for _ in range(steps)]
    for p in range(SSD_HEAD_DIM):
        h = h0_ref[p]
        for t in range(steps):
            x_row = xh[p:p + 1, lanes(t)]
            h = dech[:, lanes(t)] * h + (x_row * dth[:, lanes(t)]) * bh[:, lanes(t)]
            y_rows[t].append(jnp.sum(ch[:, lanes(t)] * h, axis=0, keepdims=True) + d_skip * x_row)
        hout_ref[p] = h
    rows = pl.ds(pl.multiple_of(hd * SSD_HEAD_DIM, SSD_HEAD_DIM), SSD_HEAD_DIM)
    for t in range(steps):
        ysc[rows, lanes(t)] = jnp.concatenate(y_rows[t], axis=0)

    @pl.when(hd == SSD_HEADS - 1)
    def _():
        y = ysc[...] * _silu(pt_ref[0:D_MODEL, :])
        y = y * lax.rsqrt(jnp.mean(jnp.square(y), axis=0, keepdims=True) + EPS)
        y_ref[...] = (y * jnp.concatenate([nw_ref[...]] * steps, axis=1)).astype(y_ref.dtype)


def _ssd_lanes(pt, cst, h0, w, layer, prev, *, steps, batch):
    n_ch = D_MODEL + SSD_BC
    tb = steps * batch
    keep = SSD_CONV - 1
    assert steps >= keep and batch % LANES == 0
    kern = functools.partial(_ssd_lanes_kernel, steps=steps, batch=batch)
    full = lambda *shape: pl.BlockSpec(shape, lambda hd: (0,) * len(shape))
    hblock = pl.BlockSpec((None, None, SSD_HEAD_DIM, SSD_STATE, batch), lambda hd: (layer, hd, 0, 0, 0))
    return _stacked_call(
        kern, name="ssd_lanes", grid=(SSD_HEADS,),
        inputs=[pt, cst, h0, w["cw_b"], w["cb_b"], w["dtb_b"], w["alog_b"], w["dexp_b"], w["nw_b"]],
        in_specs=[full(PT_ROWS, tb), _layer_spec(layer, n_ch, keep * batch), hblock,
                  _layer_spec(layer, SSD_CONV, n_ch, batch), _layer_spec(layer, n_ch, batch),
                  _layer_spec(layer, SSD_HEADS, batch), _layer_spec(layer, SSD_HEADS, batch),
                  _layer_spec(layer, SSD_HEADS, batch), _layer_spec(layer, D_MODEL, batch)],
        out_shape=(jax.ShapeDtypeStruct((D_MODEL, tb), bf16),
                   jax.ShapeDtypeStruct((DEPTH, n_ch, keep * batch), f32),
                   jax.ShapeDtypeStruct((DEPTH, SSD_HEADS, SSD_HEAD_DIM, SSD_STATE, batch), f32)),
        out_specs=(full(D_MODEL, tb), _layer_spec(layer, n_ch, keep * batch), hblock),
        stacked={1: prev and prev[0], 2: prev and prev[1]},
        scratch_shapes=[pltpu.VMEM((n_ch, tb), f32), pltpu.VMEM((SSD_HEADS, tb), f32),
                        pltpu.VMEM((SSD_HEADS, tb), f32), pltpu.VMEM((D_MODEL, tb), f32)],
        dimension_semantics=("arbitrary",))


def _mlstm_kernel(*refs, q, lr, nc, has_state, gs, cps, perm, tile3d):
    seq = functools.partial(_mlstm_seq, q=q, lr=lr, nc=nc, has_state=has_state, cps=cps, perm=perm)
    _per_sequence(seq, refs, 5, 3, 2, gs, has_state, nc, cps * q, tile3d)


def _mlstm_seq(*refs, q, lr, nc, has_state, cps, perm):
    q_ref, k_ref, v_ref, o_ref, sm_ref = refs[:5]
    refs = refs[5:]
    if has_state:
        c0_ref, n0_ref, m0_ref = refs[:3]
        refs = refs[3:]
    gb_ref, nw_ref, h_ref, cout_ref, nout_ref, mout_ref, cs, ns, ms = refs
    direct = has_state and nc == 1 and cps == 1

    def init():
        if direct:
            return
        if has_state:
            cs[...] = c0_ref[...]
            ns[...] = n0_ref[...]
            ms[...] = m0_ref[...]
        else:
            cs[...] = jnp.zeros_like(cs)
            ns[...] = jnp.zeros_like(ns)
            ms[...] = jnp.zeros_like(ms)

    def final():
        if direct:
            return
        cout_ref[...] = cs[...]
        nout_ref[...] = ns[...]
        mout_ref[...] = ms[...]

    def body():
        chunks = []
        for k in range(cps):
            sub = lambda r, k=k: r.rows(k * q, q)
            is_last = (pl.program_id(1) == nc - 1) if k == cps - 1 else False
            src = (c0_ref, n0_ref, m0_ref) if direct else (cs, ns, ms)
            dst = (cout_ref, nout_ref, mout_ref) if direct else (cs, ns, ms)
            chunks.append(_mlstm_body(sub(q_ref), sub(k_ref), sub(v_ref), sub(o_ref), sub(sm_ref), gb_ref, nw_ref,
                                      sub(h_ref), src, dst, q=q, lr=lr, is_last=is_last, perm=perm))
        yield from _staggered(chunks, MLSTM_STATE_STAGES)

    return init, body, final


MLSTM_STATE_STAGES = 8


def _staggered(gens, skew):
    done = [False] * len(gens)
    t = 0
    while not all(done):
        for i, g in enumerate(gens):
            if done[i] or t < i * skew:
                continue
            try:
                next(g)
            except StopIteration:
                done[i] = True
        t += 1
        yield


def _mlstm_body(q_ref, k_ref, v_ref, o_ref, sm_ref, gb_ref, nw_ref, h_ref, src, dst, *, q, lr, is_last, perm):
    c_src, n_src, m_src = src
    c_dst, n_dst, m_dst = dst
    sm = sm_ref[...] + gb_ref[...]
    logf = -_softplus(-sm)
    ipre = sm
    if lr < q:
        valid = _valid_rows(q, SMALL, lr, is_last)
        logf = jnp.where(valid, logf, 0.0)
        ipre = jnp.where(valid, ipre, NEG_BIG)
    causal = _tri(q, perm)
    tril = jnp.where(causal, 1.0, 0.0).astype(bf16)
    yield
    bcum = _dot01_lhs(tril, logf)
    ipre_t = ipre.T
    yield
    bcum_t = bcum.T
    lane = lax.broadcasted_iota(jnp.int32, (1, SMALL), 1)
    k_scale = MLSTM_HEAD_DIM ** -0.5

    heads = range(MLSTM_HEADS)
    sls = [slice(h * MLSTM_HEAD_DIM, (h + 1) * MLSTM_HEAD_DIM) for h in heads]
    q_all, k_all, v_all, o_all = q_ref[...], k_ref[...], v_ref[...], o_ref[...]
    qs = [q_all[:, sl] for sl in sls]
    ks = [k_all[:, sl] * k_scale for sl in sls]
    vs = [v_all[:, sl] for sl in sls]
    b_cols = [bcum[:, F_OFF + h:F_OFF + h + 1] for h in heads]
    i_cols = [ipre[:, I_OFF + h:I_OFF + h + 1] for h in heads]
    dmats = [jnp.where(causal, b_cols[h] - bcum_t[F_OFF + h:F_OFF + h + 1, :] + ipre_t[I_OFF + h:I_OFF + h + 1, :],
                       -jnp.inf) for h in heads]
    yield
    qk = [lax.dot_general(qs[h], ks[h], NT_DIMS, preferred_element_type=f32) for h in heads]
    d_max = [jnp.max(dmats[h], axis=-1, keepdims=True) for h in heads]
    yield
    n_all = n_src[...]
    m_all = m_src[...]
    m_new = m_all
    cs_in = [c_src[h] for h in heads]
    m_prevs = [m_all[:, h:h + 1] for h in heads]
    qc = [_dot(qs[h], cs_in[h].astype(bf16)) for h in heads]
    inters = [b_cols[h] + m_prevs[h] for h in heads]
    m_ts = [jnp.maximum(inters[h], d_max[h]) for h in heads]
    yield
    w_inters = [jnp.exp(inters[h] - m_ts[h]) for h in heads]
    ss = [qk[h] * jnp.exp(dmats[h] - m_ts[h]) for h in heads]
    yield
    sv = [_dot(ss[h].astype(bf16), vs[h]) for h in heads]
    m_ends = [m_ts[h][q - 1:q, :] for h in heads]
    b_lasts = [b_cols[h][q - 1:q, :] for h in heads]
    kws = [ks[h].astype(f32) * jnp.exp(b_lasts[h] - b_cols[h] + i_cols[h] - m_ends[h]) for h in heads]
    yield
    kv = [lax.dot_general(kws[h].astype(bf16), vs[h], TN_DIMS, preferred_element_type=f32) for h in heads]
    qns = [jnp.sum(qs[h].astype(f32) * n_all[h:h + 1, :], axis=-1, keepdims=True) for h in heads]
    yield
    dens = [jnp.sum(ss[h], axis=-1, keepdims=True) + w_inters[h] * qns[h] for h in heads]
    yield
    hvs = [(sv[h] + w_inters[h] * qc[h]) / jnp.maximum(jnp.abs(dens[h]), jnp.exp(-m_ts[h])) for h in heads]
    yield
    rms = [lax.rsqrt(jnp.mean(jnp.square(hvs[h]), axis=-1, keepdims=True) + EPS) for h in heads]
    yield
    h_new = [(hvs[h] * rms[h] * nw_ref[:, sls[h]] * jax.nn.sigmoid(o_all[:, sls[h]])).astype(h_ref.dtype)
             for h in heads]
    w_cs = [jnp.exp(b_lasts[h] + m_prevs[h] - m_ends[h]) for h in heads]
    yield
    for h in heads:
        c_dst[h] = w_cs[h] * cs_in[h] + kv[h]
        m_new = jnp.where(lane == h, m_ends[h], m_new)
    h_ref[...] = jnp.concatenate(h_new, axis=1)
    n_dst[...] = jnp.concatenate(
        [w_cs[h] * n_all[h:h + 1, :] + jnp.sum(kws[h], axis=0, keepdims=True) for h in heads], axis=0)
    m_dst[...] = m_new


def _merge_kernel(ys_ref, hm_ref, ga_ref, gb_ref, x_ref, wa_ref, wb_ref, wo_ref, g_ref, b_ref, o_ref, *, parts):
    tm = x_ref.shape[0]
    rows = [pl.ds(i * (tm // parts), tm // parts) for i in range(parts)]
    br = [(_dot(ys_ref[r, :], wa_ref[...]), _dot(hm_ref[r, :], wb_ref[...])) for r in rows]
    merged = [(jax.nn.sigmoid(ga_ref[r, :]) * a + jax.nn.sigmoid(gb_ref[r, :]) * b).astype(bf16)
              for r, (a, b) in zip(rows, br)]
    mix = [_dot(m, wo_ref[...]) for m in merged]
    for r, m in zip(rows, mix):
        o_ref[r, :] = _layer_norm(ALPHA * x_ref[r, :] + m, g_ref[...], b_ref[...])


def _merge(ys, hm, p32, x, w, layer, tm, parts=MERGE_PARTS):
    m = x.shape[0]
    row = lambda blk: pl.BlockSpec((tm, D_MODEL), lambda i: (i, blk))
    return pl.pallas_call(
        functools.partial(_merge_kernel, parts=parts),
        out_shape=jax.ShapeDtypeStruct((m, D_MODEL), f32),
        grid=(m // tm,),
        in_specs=[row(0), row(0), row(P32_GA), row(P32_GB), row(0),
                  _layer_spec(layer, D_MODEL, D_MODEL), _layer_spec(layer, D_MODEL, D_MODEL),
                  _layer_spec(layer, D_MODEL, D_MODEL), _layer_spec(layer, 1, D_MODEL),
                  _layer_spec(layer, 1, D_MODEL)],
        out_specs=row(0),
        compiler_params=pltpu.CompilerParams(dimension_semantics=("parallel",)),
        name="merge",
    )(ys, hm, p32, p32, x, w["wa"], w["wb"], w["wo"], w["ln1_g"], w["ln1_b"])


def _ffn_kernel(*refs, tm, seq_len, perm_q, ahead_n, bounds):
    multi = seq_len > 0
    keep_rows = (FFN_CONV - 1) * SUBLANES
    n_chunks = len(bounds) - 1
    if multi:
        x_ref, st_ref, wup_ref, cw_ref, cb_ref, wdn_ref, g_ref, b_ref, o_ref, sout_ref, xp = refs
    else:
        x_ref, wup_ref, cw_ref, cb_ref, wdn_ref, g_ref, b_ref, o_ref, sout_ref, xp, carry = refs
        @pl.when(pl.program_id(1) == 0)
        def _():
            carry[...] = jnp.zeros_like(carry)

    hdr = SUBLANES
    x = x_ref[...]
    xb = x.astype(bf16)
    if multi:
        assert seq_len & (seq_len - 1) == 0
        nseq = tm // seq_len
        t = lax.broadcasted_iota(jnp.int32, (tm, 1), 0) & (seq_len - 1)
        row = lax.broadcasted_iota(jnp.int32, (tm, 2 * nseq), 0)
        col = lax.broadcasted_iota(jnp.int32, (tm, 2 * nseq), 1)
        t_sel = row & (seq_len - 1)
        seq0 = lax.shift_right_logical(row - t_sel, (seq_len // 2).bit_length() - 1)
        sel_p2 = jnp.where(col == seq0 + t_sel, jnp.where(t_sel < 2, 1.0, 0.0), 0.0).astype(bf16)
        sel_p1 = jnp.where(col == seq0 + 1, jnp.where(t_sel == 0, 1.0, 0.0), 0.0).astype(bf16)
        xp[:, 0:hdr, :] = jnp.zeros((FF_XP_SLOTS, hdr, xp.shape[2]), f32)

    def cols_of(c, part):
        return slice(part * D_FF + bounds[c], part * D_FF + bounds[c + 1])

    def up(c):
        return [_dot(xb, wup_ref[:, cols_of(c, part)]) for part in range(2)]

    def conv_act(c, us):
        halves = []
        for part, u in enumerate(us):
            cols = cols_of(c, part)
            slot = (2 * c + part) % FF_XP_SLOTS
            if perm_q:
                prev = carry[:, cols]
                p1, p2 = [], []
                for kk in range(tm // perm_q):
                    uc = u[kk * perm_q:(kk + 1) * perm_q, :]
                    wrapped = _wrap_rows(uc[perm_q - keep_rows:, :], prev)
                    p1.append(_shift_back(uc, wrapped, 1))
                    p2.append(_shift_back(uc, wrapped, 2))
                    prev = uc[perm_q - keep_rows:, :]
                carry[:, cols] = prev
                p1, p2 = jnp.concatenate(p1, axis=0), jnp.concatenate(p2, axis=0)
                w = cw_ref[:, cols]
                halves.append(p2 * w[0:1, :] + p1 * w[1:2, :] + u * w[2:3, :] + cb_ref[:, cols])
                continue
            wd = slice(0, u.shape[1])
            xp[slot, hdr:hdr + tm, wd] = u
            if multi:
                sout_ref[:, cols] = u
                st = st_ref[:, cols]
                p1 = jnp.where(t == 0, _dot01_lhs(sel_p1, st), xp[slot, hdr - 1:hdr - 1 + tm, wd])
                p2 = jnp.where(t < 2, _dot01_lhs(sel_p2, st), xp[slot, hdr - 2:hdr - 2 + tm, wd])
            else:
                xp[slot, 0:hdr, wd] = carry[:, cols]
                p1 = xp[slot, hdr - 1:hdr - 1 + tm, wd]
                p2 = xp[slot, hdr - 2:hdr - 2 + tm, wd]
                carry[:, cols] = u[tm - hdr:tm, :]
            w = cw_ref[:, cols]
            halves.append(p2 * w[0:1, :] + p1 * w[1:2, :] + u * w[2:3, :] + cb_ref[:, cols])
        return (_silu(halves[0]) * halves[1]).astype(bf16)

    acc = None
    ahead = [up(c) for c in range(min(ahead_n, n_chunks))]
    pending = None
    for c in range(n_chunks):
        if c + ahead_n < n_chunks:
            ahead.append(up(c + ahead_n))
        if pending is not None:
            d = _dot(pending, wdn_ref[bounds[c - 1]:bounds[c], :])
            acc = d if acc is None else acc + d
        pending = conv_act(c, ahead.pop(0))
    acc = acc + _dot(pending, wdn_ref[bounds[n_chunks - 1]:bounds[n_chunks], :])

    if not multi:
        sout_ref[...] = carry[...]
    o_ref[...] = _layer_norm(ALPHA * x + acc, g_ref[...], b_ref[...])


def _ffn(x, st, w, layer, *, groups, tm, seq_len, perm_q=0, ahead=FF_UP_AHEAD, bounds=FF_BOUNDS):
    m = x.shape[0]
    bounds = bounds[layer] if isinstance(bounds[0], tuple) else bounds
    chunk_w = max(b - a for a, b in zip(bounds, bounds[1:]))
    tiles = m // (groups * tm)
    multi = seq_len > 0
    kern = functools.partial(_ffn_kernel, tm=tm, seq_len=seq_len, perm_q=perm_q, ahead_n=ahead, bounds=bounds)
    carry_rows = (FFN_CONV - 1) * SUBLANES if perm_q else SUBLANES
    once = dict(pipeline_mode=pl.Buffered(1))
    x_spec = pl.BlockSpec((tm, D_MODEL), lambda s, j: (s * tiles + j, 0))
    w_specs = [_layer_spec(layer, D_MODEL, 2 * D_FF, **once), _layer_spec(layer, FFN_CONV, 2 * D_FF, **once),
               _layer_spec(layer, 1, 2 * D_FF, **once), _layer_spec(layer, D_FF, D_MODEL, **once),
               _layer_spec(layer, 1, D_MODEL, **once), _layer_spec(layer, 1, D_MODEL, **once)]
    w_args = (w["wup"], w["fcw"], w["fcb"], w["wdn"], w["ln2_g"], w["ln2_b"])
    xp = pltpu.VMEM((FF_XP_SLOTS, SUBLANES + (0 if perm_q else tm), chunk_w), f32)
    if multi:
        nst = 2 * (tm // seq_len)
        inputs = (x, st) + w_args
        in_specs = [x_spec, pl.BlockSpec((None, nst, 2 * D_FF), lambda s, j: (layer, s * tiles + j, 0))] + w_specs
        sout_shape = jax.ShapeDtypeStruct((m, 2 * D_FF), f32)
        sout_spec = pl.BlockSpec((tm, 2 * D_FF), lambda s, j: (s * tiles + j, 0))
        scratch = [xp]
    else:
        inputs = (x,) + w_args
        in_specs = [x_spec] + w_specs
        sout_shape = jax.ShapeDtypeStruct((groups, carry_rows, 2 * D_FF), f32)
        sout_spec = pl.BlockSpec((None, carry_rows, 2 * D_FF), lambda s, j: (s, 0, 0))
        scratch = [xp, pltpu.VMEM((carry_rows, 2 * D_FF), f32)]
    return pl.pallas_call(
        kern,
        out_shape=(jax.ShapeDtypeStruct((m, D_MODEL), f32), sout_shape),
        grid=(groups, tiles),
        in_specs=in_specs,
        out_specs=(x_spec, sout_spec),
        scratch_shapes=scratch,
        compiler_params=pltpu.CompilerParams(dimension_semantics=("parallel", "arbitrary"),
                                             vmem_limit_bytes=56 * 1024 * 1024),
        name="ffn",
    )(*inputs)


def _pad_lanes(v, off, width=SMALL):
    out = jnp.zeros((v.shape[0], 1, width), f32)
    return out.at[:, 0, off:off + v.shape[1]].set(v.astype(f32))


def _prep_weights(w_in, ssd_conv_w, ssd_conv_b, ssd_dt_bias, ssd_a_log, ssd_d, ssd_norm_w, mlstm_gate_b,
                  mlstm_norm_w, w_branch_a, w_branch_b, w_out, ln1_g, ln1_b, ffn_w_up, ffn_conv_w,
                  ffn_conv_b, ffn_w_down, ln2_g, ln2_b):
    d = D_MODEL
    o_z, o_xbc, o_dt = 0, d, d + d + SSD_BC
    o_q = o_dt + SSD_HEADS
    o_if = o_q + 3 * d
    o_o = o_if + 2 * MLSTM_HEADS
    o_g = o_o + d
    w_t = jnp.swapaxes(w_in, 1, 2)
    cols = lambda a, n: w_t[:, a:a + n, :]
    zeros = lambda n: jnp.zeros((DEPTH, n, d), w_in.dtype)
    w32 = jnp.concatenate([cols(o_z, d), cols(o_o, d), cols(o_g, 2 * d), cols(o_xbc, d + SSD_BC),
                           cols(o_dt, SSD_HEADS), cols(o_if, 2 * MLSTM_HEADS),
                           zeros(P32_W - P32_SM_OFF - SSD_HEADS - 2 * MLSTM_HEADS)], axis=1).astype(bf16)
    e = (np.arange(SSD_HP)[None, :] // SSD_HEAD_DIM == np.arange(LANES)[:, None])
    bd = ((np.arange(SSD_GROUPS * SSD_STATE)[:, None] < SSD_STATE)
          == (np.arange(SSD_HP)[None, :] < SSD_HP // SSD_GROUPS))
    row = lambda a: a[:, None, :]
    return dict(
        w32=w32, wqkv=cols(o_q, 3 * d).astype(bf16),
        cwx=ssd_conv_w[:, :, :d], cbx=row(ssd_conv_b[:, :d]),
        cwb=ssd_conv_w[:, :, d:], cbb=row(ssd_conv_b[:, d:]),
        dtb=_pad_lanes(ssd_dt_bias, DT_OFF), alog=_pad_lanes(ssd_a_log, DT_OFF),
        dexp=row(jnp.repeat(ssd_d.astype(f32), SSD_HEAD_DIM, axis=1)), ssd_nw=row(ssd_norm_w),
        e=jnp.asarray(e, bf16), bd=jnp.asarray(bd, f32),
        gate_b=_pad_lanes(mlstm_gate_b, I_OFF), mlstm_nw=row(mlstm_norm_w),
        wa=w_branch_a.astype(bf16), wb=w_branch_b.astype(bf16), wo=w_out.astype(bf16),
        ln1_g=row(ln1_g), ln1_b=row(ln1_b),
        wup=ffn_w_up.astype(bf16), fcw=ffn_conv_w, fcb=row(ffn_conv_b), wdn=ffn_w_down.astype(bf16),
        ln2_g=row(ln2_g), ln2_b=row(ln2_b),
    )


class _Group:
    def __init__(self, batch, length, q, lr, gs, ssd_cps, mlstm_cps, proj_tm, merge_tm, ffn,
                 perm=False, lanes_ssd=False, mlstm_gs=None, ssd_gs=None, merge_parts=MERGE_PARTS):
        self.mlstm_gs = gs if mlstm_gs is None else mlstm_gs
        self.ssd_gs = gs if ssd_gs is None else ssd_gs
        self.merge_parts = merge_parts
        self.perm = perm
        self.lanes_ssd = lanes_ssd
        self.batch, self.length, self.q, self.lr, self.gs = batch, length, q, lr, gs
        self.ssd_cps, self.mlstm_cps = ssd_cps, mlstm_cps
        self.rows = batch * length
        self.proj_tm, self.merge_tm, self.ffn = proj_tm, merge_tm, ffn

    def cfg(self, name, layer):
        v = getattr(self, name)
        return v[layer] if isinstance(v, tuple) else v

    def tiling(self, cps, gs=None):
        rows = cps * self.q
        gs = self.gs if gs is None else gs
        steps = self.length // rows
        if gs == 1 or steps == 1:
            spec = lambda width, blk: pl.BlockSpec((gs * rows, width), lambda b, c: (b * steps + c, blk))
            return steps, spec, (lambda a: a), False
        spec = lambda width, blk: pl.BlockSpec((gs, rows, width), lambda b, c: (b, c, blk))
        return steps, spec, (lambda a: a.reshape(self.batch, self.length, a.shape[-1])), True


def _ssd(grp, p32, state, w, layer, prev):
    q, b, gs = grp.q, grp.batch, grp.cfg("ssd_gs", layer)
    has_state = state is not None
    cps = grp.cfg("ssd_cps", layer)
    steps, tile, view, tile3d = grp.tiling(cps, gs)
    kern = functools.partial(_ssd_kernel, q=q, lr=grp.lr, nc=steps, has_state=has_state, gs=gs, cps=cps,
                             perm=grp.perm, tile3d=tile3d)
    p32 = view(p32)
    inputs = [p32, p32, p32, p32]
    in_specs = [tile(D_MODEL, P32_Z), tile(D_MODEL, P32_XS),
                tile(SSD_BC, P32_BC_OFF // SSD_BC), tile(SMALL, P32_SM_OFF // SMALL)]
    if has_state:
        inputs += [state["csx"], state["csb"], state["h"]]
        in_specs += [_seq_spec(layer, gs, SSD_CONV - 1, D_MODEL), _seq_spec(layer, gs, SSD_CONV - 1, SSD_BC),
                     _seq_spec(layer, gs, SSD_HP, SSD_STATE)]
    inputs += [w["cwx"], w["cbx"], w["cwb"], w["cbb"], w["dtb"], w["alog"], w["dexp"], w["ssd_nw"], w["e"],
               w["bd"]]
    const = lambda *shape: pl.BlockSpec(shape, lambda b, c: (0,) * len(shape))
    in_specs += [_layer_spec(layer, SSD_CONV, D_MODEL), _layer_spec(layer, 1, D_MODEL),
                 _layer_spec(layer, SSD_CONV, SSD_BC), _layer_spec(layer, 1, SSD_BC),
                 _layer_spec(layer, 1, SMALL), _layer_spec(layer, 1, SMALL), _layer_spec(layer, 1, D_MODEL),
                 _layer_spec(layer, 1, D_MODEL), const(LANES, SSD_HP), const(SSD_GROUPS * SSD_STATE, SSD_HP)]
    ys, *new = _stacked_call(
        kern, name="ssd", grid=(b // gs, steps), inputs=inputs, in_specs=in_specs,
        out_shape=(jax.ShapeDtypeStruct((b, grp.length, D_MODEL) if tile3d else (grp.rows, D_MODEL), bf16),
                   jax.ShapeDtypeStruct((DEPTH, b, SSD_CONV - 1, D_MODEL), f32),
                   jax.ShapeDtypeStruct((DEPTH, b, SSD_CONV - 1, SSD_BC), f32),
                   jax.ShapeDtypeStruct((DEPTH, b, SSD_HP, SSD_STATE), f32)),
        out_specs=(tile(D_MODEL, 0), _seq_spec(layer, gs, SSD_CONV - 1, D_MODEL),
                   _seq_spec(layer, gs, SSD_CONV - 1, SSD_BC), _seq_spec(layer, gs, SSD_HP, SSD_STATE)),
        stacked={1: prev and prev[0], 2: prev and prev[1], 3: prev and prev[2]},
        scratch_shapes=[pltpu.VMEM((gs, SUBLANES + q, D_MODEL), f32), pltpu.VMEM((gs, SUBLANES + q, SSD_BC), f32),
                        pltpu.VMEM((gs, SSD_GROUPS * SSD_STATE, SSD_HP), f32), pltpu.VMEM((gs, q, D_MODEL), f32)],
        dimension_semantics=("parallel", "arbitrary"))
    return (ys.reshape(grp.rows, D_MODEL), *new)


def _mlstm(grp, qkv, p32, state, w, layer, prev):
    q, b, gs = grp.q, grp.batch, grp.mlstm_gs
    has_state = state is not None
    cps = grp.cfg("mlstm_cps", layer)
    steps, tile, view, tile3d = grp.tiling(cps, gs)
    kern = functools.partial(_mlstm_kernel, q=q, lr=grp.lr, nc=steps, has_state=has_state, gs=gs, cps=cps,
                             perm=grp.perm, tile3d=tile3d)
    hd = MLSTM_HEAD_DIM
    carried = not (has_state and steps == 1 and cps == 1)
    qkv, p32 = view(qkv), view(p32)
    inputs = [qkv, qkv, qkv, p32, p32]
    in_specs = [tile(D_MODEL, 0), tile(D_MODEL, 1), tile(D_MODEL, 2), tile(D_MODEL, P32_O),
                tile(SMALL, P32_SM_OFF // SMALL)]
    if has_state:
        inputs += [state["c"], state["n"], state["m"]]
        in_specs += [_seq_spec(layer, gs, MLSTM_HEADS, hd, hd), _seq_spec(layer, gs, MLSTM_HEADS, hd),
                     _seq_spec(layer, gs, 1, SMALL)]
    inputs += [w["gate_b"], w["mlstm_nw"]]
    in_specs += [_layer_spec(layer, 1, SMALL), _layer_spec(layer, 1, D_MODEL)]
    hm, *new = _stacked_call(
        kern, name="mlstm", grid=(b // gs, steps), inputs=inputs, in_specs=in_specs,
        out_shape=(jax.ShapeDtypeStruct((b, grp.length, D_MODEL) if tile3d else (grp.rows, D_MODEL), bf16),
                   jax.ShapeDtypeStruct((DEPTH, b, MLSTM_HEADS, hd, hd), f32),
                   jax.ShapeDtypeStruct((DEPTH, b, MLSTM_HEADS, hd), f32),
                   jax.ShapeDtypeStruct((DEPTH, b, 1, SMALL), f32)),
        out_specs=(tile(D_MODEL, 0), _seq_spec(layer, gs, MLSTM_HEADS, hd, hd),
                   _seq_spec(layer, gs, MLSTM_HEADS, hd), _seq_spec(layer, gs, 1, SMALL)),
        stacked={1: prev and prev[0], 2: prev and prev[1], 3: prev and prev[2]},
        scratch_shapes=[pltpu.VMEM((gs, MLSTM_HEADS, hd, hd) if carried else (gs, 1, SUBLANES, LANES), f32),
                        pltpu.VMEM((gs, MLSTM_HEADS, hd), f32), pltpu.VMEM((gs, 1, SMALL), f32)],
        dimension_semantics=("parallel", "arbitrary"))
    return (hm.reshape(grp.rows, D_MODEL), *new)


def _trunk(grp, x, state, w):
    ssd_out = mlstm_out = None
    ffn_out = []
    for layer in range(DEPTH):
        p32, qkv = _proj(x, w["w32"], w["wqkv"], layer, grp.cfg("proj_tm", layer))
        if grp.lanes_ssd:
            b, t = grp.batch, grp.length
            x_tm = x.reshape(b, t, D_MODEL).swapaxes(0, 1).reshape(t * b, D_MODEL)
            ys_t, *ssd_out = _ssd_lanes(_proj_t(x_tm, w["w32"], layer), state["cs_t"], state["h_lanes"], w, layer,
                                        ssd_out, steps=t, batch=b)
            ys = ys_t.reshape(D_MODEL, t, b).transpose(2, 1, 0).reshape(b * t, D_MODEL)
        else:
            ys, *ssd_out = _ssd(grp, p32, state, w, layer, ssd_out)
        hm, *mlstm_out = _mlstm(grp, qkv, p32, state, w, layer, mlstm_out)
        x1 = _merge(ys, hm, p32, x, w, layer, grp.cfg("merge_tm", layer), grp.cfg("merge_parts", layer))
        x, s_ffn = _ffn(x1, state["ffn"] if state is not None else None, w, layer, **grp.ffn)
        ffn_out.append(s_ffn)
    return x, ssd_out, mlstm_out, ffn_out


def _unpack_states(batch, ssd_out, mlstm_out):
    csx, csb, h = ssd_out
    c, n, m = mlstm_out
    return (h.reshape(DEPTH, batch, SSD_HEADS, SSD_HEAD_DIM, SSD_STATE),
            jnp.concatenate([csx, csb], axis=-1), c, n, m[:, :, 0, :MLSTM_HEADS])


def kernel(x_prompt, x_sample, state_ssd, state_ssd_conv, state_mlstm_c, state_mlstm_n, state_mlstm_m,
           state_ffn_conv, w_in, ssd_conv_w, ssd_conv_b, ssd_dt_bias, ssd_a_log, ssd_d, ssd_norm_w,
           mlstm_gate_b, mlstm_norm_w, w_branch_a, w_branch_b, w_out, ln1_g, ln1_b, ffn_w_up, ffn_conv_w,
           ffn_conv_b, ffn_w_down, ln2_g, ln2_b):
    w = _prep_weights(w_in, ssd_conv_w, ssd_conv_b, ssd_dt_bias, ssd_a_log, ssd_d, ssd_norm_w, mlstm_gate_b,
                      mlstm_norm_w, w_branch_a, w_branch_b, w_out, ln1_g, ln1_b, ffn_w_up, ffn_conv_w,
                      ffn_conv_b, ffn_w_down, ln2_g, ln2_b)
    keep = FFN_CONV - 1
    keep_ssd = SSD_CONV - 1

    bp, lp, _ = x_prompt.shape
    prompt = _Group(bp, lp, CHUNK, CHUNK, gs=1, ssd_cps=4, mlstm_cps=4, proj_tm=512, merge_tm=1024,
                    ffn=dict(groups=bp, tm=1024, seq_len=0, perm_q=CHUNK), perm=True, mlstm_gs=2, ssd_gs=2)
    per = CHUNK // SUBLANES
    xp_rows = x_prompt.reshape(bp, lp // CHUNK, SUBLANES, per, D_MODEL).swapaxes(2, 3)
    y_p, ssd_p, mlstm_p, ffn_p = _trunk(prompt, xp_rows.reshape(bp * lp, D_MODEL), None, w)
    y_p = y_p.reshape(bp, lp // CHUNK, per, SUBLANES, D_MODEL).swapaxes(2, 3)
    st_p = _unpack_states(bp, ssd_p, mlstm_p)
    ffn_conv_p = jnp.stack(ffn_p)[:, :, SUBLANES - 1::SUBLANES, :]

    bs, ls, _ = x_sample.shape
    s_rows = bs * ls
    sample = _Group(bs, ls, ls, ls, gs=8, ssd_cps=1, mlstm_cps=1, proj_tm=s_rows, merge_tm=s_rows,
                    ffn=dict(groups=1, tm=256, seq_len=ls, bounds=tuple(range(0, D_FF + 1, 256))), lanes_ssd=True)
    lane_b = lambda a: jnp.broadcast_to(a.astype(f32)[..., None], a.shape + (bs,))
    w.update(cw_b=lane_b(ssd_conv_w), cb_b=lane_b(ssd_conv_b), dtb_b=lane_b(ssd_dt_bias), alog_b=lane_b(ssd_a_log),
             dexp_b=lane_b(ssd_d), nw_b=lane_b(ssd_norm_w))
    s_state = dict(
        cs_t=jnp.transpose(state_ssd_conv, (0, 3, 2, 1)).reshape(DEPTH, D_MODEL + SSD_BC, keep_ssd * bs),
        h_lanes=jnp.transpose(state_ssd, (0, 2, 3, 4, 1)),
        c=state_mlstm_c, n=state_mlstm_n,
        m=jnp.pad(state_mlstm_m, ((0, 0), (0, 0), (0, SMALL - MLSTM_HEADS)))[:, :, None, :],
        ffn=state_ffn_conv.reshape(DEPTH, bs * keep, 2 * D_FF),
    )
    y_s, (cs_t, h_lanes), mlstm_s, ffn_s = _trunk(sample, x_sample.reshape(s_rows, D_MODEL), s_state, w)
    c_s, n_s, m_s = mlstm_s
    st_s = (jnp.transpose(h_lanes, (0, 4, 1, 2, 3)),
            jnp.transpose(cs_t.reshape(DEPTH, D_MODEL + SSD_BC, keep_ssd, bs), (0, 3, 2, 1)),
            c_s, n_s, m_s[:, :, 0, :MLSTM_HEADS])
    ffn_conv_s = jnp.stack([u.reshape(bs, ls, 2 * D_FF)[:, ls - keep:, :] for u in ffn_s])
    y_sample = y_s.reshape(bs, ls, D_MODEL)

    return (y_p.reshape(bp, lp, D_MODEL), y_sample, st_p[0], st_s[0], st_p[1], st_s[1], st_p[2], st_s[2],
            st_p[3], st_s[3], st_p[4], st_s[4], ffn_conv_p, ffn_conv_s)
```

```python
import functools
import itertools

import jax
import jax.numpy as jnp
import numpy as np
from jax import lax
from jax.experimental import pallas as pl
from jax.experimental.pallas import tpu as pltpu

f32 = jnp.float32
bf16 = jnp.bfloat16

D_MODEL = 1024
DEPTH = 2
SSD_HEADS = 16
SSD_HEAD_DIM = 64
SSD_STATE = 64
SSD_GROUPS = 2
SSD_CONV = 4
SSD_BC = 2 * SSD_GROUPS * SSD_STATE
SSD_HP = SSD_HEADS * SSD_HEAD_DIM
MLSTM_HEADS = 4
MLSTM_HEAD_DIM = 256
CHUNK = 128
D_FF = 2816
FFN_CONV = 3
ALPHA = (2 * DEPTH) ** 0.25
EPS = 1e-5

LANES = 128
SUBLANES = 8
SMALL = LANES
DT_OFF, I_OFF, F_OFF = 0, 16, 20
P32_Z, P32_O, P32_GA, P32_GB, P32_XS = 0, 1, 2, 3, 4
P32_BC_OFF = 5 * D_MODEL
P32_SM_OFF = P32_BC_OFF + SSD_BC
P32_W = P32_SM_OFF + 2 * SMALL
FF_BOUNDS = (0, 256, 1280, 2304, 2816)
MERGE_PARTS = 2
FF_UP_AHEAD = 2
FF_XP_SLOTS = 4
NEG_BIG = -1e30
STATE_BUFFERS = 3

NT_DIMS = (((1,), (1,)), ((), ()))
TN_DIMS = (((0,), (0,)), ((), ()))


def _dot(a, b):
    return jnp.dot(a, b, preferred_element_type=f32)


def _split3(x):
    hi = x.astype(bf16)
    r = x - hi.astype(f32)
    mid = r.astype(bf16)
    lo = (r - mid.astype(f32)).astype(bf16)
    return hi, mid, lo


def _dot01_rhs(x, e):
    hi, mid, lo = _split3(x)
    return _dot(hi, e) + _dot(mid, e) + _dot(lo, e)


def _dot01_lhs(t, x):
    hi, mid, lo = _split3(x)
    return _dot(t, hi) + _dot(t, mid) + _dot(t, lo)


def _softplus(x):
    return jnp.maximum(x, 0.0) + jnp.log1p(jnp.exp(-jnp.abs(x)))


def _silu(x):
    return x * jax.nn.sigmoid(x)


def _row_time(i, q, perm):
    if not perm:
        return i
    return (i & (SUBLANES - 1)) * (q // SUBLANES) + lax.shift_right_logical(i, SUBLANES.bit_length() - 1)


def _tri(q, perm=False):
    row = lax.broadcasted_iota(jnp.int32, (q, q), 0)
    col = lax.broadcasted_iota(jnp.int32, (q, q), 1)
    return _row_time(row, q, perm) >= _row_time(col, q, perm)


def _wrap_rows(cur_tail, prev_tail):
    out = []
    for i in range(cur_tail.shape[0] // SUBLANES):
        rows = slice(i * SUBLANES, (i + 1) * SUBLANES)
        first = lax.broadcasted_iota(jnp.int32, (SUBLANES, cur_tail.shape[1]), 0) == 0
        out.append(jnp.where(first, pltpu.roll(prev_tail[rows], 1, axis=0), pltpu.roll(cur_tail[rows], 1, axis=0)))
    return jnp.concatenate(out, axis=0)


def _shift_back(x, wrapped, j):
    n = j * SUBLANES
    return jnp.concatenate([wrapped[wrapped.shape[0] - n:], x[:x.shape[0] - n]], axis=0)


def _valid_rows(q, width, lr, is_last):
    row = lax.broadcasted_iota(jnp.int32, (q, width), 0)
    return row < jnp.where(is_last, lr, q)


def _layer_norm(r, g, b):
    mu = jnp.mean(r, axis=-1, keepdims=True)
    var = jnp.mean(jnp.square(r - mu), axis=-1, keepdims=True)
    return (r - mu) * lax.rsqrt(var + EPS) * g + b


def _layer_spec(layer, *shape, **kw):
    zeros = (0,) * len(shape)
    return pl.BlockSpec((None,) + shape, lambda *_: (layer,) + zeros, **kw)


def _seq_spec(layer, gs, *shape, **kw):
    zeros = (0,) * len(shape)
    return pl.BlockSpec((None, gs) + shape, lambda b, c: (layer, b) + zeros, **kw)


def _stacked_call(kern, *, name, grid, inputs, in_specs, out_shape, out_specs, stacked, scratch_shapes,
                  dimension_semantics, vmem_limit_bytes=None):
    prev = [(i, a) for i, a in sorted(stacked.items()) if a is not None]
    n_in = len(inputs)

    def body(*refs):
        kern(*refs[:n_in], *refs[n_in + len(prev):])

    return pl.pallas_call(
        body,
        out_shape=out_shape,
        grid=grid,
        in_specs=list(in_specs) + [pl.BlockSpec(memory_space=pl.ANY)] * len(prev),
        out_specs=out_specs,
        scratch_shapes=scratch_shapes,
        input_output_aliases={n_in + k: i for k, (i, _) in enumerate(prev)},
        compiler_params=pltpu.CompilerParams(dimension_semantics=dimension_semantics,
                                             vmem_limit_bytes=vmem_limit_bytes),
        name=name,
    )(*inputs, *[a for _, a in prev])


def _proj_kernel(x_ref, w32_ref, wqkv_ref, o32_ref, oqkv_ref):
    xb = x_ref[...].astype(bf16)
    o32_ref[...] = lax.dot_general(xb, w32_ref[...], NT_DIMS, preferred_element_type=f32)
    oqkv_ref[...] = lax.dot_general(xb, wqkv_ref[...], NT_DIMS, preferred_element_type=f32).astype(oqkv_ref.dtype)


def _proj(x, w32, wqkv, layer, tm):
    m, k = x.shape
    n32, nq = w32.shape[1], wqkv.shape[1]
    once = dict(pipeline_mode=pl.Buffered(1))
    return pl.pallas_call(
        _proj_kernel,
        out_shape=(jax.ShapeDtypeStruct((m, n32), f32), jax.ShapeDtypeStruct((m, nq), bf16)),
        grid=(m // tm,),
        in_specs=[pl.BlockSpec((tm, k), lambda i: (i, 0)),
                  _layer_spec(layer, n32, k, **once), _layer_spec(layer, nq, k, **once)],
        out_specs=(pl.BlockSpec((tm, n32), lambda i: (i, 0)), pl.BlockSpec((tm, nq), lambda i: (i, 0))),
        compiler_params=pltpu.CompilerParams(dimension_semantics=("parallel",),
                                             vmem_limit_bytes=56 * 1024 * 1024),
        name="proj",
    )(x, w32, wqkv)


class _Rows:
    def __init__(self, ref, start, n):
        self.ref, self.start, self.n, self.dtype = ref, start, n, ref.dtype

    def rows(self, off, n):
        return _Rows(self.ref, self.start + off, n)

    def _index(self, idx):
        cols = slice(None) if idx is Ellipsis else idx[1]
        return (slice(self.start, self.start + self.n), cols)

    def __getitem__(self, idx):
        return self.ref[self._index(idx)]

    def __setitem__(self, idx, value):
        self.ref[self._index(idx)] = value


def _per_sequence(seq_fn, refs, n_tile, n_state, n_param, gs, has_state, nc, rows, tile3d=False):
    n_state = n_state if has_state else 0
    tiles, refs = refs[:n_tile], refs[n_tile:]
    state, refs = refs[:n_state], refs[n_state:]
    params, (y_tile, *rest) = refs[:n_param], refs[n_param:]
    phases = []
    for g in range(gs):
        at = lambda group: tuple(r.at[g] for r in group)
        if tile3d:
            seq_rows = lambda group: tuple(_Rows(r.at[g], 0, rows) for r in group)
        else:
            seq_rows = lambda group: tuple(_Rows(r, g * rows, rows) for r in group)
        phases.append(seq_fn(*seq_rows(tiles), *at(state), *params, *seq_rows((y_tile,)), *at(rest)))
    c = pl.program_id(1)

    @pl.when(c == 0)
    def _():
        for init, _, _ in phases:
            init()

    for _ in itertools.zip_longest(*[body() for _, body, _ in phases]):
        pass

    @pl.when(c == nc - 1)
    def _():
        for _, _, final in phases:
            final()


def _ssd_kernel(*refs, q, lr, nc, has_state, gs, cps, perm, tile3d):
    seq = functools.partial(_ssd_seq, q=q, lr=lr, nc=nc, has_state=has_state, cps=cps, perm=perm)
    _per_sequence(seq, refs, 4, 3, 10, gs, has_state, nc, cps * q, tile3d)


def _ssd_seq(*refs, q, lr, nc, has_state, cps, perm):
    z_ref, xs_ref, bc_ref, sm_ref = refs[:4]
    refs = refs[4:]
    if has_state:
        csx_ref, csb_ref, h0_ref = refs[:3]
        refs = refs[3:]
    (cwx_ref, cbx_ref, cwb_ref, cbb_ref, dtb_ref, alog_ref, dexp_ref, nw_ref, e_ref, bd_ref,
     y_ref, ncsx_ref, ncsb_ref, hout_ref, xpx, xpb, ht, yb) = refs
    hdr = SUBLANES
    lo = hdr - (SSD_CONV - 1)
    n2 = SSD_GROUPS * SSD_STATE
    assert lr >= SSD_CONV - 1
    keep = SSD_CONV - 1
    assert not (perm and (has_state or lr != q))
    carried = [(i + 1) * SUBLANES - 1 for i in range(keep)]

    def init():
        if has_state:
            h_t = h0_ref[...].T
            ht[...] = jnp.where(bd_ref[...] > 0.5, jnp.concatenate([h_t, h_t], axis=0), 0.0)
            xpx[lo:hdr, :] = csx_ref[...]
            xpb[lo:hdr, :] = csb_ref[...]
        else:
            ht[...] = jnp.zeros_like(ht)
            rows = slice(0, keep * SUBLANES) if perm else slice(lo, hdr)
            xpx[rows, :] = jnp.zeros((rows.stop - rows.start, SSD_HP), f32)
            xpb[rows, :] = jnp.zeros((rows.stop - rows.start, SSD_BC), f32)

    def final():
        if perm:
            for i, r in enumerate(carried):
                ncsx_ref[i:i + 1, :] = xpx[r:r + 1, :]
                ncsb_ref[i:i + 1, :] = xpb[r:r + 1, :]
        else:
            ncsx_ref[...] = xpx[lo + lr:hdr + lr, :]
            ncsb_ref[...] = xpb[lo + lr:hdr + lr, :]
        h_new = ht[...]
        hout_ref[...] = (h_new[:SSD_STATE, :] + h_new[SSD_STATE:, :]).T

    def body():
        for k in range(cps):
            sub = lambda r: r.rows(k * q, q)
            is_last = (pl.program_id(1) == nc - 1) if k == cps - 1 else False
            yield from _ssd_body(sub(z_ref), sub(xs_ref), sub(bc_ref), sub(sm_ref), cwx_ref, cbx_ref, cwb_ref,
                                 cbb_ref, dtb_ref, alog_ref, dexp_ref, nw_ref, e_ref, bd_ref, sub(y_ref),
                                 xpx, xpb, ht, yb, q=q, lr=lr, is_last=is_last, perm=perm)

    return init, body, final


def _ssd_body(z_ref, xs_ref, bc_ref, sm_ref, cwx_ref, cbx_ref, cwb_ref, cbb_ref, dtb_ref, alog_ref,
              dexp_ref, nw_ref, e_ref, bd_ref, y_ref, xpx, xpb, ht, yb, *, q, lr, is_last, perm):
    hdr = SUBLANES
    lo = hdr - (SSD_CONV - 1)
    n2 = SSD_GROUPS * SSD_STATE
    block_diag = bd_ref[...] > 0.5

    dt = _softplus(sm_ref[...] + dtb_ref[...])
    if lr < q:
        dt = jnp.where(_valid_rows(q, SMALL, lr, is_last), dt, 0.0)
    a = -jnp.exp(alog_ref[...])
    d_a = dt * a
    causal = _tri(q, perm)
    tril = jnp.where(causal, 1.0, 0.0).astype(bf16)
    e = e_ref[...]
    acs = _dot01_lhs(tril, d_a)
    dt_x = _dot01_rhs(dt, e)
    yield

    if perm:
        keep_rows = (SSD_CONV - 1) * SUBLANES

        def conv(xp, x_ref, w_ref, b_ref):
            w = w_ref[...]
            x = x_ref[...]
            wrapped = _wrap_rows(x[q - keep_rows:, :], xp[0:keep_rows, :])
            acc = _shift_back(x, wrapped, SSD_CONV - 1) * w[0:1, :]
            for j in range(1, SSD_CONV - 1):
                acc = acc + _shift_back(x, wrapped, SSD_CONV - 1 - j) * w[j:j + 1, :]
            acc = acc + x * w[SSD_CONV - 1:SSD_CONV, :]
            xp[0:keep_rows, :] = x[q - keep_rows:, :]
            return acc + b_ref[...]
    else:
        xpx[hdr:hdr + q, :] = xs_ref[...]
        xpb[hdr:hdr + q, :] = bc_ref[...]

        def conv(xp, x_ref, w_ref, b_ref):
            w = w_ref[...]
            acc = xp[lo:lo + q, :] * w[0:1, :]
            for j in range(1, SSD_CONV):
                acc = acc + xp[lo + j:lo + j + q, :] * w[j:j + 1, :]
            return acc + b_ref[...]

    cb = conv(xpb, bc_ref, cwb_ref, cbb_ref)
    bcv = _silu(cb)
    bm = bcv[:, :n2].astype(bf16)
    cm = bcv[:, n2:]
    lane_g0 = lax.broadcasted_iota(jnp.int32, (q, n2), 1) < SSD_STATE
    acs_t = acs.T
    acs_x = _dot01_rhs(acs, e)
    yield
    cbms = [lax.dot_general(jnp.where(lane_g0 if g == 0 else jnp.logical_not(lane_g0), cm, 0.0).astype(bf16),
                            bm, NT_DIMS, preferred_element_type=f32) for g in range(SSD_GROUPS)]
    h_prev = ht[...]
    y_off = _dot(cm.astype(bf16), h_prev.astype(bf16))
    cx = conv(xpx, xs_ref, cwx_ref, cbx_ref)
    if not perm:
        tail_x = xpx[lo + q:hdr + q, :]
        tail_b = xpb[lo + q:hdr + q, :]
        xpx[lo:hdr, :] = tail_x
        xpb[lo:hdr, :] = tail_b
    yield
    xs = _silu(cx)
    last_x = acs_x[q - 1:q, :]
    xdt = xs * dt_x
    xdt_b = xdt.astype(bf16)
    yield
    lane_lo = lax.broadcasted_iota(jnp.int32, (q, LANES), 1) < SSD_HEAD_DIM
    heads_per_group = SSD_HEADS // SSD_GROUPS
    decays = [jnp.exp(jnp.where(causal, acs[:, hh:hh + 1] - acs_t[hh:hh + 1, :], -jnp.inf))
              for hh in range(SSD_HEADS)]
    yield
    weights = [(cbms[hh // heads_per_group] * decays[hh]).astype(bf16) for hh in range(SSD_HEADS)]
    xdtw = (xdt * jnp.exp(last_x - acs_x)).astype(bf16)
    yield
    ys = [_dot(weights[hh], xdt_b[:, (hh // 2) * LANES:(hh // 2 + 1) * LANES]) for hh in range(SSD_HEADS)]
    upd = lax.dot_general(bm, xdtw, TN_DIMS, preferred_element_type=f32)
    yield
    for p in range(SSD_HEADS // 2):
        yb[:, p * LANES:(p + 1) * LANES] = jnp.where(lane_lo, ys[2 * p], ys[2 * p + 1])
    ht[...] = jnp.exp(last_x) * h_prev + jnp.where(block_diag, upd, 0.0)
    yield
    y = yb[...] + y_off * jnp.exp(acs_x) + dexp_ref[...] * xs
    y = y * _silu(z_ref[...])
    yield
    y = y * lax.rsqrt(jnp.mean(jnp.square(y), axis=-1, keepdims=True) + EPS) * nw_ref[...]
    y_ref[...] = y.astype(y_ref.dtype)


PT_ROWS = 5 * 512
PT_XS, PT_BC, PT_SM = D_MODEL, 2 * D_MODEL, 2 * D_MODEL + SSD_BC


def _proj_t_kernel(x_ref, w_ref, o_ref, xb):
    @pl.when(pl.program_id(0) == 0)
    def _():
        xb[...] = x_ref[...].astype(bf16)

    o_ref[...] = lax.dot_general(w_ref[...], xb[...], NT_DIMS, preferred_element_type=f32)


def _proj_t(x_tm, w32, layer):
    m, k = x_tm.shape
    tn = 512
    xs_blk = P32_XS * D_MODEL // tn
    return pl.pallas_call(
        _proj_t_kernel,
        out_shape=jax.ShapeDtypeStruct((PT_ROWS, m), f32),
        grid=(PT_ROWS // tn,),
        in_specs=[pl.BlockSpec((m, k), lambda j: (0, 0)),
                  pl.BlockSpec((None, tn, k), lambda j: (layer, jnp.where(j < D_MODEL // tn, j, j + xs_blk - D_MODEL // tn), 0))],
        out_specs=pl.BlockSpec((tn, m), lambda j: (j, 0)),
        scratch_shapes=[pltpu.VMEM((m, k), bf16)],
        compiler_params=pltpu.CompilerParams(dimension_semantics=("arbitrary",)),
        name="proj_t",
    )(x_tm, w32)


def _ssd_lanes_kernel(pt_ref, cst_ref, h0_ref, cw_ref, cb_ref, dtb_ref, alog_ref, dexp_ref, nw_ref,
                      y_ref, ncs_ref, hout_ref, xc, dts, decs, ysc, *, steps, batch):
    hd = pl.program_id(0)
    n_ch = D_MODEL + SSD_BC
    keep = SSD_CONV - 1
    lanes = lambda t: slice(t * batch, (t + 1) * batch)

    @pl.when(hd == 0)
    def _():
        for t in range(steps):
            acc = None
            for j in range(SSD_CONV):
                i = t + j
                src = cst_ref[:, lanes(i)] if i < keep else pt_ref[PT_XS:PT_XS + n_ch, lanes(i - keep)]
                term = src * cw_ref[j]
                acc = term if acc is None else acc + term
            xc[:, lanes(t)] = _silu(acc + cb_ref[...])
        ncs_ref[...] = pt_ref[PT_XS:PT_XS + n_ch, (steps - keep) * batch:steps * batch]
        dt = _softplus(pt_ref[PT_SM:PT_SM + SSD_HEADS, :] + jnp.concatenate([dtb_ref[...]] * steps, axis=1))
        dts[...] = dt
        decs[...] = jnp.exp(dt * jnp.concatenate([-jnp.exp(alog_ref[...])] * steps, axis=1))

    grp_row = (hd // (SSD_HEADS // SSD_GROUPS)) * SSD_STATE
    xh = xc[pl.ds(pl.multiple_of(hd * SSD_HEAD_DIM, SSD_HEAD_DIM), SSD_HEAD_DIM), :]
    bh = xc[pl.ds(pl.multiple_of(D_MODEL + grp_row, SSD_STATE), SSD_STATE), :]
    ch = xc[pl.ds(pl.multiple_of(D_MODEL + SSD_GROUPS * SSD_STATE + grp_row, SSD_STATE), SSD_STATE), :]
    dth = dts[pl.ds(hd, 1), :]
    dech = decs[pl.ds(hd, 1), :]
    d_skip = dexp_ref[pl.ds(hd, 1), :]
    y_rows = [[] for _ in range(steps)]
    for p in range(SSD_HEAD_DIM):
        h = h0_ref[p]
        for t in range(steps):
            x_row = xh[p:p + 1, lanes(t)]
            h = dech[:, lanes(t)] * h + (x_row * dth[:, lanes(t)]) * bh[:, lanes(t)]
            y_rows[t].append(jnp.sum(ch[:, lanes(t)] * h, axis=0, keepdims=True) + d_skip * x_row)
        hout_ref[p] = h
    rows = pl.ds(pl.multiple_of(hd * SSD_HEAD_DIM, SSD_HEAD_DIM), SSD_HEAD_DIM)
    for t in range(steps):
        ysc[rows, lanes(t)] = jnp.concatenate(y_rows[t], axis=0)

    @pl.when(hd == SSD_HEADS - 1)
    def _():
        y = ysc[...] * _silu(pt_ref[0:D_MODEL, :])
        y = y * lax.rsqrt(jnp.mean(jnp.square(y), axis=0, keepdims=True) + EPS)
        y_ref[...] = (y * jnp.concatenate([nw_ref[...]] * steps, axis=1)).astype(y_ref.dtype)


def _ssd_lanes(pt, cst, h0, w, layer, prev, *, steps, batch):
    n_ch = D_MODEL + SSD_BC
    tb = steps * batch
    keep = SSD_CONV - 1
    assert steps >= keep and batch % LANES == 0
    kern = functools.partial(_ssd_lanes_kernel, steps=steps, batch=batch)
    full = lambda *shape: pl.BlockSpec(shape, lambda hd: (0,) * len(shape))
    hblock = pl.BlockSpec((None, None, SSD_HEAD_DIM, SSD_STATE, batch), lambda hd: (layer, hd, 0, 0, 0))
    return _stacked_call(
        kern, name="ssd_lanes", grid=(SSD_HEADS,),
        inputs=[pt, cst, h0, w["cw_b"], w["cb_b"], w["dtb_b"], w["alog_b"], w["dexp_b"], w["nw_b"]],
        in_specs=[full(PT_ROWS, tb), _layer_spec(layer, n_ch, keep * batch), hblock,
                  _layer_spec(layer, SSD_CONV, n_ch, batch), _layer_spec(layer, n_ch, batch),
                  _layer_spec(layer, SSD_HEADS, batch), _layer_spec(layer, SSD_HEADS, batch),
                  _layer_spec(layer, SSD_HEADS, batch), _layer_spec(layer, D_MODEL, batch)],
        out_shape=(jax.ShapeDtypeStruct((D_MODEL, tb), bf16),
                   jax.ShapeDtypeStruct((DEPTH, n_ch, keep * batch), f32),
                   jax.ShapeDtypeStruct((DEPTH, SSD_HEADS, SSD_HEAD_DIM, SSD_STATE, batch), f32)),
        out_specs=(full(D_MODEL, tb), _layer_spec(layer, n_ch, keep * batch), hblock),
        stacked={1: prev and prev[0], 2: prev and prev[1]},
        scratch_shapes=[pltpu.VMEM((n_ch, tb), f32), pltpu.VMEM((SSD_HEADS, tb), f32),
                        pltpu.VMEM((SSD_HEADS, tb), f32), pltpu.VMEM((D_MODEL, tb), f32)],
        dimension_semantics=("arbitrary",))


def _mlstm_kernel(*refs, q, lr, nc, has_state, gs, cps, perm, tile3d, ring, layer, groups):
    seq = functools.partial(_mlstm_seq, q=q, lr=lr, nc=nc, has_state=has_state, cps=cps, perm=perm)
    if ring:
        *refs, cbuf, sem = refs
        c_hbm = refs[5]
        b = pl.program_id(0)

        def copy(k):
            slot = lax.rem(k, STATE_BUFFERS)
            return pltpu.make_async_copy(c_hbm.at[layer, pl.ds(k * gs, gs)], cbuf.at[slot], sem.at[slot])

        @pl.when(b == 0)
        def _():
            for k in range(min(STATE_BUFFERS - 1, groups)):
                copy(k).start()

        @pl.when(b + STATE_BUFFERS - 1 < groups)
        def _():
            copy(b + STATE_BUFFERS - 1).start()

        copy(b).wait()
        refs = list(refs)
        refs[5] = cbuf.at[lax.rem(b, STATE_BUFFERS)]
    _per_sequence(seq, refs, 5, 3, 2, gs, has_state, nc, cps * q, tile3d)


def _mlstm_seq(*refs, q, lr, nc, has_state, cps, perm):
    q_ref, k_ref, v_ref, o_ref, sm_ref = refs[:5]
    refs = refs[5:]
    if has_state:
        c0_ref, n0_ref, m0_ref = refs[:3]
        refs = refs[3:]
    gb_ref, nw_ref, h_ref, cout_ref, nout_ref, mout_ref, cs, ns, ms = refs
    direct = has_state and nc == 1 and cps == 1

    def init():
        if direct:
            return
        if has_state:
            cs[...] = c0_ref[...]
            ns[...] = n0_ref[...]
            ms[...] = m0_ref[...]
        else:
            cs[...] = jnp.zeros_like(cs)
            ns[...] = jnp.zeros_like(ns)
            ms[...] = jnp.zeros_like(ms)

    def final():
        if direct:
            return
        cout_ref[...] = cs[...]
        nout_ref[...] = ns[...]
        mout_ref[...] = ms[...]

    def body():
        chunks = []
        for k in range(cps):
            sub = lambda r, k=k: r.rows(k * q, q)
            is_last = (pl.program_id(1) == nc - 1) if k == cps - 1 else False
            src = (c0_ref, n0_ref, m0_ref) if direct else (cs, ns, ms)
            dst = (cout_ref, nout_ref, mout_ref) if direct else (cs, ns, ms)
            chunks.append(_mlstm_body(sub(q_ref), sub(k_ref), sub(v_ref), sub(o_ref), sub(sm_ref), gb_ref, nw_ref,
                                      sub(h_ref), src, dst, q=q, lr=lr, is_last=is_last, perm=perm))
        yield from _staggered(chunks, MLSTM_STATE_STAGES)

    return init, body, final


MLSTM_STATE_STAGES = 8


def _staggered(gens, skew):
    done = [False] * len(gens)
    t = 0
    while not all(done):
        for i, g in enumerate(gens):
            if done[i] or t < i * skew:
                continue
            try:
                next(g)
            except StopIteration:
                done[i] = True
        t += 1
        yield


def _mlstm_body(q_ref, k_ref, v_ref, o_ref, sm_ref, gb_ref, nw_ref, h_ref, src, dst, *, q, lr, is_last, perm):
    c_src, n_src, m_src = src
    c_dst, n_dst, m_dst = dst
    sm = sm_ref[...] + gb_ref[...]
    logf = -_softplus(-sm)
    ipre = sm
    if lr < q:
        valid = _valid_rows(q, SMALL, lr, is_last)
        logf = jnp.where(valid, logf, 0.0)
        ipre = jnp.where(valid, ipre, NEG_BIG)
    causal = _tri(q, perm)
    tril = jnp.where(causal, 1.0, 0.0).astype(bf16)
    yield
    bcum = _dot01_lhs(tril, logf)
    ipre_t = ipre.T
    yield
    bcum_t = bcum.T
    lane = lax.broadcasted_iota(jnp.int32, (1, SMALL), 1)
    k_scale = MLSTM_HEAD_DIM ** -0.5

    heads = range(MLSTM_HEADS)
    sls = [slice(h * MLSTM_HEAD_DIM, (h + 1) * MLSTM_HEAD_DIM) for h in heads]
    q_all, k_all, v_all, o_all = q_ref[...], k_ref[...], v_ref[...], o_ref[...]
    qs = [q_all[:, sl] for sl in sls]
    ks = [k_all[:, sl] * k_scale for sl in sls]
    vs = [v_all[:, sl] for sl in sls]
    b_cols = [bcum[:, F_OFF + h:F_OFF + h + 1] for h in heads]
    i_cols = [ipre[:, I_OFF + h:I_OFF + h + 1] for h in heads]
    dmats = [jnp.where(causal, b_cols[h] - bcum_t[F_OFF + h:F_OFF + h + 1, :] + ipre_t[I_OFF + h:I_OFF + h + 1, :],
                       -jnp.inf) for h in heads]
    yield
    qk = [lax.dot_general(qs[h], ks[h], NT_DIMS, preferred_element_type=f32) for h in heads]
    d_max = [jnp.max(dmats[h], axis=-1, keepdims=True) for h in heads]
    yield
    n_all = n_src[...]
    m_all = m_src[...]
    m_new = m_all
    cs_in = [c_src[h] for h in heads]
    m_prevs = [m_all[:, h:h + 1] for h in heads]
    qc = [_dot(qs[h], cs_in[h].astype(bf16)) for h in heads]
    inters = [b_cols[h] + m_prevs[h] for h in heads]
    m_ts = [jnp.maximum(inters[h], d_max[h]) for h in heads]
    yield
    w_inters = [jnp.exp(inters[h] - m_ts[h]) for h in heads]
    ss = [qk[h] * jnp.exp(dmats[h] - m_ts[h]) for h in heads]
    yield
    sv = [_dot(ss[h].astype(bf16), vs[h]) for h in heads]
    m_ends = [m_ts[h][q - 1:q, :] for h in heads]
    b_lasts = [b_cols[h][q - 1:q, :] for h in heads]
    kws = [ks[h].astype(f32) * jnp.exp(b_lasts[h] - b_cols[h] + i_cols[h] - m_ends[h]) for h in heads]
    yield
    kv = [lax.dot_general(kws[h].astype(bf16), vs[h], TN_DIMS, preferred_element_type=f32) for h in heads]
    qns = [jnp.sum(qs[h].astype(f32) * n_all[h:h + 1, :], axis=-1, keepdims=True) for h in heads]
    yield
    dens = [jnp.sum(ss[h], axis=-1, keepdims=True) + w_inters[h] * qns[h] for h in heads]
    yield
    hvs = [(sv[h] + w_inters[h] * qc[h]) / jnp.maximum(jnp.abs(dens[h]), jnp.exp(-m_ts[h])) for h in heads]
    yield
    rms = [lax.rsqrt(jnp.mean(jnp.square(hvs[h]), axis=-1, keepdims=True) + EPS) for h in heads]
    yield
    h_new = [(hvs[h] * rms[h] * nw_ref[:, sls[h]] * jax.nn.sigmoid(o_all[:, sls[h]])).astype(h_ref.dtype)
             for h in heads]
    w_cs = [jnp.exp(b_lasts[h] + m_prevs[h] - m_ends[h]) for h in heads]
    yield
    for h in heads:
        c_dst[h] = w_cs[h] * cs_in[h] + kv[h]
        m_new = jnp.where(lane == h, m_ends[h], m_new)
    h_ref[...] = jnp.concatenate(h_new, axis=1)
    n_dst[...] = jnp.concatenate(
        [w_cs[h] * n_all[h:h + 1, :] + jnp.sum(kws[h], axis=0, keepdims=True) for h in heads], axis=0)
    m_dst[...] = m_new


def _merge_kernel(ys_ref, hm_ref, ga_ref, gb_ref, x_ref, wa_ref, wb_ref, wo_ref, g_ref, b_ref, o_ref, *, parts):
    tm = x_ref.shape[0]
    rows = [pl.ds(i * (tm // parts), tm // parts) for i in range(parts)]
    br = [(_dot(ys_ref[r, :], wa_ref[...]), _dot(hm_ref[r, :], wb_ref[...])) for r in rows]
    merged = [(jax.nn.sigmoid(ga_ref[r, :]) * a + jax.nn.sigmoid(gb_ref[r, :]) * b).astype(bf16)
              for r, (a, b) in zip(rows, br)]
    mix = [_dot(m, wo_ref[...]) for m in merged]
    for r, m in zip(rows, mix):
        o_ref[r, :] = _layer_norm(ALPHA * x_ref[r, :] + m, g_ref[...], b_ref[...])


def _merge(ys, hm, p32, x, w, layer, tm, parts=MERGE_PARTS):
    m = x.shape[0]
    row = lambda blk: pl.BlockSpec((tm, D_MODEL), lambda i: (i, blk))
    return pl.pallas_call(
        functools.partial(_merge_kernel, parts=parts),
        out_shape=jax.ShapeDtypeStruct((m, D_MODEL), f32),
        grid=(m // tm,),
        in_specs=[row(0), row(0), row(P32_GA), row(P32_GB), row(0),
                  _layer_spec(layer, D_MODEL, D_MODEL), _layer_spec(layer, D_MODEL, D_MODEL),
                  _layer_spec(layer, D_MODEL, D_MODEL), _layer_spec(layer, 1, D_MODEL),
                  _layer_spec(layer, 1, D_MODEL)],
        out_specs=row(0),
        compiler_params=pltpu.CompilerParams(dimension_semantics=("parallel",)),
        name="merge",
    )(ys, hm, p32, p32, x, w["wa"], w["wb"], w["wo"], w["ln1_g"], w["ln1_b"])


def _ffn_kernel(*refs, tm, seq_len, perm_q, ahead_n, bounds):
    multi = seq_len > 0
    keep_rows = (FFN_CONV - 1) * SUBLANES
    n_chunks = len(bounds) - 1
    if multi:
        x_ref, st_ref, wup_ref, cw_ref, cb_ref, wdn_ref, g_ref, b_ref, o_ref, sout_ref, xp = refs
    else:
        x_ref, wup_ref, cw_ref, cb_ref, wdn_ref, g_ref, b_ref, o_ref, sout_ref, xp, carry = refs
        @pl.when(pl.program_id(1) == 0)
        def _():
            carry[...] = jnp.zeros_like(carry)

    hdr = SUBLANES
    x = x_ref[...]
    xb = x.astype(bf16)
    if multi:
        assert seq_len & (seq_len - 1) == 0
        nseq = tm // seq_len
        t = lax.broadcasted_iota(jnp.int32, (tm, 1), 0) & (seq_len - 1)
        row = lax.broadcasted_iota(jnp.int32, (tm, 2 * nseq), 0)
        col = lax.broadcasted_iota(jnp.int32, (tm, 2 * nseq), 1)
        t_sel = row & (seq_len - 1)
        seq0 = lax.shift_right_logical(row - t_sel, (seq_len // 2).bit_length() - 1)
        sel_p2 = jnp.where(col == seq0 + t_sel, jnp.where(t_sel < 2, 1.0, 0.0), 0.0).astype(bf16)
        sel_p1 = jnp.where(col == seq0 + 1, jnp.where(t_sel == 0, 1.0, 0.0), 0.0).astype(bf16)
        xp[:, 0:hdr, :] = jnp.zeros((FF_XP_SLOTS, hdr, xp.shape[2]), f32)

    def cols_of(c, part):
        return slice(part * D_FF + bounds[c], part * D_FF + bounds[c + 1])

    def up(c):
        return [_dot(xb, wup_ref[:, cols_of(c, part)]) for part in range(2)]

    def conv_act(c, us):
        halves = []
        for part, u in enumerate(us):
            cols = cols_of(c, part)
            slot = (2 * c + part) % FF_XP_SLOTS
            if perm_q:
                prev = carry[:, cols]
                p1, p2 = [], []
                for kk in range(tm // perm_q):
                    uc = u[kk * perm_q:(kk + 1) * perm_q, :]
                    wrapped = _wrap_rows(uc[perm_q - keep_rows:, :], prev)
                    p1.append(_shift_back(uc, wrapped, 1))
                    p2.append(_shift_back(uc, wrapped, 2))
                    prev = uc[perm_q - keep_rows:, :]
                carry[:, cols] = prev
                p1, p2 = jnp.concatenate(p1, axis=0), jnp.concatenate(p2, axis=0)
                w = cw_ref[:, cols]
                halves.append(p2 * w[0:1, :] + p1 * w[1:2, :] + u * w[2:3, :] + cb_ref[:, cols])
                continue
            wd = slice(0, u.shape[1])
            xp[slot, hdr:hdr + tm, wd] = u
            if multi:
                sout_ref[:, cols] = u
                st = st_ref[:, cols]
                p1 = jnp.where(t == 0, _dot01_lhs(sel_p1, st), xp[slot, hdr - 1:hdr - 1 + tm, wd])
                p2 = jnp.where(t < 2, _dot01_lhs(sel_p2, st), xp[slot, hdr - 2:hdr - 2 + tm, wd])
            else:
                xp[slot, 0:hdr, wd] = carry[:, cols]
                p1 = xp[slot, hdr - 1:hdr - 1 + tm, wd]
                p2 = xp[slot, hdr - 2:hdr - 2 + tm, wd]
                carry[:, cols] = u[tm - hdr:tm, :]
            w = cw_ref[:, cols]
            halves.append(p2 * w[0:1, :] + p1 * w[1:2, :] + u * w[2:3, :] + cb_ref[:, cols])
        return (_silu(halves[0]) * halves[1]).astype(bf16)

    acc = None
    ahead = [up(c) for c in range(min(ahead_n, n_chunks))]
    pending = None
    for c in range(n_chunks):
        if c + ahead_n < n_chunks:
            ahead.append(up(c + ahead_n))
        if pending is not None:
            d = _dot(pending, wdn_ref[bounds[c - 1]:bounds[c], :])
            acc = d if acc is None else acc + d
        pending = conv_act(c, ahead.pop(0))
    acc = acc + _dot(pending, wdn_ref[bounds[n_chunks - 1]:bounds[n_chunks], :])

    if not multi:
        sout_ref[...] = carry[...]
    o_ref[...] = _layer_norm(ALPHA * x + acc, g_ref[...], b_ref[...])


def _ffn(x, st, w, layer, *, groups, tm, seq_len, perm_q=0, ahead=FF_UP_AHEAD, bounds=FF_BOUNDS):
    m = x.shape[0]
    bounds = bounds[layer] if isinstance(bounds[0], tuple) else bounds
    chunk_w = max(b - a for a, b in zip(bounds, bounds[1:]))
    tiles = m // (groups * tm)
    multi = seq_len > 0
    kern = functools.partial(_ffn_kernel, tm=tm, seq_len=seq_len, perm_q=perm_q, ahead_n=ahead, bounds=bounds)
    carry_rows = (FFN_CONV - 1) * SUBLANES if perm_q else SUBLANES
    once = dict(pipeline_mode=pl.Buffered(1))
    x_spec = pl.BlockSpec((tm, D_MODEL), lambda s, j: (s * tiles + j, 0))
    w_specs = [_layer_spec(layer, D_MODEL, 2 * D_FF, **once), _layer_spec(layer, FFN_CONV, 2 * D_FF, **once),
               _layer_spec(layer, 1, 2 * D_FF, **once), _layer_spec(layer, D_FF, D_MODEL, **once),
               _layer_spec(layer, 1, D_MODEL, **once), _layer_spec(layer, 1, D_MODEL, **once)]
    w_args = (w["wup"], w["fcw"], w["fcb"], w["wdn"], w["ln2_g"], w["ln2_b"])
    xp = pltpu.VMEM((FF_XP_SLOTS, SUBLANES + (0 if perm_q else tm), chunk_w), f32)
    if multi:
        nst = 2 * (tm // seq_len)
        inputs = (x, st) + w_args
        in_specs = [x_spec, pl.BlockSpec((None, nst, 2 * D_FF), lambda s, j: (layer, s * tiles + j, 0))] + w_specs
        sout_shape = jax.ShapeDtypeStruct((m, 2 * D_FF), f32)
        sout_spec = pl.BlockSpec((tm, 2 * D_FF), lambda s, j: (s * tiles + j, 0))
        scratch = [xp]
    else:
        inputs = (x,) + w_args
        in_specs = [x_spec] + w_specs
        sout_shape = jax.ShapeDtypeStruct((groups, carry_rows, 2 * D_FF), f32)
        sout_spec = pl.BlockSpec((None, carry_rows, 2 * D_FF), lambda s, j: (s, 0, 0))
        scratch = [xp, pltpu.VMEM((carry_rows, 2 * D_FF), f32)]
    return pl.pallas_call(
        kern,
        out_shape=(jax.ShapeDtypeStruct((m, D_MODEL), f32), sout_shape),
        grid=(groups, tiles),
        in_specs=in_specs,
        out_specs=(x_spec, sout_spec),
        scratch_shapes=scratch,
        compiler_params=pltpu.CompilerParams(dimension_semantics=("parallel", "arbitrary"),
                                             vmem_limit_bytes=56 * 1024 * 1024),
        name="ffn",
    )(*inputs)


def _pad_lanes(v, off, width=SMALL):
    out = jnp.zeros((v.shape[0], 1, width), f32)
    return out.at[:, 0, off:off + v.shape[1]].set(v.astype(f32))


def _prep_weights(w_in, ssd_conv_w, ssd_conv_b, ssd_dt_bias, ssd_a_log, ssd_d, ssd_norm_w, mlstm_gate_b,
                  mlstm_norm_w, w_branch_a, w_branch_b, w_out, ln1_g, ln1_b, ffn_w_up, ffn_conv_w,
                  ffn_conv_b, ffn_w_down, ln2_g, ln2_b):
    d = D_MODEL
    o_z, o_xbc, o_dt = 0, d, d + d + SSD_BC
    o_q = o_dt + SSD_HEADS
    o_if = o_q + 3 * d
    o_o = o_if + 2 * MLSTM_HEADS
    o_g = o_o + d
    w_t = jnp.swapaxes(w_in, 1, 2)
    cols = lambda a, n: w_t[:, a:a + n, :]
    zeros = lambda n: jnp.zeros((DEPTH, n, d), w_in.dtype)
    w32 = jnp.concatenate([cols(o_z, d), cols(o_o, d), cols(o_g, 2 * d), cols(o_xbc, d + SSD_BC),
                           cols(o_dt, SSD_HEADS), cols(o_if, 2 * MLSTM_HEADS),
                           zeros(P32_W - P32_SM_OFF - SSD_HEADS - 2 * MLSTM_HEADS)], axis=1).astype(bf16)
    e = (np.arange(SSD_HP)[None, :] // SSD_HEAD_DIM == np.arange(LANES)[:, None])
    bd = ((np.arange(SSD_GROUPS * SSD_STATE)[:, None] < SSD_STATE)
          == (np.arange(SSD_HP)[None, :] < SSD_HP // SSD_GROUPS))
    row = lambda a: a[:, None, :]
    return dict(
        w32=w32, wqkv=cols(o_q, 3 * d).astype(bf16),
        cwx=ssd_conv_w[:, :, :d], cbx=row(ssd_conv_b[:, :d]),
        cwb=ssd_conv_w[:, :, d:], cbb=row(ssd_conv_b[:, d:]),
        dtb=_pad_lanes(ssd_dt_bias, DT_OFF), alog=_pad_lanes(ssd_a_log, DT_OFF),
        dexp=row(jnp.repeat(ssd_d.astype(f32), SSD_HEAD_DIM, axis=1)), ssd_nw=row(ssd_norm_w),
        e=jnp.asarray(e, bf16), bd=jnp.asarray(bd, f32),
        gate_b=_pad_lanes(mlstm_gate_b, I_OFF), mlstm_nw=row(mlstm_norm_w),
        wa=w_branch_a.astype(bf16), wb=w_branch_b.astype(bf16), wo=w_out.astype(bf16),
        ln1_g=row(ln1_g), ln1_b=row(ln1_b),
        wup=ffn_w_up.astype(bf16), fcw=ffn_conv_w, fcb=row(ffn_conv_b), wdn=ffn_w_down.astype(bf16),
        ln2_g=row(ln2_g), ln2_b=row(ln2_b),
    )


class _Group:
    def __init__(self, batch, length, q, lr, gs, ssd_cps, mlstm_cps, proj_tm, merge_tm, ffn,
                 perm=False, lanes_ssd=False, mlstm_gs=None, ssd_gs=None, merge_parts=MERGE_PARTS):
        self.mlstm_gs = gs if mlstm_gs is None else mlstm_gs
        self.ssd_gs = gs if ssd_gs is None else ssd_gs
        self.merge_parts = merge_parts
        self.perm = perm
        self.lanes_ssd = lanes_ssd
        self.batch, self.length, self.q, self.lr, self.gs = batch, length, q, lr, gs
        self.ssd_cps, self.mlstm_cps = ssd_cps, mlstm_cps
        self.rows = batch * length
        self.proj_tm, self.merge_tm, self.ffn = proj_tm, merge_tm, ffn

    def cfg(self, name, layer):
        v = getattr(self, name)
        return v[layer] if isinstance(v, tuple) else v

    def tiling(self, cps, gs=None):
        rows = cps * self.q
        gs = self.gs if gs is None else gs
        steps = self.length // rows
        if gs == 1 or steps == 1:
            spec = lambda width, blk: pl.BlockSpec((gs * rows, width), lambda b, c: (b * steps + c, blk))
            return steps, spec, (lambda a: a), False
        spec = lambda width, blk: pl.BlockSpec((gs, rows, width), lambda b, c: (b, c, blk))
        return steps, spec, (lambda a: a.reshape(self.batch, self.length, a.shape[-1])), True


def _ssd(grp, p32, state, w, layer, prev):
    q, b, gs = grp.q, grp.batch, grp.cfg("ssd_gs", layer)
    has_state = state is not None
    cps = grp.cfg("ssd_cps", layer)
    steps, tile, view, tile3d = grp.tiling(cps, gs)
    kern = functools.partial(_ssd_kernel, q=q, lr=grp.lr, nc=steps, has_state=has_state, gs=gs, cps=cps,
                             perm=grp.perm, tile3d=tile3d)
    p32 = view(p32)
    inputs = [p32, p32, p32, p32]
    in_specs = [tile(D_MODEL, P32_Z), tile(D_MODEL, P32_XS),
                tile(SSD_BC, P32_BC_OFF // SSD_BC), tile(SMALL, P32_SM_OFF // SMALL)]
    if has_state:
        inputs += [state["csx"], state["csb"], state["h"]]
        in_specs += [_seq_spec(layer, gs, SSD_CONV - 1, D_MODEL), _seq_spec(layer, gs, SSD_CONV - 1, SSD_BC),
                     _seq_spec(layer, gs, SSD_HP, SSD_STATE)]
    inputs += [w["cwx"], w["cbx"], w["cwb"], w["cbb"], w["dtb"], w["alog"], w["dexp"], w["ssd_nw"], w["e"],
               w["bd"]]
    const = lambda *shape: pl.BlockSpec(shape, lambda b, c: (0,) * len(shape))
    in_specs += [_layer_spec(layer, SSD_CONV, D_MODEL), _layer_spec(layer, 1, D_MODEL),
                 _layer_spec(layer, SSD_CONV, SSD_BC), _layer_spec(layer, 1, SSD_BC),
                 _layer_spec(layer, 1, SMALL), _layer_spec(layer, 1, SMALL), _layer_spec(layer, 1, D_MODEL),
                 _layer_spec(layer, 1, D_MODEL), const(LANES, SSD_HP), const(SSD_GROUPS * SSD_STATE, SSD_HP)]
    ys, *new = _stacked_call(
        kern, name="ssd", grid=(b // gs, steps), inputs=inputs, in_specs=in_specs,
        out_shape=(jax.ShapeDtypeStruct((b, grp.length, D_MODEL) if tile3d else (grp.rows, D_MODEL), bf16),
                   jax.ShapeDtypeStruct((DEPTH, b, SSD_CONV - 1, D_MODEL), f32),
                   jax.ShapeDtypeStruct((DEPTH, b, SSD_CONV - 1, SSD_BC), f32),
                   jax.ShapeDtypeStruct((DEPTH, b, SSD_HP, SSD_STATE), f32)),
        out_specs=(tile(D_MODEL, 0), _seq_spec(layer, gs, SSD_CONV - 1, D_MODEL),
                   _seq_spec(layer, gs, SSD_CONV - 1, SSD_BC), _seq_spec(layer, gs, SSD_HP, SSD_STATE)),
        stacked={1: prev and prev[0], 2: prev and prev[1], 3: prev and prev[2]},
        scratch_shapes=[pltpu.VMEM((gs, SUBLANES + q, D_MODEL), f32), pltpu.VMEM((gs, SUBLANES + q, SSD_BC), f32),
                        pltpu.VMEM((gs, SSD_GROUPS * SSD_STATE, SSD_HP), f32), pltpu.VMEM((gs, q, D_MODEL), f32)],
        dimension_semantics=("parallel", "arbitrary"))
    return (ys.reshape(grp.rows, D_MODEL), *new)


def _mlstm(grp, qkv, p32, state, w, layer, prev):
    q, b, gs = grp.q, grp.batch, grp.mlstm_gs
    has_state = state is not None
    cps = grp.cfg("mlstm_cps", layer)
    steps, tile, view, tile3d = grp.tiling(cps, gs)
    hd = MLSTM_HEAD_DIM
    carried = not (has_state and steps == 1 and cps == 1)
    ring = not carried
    kern = functools.partial(_mlstm_kernel, q=q, lr=grp.lr, nc=steps, has_state=has_state, gs=gs, cps=cps,
                             perm=grp.perm, tile3d=tile3d, ring=ring, layer=layer, groups=b // gs)
    qkv, p32 = view(qkv), view(p32)
    inputs = [qkv, qkv, qkv, p32, p32]
    in_specs = [tile(D_MODEL, 0), tile(D_MODEL, 1), tile(D_MODEL, 2), tile(D_MODEL, P32_O),
                tile(SMALL, P32_SM_OFF // SMALL)]
    if has_state:
        inputs += [state["c"], state["n"], state["m"]]
        c_spec = pl.BlockSpec(memory_space=pl.ANY) if ring else _seq_spec(layer, gs, MLSTM_HEADS, hd, hd)
        in_specs += [c_spec, _seq_spec(layer, gs, MLSTM_HEADS, hd),
                     _seq_spec(layer, gs, 1, SMALL)]
    inputs += [w["gate_b"], w["mlstm_nw"]]
    in_specs += [_layer_spec(layer, 1, SMALL), _layer_spec(layer, 1, D_MODEL)]
    hm, *new = _stacked_call(
        kern, name="mlstm", grid=(b // gs, steps), inputs=inputs, in_specs=in_specs,
        out_shape=(jax.ShapeDtypeStruct((b, grp.length, D_MODEL) if tile3d else (grp.rows, D_MODEL), bf16),
                   jax.ShapeDtypeStruct((DEPTH, b, MLSTM_HEADS, hd, hd), f32),
                   jax.ShapeDtypeStruct((DEPTH, b, MLSTM_HEADS, hd), f32),
                   jax.ShapeDtypeStruct((DEPTH, b, 1, SMALL), f32)),
        out_specs=(tile(D_MODEL, 0), _seq_spec(layer, gs, MLSTM_HEADS, hd, hd),
                   _seq_spec(layer, gs, MLSTM_HEADS, hd), _seq_spec(layer, gs, 1, SMALL)),
        stacked={1: prev and prev[0], 2: prev and prev[1], 3: prev and prev[2]},
        scratch_shapes=[pltpu.VMEM((gs, MLSTM_HEADS, hd, hd) if carried else (gs, 1, SUBLANES, LANES), f32),
                        pltpu.VMEM((gs, MLSTM_HEADS, hd), f32), pltpu.VMEM((gs, 1, SMALL), f32)]
        + ([pltpu.VMEM((STATE_BUFFERS, gs, MLSTM_HEADS, hd, hd), f32), pltpu.SemaphoreType.DMA((STATE_BUFFERS,))]
           if ring else []),
        dimension_semantics=("arbitrary" if ring else "parallel", "arbitrary"))
    return (hm.reshape(grp.rows, D_MODEL), *new)


def _trunk(grp, x, state, w):
    ssd_out = mlstm_out = None
    ffn_out = []
    for layer in range(DEPTH):
        p32, qkv = _proj(x, w["w32"], w["wqkv"], layer, grp.cfg("proj_tm", layer))
        if grp.lanes_ssd:
            b, t = grp.batch, grp.length
            x_tm = x.reshape(b, t, D_MODEL).swapaxes(0, 1).reshape(t * b, D_MODEL)
            ys_t, *ssd_out = _ssd_lanes(_proj_t(x_tm, w["w32"], layer), state["cs_t"], state["h_lanes"], w, layer,
                                        ssd_out, steps=t, batch=b)
            ys = ys_t.reshape(D_MODEL, t, b).transpose(2, 1, 0).reshape(b * t, D_MODEL)
        else:
            ys, *ssd_out = _ssd(grp, p32, state, w, layer, ssd_out)
        hm, *mlstm_out = _mlstm(grp, qkv, p32, state, w, layer, mlstm_out)
        x1 = _merge(ys, hm, p32, x, w, layer, grp.cfg("merge_tm", layer), grp.cfg("merge_parts", layer))
        x, s_ffn = _ffn(x1, state["ffn"] if state is not None else None, w, layer, **grp.ffn)
        ffn_out.append(s_ffn)
    return x, ssd_out, mlstm_out, ffn_out


def _unpack_states(batch, ssd_out, mlstm_out):
    csx, csb, h = ssd_out
    c, n, m = mlstm_out
    return (h.reshape(DEPTH, batch, SSD_HEADS, SSD_HEAD_DIM, SSD_STATE),
            jnp.concatenate([csx, csb], axis=-1), c, n, m[:, :, 0, :MLSTM_HEADS])


def kernel(x_prompt, x_sample, state_ssd, state_ssd_conv, state_mlstm_c, state_mlstm_n, state_mlstm_m,
           state_ffn_conv, w_in, ssd_conv_w, ssd_conv_b, ssd_dt_bias, ssd_a_log, ssd_d, ssd_norm_w,
           mlstm_gate_b, mlstm_norm_w, w_branch_a, w_branch_b, w_out, ln1_g, ln1_b, ffn_w_up, ffn_conv_w,
           ffn_conv_b, ffn_w_down, ln2_g, ln2_b):
    w = _prep_weights(w_in, ssd_conv_w, ssd_conv_b, ssd_dt_bias, ssd_a_log, ssd_d, ssd_norm_w, mlstm_gate_b,
                      mlstm_norm_w, w_branch_a, w_branch_b, w_out, ln1_g, ln1_b, ffn_w_up, ffn_conv_w,
                      ffn_conv_b, ffn_w_down, ln2_g, ln2_b)
    keep = FFN_CONV - 1
    keep_ssd = SSD_CONV - 1

    bp, lp, _ = x_prompt.shape
    prompt = _Group(bp, lp, CHUNK, CHUNK, gs=1, ssd_cps=4, mlstm_cps=4, proj_tm=512, merge_tm=1024,
                    ffn=dict(groups=bp, tm=1024, seq_len=0, perm_q=CHUNK), perm=True, mlstm_gs=2, ssd_gs=2)
    per = CHUNK // SUBLANES
    xp_rows = x_prompt.reshape(bp, lp // CHUNK, SUBLANES, per, D_MODEL).swapaxes(2, 3)
    y_p, ssd_p, mlstm_p, ffn_p = _trunk(prompt, xp_rows.reshape(bp * lp, D_MODEL), None, w)
    y_p = y_p.reshape(bp, lp // CHUNK, per, SUBLANES, D_MODEL).swapaxes(2, 3)
    st_p = _unpack_states(bp, ssd_p, mlstm_p)
    ffn_conv_p = jnp.stack(ffn_p)[:, :, SUBLANES - 1::SUBLANES, :]

    bs, ls, _ = x_sample.shape
    s_rows = bs * ls
    sample = _Group(bs, ls, ls, ls, gs=8, ssd_cps=1, mlstm_cps=1, proj_tm=s_rows, merge_tm=s_rows,
                    ffn=dict(groups=1, tm=256, seq_len=ls, bounds=tuple(range(0, D_FF + 1, 256))), lanes_ssd=True)
    lane_b = lambda a: jnp.broadcast_to(a.astype(f32)[..., None], a.shape + (bs,))
    w.update(cw_b=lane_b(ssd_conv_w), cb_b=lane_b(ssd_conv_b), dtb_b=lane_b(ssd_dt_bias), alog_b=lane_b(ssd_a_log),
             dexp_b=lane_b(ssd_d), nw_b=lane_b(ssd_norm_w))
    s_state = dict(
        cs_t=jnp.transpose(state_ssd_conv, (0, 3, 2, 1)).reshape(DEPTH, D_MODEL + SSD_BC, keep_ssd * bs),
        h_lanes=jnp.transpose(state_ssd, (0, 2, 3, 4, 1)),
        c=state_mlstm_c, n=state_mlstm_n,
        m=jnp.pad(state_mlstm_m, ((0, 0), (0, 0), (0, SMALL - MLSTM_HEADS)))[:, :, None, :],
        ffn=state_ffn_conv.reshape(DEPTH, bs * keep, 2 * D_FF),
    )
    y_s, (cs_t, h_lanes), mlstm_s, ffn_s = _trunk(sample, x_sample.reshape(s_rows, D_MODEL), s_state, w)
    c_s, n_s, m_s = mlstm_s
    st_s = (jnp.transpose(h_lanes, (0, 4, 1, 2, 3)),
            jnp.transpose(cs_t.reshape(DEPTH, D_MODEL + SSD_BC, keep_ssd, bs), (0, 3, 2, 1)),
            c_s, n_s, m_s[:, :, 0, :MLSTM_HEADS])
    ffn_conv_s = jnp.stack([u.reshape(bs, ls, 2 * D_FF)[:, ls - keep:, :] for u in ffn_s])
    y_sample = y_s.reshape(bs, ls, D_MODEL)

    return (y_p.reshape(bp, lp, D_MODEL), y_sample, st_p[0], st_s[0], st_p[1], st_s[1], st_p[2], st_s[2],
            st_p[3], st_s[3], st_p[4], st_s[4], ffn_conv_p, ffn_conv_s)
```
